```python
import jax, jax.numpy as jnp
from jax import lax
import numpy as np

D_MODEL = 1024
BATCH = 8
SEQ = 8192
DEPTH = 4

MEM_LEN = 256
HEAD_DIM = 64
N_CONV_GROUPS = 6
CONV_CH = N_CONV_GROUPS * HEAD_DIM
CONV_WIDTH = 31
N_Q_HEADS = 6
N_KV_HEADS = 2
SWA_Q = N_Q_HEADS * HEAD_DIM
SWA_KV = N_KV_HEADS * HEAD_DIM
WINDOW = 128
BLOCK = 128
N_MEM_HEADS = 4
MEM_W = N_MEM_HEADS * HEAD_DIM
D_MIX = CONV_CH + SWA_Q + MEM_W
D_IN = 2 * CONV_CH + SWA_Q + 2 * SWA_KV + MEM_W
D_FF = 2816
ROPE_THETA = 10000.0
EPS = 1e-6

kernel_name = "hymba_style_conformer_swa_memory_macaron"


def rms_norm(x, g):
    xf = x.astype(jnp.float32)
    y = xf * lax.rsqrt(jnp.mean(xf * xf, axis=-1, keepdims=True) + EPS)
    return (y * g.astype(jnp.float32)).astype(x.dtype)


def layer_norm(x, g, b):
    xf = x.astype(jnp.float32)
    mu = jnp.mean(xf, axis=-1, keepdims=True)
    var = jnp.mean(jnp.square(xf - mu), axis=-1, keepdims=True)
    y = (xf - mu) * lax.rsqrt(var + EPS)
    return (y * g.astype(jnp.float32) + b.astype(jnp.float32)).astype(x.dtype)


def swiglu(x, w1, w3, w2):
    return (jax.nn.silu(x @ w1) * (x @ w3)) @ w2


def rope_tables(positions):
    inv_freq = ROPE_THETA ** (-jnp.arange(0, HEAD_DIM, 2, dtype=jnp.float32) / HEAD_DIM)
    ang = positions.astype(jnp.float32)[..., None] * inv_freq
    return jnp.cos(ang)[:, :, None, :], jnp.sin(ang)[:, :, None, :]


def apply_rope(x, cos, sin):
    xf = x.astype(jnp.float32)
    x1, x2 = jnp.split(xf, 2, axis=-1)
    out = jnp.concatenate([x1 * cos - x2 * sin, x2 * cos + x1 * sin], axis=-1)
    return out.astype(x.dtype)


def conv_module(u, w_dw, b_dw, ln_g, ln_b):
    a, gate = jnp.split(u, 2, axis=-1)
    y = a * jax.nn.sigmoid(gate)
    y = lax.conv_general_dilated(
        y, w_dw[:, None, :], window_strides=(1,),
        padding=[(CONV_WIDTH - 1, 0)],
        dimension_numbers=('NWC', 'WIO', 'NWC'),
        feature_group_count=CONV_CH) + b_dw
    y = layer_norm(y, ln_g, ln_b)
    return jax.nn.silu(y)


def sliding_window_attention(q, k, v, sinks):
    B, S, _, Dh = q.shape
    nb = S // BLOCK
    g = N_Q_HEADS // N_KV_HEADS
    qb = q.reshape(B, nb, BLOCK, N_KV_HEADS, g, Dh)
    kb = k.reshape(B, nb, BLOCK, N_KV_HEADS, Dh)
    vb = v.reshape(B, nb, BLOCK, N_KV_HEADS, Dh)

    def with_prev(t):
        prev = jnp.concatenate([jnp.zeros_like(t[:, :1]), t[:, :-1]], axis=1)
        return jnp.concatenate([prev, t], axis=2)

    kw, vw = with_prev(kb), with_prev(vb)
    scores = jnp.einsum('bnqhgd,bnkhd->bnhgqk', qb, kw).astype(jnp.float32) * (Dh ** -0.5)
    qi = jnp.arange(BLOCK)[:, None] + BLOCK
    kj = jnp.arange(2 * BLOCK)[None, :]
    rel = qi - kj
    band = (rel >= 0) & (rel < WINDOW)
    first_ok = (jnp.arange(nb)[:, None, None] > 0) | (kj[None] >= BLOCK)
    mask = band[None] & first_ok
    scores = jnp.where(mask[None, :, None, None], scores, -jnp.inf)
    sink = jnp.broadcast_to(
        sinks.astype(jnp.float32).reshape(N_KV_HEADS, g)[None, None, :, :, None, None],
        scores.shape[:-1] + (1,))
    probs = jax.nn.softmax(jnp.concatenate([scores, sink], axis=-1), axis=-1)[..., :-1]
    out = jnp.einsum('bnhgqk,bnkhd->bnqhgd', probs.astype(v.dtype), vw)
    return out.reshape(B, S, N_Q_HEADS * Dh)


def memory_attention(q, mk, mv):
    B, S, _, Dh = q.shape
    scores = jnp.einsum('bshd,bmhd->bhsm', q, mk).astype(jnp.float32) * (Dh ** -0.5)
    probs = jax.nn.softmax(scores, axis=-1)
    out = jnp.einsum('bhsm,bmhd->bshd', probs.astype(mv.dtype), mv)
    return out.reshape(B, S, N_MEM_HEADS * Dh)


def _fwd_setup_inputs(seed: int = 0) -> dict:
    key = jax.random.key(seed)
    ks = iter(jax.random.split(key, 32))
    f32 = jnp.float32

    def w(shape, fan_in):
        return jax.random.normal(next(ks), shape, f32) * (fan_in ** -0.5)

    def gain(shape):
        return 1.0 + 0.02 * jax.random.normal(next(ks), shape, f32)

    def bias(shape):
        return 0.02 * jax.random.normal(next(ks), shape, f32)

    L = DEPTH
    x = jax.random.normal(next(ks), (BATCH, SEQ, D_MODEL), f32)
    mem = jax.random.normal(next(ks), (BATCH, MEM_LEN, D_MODEL), f32)
    positions = jnp.broadcast_to(jnp.arange(SEQ, dtype=jnp.int32)[None, :], (BATCH, SEQ))
    return {
        "x": x,
        "mem": mem,
        "positions": positions,
        "ffn1_norm": gain((L, D_MODEL)),
        "ffn1_w1": w((L, D_MODEL, D_FF), D_MODEL),
        "ffn1_w3": w((L, D_MODEL, D_FF), D_MODEL),
        "ffn1_w2": w((L, D_FF, D_MODEL), D_FF),
        "mix_norm": gain((L, D_MODEL)),
        "w_in": w((L, D_MODEL, D_IN), D_MODEL),
        "conv_w": w((L, CONV_WIDTH, CONV_CH), CONV_WIDTH),
        "conv_b": bias((L, CONV_CH)),
        "conv_ln_g": gain((L, CONV_CH)),
        "conv_ln_b": bias((L, CONV_CH)),
        "swa_q_norm": gain((L, HEAD_DIM)),
        "swa_k_norm": gain((L, HEAD_DIM)),
        "swa_sinks": jax.random.normal(next(ks), (L, N_Q_HEADS), f32),
        "mem_norm": gain((L, D_MODEL)),
        "w_mem_kv": w((L, D_MODEL, 2 * MEM_W), D_MODEL),
        "mem_q_norm": gain((L, HEAD_DIM)),
        "mem_k_norm": gain((L, HEAD_DIM)),
        "w_out": w((L, D_MIX, D_MODEL), D_MIX),
        "ffn2_norm": gain((L, D_MODEL)),
        "ffn2_w1": w((L, D_MODEL, D_FF), D_MODEL),
        "ffn2_w3": w((L, D_MODEL, D_FF), D_MODEL),
        "ffn2_w2": w((L, D_FF, D_MODEL), D_FF),
        "final_norm": gain((L, D_MODEL)),
    }


def _fwd_reference(x, mem, positions, ffn1_norm, ffn1_w1, ffn1_w3, ffn1_w2, mix_norm, w_in,
              conv_w, conv_b, conv_ln_g, conv_ln_b, swa_q_norm, swa_k_norm, swa_sinks,
              mem_norm, w_mem_kv, mem_q_norm, mem_k_norm, w_out,
              ffn2_norm, ffn2_w1, ffn2_w3, ffn2_w2, final_norm):
    B, S, _ = x.shape
    cos, sin = rope_tables(positions)
    splits = [2 * CONV_CH, 2 * CONV_CH + SWA_Q, 2 * CONV_CH + SWA_Q + SWA_KV,
              2 * CONV_CH + SWA_Q + 2 * SWA_KV]
    for l in range(DEPTH):
        h = x + 0.5 * swiglu(rms_norm(x, ffn1_norm[l]), ffn1_w1[l], ffn1_w3[l], ffn1_w2[l])

        n = rms_norm(h, mix_norm[l])
        p = n @ w_in[l]
        u_conv, q, k, v, q_mem = jnp.split(p, splits, axis=-1)

        y_conv = conv_module(u_conv, conv_w[l], conv_b[l], conv_ln_g[l], conv_ln_b[l])

        q = apply_rope(rms_norm(q.reshape(B, S, N_Q_HEADS, HEAD_DIM), swa_q_norm[l]), cos, sin)
        k = apply_rope(rms_norm(k.reshape(B, S, N_KV_HEADS, HEAD_DIM), swa_k_norm[l]), cos, sin)
        v = v.reshape(B, S, N_KV_HEADS, HEAD_DIM)
        y_swa = sliding_window_attention(q, k, v, swa_sinks[l])

        mkv = rms_norm(mem, mem_norm[l]) @ w_mem_kv[l]
        mk, mv = jnp.split(mkv, 2, axis=-1)
        mk = rms_norm(mk.reshape(B, MEM_LEN, N_MEM_HEADS, HEAD_DIM), mem_k_norm[l])
        mv = mv.reshape(B, MEM_LEN, N_MEM_HEADS, HEAD_DIM)
        qm = rms_norm(q_mem.reshape(B, S, N_MEM_HEADS, HEAD_DIM), mem_q_norm[l])
        y_mem = memory_attention(qm, mk, mv)

        y = jnp.concatenate([y_conv, y_swa, y_mem], axis=-1)
        h = h + y @ w_out[l]

        h = h + 0.5 * swiglu(rms_norm(h, ffn2_norm[l]), ffn2_w1[l], ffn2_w3[l], ffn2_w2[l])

        x = rms_norm(h, final_norm[l])
    return x


import jax as _jax
import jax.numpy as _jnp

TWIN_FORMAT = 'train_step'
FWD_PARAMS = ['x', 'mem', 'positions', 'ffn1_norm', 'ffn1_w1', 'ffn1_w3', 'ffn1_w2', 'mix_norm', 'w_in', 'conv_w', 'conv_b', 'conv_ln_g', 'conv_ln_b', 'swa_q_norm', 'swa_k_norm', 'swa_sinks', 'mem_norm', 'w_mem_kv', 'mem_q_norm', 'mem_k_norm', 'w_out', 'ffn2_norm', 'ffn2_w1', 'ffn2_w3', 'ffn2_w2', 'final_norm']
TWIN_WEIGHTS = ['ffn1_norm', 'ffn1_w1', 'ffn1_w3', 'ffn1_w2', 'mix_norm', 'w_in', 'conv_w', 'conv_b', 'conv_ln_g', 'conv_ln_b', 'swa_q_norm', 'swa_k_norm', 'swa_sinks', 'mem_norm', 'w_mem_kv', 'mem_q_norm', 'mem_k_norm', 'w_out', 'ffn2_norm', 'ffn2_w1', 'ffn2_w3', 'ffn2_w2', 'final_norm']
TWIN_DIFF_INPUT = 'x'
TWIN_INPUTS = ['x', 'mem', 'positions', 'ffn1_norm', 'ffn1_w1', 'ffn1_w3', 'ffn1_w2', 'mix_norm', 'w_in', 'conv_w', 'conv_b', 'conv_ln_g', 'conv_ln_b', 'swa_q_norm', 'swa_k_norm', 'swa_sinks', 'mem_norm', 'w_mem_kv', 'mem_q_norm', 'mem_k_norm', 'w_out', 'ffn2_norm', 'ffn2_w1', 'ffn2_w3', 'ffn2_w2', 'final_norm', 'loss_target', 'm_ffn1_norm', 'm_ffn1_w1', 'm_ffn1_w3', 'm_ffn1_w2', 'm_mix_norm', 'm_w_in', 'm_conv_w', 'm_conv_b', 'm_conv_ln_g', 'm_conv_ln_b', 'm_swa_q_norm', 'm_swa_k_norm', 'm_swa_sinks', 'm_mem_norm', 'm_w_mem_kv', 'm_mem_q_norm', 'm_mem_k_norm', 'm_w_out', 'm_ffn2_norm', 'm_ffn2_w1', 'm_ffn2_w3', 'm_ffn2_w2', 'm_final_norm', 'v_ffn1_norm', 'v_ffn1_w1', 'v_ffn1_w3', 'v_ffn1_w2', 'v_mix_norm', 'v_w_in', 'v_conv_w', 'v_conv_b', 'v_conv_ln_g', 'v_conv_ln_b', 'v_swa_q_norm', 'v_swa_k_norm', 'v_swa_sinks', 'v_mem_norm', 'v_w_mem_kv', 'v_mem_q_norm', 'v_mem_k_norm', 'v_w_out', 'v_ffn2_norm', 'v_ffn2_w1', 'v_ffn2_w3', 'v_ffn2_w2', 'v_final_norm']
TWIN_OUTPUTS = ['loss', 'grad_x', 'grad_ffn1_norm', 'grad_ffn1_w1', 'grad_ffn1_w3', 'grad_ffn1_w2', 'grad_mix_norm', 'grad_w_in', 'grad_conv_w', 'grad_conv_b', 'grad_conv_ln_g', 'grad_conv_ln_b', 'grad_swa_q_norm', 'grad_swa_k_norm', 'grad_swa_sinks', 'grad_mem_norm', 'grad_w_mem_kv', 'grad_mem_q_norm', 'grad_mem_k_norm', 'grad_w_out', 'grad_ffn2_norm', 'grad_ffn2_w1', 'grad_ffn2_w3', 'grad_ffn2_w2', 'grad_final_norm', 'delta_ffn1_norm', 'delta_ffn1_w1', 'delta_ffn1_w3', 'delta_ffn1_w2', 'delta_mix_norm', 'delta_w_in', 'delta_conv_w', 'delta_conv_b', 'delta_conv_ln_g', 'delta_conv_ln_b', 'delta_swa_q_norm', 'delta_swa_k_norm', 'delta_swa_sinks', 'delta_mem_norm', 'delta_w_mem_kv', 'delta_mem_q_norm', 'delta_mem_k_norm', 'delta_w_out', 'delta_ffn2_norm', 'delta_ffn2_w1', 'delta_ffn2_w3', 'delta_ffn2_w2', 'delta_final_norm', 'new_m_ffn1_norm', 'new_m_ffn1_w1', 'new_m_ffn1_w3', 'new_m_ffn1_w2', 'new_m_mix_norm', 'new_m_w_in', 'new_m_conv_w', 'new_m_conv_b', 'new_m_conv_ln_g', 'new_m_conv_ln_b', 'new_m_swa_q_norm', 'new_m_swa_k_norm', 'new_m_swa_sinks', 'new_m_mem_norm', 'new_m_w_mem_kv', 'new_m_mem_q_norm', 'new_m_mem_k_norm', 'new_m_w_out', 'new_m_ffn2_norm', 'new_m_ffn2_w1', 'new_m_ffn2_w3', 'new_m_ffn2_w2', 'new_m_final_norm', 'new_v_ffn1_norm', 'new_v_ffn1_w1', 'new_v_ffn1_w3', 'new_v_ffn1_w2', 'new_v_mix_norm', 'new_v_w_in', 'new_v_conv_w', 'new_v_conv_b', 'new_v_conv_ln_g', 'new_v_conv_ln_b', 'new_v_swa_q_norm', 'new_v_swa_k_norm', 'new_v_swa_sinks', 'new_v_mem_norm', 'new_v_w_mem_kv', 'new_v_mem_q_norm', 'new_v_mem_k_norm', 'new_v_w_out', 'new_v_ffn2_norm', 'new_v_ffn2_w1', 'new_v_ffn2_w3', 'new_v_ffn2_w2', 'new_v_final_norm']
TWIN_LEAF_KINDS = {'loss': 'loss', 'grad_x': 'grad_x', 'grad_ffn1_norm': 'grad_w', 'grad_ffn1_w1': 'grad_w', 'grad_ffn1_w3': 'grad_w', 'grad_ffn1_w2': 'grad_w', 'grad_mix_norm': 'grad_w', 'grad_w_in': 'grad_w', 'grad_conv_w': 'grad_w', 'grad_conv_b': 'grad_w', 'grad_conv_ln_g': 'grad_w', 'grad_conv_ln_b': 'grad_w', 'grad_swa_q_norm': 'grad_w', 'grad_swa_k_norm': 'grad_w', 'grad_swa_sinks': 'grad_w', 'grad_mem_norm': 'grad_w', 'grad_w_mem_kv': 'grad_w', 'grad_mem_q_norm': 'grad_w', 'grad_mem_k_norm': 'grad_w', 'grad_w_out': 'grad_w', 'grad_ffn2_norm': 'grad_w', 'grad_ffn2_w1': 'grad_w', 'grad_ffn2_w3': 'grad_w', 'grad_ffn2_w2': 'grad_w', 'grad_final_norm': 'grad_w', 'delta_ffn1_norm': 'delta_w', 'delta_ffn1_w1': 'delta_w', 'delta_ffn1_w3': 'delta_w', 'delta_ffn1_w2': 'delta_w', 'delta_mix_norm': 'delta_w', 'delta_w_in': 'delta_w', 'delta_conv_w': 'delta_w', 'delta_conv_b': 'delta_w', 'delta_conv_ln_g': 'delta_w', 'delta_conv_ln_b': 'delta_w', 'delta_swa_q_norm': 'delta_w', 'delta_swa_k_norm': 'delta_w', 'delta_swa_sinks': 'delta_w', 'delta_mem_norm': 'delta_w', 'delta_w_mem_kv': 'delta_w', 'delta_mem_q_norm': 'delta_w', 'delta_mem_k_norm': 'delta_w', 'delta_w_out': 'delta_w', 'delta_ffn2_norm': 'delta_w', 'delta_ffn2_w1': 'delta_w', 'delta_ffn2_w3': 'delta_w', 'delta_ffn2_w2': 'delta_w', 'delta_final_norm': 'delta_w', 'new_m_ffn1_norm': 'new_m', 'new_m_ffn1_w1': 'new_m', 'new_m_ffn1_w3': 'new_m', 'new_m_ffn1_w2': 'new_m', 'new_m_mix_norm': 'new_m', 'new_m_w_in': 'new_m', 'new_m_conv_w': 'new_m', 'new_m_conv_b': 'new_m', 'new_m_conv_ln_g': 'new_m', 'new_m_conv_ln_b': 'new_m', 'new_m_swa_q_norm': 'new_m', 'new_m_swa_k_norm': 'new_m', 'new_m_swa_sinks': 'new_m', 'new_m_mem_norm': 'new_m', 'new_m_w_mem_kv': 'new_m', 'new_m_mem_q_norm': 'new_m', 'new_m_mem_k_norm': 'new_m', 'new_m_w_out': 'new_m', 'new_m_ffn2_norm': 'new_m', 'new_m_ffn2_w1': 'new_m', 'new_m_ffn2_w3': 'new_m', 'new_m_ffn2_w2': 'new_m', 'new_m_final_norm': 'new_m', 'new_v_ffn1_norm': 'new_v', 'new_v_ffn1_w1': 'new_v', 'new_v_ffn1_w3': 'new_v', 'new_v_ffn1_w2': 'new_v', 'new_v_mix_norm': 'new_v', 'new_v_w_in': 'new_v', 'new_v_conv_w': 'new_v', 'new_v_conv_b': 'new_v', 'new_v_conv_ln_g': 'new_v', 'new_v_conv_ln_b': 'new_v', 'new_v_swa_q_norm': 'new_v', 'new_v_swa_k_norm': 'new_v', 'new_v_swa_sinks': 'new_v', 'new_v_mem_norm': 'new_v', 'new_v_w_mem_kv': 'new_v', 'new_v_mem_q_norm': 'new_v', 'new_v_mem_k_norm': 'new_v', 'new_v_w_out': 'new_v', 'new_v_ffn2_norm': 'new_v', 'new_v_ffn2_w1': 'new_v', 'new_v_ffn2_w3': 'new_v', 'new_v_ffn2_w2': 'new_v', 'new_v_final_norm': 'new_v'}


def _forward(args):
    return _fwd_reference(*[args[k] for k in FWD_PARAMS])


def _output_shape():
    def fwd():
        inp = _fwd_setup_inputs(0)
        return _fwd_reference(*[inp[k] for k in FWD_PARAMS])
    out = _jax.eval_shape(fwd)
    return out.shape, out.dtype

N_MICROBATCH = 1
ADAM_LR = 0.001
ADAM_B1 = 0.9
ADAM_B2 = 0.999
ADAM_EPS = 1e-08
ADAM_WD = 0.01
ADAM_STEP = 10
PER_EXAMPLE_BATCH_AXIS = {'x': 0, 'mem': 0, 'positions': 0, 'loss_target': 0}
SHARED_INPUTS = []
_WEIGHT_DTYPES = {'ffn1_norm': _jnp.float32, 'ffn1_w1': _jnp.float32, 'ffn1_w3': _jnp.float32, 'ffn1_w2': _jnp.float32, 'mix_norm': _jnp.float32, 'w_in': _jnp.float32, 'conv_w': _jnp.float32, 'conv_b': _jnp.float32, 'conv_ln_g': _jnp.float32, 'conv_ln_b': _jnp.float32, 'swa_q_norm': _jnp.float32, 'swa_k_norm': _jnp.float32, 'swa_sinks': _jnp.float32, 'mem_norm': _jnp.float32, 'w_mem_kv': _jnp.float32, 'mem_q_norm': _jnp.float32, 'mem_k_norm': _jnp.float32, 'w_out': _jnp.float32, 'ffn2_norm': _jnp.float32, 'ffn2_w1': _jnp.float32, 'ffn2_w3': _jnp.float32, 'ffn2_w2': _jnp.float32, 'final_norm': _jnp.float32}
MOMENT_SCALE = {'ffn1_norm': 1.268595e-01, 'ffn1_w1': 5.463281e-02, 'ffn1_w3': 5.301112e-02, 'ffn1_w2': 8.789727e-02, 'mix_norm': 1.273398e-01, 'w_in': 9.946452e-02, 'conv_w': 1.689281e-01, 'conv_b': 4.078521e-01, 'conv_ln_g': 2.299568e-01, 'conv_ln_b': 2.255073e-01, 'swa_q_norm': 1.160071e-01, 'swa_k_norm': 1.166376e-01, 'swa_sinks': 6.535625e-02, 'mem_norm': 2.152687e-02, 'w_mem_kv': 2.887035e-02, 'mem_q_norm': 6.415944e-02, 'mem_k_norm': 6.341935e-02, 'w_out': 1.118667e-01, 'ffn2_norm': 1.083859e-01, 'ffn2_w1': 4.674460e-02, 'ffn2_w3': 4.553962e-02, 'ffn2_w2': 7.541786e-02, 'final_norm': 3.209475e+01}


def _to_microbatches(a, axis):
    t = _jnp.moveaxis(a, axis, 0)
    t = t.reshape((N_MICROBATCH, t.shape[0] // N_MICROBATCH) + t.shape[1:])
    return _jnp.moveaxis(t, 1, axis + 1)


def setup_inputs(seed: int = 0) -> dict:
    inp = _fwd_setup_inputs(seed)
    key = _jax.random.fold_in(_jax.random.key(seed), 7919)
    shape, _ = _output_shape()
    out = dict(inp)
    out["loss_target"] = _jax.random.normal(_jax.random.fold_in(key, 0), shape, _jnp.float32)
    for i, name in enumerate(TWIN_WEIGHTS):
        w = inp[name].astype(_jnp.float32)
        if MOMENT_SCALE is None:
            s = _jnp.sqrt(_jnp.mean(_jnp.square(w)) + 1e-30)
        else:
            s = MOMENT_SCALE[name]
        km, kv = _jax.random.split(_jax.random.fold_in(key, i + 1))
        out[name] = w
        out["m_" + name] = s * _jax.random.normal(km, w.shape, _jnp.float32)
        out["v_" + name] = (s * s) * _jax.random.uniform(kv, w.shape, _jnp.float32, 0.5, 1.5)
    if N_MICROBATCH > 1:
        for name, axis in PER_EXAMPLE_BATCH_AXIS.items():
            out[name] = _to_microbatches(out[name], axis)
    return {'x': out['x'], 'mem': out['mem'], 'positions': out['positions'], 'ffn1_norm': out['ffn1_norm'], 'ffn1_w1': out['ffn1_w1'], 'ffn1_w3': out['ffn1_w3'], 'ffn1_w2': out['ffn1_w2'], 'mix_norm': out['mix_norm'], 'w_in': out['w_in'], 'conv_w': out['conv_w'], 'conv_b': out['conv_b'], 'conv_ln_g': out['conv_ln_g'], 'conv_ln_b': out['conv_ln_b'], 'swa_q_norm': out['swa_q_norm'], 'swa_k_norm': out['swa_k_norm'], 'swa_sinks': out['swa_sinks'], 'mem_norm': out['mem_norm'], 'w_mem_kv': out['w_mem_kv'], 'mem_q_norm': out['mem_q_norm'], 'mem_k_norm': out['mem_k_norm'], 'w_out': out['w_out'], 'ffn2_norm': out['ffn2_norm'], 'ffn2_w1': out['ffn2_w1'], 'ffn2_w3': out['ffn2_w3'], 'ffn2_w2': out['ffn2_w2'], 'final_norm': out['final_norm'], 'loss_target': out['loss_target'], 'm_ffn1_norm': out['m_ffn1_norm'], 'm_ffn1_w1': out['m_ffn1_w1'], 'm_ffn1_w3': out['m_ffn1_w3'], 'm_ffn1_w2': out['m_ffn1_w2'], 'm_mix_norm': out['m_mix_norm'], 'm_w_in': out['m_w_in'], 'm_conv_w': out['m_conv_w'], 'm_conv_b': out['m_conv_b'], 'm_conv_ln_g': out['m_conv_ln_g'], 'm_conv_ln_b': out['m_conv_ln_b'], 'm_swa_q_norm': out['m_swa_q_norm'], 'm_swa_k_norm': out['m_swa_k_norm'], 'm_swa_sinks': out['m_swa_sinks'], 'm_mem_norm': out['m_mem_norm'], 'm_w_mem_kv': out['m_w_mem_kv'], 'm_mem_q_norm': out['m_mem_q_norm'], 'm_mem_k_norm': out['m_mem_k_norm'], 'm_w_out': out['m_w_out'], 'm_ffn2_norm': out['m_ffn2_norm'], 'm_ffn2_w1': out['m_ffn2_w1'], 'm_ffn2_w3': out['m_ffn2_w3'], 'm_ffn2_w2': out['m_ffn2_w2'], 'm_final_norm': out['m_final_norm'], 'v_ffn1_norm': out['v_ffn1_norm'], 'v_ffn1_w1': out['v_ffn1_w1'], 'v_ffn1_w3': out['v_ffn1_w3'], 'v_ffn1_w2': out['v_ffn1_w2'], 'v_mix_norm': out['v_mix_norm'], 'v_w_in': out['v_w_in'], 'v_conv_w': out['v_conv_w'], 'v_conv_b': out['v_conv_b'], 'v_conv_ln_g': out['v_conv_ln_g'], 'v_conv_ln_b': out['v_conv_ln_b'], 'v_swa_q_norm': out['v_swa_q_norm'], 'v_swa_k_norm': out['v_swa_k_norm'], 'v_swa_sinks': out['v_swa_sinks'], 'v_mem_norm': out['v_mem_norm'], 'v_w_mem_kv': out['v_w_mem_kv'], 'v_mem_q_norm': out['v_mem_q_norm'], 'v_mem_k_norm': out['v_mem_k_norm'], 'v_w_out': out['v_w_out'], 'v_ffn2_norm': out['v_ffn2_norm'], 'v_ffn2_w1': out['v_ffn2_w1'], 'v_ffn2_w3': out['v_ffn2_w3'], 'v_ffn2_w2': out['v_ffn2_w2'], 'v_final_norm': out['v_final_norm']}


def _loss(weights, diff, rest, loss_target):
    with _jax.named_scope("forward"):
        args = {**rest, TWIN_DIFF_INPUT: diff, **{k: w.astype(_WEIGHT_DTYPES[k]) for k, w in weights.items()}}
        y = _forward(args)
    with _jax.named_scope("loss_head"):
        err = _jnp.square(y.astype(_jnp.float32) - loss_target)
        return 0.5 * _jnp.sum(_jnp.mean(err, axis=-1)) if err.ndim else 0.5 * err


def _adamw(w, g, m, v):
    m = ADAM_B1 * m + (1.0 - ADAM_B1) * g
    v = ADAM_B2 * v + (1.0 - ADAM_B2) * _jnp.square(g)
    m_hat = m / (1.0 - ADAM_B1 ** ADAM_STEP)
    v_hat = v / (1.0 - ADAM_B2 ** ADAM_STEP)
    delta = -ADAM_LR * (m_hat / (_jnp.sqrt(v_hat) + ADAM_EPS) + ADAM_WD * w)
    return delta, m, v


def reference(x, mem, positions, ffn1_norm, ffn1_w1, ffn1_w3, ffn1_w2, mix_norm, w_in, conv_w, conv_b, conv_ln_g, conv_ln_b, swa_q_norm, swa_k_norm, swa_sinks, mem_norm, w_mem_kv, mem_q_norm, mem_k_norm, w_out, ffn2_norm, ffn2_w1, ffn2_w3, ffn2_w2, final_norm, loss_target, m_ffn1_norm, m_ffn1_w1, m_ffn1_w3, m_ffn1_w2, m_mix_norm, m_w_in, m_conv_w, m_conv_b, m_conv_ln_g, m_conv_ln_b, m_swa_q_norm, m_swa_k_norm, m_swa_sinks, m_mem_norm, m_w_mem_kv, m_mem_q_norm, m_mem_k_norm, m_w_out, m_ffn2_norm, m_ffn2_w1, m_ffn2_w3, m_ffn2_w2, m_final_norm, v_ffn1_norm, v_ffn1_w1, v_ffn1_w3, v_ffn1_w2, v_mix_norm, v_w_in, v_conv_w, v_conv_b, v_conv_ln_g, v_conv_ln_b, v_swa_q_norm, v_swa_k_norm, v_swa_sinks, v_mem_norm, v_w_mem_kv, v_mem_q_norm, v_mem_k_norm, v_w_out, v_ffn2_norm, v_ffn2_w1, v_ffn2_w3, v_ffn2_w2, v_final_norm):
    given = dict(x=x, mem=mem, positions=positions, ffn1_norm=ffn1_norm, ffn1_w1=ffn1_w1, ffn1_w3=ffn1_w3, ffn1_w2=ffn1_w2, mix_norm=mix_norm, w_in=w_in, conv_w=conv_w, conv_b=conv_b, conv_ln_g=conv_ln_g, conv_ln_b=conv_ln_b, swa_q_norm=swa_q_norm, swa_k_norm=swa_k_norm, swa_sinks=swa_sinks, mem_norm=mem_norm, w_mem_kv=w_mem_kv, mem_q_norm=mem_q_norm, mem_k_norm=mem_k_norm, w_out=w_out, ffn2_norm=ffn2_norm, ffn2_w1=ffn2_w1, ffn2_w3=ffn2_w3, ffn2_w2=ffn2_w2, final_norm=final_norm, loss_target=loss_target, m_ffn1_norm=m_ffn1_norm, m_ffn1_w1=m_ffn1_w1, m_ffn1_w3=m_ffn1_w3, m_ffn1_w2=m_ffn1_w2, m_mix_norm=m_mix_norm, m_w_in=m_w_in, m_conv_w=m_conv_w, m_conv_b=m_conv_b, m_conv_ln_g=m_conv_ln_g, m_conv_ln_b=m_conv_ln_b, m_swa_q_norm=m_swa_q_norm, m_swa_k_norm=m_swa_k_norm, m_swa_sinks=m_swa_sinks, m_mem_norm=m_mem_norm, m_w_mem_kv=m_w_mem_kv, m_mem_q_norm=m_mem_q_norm, m_mem_k_norm=m_mem_k_norm, m_w_out=m_w_out, m_ffn2_norm=m_ffn2_norm, m_ffn2_w1=m_ffn2_w1, m_ffn2_w3=m_ffn2_w3, m_ffn2_w2=m_ffn2_w2, m_final_norm=m_final_norm, v_ffn1_norm=v_ffn1_norm, v_ffn1_w1=v_ffn1_w1, v_ffn1_w3=v_ffn1_w3, v_ffn1_w2=v_ffn1_w2, v_mix_norm=v_mix_norm, v_w_in=v_w_in, v_conv_w=v_conv_w, v_conv_b=v_conv_b, v_conv_ln_g=v_conv_ln_g, v_conv_ln_b=v_conv_ln_b, v_swa_q_norm=v_swa_q_norm, v_swa_k_norm=v_swa_k_norm, v_swa_sinks=v_swa_sinks, v_mem_norm=v_mem_norm, v_w_mem_kv=v_w_mem_kv, v_mem_q_norm=v_mem_q_norm, v_mem_k_norm=v_mem_k_norm, v_w_out=v_w_out, v_ffn2_norm=v_ffn2_norm, v_ffn2_w1=v_ffn2_w1, v_ffn2_w3=v_ffn2_w3, v_ffn2_w2=v_ffn2_w2, v_final_norm=v_final_norm)
    weights = {n: given[n] for n in TWIN_WEIGHTS}
    shared = {n: given[n] for n in SHARED_INPUTS}
    per_example = {n: given[n] for n in ['x', 'mem', 'positions']}
    grad_fn = _jax.value_and_grad(_loss, argnums=(0, 1))

    def one_microbatch(ex, loss_target):
        ex = dict(ex)
        diff = ex.pop(TWIN_DIFF_INPUT)
        return grad_fn(weights, diff, {**shared, **ex}, loss_target)

    if N_MICROBATCH == 1:
        loss, (grad_w, grad_x) = one_microbatch(per_example, given["loss_target"])
    else:
        def body(carry, xs):
            loss_sum, grad_sum = carry
            l_k, (gw_k, gx_k) = one_microbatch(xs[0], xs[1])
            with _jax.named_scope("update"):
                return (loss_sum + l_k, _jax.tree.map(_jnp.add, grad_sum, gw_k)), gx_k

        init = (_jnp.zeros((), _jnp.float32), _jax.tree.map(_jnp.zeros_like, weights))
        (loss, grad_w), grad_x = _jax.lax.scan(body, init, (per_example, given["loss_target"]))
    with _jax.named_scope("update"):
        delta_w, new_m, new_v = {}, {}, {}
        for n in TWIN_WEIGHTS:
            delta_w[n], new_m[n], new_v[n] = _adamw(weights[n], grad_w[n], given["m_" + n], given["v_" + n])
    return (loss, grad_x, *[grad_w[n] for n in TWIN_WEIGHTS], *[delta_w[n] for n in TWIN_WEIGHTS],
            *[new_m[n] for n in TWIN_WEIGHTS], *[new_v[n] for n in TWIN_WEIGHTS])
```

```python
import functools

import jax
import jax.numpy as jnp
from jax import lax
from jax.experimental import pallas as pl
from jax.experimental.pallas import tpu as pltpu

F32 = jnp.float32
BF16 = jnp.bfloat16
MESH = pl.DeviceIdType.MESH
AXES = ("x", "y", "c")

EPS = 1e-6
HEAD_DIM = 64
SLOT = 128
CONV_CH = 384
CONV_WIDTH = 31
N_Q, N_KV, N_MEMH = 6, 2, 4
GROUP = N_Q // N_KV
BLK = 128
ROPE_THETA = 10000.0
SCALE = HEAD_DIM ** -0.5
NEG = -1e30

N_SLOTS = N_Q + 2 * N_KV + N_MEMH
PP = 2 * CONV_CH + SLOT * N_SLOTS
QO = 2 * CONV_CH
KO = QO + SLOT * N_Q
VO = KO + SLOT * N_KV
MO = VO + SLOT * N_KV
YP = CONV_CH + SLOT * (N_Q + N_MEMH)
YS = CONV_CH
YM = YS + SLOT * N_Q
D_IN = 2 * CONV_CH + HEAD_DIM * N_SLOTS
D_MIX = CONV_CH + HEAD_DIM * (N_Q + N_MEMH)
MEM_KV = 2 * HEAD_DIM * N_MEMH

ADAM_LR, ADAM_B1, ADAM_B2, ADAM_EPS, ADAM_WD, ADAM_STEP = 0.001, 0.9, 0.999, 1e-08, 0.01, 10

VMEM_LIMIT_MB = 56


def _cp(mb=VMEM_LIMIT_MB):
    return pltpu.CompilerParams(vmem_limit_bytes=mb * 1024 * 1024)


def _dot_nn(a, b):
    return lax.dot_general(a, b, (((1,), (0,)), ((), ())), preferred_element_type=F32)


def _dot_nt(a, b):
    return lax.dot_general(a, b, (((1,), (1,)), ((), ())), preferred_element_type=F32)


def _dot_tn(a, b):
    return lax.dot_general(a, b, (((0,), (0,)), ((), ())), preferred_element_type=F32)


def _sigmoid(x):
    return 1.0 / (1.0 + jnp.exp(-x))


def _rms(x):
    return lax.rsqrt(jnp.mean(x * x, axis=-1, keepdims=True) + EPS)


def _rms_bwd(dn, x, r, g):
    xhat = x * r
    dxhat = dn * g
    dx = r * (dxhat - xhat * jnp.mean(dxhat * xhat, axis=-1, keepdims=True))
    return dx, dn * xhat


def _colsum(v):
    return jnp.sum(v, axis=0, keepdims=True)


def _lane(n):
    return lax.broadcasted_iota(jnp.int32, (n, SLOT), 1)


def _partner(v, lane):
    up = pltpu.roll(v, SLOT - HEAD_DIM // 2, 1)
    dn = pltpu.roll(v, HEAD_DIM // 2, 1)
    return jnp.where(lane < HEAD_DIM // 2, up, jnp.where(lane < HEAD_DIM, dn, 0.0))


def _head_rms(xs):
    return lax.rsqrt(jnp.sum(xs * xs, axis=-1, keepdims=True) * (1.0 / HEAD_DIM) + EPS)


def _head_fwd(xs, g, cosv, sinv, lane):
    xn = xs * _head_rms(xs) * g
    if cosv is None:
        return xn
    return xn * cosv + _partner(xn, lane) * sinv


def _head_bwd(dout, xs, g, cosv, sinv, lane):
    if cosv is not None:
        dout = dout * cosv + _partner(dout * sinv, lane)
    r = _head_rms(xs)
    xhat = xs * r
    dxhat = dout * g
    dx = r * (dxhat - xhat * (jnp.sum(dxhat * xhat, axis=-1, keepdims=True) * (1.0 / HEAD_DIM)))
    return dx, dout * xhat


def _col(v, h, lane):
    return jnp.sum(jnp.where(lane == h, v, 0.0), axis=-1, keepdims=True)


def _rope_tables(pos, invf, tm):
    S = pos.shape[0]

    def body(pos_ref, f_ref, cos_ref, sin_ref):
        ang = pos_ref[...].astype(F32) * f_ref[...]
        lane = _lane(tm)
        c, s = jnp.cos(ang), jnp.sin(ang)
        cos_ref[...] = jnp.where(lane < HEAD_DIM, c, 0.0)
        sin_ref[...] = jnp.where(lane < HEAD_DIM // 2, -s, jnp.where(lane < HEAD_DIM, s, 0.0))

    return pl.pallas_call(
        body, name="rope_tables", grid=(S // tm,),
        in_specs=[pl.BlockSpec((tm, 1), lambda i: (i, 0)), pl.BlockSpec((1, SLOT), lambda i: (0, 0))],
        out_specs=[pl.BlockSpec((tm, SLOT), lambda i: (i, 0))] * 2,
        out_shape=[jax.ShapeDtypeStruct((S, SLOT), F32)] * 2,
    )(pos, invf)


def _ffn_fwd(x, g, wf, l, which, gfin, tm, tf):
    S, D = x.shape
    F = wf.shape[2]
    nf = F // tf
    final = gfin is not None

    def body(*refs):
        if final:
            x_ref, g_ref, w1_ref, w3_ref, w2_ref, gf_ref, h_ref, a_ref, b_ref, xn_ref, n_scr, acc = refs
        else:
            x_ref, g_ref, w1_ref, w3_ref, w2_ref, h_ref, a_ref, b_ref, n_scr, acc = refs
        j = pl.program_id(1)

        @pl.when(j == 0)
        def _():
            xv = x_ref[...]
            n_scr[...] = (xv * _rms(xv) * g_ref[...]).astype(BF16)
            acc[...] = jnp.zeros_like(acc)

        n = n_scr[...]
        a = _dot_nt(n, w1_ref[...])
        b = _dot_nt(n, w3_ref[...])
        a_ref[...] = a.astype(BF16)
        b_ref[...] = b.astype(BF16)
        t = (a * _sigmoid(a) * b).astype(BF16)
        acc[...] += _dot_nn(t, w2_ref[...])

        @pl.when(j == nf - 1)
        def _():
            h = x_ref[...] + 0.5 * acc[...]
            h_ref[...] = h
            if final:
                xn_ref[...] = h * _rms(h) * gf_ref[...]

    def wspec(k):
        return pl.BlockSpec((None, None, tf, D), lambda i, j, k=k: (l, 3 * which + k, j, 0))

    row = pl.BlockSpec((tm, D), lambda i, j: (i, 0))
    vec = pl.BlockSpec((1, D), lambda i, j: (0, 0))
    act = pl.BlockSpec((tm, tf), lambda i, j: (i, j))
    in_specs = [row, vec, wspec(0), wspec(1), wspec(2)] + ([vec] if final else [])
    out_specs = [row, act, act] + ([row] if final else [])
    out_shape = [jax.ShapeDtypeStruct((S, D), F32), jax.ShapeDtypeStruct((S, F), BF16),
                 jax.ShapeDtypeStruct((S, F), BF16)] + ([jax.ShapeDtypeStruct((S, D), F32)] if final else [])
    args = [x, g, wf, wf, wf] + ([gfin] if final else [])
    return pl.pallas_call(
        body, name="ffn_fwd_final" if final else "ffn_fwd", grid=(S // tm, nf),
        in_specs=in_specs, out_specs=out_specs, out_shape=out_shape,
        scratch_shapes=[pltpu.VMEM((tm, D), BF16), pltpu.VMEM((tm, D), F32)],
        compiler_params=_cp(),
    )(*args)


def _ffn_bwd_act(dh, x, g, a, b, wf, l, which, tm, tf):
    S, D = x.shape
    F = wf.shape[2]
    nf = F // tf

    def body(dh_ref, x_ref, g_ref, a_ref, b_ref, w1_ref, w3_ref, w2_ref,
             dx_ref, dg_ref, da_ref, db_ref, t_ref, n_ref, dy_ref, acc):
        i, j = pl.program_id(0), pl.program_id(1)

        @pl.when(j == 0)
        def _():
            xv = x_ref[...]
            n_ref[...] = (xv * _rms(xv) * g_ref[...]).astype(BF16)
            dy_ref[...] = (0.5 * dh_ref[...]).astype(BF16)
            acc[...] = jnp.zeros_like(acc)

            @pl.when(i == 0)
            def _():
                dg_ref[...] = jnp.zeros_like(dg_ref)

        av = a_ref[...].astype(F32)
        bv = b_ref[...].astype(F32)
        sg = _sigmoid(av)
        s = av * sg
        dt = _dot_nt(dy_ref[...], w2_ref[...])
        t_ref[...] = (s * bv).astype(BF16)
        db = (dt * s).astype(BF16)
        da = (dt * bv * (sg * (1.0 + av * (1.0 - sg)))).astype(BF16)
        da_ref[...] = da
        db_ref[...] = db
        acc[...] += _dot_nn(da, w1_ref[...]) + _dot_nn(db, w3_ref[...])

        @pl.when(j == nf - 1)
        def _():
            xv = x_ref[...]
            dx, dgrow = _rms_bwd(acc[...], xv, _rms(xv), g_ref[...])
            dx_ref[...] = dh_ref[...] + dx
            dg_ref[...] += _colsum(dgrow)

    def wspec(k):
        return pl.BlockSpec((None, None, tf, D), lambda i, j, k=k: (l, 3 * which + k, j, 0))

    row = pl.BlockSpec((tm, D), lambda i, j: (i, 0))
    vec = pl.BlockSpec((1, D), lambda i, j: (0, 0))
    act = pl.BlockSpec((tm, tf), lambda i, j: (i, j))
    sd = lambda shp, dt: jax.ShapeDtypeStruct(shp, dt)
    return pl.pallas_call(
        body, name="ffn_bwd_act", grid=(S // tm, nf),
        in_specs=[row, row, vec, act, act, wspec(0), wspec(1), wspec(2)],
        out_specs=[row, vec, act, act, act, row, row],
        out_shape=[sd((S, D), F32), sd((1, D), F32), sd((S, F), BF16), sd((S, F), BF16), sd((S, F), BF16),
                   sd((S, D), BF16), sd((S, D), BF16)],
        scratch_shapes=[pltpu.VMEM((tm, D), F32)],
        compiler_params=_cp(),
    )(dh, x, g, a, b, wf, wf, wf)


def _ffn_bwd_w(da, db, t, n, dy, tm, tf):
    S, F = da.shape
    D = n.shape[1]
    nt = S // tm

    def body(da_ref, db_ref, t_ref, n_ref, dy_ref, out_ref, acc):
        i = pl.program_id(1)

        @pl.when(i == 0)
        def _():
            acc[...] = jnp.zeros_like(acc)

        nv = n_ref[...]
        acc[0] += _dot_tn(da_ref[...], nv)
        acc[1] += _dot_tn(db_ref[...], nv)
        acc[2] += _dot_tn(t_ref[...], dy_ref[...])

        @pl.when(i == nt - 1)
        def _():
            out_ref[...] = acc[...].astype(BF16)

    act = pl.BlockSpec((tm, tf), lambda j, i: (i, j))
    row = pl.BlockSpec((tm, D), lambda j, i: (i, 0))
    return pl.pallas_call(
        body, name="ffn_bwd_w", grid=(F // tf, nt),
        in_specs=[act, act, act, row, row],
        out_specs=pl.BlockSpec((3, tf, D), lambda j, i: (0, j, 0)),
        out_shape=jax.ShapeDtypeStruct((3, F, D), BF16),
        scratch_shapes=[pltpu.VMEM((3, tf, D), F32)],
        compiler_params=_cp(),
    )(da, db, t, n, dy)


def _proj_fwd(h, g, w_inp, tm):
    S, D = h.shape

    def body(h_ref, g_ref, w_ref, p_ref, n_ref):
        hv = h_ref[...]
        n = (hv * _rms(hv) * g_ref[...]).astype(BF16)
        n_ref[...] = n
        p_ref[...] = _dot_nt(n, w_ref[...])

    return pl.pallas_call(
        body, name="proj_fwd", grid=(S // tm,),
        in_specs=[pl.BlockSpec((tm, D), lambda i: (i, 0)), pl.BlockSpec((1, D), lambda i: (0, 0)),
                  pl.BlockSpec((PP, D), lambda i: (0, 0))],
        out_specs=[pl.BlockSpec((tm, PP), lambda i: (i, 0)), pl.BlockSpec((tm, D), lambda i: (i, 0))],
        out_shape=[jax.ShapeDtypeStruct((S, PP), F32), jax.ShapeDtypeStruct((S, D), BF16)],
        compiler_params=_cp(),
    )(h, g, w_inp)


def _glu(u):
    return u[:, :CONV_CH] * _sigmoid(u[:, CONV_CH:2 * CONV_CH])


def _layer_norm_stats(yc):
    mu = jnp.mean(yc, axis=-1, keepdims=True)
    d = yc - mu
    rstd = lax.rsqrt(jnp.mean(d * d, axis=-1, keepdims=True) + EPS)
    return d * rstd, rstd


def _mem_kv_fwd(mem, g, w_mkvp, gk):
    M, D = mem.shape
    W = SLOT * N_MEMH

    def body(mem_ref, g_ref, w_ref, gk_ref, nm_ref, raw_ref, mk_ref, mv_ref):
        mv_ = mem_ref[...]
        nm = (mv_ * _rms(mv_) * g_ref[...]).astype(BF16)
        nm_ref[...] = nm
        raw = _dot_nn(nm, w_ref[...])
        raw_ref[...] = raw
        for hh in range(N_MEMH):
            sl = slice(SLOT * hh, SLOT * (hh + 1))
            mk_ref[:, sl] = _head_fwd(raw[:, sl], gk_ref[...], None, None, None).astype(BF16)
        mv_ref[...] = raw[:, W:].astype(BF16)

    sd = jax.ShapeDtypeStruct
    return pl.pallas_call(
        body, name="mem_kv_fwd",
        out_shape=[sd((M, D), BF16), sd((M, 2 * W), F32), sd((M, W), BF16), sd((M, W), BF16)],
        compiler_params=_cp(),
    )(mem, g, w_mkvp, gk)


def _mem_kv_bwd(dmk, dmv, raw, nm, mem, g, w_mkvp, gk):
    M, D = mem.shape
    W = SLOT * N_MEMH

    def body(dmk_ref, dmv_ref, raw_ref, nm_ref, mem_ref, g_ref, w_ref, gk_ref, dw_ref, dg_ref, dgk_ref, draw):
        dgk = jnp.zeros((1, SLOT), F32)
        for hh in range(N_MEMH):
            sl = slice(SLOT * hh, SLOT * (hh + 1))
            dx, prod = _head_bwd(dmk_ref[:, sl], raw_ref[:, sl], gk_ref[...], None, None, None)
            draw[:, sl] = dx.astype(BF16)
            dgk = dgk + _colsum(prod)
        dgk_ref[...] = dgk
        draw[:, W:] = dmv_ref[...].astype(BF16)
        dr = draw[...]
        dw_ref[...] = _dot_tn(nm_ref[...], dr)
        dnm = _dot_nt(dr, w_ref[...])
        mv_ = mem_ref[...]
        dg_ref[...] = _colsum(dnm * (mv_ * _rms(mv_)))

    sd = jax.ShapeDtypeStruct
    return pl.pallas_call(
        body, name="mem_kv_bwd",
        out_shape=[sd((D, 2 * W), F32), sd((1, D), F32), sd((1, SLOT), F32)],
        scratch_shapes=[pltpu.VMEM((M, 2 * W), BF16)],
        compiler_params=_cp(),
    )(dmk, dmv, raw, nm, mem, g, w_mkvp, gk)


def _mixer_fwd(p, h, cosT, sinT, conv_w, conv_b, ln_g, ln_b, gq, gk, sinks, mk, mv, gqm, w_outp, tm):
    S, D = h.shape
    M = mk.shape[0]
    nb = tm // BLK
    nblocks = S // BLK

    def body(p_ref, ph_ref, h_ref, cos_ref, cosh_ref, sin_ref, sinh_ref, cw_ref, cb_ref, lg_ref, lb_ref,
             gq_ref, gk_ref, sink_ref, mk_ref, mv_ref, gqm_ref, wo_ref,
             h2_ref, y_ref, yc_ref, lse_ref, ext, y_scr):
        i = pl.program_id(0)
        not_first = (i > 0).astype(F32)
        lane = _lane(tm)
        lane_e = _lane(tm + BLK)

        ext[0:BLK, :] = _glu(ph_ref[:, 0:2 * CONV_CH]) * not_first
        ext[BLK:BLK + tm, :] = _glu(p_ref[:, 0:2 * CONV_CH])
        yc = jnp.zeros((tm, CONV_CH), F32) + cb_ref[...]
        for k in range(CONV_WIDTH):
            off = BLK - (CONV_WIDTH - 1) + k
            yc = yc + cw_ref[k:k + 1, :] * ext[off:off + tm, :]
        yc_ref[...] = yc
        z, _ = _layer_norm_stats(yc)
        ln = z * lg_ref[...] + lb_ref[...]
        y_scr[:, 0:CONV_CH] = (ln * _sigmoid(ln)).astype(BF16)

        cos_e = jnp.concatenate([cosh_ref[...], cos_ref[...]], axis=0)
        sin_e = jnp.concatenate([sinh_ref[...], sin_ref[...]], axis=0)
        qi = lax.broadcasted_iota(jnp.int32, (BLK, 2 * BLK), 0)
        kj = lax.broadcasted_iota(jnp.int32, (BLK, 2 * BLK), 1)
        band = (kj > qi) & (kj <= qi + BLK)
        band0 = band & ((kj >= BLK) | (i > 0))
        lse = jnp.zeros((tm, SLOT), F32)
        for kvh in range(N_KV):
            ks = slice(KO + SLOT * kvh, KO + SLOT * (kvh + 1))
            vs = slice(VO + SLOT * kvh, VO + SLOT * (kvh + 1))
            k_raw = jnp.concatenate([ph_ref[:, ks], p_ref[:, ks]], axis=0)
            k_e = _head_fwd(k_raw, gk_ref[...], cos_e, sin_e, lane_e).astype(BF16)
            v_e = jnp.concatenate([ph_ref[:, vs], p_ref[:, vs]], axis=0).astype(BF16)
            for gi in range(GROUP):
                hq = GROUP * kvh + gi
                qs = slice(QO + SLOT * hq, QO + SLOT * (hq + 1))
                q = _head_fwd(p_ref[:, qs], gq_ref[...], cos_ref[...], sin_ref[...], lane).astype(BF16)
                sink = sink_ref[hq]
                outs, lses = [], []
                for m in range(nb):
                    rows = slice(BLK * m, BLK * (m + 1))
                    win = slice(BLK * m, BLK * (m + 2))
                    s = _dot_nt(q[rows], k_e[win]) * SCALE
                    s = jnp.where(band0 if m == 0 else band, s, NEG)
                    mx = jnp.maximum(jnp.max(s, axis=-1, keepdims=True), sink)
                    e = jnp.exp(s - mx)
                    den = jnp.sum(e, axis=-1, keepdims=True) + jnp.exp(sink - mx)
                    prob = e / den
                    outs.append(_dot_nn(prob.astype(BF16), v_e[win]))
                    lses.append(mx + jnp.log(den))
                y_scr[:, YS + SLOT * hq:YS + SLOT * (hq + 1)] = jnp.concatenate(outs, axis=0).astype(BF16)
                lse = jnp.where(lane == hq, jnp.concatenate(lses, axis=0), lse)

        for hm in range(N_MEMH):
            ms = slice(SLOT * hm, SLOT * (hm + 1))
            qm = _head_fwd(p_ref[:, MO + SLOT * hm:MO + SLOT * (hm + 1)], gqm_ref[...], None, None, None)
            s = _dot_nt(qm.astype(BF16), mk_ref[:, ms]) * SCALE
            mx = jnp.max(s, axis=-1, keepdims=True)
            e = jnp.exp(s - mx)
            den = jnp.sum(e, axis=-1, keepdims=True)
            o = _dot_nn((e / den).astype(BF16), mv_ref[:, ms])
            y_scr[:, YM + SLOT * hm:YM + SLOT * (hm + 1)] = o.astype(BF16)
            lse = jnp.where(lane == N_Q + hm, mx + jnp.log(den), lse)
        lse_ref[...] = lse

        yv = y_scr[...]
        y_ref[...] = yv
        h2_ref[...] = h_ref[...] + _dot_nn(yv, wo_ref[...])

    cur = lambda w: pl.BlockSpec((tm, w), lambda i: (i, 0))
    prev = lambda w: pl.BlockSpec((BLK, w), lambda i: (jnp.maximum(i * nb - 1, 0), 0))
    full = lambda a: pl.BlockSpec(a.shape, lambda i: (0,) * a.ndim)
    sd = jax.ShapeDtypeStruct
    return pl.pallas_call(
        body, name="mixer_fwd", grid=(S // tm,),
        in_specs=[cur(PP), prev(PP), cur(D), cur(SLOT), prev(SLOT), cur(SLOT), prev(SLOT),
                  full(conv_w), full(conv_b), full(ln_g), full(ln_b), full(gq), full(gk),
                  pl.BlockSpec(memory_space=pltpu.SMEM), full(mk), full(mv), full(gqm), full(w_outp)],
        out_specs=[cur(D), cur(YP), cur(CONV_CH), cur(SLOT)],
        out_shape=[sd((S, D), F32), sd((S, YP), BF16), sd((S, CONV_CH), F32), sd((S, SLOT), F32)],
        scratch_shapes=[pltpu.VMEM((tm + BLK, CONV_CH), F32), pltpu.VMEM((tm, YP), BF16)],
        compiler_params=_cp(),
    )(p, p, h, cosT, cosT, sinT, sinT, conv_w, conv_b, ln_g, ln_b, gq, gk, sinks, mk, mv, gqm, w_outp)


def _outproj_bwd(dh2, y, yc, ln_g, ln_b, w_outp, tm):
    S, D = dh2.shape
    NH = N_Q + N_MEMH

    def body(dh_ref, y_ref, yc_ref, lg_ref, lb_ref, wo_ref, dyv_ref, del_ref, dwo_ref, dlg_ref, dlb_ref):
        i = pl.program_id(0)

        @pl.when(i == 0)
        def _():
            dwo_ref[...] = jnp.zeros_like(dwo_ref)
            dlg_ref[...] = jnp.zeros_like(dlg_ref)
            dlb_ref[...] = jnp.zeros_like(dlb_ref)

        dhb = dh_ref[...].astype(BF16)
        yv = y_ref[...]
        dy = _dot_nt(dhb, wo_ref[...])
        dwo_ref[...] += _dot_tn(yv, dhb)

        z, rstd = _layer_norm_stats(yc_ref[...])
        ln = z * lg_ref[...] + lb_ref[...]
        sg = _sigmoid(ln)
        dln = dy[:, 0:CONV_CH] * (sg * (1.0 + ln * (1.0 - sg)))
        dlg_ref[...] += _colsum(dln * z)
        dlb_ref[...] += _colsum(dln)
        dz = dln * lg_ref[...]
        dyv_ref[:, 0:CONV_CH] = rstd * (dz - jnp.mean(dz, axis=-1, keepdims=True)
                                        - z * jnp.mean(dz * z, axis=-1, keepdims=True))
        dyv_ref[:, CONV_CH:] = dy[:, CONV_CH:]

        lane = _lane(tm)
        delta = jnp.zeros((tm, SLOT), F32)
        for hh in range(NH):
            sl = slice(YS + SLOT * hh, YS + SLOT * (hh + 1))
            d = jnp.sum(dy[:, sl] * yv[:, sl].astype(F32), axis=-1, keepdims=True)
            delta = jnp.where(lane == hh, d, delta)
        del_ref[...] = delta

    cur = lambda w: pl.BlockSpec((tm, w), lambda i: (i, 0))
    full = lambda a: pl.BlockSpec(a.shape, lambda i: (0,) * a.ndim)
    sd = jax.ShapeDtypeStruct
    return pl.pallas_call(
        body, name="outproj_bwd", grid=(S // tm,),
        in_specs=[cur(D), cur(YP), cur(CONV_CH), full(ln_g), full(ln_b), full(w_outp)],
        out_specs=[cur(YP), cur(SLOT), pl.BlockSpec((YP, D), lambda i: (0, 0)),
                   pl.BlockSpec((1, CONV_CH), lambda i: (0, 0)), pl.BlockSpec((1, CONV_CH), lambda i: (0, 0))],
        out_shape=[sd((S, YP), F32), sd((S, SLOT), F32), sd((YP, D), F32), sd((1, CONV_CH), F32),
                   sd((1, CONV_CH), F32)],
        compiler_params=_cp(),
    )(dh2, y, yc, ln_g, ln_b, w_outp)


def _mixer_bwd(p, dyv, lse, delta, cosT, sinT, conv_w, gq, gk, sinks, mk, mv, gqm, tm):
    S = p.shape[0]
    M = mk.shape[0]
    nb = tm // BLK
    nt = S // tm
    nblocks = S // BLK
    W = SLOT * N_MEMH

    def body(p_ref, pp_ref, pn_ref, dy_ref, dyn_ref, lse_ref, lsen_ref, del_ref, deln_ref,
             cos_ref, cosp_ref, cosn_ref, sin_ref, sinp_ref, sinn_ref,
             cw_ref, gq_ref, gk_ref, sink_ref, mk_ref, mv_ref, gqm_ref,
             dp_ref, dcw_ref, dcb_ref, dgq_ref, dgk_ref, dgqm_ref, dsink_ref, dmk_ref, dmv_ref, ext, ext2):
        i = pl.program_id(0)

        @pl.when(i == 0)
        def _():
            for r in (dcw_ref, dcb_ref, dgq_ref, dgk_ref, dgqm_ref, dsink_ref, dmk_ref, dmv_ref):
                r[...] = jnp.zeros_like(r)

        not_first = (i > 0).astype(F32)
        not_last = (i < nt - 1).astype(F32)
        lane = _lane(tm)
        lane_e = _lane(tm + BLK)

        u = p_ref[:, 0:2 * CONV_CH]
        a_, sg = u[:, :CONV_CH], _sigmoid(u[:, CONV_CH:])
        ext[0:BLK, :] = _glu(pp_ref[:, 0:2 * CONV_CH]) * not_first
        ext[BLK:BLK + tm, :] = a_ * sg
        dyc = dy_ref[:, 0:CONV_CH]
        ext2[0:tm, :] = dyc
        ext2[tm:tm + BLK, :] = dyn_ref[:, 0:CONV_CH] * not_last
        dcb_ref[...] += _colsum(dyc)
        dyg = jnp.zeros((tm, CONV_CH), F32)
        for k in range(CONV_WIDTH):
            off = BLK - (CONV_WIDTH - 1) + k
            dcw_ref[k:k + 1, :] += _colsum(dyc * ext[off:off + tm, :])
            fo = CONV_WIDTH - 1 - k
            dyg = dyg + cw_ref[k:k + 1, :] * ext2[fo:fo + tm, :]
        dp_ref[:, 0:CONV_CH] = dyg * sg
        dp_ref[:, CONV_CH:2 * CONV_CH] = dyg * a_ * sg * (1.0 - sg)

        cos_k = jnp.concatenate([cosp_ref[...], cos_ref[...]], axis=0)
        sin_k = jnp.concatenate([sinp_ref[...], sin_ref[...]], axis=0)
        cos_q = jnp.concatenate([cos_ref[...], cosn_ref[...]], axis=0)
        sin_q = jnp.concatenate([sin_ref[...], sinn_ref[...]], axis=0)
        lse_e = jnp.concatenate([lse_ref[...], lsen_ref[...]], axis=0)
        del_e = jnp.concatenate([del_ref[...], deln_ref[...]], axis=0)
        qi = lax.broadcasted_iota(jnp.int32, (BLK, BLK), 0)
        kj = lax.broadcasted_iota(jnp.int32, (BLK, BLK), 1)
        diag = kj <= qi
        offd = kj > qi
        dgq = jnp.zeros((1, SLOT), F32)
        dgk = jnp.zeros((1, SLOT), F32)
        dsink = jnp.zeros((1, SLOT), F32)
        lane1 = lax.broadcasted_iota(jnp.int32, (1, SLOT), 1)
        for kvh in range(N_KV):
            ks = slice(KO + SLOT * kvh, KO + SLOT * (kvh + 1))
            vs = slice(VO + SLOT * kvh, VO + SLOT * (kvh + 1))
            k_raw = jnp.concatenate([pp_ref[:, ks], p_ref[:, ks]], axis=0)
            k_e = _head_fwd(k_raw, gk_ref[...], cos_k, sin_k, lane_e).astype(BF16)
            v_e = jnp.concatenate([pp_ref[:, vs], p_ref[:, vs]], axis=0).astype(BF16)
            dk = [jnp.zeros((BLK, SLOT), F32) for _ in range(nb)]
            dv = [jnp.zeros((BLK, SLOT), F32) for _ in range(nb)]
            for gi in range(GROUP):
                hq = GROUP * kvh + gi
                qs = slice(QO + SLOT * hq, QO + SLOT * (hq + 1))
                os_ = slice(YS + SLOT * hq, YS + SLOT * (hq + 1))
                q_raw = jnp.concatenate([p_ref[:, qs], pn_ref[:, qs]], axis=0)
                q_e = _head_fwd(q_raw, gq_ref[...], cos_q, sin_q, lane_e).astype(BF16)
                do_e = jnp.concatenate([dy_ref[:, os_], dyn_ref[:, os_]], axis=0).astype(BF16)
                lse_h = _col(lse_e, hq, lane_e)
                del_h = _col(del_e, hq, lane_e)
                sink = sink_ref[hq]
                dq = [jnp.zeros((BLK, SLOT), F32) for _ in range(nb)]
                for m in range(nb + 1):
                    rows = slice(BLK * m, BLK * (m + 1))
                    qb, dob, lb, db_ = q_e[rows], do_e[rows], lse_h[rows], del_h[rows]
                    for n in (m - 1, m):
                        if n == nb:
                            continue
                        krows = slice(BLK * (n + 1), BLK * (n + 2))
                        kb, vb = k_e[krows], v_e[krows]
                        s = _dot_nt(qb, kb) * SCALE
                        mask = diag if n == m else offd
                        if n == -1:
                            mask = mask & (i > 0)
                        if m == nb:
                            mask = mask & (i < nt - 1)
                        prob = jnp.where(mask, jnp.exp(jnp.where(mask, s - lb, NEG)), 0.0)
                        dpb = _dot_nt(dob, vb)
                        ds = (prob * (dpb - db_) * SCALE).astype(BF16)
                        if m < nb:
                            dq[m] = dq[m] + _dot_nn(ds, kb)
                        if n >= 0:
                            dk[n] = dk[n] + _dot_tn(ds, qb)
                            dv[n] = dv[n] + _dot_tn(prob.astype(BF16), dob)
                dqr, prod = _head_bwd(jnp.concatenate(dq, axis=0), p_ref[:, qs], gq_ref[...],
                                      cos_ref[...], sin_ref[...], lane)
                dp_ref[:, qs] = dqr
                dgq = dgq + _colsum(prod)
                psink = jnp.exp(sink - lse_h[0:tm])
                dsink = dsink + jnp.where(lane1 == hq, -_colsum(psink * del_h[0:tm]), 0.0)
            dkr, prod = _head_bwd(jnp.concatenate(dk, axis=0), p_ref[:, ks], gk_ref[...],
                                  cos_ref[...], sin_ref[...], lane)
            dp_ref[:, ks] = dkr
            dgk = dgk + _colsum(prod)
            dp_ref[:, vs] = jnp.concatenate(dv, axis=0)
        dgq_ref[...] += dgq
        dgk_ref[...] += dgk
        dsink_ref[...] += dsink

        dgqm = jnp.zeros((1, SLOT), F32)
        for hm in range(N_MEMH):
            ms = slice(SLOT * hm, SLOT * (hm + 1))
            qs = slice(MO + SLOT * hm, MO + SLOT * (hm + 1))
            os_ = slice(YM + SLOT * hm, YM + SLOT * (hm + 1))
            qraw = p_ref[:, qs]
            qm = _head_fwd(qraw, gqm_ref[...], None, None, None).astype(BF16)
            kb, vb = mk_ref[:, ms], mv_ref[:, ms]
            dob = dy_ref[:, os_].astype(BF16)
            s = _dot_nt(qm, kb) * SCALE
            prob = jnp.exp(s - _col(lse_ref[...], N_Q + hm, lane))
            dpb = _dot_nt(dob, vb)
            ds = (prob * (dpb - _col(del_ref[...], N_Q + hm, lane)) * SCALE).astype(BF16)
            dqr, prod = _head_bwd(_dot_nn(ds, kb), qraw, gqm_ref[...], None, None, None)
            dp_ref[:, qs] = dqr
            dgqm = dgqm + _colsum(prod)
            dmk_ref[:, ms] += _dot_tn(ds, qm)
            dmv_ref[:, ms] += _dot_tn(prob.astype(BF16), dob)
        dgqm_ref[...] += dgqm

    cur = lambda w: pl.BlockSpec((tm, w), lambda i: (i, 0))
    prev = lambda w: pl.BlockSpec((BLK, w), lambda i: (jnp.maximum(i * nb - 1, 0), 0))
    nxt = lambda w: pl.BlockSpec((BLK, w), lambda i: (jnp.minimum((i + 1) * nb, nblocks - 1), 0))
    full = lambda a: pl.BlockSpec(a.shape, lambda i: (0,) * a.ndim)
    acc = lambda r, w: pl.BlockSpec((r, w), lambda i: (0, 0))
    sd = jax.ShapeDtypeStruct
    return pl.pallas_call(
        body, name="mixer_bwd", grid=(nt,),
        in_specs=[cur(PP), prev(PP), nxt(PP), cur(YP), nxt(YP), cur(SLOT), nxt(SLOT), cur(SLOT), nxt(SLOT),
                  cur(SLOT), prev(SLOT), nxt(SLOT), cur(SLOT), prev(SLOT), nxt(SLOT),
                  full(conv_w), full(gq), full(gk), pl.BlockSpec(memory_space=pltpu.SMEM),
                  full(mk), full(mv), full(gqm)],
        out_specs=[cur(PP), acc(32, CONV_CH), acc(1, CONV_CH), acc(1, SLOT), acc(1, SLOT), acc(1, SLOT),
                   acc(1, SLOT), acc(M, W), acc(M, W)],
        out_shape=[sd((S, PP), F32), sd((32, CONV_CH), F32), sd((1, CONV_CH), F32), sd((1, SLOT), F32),
                   sd((1, SLOT), F32), sd((1, SLOT), F32), sd((1, SLOT), F32), sd((M, W), F32), sd((M, W), F32)],
        scratch_shapes=[pltpu.VMEM((tm + BLK, CONV_CH), F32), pltpu.VMEM((tm + BLK, CONV_CH), F32)],
        compiler_params=_cp(),
    )(p, p, p, dyv, dyv, lse, lse, delta, delta, cosT, cosT, cosT, sinT, sinT, sinT,
      conv_w, gq, gk, sinks, mk, mv, gqm)


def _proj_bwd(dp, h, dh2, g, n, w_inp, tm):
    S, D = h.shape

    def body(dp_ref, h_ref, dh2_ref, g_ref, n_ref, w_ref, dh_ref, dg_ref, dw_ref):
        i = pl.program_id(0)

        @pl.when(i == 0)
        def _():
            dg_ref[...] = jnp.zeros_like(dg_ref)
            dw_ref[...] = jnp.zeros_like(dw_ref)

        dpb = dp_ref[...].astype(BF16)
        dn = _dot_nn(dpb, w_ref[...])
        dw_ref[...] += _dot_tn(dpb, n_ref[...])
        hv = h_ref[...]
        dx, dgrow = _rms_bwd(dn, hv, _rms(hv), g_ref[...])
        dh_ref[...] = dh2_ref[...] + dx
        dg_ref[...] += _colsum(dgrow)

    cur = lambda w: pl.BlockSpec((tm, w), lambda i: (i, 0))
    sd = jax.ShapeDtypeStruct
    return pl.pallas_call(
        body, name="proj_bwd", grid=(S // tm,),
        in_specs=[cur(PP), cur(D), cur(D), pl.BlockSpec((1, D), lambda i: (0, 0)), cur(D),
                  pl.BlockSpec((PP, D), lambda i: (0, 0))],
        out_specs=[cur(D), pl.BlockSpec((1, D), lambda i: (0, 0)), pl.BlockSpec((PP, D), lambda i: (0, 0))],
        out_shape=[sd((S, D), F32), sd((1, D), F32), sd((PP, D), F32)],
        compiler_params=_cp(),
    )(dp, h, dh2, g, n, w_inp)


def _norm_bwd(dxn, h, g, tm):
    S, D = h.shape

    def body(d_ref, h_ref, g_ref, dh_ref, dg_ref):
        @pl.when(pl.program_id(0) == 0)
        def _():
            dg_ref[...] = jnp.zeros_like(dg_ref)

        hv = h_ref[...]
        dx, dgrow = _rms_bwd(d_ref[...], hv, _rms(hv), g_ref[...])
        dh_ref[...] = dx
        dg_ref[...] += _colsum(dgrow)

    cur = pl.BlockSpec((tm, D), lambda i: (i, 0))
    vec = pl.BlockSpec((1, D), lambda i: (0, 0))
    return pl.pallas_call(
        body, name="norm_bwd", grid=(S // tm,), in_specs=[cur, cur, vec], out_specs=[cur, vec],
        out_shape=[jax.ShapeDtypeStruct((S, D), F32), jax.ShapeDtypeStruct((1, D), F32)],
        compiler_params=_cp(),
    )(dxn, h, g)


def _loss_bwd(xn, h, g, target, tm):
    S, D = h.shape

    def body(y_ref, h_ref, g_ref, t_ref, loss_ref, dh_ref, dg_ref):
        @pl.when(pl.program_id(0) == 0)
        def _():
            dg_ref[...] = jnp.zeros_like(dg_ref)
            loss_ref[...] = jnp.zeros_like(loss_ref)

        err = y_ref[...] - t_ref[...]
        part = jnp.sum(jnp.mean(err * err, axis=-1, keepdims=True), axis=0, keepdims=True)
        loss_ref[...] += 0.5 * part
        hv = h_ref[...]
        dx, dgrow = _rms_bwd(err * (1.0 / D), hv, _rms(hv), g_ref[...])
        dh_ref[...] = dx
        dg_ref[...] += _colsum(dgrow)

    cur = pl.BlockSpec((tm, D), lambda i: (i, 0))
    vec = pl.BlockSpec((1, D), lambda i: (0, 0))
    return pl.pallas_call(
        body, name="loss_bwd", grid=(S // tm,), in_specs=[cur, cur, vec, cur],
        out_specs=[pl.BlockSpec((1, SLOT), lambda i: (0, 0)), cur, vec],
        out_shape=[jax.ShapeDtypeStruct((1, SLOT), F32), jax.ShapeDtypeStruct((S, D), F32),
                   jax.ShapeDtypeStruct((1, D), F32)],
        compiler_params=_cp(),
    )(xn, h, g, target)


def _place():
    x, y, c = lax.axis_index("x"), lax.axis_index("y"), lax.axis_index("c")
    chips = [(1 - x, y), (x, 1 - y), (1 - x, 1 - y)]
    return x, y, c, chips


def _all_gather_weights(bufs):
    nbuf = len(bufs)

    def body(*refs):
        ins, outs = refs[:nbuf], refs[nbuf:2 * nbuf]
        ssem, rsem, lsem = refs[2 * nbuf:]
        x, y, c, chips = _place()
        mine = 2 * x + y
        local = [pltpu.make_async_copy(ins[b], outs[b].at[:, pl.ds(mine, 1)], lsem.at[b]) for b in range(nbuf)]
        for cp in local:
            cp.start()

        def copy(b, p, shard):
            return pltpu.make_async_remote_copy(
                src_ref=ins[b], dst_ref=outs[b].at[:, pl.ds(shard, 1)],
                send_sem=ssem.at[3 * b + p], recv_sem=rsem.at[3 * b + p],
                device_id=(chips[p][0], chips[p][1], c), device_id_type=MESH)

        sends = [copy(b, p, mine) for b in range(nbuf) for p in range(3)]
        for cp in sends:
            cp.start()
        for b in range(nbuf):
            for p in range(3):
                copy(b, p, 2 * chips[p][0] + chips[p][1]).wait_recv()
        for cp in sends:
            cp.wait_send()
        for cp in local:
            cp.wait()

    hbm = pl.BlockSpec(memory_space=pl.ANY)
    return pl.pallas_call(
        body, name="all_gather_weights",
        in_specs=[hbm] * nbuf, out_specs=[hbm] * nbuf,
        out_shape=[jax.ShapeDtypeStruct((b.shape[0], 4) + b.shape[2:], b.dtype) for b in bufs],
        scratch_shapes=[pltpu.SemaphoreType.DMA((3 * nbuf,)), pltpu.SemaphoreType.DMA((3 * nbuf,)),
                        pltpu.SemaphoreType.DMA((nbuf,))],
    )(*bufs)


def _scatter_grads(bufs):
    nbuf = len(bufs)

    def body(*refs):
        ins, outs = refs[:nbuf], refs[nbuf:2 * nbuf]
        ssem, rsem = refs[2 * nbuf:]
        x, y, c, chips = _place()

        def copy(b, p):
            shard = 2 * chips[p][0] + chips[p][1]
            return pltpu.make_async_remote_copy(
                src_ref=ins[b].at[:, pl.ds(shard, 1)], dst_ref=outs[b].at[p],
                send_sem=ssem.at[3 * b + p], recv_sem=rsem.at[3 * b + p],
                device_id=(chips[p][0], chips[p][1], c), device_id_type=MESH)

        sends = [copy(b, p) for b in range(nbuf) for p in range(3)]
        for cp in sends:
            cp.start()
        for cp in sends:
            cp.wait_recv()
        for cp in sends:
            cp.wait_send()

    hbm = pl.BlockSpec(memory_space=pl.ANY)
    return pl.pallas_call(
        body, name="scatter_grads",
        in_specs=[hbm] * nbuf, out_specs=[hbm] * nbuf,
        out_shape=[jax.ShapeDtypeStruct((3, b.shape[0], 1) + b.shape[2:], b.dtype) for b in bufs],
        scratch_shapes=[pltpu.SemaphoreType.DMA((3 * nbuf,)), pltpu.SemaphoreType.DMA((3 * nbuf,))],
    )(*bufs)


def _swap_with_sibling(bufs):
    nbuf = len(bufs)

    def body(*refs):
        ins, outs = refs[:nbuf], refs[nbuf:2 * nbuf]
        ssem, rsem = refs[2 * nbuf:]
        x, y, c, _ = _place()
        sends = [pltpu.make_async_remote_copy(src_ref=ins[b], dst_ref=outs[b], send_sem=ssem.at[b],
                                              recv_sem=rsem.at[b], device_id=(x, y, 1 - c), device_id_type=MESH)
                 for b in range(nbuf)]
        for cp in sends:
            cp.start()
        for cp in sends:
            cp.wait_recv()
        for cp in sends:
            cp.wait_send()

    hbm = pl.BlockSpec(memory_space=pl.ANY)
    return pl.pallas_call(
        body, name="swap_with_sibling",
        in_specs=[hbm] * nbuf, out_specs=[hbm] * nbuf,
        out_shape=[jax.ShapeDtypeStruct(b.shape, b.dtype) for b in bufs],
        scratch_shapes=[pltpu.SemaphoreType.DMA((nbuf,)), pltpu.SemaphoreType.DMA((nbuf,))],
    )(*bufs)


def _all_gather_small(buf):
    _, R, W = buf.shape

    def body(in_ref, out_ref, ssem, rsem, lsem):
        x, y, c, _ = _place()
        me = 4 * x + 2 * y + c
        local = pltpu.make_async_copy(in_ref, out_ref.at[pl.ds(me, 1)], lsem)
        local.start()

        def copy(k, block):
            fx, fy, fc = (k >> 2) & 1, (k >> 1) & 1, k & 1
            peer = (x ^ fx, y ^ fy, c ^ fc)
            return pltpu.make_async_remote_copy(
                src_ref=in_ref, dst_ref=out_ref.at[pl.ds(block, 1)], send_sem=ssem.at[k - 1],
                recv_sem=rsem.at[k - 1], device_id=peer, device_id_type=MESH)

        sends = [copy(k, me) for k in range(1, 8)]
        for cp in sends:
            cp.start()
        for k in range(1, 8):
            copy(k, me ^ k).wait_recv()
        for cp in sends:
            cp.wait_send()
        local.wait()

    hbm = pl.BlockSpec(memory_space=pl.ANY)
    return pl.pallas_call(
        body, name="all_gather_small", in_specs=[hbm], out_specs=hbm,
        out_shape=jax.ShapeDtypeStruct((8, R, W), buf.dtype),
        scratch_shapes=[pltpu.SemaphoreType.DMA((7,)), pltpu.SemaphoreType.DMA((7,)), pltpu.SemaphoreType.DMA],
    )(buf)


def _row_tile(n):
    for t in range(min(n, 1024) // 8 * 8, 7, -8):
        if n % t == 0:
            return t
    return n


def _sum4(own, recv, q):
    n, rows, D = own.shape
    tr = _row_tile(rows)

    def body(o_ref, r0_ref, r1_ref, r2_ref, out_ref):
        out_ref[...] = ((o_ref[...].astype(F32) + r0_ref[...].astype(F32)) + r1_ref[...].astype(F32)) \
            + r2_ref[...].astype(F32)

    def rspec(p):
        return pl.BlockSpec((None, None, None, tr, D), lambda k, i, p=p: (p, q * n + k, 0, i, 0))

    blk = pl.BlockSpec((None, tr, D), lambda k, i: (k, i, 0))
    return pl.pallas_call(
        body, name="sum4", grid=(n, rows // tr),
        in_specs=[blk, rspec(0), rspec(1), rspec(2)], out_specs=blk,
        out_shape=jax.ShapeDtypeStruct((n, rows, D), F32),
    )(own, recv, recv, recv)


def _add2(a, b):
    rows, D = a.shape
    tr = _row_tile(rows)

    def body(a_ref, b_ref, o_ref):
        o_ref[...] = a_ref[...] + b_ref[...]

    blk = pl.BlockSpec((tr, D), lambda i: (i, 0))
    return pl.pallas_call(body, name="add2", grid=(rows // tr,), in_specs=[blk, blk], out_specs=blk,
                          out_shape=jax.ShapeDtypeStruct((rows, D), F32))(a, b)


def _adam_math(w, g, m, v):
    m = ADAM_B1 * m + (1.0 - ADAM_B1) * g
    v = ADAM_B2 * v + (1.0 - ADAM_B2) * (g * g)
    m_hat = m / (1.0 - ADAM_B1 ** ADAM_STEP)
    v_hat = v / (1.0 - ADAM_B2 ** ADAM_STEP)
    delta = -ADAM_LR * (m_hat / (jnp.sqrt(v_hat) + ADAM_EPS) + ADAM_WD * w)
    return delta, m, v


def _adam(w, g, m, v):
    rows, cols = w.shape
    tr = _row_tile(rows)

    def body(w_ref, g_ref, m_ref, v_ref, d_ref, nm_ref, nv_ref):
        d, m_, v_ = _adam_math(w_ref[...], g_ref[...], m_ref[...], v_ref[...])
        d_ref[...] = d
        nm_ref[...] = m_
        nv_ref[...] = v_

    blk = pl.BlockSpec((tr, cols), lambda i: (i, 0))
    return pl.pallas_call(body, name="adam", grid=(rows // tr,), in_specs=[blk] * 4, out_specs=[blk] * 3,
                          out_shape=[jax.ShapeDtypeStruct((rows, cols), F32)] * 3)(w, g, m, v)


def _small_sum_adam(g8, w, m, v):
    _, R, W = g8.shape

    def body(g_ref, w_ref, m_ref, v_ref, go_ref, d_ref, nm_ref, nv_ref):
        g = g_ref[0]
        for k in range(1, 8):
            g = g + g_ref[k]
        go_ref[...] = g
        d, m_, v_ = _adam_math(w_ref[...], g, m_ref[...], v_ref[...])
        d_ref[...] = d
        nm_ref[...] = m_
        nv_ref[...] = v_

    return pl.pallas_call(body, name="small_sum_adam",
                          out_shape=[jax.ShapeDtypeStruct((R, W), F32)] * 4)(g8, w, m, v)


def _pad_heads_rows(w, first):
    lead, D = w.shape[:-2], w.shape[-1]
    heads = w[..., first:, :]
    n = heads.shape[-2] // HEAD_DIM
    heads = heads.reshape(lead + (n, HEAD_DIM, D))
    heads = jnp.pad(heads, [(0, 0)] * (len(lead) + 1) + [(0, SLOT - HEAD_DIM), (0, 0)])
    return jnp.concatenate([w[..., :first, :], heads.reshape(lead + (n * SLOT, D))], axis=-2)


def _unpad_heads_rows(w, first):
    lead, D = w.shape[:-2], w.shape[-1]
    heads = w[..., first:, :]
    n = heads.shape[-2] // SLOT
    heads = heads.reshape(lead + (n, SLOT, D))[..., :HEAD_DIM, :]
    return jnp.concatenate([w[..., :first, :], heads.reshape(lead + (n * HEAD_DIM, D))], axis=-2)


def _pad_vec(v):
    return jnp.pad(v, (0, SLOT - v.shape[0]))[None, :]


class _Pack:
    def __init__(self, shapes):
        self.shapes = shapes
        self.sizes = [int(functools.reduce(lambda a, b: a * b, s, 1)) for s in shapes]
        total = sum(self.sizes)
        self.rows = -(-total // (8 * SLOT)) * 8
        self.pad = self.rows * SLOT - total

    def pack(self, arrs):
        flat = jnp.concatenate([a.reshape(-1).astype(F32) for a in arrs] + [jnp.zeros((self.pad,), F32)])
        return flat.reshape(self.rows, SLOT)

    def unpack(self, buf):
        flat, out, o = buf.reshape(-1), [], 0
        for s, n in zip(self.shapes, self.sizes):
            out.append(flat[o:o + n].reshape(s))
            o += n
        return out


def kernel(x, mem, positions, ffn1_norm, ffn1_w1, ffn1_w3, ffn1_w2, mix_norm, w_in, conv_w, conv_b, conv_ln_g, conv_ln_b, swa_q_norm, swa_k_norm, swa_sinks, mem_norm, w_mem_kv, mem_q_norm, mem_k_norm, w_out, ffn2_norm, ffn2_w1, ffn2_w3, ffn2_w2, final_norm, loss_target, m_ffn1_norm, m_ffn1_w1, m_ffn1_w3, m_ffn1_w2, m_mix_norm, m_w_in, m_conv_w, m_conv_b, m_conv_ln_g, m_conv_ln_b, m_swa_q_norm, m_swa_k_norm, m_swa_sinks, m_mem_norm, m_w_mem_kv, m_mem_q_norm, m_mem_k_norm, m_w_out, m_ffn2_norm, m_ffn2_w1, m_ffn2_w3, m_ffn2_w2, m_final_norm, v_ffn1_norm, v_ffn1_w1, v_ffn1_w3, v_ffn1_w2, v_mix_norm, v_w_in, v_conv_w, v_conv_b, v_conv_ln_g, v_conv_ln_b, v_swa_q_norm, v_swa_k_norm, v_swa_sinks, v_mem_norm, v_w_mem_kv, v_mem_q_norm, v_mem_k_norm, v_w_out, v_ffn2_norm, v_ffn2_w1, v_ffn2_w3, v_ffn2_w2, v_final_norm):
    names = ['ffn1_norm', 'ffn1_w1', 'ffn1_w3', 'ffn1_w2', 'mix_norm', 'w_in', 'conv_w', 'conv_b', 'conv_ln_g',
             'conv_ln_b', 'swa_q_norm', 'swa_k_norm', 'swa_sinks', 'mem_norm', 'w_mem_kv', 'mem_q_norm',
             'mem_k_norm', 'w_out', 'ffn2_norm', 'ffn2_w1', 'ffn2_w3', 'ffn2_w2', 'final_norm']
    loc = locals()
    W = {n: loc[n] for n in names}
    M1 = {n: loc['m_' + n] for n in names}
    V1 = {n: loc['v_' + n] for n in names}

    S, D = x.shape[1], x.shape[2]
    L = ffn1_norm.shape[0]
    Fs = ffn1_w1.shape[2]
    F = 4 * Fs
    Mlen = mem.shape[1]
    cw_sh = conv_w.shape[2]
    tm = 512 if S >= 2048 else 256
    tf = 256
    tmw = 1024 if S >= 2048 else 256
    x0 = x[0]
    mem0 = mem[0]
    target = loss_target[0]
    my_chip = 2 * lax.axis_index("x") + lax.axis_index("y")

    tr = lambda w: jnp.swapaxes(w, 1, 2)
    ffn_sh = jnp.stack([tr(ffn1_w1), tr(ffn1_w3), ffn1_w2, tr(ffn2_w1), tr(ffn2_w3), ffn2_w2], axis=1)
    ffn_sh = ffn_sh.astype(BF16).reshape(L * 6, 1, Fs, D)
    mkv_rows = w_mem_kv.shape[1] * MEM_KV // D
    mix_sh = jnp.concatenate([tr(w_in), w_out, w_mem_kv.reshape(L, mkv_rows, D)], axis=1).astype(BF16)
    rm = mix_sh.shape[1]
    mix_sh = mix_sh.reshape(L, 1, rm, D)
    cw_rows = -(-(L * CONV_WIDTH) // 8) * 8
    cw_pad = jnp.pad(conv_w.reshape(L * CONV_WIDTH, cw_sh), ((0, cw_rows - L * CONV_WIDTH), (0, SLOT - cw_sh)))
    ffn_g, mix_g, cw_g = _all_gather_weights([ffn_sh, mix_sh, cw_pad.reshape(1, 1, cw_rows, SLOT)])
    wf = ffn_g.reshape(L, 6, F, D)
    r_in, r_out = D_IN // 4, D_MIX // 4
    w_inT = mix_g[:, :, :r_in].reshape(L, D_IN, D)
    w_outF = mix_g[:, :, r_in:r_in + r_out].reshape(L, D_MIX, D)
    w_mkvF = mix_g[:, :, r_in + r_out:].reshape(L, D, MEM_KV)
    w_inP = _pad_heads_rows(w_inT, 2 * CONV_CH)
    w_outP = _pad_heads_rows(w_outF, CONV_CH)
    w_mkvP = jnp.pad(w_mkvF.reshape(L, D, 2 * N_MEMH, HEAD_DIM),
                     ((0, 0), (0, 0), (0, 0), (0, SLOT - HEAD_DIM))).reshape(L, D, 2 * N_MEMH * SLOT)
    conv_wF = cw_g[0, :, :L * CONV_WIDTH, :cw_sh].reshape(4, L, CONV_WIDTH, cw_sh)
    conv_wF = jnp.moveaxis(conv_wF, 0, 2).reshape(L, CONV_WIDTH, 4 * cw_sh)
    conv_wP = jnp.pad(conv_wF, ((0, 0), (0, 32 - CONV_WIDTH), (0, 0)))

    inv_freq = ROPE_THETA ** (-jnp.arange(0, HEAD_DIM, 2, dtype=F32) / HEAD_DIM)
    invf = jnp.concatenate([inv_freq, inv_freq, jnp.zeros((SLOT - HEAD_DIM,), F32)])[None, :]
    cosT, sinT = _rope_tables(positions.reshape(S, 1), invf, tm)

    row = lambda a, l: a[l][None, :]
    sinks_p = jnp.pad(swa_sinks, ((0, 0), (0, 8 - N_Q)))

    saved = []
    xin = x0
    xn = None
    for l in range(L):
        h1, a1, b1 = _ffn_fwd(xin, row(ffn1_norm, l), wf, l, 0, None, tm, tf)
        p, n2 = _proj_fwd(h1, row(mix_norm, l), w_inP[l], tm)
        gk_m = _pad_vec(mem_k_norm[l])
        nm, mraw, mk, mv = _mem_kv_fwd(mem0, row(mem_norm, l), w_mkvP[l], gk_m)
        gq, gk, gqm = _pad_vec(swa_q_norm[l]), _pad_vec(swa_k_norm[l]), _pad_vec(mem_q_norm[l])
        h2, y, yc, lse = _mixer_fwd(p, h1, cosT, sinT, conv_wP[l], row(conv_b, l), row(conv_ln_g, l),
                                    row(conv_ln_b, l), gq, gk, sinks_p[l], mk, mv, gqm, w_outP[l], tm)
        h3, a2, b2, xn = _ffn_fwd(h2, row(ffn2_norm, l), wf, l, 1, row(final_norm, l), tm, tf)
        saved.append(dict(xin=xin, h1=h1, a1=a1, b1=b1, p=p, n2=n2, nm=nm, mraw=mraw, mk=mk, mv=mv, gk_m=gk_m,
                          gq=gq, gk=gk, gqm=gqm, h2=h2, y=y, yc=yc, lse=lse, h3=h3, a2=a2, b2=b2))
        xin = xn

    G = {n: [None] * L for n in names}
    ffn_grads = [None] * (2 * L)
    mix_grads = [None] * L
    dxn = None
    loss_part = None
    for l in reversed(range(L)):
        sv = saved[l]
        if l == L - 1:
            loss_part, dh3, G['final_norm'][l] = _loss_bwd(xn, sv['h3'], row(final_norm, l), target, tm)
        else:
            dh3, G['final_norm'][l] = _norm_bwd(dxn, sv['h3'], row(final_norm, l), tm)
        dh2, G['ffn2_norm'][l], da, db, t, n, dy = _ffn_bwd_act(dh3, sv['h2'], row(ffn2_norm, l), sv['a2'], sv['b2'],
                                                               wf, l, 1, tm, tf)
        ffn_grads[2 * l + 1] = _ffn_bwd_w(da, db, t, n, dy, tmw, tf)
        dyv, delta, dwo, G['conv_ln_g'][l], G['conv_ln_b'][l] = _outproj_bwd(
            dh2, sv['y'], sv['yc'], row(conv_ln_g, l), row(conv_ln_b, l), w_outP[l], tm)
        dp, dcw, G['conv_b'][l], dgq, dgk, dgqm, dsink, dmk, dmv = _mixer_bwd(
            sv['p'], dyv, sv['lse'], delta, cosT, sinT, conv_wP[l], sv['gq'], sv['gk'], sinks_p[l],
            sv['mk'], sv['mv'], sv['gqm'], tm)
        dwm, G['mem_norm'][l], dgk_m = _mem_kv_bwd(dmk, dmv, sv['mraw'], sv['nm'], mem0, row(mem_norm, l),
                                                   w_mkvP[l], sv['gk_m'])
        dh1, G['mix_norm'][l], dwi = _proj_bwd(dp, sv['h1'], dh2, row(mix_norm, l), sv['n2'], w_inP[l], tm)
        dxl, G['ffn1_norm'][l], da, db, t, n, dy = _ffn_bwd_act(dh1, sv['xin'], row(ffn1_norm, l), sv['a1'], sv['b1'],
                                                               wf, l, 0, tm, tf)
        ffn_grads[2 * l] = _ffn_bwd_w(da, db, t, n, dy, tmw, tf)
        dxn = dxl
        G['conv_w'][l] = dcw[:CONV_WIDTH]
        G['swa_q_norm'][l] = dgq[0, :HEAD_DIM]
        G['swa_k_norm'][l] = dgk[0, :HEAD_DIM]
        G['mem_q_norm'][l] = dgqm[0, :HEAD_DIM]
        G['mem_k_norm'][l] = dgk_m[0, :HEAD_DIM]
        G['swa_sinks'][l] = dsink[0, :N_Q]
        dwiT = _unpad_heads_rows(dwi, 2 * CONV_CH).reshape(4, r_in, D)
        dwoF = _unpad_heads_rows(dwo, CONV_CH).reshape(4, r_out, D)
        dwmF = dwm.reshape(D, 2 * N_MEMH, SLOT)[:, :, :HEAD_DIM].reshape(4, mkv_rows, D)
        mix_grads[l] = jnp.concatenate([dwiT, dwoF, dwmF], axis=1).astype(BF16)
    grad_x = dxn[None]
    loss = lax.psum(loss_part[0, 0], AXES)

    ffn_bufs = [g.reshape(3, 4, Fs, D) for g in ffn_grads]
    mix_bufs = [g.reshape(1, 4, rm, D) for g in mix_grads]
    recv = _scatter_grads(ffn_bufs + mix_bufs)
    parts = []
    for q, b in enumerate(ffn_bufs + mix_bufs):
        own = lax.dynamic_index_in_dim(b, my_chip, axis=1, keepdims=False)
        parts.append(_sum4(own, recv[q], 0))
    theirs = _swap_with_sibling(parts)
    gsum = [_add2(a.reshape(-1, D), b.reshape(-1, D)).reshape(a.shape) for a, b in zip(parts, theirs)]
    for l in range(L):
        for f, pre in enumerate(('ffn1', 'ffn2')):
            g3 = gsum[2 * l + f]
            G[pre + '_w1'][l] = g3[0].T
            G[pre + '_w3'][l] = g3[1].T
            G[pre + '_w2'][l] = g3[2]
        gm = gsum[2 * L + l][0]
        G['w_in'][l] = gm[:r_in].T
        G['w_out'][l] = gm[r_in:r_in + r_out]
        G['w_mem_kv'][l] = gm[r_in + r_out:].reshape(w_mem_kv.shape[1], MEM_KV)

    small = ['ffn1_norm', 'mix_norm', 'conv_b', 'conv_ln_g', 'conv_ln_b', 'swa_q_norm', 'swa_k_norm', 'swa_sinks',
             'mem_norm', 'mem_q_norm', 'mem_k_norm', 'ffn2_norm', 'final_norm']
    gsmall = [jnp.stack([G[n][l].reshape(-1) for l in range(L)]) for n in small]
    gcw = jnp.stack(G['conv_w'])
    cw_cols = 4 * cw_sh
    full_of = lambda a: lax.dynamic_update_slice(jnp.zeros((L, CONV_WIDTH, cw_cols), F32), a, (0, 0, my_chip * cw_sh))
    pk = _Pack([W[n].shape for n in small] + [(L, CONV_WIDTH, cw_cols)])
    g8 = _all_gather_small(pk.pack(gsmall + [gcw])[None])
    outs4 = _small_sum_adam(g8, pk.pack([W[n] for n in small] + [full_of(conv_w)]),
                            pk.pack([M1[n] for n in small] + [full_of(m_conv_w)]),
                            pk.pack([V1[n] for n in small] + [full_of(v_conv_w)]))
    un = [pk.unpack(o) for o in outs4]
    grads, deltas, new_m, new_v = {}, {}, {}, {}
    for k, n in enumerate(small):
        grads[n], deltas[n], new_m[n], new_v[n] = un[0][k], un[1][k], un[2][k], un[3][k]
    mine = lambda a: lax.dynamic_slice(a, (0, 0, my_chip * cw_sh), (L, CONV_WIDTH, cw_sh))
    grads['conv_w'], deltas['conv_w'], new_m['conv_w'], new_v['conv_w'] = [mine(u[-1]) for u in un]

    for n in ('ffn1_w1', 'ffn1_w3', 'ffn1_w2', 'w_in', 'w_mem_kv', 'w_out', 'ffn2_w1', 'ffn2_w3', 'ffn2_w2'):
        g = jnp.stack(G[n])
        shp = W[n].shape
        v2 = lambda a: a.reshape(-1, D)
        d_, m_, v_ = _adam(v2(W[n]), v2(g), v2(M1[n]), v2(V1[n]))
        grads[n], deltas[n], new_m[n], new_v[n] = g, d_.reshape(shp), m_.reshape(shp), v_.reshape(shp)

    return (loss, grad_x, *[grads[n] for n in names], *[deltas[n] for n in names],
            *[new_m[n] for n in names], *[new_v[n] for n in names])
```

```python
import functools

import jax
import jax.numpy as jnp
from jax import lax
from jax.experimental import pallas as pl
from jax.experimental.pallas import tpu as pltpu

F32 = jnp.float32
BF16 = jnp.bfloat16
MESH = pl.DeviceIdType.MESH
AXES = ("x", "y", "c")

EPS = 1e-6
HEAD_DIM = 64
SLOT = 128
CONV_CH = 384
CONV_WIDTH = 31
N_Q, N_KV, N_MEMH = 6, 2, 4
GROUP = N_Q // N_KV
BLK = 128
ROPE_THETA = 10000.0
SCALE = HEAD_DIM ** -0.5
NEG = -1e30

N_SLOTS = N_Q + 2 * N_KV + N_MEMH
PP = 2 * CONV_CH + SLOT * N_SLOTS
QO = 2 * CONV_CH
KO = QO + SLOT * N_Q
VO = KO + SLOT * N_KV
MO = VO + SLOT * N_KV
YP = CONV_CH + SLOT * (N_Q + N_MEMH)
YS = CONV_CH
YM = YS + SLOT * N_Q
D_IN = 2 * CONV_CH + HEAD_DIM * N_SLOTS
D_MIX = CONV_CH + HEAD_DIM * (N_Q + N_MEMH)
MEM_KV = 2 * HEAD_DIM * N_MEMH

ADAM_LR, ADAM_B1, ADAM_B2, ADAM_EPS, ADAM_WD, ADAM_STEP = 0.001, 0.9, 0.999, 1e-08, 0.01, 10

VMEM_LIMIT_MB = 56


def _cp(mb=VMEM_LIMIT_MB):
    return pltpu.CompilerParams(vmem_limit_bytes=mb * 1024 * 1024)


def _dot_nn(a, b):
    return lax.dot_general(a, b, (((1,), (0,)), ((), ())), preferred_element_type=F32)


def _dot_nt(a, b):
    return lax.dot_general(a, b, (((1,), (1,)), ((), ())), preferred_element_type=F32)


def _dot_tn(a, b):
    return lax.dot_general(a, b, (((0,), (0,)), ((), ())), preferred_element_type=F32)


def _sigmoid(x):
    return 1.0 / (1.0 + jnp.exp(-x))


def _rms(x):
    return lax.rsqrt(jnp.mean(x * x, axis=-1, keepdims=True) + EPS)


def _rms_bwd(dn, x, r, g):
    xhat = x * r
    dxhat = dn * g
    dx = r * (dxhat - xhat * jnp.mean(dxhat * xhat, axis=-1, keepdims=True))
    return dx, dn * xhat


def _colsum(v):
    return jnp.sum(v, axis=0, keepdims=True)


def _lane(n):
    return lax.broadcasted_iota(jnp.int32, (n, SLOT), 1)


def _partner(v, lane):
    up = pltpu.roll(v, SLOT - HEAD_DIM // 2, 1)
    dn = pltpu.roll(v, HEAD_DIM // 2, 1)
    return jnp.where(lane < HEAD_DIM // 2, up, jnp.where(lane < HEAD_DIM, dn, 0.0))


def _head_rms(xs):
    return lax.rsqrt(jnp.sum(xs * xs, axis=-1, keepdims=True) * (1.0 / HEAD_DIM) + EPS)


def _head_fwd(xs, g, cosv, sinv, lane):
    xn = xs * _head_rms(xs) * g
    if cosv is None:
        return xn
    return xn * cosv + _partner(xn, lane) * sinv


def _head_bwd(dout, xs, g, cosv, sinv, lane):
    if cosv is not None:
        dout = dout * cosv + _partner(dout * sinv, lane)
    r = _head_rms(xs)
    xhat = xs * r
    dxhat = dout * g
    dx = r * (dxhat - xhat * (jnp.sum(dxhat * xhat, axis=-1, keepdims=True) * (1.0 / HEAD_DIM)))
    return dx, dout * xhat


def _col(v, h, lane):
    return jnp.sum(jnp.where(lane == h, v, 0.0), axis=-1, keepdims=True)


def _place():
    x, y, c = lax.axis_index("x"), lax.axis_index("y"), lax.axis_index("c")
    chips = [(1 - x, y), (x, 1 - y), (1 - x, 1 - y)]
    return x, y, c, chips


class _Gather:
    tag = "_gather"

    def __init__(self, bufs):
        self.bufs = list(bufs)
        nb = len(self.bufs)
        self.out_shape = [jax.ShapeDtypeStruct((b.shape[0], 4) + b.shape[2:], b.dtype) for b in self.bufs]
        self.sems = [pltpu.SemaphoreType.DMA((3 * nb,)), pltpu.SemaphoreType.DMA((3 * nb,)),
                     pltpu.SemaphoreType.DMA((nb,))]

    def _copies(self, ins, outs, sems):
        ssem, rsem, lsem = sems
        nb = len(self.bufs)
        x, y, c, chips = _place()
        mine = 2 * x + y

        def copy(b, p, shard):
            return pltpu.make_async_remote_copy(
                src_ref=ins[b], dst_ref=outs[b].at[:, pl.ds(shard, 1)],
                send_sem=ssem.at[3 * b + p], recv_sem=rsem.at[3 * b + p],
                device_id=(chips[p][0], chips[p][1], c), device_id_type=MESH)

        local = [pltpu.make_async_copy(ins[b], outs[b].at[:, pl.ds(mine, 1)], lsem.at[b]) for b in range(nb)]
        sends = [copy(b, p, mine) for b in range(nb) for p in range(3)]
        recvs = [copy(b, p, 2 * chips[p][0] + chips[p][1]) for b in range(nb) for p in range(3)]
        return local, sends, recvs

    def start(self, ins, outs, sems):
        local, sends, _ = self._copies(ins, outs, sems)
        for cp in local + sends:
            cp.start()

    def wait(self, ins, outs, sems):
        local, sends, recvs = self._copies(ins, outs, sems)
        for cp in recvs:
            cp.wait_recv()
        for cp in sends:
            cp.wait_send()
        for cp in local:
            cp.wait()


class _Scatter:
    tag = "_scatter"

    def __init__(self, bufs):
        self.bufs = list(bufs)
        nb = len(self.bufs)
        self.out_shape = [jax.ShapeDtypeStruct((3, b.shape[0], 1) + b.shape[2:], b.dtype) for b in self.bufs]
        self.sems = [pltpu.SemaphoreType.DMA((3 * nb,)), pltpu.SemaphoreType.DMA((3 * nb,))]

    def _copies(self, ins, outs, sems):
        ssem, rsem = sems
        x, y, c, chips = _place()

        def copy(b, p):
            shard = 2 * chips[p][0] + chips[p][1]
            return pltpu.make_async_remote_copy(
                src_ref=ins[b].at[:, pl.ds(shard, 1)], dst_ref=outs[b].at[p],
                send_sem=ssem.at[3 * b + p], recv_sem=rsem.at[3 * b + p],
                device_id=(chips[p][0], chips[p][1], c), device_id_type=MESH)

        return [copy(b, p) for b in range(len(self.bufs)) for p in range(3)]

    def start(self, ins, outs, sems):
        for cp in self._copies(ins, outs, sems):
            cp.start()

    def wait(self, ins, outs, sems):
        cps = self._copies(ins, outs, sems)
        for cp in cps:
            cp.wait_recv()
        for cp in cps:
            cp.wait_send()


def _run_rider(rider, name):
    nb = len(rider.bufs)

    def body(*refs):
        ins, outs, sems = refs[:nb], refs[nb:2 * nb], refs[2 * nb:]
        rider.start(ins, outs, sems)
        rider.wait(ins, outs, sems)

    hbm = pl.BlockSpec(memory_space=pl.ANY)
    return pl.pallas_call(body, name=name, in_specs=[hbm] * nb, out_specs=[hbm] * nb,
                          out_shape=rider.out_shape, scratch_shapes=rider.sems)(*rider.bufs)


def _call(body, *, name, grid, in_specs, out_specs, out_shape, args, scratch=(), rider=None):
    if rider is None:
        outs = pl.pallas_call(body, name=name, grid=grid, in_specs=list(in_specs), out_specs=list(out_specs),
                              out_shape=list(out_shape), scratch_shapes=list(scratch),
                              compiler_params=_cp())(*args)
        return list(outs), None
    n_in, n_out, n_scr, nb = len(in_specs), len(out_specs), len(scratch), len(rider.bufs)

    def wrapped(*refs):
        cuts = [n_in, nb, n_out, nb, n_scr]
        parts, o = [], 0
        for n in cuts:
            parts.append(refs[o:o + n])
            o += n
        ins, rin, outs, rout, scr = parts
        sems = refs[o:]
        ids = [pl.program_id(k) for k in range(len(grid))]
        first = functools.reduce(jnp.logical_and, [i == 0 for i in ids])
        last = functools.reduce(jnp.logical_and, [i == n - 1 for i, n in zip(ids, grid)])

        @pl.when(first)
        def _():
            rider.start(rin, rout, sems)

        body(*ins, *outs, *scr)

        @pl.when(last)
        def _():
            rider.wait(rin, rout, sems)

    hbm = pl.BlockSpec(memory_space=pl.ANY)
    res = pl.pallas_call(
        wrapped, name=name + rider.tag, grid=grid,
        in_specs=list(in_specs) + [hbm] * nb, out_specs=list(out_specs) + [hbm] * nb,
        out_shape=list(out_shape) + rider.out_shape, scratch_shapes=list(scratch) + rider.sems,
        compiler_params=_cp())(*args, *rider.bufs)
    return list(res[:n_out]), list(res[n_out:])


def _rope_tables(pos, invf, tm):
    S = pos.shape[0]

    def body(pos_ref, f_ref, cos_ref, sin_ref):
        ang = pos_ref[...].astype(F32) * f_ref[...]
        lane = _lane(tm)
        c, s = jnp.cos(ang), jnp.sin(ang)
        cos_ref[...] = jnp.where(lane < HEAD_DIM, c, 0.0)
        sin_ref[...] = jnp.where(lane < HEAD_DIM // 2, -s, jnp.where(lane < HEAD_DIM, s, 0.0))

    return pl.pallas_call(
        body, name="rope_tables", grid=(S // tm,),
        in_specs=[pl.BlockSpec((tm, 1), lambda i: (i, 0)), pl.BlockSpec((1, SLOT), lambda i: (0, 0))],
        out_specs=[pl.BlockSpec((tm, SLOT), lambda i: (i, 0))] * 2,
        out_shape=[jax.ShapeDtypeStruct((S, SLOT), F32)] * 2,
    )(pos, invf)


def _ffn_fwd(x, g, wf, gfin, tm, tf, rider=None):
    S, D = x.shape
    F = wf.shape[1]
    nf = F // tf
    final = gfin is not None

    def body(*refs):
        if final:
            x_ref, g_ref, w1_ref, w3_ref, w2_ref, gf_ref, h_ref, a_ref, b_ref, xn_ref, n_scr, acc = refs
        else:
            x_ref, g_ref, w1_ref, w3_ref, w2_ref, h_ref, a_ref, b_ref, n_scr, acc = refs
        j = pl.program_id(1)

        @pl.when(j == 0)
        def _():
            xv = x_ref[...]
            n_scr[...] = (xv * _rms(xv) * g_ref[...]).astype(BF16)
            acc[...] = jnp.zeros_like(acc)

        n = n_scr[...]
        a = _dot_nt(n, w1_ref[...])
        b = _dot_nt(n, w3_ref[...])
        a_ref[...] = a.astype(BF16)
        b_ref[...] = b.astype(BF16)
        t = (a * _sigmoid(a) * b).astype(BF16)
        acc[...] += _dot_nn(t, w2_ref[...])

        @pl.when(j == nf - 1)
        def _():
            h = x_ref[...] + 0.5 * acc[...]
            h_ref[...] = h
            if final:
                xn_ref[...] = h * _rms(h) * gf_ref[...]

    def wspec(k):
        return pl.BlockSpec((None, tf, D), lambda i, j, k=k: (k, j, 0))

    row = pl.BlockSpec((tm, D), lambda i, j: (i, 0))
    vec = pl.BlockSpec((1, D), lambda i, j: (0, 0))
    act = pl.BlockSpec((tm, tf), lambda i, j: (i, j))
    in_specs = [row, vec, wspec(0), wspec(1), wspec(2)] + ([vec] if final else [])
    out_specs = [row, act, act] + ([row] if final else [])
    out_shape = [jax.ShapeDtypeStruct((S, D), F32), jax.ShapeDtypeStruct((S, F), BF16),
                 jax.ShapeDtypeStruct((S, F), BF16)] + ([jax.ShapeDtypeStruct((S, D), F32)] if final else [])
    args = [x, g, wf, wf, wf] + ([gfin] if final else [])
    return _call(body, name="ffn_fwd_final" if final else "ffn_fwd", grid=(S // tm, nf),
                 in_specs=in_specs, out_specs=out_specs, out_shape=out_shape, args=args,
                 scratch=[pltpu.VMEM((tm, D), BF16), pltpu.VMEM((tm, D), F32)], rider=rider)


def _ffn_bwd_act(dh, x, g, a, b, wf, tm, tf, rider=None):
    S, D = x.shape
    F = wf.shape[1]
    nf = F // tf

    def body(dh_ref, x_ref, g_ref, a_ref, b_ref, w1_ref, w3_ref, w2_ref,
             dx_ref, dg_ref, da_ref, db_ref, t_ref, n_ref, dy_ref, acc):
        i, j = pl.program_id(0), pl.program_id(1)

        @pl.when(j == 0)
        def _():
            xv = x_ref[...]
            n_ref[...] = (xv * _rms(xv) * g_ref[...]).astype(BF16)
            dy_ref[...] = (0.5 * dh_ref[...]).astype(BF16)
            acc[...] = jnp.zeros_like(acc)

            @pl.when(i == 0)
            def _():
                dg_ref[...] = jnp.zeros_like(dg_ref)

        av = a_ref[...].astype(F32)
        bv = b_ref[...].astype(F32)
        sg = _sigmoid(av)
        s = av * sg
        dt = _dot_nt(dy_ref[...], w2_ref[...])
        t_ref[...] = (s * bv).astype(BF16)
        db = (dt * s).astype(BF16)
        da = (dt * bv * (sg * (1.0 + av * (1.0 - sg)))).astype(BF16)
        da_ref[...] = da
        db_ref[...] = db
        acc[...] += _dot_nn(da, w1_ref[...]) + _dot_nn(db, w3_ref[...])

        @pl.when(j == nf - 1)
        def _():
            xv = x_ref[...]
            dx, dgrow = _rms_bwd(acc[...], xv, _rms(xv), g_ref[...])
            dx_ref[...] = dh_ref[...] + dx
            dg_ref[...] += _colsum(dgrow)

    def wspec(k):
        return pl.BlockSpec((None, tf, D), lambda i, j, k=k: (k, j, 0))

    row = pl.BlockSpec((tm, D), lambda i, j: (i, 0))
    vec = pl.BlockSpec((1, D), lambda i, j: (0, 0))
    act = pl.BlockSpec((tm, tf), lambda i, j: (i, j))
    sd = lambda shp, dt: jax.ShapeDtypeStruct(shp, dt)
    return _call(body, name="ffn_bwd_act", grid=(S // tm, nf),
                 in_specs=[row, row, vec, act, act, wspec(0), wspec(1), wspec(2)],
                 out_specs=[row, vec, act, act, act, row, row],
                 out_shape=[sd((S, D), F32), sd((1, D), F32), sd((S, F), BF16), sd((S, F), BF16),
                            sd((S, F), BF16), sd((S, D), BF16), sd((S, D), BF16)],
                 args=[dh, x, g, a, b, wf, wf, wf], scratch=[pltpu.VMEM((tm, D), F32)], rider=rider)


def _ffn_bwd_w(da, db, t, n, dy, tm, tf):
    S, F = da.shape
    D = n.shape[1]
    nt = S // tm

    def body(da_ref, db_ref, t_ref, n_ref, dy_ref, out_ref, acc):
        i = pl.program_id(1)

        @pl.when(i == 0)
        def _():
            acc[...] = jnp.zeros_like(acc)

        nv = n_ref[...]
        acc[0] += _dot_tn(da_ref[...], nv)
        acc[1] += _dot_tn(db_ref[...], nv)
        acc[2] += _dot_tn(t_ref[...], dy_ref[...])

        @pl.when(i == nt - 1)
        def _():
            out_ref[...] = acc[...].astype(BF16)

    act = pl.BlockSpec((tm, tf), lambda j, i: (i, j))
    row = pl.BlockSpec((tm, D), lambda j, i: (i, 0))
    return pl.pallas_call(
        body, name="ffn_bwd_w", grid=(F // tf, nt),
        in_specs=[act, act, act, row, row],
        out_specs=pl.BlockSpec((3, tf, D), lambda j, i: (0, j, 0)),
        out_shape=jax.ShapeDtypeStruct((3, F, D), BF16),
        scratch_shapes=[pltpu.VMEM((3, tf, D), F32)],
        compiler_params=_cp(),
    )(da, db, t, n, dy)


def _proj_fwd(h, g, w_inp, tm):
    S, D = h.shape

    def body(h_ref, g_ref, w_ref, p_ref, n_ref):
        hv = h_ref[...]
        n = (hv * _rms(hv) * g_ref[...]).astype(BF16)
        n_ref[...] = n
        p_ref[...] = _dot_nt(n, w_ref[...])

    return pl.pallas_call(
        body, name="proj_fwd", grid=(S // tm,),
        in_specs=[pl.BlockSpec((tm, D), lambda i: (i, 0)), pl.BlockSpec((1, D), lambda i: (0, 0)),
                  pl.BlockSpec((PP, D), lambda i: (0, 0))],
        out_specs=[pl.BlockSpec((tm, PP), lambda i: (i, 0)), pl.BlockSpec((tm, D), lambda i: (i, 0))],
        out_shape=[jax.ShapeDtypeStruct((S, PP), F32), jax.ShapeDtypeStruct((S, D), BF16)],
        compiler_params=_cp(),
    )(h, g, w_inp)


def _glu(u):
    return u[:, :CONV_CH] * _sigmoid(u[:, CONV_CH:2 * CONV_CH])


def _layer_norm_stats(yc):
    mu = jnp.mean(yc, axis=-1, keepdims=True)
    d = yc - mu
    rstd = lax.rsqrt(jnp.mean(d * d, axis=-1, keepdims=True) + EPS)
    return d * rstd, rstd


def _mem_kv_fwd(mem, g, w_mkvp, gk):
    M, D = mem.shape
    W = SLOT * N_MEMH

    def body(mem_ref, g_ref, w_ref, gk_ref, nm_ref, raw_ref, mk_ref, mv_ref):
        mv_ = mem_ref[...]
        nm = (mv_ * _rms(mv_) * g_ref[...]).astype(BF16)
        nm_ref[...] = nm
        raw = _dot_nn(nm, w_ref[...])
        raw_ref[...] = raw
        for hh in range(N_MEMH):
            sl = slice(SLOT * hh, SLOT * (hh + 1))
            mk_ref[:, sl] = _head_fwd(raw[:, sl], gk_ref[...], None, None, None).astype(BF16)
        mv_ref[...] = raw[:, W:].astype(BF16)

    sd = jax.ShapeDtypeStruct
    return pl.pallas_call(
        body, name="mem_kv_fwd",
        out_shape=[sd((M, D), BF16), sd((M, 2 * W), F32), sd((M, W), BF16), sd((M, W), BF16)],
        compiler_params=_cp(),
    )(mem, g, w_mkvp, gk)


def _mem_kv_bwd(dmk, dmv, raw, nm, mem, g, w_mkvp, gk):
    M, D = mem.shape
    W = SLOT * N_MEMH

    def body(dmk_ref, dmv_ref, raw_ref, nm_ref, mem_ref, g_ref, w_ref, gk_ref, dw_ref, dg_ref, dgk_ref, draw):
        dgk = jnp.zeros((1, SLOT), F32)
        for hh in range(N_MEMH):
            sl = slice(SLOT * hh, SLOT * (hh + 1))
            dx, prod = _head_bwd(dmk_ref[:, sl], raw_ref[:, sl], gk_ref[...], None, None, None)
            draw[:, sl] = dx.astype(BF16)
            dgk = dgk + _colsum(prod)
        dgk_ref[...] = dgk
        draw[:, W:] = dmv_ref[...].astype(BF16)
        dr = draw[...]
        dw_ref[...] = _dot_tn(nm_ref[...], dr)
        dnm = _dot_nt(dr, w_ref[...])
        mv_ = mem_ref[...]
        dg_ref[...] = _colsum(dnm * (mv_ * _rms(mv_)))

    sd = jax.ShapeDtypeStruct
    return pl.pallas_call(
        body, name="mem_kv_bwd",
        out_shape=[sd((D, 2 * W), F32), sd((1, D), F32), sd((1, SLOT), F32)],
        scratch_shapes=[pltpu.VMEM((M, 2 * W), BF16)],
        compiler_params=_cp(),
    )(dmk, dmv, raw, nm, mem, g, w_mkvp, gk)


def _mixer_fwd(p, h, cosT, sinT, conv_w, conv_b, ln_g, ln_b, gq, gk, sinks, mk, mv, gqm, w_outp, tm, rider=None):
    S, D = h.shape
    M = mk.shape[0]
    nb = tm // BLK
    nblocks = S // BLK

    def body(p_ref, ph_ref, h_ref, cos_ref, cosh_ref, sin_ref, sinh_ref, cw_ref, cb_ref, lg_ref, lb_ref,
             gq_ref, gk_ref, sink_ref, mk_ref, mv_ref, gqm_ref, wo_ref,
             h2_ref, y_ref, yc_ref, lse_ref, ext, y_scr):
        i = pl.program_id(0)
        not_first = (i > 0).astype(F32)
        lane = _lane(tm)
        lane_e = _lane(tm + BLK)

        ext[0:BLK, :] = _glu(ph_ref[:, 0:2 * CONV_CH]) * not_first
        ext[BLK:BLK + tm, :] = _glu(p_ref[:, 0:2 * CONV_CH])
        yc = jnp.zeros((tm, CONV_CH), F32) + cb_ref[...]
        for k in range(CONV_WIDTH):
            off = BLK - (CONV_WIDTH - 1) + k
            yc = yc + cw_ref[k:k + 1, :] * ext[off:off + tm, :]
        yc_ref[...] = yc
        z, _ = _layer_norm_stats(yc)
        ln = z * lg_ref[...] + lb_ref[...]
        y_scr[:, 0:CONV_CH] = (ln * _sigmoid(ln)).astype(BF16)

        cos_e = jnp.concatenate([cosh_ref[...], cos_ref[...]], axis=0)
        sin_e = jnp.concatenate([sinh_ref[...], sin_ref[...]], axis=0)
        qi = lax.broadcasted_iota(jnp.int32, (BLK, 2 * BLK), 0)
        kj = lax.broadcasted_iota(jnp.int32, (BLK, 2 * BLK), 1)
        band = (kj > qi) & (kj <= qi + BLK)
        band0 = band & ((kj >= BLK) | (i > 0))
        lse = jnp.zeros((tm, SLOT), F32)
        for kvh in range(N_KV):
            ks = slice(KO + SLOT * kvh, KO + SLOT * (kvh + 1))
            vs = slice(VO + SLOT * kvh, VO + SLOT * (kvh + 1))
            k_raw = jnp.concatenate([ph_ref[:, ks], p_ref[:, ks]], axis=0)
            k_e = _head_fwd(k_raw, gk_ref[...], cos_e, sin_e, lane_e).astype(BF16)
            v_e = jnp.concatenate([ph_ref[:, vs], p_ref[:, vs]], axis=0).astype(BF16)
            for gi in range(GROUP):
                hq = GROUP * kvh + gi
                qs = slice(QO + SLOT * hq, QO + SLOT * (hq + 1))
                q = _head_fwd(p_ref[:, qs], gq_ref[...], cos_ref[...], sin_ref[...], lane).astype(BF16)
                sink = sink_ref[hq]
                outs, lses = [], []
                for m in range(nb):
                    rows = slice(BLK * m, BLK * (m + 1))
                    win = slice(BLK * m, BLK * (m + 2))
                    s = _dot_nt(q[rows], k_e[win]) * SCALE
                    s = jnp.where(band0 if m == 0 else band, s, NEG)
                    mx = jnp.maximum(jnp.max(s, axis=-1, keepdims=True), sink)
                    e = jnp.exp(s - mx)
                    den = jnp.sum(e, axis=-1, keepdims=True) + jnp.exp(sink - mx)
                    prob = e / den
                    outs.append(_dot_nn(prob.astype(BF16), v_e[win]))
                    lses.append(mx + jnp.log(den))
                y_scr[:, YS + SLOT * hq:YS + SLOT * (hq + 1)] = jnp.concatenate(outs, axis=0).astype(BF16)
                lse = jnp.where(lane == hq, jnp.concatenate(lses, axis=0), lse)

        for hm in range(N_MEMH):
            ms = slice(SLOT * hm, SLOT * (hm + 1))
            qm = _head_fwd(p_ref[:, MO + SLOT * hm:MO + SLOT * (hm + 1)], gqm_ref[...], None, None, None)
            s = _dot_nt(qm.astype(BF16), mk_ref[:, ms]) * SCALE
            mx = jnp.max(s, axis=-1, keepdims=True)
            e = jnp.exp(s - mx)
            den = jnp.sum(e, axis=-1, keepdims=True)
            o = _dot_nn((e / den).astype(BF16), mv_ref[:, ms])
            y_scr[:, YM + SLOT * hm:YM + SLOT * (hm + 1)] = o.astype(BF16)
            lse = jnp.where(lane == N_Q + hm, mx + jnp.log(den), lse)
        lse_ref[...] = lse

        yv = y_scr[...]
        y_ref[...] = yv
        h2_ref[...] = h_ref[...] + _dot_nn(yv, wo_ref[...])

    cur = lambda w: pl.BlockSpec((tm, w), lambda i: (i, 0))
    prev = lambda w: pl.BlockSpec((BLK, w), lambda i: (jnp.maximum(i * nb - 1, 0), 0))
    full = lambda a: pl.BlockSpec(a.shape, lambda i: (0,) * a.ndim)
    sd = jax.ShapeDtypeStruct
    return _call(
        body, name="mixer_fwd", grid=(S // tm,),
        in_specs=[cur(PP), prev(PP), cur(D), cur(SLOT), prev(SLOT), cur(SLOT), prev(SLOT),
                  full(conv_w), full(conv_b), full(ln_g), full(ln_b), full(gq), full(gk),
                  pl.BlockSpec(memory_space=pltpu.SMEM), full(mk), full(mv), full(gqm), full(w_outp)],
        out_specs=[cur(D), cur(YP), cur(CONV_CH), cur(SLOT)],
        out_shape=[sd((S, D), F32), sd((S, YP), BF16), sd((S, CONV_CH), F32), sd((S, SLOT), F32)],
        args=[p, p, h, cosT, cosT, sinT, sinT, conv_w, conv_b, ln_g, ln_b, gq, gk, sinks, mk, mv, gqm, w_outp],
        scratch=[pltpu.VMEM((tm + BLK, CONV_CH), F32), pltpu.VMEM((tm, YP), BF16)], rider=rider)


def _outproj_bwd(dh2, y, yc, ln_g, ln_b, w_outp, tm):
    S, D = dh2.shape
    NH = N_Q + N_MEMH

    def body(dh_ref, y_ref, yc_ref, lg_ref, lb_ref, wo_ref, dyv_ref, del_ref, dwo_ref, dlg_ref, dlb_ref):
        i = pl.program_id(0)

        @pl.when(i == 0)
        def _():
            dwo_ref[...] = jnp.zeros_like(dwo_ref)
            dlg_ref[...] = jnp.zeros_like(dlg_ref)
            dlb_ref[...] = jnp.zeros_like(dlb_ref)

        dhb = dh_ref[...].astype(BF16)
        yv = y_ref[...]
        dy = _dot_nt(dhb, wo_ref[...])
        dwo_ref[...] += _dot_tn(yv, dhb)

        z, rstd = _layer_norm_stats(yc_ref[...])
        ln = z * lg_ref[...] + lb_ref[...]
        sg = _sigmoid(ln)
        dln = dy[:, 0:CONV_CH] * (sg * (1.0 + ln * (1.0 - sg)))
        dlg_ref[...] += _colsum(dln * z)
        dlb_ref[...] += _colsum(dln)
        dz = dln * lg_ref[...]
        dyv_ref[:, 0:CONV_CH] = rstd * (dz - jnp.mean(dz, axis=-1, keepdims=True)
                                        - z * jnp.mean(dz * z, axis=-1, keepdims=True))
        dyv_ref[:, CONV_CH:] = dy[:, CONV_CH:]

        lane = _lane(tm)
        delta = jnp.zeros((tm, SLOT), F32)
        for hh in range(NH):
            sl = slice(YS + SLOT * hh, YS + SLOT * (hh + 1))
            d = jnp.sum(dy[:, sl] * yv[:, sl].astype(F32), axis=-1, keepdims=True)
            delta = jnp.where(lane == hh, d, delta)
        del_ref[...] = delta

    cur = lambda w: pl.BlockSpec((tm, w), lambda i: (i, 0))
    full = lambda a: pl.BlockSpec(a.shape, lambda i: (0,) * a.ndim)
    sd = jax.ShapeDtypeStruct
    return pl.pallas_call(
        body, name="outproj_bwd", grid=(S // tm,),
        in_specs=[cur(D), cur(YP), cur(CONV_CH), full(ln_g), full(ln_b), full(w_outp)],
        out_specs=[cur(YP), cur(SLOT), pl.BlockSpec((YP, D), lambda i: (0, 0)),
                   pl.BlockSpec((1, CONV_CH), lambda i: (0, 0)), pl.BlockSpec((1, CONV_CH), lambda i: (0, 0))],
        out_shape=[sd((S, YP), F32), sd((S, SLOT), F32), sd((YP, D), F32), sd((1, CONV_CH), F32),
                   sd((1, CONV_CH), F32)],
        compiler_params=_cp(),
    )(dh2, y, yc, ln_g, ln_b, w_outp)


def _mixer_bwd(p, dyv, lse, delta, cosT, sinT, conv_w, gq, gk, sinks, mk, mv, gqm, tm, rider=None):
    S = p.shape[0]
    M = mk.shape[0]
    nb = tm // BLK
    nt = S // tm
    nblocks = S // BLK
    W = SLOT * N_MEMH

    def body(p_ref, pp_ref, pn_ref, dy_ref, dyn_ref, lse_ref, lsen_ref, del_ref, deln_ref,
             cos_ref, cosp_ref, cosn_ref, sin_ref, sinp_ref, sinn_ref,
             cw_ref, gq_ref, gk_ref, sink_ref, mk_ref, mv_ref, gqm_ref,
             dp_ref, dcw_ref, dcb_ref, dgq_ref, dgk_ref, dgqm_ref, dsink_ref, dmk_ref, dmv_ref, ext, ext2):
        i = pl.program_id(0)

        @pl.when(i == 0)
        def _():
            for r in (dcw_ref, dcb_ref, dgq_ref, dgk_ref, dgqm_ref, dsink_ref, dmk_ref, dmv_ref):
                r[...] = jnp.zeros_like(r)

        not_first = (i > 0).astype(F32)
        not_last = (i < nt - 1).astype(F32)
        lane = _lane(tm)
        lane_e = _lane(tm + BLK)

        u = p_ref[:, 0:2 * CONV_CH]
        a_, sg = u[:, :CONV_CH], _sigmoid(u[:, CONV_CH:])
        ext[0:BLK, :] = _glu(pp_ref[:, 0:2 * CONV_CH]) * not_first
        ext[BLK:BLK + tm, :] = a_ * sg
        dyc = dy_ref[:, 0:CONV_CH]
        ext2[0:tm, :] = dyc
        ext2[tm:tm + BLK, :] = dyn_ref[:, 0:CONV_CH] * not_last
        dcb_ref[...] += _colsum(dyc)
        dyg = jnp.zeros((tm, CONV_CH), F32)
        for k in range(CONV_WIDTH):
            off = BLK - (CONV_WIDTH - 1) + k
            dcw_ref[k:k + 1, :] += _colsum(dyc * ext[off:off + tm, :])
            fo = CONV_WIDTH - 1 - k
            dyg = dyg + cw_ref[k:k + 1, :] * ext2[fo:fo + tm, :]
        dp_ref[:, 0:CONV_CH] = dyg * sg
        dp_ref[:, CONV_CH:2 * CONV_CH] = dyg * a_ * sg * (1.0 - sg)

        cos_k = jnp.concatenate([cosp_ref[...], cos_ref[...]], axis=0)
        sin_k = jnp.concatenate([sinp_ref[...], sin_ref[...]], axis=0)
        cos_q = jnp.concatenate([cos_ref[...], cosn_ref[...]], axis=0)
        sin_q = jnp.concatenate([sin_ref[...], sinn_ref[...]], axis=0)
        lse_e = jnp.concatenate([lse_ref[...], lsen_ref[...]], axis=0)
        del_e = jnp.concatenate([del_ref[...], deln_ref[...]], axis=0)
        qi = lax.broadcasted_iota(jnp.int32, (BLK, BLK), 0)
        kj = lax.broadcasted_iota(jnp.int32, (BLK, BLK), 1)
        diag = kj <= qi
        offd = kj > qi
        dgq = jnp.zeros((1, SLOT), F32)
        dgk = jnp.zeros((1, SLOT), F32)
        dsink = jnp.zeros((1, SLOT), F32)
        lane1 = lax.broadcasted_iota(jnp.int32, (1, SLOT), 1)
        for kvh in range(N_KV):
            ks = slice(KO + SLOT * kvh, KO + SLOT * (kvh + 1))
            vs = slice(VO + SLOT * kvh, VO + SLOT * (kvh + 1))
            k_raw = jnp.concatenate([pp_ref[:, ks], p_ref[:, ks]], axis=0)
            k_e = _head_fwd(k_raw, gk_ref[...], cos_k, sin_k, lane_e).astype(BF16)
            v_e = jnp.concatenate([pp_ref[:, vs], p_ref[:, vs]], axis=0).astype(BF16)
            dk = [jnp.zeros((BLK, SLOT), F32) for _ in range(nb)]
            dv = [jnp.zeros((BLK, SLOT), F32) for _ in range(nb)]
            for gi in range(GROUP):
                hq = GROUP * kvh + gi
                qs = slice(QO + SLOT * hq, QO + SLOT * (hq + 1))
                os_ = slice(YS + SLOT * hq, YS + SLOT * (hq + 1))
                q_raw = jnp.concatenate([p_ref[:, qs], pn_ref[:, qs]], axis=0)
                q_e = _head_fwd(q_raw, gq_ref[...], cos_q, sin_q, lane_e).astype(BF16)
                do_e = jnp.concatenate([dy_ref[:, os_], dyn_ref[:, os_]], axis=0).astype(BF16)
                lse_h = _col(lse_e, hq, lane_e)
                del_h = _col(del_e, hq, lane_e)
                sink = sink_ref[hq]
                dq = [jnp.zeros((BLK, SLOT), F32) for _ in range(nb)]
                for m in range(nb + 1):
                    rows = slice(BLK * m, BLK * (m + 1))
                    qb, dob, lb, db_ = q_e[rows], do_e[rows], lse_h[rows], del_h[rows]
                    for n in (m - 1, m):
                        if n == nb:
                            continue
                        krows = slice(BLK * (n + 1), BLK * (n + 2))
                        kb, vb = k_e[krows], v_e[krows]
                        s = _dot_nt(qb, kb) * SCALE
                        mask = diag if n == m else offd
                        if n == -1:
                            mask = mask & (i > 0)
                        if m == nb:
                            mask = mask & (i < nt - 1)
                        prob = jnp.where(mask, jnp.exp(jnp.where(mask, s - lb, NEG)), 0.0)
                        dpb = _dot_nt(dob, vb)
                        ds = (prob * (dpb - db_) * SCALE).astype(BF16)
                        if m < nb:
                            dq[m] = dq[m] + _dot_nn(ds, kb)
                        if n >= 0:
                            dk[n] = dk[n] + _dot_tn(ds, qb)
                            dv[n] = dv[n] + _dot_tn(prob.astype(BF16), dob)
                dqr, prod = _head_bwd(jnp.concatenate(dq, axis=0), p_ref[:, qs], gq_ref[...],
                                      cos_ref[...], sin_ref[...], lane)
                dp_ref[:, qs] = dqr
                dgq = dgq + _colsum(prod)
                psink = jnp.exp(sink - lse_h[0:tm])
                dsink = dsink + jnp.where(lane1 == hq, -_colsum(psink * del_h[0:tm]), 0.0)
            dkr, prod = _head_bwd(jnp.concatenate(dk, axis=0), p_ref[:, ks], gk_ref[...],
                                  cos_ref[...], sin_ref[...], lane)
            dp_ref[:, ks] = dkr
            dgk = dgk + _colsum(prod)
            dp_ref[:, vs] = jnp.concatenate(dv, axis=0)
        dgq_ref[...] += dgq
        dgk_ref[...] += dgk
        dsink_ref[...] += dsink

        dgqm = jnp.zeros((1, SLOT), F32)
        for hm in range(N_MEMH):
            ms = slice(SLOT * hm, SLOT * (hm + 1))
            qs = slice(MO + SLOT * hm, MO + SLOT * (hm + 1))
            os_ = slice(YM + SLOT * hm, YM + SLOT * (hm + 1))
            qraw = p_ref[:, qs]
            qm = _head_fwd(qraw, gqm_ref[...], None, None, None).astype(BF16)
            kb, vb = mk_ref[:, ms], mv_ref[:, ms]
            dob = dy_ref[:, os_].astype(BF16)
            s = _dot_nt(qm, kb) * SCALE
            prob = jnp.exp(s - _col(lse_ref[...], N_Q + hm, lane))
            dpb = _dot_nt(dob, vb)
            ds = (prob * (dpb - _col(del_ref[...], N_Q + hm, lane)) * SCALE).astype(BF16)
            dqr, prod = _head_bwd(_dot_nn(ds, kb), qraw, gqm_ref[...], None, None, None)
            dp_ref[:, qs] = dqr
            dgqm = dgqm + _colsum(prod)
            dmk_ref[:, ms] += _dot_tn(ds, qm)
            dmv_ref[:, ms] += _dot_tn(prob.astype(BF16), dob)
        dgqm_ref[...] += dgqm

    cur = lambda w: pl.BlockSpec((tm, w), lambda i: (i, 0))
    prev = lambda w: pl.BlockSpec((BLK, w), lambda i: (jnp.maximum(i * nb - 1, 0), 0))
    nxt = lambda w: pl.BlockSpec((BLK, w), lambda i: (jnp.minimum((i + 1) * nb, nblocks - 1), 0))
    full = lambda a: pl.BlockSpec(a.shape, lambda i: (0,) * a.ndim)
    acc = lambda r, w: pl.BlockSpec((r, w), lambda i: (0, 0))
    sd = jax.ShapeDtypeStruct
    return _call(
        body, name="mixer_bwd", grid=(nt,),
        in_specs=[cur(PP), prev(PP), nxt(PP), cur(YP), nxt(YP), cur(SLOT), nxt(SLOT), cur(SLOT), nxt(SLOT),
                  cur(SLOT), prev(SLOT), nxt(SLOT), cur(SLOT), prev(SLOT), nxt(SLOT),
                  full(conv_w), full(gq), full(gk), pl.BlockSpec(memory_space=pltpu.SMEM),
                  full(mk), full(mv), full(gqm)],
        out_specs=[cur(PP), acc(32, CONV_CH), acc(1, CONV_CH), acc(1, SLOT), acc(1, SLOT), acc(1, SLOT),
                   acc(1, SLOT), acc(M, W), acc(M, W)],
        out_shape=[sd((S, PP), F32), sd((32, CONV_CH), F32), sd((1, CONV_CH), F32), sd((1, SLOT), F32),
                   sd((1, SLOT), F32), sd((1, SLOT), F32), sd((1, SLOT), F32), sd((M, W), F32), sd((M, W), F32)],
        args=[p, p, p, dyv, dyv, lse, lse, delta, delta, cosT, cosT, cosT, sinT, sinT, sinT,
              conv_w, gq, gk, sinks, mk, mv, gqm],
        scratch=[pltpu.VMEM((tm + BLK, CONV_CH), F32), pltpu.VMEM((tm + BLK, CONV_CH), F32)], rider=rider)


def _proj_bwd(dp, h, dh2, g, n, w_inp, tm):
    S, D = h.shape

    def body(dp_ref, h_ref, dh2_ref, g_ref, n_ref, w_ref, dh_ref, dg_ref, dw_ref):
        i = pl.program_id(0)

        @pl.when(i == 0)
        def _():
            dg_ref[...] = jnp.zeros_like(dg_ref)
            dw_ref[...] = jnp.zeros_like(dw_ref)

        dpb = dp_ref[...].astype(BF16)
        dn = _dot_nn(dpb, w_ref[...])
        dw_ref[...] += _dot_tn(dpb, n_ref[...])
        hv = h_ref[...]
        dx, dgrow = _rms_bwd(dn, hv, _rms(hv), g_ref[...])
        dh_ref[...] = dh2_ref[...] + dx
        dg_ref[...] += _colsum(dgrow)

    cur = lambda w: pl.BlockSpec((tm, w), lambda i: (i, 0))
    sd = jax.ShapeDtypeStruct
    return pl.pallas_call(
        body, name="proj_bwd", grid=(S // tm,),
        in_specs=[cur(PP), cur(D), cur(D), pl.BlockSpec((1, D), lambda i: (0, 0)), cur(D),
                  pl.BlockSpec((PP, D), lambda i: (0, 0))],
        out_specs=[cur(D), pl.BlockSpec((1, D), lambda i: (0, 0)), pl.BlockSpec((PP, D), lambda i: (0, 0))],
        out_shape=[sd((S, D), F32), sd((1, D), F32), sd((PP, D), F32)],
        compiler_params=_cp(),
    )(dp, h, dh2, g, n, w_inp)


def _norm_bwd(dxn, h, g, tm):
    S, D = h.shape

    def body(d_ref, h_ref, g_ref, dh_ref, dg_ref):
        @pl.when(pl.program_id(0) == 0)
        def _():
            dg_ref[...] = jnp.zeros_like(dg_ref)

        hv = h_ref[...]
        dx, dgrow = _rms_bwd(d_ref[...], hv, _rms(hv), g_ref[...])
        dh_ref[...] = dx
        dg_ref[...] += _colsum(dgrow)

    cur = pl.BlockSpec((tm, D), lambda i: (i, 0))
    vec = pl.BlockSpec((1, D), lambda i: (0, 0))
    return pl.pallas_call(
        body, name="norm_bwd", grid=(S // tm,), in_specs=[cur, cur, vec], out_specs=[cur, vec],
        out_shape=[jax.ShapeDtypeStruct((S, D), F32), jax.ShapeDtypeStruct((1, D), F32)],
        compiler_params=_cp(),
    )(dxn, h, g)


def _loss_bwd(xn, h, g, target, tm):
    S, D = h.shape

    def body(y_ref, h_ref, g_ref, t_ref, loss_ref, dh_ref, dg_ref):
        @pl.when(pl.program_id(0) == 0)
        def _():
            dg_ref[...] = jnp.zeros_like(dg_ref)
            loss_ref[...] = jnp.zeros_like(loss_ref)

        err = y_ref[...] - t_ref[...]
        part = jnp.sum(jnp.mean(err * err, axis=-1, keepdims=True), axis=0, keepdims=True)
        loss_ref[...] += 0.5 * part
        hv = h_ref[...]
        dx, dgrow = _rms_bwd(err * (1.0 / D), hv, _rms(hv), g_ref[...])
        dh_ref[...] = dx
        dg_ref[...] += _colsum(dgrow)

    cur = pl.BlockSpec((tm, D), lambda i: (i, 0))
    vec = pl.BlockSpec((1, D), lambda i: (0, 0))
    return pl.pallas_call(
        body, name="loss_bwd", grid=(S // tm,), in_specs=[cur, cur, vec, cur],
        out_specs=[pl.BlockSpec((1, SLOT), lambda i: (0, 0)), cur, vec],
        out_shape=[jax.ShapeDtypeStruct((1, SLOT), F32), jax.ShapeDtypeStruct((S, D), F32),
                   jax.ShapeDtypeStruct((1, D), F32)],
        compiler_params=_cp(),
    )(xn, h, g, target)


def _swap_with_sibling(bufs):
    nbuf = len(bufs)

    def body(*refs):
        ins, outs = refs[:nbuf], refs[nbuf:2 * nbuf]
        ssem, rsem = refs[2 * nbuf:]
        x, y, c, _ = _place()
        sends = [pltpu.make_async_remote_copy(src_ref=ins[b], dst_ref=outs[b], send_sem=ssem.at[b],
                                              recv_sem=rsem.at[b], device_id=(x, y, 1 - c), device_id_type=MESH)
                 for b in range(nbuf)]
        for cp in sends:
            cp.start()
        for cp in sends:
            cp.wait_recv()
        for cp in sends:
            cp.wait_send()

    hbm = pl.BlockSpec(memory_space=pl.ANY)
    return pl.pallas_call(
        body, name="swap_with_sibling",
        in_specs=[hbm] * nbuf, out_specs=[hbm] * nbuf,
        out_shape=[jax.ShapeDtypeStruct(b.shape, b.dtype) for b in bufs],
        scratch_shapes=[pltpu.SemaphoreType.DMA((nbuf,)), pltpu.SemaphoreType.DMA((nbuf,))],
    )(*bufs)


def _all_gather_small(buf):
    _, R, W = buf.shape

    def body(in_ref, out_ref, ssem, rsem, lsem):
        x, y, c, _ = _place()
        me = 4 * x + 2 * y + c
        local = pltpu.make_async_copy(in_ref, out_ref.at[pl.ds(me, 1)], lsem)
        local.start()

        def copy(k, block):
            fx, fy, fc = (k >> 2) & 1, (k >> 1) & 1, k & 1
            peer = (x ^ fx, y ^ fy, c ^ fc)
            return pltpu.make_async_remote_copy(
                src_ref=in_ref, dst_ref=out_ref.at[pl.ds(block, 1)], send_sem=ssem.at[k - 1],
                recv_sem=rsem.at[k - 1], device_id=peer, device_id_type=MESH)

        sends = [copy(k, me) for k in range(1, 8)]
        for cp in sends:
            cp.start()
        for k in range(1, 8):
            copy(k, me ^ k).wait_recv()
        for cp in sends:
            cp.wait_send()
        local.wait()

    hbm = pl.BlockSpec(memory_space=pl.ANY)
    return pl.pallas_call(
        body, name="all_gather_small", in_specs=[hbm], out_specs=hbm,
        out_shape=jax.ShapeDtypeStruct((8, R, W), buf.dtype),
        scratch_shapes=[pltpu.SemaphoreType.DMA((7,)), pltpu.SemaphoreType.DMA((7,)), pltpu.SemaphoreType.DMA],
    )(buf)


def _row_tile(n, cap=1024):
    for t in range(min(n, cap) // 8 * 8, 7, -8):
        if n % t == 0:
            return t
    return n


def _sum4(own, recv):
    n, rows, D = own.shape
    tr = _row_tile(rows)

    def body(o_ref, r0_ref, r1_ref, r2_ref, out_ref):
        out_ref[...] = ((o_ref[...].astype(F32) + r0_ref[...].astype(F32)) + r1_ref[...].astype(F32)) \
            + r2_ref[...].astype(F32)

    def rspec(p):
        return pl.BlockSpec((None, None, None, tr, D), lambda k, i, p=p: (p, k, 0, i, 0))

    blk = pl.BlockSpec((None, tr, D), lambda k, i: (k, i, 0))
    return pl.pallas_call(
        body, name="sum4", grid=(n, rows // tr),
        in_specs=[blk, rspec(0), rspec(1), rspec(2)], out_specs=blk,
        out_shape=jax.ShapeDtypeStruct((n, rows, D), F32),
    )(own, recv, recv, recv)


def _add2(a, b):
    rows, D = a.shape
    tr = _row_tile(rows)

    def body(a_ref, b_ref, o_ref):
        o_ref[...] = a_ref[...] + b_ref[...]

    blk = pl.BlockSpec((tr, D), lambda i: (i, 0))
    return pl.pallas_call(body, name="add2", grid=(rows // tr,), in_specs=[blk, blk], out_specs=blk,
                          out_shape=jax.ShapeDtypeStruct((rows, D), F32))(a, b)


def _adam_math(w, g, m, v):
    m = ADAM_B1 * m + (1.0 - ADAM_B1) * g
    v = ADAM_B2 * v + (1.0 - ADAM_B2) * (g * g)
    m_hat = m / (1.0 - ADAM_B1 ** ADAM_STEP)
    v_hat = v / (1.0 - ADAM_B2 ** ADAM_STEP)
    delta = -ADAM_LR * (m_hat / (jnp.sqrt(v_hat) + ADAM_EPS) + ADAM_WD * w)
    return delta, m, v


def _adam(w, g, m, v):
    rows, cols = w.shape
    tr = _row_tile(rows, 512)

    def body(w_ref, g_ref, m_ref, v_ref, d_ref, nm_ref, nv_ref):
        d, m_, v_ = _adam_math(w_ref[...], g_ref[...], m_ref[...], v_ref[...])
        d_ref[...] = d
        nm_ref[...] = m_
        nv_ref[...] = v_

    blk = pl.BlockSpec((tr, cols), lambda i: (i, 0))
    return pl.pallas_call(body, name="adam", grid=(rows // tr,), in_specs=[blk] * 4, out_specs=[blk] * 3,
                          out_shape=[jax.ShapeDtypeStruct((rows, cols), F32)] * 3)(w, g, m, v)


def _small_sum_adam(g8, w, m, v):
    _, R, W = g8.shape

    def body(g_ref, w_ref, m_ref, v_ref, go_ref, d_ref, nm_ref, nv_ref):
        g = g_ref[0]
        for k in range(1, 8):
            g = g + g_ref[k]
        go_ref[...] = g
        d, m_, v_ = _adam_math(w_ref[...], g, m_ref[...], v_ref[...])
        d_ref[...] = d
        nm_ref[...] = m_
        nv_ref[...] = v_

    return pl.pallas_call(body, name="small_sum_adam",
                          out_shape=[jax.ShapeDtypeStruct((R, W), F32)] * 4)(g8, w, m, v)


def _pad_heads_rows(w, first):
    lead, D = w.shape[:-2], w.shape[-1]
    heads = w[..., first:, :]
    n = heads.shape[-2] // HEAD_DIM
    heads = heads.reshape(lead + (n, HEAD_DIM, D))
    heads = jnp.pad(heads, [(0, 0)] * (len(lead) + 1) + [(0, SLOT - HEAD_DIM), (0, 0)])
    return jnp.concatenate([w[..., :first, :], heads.reshape(lead + (n * SLOT, D))], axis=-2)


def _unpad_heads_rows(w, first):
    lead, D = w.shape[:-2], w.shape[-1]
    heads = w[..., first:, :]
    n = heads.shape[-2] // SLOT
    heads = heads.reshape(lead + (n, SLOT, D))[..., :HEAD_DIM, :]
    return jnp.concatenate([w[..., :first, :], heads.reshape(lead + (n * HEAD_DIM, D))], axis=-2)


def _pad_vec(v):
    return jnp.pad(v, (0, SLOT - v.shape[0]))[None, :]


class _Pack:
    def __init__(self, shapes):
        self.shapes = shapes
        self.sizes = [int(functools.reduce(lambda a, b: a * b, s, 1)) for s in shapes]
        total = sum(self.sizes)
        self.rows = -(-total // (8 * SLOT)) * 8
        self.pad = self.rows * SLOT - total

    def pack(self, arrs):
        flat = jnp.concatenate([a.reshape(-1).astype(F32) for a in arrs] + [jnp.zeros((self.pad,), F32)])
        return flat.reshape(self.rows, SLOT)

    def unpack(self, buf):
        flat, out, o = buf.reshape(-1), [], 0
        for s, n in zip(self.shapes, self.sizes):
            out.append(flat[o:o + n].reshape(s))
            o += n
        return out


def kernel(x, mem, positions, ffn1_norm, ffn1_w1, ffn1_w3, ffn1_w2, mix_norm, w_in, conv_w, conv_b, conv_ln_g, conv_ln_b, swa_q_norm, swa_k_norm, swa_sinks, mem_norm, w_mem_kv, mem_q_norm, mem_k_norm, w_out, ffn2_norm, ffn2_w1, ffn2_w3, ffn2_w2, final_norm, loss_target, m_ffn1_norm, m_ffn1_w1, m_ffn1_w3, m_ffn1_w2, m_mix_norm, m_w_in, m_conv_w, m_conv_b, m_conv_ln_g, m_conv_ln_b, m_swa_q_norm, m_swa_k_norm, m_swa_sinks, m_mem_norm, m_w_mem_kv, m_mem_q_norm, m_mem_k_norm, m_w_out, m_ffn2_norm, m_ffn2_w1, m_ffn2_w3, m_ffn2_w2, m_final_norm, v_ffn1_norm, v_ffn1_w1, v_ffn1_w3, v_ffn1_w2, v_mix_norm, v_w_in, v_conv_w, v_conv_b, v_conv_ln_g, v_conv_ln_b, v_swa_q_norm, v_swa_k_norm, v_swa_sinks, v_mem_norm, v_w_mem_kv, v_mem_q_norm, v_mem_k_norm, v_w_out, v_ffn2_norm, v_ffn2_w1, v_ffn2_w3, v_ffn2_w2, v_final_norm):
    names = ['ffn1_norm', 'ffn1_w1', 'ffn1_w3', 'ffn1_w2', 'mix_norm', 'w_in', 'conv_w', 'conv_b', 'conv_ln_g',
             'conv_ln_b', 'swa_q_norm', 'swa_k_norm', 'swa_sinks', 'mem_norm', 'w_mem_kv', 'mem_q_norm',
             'mem_k_norm', 'w_out', 'ffn2_norm', 'ffn2_w1', 'ffn2_w3', 'ffn2_w2', 'final_norm']
    loc = locals()
    W = {n: loc[n] for n in names}
    M1 = {n: loc['m_' + n] for n in names}
    V1 = {n: loc['v_' + n] for n in names}

    S, D = x.shape[1], x.shape[2]
    L = ffn1_norm.shape[0]
    Fs = ffn1_w1.shape[2]
    F = 4 * Fs
    Mlen = mem.shape[1]
    cw_sh = conv_w.shape[2]
    tm = 512 if S >= 2048 else 256
    tf = 1408 if F % 1408 == 0 else 256
    tmb = 256
    tfw = 256
    tmw = 1024 if S >= 2048 else 256
    x0 = x[0]
    mem0 = mem[0]
    target = loss_target[0]
    my_chip = 2 * lax.axis_index("x") + lax.axis_index("y")

    mkv_rows = w_mem_kv.shape[1] * MEM_KV // D
    r_in, r_out = D_IN // 4, D_MIX // 4
    rm = r_in + r_out + mkv_rows

    def ffn_group(w1, w3, w2):
        return jnp.stack([w1.T, w3.T, w2]).astype(BF16).reshape(3, 1, Fs, D)

    groups = []
    for l in range(L):
        groups.append(ffn_group(ffn1_w1[l], ffn1_w3[l], ffn1_w2[l]))
        groups.append(jnp.concatenate([w_in[l].T, w_out[l], w_mem_kv[l].reshape(mkv_rows, D)])
                      .astype(BF16).reshape(1, 1, rm, D))
        groups.append(ffn_group(ffn2_w1[l], ffn2_w3[l], ffn2_w2[l]))
    gathered = [None] * len(groups)
    cw_rows = -(-(L * CONV_WIDTH) // 8) * 8
    cw_pad = jnp.pad(conv_w.reshape(L * CONV_WIDTH, cw_sh), ((0, cw_rows - L * CONV_WIDTH), (0, SLOT - cw_sh)))
    gathered[0], gathered[1], cw_g = _run_rider(
        _Gather([groups[0], groups[1], cw_pad.reshape(1, 1, cw_rows, SLOT)]), "all_gather_first")
    conv_wF = cw_g[0, :, :L * CONV_WIDTH, :cw_sh].reshape(4, L, CONV_WIDTH, cw_sh)
    conv_wF = jnp.moveaxis(conv_wF, 0, 2).reshape(L, CONV_WIDTH, 4 * cw_sh)
    conv_wP = jnp.pad(conv_wF, ((0, 0), (0, 32 - CONV_WIDTH), (0, 0)))

    def gather_rider(j):
        return _Gather([groups[j + 2]]) if j + 2 < len(groups) else None

    def mix_weights(l):
        g = gathered[3 * l + 1][0]
        w_inp = _pad_heads_rows(g[:, :r_in].reshape(D_IN, D), 2 * CONV_CH)
        w_outp = _pad_heads_rows(g[:, r_in:r_in + r_out].reshape(D_MIX, D), CONV_CH)
        w_mkvp = jnp.pad(g[:, r_in + r_out:].reshape(D, 2 * N_MEMH, HEAD_DIM),
                         ((0, 0), (0, 0), (0, SLOT - HEAD_DIM))).reshape(D, 2 * N_MEMH * SLOT)
        return w_inp, w_outp, w_mkvp

    inv_freq = ROPE_THETA ** (-jnp.arange(0, HEAD_DIM, 2, dtype=F32) / HEAD_DIM)
    invf = jnp.concatenate([inv_freq, inv_freq, jnp.zeros((SLOT - HEAD_DIM,), F32)])[None, :]
    cosT, sinT = _rope_tables(positions.reshape(S, 1), invf, tm)

    row = lambda a, l: a[l][None, :]
    sinks_p = jnp.pad(swa_sinks, ((0, 0), (0, 8 - N_Q)))

    saved = []
    xin = x0
    xn = None
    for l in range(L):
        wf1 = gathered[3 * l].reshape(3, F, D)
        (h1, a1, b1), got = _ffn_fwd(xin, row(ffn1_norm, l), wf1, None, tm, tf, rider=gather_rider(3 * l))
        if got:
            gathered[3 * l + 2] = got[0]
        w_inp, w_outp, w_mkvp = mix_weights(l)
        p, n2 = _proj_fwd(h1, row(mix_norm, l), w_inp, tm)
        gk_m = _pad_vec(mem_k_norm[l])
        nm, mraw, mk, mv = _mem_kv_fwd(mem0, row(mem_norm, l), w_mkvp, gk_m)
        gq, gk, gqm = _pad_vec(swa_q_norm[l]), _pad_vec(swa_k_norm[l]), _pad_vec(mem_q_norm[l])
        (h2, y, yc, lse), got = _mixer_fwd(p, h1, cosT, sinT, conv_wP[l], row(conv_b, l), row(conv_ln_g, l),
                                           row(conv_ln_b, l), gq, gk, sinks_p[l], mk, mv, gqm, w_outp, tm,
                                           rider=gather_rider(3 * l + 1))
        if got:
            gathered[3 * l + 3] = got[0]
        wf2 = gathered[3 * l + 2].reshape(3, F, D)
        (h3, a2, b2, xn), got = _ffn_fwd(h2, row(ffn2_norm, l), wf2, row(final_norm, l), tm, tf,
                                         rider=gather_rider(3 * l + 2))
        if got:
            gathered[3 * l + 4] = got[0]
        saved.append(dict(xin=xin, h1=h1, a1=a1, b1=b1, p=p, n2=n2, nm=nm, mraw=mraw, mk=mk, mv=mv, gk_m=gk_m,
                          gq=gq, gk=gk, gqm=gqm, h2=h2, y=y, yc=yc, lse=lse, h3=h3, a2=a2, b2=b2,
                          wf1=wf1, wf2=wf2, w_inp=w_inp, w_outp=w_outp, w_mkvp=w_mkvp))
        xin = xn

    G = {n: [None] * L for n in names}
    ffn_bufs = [None] * (2 * L)
    mix_bufs = [None] * L
    ffn_recv = [None] * (2 * L)
    mix_recv = [None] * L
    dxn = None
    loss_part = None
    for l in reversed(range(L)):
        sv = saved[l]
        if l == L - 1:
            loss_part, dh3, G['final_norm'][l] = _loss_bwd(xn, sv['h3'], row(final_norm, l), target, tm)
        else:
            dh3, G['final_norm'][l] = _norm_bwd(dxn, sv['h3'], row(final_norm, l), tm)
        rider = _Scatter([ffn_bufs[2 * l + 2]]) if l < L - 1 else None
        (dh2, G['ffn2_norm'][l], da, db, t, n, dy), got = _ffn_bwd_act(
            dh3, sv['h2'], row(ffn2_norm, l), sv['a2'], sv['b2'], sv['wf2'], tmb, tf, rider=rider)
        if got:
            ffn_recv[2 * l + 2] = got[0]
        ffn_bufs[2 * l + 1] = _ffn_bwd_w(da, db, t, n, dy, tmw, tfw).reshape(3, 4, Fs, D)
        dyv, delta, dwo, G['conv_ln_g'][l], G['conv_ln_b'][l] = _outproj_bwd(
            dh2, sv['y'], sv['yc'], row(conv_ln_g, l), row(conv_ln_b, l), sv['w_outp'], tm)
        (dp, dcw, G['conv_b'][l], dgq, dgk, dgqm, dsink, dmk, dmv), got = _mixer_bwd(
            sv['p'], dyv, sv['lse'], delta, cosT, sinT, conv_wP[l], sv['gq'], sv['gk'], sinks_p[l],
            sv['mk'], sv['mv'], sv['gqm'], tm, rider=_Scatter([ffn_bufs[2 * l + 1]]))
        ffn_recv[2 * l + 1] = got[0]
        dwm, G['mem_norm'][l], dgk_m = _mem_kv_bwd(dmk, dmv, sv['mraw'], sv['nm'], mem0, row(mem_norm, l),
                                                   sv['w_mkvp'], sv['gk_m'])
        dh1, G['mix_norm'][l], dwi = _proj_bwd(dp, sv['h1'], dh2, row(mix_norm, l), sv['n2'], sv['w_inp'], tm)
        dwiT = _unpad_heads_rows(dwi, 2 * CONV_CH).reshape(4, r_in, D)
        dwoF = _unpad_heads_rows(dwo, CONV_CH).reshape(4, r_out, D)
        dwmF = dwm.reshape(D, 2 * N_MEMH, SLOT)[:, :, :HEAD_DIM].reshape(4, mkv_rows, D)
        mix_bufs[l] = jnp.concatenate([dwiT, dwoF, dwmF], axis=1).astype(BF16).reshape(1, 4, rm, D)
        (dxl, G['ffn1_norm'][l], da, db, t, n, dy), got = _ffn_bwd_act(
            dh1, sv['xin'], row(ffn1_norm, l), sv['a1'], sv['b1'], sv['wf1'], tmb, tf,
            rider=_Scatter([mix_bufs[l]]))
        mix_recv[l] = got[0]
        ffn_bufs[2 * l] = _ffn_bwd_w(da, db, t, n, dy, tmw, tfw).reshape(3, 4, Fs, D)
        dxn = dxl
        G['conv_w'][l] = dcw[:CONV_WIDTH]
        G['swa_q_norm'][l] = dgq[0, :HEAD_DIM]
        G['swa_k_norm'][l] = dgk[0, :HEAD_DIM]
        G['mem_q_norm'][l] = dgqm[0, :HEAD_DIM]
        G['mem_k_norm'][l] = dgk_m[0, :HEAD_DIM]
        G['swa_sinks'][l] = dsink[0, :N_Q]
    ffn_recv[0] = _run_rider(_Scatter([ffn_bufs[0]]), "scatter_last")[0]
    grad_x = dxn[None]
    loss = lax.psum(loss_part[0, 0], AXES)

    parts = []
    for b, r in zip(ffn_bufs + mix_bufs, ffn_recv + mix_recv):
        own = lax.dynamic_index_in_dim(b, my_chip, axis=1, keepdims=False)
        parts.append(_sum4(own, r))
    theirs = _swap_with_sibling(parts)
    gsum = [_add2(a.reshape(-1, D), b.reshape(-1, D)).reshape(a.shape) for a, b in zip(parts, theirs)]
    for l in range(L):
        for f, pre in enumerate(('ffn1', 'ffn2')):
            g3 = gsum[2 * l + f]
            G[pre + '_w1'][l] = g3[0].T
            G[pre + '_w3'][l] = g3[1].T
            G[pre + '_w2'][l] = g3[2]
        gm = gsum[2 * L + l][0]
        G['w_in'][l] = gm[:r_in].T
        G['w_out'][l] = gm[r_in:r_in + r_out]
        G['w_mem_kv'][l] = gm[r_in + r_out:].reshape(w_mem_kv.shape[1], MEM_KV)

    small = ['ffn1_norm', 'mix_norm', 'conv_b', 'conv_ln_g', 'conv_ln_b', 'swa_q_norm', 'swa_k_norm', 'swa_sinks',
             'mem_norm', 'mem_q_norm', 'mem_k_norm', 'ffn2_norm', 'final_norm']
    gsmall = [jnp.stack([G[n][l].reshape(-1) for l in range(L)]) for n in small]
    gcw = jnp.stack(G['conv_w'])
    cw_cols = 4 * cw_sh
    full_of = lambda a: lax.dynamic_update_slice(jnp.zeros((L, CONV_WIDTH, cw_cols), F32), a, (0, 0, my_chip * cw_sh))
    pk = _Pack([W[n].shape for n in small] + [(L, CONV_WIDTH, cw_cols)])
    g8 = _all_gather_small(pk.pack(gsmall + [gcw])[None])
    outs4 = _small_sum_adam(g8, pk.pack([W[n] for n in small] + [full_of(conv_w)]),
                            pk.pack([M1[n] for n in small] + [full_of(m_conv_w)]),
                            pk.pack([V1[n] for n in small] + [full_of(v_conv_w)]))
    un = [pk.unpack(o) for o in outs4]
    grads, deltas, new_m, new_v = {}, {}, {}, {}
    for k, n in enumerate(small):
        grads[n], deltas[n], new_m[n], new_v[n] = un[0][k], un[1][k], un[2][k], un[3][k]
    mine = lambda a: lax.dynamic_slice(a, (0, 0, my_chip * cw_sh), (L, CONV_WIDTH, cw_sh))
    grads['conv_w'], deltas['conv_w'], new_m['conv_w'], new_v['conv_w'] = [mine(u[-1]) for u in un]

    for n in ('ffn1_w1', 'ffn1_w3', 'ffn1_w2', 'w_in', 'w_mem_kv', 'w_out', 'ffn2_w1', 'ffn2_w3', 'ffn2_w2'):
        g = jnp.stack(G[n])
        shp = W[n].shape
        v2 = lambda a: a.reshape(-1, shp[-1])
        d_, m_, v_ = _adam(v2(W[n]), v2(g), v2(M1[n]), v2(V1[n]))
        grads[n], deltas[n], new_m[n], new_v[n] = g, d_.reshape(shp), m_.reshape(shp), v_.reshape(shp)

    return (loss, grad_x, *[grads[n] for n in names], *[deltas[n] for n in names],
            *[new_m[n] for n in names], *[new_v[n] for n in names])
```

```python
import functools

import jax
import jax.numpy as jnp
from jax import lax
from jax.experimental import pallas as pl
from jax.experimental.pallas import tpu as pltpu

F32 = jnp.float32
BF16 = jnp.bfloat16
MESH = pl.DeviceIdType.MESH
AXES = ("x", "y", "c")

EPS = 1e-6
HEAD_DIM = 64
SLOT = 128
CONV_CH = 384
CONV_WIDTH = 31
N_Q, N_KV, N_MEMH = 6, 2, 4
GROUP = N_Q // N_KV
BLK = 128
HALO = 32
CONV_ROWS = 64
ROPE_THETA = 10000.0
SCALE = HEAD_DIM ** -0.5
NEG = -1e30

N_SLOTS = N_Q + 2 * N_KV + N_MEMH
PU = 2 * CONV_CH
PH = SLOT * N_SLOTS
PP = PU + PH
QO = 0
KO = QO + SLOT * N_Q
VO = KO + SLOT * N_KV
MO = VO + SLOT * N_KV
NH = N_Q + N_MEMH
YH = SLOT * NH
YP = CONV_CH + SLOT * (N_Q + N_MEMH)
YS = CONV_CH
YM = YS + SLOT * N_Q
D_IN = 2 * CONV_CH + HEAD_DIM * N_SLOTS
D_MIX = CONV_CH + HEAD_DIM * (N_Q + N_MEMH)
MEM_KV = 2 * HEAD_DIM * N_MEMH

ADAM_LR, ADAM_B1, ADAM_B2, ADAM_EPS, ADAM_WD, ADAM_STEP = 0.001, 0.9, 0.999, 1e-08, 0.01, 10

VMEM_LIMIT_MB = 56


def _cp(mb=VMEM_LIMIT_MB):
    return pltpu.CompilerParams(vmem_limit_bytes=mb * 1024 * 1024)


def _dot_nn(a, b):
    return lax.dot_general(a, b, (((1,), (0,)), ((), ())), preferred_element_type=F32)


def _dot_nt(a, b):
    return lax.dot_general(a, b, (((1,), (1,)), ((), ())), preferred_element_type=F32)


def _dot_tn(a, b):
    return lax.dot_general(a, b, (((0,), (0,)), ((), ())), preferred_element_type=F32)


def _sigmoid(x):
    return 1.0 / (1.0 + jnp.exp(-x))


def _rms(x):
    return lax.rsqrt(jnp.mean(x * x, axis=-1, keepdims=True) + EPS)


def _rms_bwd(dn, x, r, g):
    xhat = x * r
    dxhat = dn * g
    dx = r * (dxhat - xhat * jnp.mean(dxhat * xhat, axis=-1, keepdims=True))
    return dx, dn * xhat


def _colsum(v):
    return jnp.sum(v, axis=0, keepdims=True)


def _lane(n):
    return lax.broadcasted_iota(jnp.int32, (n, SLOT), 1)


def _partner(v, lane):
    up = pltpu.roll(v, SLOT - HEAD_DIM // 2, 1)
    dn = pltpu.roll(v, HEAD_DIM // 2, 1)
    return jnp.where(lane < HEAD_DIM // 2, up, jnp.where(lane < HEAD_DIM, dn, 0.0))


def _head_rms(xs):
    return lax.rsqrt(jnp.sum(xs * xs, axis=-1, keepdims=True) * (1.0 / HEAD_DIM) + EPS)


def _head_fwd(xs, g, cosv, sinv, lane):
    xn = xs * _head_rms(xs) * g
    if cosv is None:
        return xn
    return xn * cosv + _partner(xn, lane) * sinv


def _head_bwd(dout, xs, g, cosv, sinv, lane):
    if cosv is not None:
        dout = dout * cosv + _partner(dout * sinv, lane)
    r = _head_rms(xs)
    xhat = xs * r
    dxhat = dout * g
    dx = r * (dxhat - xhat * (jnp.sum(dxhat * xhat, axis=-1, keepdims=True) * (1.0 / HEAD_DIM)))
    return dx, dout * xhat


def _col(v, h, lane):
    return jnp.sum(jnp.where(lane == h, v, 0.0), axis=-1, keepdims=True)


def _place():
    x, y, c = lax.axis_index("x"), lax.axis_index("y"), lax.axis_index("c")
    chips = [(1 - x, y), (x, 1 - y), (1 - x, 1 - y)]
    return x, y, c, chips


class _Gather:
    tag = "_gather"

    def __init__(self, bufs):
        self.bufs = list(bufs)
        nb = len(self.bufs)
        self.out_shape = [jax.ShapeDtypeStruct((b.shape[0], 4) + b.shape[2:], b.dtype) for b in self.bufs]
        self.sems = [pltpu.SemaphoreType.DMA((3 * nb,)), pltpu.SemaphoreType.DMA((3 * nb,)),
                     pltpu.SemaphoreType.DMA((nb,))]

    def _copies(self, ins, outs, sems):
        ssem, rsem, lsem = sems
        nb = len(self.bufs)
        x, y, c, chips = _place()
        mine = 2 * x + y

        def copy(b, p, shard):
            return pltpu.make_async_remote_copy(
                src_ref=ins[b], dst_ref=outs[b].at[:, pl.ds(shard, 1)],
                send_sem=ssem.at[3 * b + p], recv_sem=rsem.at[3 * b + p],
                device_id=(chips[p][0], chips[p][1], c), device_id_type=MESH)

        local = [pltpu.make_async_copy(ins[b], outs[b].at[:, pl.ds(mine, 1)], lsem.at[b]) for b in range(nb)]
        sends = [copy(b, p, mine) for b in range(nb) for p in range(3)]
        recvs = [copy(b, p, 2 * chips[p][0] + chips[p][1]) for b in range(nb) for p in range(3)]
        return local, sends, recvs

    def start(self, ins, outs, sems):
        local, sends, _ = self._copies(ins, outs, sems)
        for cp in local + sends:
            cp.start()

    def wait(self, ins, outs, sems):
        local, sends, recvs = self._copies(ins, outs, sems)
        for cp in recvs:
            cp.wait_recv()
        for cp in sends:
            cp.wait_send()
        for cp in local:
            cp.wait()


class _Scatter:
    tag = "_scatter"

    def __init__(self, bufs):
        self.bufs = list(bufs)
        nb = len(self.bufs)
        self.out_shape = [jax.ShapeDtypeStruct((3, b.shape[0], 1) + b.shape[2:], b.dtype) for b in self.bufs]
        self.sems = [pltpu.SemaphoreType.DMA((3 * nb,)), pltpu.SemaphoreType.DMA((3 * nb,))]

    def _copies(self, ins, outs, sems):
        ssem, rsem = sems
        x, y, c, chips = _place()

        def copy(b, p):
            shard = 2 * chips[p][0] + chips[p][1]
            return pltpu.make_async_remote_copy(
                src_ref=ins[b].at[:, pl.ds(shard, 1)], dst_ref=outs[b].at[p],
                send_sem=ssem.at[3 * b + p], recv_sem=rsem.at[3 * b + p],
                device_id=(chips[p][0], chips[p][1], c), device_id_type=MESH)

        return [copy(b, p) for b in range(len(self.bufs)) for p in range(3)]

    def start(self, ins, outs, sems):
        for cp in self._copies(ins, outs, sems):
            cp.start()

    def wait(self, ins, outs, sems):
        cps = self._copies(ins, outs, sems)
        for cp in cps:
            cp.wait_recv()
        for cp in cps:
            cp.wait_send()


def _run_rider(rider, name):
    nb = len(rider.bufs)

    def body(*refs):
        ins, outs, sems = refs[:nb], refs[nb:2 * nb], refs[2 * nb:]
        rider.start(ins, outs, sems)
        rider.wait(ins, outs, sems)

    hbm = pl.BlockSpec(memory_space=pl.ANY)
    return pl.pallas_call(body, name=name, in_specs=[hbm] * nb, out_specs=[hbm] * nb,
                          out_shape=rider.out_shape, scratch_shapes=rider.sems)(*rider.bufs)


def _call(body, *, name, grid, in_specs, out_specs, out_shape, args, scratch=(), rider=None):
    if rider is None:
        outs = pl.pallas_call(body, name=name, grid=grid, in_specs=list(in_specs), out_specs=list(out_specs),
                              out_shape=list(out_shape), scratch_shapes=list(scratch),
                              compiler_params=_cp())(*args)
        return list(outs), None
    n_in, n_out, n_scr, nb = len(in_specs), len(out_specs), len(scratch), len(rider.bufs)

    def wrapped(*refs):
        cuts = [n_in, nb, n_out, nb, n_scr]
        parts, o = [], 0
        for n in cuts:
            parts.append(refs[o:o + n])
            o += n
        ins, rin, outs, rout, scr = parts
        sems = refs[o:]
        ids = [pl.program_id(k) for k in range(len(grid))]
        first = functools.reduce(jnp.logical_and, [i == 0 for i in ids])
        last = functools.reduce(jnp.logical_and, [i == n - 1 for i, n in zip(ids, grid)])

        @pl.when(first)
        def _():
            rider.start(rin, rout, sems)

        body(*ins, *outs, *scr)

        @pl.when(last)
        def _():
            rider.wait(rin, rout, sems)

    hbm = pl.BlockSpec(memory_space=pl.ANY)
    res = pl.pallas_call(
        wrapped, name=name + rider.tag, grid=grid,
        in_specs=list(in_specs) + [hbm] * nb, out_specs=list(out_specs) + [hbm] * nb,
        out_shape=list(out_shape) + rider.out_shape, scratch_shapes=list(scratch) + rider.sems,
        compiler_params=_cp())(*args, *rider.bufs)
    return list(res[:n_out]), list(res[n_out:])


def _rope_tables(pos, invf, tm):
    S = pos.shape[0]

    def body(pos_ref, f_ref, cos_ref, sin_ref):
        ang = pos_ref[...].astype(F32) * f_ref[...]
        lane = _lane(tm)
        c, s = jnp.cos(ang), jnp.sin(ang)
        cos_ref[...] = jnp.where(lane < HEAD_DIM, c, 0.0)
        sin_ref[...] = jnp.where(lane < HEAD_DIM // 2, -s, jnp.where(lane < HEAD_DIM, s, 0.0))

    return pl.pallas_call(
        body, name="rope_tables", grid=(S // tm,),
        in_specs=[pl.BlockSpec((tm, 1), lambda i: (i, 0)), pl.BlockSpec((1, SLOT), lambda i: (0, 0))],
        out_specs=[pl.BlockSpec((tm, SLOT), lambda i: (i, 0))] * 2,
        out_shape=[jax.ShapeDtypeStruct((S, SLOT), F32)] * 2,
    )(pos, invf)


def _ffn_fwd(x, g, wf, gfin, tm, tf, rider=None):
    S, D = x.shape
    F = wf.shape[1]
    nf = F // tf
    final = gfin is not None

    chunks = [(c, min(256, tf - c)) for c in range(0, tf, 256)]

    def body(*refs):
        if final:
            x_ref, g_ref, w1_ref, w3_ref, w2_ref, gf_ref, h_ref, a_ref, b_ref, t_ref, xn_ref, n_scr, acc = refs
        else:
            x_ref, g_ref, w1_ref, w3_ref, w2_ref, h_ref, a_ref, b_ref, t_ref, n_scr, acc = refs
        j = pl.program_id(1)

        @pl.when(j == 0)
        def _():
            xv = x_ref[...]
            n_scr[...] = (xv * _rms(xv) * g_ref[...]).astype(BF16)
            acc[...] = jnp.zeros_like(acc)

        n = n_scr[...]
        for c0, cw in chunks:
            cols = slice(c0, c0 + cw)
            a = _dot_nt(n, w1_ref[cols, :])
            b = _dot_nt(n, w3_ref[cols, :])
            a_ref[:, cols] = a.astype(BF16)
            b_ref[:, cols] = b.astype(BF16)
            t_ref[:, cols] = (a * _sigmoid(a) * b).astype(BF16)
        acc[...] += _dot_nn(t_ref[...], w2_ref[...])

        @pl.when(j == nf - 1)
        def _():
            h = x_ref[...] + 0.5 * acc[...]
            h_ref[...] = h
            if final:
                xn_ref[...] = h * _rms(h) * gf_ref[...]

    def wspec(k):
        return pl.BlockSpec((None, tf, D), lambda i, j, k=k: (k, j, 0))

    row = pl.BlockSpec((tm, D), lambda i, j: (i, 0))
    vec = pl.BlockSpec((1, D), lambda i, j: (0, 0))
    act = pl.BlockSpec((tm, tf), lambda i, j: (i, j))
    in_specs = [row, vec, wspec(0), wspec(1), wspec(2)] + ([vec] if final else [])
    out_specs = [row, act, act, act] + ([row] if final else [])
    out_shape = [jax.ShapeDtypeStruct((S, D), F32)] + [jax.ShapeDtypeStruct((S, F), BF16)] * 3 \
        + ([jax.ShapeDtypeStruct((S, D), F32)] if final else [])
    args = [x, g, wf, wf, wf] + ([gfin] if final else [])
    return _call(body, name="ffn_fwd_final" if final else "ffn_fwd", grid=(S // tm, nf),
                 in_specs=in_specs, out_specs=out_specs, out_shape=out_shape, args=args,
                 scratch=[pltpu.VMEM((tm, D), BF16), pltpu.VMEM((tm, D), F32)], rider=rider)


def _ffn_bwd_act(dh, x, g, a, b, wf, tm, tf, rider=None):
    S, D = x.shape
    F = wf.shape[1]
    nf = F // tf

    chunks = [(c, min(256, tf - c)) for c in range(0, tf, 256)]

    def body(dh_ref, x_ref, g_ref, a_ref, b_ref, w1_ref, w3_ref, w2_ref,
             dx_ref, dg_ref, da_ref, db_ref, n_ref, dy_ref, acc):
        i, j = pl.program_id(0), pl.program_id(1)

        @pl.when(j == 0)
        def _():
            xv = x_ref[...]
            n_ref[...] = (xv * _rms(xv) * g_ref[...]).astype(BF16)
            dy_ref[...] = (0.5 * dh_ref[...]).astype(BF16)
            acc[...] = jnp.zeros_like(acc)

            @pl.when(i == 0)
            def _():
                dg_ref[...] = jnp.zeros_like(dg_ref)

        dyv = dy_ref[...]
        for c0, cw in chunks:
            cols = slice(c0, c0 + cw)
            av = a_ref[:, cols].astype(F32)
            bv = b_ref[:, cols].astype(F32)
            sg = _sigmoid(av)
            dt = _dot_nt(dyv, w2_ref[cols, :])
            db_ref[:, cols] = (dt * (av * sg)).astype(BF16)
            da_ref[:, cols] = (dt * bv * (sg * (1.0 + av * (1.0 - sg)))).astype(BF16)
        acc[...] += _dot_nn(da_ref[...], w1_ref[...]) + _dot_nn(db_ref[...], w3_ref[...])

        @pl.when(j == nf - 1)
        def _():
            xv = x_ref[...]
            dx, dgrow = _rms_bwd(acc[...], xv, _rms(xv), g_ref[...])
            dx_ref[...] = dh_ref[...] + dx
            dg_ref[...] += _colsum(dgrow)

    def wspec(k):
        return pl.BlockSpec((None, tf, D), lambda i, j, k=k: (k, j, 0))

    row = pl.BlockSpec((tm, D), lambda i, j: (i, 0))
    vec = pl.BlockSpec((1, D), lambda i, j: (0, 0))
    act = pl.BlockSpec((tm, tf), lambda i, j: (i, j))
    sd = lambda shp, dt: jax.ShapeDtypeStruct(shp, dt)
    return _call(body, name="ffn_bwd_act", grid=(S // tm, nf),
                 in_specs=[row, row, vec, act, act, wspec(0), wspec(1), wspec(2)],
                 out_specs=[row, vec, act, act, row, row],
                 out_shape=[sd((S, D), F32), sd((1, D), F32), sd((S, F), BF16), sd((S, F), BF16),
                            sd((S, D), BF16), sd((S, D), BF16)],
                 args=[dh, x, g, a, b, wf, wf, wf], scratch=[pltpu.VMEM((tm, D), F32)], rider=rider)


def _ffn_bwd_w(da, db, t, n, dy, tm, tf):
    S, F = da.shape
    D = n.shape[1]
    nt = S // tm

    def body(da_ref, db_ref, t_ref, n_ref, dy_ref, out_ref, acc):
        i = pl.program_id(1)

        @pl.when(i == 0)
        def _():
            acc[...] = jnp.zeros_like(acc)

        nv = n_ref[...]
        acc[0] += _dot_tn(da_ref[...], nv)
        acc[1] += _dot_tn(db_ref[...], nv)
        acc[2] += _dot_tn(t_ref[...], dy_ref[...])

        @pl.when(i == nt - 1)
        def _():
            out_ref[...] = acc[...].astype(BF16)

    act = pl.BlockSpec((tm, tf), lambda j, i: (i, j))
    row = pl.BlockSpec((tm, D), lambda j, i: (i, 0))
    return pl.pallas_call(
        body, name="ffn_bwd_w", grid=(F // tf, nt),
        in_specs=[act, act, act, row, row],
        out_specs=pl.BlockSpec((3, tf, D), lambda j, i: (0, j, 0)),
        out_shape=jax.ShapeDtypeStruct((3, F, D), BF16),
        scratch_shapes=[pltpu.VMEM((3, tf, D), F32)],
        compiler_params=_cp(),
    )(da, db, t, n, dy)


def _proj_fwd(h, g, w_inp, tm):
    S, D = h.shape

    def body(h_ref, g_ref, w_ref, pu_ref, ph_ref, n_ref):
        hv = h_ref[...]
        n = (hv * _rms(hv) * g_ref[...]).astype(BF16)
        n_ref[...] = n
        pu_ref[...] = _dot_nt(n, w_ref[0:PU, :])
        ph_ref[...] = _dot_nt(n, w_ref[PU:PP, :])

    cur = lambda w: pl.BlockSpec((tm, w), lambda i: (i, 0))
    return pl.pallas_call(
        body, name="proj_fwd", grid=(S // tm,),
        in_specs=[cur(D), pl.BlockSpec((1, D), lambda i: (0, 0)), pl.BlockSpec((PP, D), lambda i: (0, 0))],
        out_specs=[cur(PU), cur(PH), cur(D)],
        out_shape=[jax.ShapeDtypeStruct((S, PU), F32), jax.ShapeDtypeStruct((S, PH), F32),
                   jax.ShapeDtypeStruct((S, D), BF16)],
        compiler_params=_cp(),
    )(h, g, w_inp)


def _glu(u):
    return u[:, :CONV_CH] * _sigmoid(u[:, CONV_CH:2 * CONV_CH])


def _shifted_copies(ext8):
    n = ext8.shape[1]
    for b in range(1, 8):
        ext8[b, 0:n - 8, :] = ext8[0, b:b + n - 8, :]


def _window(ext8, off, rows, r0=0):
    return ext8[off % 8, pl.ds(r0 + (off - off % 8), rows), :]


def _layer_norm_stats(yc):
    mu = jnp.mean(yc, axis=-1, keepdims=True)
    d = yc - mu
    rstd = lax.rsqrt(jnp.mean(d * d, axis=-1, keepdims=True) + EPS)
    return d * rstd, rstd


def _mem_kv_fwd(mem, g, w_mkvp, gk):
    M, D = mem.shape
    W = SLOT * N_MEMH

    def body(mem_ref, g_ref, w_ref, gk_ref, nm_ref, raw_ref, mk_ref, mv_ref):
        mv_ = mem_ref[...]
        nm = (mv_ * _rms(mv_) * g_ref[...]).astype(BF16)
        nm_ref[...] = nm
        raw = _dot_nn(nm, w_ref[...])
        raw_ref[...] = raw
        for hh in range(N_MEMH):
            sl = slice(SLOT * hh, SLOT * (hh + 1))
            mk_ref[:, sl] = _head_fwd(raw[:, sl], gk_ref[...], None, None, None).astype(BF16)
        mv_ref[...] = raw[:, W:].astype(BF16)

    sd = jax.ShapeDtypeStruct
    return pl.pallas_call(
        body, name="mem_kv_fwd",
        out_shape=[sd((M, D), BF16), sd((M, 2 * W), F32), sd((M, W), BF16), sd((M, W), BF16)],
        compiler_params=_cp(),
    )(mem, g, w_mkvp, gk)


def _mem_kv_bwd(dmk, dmv, raw, nm, mem, g, w_mkvp, gk):
    M, D = mem.shape
    W = SLOT * N_MEMH

    def body(dmk_ref, dmv_ref, raw_ref, nm_ref, mem_ref, g_ref, w_ref, gk_ref, dw_ref, dg_ref, dgk_ref, draw):
        dgk = jnp.zeros((1, SLOT), F32)
        for hh in range(N_MEMH):
            sl = slice(SLOT * hh, SLOT * (hh + 1))
            dx, prod = _head_bwd(dmk_ref[:, sl], raw_ref[:, sl], gk_ref[...], None, None, None)
            draw[:, sl] = dx.astype(BF16)
            dgk = dgk + _colsum(prod)
        dgk_ref[...] = dgk
        draw[:, W:] = dmv_ref[...].astype(BF16)
        dr = draw[...]
        dw_ref[...] = _dot_tn(nm_ref[...], dr)
        dnm = _dot_nt(dr, w_ref[...])
        mv_ = mem_ref[...]
        dg_ref[...] = _colsum(dnm * (mv_ * _rms(mv_)))

    sd = jax.ShapeDtypeStruct
    return pl.pallas_call(
        body, name="mem_kv_bwd",
        out_shape=[sd((D, 2 * W), F32), sd((1, D), F32), sd((1, SLOT), F32)],
        scratch_shapes=[pltpu.VMEM((M, 2 * W), BF16)],
        compiler_params=_cp(),
    )(dmk, dmv, raw, nm, mem, g, w_mkvp, gk)


def _mixer_fwd(pu, ph, h, cosT, sinT, conv_w, conv_b, ln_g, ln_b, gq, gk, sinks, mk, mv, gqm, w_outp, tm, rider=None):
    S, D = h.shape
    M = mk.shape[0]
    nb = tm // BLK
    nblocks = S // BLK

    def body(pu_ref, pup_ref, p_ref, ph_ref, h_ref, cos_ref, cosh_ref, sin_ref, sinh_ref, cw_ref, cb_ref,
             lg_ref, lb_ref, gq_ref, gk_ref, sink_ref, mk_ref, mv_ref, gqm_ref, wo_ref,
             h2_ref, y_ref, yc_ref, lse_ref, ext, y_scr):
        i = pl.program_id(0)
        not_first = (i > 0).astype(F32)
        lane = _lane(tm)
        lane_e = _lane(tm + BLK)

        ext[0, 0:HALO, :] = _glu(pup_ref[...]) * not_first
        ext[0, HALO:HALO + tm, :] = _glu(pu_ref[...])
        _shifted_copies(ext)

        def rows_chunk(r, carry):
            r0 = pl.multiple_of(r * CONV_ROWS, CONV_ROWS)
            yc = jnp.zeros((CONV_ROWS, CONV_CH), F32) + cb_ref[...]
            for k in range(CONV_WIDTH):
                yc = yc + cw_ref[k:k + 1, :] * _window(ext, HALO - (CONV_WIDTH - 1) + k, CONV_ROWS, r0)
            yc_ref[pl.ds(r0, CONV_ROWS), :] = yc
            z, _ = _layer_norm_stats(yc)
            ln = z * lg_ref[...] + lb_ref[...]
            y_scr[pl.ds(r0, CONV_ROWS), 0:CONV_CH] = (ln * _sigmoid(ln)).astype(BF16)
            return carry

        lax.fori_loop(0, tm // CONV_ROWS, rows_chunk, 0)

        cos_e = jnp.concatenate([cosh_ref[...], cos_ref[...]], axis=0)
        sin_e = jnp.concatenate([sinh_ref[...], sin_ref[...]], axis=0)
        qi = lax.broadcasted_iota(jnp.int32, (BLK, 2 * BLK), 0)
        kj = lax.broadcasted_iota(jnp.int32, (BLK, 2 * BLK), 1)
        band = (kj > qi) & (kj <= qi + BLK)
        band0 = band & ((kj >= BLK) | (i > 0))
        lse = jnp.zeros((tm, SLOT), F32)
        for kvh in range(N_KV):
            ks = slice(KO + SLOT * kvh, KO + SLOT * (kvh + 1))
            vs = slice(VO + SLOT * kvh, VO + SLOT * (kvh + 1))
            k_raw = jnp.concatenate([ph_ref[:, ks], p_ref[:, ks]], axis=0)
            k_e = _head_fwd(k_raw, gk_ref[...], cos_e, sin_e, lane_e).astype(BF16)
            v_e = jnp.concatenate([ph_ref[:, vs], p_ref[:, vs]], axis=0).astype(BF16)
            for gi in range(GROUP):
                hq = GROUP * kvh + gi
                qs = slice(QO + SLOT * hq, QO + SLOT * (hq + 1))
                q = _head_fwd(p_ref[:, qs], gq_ref[...], cos_ref[...], sin_ref[...], lane).astype(BF16)
                sink = sink_ref[hq]
                outs, lses = [], []
                for m in range(nb):
                    rows = slice(BLK * m, BLK * (m + 1))
                    win = slice(BLK * m, BLK * (m + 2))
                    s = _dot_nt(q[rows], k_e[win]) * SCALE
                    s = jnp.where(band0 if m == 0 else band, s, NEG)
                    mx = jnp.maximum(jnp.max(s, axis=-1, keepdims=True), sink)
                    e = jnp.exp(s - mx)
                    den = jnp.sum(e, axis=-1, keepdims=True) + jnp.exp(sink - mx)
                    prob = e / den
                    outs.append(_dot_nn(prob.astype(BF16), v_e[win]))
                    lses.append(mx + jnp.log(den))
                y_scr[:, YS + SLOT * hq:YS + SLOT * (hq + 1)] = jnp.concatenate(outs, axis=0).astype(BF16)
                lse = jnp.where(lane == hq, jnp.concatenate(lses, axis=0), lse)

        for hm in range(N_MEMH):
            ms = slice(SLOT * hm, SLOT * (hm + 1))
            qm = _head_fwd(p_ref[:, MO + SLOT * hm:MO + SLOT * (hm + 1)], gqm_ref[...], None, None, None)
            s = _dot_nt(qm.astype(BF16), mk_ref[:, ms]) * SCALE
            mx = jnp.max(s, axis=-1, keepdims=True)
            e = jnp.exp(s - mx)
            den = jnp.sum(e, axis=-1, keepdims=True)
            o = _dot_nn((e / den).astype(BF16), mv_ref[:, ms])
            y_scr[:, YM + SLOT * hm:YM + SLOT * (hm + 1)] = o.astype(BF16)
            lse = jnp.where(lane == N_Q + hm, mx + jnp.log(den), lse)
        lse_ref[...] = lse

        yv = y_scr[...]
        y_ref[...] = yv
        h2_ref[...] = h_ref[...] + _dot_nn(yv, wo_ref[...])

    cur = lambda w: pl.BlockSpec((tm, w), lambda i: (i, 0))
    prev = lambda w: pl.BlockSpec((BLK, w), lambda i: (jnp.maximum(i * nb - 1, 0), 0))
    full = lambda a: pl.BlockSpec(a.shape, lambda i: (0,) * a.ndim)
    sd = jax.ShapeDtypeStruct
    prev32 = pl.BlockSpec((HALO, PU), lambda i: (jnp.maximum(i * (tm // HALO) - 1, 0), 0))
    return _call(
        body, name="mixer_fwd", grid=(S // tm,),
        in_specs=[cur(PU), prev32, cur(PH), prev(PH), cur(D), cur(SLOT), prev(SLOT), cur(SLOT), prev(SLOT),
                  full(conv_w), full(conv_b), full(ln_g), full(ln_b), full(gq), full(gk),
                  pl.BlockSpec(memory_space=pltpu.SMEM), full(mk), full(mv), full(gqm), full(w_outp)],
        out_specs=[cur(D), cur(YP), cur(CONV_CH), cur(SLOT)],
        out_shape=[sd((S, D), F32), sd((S, YP), BF16), sd((S, CONV_CH), F32), sd((S, SLOT), F32)],
        args=[pu, pu, ph, ph, h, cosT, cosT, sinT, sinT, conv_w, conv_b, ln_g, ln_b, gq, gk, sinks, mk, mv, gqm,
              w_outp],
        scratch=[pltpu.VMEM((8, tm + HALO, CONV_CH), F32), pltpu.VMEM((tm, YP), BF16)], rider=rider)


def _outproj_bwd(dh2, y, yc, ln_g, ln_b, w_outp, tm):
    S, D = dh2.shape

    def body(dh_ref, y_ref, yc_ref, lg_ref, lb_ref, wo_ref, dyc_ref, do_ref, del_ref, dwo_ref, dlg_ref, dlb_ref):
        i = pl.program_id(0)

        @pl.when(i == 0)
        def _():
            dwo_ref[...] = jnp.zeros_like(dwo_ref)
            dlg_ref[...] = jnp.zeros_like(dlg_ref)
            dlb_ref[...] = jnp.zeros_like(dlb_ref)

        dhb = dh_ref[...].astype(BF16)
        yv = y_ref[...]
        dy = _dot_nt(dhb, wo_ref[...])
        dwo_ref[...] += _dot_tn(yv, dhb)

        z, rstd = _layer_norm_stats(yc_ref[...])
        ln = z * lg_ref[...] + lb_ref[...]
        sg = _sigmoid(ln)
        dln = dy[:, 0:CONV_CH] * (sg * (1.0 + ln * (1.0 - sg)))
        dlg_ref[...] += _colsum(dln * z)
        dlb_ref[...] += _colsum(dln)
        dz = dln * lg_ref[...]
        dyc_ref[...] = rstd * (dz - jnp.mean(dz, axis=-1, keepdims=True)
                               - z * jnp.mean(dz * z, axis=-1, keepdims=True))
        do_ref[...] = dy[:, CONV_CH:].astype(BF16)

        lane = _lane(tm)
        delta = jnp.zeros((tm, SLOT), F32)
        for hh in range(NH):
            sl = slice(YS + SLOT * hh, YS + SLOT * (hh + 1))
            d = jnp.sum(dy[:, sl] * yv[:, sl].astype(F32), axis=-1, keepdims=True)
            delta = jnp.where(lane == hh, d, delta)
        del_ref[...] = delta

    cur = lambda w: pl.BlockSpec((tm, w), lambda i: (i, 0))
    full = lambda a: pl.BlockSpec(a.shape, lambda i: (0,) * a.ndim)
    sd = jax.ShapeDtypeStruct
    return pl.pallas_call(
        body, name="outproj_bwd", grid=(S // tm,),
        in_specs=[cur(D), cur(YP), cur(CONV_CH), full(ln_g), full(ln_b), full(w_outp)],
        out_specs=[cur(CONV_CH), cur(YH), cur(SLOT), pl.BlockSpec((YP, D), lambda i: (0, 0)),
                   pl.BlockSpec((1, CONV_CH), lambda i: (0, 0)), pl.BlockSpec((1, CONV_CH), lambda i: (0, 0))],
        out_shape=[sd((S, CONV_CH), F32), sd((S, YH), BF16), sd((S, SLOT), F32), sd((YP, D), F32),
                   sd((1, CONV_CH), F32), sd((1, CONV_CH), F32)],
        compiler_params=_cp(),
    )(dh2, y, yc, ln_g, ln_b, w_outp)


def _conv_bwd(pu, dyc, conv_w, tm):
    S = pu.shape[0]
    nt = S // tm
    nh = tm // HALO

    def body(pu_ref, pup_ref, dy_ref, dyn_ref, cw_ref, dpu_ref, dcw_ref, dcb_ref, ext, ext2, dcw8):
        i = pl.program_id(0)

        @pl.when(i == 0)
        def _():
            dcw8[...] = jnp.zeros_like(dcw8)
            dcb_ref[...] = jnp.zeros_like(dcb_ref)

        not_first = (i > 0).astype(F32)
        not_last = (i < nt - 1).astype(F32)
        ext[0, 0:HALO, :] = _glu(pup_ref[...]) * not_first
        ext[0, HALO:HALO + tm, :] = _glu(pu_ref[...])
        _shifted_copies(ext)
        ext2[0, 0:tm, :] = dy_ref[...]
        ext2[0, tm:tm + HALO, :] = dyn_ref[...] * not_last
        _shifted_copies(ext2)
        dcb_ref[...] += _colsum(dy_ref[...])

        def rows_chunk(r, carry):
            r0 = pl.multiple_of(r * CONV_ROWS, CONV_ROWS)
            dyc_ = dy_ref[pl.ds(r0, CONV_ROWS), :]
            dyg = jnp.zeros((CONV_ROWS, CONV_CH), F32)
            for k in range(CONV_WIDTH):
                prod = dyc_ * _window(ext, HALO - (CONV_WIDTH - 1) + k, CONV_ROWS, r0)
                dcw8[k] += jnp.sum(prod.reshape(CONV_ROWS // 8, 8, CONV_CH), axis=0)
                dyg = dyg + cw_ref[k:k + 1, :] * _window(ext2, CONV_WIDTH - 1 - k, CONV_ROWS, r0)
            u = pu_ref[pl.ds(r0, CONV_ROWS), :]
            a_, sg = u[:, :CONV_CH], _sigmoid(u[:, CONV_CH:])
            dpu_ref[pl.ds(r0, CONV_ROWS), 0:CONV_CH] = (dyg * sg).astype(BF16)
            dpu_ref[pl.ds(r0, CONV_ROWS), CONV_CH:PU] = (dyg * a_ * sg * (1.0 - sg)).astype(BF16)
            return carry

        lax.fori_loop(0, tm // CONV_ROWS, rows_chunk, 0)

        @pl.when(i == nt - 1)
        def _():
            dcw_ref[...] = jnp.sum(dcw8[...], axis=1)

    cur = lambda w: pl.BlockSpec((tm, w), lambda i: (i, 0))
    prev = lambda w: pl.BlockSpec((HALO, w), lambda i: (jnp.maximum(i * nh - 1, 0), 0))
    nxt = lambda w: pl.BlockSpec((HALO, w), lambda i: (jnp.minimum((i + 1) * nh, S // HALO - 1), 0))
    acc = lambda r, w: pl.BlockSpec((r, w), lambda i: (0, 0))
    sd = jax.ShapeDtypeStruct
    return pl.pallas_call(
        body, name="conv_bwd", grid=(nt,),
        in_specs=[cur(PU), prev(PU), cur(CONV_CH), nxt(CONV_CH), acc(32, CONV_CH)],
        out_specs=[cur(PU), acc(32, CONV_CH), acc(1, CONV_CH)],
        out_shape=[sd((S, PU), BF16), sd((32, CONV_CH), F32), sd((1, CONV_CH), F32)],
        scratch_shapes=[pltpu.VMEM((8, tm + HALO, CONV_CH), F32), pltpu.VMEM((8, tm + HALO, CONV_CH), F32),
                        pltpu.VMEM((32, 8, CONV_CH), F32)],
        compiler_params=_cp(),
    )(pu, pu, dyc, dyc, conv_w)


def _attn_bwd(p, do, lse, delta, cosT, sinT, gq, gk, sinks, mk, mv, gqm, tm, rider=None):
    S = p.shape[0]
    M = mk.shape[0]
    nb = tm // BLK
    nt = S // tm
    nblocks = S // BLK
    W = SLOT * N_MEMH

    def body(p_ref, pp_ref, pn_ref, dy_ref, dyn_ref, lse_ref, lsen_ref, del_ref, deln_ref,
             cos_ref, cosp_ref, cosn_ref, sin_ref, sinp_ref, sinn_ref,
             gq_ref, gk_ref, sink_ref, mk_ref, mv_ref, gqm_ref,
             dp_ref, dgq_ref, dgk_ref, dgqm_ref, dsink_ref, dmk_ref, dmv_ref):
        i = pl.program_id(0)

        @pl.when(i == 0)
        def _():
            for r in (dgq_ref, dgk_ref, dgqm_ref, dsink_ref, dmk_ref, dmv_ref):
                r[...] = jnp.zeros_like(r)

        lane = _lane(tm)
        lane_e = _lane(tm + BLK)

        cos_k = jnp.concatenate([cosp_ref[...], cos_ref[...]], axis=0)
        sin_k = jnp.concatenate([sinp_ref[...], sin_ref[...]], axis=0)
        cos_q = jnp.concatenate([cos_ref[...], cosn_ref[...]], axis=0)
        sin_q = jnp.concatenate([sin_ref[...], sinn_ref[...]], axis=0)
        lse_e = jnp.concatenate([lse_ref[...], lsen_ref[...]], axis=0)
        del_e = jnp.concatenate([del_ref[...], deln_ref[...]], axis=0)
        qi = lax.broadcasted_iota(jnp.int32, (BLK, BLK), 0)
        kj = lax.broadcasted_iota(jnp.int32, (BLK, BLK), 1)
        diag = kj <= qi
        offd = kj > qi
        dgq = jnp.zeros((1, SLOT), F32)
        dgk = jnp.zeros((1, SLOT), F32)
        dsink = jnp.zeros((1, SLOT), F32)
        lane1 = lax.broadcasted_iota(jnp.int32, (1, SLOT), 1)
        for kvh in range(N_KV):
            ks = slice(KO + SLOT * kvh, KO + SLOT * (kvh + 1))
            vs = slice(VO + SLOT * kvh, VO + SLOT * (kvh + 1))
            k_raw = jnp.concatenate([pp_ref[:, ks], p_ref[:, ks]], axis=0)
            k_e = _head_fwd(k_raw, gk_ref[...], cos_k, sin_k, lane_e).astype(BF16)
            v_e = jnp.concatenate([pp_ref[:, vs], p_ref[:, vs]], axis=0).astype(BF16)
            dk = [jnp.zeros((BLK, SLOT), F32) for _ in range(nb)]
            dv = [jnp.zeros((BLK, SLOT), F32) for _ in range(nb)]
            for gi in range(GROUP):
                hq = GROUP * kvh + gi
                qs = slice(QO + SLOT * hq, QO + SLOT * (hq + 1))
                os_ = slice(SLOT * hq, SLOT * (hq + 1))
                q_raw = jnp.concatenate([p_ref[:, qs], pn_ref[:, qs]], axis=0)
                q_e = _head_fwd(q_raw, gq_ref[...], cos_q, sin_q, lane_e).astype(BF16)
                do_e = jnp.concatenate([dy_ref[:, os_], dyn_ref[:, os_]], axis=0)
                lse_h = _col(lse_e, hq, lane_e)
                del_h = _col(del_e, hq, lane_e)
                sink = sink_ref[hq]
                dq = [jnp.zeros((BLK, SLOT), F32) for _ in range(nb)]
                for m in range(nb + 1):
                    rows = slice(BLK * m, BLK * (m + 1))
                    qb, dob, lb, db_ = q_e[rows], do_e[rows], lse_h[rows], del_h[rows]
                    for n in (m - 1, m):
                        if n == nb:
                            continue
                        krows = slice(BLK * (n + 1), BLK * (n + 2))
                        kb, vb = k_e[krows], v_e[krows]
                        s = _dot_nt(qb, kb) * SCALE
                        mask = diag if n == m else offd
                        if n == -1:
                            mask = mask & (i > 0)
                        if m == nb:
                            mask = mask & (i < nt - 1)
                        prob = jnp.where(mask, jnp.exp(jnp.where(mask, s - lb, NEG)), 0.0)
                        dpb = _dot_nt(dob, vb)
                        ds = (prob * (dpb - db_) * SCALE).astype(BF16)
                        if m < nb:
                            dq[m] = dq[m] + _dot_nn(ds, kb)
                        if n >= 0:
                            dk[n] = dk[n] + _dot_tn(ds, qb)
                            dv[n] = dv[n] + _dot_tn(prob.astype(BF16), dob)
                dqr, prod = _head_bwd(jnp.concatenate(dq, axis=0), p_ref[:, qs], gq_ref[...],
                                      cos_ref[...], sin_ref[...], lane)
                dp_ref[:, qs] = dqr.astype(BF16)
                dgq = dgq + _colsum(prod)
                psink = jnp.exp(sink - lse_h[0:tm])
                dsink = dsink + jnp.where(lane1 == hq, -_colsum(psink * del_h[0:tm]), 0.0)
            dkr, prod = _head_bwd(jnp.concatenate(dk, axis=0), p_ref[:, ks], gk_ref[...],
                                  cos_ref[...], sin_ref[...], lane)
            dp_ref[:, ks] = dkr.astype(BF16)
            dgk = dgk + _colsum(prod)
            dp_ref[:, vs] = jnp.concatenate(dv, axis=0).astype(BF16)
        dgq_ref[...] += dgq
        dgk_ref[...] += dgk
        dsink_ref[...] += dsink

        dgqm = jnp.zeros((1, SLOT), F32)
        for hm in range(N_MEMH):
            ms = slice(SLOT * hm, SLOT * (hm + 1))
            qs = slice(MO + SLOT * hm, MO + SLOT * (hm + 1))
            os_ = slice(SLOT * (N_Q + hm), SLOT * (N_Q + hm + 1))
            qraw = p_ref[:, qs]
            qm = _head_fwd(qraw, gqm_ref[...], None, None, None).astype(BF16)
            kb, vb = mk_ref[:, ms], mv_ref[:, ms]
            dob = dy_ref[:, os_]
            s = _dot_nt(qm, kb) * SCALE
            prob = jnp.exp(s - _col(lse_ref[...], N_Q + hm, lane))
            dpb = _dot_nt(dob, vb)
            ds = (prob * (dpb - _col(del_ref[...], N_Q + hm, lane)) * SCALE).astype(BF16)
            dqr, prod = _head_bwd(_dot_nn(ds, kb), qraw, gqm_ref[...], None, None, None)
            dp_ref[:, qs] = dqr.astype(BF16)
            dgqm = dgqm + _colsum(prod)
            dmk_ref[:, ms] += _dot_tn(ds, qm)
            dmv_ref[:, ms] += _dot_tn(prob.astype(BF16), dob)
        dgqm_ref[...] += dgqm

    cur = lambda w: pl.BlockSpec((tm, w), lambda i: (i, 0))
    prev = lambda w: pl.BlockSpec((BLK, w), lambda i: (jnp.maximum(i * nb - 1, 0), 0))
    nxt = lambda w: pl.BlockSpec((BLK, w), lambda i: (jnp.minimum((i + 1) * nb, nblocks - 1), 0))
    full = lambda a: pl.BlockSpec(a.shape, lambda i: (0,) * a.ndim)
    acc = lambda r, w: pl.BlockSpec((r, w), lambda i: (0, 0))
    sd = jax.ShapeDtypeStruct
    return _call(
        body, name="attn_bwd", grid=(nt,),
        in_specs=[cur(PH), prev(PH), nxt(PH), cur(YH), nxt(YH), cur(SLOT), nxt(SLOT), cur(SLOT), nxt(SLOT),
                  cur(SLOT), prev(SLOT), nxt(SLOT), cur(SLOT), prev(SLOT), nxt(SLOT),
                  full(gq), full(gk), pl.BlockSpec(memory_space=pltpu.SMEM), full(mk), full(mv), full(gqm)],
        out_specs=[cur(PH), acc(1, SLOT), acc(1, SLOT), acc(1, SLOT), acc(1, SLOT), acc(M, W), acc(M, W)],
        out_shape=[sd((S, PH), BF16), sd((1, SLOT), F32), sd((1, SLOT), F32), sd((1, SLOT), F32),
                   sd((1, SLOT), F32), sd((M, W), F32), sd((M, W), F32)],
        args=[p, p, p, do, do, lse, lse, delta, delta, cosT, cosT, cosT, sinT, sinT, sinT,
              gq, gk, sinks, mk, mv, gqm],
        rider=rider)


def _proj_bwd(dpu, dph, h, dh2, g, n, w_inp, tm):
    S, D = h.shape

    def body(dpu_ref, dph_ref, h_ref, dh2_ref, g_ref, n_ref, w_ref, dh_ref, dg_ref, dw_ref):
        i = pl.program_id(0)

        @pl.when(i == 0)
        def _():
            dg_ref[...] = jnp.zeros_like(dg_ref)
            dw_ref[...] = jnp.zeros_like(dw_ref)

        dpu, dph, nv = dpu_ref[...], dph_ref[...], n_ref[...]
        dn = _dot_nn(dpu, w_ref[0:PU, :]) + _dot_nn(dph, w_ref[PU:PP, :])
        dw_ref[0:PU, :] += _dot_tn(dpu, nv)
        dw_ref[PU:PP, :] += _dot_tn(dph, nv)
        hv = h_ref[...]
        dx, dgrow = _rms_bwd(dn, hv, _rms(hv), g_ref[...])
        dh_ref[...] = dh2_ref[...] + dx
        dg_ref[...] += _colsum(dgrow)

    cur = lambda w: pl.BlockSpec((tm, w), lambda i: (i, 0))
    sd = jax.ShapeDtypeStruct
    return pl.pallas_call(
        body, name="proj_bwd", grid=(S // tm,),
        in_specs=[cur(PU), cur(PH), cur(D), cur(D), pl.BlockSpec((1, D), lambda i: (0, 0)), cur(D),
                  pl.BlockSpec((PP, D), lambda i: (0, 0))],
        out_specs=[cur(D), pl.BlockSpec((1, D), lambda i: (0, 0)), pl.BlockSpec((PP, D), lambda i: (0, 0))],
        out_shape=[sd((S, D), F32), sd((1, D), F32), sd((PP, D), F32)],
        compiler_params=_cp(),
    )(dpu, dph, h, dh2, g, n, w_inp)


def _norm_bwd(dxn, h, g, tm):
    S, D = h.shape

    def body(d_ref, h_ref, g_ref, dh_ref, dg_ref):
        @pl.when(pl.program_id(0) == 0)
        def _():
            dg_ref[...] = jnp.zeros_like(dg_ref)

        hv = h_ref[...]
        dx, dgrow = _rms_bwd(d_ref[...], hv, _rms(hv), g_ref[...])
        dh_ref[...] = dx
        dg_ref[...] += _colsum(dgrow)

    cur = pl.BlockSpec((tm, D), lambda i: (i, 0))
    vec = pl.BlockSpec((1, D), lambda i: (0, 0))
    return pl.pallas_call(
        body, name="norm_bwd", grid=(S // tm,), in_specs=[cur, cur, vec], out_specs=[cur, vec],
        out_shape=[jax.ShapeDtypeStruct((S, D), F32), jax.ShapeDtypeStruct((1, D), F32)],
        compiler_params=_cp(),
    )(dxn, h, g)


def _loss_bwd(xn, h, g, target, tm):
    S, D = h.shape

    def body(y_ref, h_ref, g_ref, t_ref, loss_ref, dh_ref, dg_ref):
        @pl.when(pl.program_id(0) == 0)
        def _():
            dg_ref[...] = jnp.zeros_like(dg_ref)
            loss_ref[...] = jnp.zeros_like(loss_ref)

        err = y_ref[...] - t_ref[...]
        part = jnp.sum(jnp.mean(err * err, axis=-1, keepdims=True), axis=0, keepdims=True)
        loss_ref[...] += 0.5 * part
        hv = h_ref[...]
        dx, dgrow = _rms_bwd(err * (1.0 / D), hv, _rms(hv), g_ref[...])
        dh_ref[...] = dx
        dg_ref[...] += _colsum(dgrow)

    cur = pl.BlockSpec((tm, D), lambda i: (i, 0))
    vec = pl.BlockSpec((1, D), lambda i: (0, 0))
    return pl.pallas_call(
        body, name="loss_bwd", grid=(S // tm,), in_specs=[cur, cur, vec, cur],
        out_specs=[pl.BlockSpec((1, SLOT), lambda i: (0, 0)), cur, vec],
        out_shape=[jax.ShapeDtypeStruct((1, SLOT), F32), jax.ShapeDtypeStruct((S, D), F32),
                   jax.ShapeDtypeStruct((1, D), F32)],
        compiler_params=_cp(),
    )(xn, h, g, target)


def _swap_with_sibling(bufs):
    nbuf = len(bufs)

    def body(*refs):
        ins, outs = refs[:nbuf], refs[nbuf:2 * nbuf]
        ssem, rsem = refs[2 * nbuf:]
        x, y, c, _ = _place()
        sends = [pltpu.make_async_remote_copy(src_ref=ins[b], dst_ref=outs[b], send_sem=ssem.at[b],
                                              recv_sem=rsem.at[b], device_id=(x, y, 1 - c), device_id_type=MESH)
                 for b in range(nbuf)]
        for cp in sends:
            cp.start()
        for cp in sends:
            cp.wait_recv()
        for cp in sends:
            cp.wait_send()

    hbm = pl.BlockSpec(memory_space=pl.ANY)
    return pl.pallas_call(
        body, name="swap_with_sibling",
        in_specs=[hbm] * nbuf, out_specs=[hbm] * nbuf,
        out_shape=[jax.ShapeDtypeStruct(b.shape, b.dtype) for b in bufs],
        scratch_shapes=[pltpu.SemaphoreType.DMA((nbuf,)), pltpu.SemaphoreType.DMA((nbuf,))],
    )(*bufs)


def _all_gather_small(buf):
    _, R, W = buf.shape

    def body(in_ref, out_ref, ssem, rsem, lsem):
        x, y, c, _ = _place()
        me = 4 * x + 2 * y + c
        local = pltpu.make_async_copy(in_ref, out_ref.at[pl.ds(me, 1)], lsem)
        local.start()

        def copy(k, block):
            fx, fy, fc = (k >> 2) & 1, (k >> 1) & 1, k & 1
            peer = (x ^ fx, y ^ fy, c ^ fc)
            return pltpu.make_async_remote_copy(
                src_ref=in_ref, dst_ref=out_ref.at[pl.ds(block, 1)], send_sem=ssem.at[k - 1],
                recv_sem=rsem.at[k - 1], device_id=peer, device_id_type=MESH)

        sends = [copy(k, me) for k in range(1, 8)]
        for cp in sends:
            cp.start()
        for k in range(1, 8):
            copy(k, me ^ k).wait_recv()
        for cp in sends:
            cp.wait_send()
        local.wait()

    hbm = pl.BlockSpec(memory_space=pl.ANY)
    return pl.pallas_call(
        body, name="all_gather_small", in_specs=[hbm], out_specs=hbm,
        out_shape=jax.ShapeDtypeStruct((8, R, W), buf.dtype),
        scratch_shapes=[pltpu.SemaphoreType.DMA((7,)), pltpu.SemaphoreType.DMA((7,)), pltpu.SemaphoreType.DMA],
    )(buf)


def _row_tile(n, cap=1024):
    for t in range(min(n, cap) // 8 * 8, 7, -8):
        if n % t == 0:
            return t
    return n


def _sum4(own, recv):
    n, rows, D = own.shape
    tr = _row_tile(rows)

    def body(o_ref, r0_ref, r1_ref, r2_ref, out_ref):
        out_ref[...] = ((o_ref[...].astype(F32) + r0_ref[...].astype(F32)) + r1_ref[...].astype(F32)) \
            + r2_ref[...].astype(F32)

    def rspec(p):
        return pl.BlockSpec((None, None, None, tr, D), lambda k, i, p=p: (p, k, 0, i, 0))

    blk = pl.BlockSpec((None, tr, D), lambda k, i: (k, i, 0))
    return pl.pallas_call(
        body, name="sum4", grid=(n, rows // tr),
        in_specs=[blk, rspec(0), rspec(1), rspec(2)], out_specs=blk,
        out_shape=jax.ShapeDtypeStruct((n, rows, D), F32),
    )(own, recv, recv, recv)


def _add2(a, b):
    rows, D = a.shape
    tr = _row_tile(rows)

    def body(a_ref, b_ref, o_ref):
        o_ref[...] = a_ref[...] + b_ref[...]

    blk = pl.BlockSpec((tr, D), lambda i: (i, 0))
    return pl.pallas_call(body, name="add2", grid=(rows // tr,), in_specs=[blk, blk], out_specs=blk,
                          out_shape=jax.ShapeDtypeStruct((rows, D), F32))(a, b)


def _adam_math(w, g, m, v):
    m = ADAM_B1 * m + (1.0 - ADAM_B1) * g
    v = ADAM_B2 * v + (1.0 - ADAM_B2) * (g * g)
    m_hat = m / (1.0 - ADAM_B1 ** ADAM_STEP)
    v_hat = v / (1.0 - ADAM_B2 ** ADAM_STEP)
    delta = -ADAM_LR * (m_hat / (jnp.sqrt(v_hat) + ADAM_EPS) + ADAM_WD * w)
    return delta, m, v


def _adam(w, g, m, v):
    rows, cols = w.shape
    tr = _row_tile(rows, 512)

    def body(w_ref, g_ref, m_ref, v_ref, d_ref, nm_ref, nv_ref):
        d, m_, v_ = _adam_math(w_ref[...], g_ref[...], m_ref[...], v_ref[...])
        d_ref[...] = d
        nm_ref[...] = m_
        nv_ref[...] = v_

    blk = pl.BlockSpec((tr, cols), lambda i: (i, 0))
    return pl.pallas_call(body, name="adam", grid=(rows // tr,), in_specs=[blk] * 4, out_specs=[blk] * 3,
                          out_shape=[jax.ShapeDtypeStruct((rows, cols), F32)] * 3)(w, g, m, v)


def _small_sum_adam(g8, w, m, v):
    _, R, W = g8.shape

    def body(g_ref, w_ref, m_ref, v_ref, go_ref, d_ref, nm_ref, nv_ref):
        g = g_ref[0]
        for k in range(1, 8):
            g = g + g_ref[k]
        go_ref[...] = g
        d, m_, v_ = _adam_math(w_ref[...], g, m_ref[...], v_ref[...])
        d_ref[...] = d
        nm_ref[...] = m_
        nv_ref[...] = v_

    return pl.pallas_call(body, name="small_sum_adam",
                          out_shape=[jax.ShapeDtypeStruct((R, W), F32)] * 4)(g8, w, m, v)


def _pad_heads_rows(w, first):
    lead, D = w.shape[:-2], w.shape[-1]
    heads = w[..., first:, :]
    n = heads.shape[-2] // HEAD_DIM
    heads = heads.reshape(lead + (n, HEAD_DIM, D))
    heads = jnp.pad(heads, [(0, 0)] * (len(lead) + 1) + [(0, SLOT - HEAD_DIM), (0, 0)])
    return jnp.concatenate([w[..., :first, :], heads.reshape(lead + (n * SLOT, D))], axis=-2)


def _unpad_heads_rows(w, first):
    lead, D = w.shape[:-2], w.shape[-1]
    heads = w[..., first:, :]
    n = heads.shape[-2] // SLOT
    heads = heads.reshape(lead + (n, SLOT, D))[..., :HEAD_DIM, :]
    return jnp.concatenate([w[..., :first, :], heads.reshape(lead + (n * HEAD_DIM, D))], axis=-2)


def _pad_vec(v):
    return jnp.pad(v, (0, SLOT - v.shape[0]))[None, :]


class _Pack:
    def __init__(self, shapes):
        self.shapes = shapes
        self.sizes = [int(functools.reduce(lambda a, b: a * b, s, 1)) for s in shapes]
        total = sum(self.sizes)
        self.rows = -(-total // (8 * SLOT)) * 8
        self.pad = self.rows * SLOT - total

    def pack(self, arrs):
        flat = jnp.concatenate([a.reshape(-1).astype(F32) for a in arrs] + [jnp.zeros((self.pad,), F32)])
        return flat.reshape(self.rows, SLOT)

    def unpack(self, buf):
        flat, out, o = buf.reshape(-1), [], 0
        for s, n in zip(self.shapes, self.sizes):
            out.append(flat[o:o + n].reshape(s))
            o += n
        return out


def kernel(x, mem, positions, ffn1_norm, ffn1_w1, ffn1_w3, ffn1_w2, mix_norm, w_in, conv_w, conv_b, conv_ln_g, conv_ln_b, swa_q_norm, swa_k_norm, swa_sinks, mem_norm, w_mem_kv, mem_q_norm, mem_k_norm, w_out, ffn2_norm, ffn2_w1, ffn2_w3, ffn2_w2, final_norm, loss_target, m_ffn1_norm, m_ffn1_w1, m_ffn1_w3, m_ffn1_w2, m_mix_norm, m_w_in, m_conv_w, m_conv_b, m_conv_ln_g, m_conv_ln_b, m_swa_q_norm, m_swa_k_norm, m_swa_sinks, m_mem_norm, m_w_mem_kv, m_mem_q_norm, m_mem_k_norm, m_w_out, m_ffn2_norm, m_ffn2_w1, m_ffn2_w3, m_ffn2_w2, m_final_norm, v_ffn1_norm, v_ffn1_w1, v_ffn1_w3, v_ffn1_w2, v_mix_norm, v_w_in, v_conv_w, v_conv_b, v_conv_ln_g, v_conv_ln_b, v_swa_q_norm, v_swa_k_norm, v_swa_sinks, v_mem_norm, v_w_mem_kv, v_mem_q_norm, v_mem_k_norm, v_w_out, v_ffn2_norm, v_ffn2_w1, v_ffn2_w3, v_ffn2_w2, v_final_norm):
    names = ['ffn1_norm', 'ffn1_w1', 'ffn1_w3', 'ffn1_w2', 'mix_norm', 'w_in', 'conv_w', 'conv_b', 'conv_ln_g',
             'conv_ln_b', 'swa_q_norm', 'swa_k_norm', 'swa_sinks', 'mem_norm', 'w_mem_kv', 'mem_q_norm',
             'mem_k_norm', 'w_out', 'ffn2_norm', 'ffn2_w1', 'ffn2_w3', 'ffn2_w2', 'final_norm']
    loc = locals()
    W = {n: loc[n] for n in names}
    M1 = {n: loc['m_' + n] for n in names}
    V1 = {n: loc['v_' + n] for n in names}

    S, D = x.shape[1], x.shape[2]
    L = ffn1_norm.shape[0]
    Fs = ffn1_w1.shape[2]
    F = 4 * Fs
    Mlen = mem.shape[1]
    cw_sh = conv_w.shape[2]
    tm = 512 if S >= 2048 else 256
    tf = 1408 if F % 1408 == 0 else 256
    tfw = 256
    tmw = 1024 if S >= 2048 else 256
    x0 = x[0]
    mem0 = mem[0]
    target = loss_target[0]
    my_chip = 2 * lax.axis_index("x") + lax.axis_index("y")

    mkv_rows = w_mem_kv.shape[1] * MEM_KV // D
    r_in, r_out = D_IN // 4, D_MIX // 4
    rm = r_in + r_out + mkv_rows

    def ffn_group(w1, w3, w2):
        return jnp.stack([w1.T, w3.T, w2]).astype(BF16).reshape(3, 1, Fs, D)

    groups = []
    for l in range(L):
        groups.append(ffn_group(ffn1_w1[l], ffn1_w3[l], ffn1_w2[l]))
        groups.append(jnp.concatenate([w_in[l].T, w_out[l], w_mem_kv[l].reshape(mkv_rows, D)])
                      .astype(BF16).reshape(1, 1, rm, D))
        groups.append(ffn_group(ffn2_w1[l], ffn2_w3[l], ffn2_w2[l]))
    gathered = [None] * len(groups)
    cw_rows = -(-(L * CONV_WIDTH) // 8) * 8
    cw_pad = jnp.pad(conv_w.reshape(L * CONV_WIDTH, cw_sh), ((0, cw_rows - L * CONV_WIDTH), (0, SLOT - cw_sh)))
    gathered[0], gathered[1], cw_g = _run_rider(
        _Gather([groups[0], groups[1], cw_pad.reshape(1, 1, cw_rows, SLOT)]), "all_gather_first")
    conv_wF = cw_g[0, :, :L * CONV_WIDTH, :cw_sh].reshape(4, L, CONV_WIDTH, cw_sh)
    conv_wF = jnp.moveaxis(conv_wF, 0, 2).reshape(L, CONV_WIDTH, 4 * cw_sh)
    conv_wP = jnp.pad(conv_wF, ((0, 0), (0, 32 - CONV_WIDTH), (0, 0)))

    def gather_rider(j):
        return _Gather([groups[j + 2]]) if j + 2 < len(groups) else None

    def mix_weights(l):
        g = gathered[3 * l + 1][0]
        w_inp = _pad_heads_rows(g[:, :r_in].reshape(D_IN, D), 2 * CONV_CH)
        w_outp = _pad_heads_rows(g[:, r_in:r_in + r_out].reshape(D_MIX, D), CONV_CH)
        w_mkvp = jnp.pad(g[:, r_in + r_out:].reshape(D, 2 * N_MEMH, HEAD_DIM),
                         ((0, 0), (0, 0), (0, SLOT - HEAD_DIM))).reshape(D, 2 * N_MEMH * SLOT)
        return w_inp, w_outp, w_mkvp

    inv_freq = ROPE_THETA ** (-jnp.arange(0, HEAD_DIM, 2, dtype=F32) / HEAD_DIM)
    invf = jnp.concatenate([inv_freq, inv_freq, jnp.zeros((SLOT - HEAD_DIM,), F32)])[None, :]
    cosT, sinT = _rope_tables(positions.reshape(S, 1), invf, tm)

    row = lambda a, l: a[l][None, :]
    sinks_p = jnp.pad(swa_sinks, ((0, 0), (0, 8 - N_Q)))

    saved = []
    xin = x0
    xn = None
    for l in range(L):
        wf1 = gathered[3 * l].reshape(3, F, D)
        (h1, a1, b1, t1), got = _ffn_fwd(xin, row(ffn1_norm, l), wf1, None, tm, tf, rider=gather_rider(3 * l))
        if got:
            gathered[3 * l + 2] = got[0]
        w_inp, w_outp, w_mkvp = mix_weights(l)
        pu, p, n2 = _proj_fwd(h1, row(mix_norm, l), w_inp, tm)
        gk_m = _pad_vec(mem_k_norm[l])
        nm, mraw, mk, mv = _mem_kv_fwd(mem0, row(mem_norm, l), w_mkvp, gk_m)
        gq, gk, gqm = _pad_vec(swa_q_norm[l]), _pad_vec(swa_k_norm[l]), _pad_vec(mem_q_norm[l])
        (h2, y, yc, lse), got = _mixer_fwd(pu, p, h1, cosT, sinT, conv_wP[l], row(conv_b, l), row(conv_ln_g, l),
                                           row(conv_ln_b, l), gq, gk, sinks_p[l], mk, mv, gqm, w_outp, tm,
                                           rider=gather_rider(3 * l + 1))
        if got:
            gathered[3 * l + 3] = got[0]
        wf2 = gathered[3 * l + 2].reshape(3, F, D)
        (h3, a2, b2, t2, xn), got = _ffn_fwd(h2, row(ffn2_norm, l), wf2, row(final_norm, l), tm, tf,
                                         rider=gather_rider(3 * l + 2))
        if got:
            gathered[3 * l + 4] = got[0]
        saved.append(dict(xin=xin, h1=h1, a1=a1, b1=b1, pu=pu, p=p, n2=n2, nm=nm, mraw=mraw, mk=mk, mv=mv, gk_m=gk_m,
                          gq=gq, gk=gk, gqm=gqm, h2=h2, y=y, yc=yc, lse=lse, h3=h3, a2=a2, b2=b2, t1=t1, t2=t2,
                          wf1=wf1, wf2=wf2, w_inp=w_inp, w_outp=w_outp, w_mkvp=w_mkvp))
        xin = xn

    G = {n: [None] * L for n in names}
    ffn_bufs = [None] * (2 * L)
    mix_bufs = [None] * L
    ffn_recv = [None] * (2 * L)
    mix_recv = [None] * L
    dxn = None
    loss_part = None
    for l in reversed(range(L)):
        sv = saved[l]
        if l == L - 1:
            loss_part, dh3, G['final_norm'][l] = _loss_bwd(xn, sv['h3'], row(final_norm, l), target, tm)
        else:
            dh3, G['final_norm'][l] = _norm_bwd(dxn, sv['h3'], row(final_norm, l), tm)
        rider = _Scatter([ffn_bufs[2 * l + 2]]) if l < L - 1 else None
        (dh2, G['ffn2_norm'][l], da, db, n, dy), got = _ffn_bwd_act(
            dh3, sv['h2'], row(ffn2_norm, l), sv['a2'], sv['b2'], sv['wf2'], tm, tf, rider=rider)
        if got:
            ffn_recv[2 * l + 2] = got[0]
        ffn_bufs[2 * l + 1] = _ffn_bwd_w(da, db, sv['t2'], n, dy, tmw, tfw).reshape(3, 4, Fs, D)
        dyc, do, delta, dwo, G['conv_ln_g'][l], G['conv_ln_b'][l] = _outproj_bwd(
            dh2, sv['y'], sv['yc'], row(conv_ln_g, l), row(conv_ln_b, l), sv['w_outp'], tm)
        (dph, dgq, dgk, dgqm, dsink, dmk, dmv), got = _attn_bwd(
            sv['p'], do, sv['lse'], delta, cosT, sinT, sv['gq'], sv['gk'], sinks_p[l],
            sv['mk'], sv['mv'], sv['gqm'], tm, rider=_Scatter([ffn_bufs[2 * l + 1]]))
        ffn_recv[2 * l + 1] = got[0]
        dpu, dcw, G['conv_b'][l] = _conv_bwd(sv['pu'], dyc, conv_wP[l], tm)
        dwm, G['mem_norm'][l], dgk_m = _mem_kv_bwd(dmk, dmv, sv['mraw'], sv['nm'], mem0, row(mem_norm, l),
                                                   sv['w_mkvp'], sv['gk_m'])
        dh1, G['mix_norm'][l], dwi = _proj_bwd(dpu, dph, sv['h1'], dh2, row(mix_norm, l), sv['n2'], sv['w_inp'], tm)
        dwiT = _unpad_heads_rows(dwi, 2 * CONV_CH).reshape(4, r_in, D)
        dwoF = _unpad_heads_rows(dwo, CONV_CH).reshape(4, r_out, D)
        dwmF = dwm.reshape(D, 2 * N_MEMH, SLOT)[:, :, :HEAD_DIM].reshape(4, mkv_rows, D)
        mix_bufs[l] = jnp.concatenate([dwiT, dwoF, dwmF], axis=1).astype(BF16).reshape(1, 4, rm, D)
        (dxl, G['ffn1_norm'][l], da, db, n, dy), got = _ffn_bwd_act(
            dh1, sv['xin'], row(ffn1_norm, l), sv['a1'], sv['b1'], sv['wf1'], tm, tf,
            rider=_Scatter([mix_bufs[l]]))
        mix_recv[l] = got[0]
        ffn_bufs[2 * l] = _ffn_bwd_w(da, db, sv['t1'], n, dy, tmw, tfw).reshape(3, 4, Fs, D)
        dxn = dxl
        G['conv_w'][l] = dcw[:CONV_WIDTH]
        G['swa_q_norm'][l] = dgq[0, :HEAD_DIM]
        G['swa_k_norm'][l] = dgk[0, :HEAD_DIM]
        G['mem_q_norm'][l] = dgqm[0, :HEAD_DIM]
        G['mem_k_norm'][l] = dgk_m[0, :HEAD_DIM]
        G['swa_sinks'][l] = dsink[0, :N_Q]
    ffn_recv[0] = _run_rider(_Scatter([ffn_bufs[0]]), "scatter_last")[0]
    grad_x = dxn[None]
    loss = lax.psum(loss_part[0, 0], AXES)

    parts = []
    for b, r in zip(ffn_bufs + mix_bufs, ffn_recv + mix_recv):
        own = lax.dynamic_index_in_dim(b, my_chip, axis=1, keepdims=False)
        parts.append(_sum4(own, r))
    theirs = _swap_with_sibling(parts)
    gsum = [_add2(a.reshape(-1, D), b.reshape(-1, D)).reshape(a.shape) for a, b in zip(parts, theirs)]
    for l in range(L):
        for f, pre in enumerate(('ffn1', 'ffn2')):
            g3 = gsum[2 * l + f]
            G[pre + '_w1'][l] = g3[0].T
            G[pre + '_w3'][l] = g3[1].T
            G[pre + '_w2'][l] = g3[2]
        gm = gsum[2 * L + l][0]
        G['w_in'][l] = gm[:r_in].T
        G['w_out'][l] = gm[r_in:r_in + r_out]
        G['w_mem_kv'][l] = gm[r_in + r_out:].reshape(w_mem_kv.shape[1], MEM_KV)

    small = ['ffn1_norm', 'mix_norm', 'conv_b', 'conv_ln_g', 'conv_ln_b', 'swa_q_norm', 'swa_k_norm', 'swa_sinks',
             'mem_norm', 'mem_q_norm', 'mem_k_norm', 'ffn2_norm', 'final_norm']
    gsmall = [jnp.stack([G[n][l].reshape(-1) for l in range(L)]) for n in small]
    gcw = jnp.stack(G['conv_w'])
    cw_cols = 4 * cw_sh
    full_of = lambda a: lax.dynamic_update_slice(jnp.zeros((L, CONV_WIDTH, cw_cols), F32), a, (0, 0, my_chip * cw_sh))
    pk = _Pack([W[n].shape for n in small] + [(L, CONV_WIDTH, cw_cols)])
    g8 = _all_gather_small(pk.pack(gsmall + [gcw])[None])
    outs4 = _small_sum_adam(g8, pk.pack([W[n] for n in small] + [full_of(conv_w)]),
                            pk.pack([M1[n] for n in small] + [full_of(m_conv_w)]),
                            pk.pack([V1[n] for n in small] + [full_of(v_conv_w)]))
    un = [pk.unpack(o) for o in outs4]
    grads, deltas, new_m, new_v = {}, {}, {}, {}
    for k, n in enumerate(small):
        grads[n], deltas[n], new_m[n], new_v[n] = un[0][k], un[1][k], un[2][k], un[3][k]
    mine = lambda a: lax.dynamic_slice(a, (0, 0, my_chip * cw_sh), (L, CONV_WIDTH, cw_sh))
    grads['conv_w'], deltas['conv_w'], new_m['conv_w'], new_v['conv_w'] = [mine(u[-1]) for u in un]

    for n in ('ffn1_w1', 'ffn1_w3', 'ffn1_w2', 'w_in', 'w_mem_kv', 'w_out', 'ffn2_w1', 'ffn2_w3', 'ffn2_w2'):
        g = jnp.stack(G[n])
        shp = W[n].shape
        v2 = lambda a: a.reshape(-1, shp[-1])
        d_, m_, v_ = _adam(v2(W[n]), v2(g), v2(M1[n]), v2(V1[n]))
        grads[n], deltas[n], new_m[n], new_v[n] = g, d_.reshape(shp), m_.reshape(shp), v_.reshape(shp)

    return (loss, grad_x, *[grads[n] for n in names], *[deltas[n] for n in names],
            *[new_m[n] for n in names], *[new_v[n] for n in names])
```

```python
import functools

import jax
import jax.numpy as jnp
from jax import lax
from jax.experimental import pallas as pl
from jax.experimental.pallas import tpu as pltpu

F32 = jnp.float32
BF16 = jnp.bfloat16
MESH = pl.DeviceIdType.MESH
AXES = ("x", "y", "c")

EPS = 1e-6
HEAD_DIM = 64
SLOT = 128
CONV_CH = 384
CONV_WIDTH = 31
N_Q, N_KV, N_MEMH = 6, 2, 4
GROUP = N_Q // N_KV
BLK = 128
HALO = 32
CONV_ROWS = 64
ROPE_THETA = 10000.0
SCALE = HEAD_DIM ** -0.5
NEG = -1e30

N_HEADS_IN = N_Q + 2 * N_KV + N_MEMH
PU = 2 * CONV_CH
PH = HEAD_DIM * N_HEADS_IN
PP = PU + PH
QO = 0
KO = QO + HEAD_DIM * N_Q
VO = KO + HEAD_DIM * N_KV
MO = VO + HEAD_DIM * N_KV
NH = N_Q + N_MEMH
YH = HEAD_DIM * NH
YP = CONV_CH + YH
YS = CONV_CH
YM = YS + HEAD_DIM * N_Q
D_IN = PP
D_MIX = YP
MEM_KV = 2 * HEAD_DIM * N_MEMH

ADAM_LR, ADAM_B1, ADAM_B2, ADAM_EPS, ADAM_WD, ADAM_STEP = 0.001, 0.9, 0.999, 1e-08, 0.01, 10

VMEM_LIMIT_MB = 56


def _cp(mb=VMEM_LIMIT_MB):
    return pltpu.CompilerParams(vmem_limit_bytes=mb * 1024 * 1024)


def _dot_nn(a, b):
    return lax.dot_general(a, b, (((1,), (0,)), ((), ())), preferred_element_type=F32)


def _dot_nt(a, b):
    return lax.dot_general(a, b, (((1,), (1,)), ((), ())), preferred_element_type=F32)


def _dot_tn(a, b):
    return lax.dot_general(a, b, (((0,), (0,)), ((), ())), preferred_element_type=F32)


def _sigmoid(x):
    return 1.0 / (1.0 + jnp.exp(-x))


def _rms(x):
    return lax.rsqrt(jnp.mean(x * x, axis=-1, keepdims=True) + EPS)


def _rms_bwd(dn, x, r, g):
    xhat = x * r
    dxhat = dn * g
    dx = r * (dxhat - xhat * jnp.mean(dxhat * xhat, axis=-1, keepdims=True))
    return dx, dn * xhat


def _colsum(v):
    return jnp.sum(v, axis=0, keepdims=True)


def _lane(n):
    return lax.broadcasted_iota(jnp.int32, (n, SLOT), 1)


def _partner(v, lane):
    up = pltpu.roll(v, SLOT - HEAD_DIM // 2, 1)
    dn = pltpu.roll(v, HEAD_DIM // 2, 1)
    return jnp.where(lane < HEAD_DIM // 2, up, jnp.where(lane < HEAD_DIM, dn, 0.0))


def _head_rms(xs):
    return lax.rsqrt(jnp.sum(xs * xs, axis=-1, keepdims=True) * (1.0 / HEAD_DIM) + EPS)


def _head_fwd(xs, g, cosv, sinv, lane):
    xn = xs * _head_rms(xs) * g
    if cosv is None:
        return xn
    return xn * cosv + _partner(xn, lane) * sinv


def _head_bwd(dout, xs, g, cosv, sinv, lane):
    if cosv is not None:
        dout = dout * cosv + _partner(dout * sinv, lane)
    r = _head_rms(xs)
    xhat = xs * r
    dxhat = dout * g
    dx = r * (dxhat - xhat * (jnp.sum(dxhat * xhat, axis=-1, keepdims=True) * (1.0 / HEAD_DIM)))
    return dx, dout * xhat


def _col(v, h, lane):
    return jnp.sum(jnp.where(lane == h, v, 0.0), axis=-1, keepdims=True)


def _halves(v, lane):
    lo = jnp.sum(jnp.where(lane < HEAD_DIM, v, 0.0), axis=-1, keepdims=True)
    hi = jnp.sum(jnp.where(lane < HEAD_DIM, 0.0, v), axis=-1, keepdims=True)
    return jnp.where(lane < HEAD_DIM, lo, hi)


def _pair_rms(x, lane):
    return lax.rsqrt(_halves(x * x, lane) * (1.0 / HEAD_DIM) + EPS)


def _pair_partner(v, lane):
    return jnp.where((lane & (HEAD_DIM - 1)) < HEAD_DIM // 2,
                     pltpu.roll(v, SLOT - HEAD_DIM // 2, 1), pltpu.roll(v, HEAD_DIM // 2, 1))


def _pair_fwd(x, g2, cosv, sinv, lane):
    xn = x * _pair_rms(x, lane) * g2
    if cosv is None:
        return xn
    return xn * cosv + _pair_partner(xn, lane) * sinv


def _pair_bwd(dout, x, g2, cosv, sinv, lane):
    if cosv is not None:
        dout = dout * cosv + _pair_partner(dout * sinv, lane)
    r = _pair_rms(x, lane)
    xhat = x * r
    dxhat = dout * g2
    dx = r * (dxhat - xhat * (_halves(dxhat * xhat, lane) * (1.0 / HEAD_DIM)))
    return dx, dout * xhat


def _lo(x, half, lane):
    if half:
        x = pltpu.roll(x, HEAD_DIM, 1)
    return jnp.where(lane < HEAD_DIM, x, 0.0)


def _pack(even, odd, lane):
    return jnp.where(lane < HEAD_DIM, even, pltpu.roll(odd, HEAD_DIM, 1))


def _place():
    x, y, c = lax.axis_index("x"), lax.axis_index("y"), lax.axis_index("c")
    chips = [(1 - x, y), (x, 1 - y), (1 - x, 1 - y)]
    return x, y, c, chips


class _Gather:
    tag = "_gather"

    def __init__(self, bufs):
        self.bufs = list(bufs)
        nb = len(self.bufs)
        self.out_shape = [jax.ShapeDtypeStruct((b.shape[0], 4) + b.shape[2:], b.dtype) for b in self.bufs]
        self.sems = [pltpu.SemaphoreType.DMA((3 * nb,)), pltpu.SemaphoreType.DMA((3 * nb,)),
                     pltpu.SemaphoreType.DMA((nb,))]

    def _copies(self, ins, outs, sems):
        ssem, rsem, lsem = sems
        nb = len(self.bufs)
        x, y, c, chips = _place()
        mine = 2 * x + y

        def copy(b, p, shard):
            return pltpu.make_async_remote_copy(
                src_ref=ins[b], dst_ref=outs[b].at[:, pl.ds(shard, 1)],
                send_sem=ssem.at[3 * b + p], recv_sem=rsem.at[3 * b + p],
                device_id=(chips[p][0], chips[p][1], c), device_id_type=MESH)

        local = [pltpu.make_async_copy(ins[b], outs[b].at[:, pl.ds(mine, 1)], lsem.at[b]) for b in range(nb)]
        sends = [copy(b, p, mine) for b in range(nb) for p in range(3)]
        recvs = [copy(b, p, 2 * chips[p][0] + chips[p][1]) for b in range(nb) for p in range(3)]
        return local, sends, recvs

    def start(self, ins, outs, sems):
        local, sends, _ = self._copies(ins, outs, sems)
        for cp in local + sends:
            cp.start()

    def wait(self, ins, outs, sems):
        local, sends, recvs = self._copies(ins, outs, sems)
        for cp in recvs:
            cp.wait_recv()
        for cp in sends:
            cp.wait_send()
        for cp in local:
            cp.wait()


class _Scatter:
    tag = "_scatter"

    def __init__(self, bufs):
        self.bufs = list(bufs)
        nb = len(self.bufs)
        self.out_shape = [jax.ShapeDtypeStruct((3, b.shape[0], 1) + b.shape[2:], b.dtype) for b in self.bufs]
        self.sems = [pltpu.SemaphoreType.DMA((3 * nb,)), pltpu.SemaphoreType.DMA((3 * nb,))]

    def _copies(self, ins, outs, sems):
        ssem, rsem = sems
        x, y, c, chips = _place()

        def copy(b, p):
            shard = 2 * chips[p][0] + chips[p][1]
            return pltpu.make_async_remote_copy(
                src_ref=ins[b].at[:, pl.ds(shard, 1)], dst_ref=outs[b].at[p],
                send_sem=ssem.at[3 * b + p], recv_sem=rsem.at[3 * b + p],
                device_id=(chips[p][0], chips[p][1], c), device_id_type=MESH)

        return [copy(b, p) for b in range(len(self.bufs)) for p in range(3)]

    def start(self, ins, outs, sems):
        for cp in self._copies(ins, outs, sems):
            cp.start()

    def wait(self, ins, outs, sems):
        cps = self._copies(ins, outs, sems)
        for cp in cps:
            cp.wait_recv()
        for cp in cps:
            cp.wait_send()


def _run_rider(rider, name):
    nb = len(rider.bufs)

    def body(*refs):
        ins, outs, sems = refs[:nb], refs[nb:2 * nb], refs[2 * nb:]
        rider.start(ins, outs, sems)
        rider.wait(ins, outs, sems)

    hbm = pl.BlockSpec(memory_space=pl.ANY)
    return pl.pallas_call(body, name=name, in_specs=[hbm] * nb, out_specs=[hbm] * nb,
                          out_shape=rider.out_shape, scratch_shapes=rider.sems)(*rider.bufs)


def _call(body, *, name, grid, in_specs, out_specs, out_shape, args, scratch=(), rider=None):
    if rider is None:
        outs = pl.pallas_call(body, name=name, grid=grid, in_specs=list(in_specs), out_specs=list(out_specs),
                              out_shape=list(out_shape), scratch_shapes=list(scratch),
                              compiler_params=_cp())(*args)
        return list(outs), None
    n_in, n_out, n_scr, nb = len(in_specs), len(out_specs), len(scratch), len(rider.bufs)

    def wrapped(*refs):
        cuts = [n_in, nb, n_out, nb, n_scr]
        parts, o = [], 0
        for n in cuts:
            parts.append(refs[o:o + n])
            o += n
        ins, rin, outs, rout, scr = parts
        sems = refs[o:]
        ids = [pl.program_id(k) for k in range(len(grid))]
        first = functools.reduce(jnp.logical_and, [i == 0 for i in ids])
        last = functools.reduce(jnp.logical_and, [i == n - 1 for i, n in zip(ids, grid)])

        @pl.when(first)
        def _():
            rider.start(rin, rout, sems)

        body(*ins, *outs, *scr)

        @pl.when(last)
        def _():
            rider.wait(rin, rout, sems)

    hbm = pl.BlockSpec(memory_space=pl.ANY)
    res = pl.pallas_call(
        wrapped, name=name + rider.tag, grid=grid,
        in_specs=list(in_specs) + [hbm] * nb, out_specs=list(out_specs) + [hbm] * nb,
        out_shape=list(out_shape) + rider.out_shape, scratch_shapes=list(scratch) + rider.sems,
        compiler_params=_cp())(*args, *rider.bufs)
    return list(res[:n_out]), list(res[n_out:])


def _rope_tables(pos, invf, tm):
    S = pos.shape[0]

    def body(pos_ref, f_ref, cos_ref, sin_ref):
        ang = pos_ref[...].astype(F32) * f_ref[...]
        lane = _lane(tm)
        cos_ref[...] = jnp.cos(ang)
        s = jnp.sin(ang)
        sin_ref[...] = jnp.where((lane & (HEAD_DIM - 1)) < HEAD_DIM // 2, -s, s)

    return pl.pallas_call(
        body, name="rope_tables", grid=(S // tm,),
        in_specs=[pl.BlockSpec((tm, 1), lambda i: (i, 0)), pl.BlockSpec((1, SLOT), lambda i: (0, 0))],
        out_specs=[pl.BlockSpec((tm, SLOT), lambda i: (i, 0))] * 2,
        out_shape=[jax.ShapeDtypeStruct((S, SLOT), F32)] * 2,
    )(pos, invf)


def _ffn_fwd(x, g, wf, gfin, tm, tf, rider=None):
    S, D = x.shape
    F = wf.shape[1]
    nf = F // tf
    final = gfin is not None

    chunks = [(c, min(256, tf - c)) for c in range(0, tf, 256)]

    def body(*refs):
        if final:
            x_ref, g_ref, w1_ref, w3_ref, w2_ref, gf_ref, h_ref, a_ref, b_ref, t_ref, xn_ref, n_scr, acc = refs
        else:
            x_ref, g_ref, w1_ref, w3_ref, w2_ref, h_ref, a_ref, b_ref, t_ref, n_scr, acc = refs
        j = pl.program_id(1)

        @pl.when(j == 0)
        def _():
            xv = x_ref[...]
            n_scr[...] = (xv * _rms(xv) * g_ref[...]).astype(BF16)
            acc[...] = jnp.zeros_like(acc)

        n = n_scr[...]
        for c0, cw in chunks:
            cols = slice(c0, c0 + cw)
            a = _dot_nt(n, w1_ref[cols, :])
            b = _dot_nt(n, w3_ref[cols, :])
            a_ref[:, cols] = a.astype(BF16)
            b_ref[:, cols] = b.astype(BF16)
            t_ref[:, cols] = (a * _sigmoid(a) * b).astype(BF16)
        acc[...] += _dot_nn(t_ref[...], w2_ref[...])

        @pl.when(j == nf - 1)
        def _():
            h = x_ref[...] + 0.5 * acc[...]
            h_ref[...] = h
            if final:
                xn_ref[...] = h * _rms(h) * gf_ref[...]

    def wspec(k):
        return pl.BlockSpec((None, tf, D), lambda i, j, k=k: (k, j, 0))

    row = pl.BlockSpec((tm, D), lambda i, j: (i, 0))
    vec = pl.BlockSpec((1, D), lambda i, j: (0, 0))
    act = pl.BlockSpec((tm, tf), lambda i, j: (i, j))
    in_specs = [row, vec, wspec(0), wspec(1), wspec(2)] + ([vec] if final else [])
    out_specs = [row, act, act, act] + ([row] if final else [])
    out_shape = [jax.ShapeDtypeStruct((S, D), F32)] + [jax.ShapeDtypeStruct((S, F), BF16)] * 3 \
        + ([jax.ShapeDtypeStruct((S, D), F32)] if final else [])
    args = [x, g, wf, wf, wf] + ([gfin] if final else [])
    return _call(body, name="ffn_fwd_final" if final else "ffn_fwd", grid=(S // tm, nf),
                 in_specs=in_specs, out_specs=out_specs, out_shape=out_shape, args=args,
                 scratch=[pltpu.VMEM((tm, D), BF16), pltpu.VMEM((tm, D), F32)], rider=rider)


def _ffn_bwd_act(dh, x, g, a, b, wf, tm, tf, rider=None):
    S, D = x.shape
    F = wf.shape[1]
    nf = F // tf

    chunks = [(c, min(256, tf - c)) for c in range(0, tf, 256)]

    def body(dh_ref, x_ref, g_ref, a_ref, b_ref, w1_ref, w3_ref, w2_ref,
             dx_ref, dg_ref, da_ref, db_ref, n_ref, dy_ref, acc):
        i, j = pl.program_id(0), pl.program_id(1)

        @pl.when(j == 0)
        def _():
            xv = x_ref[...]
            n_ref[...] = (xv * _rms(xv) * g_ref[...]).astype(BF16)
            dy_ref[...] = (0.5 * dh_ref[...]).astype(BF16)
            acc[...] = jnp.zeros_like(acc)

            @pl.when(i == 0)
            def _():
                dg_ref[...] = jnp.zeros_like(dg_ref)

        dyv = dy_ref[...]
        for c0, cw in chunks:
            cols = slice(c0, c0 + cw)
            av = a_ref[:, cols].astype(F32)
            bv = b_ref[:, cols].astype(F32)
            sg = _sigmoid(av)
            dt = _dot_nt(dyv, w2_ref[cols, :])
            db_ref[:, cols] = (dt * (av * sg)).astype(BF16)
            da_ref[:, cols] = (dt * bv * (sg * (1.0 + av * (1.0 - sg)))).astype(BF16)
        acc[...] += _dot_nn(da_ref[...], w1_ref[...]) + _dot_nn(db_ref[...], w3_ref[...])

        @pl.when(j == nf - 1)
        def _():
            xv = x_ref[...]
            dx, dgrow = _rms_bwd(acc[...], xv, _rms(xv), g_ref[...])
            dx_ref[...] = dh_ref[...] + dx
            dg_ref[...] += _colsum(dgrow)

    def wspec(k):
        return pl.BlockSpec((None, tf, D), lambda i, j, k=k: (k, j, 0))

    row = pl.BlockSpec((tm, D), lambda i, j: (i, 0))
    vec = pl.BlockSpec((1, D), lambda i, j: (0, 0))
    act = pl.BlockSpec((tm, tf), lambda i, j: (i, j))
    sd = lambda shp, dt: jax.ShapeDtypeStruct(shp, dt)
    return _call(body, name="ffn_bwd_act", grid=(S // tm, nf),
                 in_specs=[row, row, vec, act, act, wspec(0), wspec(1), wspec(2)],
                 out_specs=[row, vec, act, act, row, row],
                 out_shape=[sd((S, D), F32), sd((1, D), F32), sd((S, F), BF16), sd((S, F), BF16),
                            sd((S, D), BF16), sd((S, D), BF16)],
                 args=[dh, x, g, a, b, wf, wf, wf], scratch=[pltpu.VMEM((tm, D), F32)], rider=rider)


def _ffn_bwd_w(da, db, t, n, dy, tm, tf):
    S, F = da.shape
    D = n.shape[1]
    nt = S // tm

    def body(da_ref, db_ref, t_ref, n_ref, dy_ref, out_ref, acc):
        i = pl.program_id(1)

        @pl.when(i == 0)
        def _():
            acc[...] = jnp.zeros_like(acc)

        nv = n_ref[...]
        acc[0] += _dot_tn(da_ref[...], nv)
        acc[1] += _dot_tn(db_ref[...], nv)
        acc[2] += _dot_tn(t_ref[...], dy_ref[...])

        @pl.when(i == nt - 1)
        def _():
            out_ref[...] = acc[...].astype(BF16)

    act = pl.BlockSpec((tm, tf), lambda j, i: (i, j))
    row = pl.BlockSpec((tm, D), lambda j, i: (i, 0))
    return pl.pallas_call(
        body, name="ffn_bwd_w", grid=(F // tf, nt),
        in_specs=[act, act, act, row, row],
        out_specs=pl.BlockSpec((3, tf, D), lambda j, i: (0, j, 0)),
        out_shape=jax.ShapeDtypeStruct((3, F, D), BF16),
        scratch_shapes=[pltpu.VMEM((3, tf, D), F32)],
        compiler_params=_cp(),
    )(da, db, t, n, dy)


def _proj_fwd(h, g, w_inp, tm):
    S, D = h.shape

    def body(h_ref, g_ref, w_ref, pu_ref, ph_ref, n_ref):
        hv = h_ref[...]
        n = (hv * _rms(hv) * g_ref[...]).astype(BF16)
        n_ref[...] = n
        pu_ref[...] = _dot_nt(n, w_ref[0:PU, :])
        ph_ref[...] = _dot_nt(n, w_ref[PU:PP, :])

    cur = lambda w: pl.BlockSpec((tm, w), lambda i: (i, 0))
    return pl.pallas_call(
        body, name="proj_fwd", grid=(S // tm,),
        in_specs=[cur(D), pl.BlockSpec((1, D), lambda i: (0, 0)), pl.BlockSpec((PP, D), lambda i: (0, 0))],
        out_specs=[cur(PU), cur(PH), cur(D)],
        out_shape=[jax.ShapeDtypeStruct((S, PU), F32), jax.ShapeDtypeStruct((S, PH), F32),
                   jax.ShapeDtypeStruct((S, D), BF16)],
        compiler_params=_cp(),
    )(h, g, w_inp)


def _glu(u):
    return u[:, :CONV_CH] * _sigmoid(u[:, CONV_CH:2 * CONV_CH])


def _shifted_copies(ext8):
    n = ext8.shape[1]
    for b in range(1, 8):
        ext8[b, 0:n - 8, :] = ext8[0, b:b + n - 8, :]


def _window(ext8, off, rows, r0=0):
    return ext8[off % 8, pl.ds(r0 + (off - off % 8), rows), :]


def _layer_norm_stats(yc):
    mu = jnp.mean(yc, axis=-1, keepdims=True)
    d = yc - mu
    rstd = lax.rsqrt(jnp.mean(d * d, axis=-1, keepdims=True) + EPS)
    return d * rstd, rstd


def _mem_kv_fwd(mem, g, w_mkvp, gk):
    M, D = mem.shape
    W = SLOT * N_MEMH

    def body(mem_ref, g_ref, w_ref, gk_ref, nm_ref, raw_ref, mk_ref, mv_ref):
        mv_ = mem_ref[...]
        nm = (mv_ * _rms(mv_) * g_ref[...]).astype(BF16)
        nm_ref[...] = nm
        raw = _dot_nn(nm, w_ref[...])
        raw_ref[...] = raw
        for hh in range(N_MEMH):
            sl = slice(SLOT * hh, SLOT * (hh + 1))
            mk_ref[:, sl] = _head_fwd(raw[:, sl], gk_ref[...], None, None, None).astype(BF16)
        mv_ref[...] = raw[:, W:].astype(BF16)

    sd = jax.ShapeDtypeStruct
    return pl.pallas_call(
        body, name="mem_kv_fwd",
        out_shape=[sd((M, D), BF16), sd((M, 2 * W), F32), sd((M, W), BF16), sd((M, W), BF16)],
        compiler_params=_cp(),
    )(mem, g, w_mkvp, gk)


def _mem_kv_bwd(dmk, dmv, raw, nm, mem, g, w_mkvp, gk):
    M, D = mem.shape
    W = SLOT * N_MEMH

    def body(dmk_ref, dmv_ref, raw_ref, nm_ref, mem_ref, g_ref, w_ref, gk_ref, dw_ref, dg_ref, dgk_ref, draw):
        dgk = jnp.zeros((1, SLOT), F32)
        for hh in range(N_MEMH):
            sl = slice(SLOT * hh, SLOT * (hh + 1))
            dx, prod = _head_bwd(dmk_ref[:, sl], raw_ref[:, sl], gk_ref[...], None, None, None)
            draw[:, sl] = dx.astype(BF16)
            dgk = dgk + _colsum(prod)
        dgk_ref[...] = dgk
        draw[:, W:] = dmv_ref[...].astype(BF16)
        dr = draw[...]
        dw_ref[...] = _dot_tn(nm_ref[...], dr)
        dnm = _dot_nt(dr, w_ref[...])
        mv_ = mem_ref[...]
        dg_ref[...] = _colsum(dnm * (mv_ * _rms(mv_)))

    sd = jax.ShapeDtypeStruct
    return pl.pallas_call(
        body, name="mem_kv_bwd",
        out_shape=[sd((D, 2 * W), F32), sd((1, D), F32), sd((1, SLOT), F32)],
        scratch_shapes=[pltpu.VMEM((M, 2 * W), BF16)],
        compiler_params=_cp(),
    )(dmk, dmv, raw, nm, mem, g, w_mkvp, gk)


def _mixer_fwd(pu, ph, h, cosT, sinT, conv_w, conv_b, ln_g, ln_b, gq, gk, sinks, mk, mv, gqm, w_outp, tm, rider=None):
    S, D = h.shape
    M = mk.shape[0]
    nb = tm // BLK
    nblocks = S // BLK

    def body(pu_ref, pup_ref, p_ref, ph_ref, h_ref, cos_ref, cosh_ref, sin_ref, sinh_ref, cw_ref, cb_ref,
             lg_ref, lb_ref, gq_ref, gk_ref, sink_ref, mk_ref, mv_ref, gqm_ref, wo_ref,
             h2_ref, y_ref, yc_ref, lse_ref, ext, y_scr):
        i = pl.program_id(0)
        not_first = (i > 0).astype(F32)
        lane = _lane(tm)
        lane_e = _lane(tm + BLK)

        ext[0, 0:HALO, :] = _glu(pup_ref[...]) * not_first
        ext[0, HALO:HALO + tm, :] = _glu(pu_ref[...])
        _shifted_copies(ext)

        def rows_chunk(r, carry):
            r0 = pl.multiple_of(r * CONV_ROWS, CONV_ROWS)
            yc = jnp.zeros((CONV_ROWS, CONV_CH), F32) + cb_ref[...]
            for k in range(CONV_WIDTH):
                yc = yc + cw_ref[k:k + 1, :] * _window(ext, HALO - (CONV_WIDTH - 1) + k, CONV_ROWS, r0)
            yc_ref[pl.ds(r0, CONV_ROWS), :] = yc
            z, _ = _layer_norm_stats(yc)
            ln = z * lg_ref[...] + lb_ref[...]
            y_scr[pl.ds(r0, CONV_ROWS), 0:CONV_CH] = (ln * _sigmoid(ln)).astype(BF16)
            return carry

        lax.fori_loop(0, tm // CONV_ROWS, rows_chunk, 0)

        cos_e = jnp.concatenate([cosh_ref[...], cos_ref[...]], axis=0)
        sin_e = jnp.concatenate([sinh_ref[...], sin_ref[...]], axis=0)
        qi = lax.broadcasted_iota(jnp.int32, (BLK, 2 * BLK), 0)
        kj = lax.broadcasted_iota(jnp.int32, (BLK, 2 * BLK), 1)
        band = (kj > qi) & (kj <= qi + BLK)
        band0 = band & ((kj >= BLK) | (i > 0))
        lse = jnp.zeros((tm, SLOT), F32)
        k_pair = jnp.concatenate([ph_ref[:, KO:KO + SLOT], p_ref[:, KO:KO + SLOT]], axis=0)
        k_pair = _pair_fwd(k_pair, gk_ref[...], cos_e, sin_e, lane_e)
        v_pair = jnp.concatenate([ph_ref[:, VO:VO + SLOT], p_ref[:, VO:VO + SLOT]], axis=0)
        k_e = [_lo(k_pair, kvh, lane_e).astype(BF16) for kvh in range(N_KV)]
        v_e = [_lo(v_pair, kvh, lane_e).astype(BF16) for kvh in range(N_KV)]
        heads = []
        for hq in range(N_Q):
            kvh = hq // GROUP
            if hq % 2 == 0:
                q_pair = _pair_fwd(p_ref[:, QO + SLOT * (hq // 2):QO + SLOT * (hq // 2 + 1)], gq_ref[...],
                                   cos_ref[...], sin_ref[...], lane)
            q = _lo(q_pair, hq % 2, lane).astype(BF16)
            sink = sink_ref[hq]
            outs, lses = [], []
            for m in range(nb):
                rows = slice(BLK * m, BLK * (m + 1))
                win = slice(BLK * m, BLK * (m + 2))
                s = _dot_nt(q[rows], k_e[kvh][win]) * SCALE
                s = jnp.where(band0 if m == 0 else band, s, NEG)
                mx = jnp.maximum(jnp.max(s, axis=-1, keepdims=True), sink)
                e = jnp.exp(s - mx)
                den = jnp.sum(e, axis=-1, keepdims=True) + jnp.exp(sink - mx)
                prob = e / den
                outs.append(_dot_nn(prob.astype(BF16), v_e[kvh][win]))
                lses.append(mx + jnp.log(den))
            heads.append(jnp.concatenate(outs, axis=0))
            lse = jnp.where(lane == hq, jnp.concatenate(lses, axis=0), lse)
            if hq % 2 == 1:
                y_scr[:, YS + SLOT * (hq // 2):YS + SLOT * (hq // 2 + 1)] = _pack(heads[-2], heads[-1], lane).astype(BF16)

        heads = []
        for hm in range(N_MEMH):
            ms = slice(SLOT * hm, SLOT * (hm + 1))
            if hm % 2 == 0:
                qm_pair = _pair_fwd(p_ref[:, MO + SLOT * (hm // 2):MO + SLOT * (hm // 2 + 1)], gqm_ref[...],
                                    None, None, lane)
            s = _dot_nt(_lo(qm_pair, hm % 2, lane).astype(BF16), mk_ref[:, ms]) * SCALE
            mx = jnp.max(s, axis=-1, keepdims=True)
            e = jnp.exp(s - mx)
            den = jnp.sum(e, axis=-1, keepdims=True)
            heads.append(_dot_nn((e / den).astype(BF16), mv_ref[:, ms]))
            lse = jnp.where(lane == N_Q + hm, mx + jnp.log(den), lse)
            if hm % 2 == 1:
                y_scr[:, YM + SLOT * (hm // 2):YM + SLOT * (hm // 2 + 1)] = _pack(heads[-2], heads[-1], lane).astype(BF16)
        lse_ref[...] = lse

        yv = y_scr[...]
        y_ref[...] = yv
        h2_ref[...] = h_ref[...] + _dot_nn(yv, wo_ref[...])

    cur = lambda w: pl.BlockSpec((tm, w), lambda i: (i, 0))
    prev = lambda w: pl.BlockSpec((BLK, w), lambda i: (jnp.maximum(i * nb - 1, 0), 0))
    full = lambda a: pl.BlockSpec(a.shape, lambda i: (0,) * a.ndim)
    sd = jax.ShapeDtypeStruct
    prev32 = pl.BlockSpec((HALO, PU), lambda i: (jnp.maximum(i * (tm // HALO) - 1, 0), 0))
    return _call(
        body, name="mixer_fwd", grid=(S // tm,),
        in_specs=[cur(PU), prev32, cur(PH), prev(PH), cur(D), cur(SLOT), prev(SLOT), cur(SLOT), prev(SLOT),
                  full(conv_w), full(conv_b), full(ln_g), full(ln_b), full(gq), full(gk),
                  pl.BlockSpec(memory_space=pltpu.SMEM), full(mk), full(mv), full(gqm), full(w_outp)],
        out_specs=[cur(D), cur(YP), cur(CONV_CH), cur(SLOT)],
        out_shape=[sd((S, D), F32), sd((S, YP), BF16), sd((S, CONV_CH), F32), sd((S, SLOT), F32)],
        args=[pu, pu, ph, ph, h, cosT, cosT, sinT, sinT, conv_w, conv_b, ln_g, ln_b, gq, gk, sinks, mk, mv, gqm,
              w_outp],
        scratch=[pltpu.VMEM((8, tm + HALO, CONV_CH), F32), pltpu.VMEM((tm, YP), BF16)], rider=rider)


def _outproj_bwd(dh2, y, yc, ln_g, ln_b, w_outp, tm):
    S, D = dh2.shape

    def body(dh_ref, y_ref, yc_ref, lg_ref, lb_ref, wo_ref, dyc_ref, do_ref, del_ref, dwo_ref, dlg_ref, dlb_ref):
        i = pl.program_id(0)

        @pl.when(i == 0)
        def _():
            dwo_ref[...] = jnp.zeros_like(dwo_ref)
            dlg_ref[...] = jnp.zeros_like(dlg_ref)
            dlb_ref[...] = jnp.zeros_like(dlb_ref)

        dhb = dh_ref[...].astype(BF16)
        yv = y_ref[...]
        dy = _dot_nt(dhb, wo_ref[...])
        dwo_ref[...] += _dot_tn(yv, dhb)

        z, rstd = _layer_norm_stats(yc_ref[...])
        ln = z * lg_ref[...] + lb_ref[...]
        sg = _sigmoid(ln)
        dln = dy[:, 0:CONV_CH] * (sg * (1.0 + ln * (1.0 - sg)))
        dlg_ref[...] += _colsum(dln * z)
        dlb_ref[...] += _colsum(dln)
        dz = dln * lg_ref[...]
        dyc_ref[...] = rstd * (dz - jnp.mean(dz, axis=-1, keepdims=True)
                               - z * jnp.mean(dz * z, axis=-1, keepdims=True))
        do_ref[...] = dy[:, CONV_CH:].astype(BF16)

        lane = _lane(tm)
        delta = jnp.zeros((tm, SLOT), F32)
        for j in range(NH // 2):
            sl = slice(YS + SLOT * j, YS + SLOT * (j + 1))
            prod = dy[:, sl] * yv[:, sl].astype(F32)
            lo = jnp.sum(jnp.where(lane < HEAD_DIM, prod, 0.0), axis=-1, keepdims=True)
            hi = jnp.sum(jnp.where(lane < HEAD_DIM, 0.0, prod), axis=-1, keepdims=True)
            delta = jnp.where(lane == 2 * j, lo, jnp.where(lane == 2 * j + 1, hi, delta))
        del_ref[...] = delta

    cur = lambda w: pl.BlockSpec((tm, w), lambda i: (i, 0))
    full = lambda a: pl.BlockSpec(a.shape, lambda i: (0,) * a.ndim)
    sd = jax.ShapeDtypeStruct
    return pl.pallas_call(
        body, name="outproj_bwd", grid=(S // tm,),
        in_specs=[cur(D), cur(YP), cur(CONV_CH), full(ln_g), full(ln_b), full(w_outp)],
        out_specs=[cur(CONV_CH), cur(YH), cur(SLOT), pl.BlockSpec((YP, D), lambda i: (0, 0)),
                   pl.BlockSpec((1, CONV_CH), lambda i: (0, 0)), pl.BlockSpec((1, CONV_CH), lambda i: (0, 0))],
        out_shape=[sd((S, CONV_CH), F32), sd((S, YH), BF16), sd((S, SLOT), F32), sd((YP, D), F32),
                   sd((1, CONV_CH), F32), sd((1, CONV_CH), F32)],
        compiler_params=_cp(),
    )(dh2, y, yc, ln_g, ln_b, w_outp)


def _conv_bwd(pu, dyc, conv_w, tm):
    S = pu.shape[0]
    nt = S // tm
    nh = tm // HALO

    def body(pu_ref, pup_ref, dy_ref, dyn_ref, cw_ref, dpu_ref, dcw_ref, dcb_ref, ext, ext2, dcw8):
        i = pl.program_id(0)

        @pl.when(i == 0)
        def _():
            dcw8[...] = jnp.zeros_like(dcw8)
            dcb_ref[...] = jnp.zeros_like(dcb_ref)

        not_first = (i > 0).astype(F32)
        not_last = (i < nt - 1).astype(F32)
        ext[0, 0:HALO, :] = _glu(pup_ref[...]) * not_first
        ext[0, HALO:HALO + tm, :] = _glu(pu_ref[...])
        _shifted_copies(ext)
        ext2[0, 0:tm, :] = dy_ref[...]
        ext2[0, tm:tm + HALO, :] = dyn_ref[...] * not_last
        _shifted_copies(ext2)
        dcb_ref[...] += _colsum(dy_ref[...])

        def rows_chunk(r, carry):
            r0 = pl.multiple_of(r * CONV_ROWS, CONV_ROWS)
            dyc_ = dy_ref[pl.ds(r0, CONV_ROWS), :]
            dyg = jnp.zeros((CONV_ROWS, CONV_CH), F32)
            for k in range(CONV_WIDTH):
                prod = dyc_ * _window(ext, HALO - (CONV_WIDTH - 1) + k, CONV_ROWS, r0)
                dcw8[k] += jnp.sum(prod.reshape(CONV_ROWS // 8, 8, CONV_CH), axis=0)
                dyg = dyg + cw_ref[k:k + 1, :] * _window(ext2, CONV_WIDTH - 1 - k, CONV_ROWS, r0)
            u = pu_ref[pl.ds(r0, CONV_ROWS), :]
            a_, sg = u[:, :CONV_CH], _sigmoid(u[:, CONV_CH:])
            dpu_ref[pl.ds(r0, CONV_ROWS), 0:CONV_CH] = (dyg * sg).astype(BF16)
            dpu_ref[pl.ds(r0, CONV_ROWS), CONV_CH:PU] = (dyg * a_ * sg * (1.0 - sg)).astype(BF16)
            return carry

        lax.fori_loop(0, tm // CONV_ROWS, rows_chunk, 0)

        @pl.when(i == nt - 1)
        def _():
            dcw_ref[...] = jnp.sum(dcw8[...], axis=1)

    cur = lambda w: pl.BlockSpec((tm, w), lambda i: (i, 0))
    prev = lambda w: pl.BlockSpec((HALO, w), lambda i: (jnp.maximum(i * nh - 1, 0), 0))
    nxt = lambda w: pl.BlockSpec((HALO, w), lambda i: (jnp.minimum((i + 1) * nh, S // HALO - 1), 0))
    acc = lambda r, w: pl.BlockSpec((r, w), lambda i: (0, 0))
    sd = jax.ShapeDtypeStruct
    return pl.pallas_call(
        body, name="conv_bwd", grid=(nt,),
        in_specs=[cur(PU), prev(PU), cur(CONV_CH), nxt(CONV_CH), acc(32, CONV_CH)],
        out_specs=[cur(PU), acc(32, CONV_CH), acc(1, CONV_CH)],
        out_shape=[sd((S, PU), BF16), sd((32, CONV_CH), F32), sd((1, CONV_CH), F32)],
        scratch_shapes=[pltpu.VMEM((8, tm + HALO, CONV_CH), F32), pltpu.VMEM((8, tm + HALO, CONV_CH), F32),
                        pltpu.VMEM((32, 8, CONV_CH), F32)],
        compiler_params=_cp(),
    )(pu, pu, dyc, dyc, conv_w)


def _attn_bwd(p, do, lse, delta, cosT, sinT, gq, gk, sinks, mk, mv, gqm, tm, rider=None):
    S = p.shape[0]
    M = mk.shape[0]
    nb = tm // BLK
    nt = S // tm
    nblocks = S // BLK
    W = SLOT * N_MEMH

    def body(p_ref, pp_ref, pn_ref, dy_ref, dyn_ref, lse_ref, lsen_ref, del_ref, deln_ref,
             cos_ref, cosp_ref, cosn_ref, sin_ref, sinp_ref, sinn_ref,
             gq_ref, gk_ref, sink_ref, mk_ref, mv_ref, gqm_ref,
             dp_ref, dgq_ref, dgk_ref, dgqm_ref, dsink_ref, dmk_ref, dmv_ref):
        i = pl.program_id(0)

        @pl.when(i == 0)
        def _():
            for r in (dgq_ref, dgk_ref, dgqm_ref, dsink_ref, dmk_ref, dmv_ref):
                r[...] = jnp.zeros_like(r)

        lane = _lane(tm)
        lane_e = _lane(tm + BLK)

        cos_k = jnp.concatenate([cosp_ref[...], cos_ref[...]], axis=0)
        sin_k = jnp.concatenate([sinp_ref[...], sin_ref[...]], axis=0)
        cos_q = jnp.concatenate([cos_ref[...], cosn_ref[...]], axis=0)
        sin_q = jnp.concatenate([sin_ref[...], sinn_ref[...]], axis=0)
        lse_e = jnp.concatenate([lse_ref[...], lsen_ref[...]], axis=0)
        del_e = jnp.concatenate([del_ref[...], deln_ref[...]], axis=0)
        qi = lax.broadcasted_iota(jnp.int32, (BLK, BLK), 0)
        kj = lax.broadcasted_iota(jnp.int32, (BLK, BLK), 1)
        diag = kj <= qi
        offd = kj > qi
        dgq = jnp.zeros((1, SLOT), F32)
        dgk = jnp.zeros((1, SLOT), F32)
        dsink = jnp.zeros((1, SLOT), F32)
        lane1 = lax.broadcasted_iota(jnp.int32, (1, SLOT), 1)
        k_pair = jnp.concatenate([pp_ref[:, KO:KO + SLOT], p_ref[:, KO:KO + SLOT]], axis=0)
        k_pair = _pair_fwd(k_pair, gk_ref[...], cos_k, sin_k, lane_e)
        v_pair = jnp.concatenate([pp_ref[:, VO:VO + SLOT], p_ref[:, VO:VO + SLOT]], axis=0)
        k_e = [_lo(k_pair, kvh, lane_e).astype(BF16) for kvh in range(N_KV)]
        v_e = [_lo(v_pair, kvh, lane_e).astype(BF16) for kvh in range(N_KV)]
        dk = [[jnp.zeros((BLK, SLOT), F32) for _ in range(nb)] for _ in range(N_KV)]
        dv = [[jnp.zeros((BLK, SLOT), F32) for _ in range(nb)] for _ in range(N_KV)]
        dq_heads = []
        for hq in range(N_Q):
            kvh = hq // GROUP
            js = slice(SLOT * (hq // 2), SLOT * (hq // 2 + 1))
            if hq % 2 == 0:
                q_pair = _pair_fwd(jnp.concatenate([p_ref[:, js], pn_ref[:, js]], axis=0), gq_ref[...],
                                   cos_q, sin_q, lane_e)
                do_pair = jnp.concatenate([dy_ref[:, js], dyn_ref[:, js]], axis=0).astype(F32)
            q_e = _lo(q_pair, hq % 2, lane_e).astype(BF16)
            do_e = _lo(do_pair, hq % 2, lane_e).astype(BF16)
            lse_h = _col(lse_e, hq, lane_e)
            del_h = _col(del_e, hq, lane_e)
            sink = sink_ref[hq]
            dq = [jnp.zeros((BLK, SLOT), F32) for _ in range(nb)]
            for m in range(nb + 1):
                rows = slice(BLK * m, BLK * (m + 1))
                qb, dob, lb, db_ = q_e[rows], do_e[rows], lse_h[rows], del_h[rows]
                for n in (m - 1, m):
                    if n == nb:
                        continue
                    krows = slice(BLK * (n + 1), BLK * (n + 2))
                    kb, vb = k_e[kvh][krows], v_e[kvh][krows]
                    s = _dot_nt(qb, kb) * SCALE
                    mask = diag if n == m else offd
                    if n == -1:
                        mask = mask & (i > 0)
                    if m == nb:
                        mask = mask & (i < nt - 1)
                    prob = jnp.where(mask, jnp.exp(jnp.where(mask, s - lb, NEG)), 0.0)
                    dpb = _dot_nt(dob, vb)
                    ds = (prob * (dpb - db_) * SCALE).astype(BF16)
                    if m < nb:
                        dq[m] = dq[m] + _dot_nn(ds, kb)
                    if n >= 0:
                        dk[kvh][n] = dk[kvh][n] + _dot_tn(ds, qb)
                        dv[kvh][n] = dv[kvh][n] + _dot_tn(prob.astype(BF16), dob)
            dq_heads.append(jnp.concatenate(dq, axis=0))
            psink = jnp.exp(sink - lse_h[0:tm])
            dsink = dsink + jnp.where(lane1 == hq, -_colsum(psink * del_h[0:tm]), 0.0)
            if hq % 2 == 1:
                dqr, prod = _pair_bwd(_pack(dq_heads[-2], dq_heads[-1], lane), p_ref[:, js], gq_ref[...],
                                      cos_ref[...], sin_ref[...], lane)
                dp_ref[:, js] = dqr.astype(BF16)
                dgq = dgq + _colsum(prod)
        dk_pair = _pack(jnp.concatenate(dk[0], axis=0), jnp.concatenate(dk[1], axis=0), lane)
        dkr, prod = _pair_bwd(dk_pair, p_ref[:, KO:KO + SLOT], gk_ref[...], cos_ref[...], sin_ref[...], lane)
        dp_ref[:, KO:KO + SLOT] = dkr.astype(BF16)
        dp_ref[:, VO:VO + SLOT] = _pack(jnp.concatenate(dv[0], axis=0), jnp.concatenate(dv[1], axis=0),
                                        lane).astype(BF16)
        dgq_ref[...] += dgq
        dgk_ref[...] += _colsum(prod)
        dsink_ref[...] += dsink

        dgqm = jnp.zeros((1, SLOT), F32)
        dq_heads = []
        for hm in range(N_MEMH):
            ms = slice(SLOT * hm, SLOT * (hm + 1))
            js = slice(MO + SLOT * (hm // 2), MO + SLOT * (hm // 2 + 1))
            os_ = slice(SLOT * ((N_Q + hm) // 2), SLOT * ((N_Q + hm) // 2 + 1))
            if hm % 2 == 0:
                qm_pair = _pair_fwd(p_ref[:, js], gqm_ref[...], None, None, lane)
                do_pair = dy_ref[:, os_].astype(F32)
            qm = _lo(qm_pair, hm % 2, lane).astype(BF16)
            dob = _lo(do_pair, hm % 2, lane).astype(BF16)
            kb, vb = mk_ref[:, ms], mv_ref[:, ms]
            s = _dot_nt(qm, kb) * SCALE
            prob = jnp.exp(s - _col(lse_ref[...], N_Q + hm, lane))
            dpb = _dot_nt(dob, vb)
            ds = (prob * (dpb - _col(del_ref[...], N_Q + hm, lane)) * SCALE).astype(BF16)
            dq_heads.append(_dot_nn(ds, kb))
            dmk_ref[:, ms] += _dot_tn(ds, qm)
            dmv_ref[:, ms] += _dot_tn(prob.astype(BF16), dob)
            if hm % 2 == 1:
                dqr, prod = _pair_bwd(_pack(dq_heads[-2], dq_heads[-1], lane), p_ref[:, js], gqm_ref[...],
                                      None, None, lane)
                dp_ref[:, js] = dqr.astype(BF16)
                dgqm = dgqm + _colsum(prod)
        dgqm_ref[...] += dgqm

    cur = lambda w: pl.BlockSpec((tm, w), lambda i: (i, 0))
    prev = lambda w: pl.BlockSpec((BLK, w), lambda i: (jnp.maximum(i * nb - 1, 0), 0))
    nxt = lambda w: pl.BlockSpec((BLK, w), lambda i: (jnp.minimum((i + 1) * nb, nblocks - 1), 0))
    full = lambda a: pl.BlockSpec(a.shape, lambda i: (0,) * a.ndim)
    acc = lambda r, w: pl.BlockSpec((r, w), lambda i: (0, 0))
    sd = jax.ShapeDtypeStruct
    return _call(
        body, name="attn_bwd", grid=(nt,),
        in_specs=[cur(PH), prev(PH), nxt(PH), cur(YH), nxt(YH), cur(SLOT), nxt(SLOT), cur(SLOT), nxt(SLOT),
                  cur(SLOT), prev(SLOT), nxt(SLOT), cur(SLOT), prev(SLOT), nxt(SLOT),
                  full(gq), full(gk), pl.BlockSpec(memory_space=pltpu.SMEM), full(mk), full(mv), full(gqm)],
        out_specs=[cur(PH), acc(1, SLOT), acc(1, SLOT), acc(1, SLOT), acc(1, SLOT), acc(M, W), acc(M, W)],
        out_shape=[sd((S, PH), BF16), sd((1, SLOT), F32), sd((1, SLOT), F32), sd((1, SLOT), F32),
                   sd((1, SLOT), F32), sd((M, W), F32), sd((M, W), F32)],
        args=[p, p, p, do, do, lse, lse, delta, delta, cosT, cosT, cosT, sinT, sinT, sinT,
              gq, gk, sinks, mk, mv, gqm],
        rider=rider)


def _proj_bwd(dpu, dph, h, dh2, g, n, w_inp, tm):
    S, D = h.shape

    def body(dpu_ref, dph_ref, h_ref, dh2_ref, g_ref, n_ref, w_ref, dh_ref, dg_ref, dw_ref):
        i = pl.program_id(0)

        @pl.when(i == 0)
        def _():
            dg_ref[...] = jnp.zeros_like(dg_ref)
            dw_ref[...] = jnp.zeros_like(dw_ref)

        dpu, dph, nv = dpu_ref[...], dph_ref[...], n_ref[...]
        dn = _dot_nn(dpu, w_ref[0:PU, :]) + _dot_nn(dph, w_ref[PU:PP, :])
        dw_ref[0:PU, :] += _dot_tn(dpu, nv)
        dw_ref[PU:PP, :] += _dot_tn(dph, nv)
        hv = h_ref[...]
        dx, dgrow = _rms_bwd(dn, hv, _rms(hv), g_ref[...])
        dh_ref[...] = dh2_ref[...] + dx
        dg_ref[...] += _colsum(dgrow)

    cur = lambda w: pl.BlockSpec((tm, w), lambda i: (i, 0))
    sd = jax.ShapeDtypeStruct
    return pl.pallas_call(
        body, name="proj_bwd", grid=(S // tm,),
        in_specs=[cur(PU), cur(PH), cur(D), cur(D), pl.BlockSpec((1, D), lambda i: (0, 0)), cur(D),
                  pl.BlockSpec((PP, D), lambda i: (0, 0))],
        out_specs=[cur(D), pl.BlockSpec((1, D), lambda i: (0, 0)), pl.BlockSpec((PP, D), lambda i: (0, 0))],
        out_shape=[sd((S, D), F32), sd((1, D), F32), sd((PP, D), F32)],
        compiler_params=_cp(),
    )(dpu, dph, h, dh2, g, n, w_inp)


def _norm_bwd(dxn, h, g, tm):
    S, D = h.shape

    def body(d_ref, h_ref, g_ref, dh_ref, dg_ref):
        @pl.when(pl.program_id(0) == 0)
        def _():
            dg_ref[...] = jnp.zeros_like(dg_ref)

        hv = h_ref[...]
        dx, dgrow = _rms_bwd(d_ref[...], hv, _rms(hv), g_ref[...])
        dh_ref[...] = dx
        dg_ref[...] += _colsum(dgrow)

    cur = pl.BlockSpec((tm, D), lambda i: (i, 0))
    vec = pl.BlockSpec((1, D), lambda i: (0, 0))
    return pl.pallas_call(
        body, name="norm_bwd", grid=(S // tm,), in_specs=[cur, cur, vec], out_specs=[cur, vec],
        out_shape=[jax.ShapeDtypeStruct((S, D), F32), jax.ShapeDtypeStruct((1, D), F32)],
        compiler_params=_cp(),
    )(dxn, h, g)


def _loss_bwd(xn, h, g, target, tm):
    S, D = h.shape

    def body(y_ref, h_ref, g_ref, t_ref, loss_ref, dh_ref, dg_ref):
        @pl.when(pl.program_id(0) == 0)
        def _():
            dg_ref[...] = jnp.zeros_like(dg_ref)
            loss_ref[...] = jnp.zeros_like(loss_ref)

        err = y_ref[...] - t_ref[...]
        part = jnp.sum(jnp.mean(err * err, axis=-1, keepdims=True), axis=0, keepdims=True)
        loss_ref[...] += 0.5 * part
        hv = h_ref[...]
        dx, dgrow = _rms_bwd(err * (1.0 / D), hv, _rms(hv), g_ref[...])
        dh_ref[...] = dx
        dg_ref[...] += _colsum(dgrow)

    cur = pl.BlockSpec((tm, D), lambda i: (i, 0))
    vec = pl.BlockSpec((1, D), lambda i: (0, 0))
    return pl.pallas_call(
        body, name="loss_bwd", grid=(S // tm,), in_specs=[cur, cur, vec, cur],
        out_specs=[pl.BlockSpec((1, SLOT), lambda i: (0, 0)), cur, vec],
        out_shape=[jax.ShapeDtypeStruct((1, SLOT), F32), jax.ShapeDtypeStruct((S, D), F32),
                   jax.ShapeDtypeStruct((1, D), F32)],
        compiler_params=_cp(),
    )(xn, h, g, target)


def _swap_with_sibling(bufs):
    nbuf = len(bufs)

    def body(*refs):
        ins, outs = refs[:nbuf], refs[nbuf:2 * nbuf]
        ssem, rsem = refs[2 * nbuf:]
        x, y, c, _ = _place()
        sends = [pltpu.make_async_remote_copy(src_ref=ins[b], dst_ref=outs[b], send_sem=ssem.at[b],
                                              recv_sem=rsem.at[b], device_id=(x, y, 1 - c), device_id_type=MESH)
                 for b in range(nbuf)]
        for cp in sends:
            cp.start()
        for cp in sends:
            cp.wait_recv()
        for cp in sends:
            cp.wait_send()

    hbm = pl.BlockSpec(memory_space=pl.ANY)
    return pl.pallas_call(
        body, name="swap_with_sibling",
        in_specs=[hbm] * nbuf, out_specs=[hbm] * nbuf,
        out_shape=[jax.ShapeDtypeStruct(b.shape, b.dtype) for b in bufs],
        scratch_shapes=[pltpu.SemaphoreType.DMA((nbuf,)), pltpu.SemaphoreType.DMA((nbuf,))],
    )(*bufs)


def _all_gather_small(buf):
    _, R, W = buf.shape

    def body(in_ref, out_ref, ssem, rsem, lsem):
        x, y, c, _ = _place()
        me = 4 * x + 2 * y + c
        local = pltpu.make_async_copy(in_ref, out_ref.at[pl.ds(me, 1)], lsem)
        local.start()

        def copy(k, block):
            fx, fy, fc = (k >> 2) & 1, (k >> 1) & 1, k & 1
            peer = (x ^ fx, y ^ fy, c ^ fc)
            return pltpu.make_async_remote_copy(
                src_ref=in_ref, dst_ref=out_ref.at[pl.ds(block, 1)], send_sem=ssem.at[k - 1],
                recv_sem=rsem.at[k - 1], device_id=peer, device_id_type=MESH)

        sends = [copy(k, me) for k in range(1, 8)]
        for cp in sends:
            cp.start()
        for k in range(1, 8):
            copy(k, me ^ k).wait_recv()
        for cp in sends:
            cp.wait_send()
        local.wait()

    hbm = pl.BlockSpec(memory_space=pl.ANY)
    return pl.pallas_call(
        body, name="all_gather_small", in_specs=[hbm], out_specs=hbm,
        out_shape=jax.ShapeDtypeStruct((8, R, W), buf.dtype),
        scratch_shapes=[pltpu.SemaphoreType.DMA((7,)), pltpu.SemaphoreType.DMA((7,)), pltpu.SemaphoreType.DMA],
    )(buf)


def _row_tile(n, cap=1024):
    for t in range(min(n, cap) // 8 * 8, 7, -8):
        if n % t == 0:
            return t
    return n


def _sum4(own, recv):
    n, rows, D = own.shape
    tr = _row_tile(rows)

    def body(o_ref, r0_ref, r1_ref, r2_ref, out_ref):
        out_ref[...] = ((o_ref[...].astype(F32) + r0_ref[...].astype(F32)) + r1_ref[...].astype(F32)) \
            + r2_ref[...].astype(F32)

    def rspec(p):
        return pl.BlockSpec((None, None, None, tr, D), lambda k, i, p=p: (p, k, 0, i, 0))

    blk = pl.BlockSpec((None, tr, D), lambda k, i: (k, i, 0))
    return pl.pallas_call(
        body, name="sum4", grid=(n, rows // tr),
        in_specs=[blk, rspec(0), rspec(1), rspec(2)], out_specs=blk,
        out_shape=jax.ShapeDtypeStruct((n, rows, D), F32),
    )(own, recv, recv, recv)


def _add2(a, b):
    rows, D = a.shape
    tr = _row_tile(rows)

    def body(a_ref, b_ref, o_ref):
        o_ref[...] = a_ref[...] + b_ref[...]

    blk = pl.BlockSpec((tr, D), lambda i: (i, 0))
    return pl.pallas_call(body, name="add2", grid=(rows // tr,), in_specs=[blk, blk], out_specs=blk,
                          out_shape=jax.ShapeDtypeStruct((rows, D), F32))(a, b)


def _adam_math(w, g, m, v):
    m = ADAM_B1 * m + (1.0 - ADAM_B1) * g
    v = ADAM_B2 * v + (1.0 - ADAM_B2) * (g * g)
    m_hat = m / (1.0 - ADAM_B1 ** ADAM_STEP)
    v_hat = v / (1.0 - ADAM_B2 ** ADAM_STEP)
    delta = -ADAM_LR * (m_hat / (jnp.sqrt(v_hat) + ADAM_EPS) + ADAM_WD * w)
    return delta, m, v


def _adam(w, g, m, v):
    rows, cols = w.shape
    tr = _row_tile(rows, 512)

    def body(w_ref, g_ref, m_ref, v_ref, d_ref, nm_ref, nv_ref):
        d, m_, v_ = _adam_math(w_ref[...], g_ref[...], m_ref[...], v_ref[...])
        d_ref[...] = d
        nm_ref[...] = m_
        nv_ref[...] = v_

    blk = pl.BlockSpec((tr, cols), lambda i: (i, 0))
    return pl.pallas_call(body, name="adam", grid=(rows // tr,), in_specs=[blk] * 4, out_specs=[blk] * 3,
                          out_shape=[jax.ShapeDtypeStruct((rows, cols), F32)] * 3)(w, g, m, v)


def _small_sum_adam(g8, w, m, v):
    _, R, W = g8.shape

    def body(g_ref, w_ref, m_ref, v_ref, go_ref, d_ref, nm_ref, nv_ref):
        g = g_ref[0]
        for k in range(1, 8):
            g = g + g_ref[k]
        go_ref[...] = g
        d, m_, v_ = _adam_math(w_ref[...], g, m_ref[...], v_ref[...])
        d_ref[...] = d
        nm_ref[...] = m_
        nv_ref[...] = v_

    return pl.pallas_call(body, name="small_sum_adam",
                          out_shape=[jax.ShapeDtypeStruct((R, W), F32)] * 4)(g8, w, m, v)


def _pad_heads_rows(w, first):
    lead, D = w.shape[:-2], w.shape[-1]
    heads = w[..., first:, :]
    n = heads.shape[-2] // HEAD_DIM
    heads = heads.reshape(lead + (n, HEAD_DIM, D))
    heads = jnp.pad(heads, [(0, 0)] * (len(lead) + 1) + [(0, SLOT - HEAD_DIM), (0, 0)])
    return jnp.concatenate([w[..., :first, :], heads.reshape(lead + (n * SLOT, D))], axis=-2)


def _unpad_heads_rows(w, first):
    lead, D = w.shape[:-2], w.shape[-1]
    heads = w[..., first:, :]
    n = heads.shape[-2] // SLOT
    heads = heads.reshape(lead + (n, SLOT, D))[..., :HEAD_DIM, :]
    return jnp.concatenate([w[..., :first, :], heads.reshape(lead + (n * HEAD_DIM, D))], axis=-2)


def _pad_vec(v):
    return jnp.pad(v, (0, SLOT - v.shape[0]))[None, :]


class _Pack:
    def __init__(self, shapes):
        self.shapes = shapes
        self.sizes = [int(functools.reduce(lambda a, b: a * b, s, 1)) for s in shapes]
        total = sum(self.sizes)
        self.rows = -(-total // (8 * SLOT)) * 8
        self.pad = self.rows * SLOT - total

    def pack(self, arrs):
        flat = jnp.concatenate([a.reshape(-1).astype(F32) for a in arrs] + [jnp.zeros((self.pad,), F32)])
        return flat.reshape(self.rows, SLOT)

    def unpack(self, buf):
        flat, out, o = buf.reshape(-1), [], 0
        for s, n in zip(self.shapes, self.sizes):
            out.append(flat[o:o + n].reshape(s))
            o += n
        return out


def kernel(x, mem, positions, ffn1_norm, ffn1_w1, ffn1_w3, ffn1_w2, mix_norm, w_in, conv_w, conv_b, conv_ln_g, conv_ln_b, swa_q_norm, swa_k_norm, swa_sinks, mem_norm, w_mem_kv, mem_q_norm, mem_k_norm, w_out, ffn2_norm, ffn2_w1, ffn2_w3, ffn2_w2, final_norm, loss_target, m_ffn1_norm, m_ffn1_w1, m_ffn1_w3, m_ffn1_w2, m_mix_norm, m_w_in, m_conv_w, m_conv_b, m_conv_ln_g, m_conv_ln_b, m_swa_q_norm, m_swa_k_norm, m_swa_sinks, m_mem_norm, m_w_mem_kv, m_mem_q_norm, m_mem_k_norm, m_w_out, m_ffn2_norm, m_ffn2_w1, m_ffn2_w3, m_ffn2_w2, m_final_norm, v_ffn1_norm, v_ffn1_w1, v_ffn1_w3, v_ffn1_w2, v_mix_norm, v_w_in, v_conv_w, v_conv_b, v_conv_ln_g, v_conv_ln_b, v_swa_q_norm, v_swa_k_norm, v_swa_sinks, v_mem_norm, v_w_mem_kv, v_mem_q_norm, v_mem_k_norm, v_w_out, v_ffn2_norm, v_ffn2_w1, v_ffn2_w3, v_ffn2_w2, v_final_norm):
    names = ['ffn1_norm', 'ffn1_w1', 'ffn1_w3', 'ffn1_w2', 'mix_norm', 'w_in', 'conv_w', 'conv_b', 'conv_ln_g',
             'conv_ln_b', 'swa_q_norm', 'swa_k_norm', 'swa_sinks', 'mem_norm', 'w_mem_kv', 'mem_q_norm',
             'mem_k_norm', 'w_out', 'ffn2_norm', 'ffn2_w1', 'ffn2_w3', 'ffn2_w2', 'final_norm']
    loc = locals()
    W = {n: loc[n] for n in names}
    M1 = {n: loc['m_' + n] for n in names}
    V1 = {n: loc['v_' + n] for n in names}

    S, D = x.shape[1], x.shape[2]
    L = ffn1_norm.shape[0]
    Fs = ffn1_w1.shape[2]
    F = 4 * Fs
    Mlen = mem.shape[1]
    cw_sh = conv_w.shape[2]
    tm = 512 if S >= 2048 else 256
    tf = 1408 if F % 1408 == 0 else 256
    tfw = 256
    tmw = 1024 if S >= 2048 else 256
    x0 = x[0]
    mem0 = mem[0]
    target = loss_target[0]
    my_chip = 2 * lax.axis_index("x") + lax.axis_index("y")

    mkv_rows = w_mem_kv.shape[1] * MEM_KV // D
    r_in, r_out = D_IN // 4, D_MIX // 4
    rm = r_in + r_out + mkv_rows

    def ffn_group(w1, w3, w2):
        return jnp.stack([w1.T, w3.T, w2]).astype(BF16).reshape(3, 1, Fs, D)

    groups = []
    for l in range(L):
        groups.append(ffn_group(ffn1_w1[l], ffn1_w3[l], ffn1_w2[l]))
        groups.append(jnp.concatenate([w_in[l].T, w_out[l], w_mem_kv[l].reshape(mkv_rows, D)])
                      .astype(BF16).reshape(1, 1, rm, D))
        groups.append(ffn_group(ffn2_w1[l], ffn2_w3[l], ffn2_w2[l]))
    gathered = [None] * len(groups)
    cw_rows = -(-(L * CONV_WIDTH) // 8) * 8
    cw_pad = jnp.pad(conv_w.reshape(L * CONV_WIDTH, cw_sh), ((0, cw_rows - L * CONV_WIDTH), (0, SLOT - cw_sh)))
    gathered[0], cw_g = _run_rider(_Gather([groups[0], cw_pad.reshape(1, 1, cw_rows, SLOT)]), "all_gather_first")
    conv_wF = cw_g[0, :, :L * CONV_WIDTH, :cw_sh].reshape(4, L, CONV_WIDTH, cw_sh)
    conv_wF = jnp.moveaxis(conv_wF, 0, 2).reshape(L, CONV_WIDTH, 4 * cw_sh)
    conv_wP = jnp.pad(conv_wF, ((0, 0), (0, 32 - CONV_WIDTH), (0, 0)))

    def gather_rider(j):
        want = [k for k in ([1, 2] if j == 0 else [j + 2]) if k < len(groups)]
        return (_Gather([groups[k] for k in want]), want) if want else (None, want)

    def mix_weights(l):
        g = gathered[3 * l + 1][0]
        w_inp = g[:, :r_in].reshape(D_IN, D)
        w_outp = g[:, r_in:r_in + r_out].reshape(D_MIX, D)
        w_mkvp = jnp.pad(g[:, r_in + r_out:].reshape(D, 2 * N_MEMH, HEAD_DIM),
                         ((0, 0), (0, 0), (0, SLOT - HEAD_DIM))).reshape(D, 2 * N_MEMH * SLOT)
        return w_inp, w_outp, w_mkvp

    inv_freq = ROPE_THETA ** (-jnp.arange(0, HEAD_DIM, 2, dtype=F32) / HEAD_DIM)
    invf = jnp.tile(inv_freq, SLOT // (HEAD_DIM // 2))[None, :]
    cosT, sinT = _rope_tables(positions.reshape(S, 1), invf, tm)

    row = lambda a, l: a[l][None, :]
    sinks_p = jnp.pad(swa_sinks, ((0, 0), (0, 8 - N_Q)))

    saved = []
    xin = x0
    xn = None
    for l in range(L):
        wf1 = gathered[3 * l].reshape(3, F, D)
        rider, want = gather_rider(3 * l)
        (h1, a1, b1, t1), got = _ffn_fwd(xin, row(ffn1_norm, l), wf1, None, tm, tf, rider=rider)
        for k, g in zip(want, got or []):
            gathered[k] = g
        w_inp, w_outp, w_mkvp = mix_weights(l)
        pu, p, n2 = _proj_fwd(h1, row(mix_norm, l), w_inp, tm)
        gk_m = _pad_vec(mem_k_norm[l])
        nm, mraw, mk, mv = _mem_kv_fwd(mem0, row(mem_norm, l), w_mkvp, gk_m)
        twice = lambda v: jnp.tile(v, 2)[None, :]
        gq, gk, gqm = twice(swa_q_norm[l]), twice(swa_k_norm[l]), twice(mem_q_norm[l])
        rider, want = gather_rider(3 * l + 1)
        (h2, y, yc, lse), got = _mixer_fwd(pu, p, h1, cosT, sinT, conv_wP[l], row(conv_b, l), row(conv_ln_g, l),
                                           row(conv_ln_b, l), gq, gk, sinks_p[l], mk, mv, gqm, w_outp, tm,
                                           rider=rider)
        for k, g in zip(want, got or []):
            gathered[k] = g
        wf2 = gathered[3 * l + 2].reshape(3, F, D)
        rider, want = gather_rider(3 * l + 2)
        (h3, a2, b2, t2, xn), got = _ffn_fwd(h2, row(ffn2_norm, l), wf2, row(final_norm, l), tm, tf, rider=rider)
        for k, g in zip(want, got or []):
            gathered[k] = g
        saved.append(dict(xin=xin, h1=h1, a1=a1, b1=b1, pu=pu, p=p, n2=n2, nm=nm, mraw=mraw, mk=mk, mv=mv, gk_m=gk_m,
                          gq=gq, gk=gk, gqm=gqm, h2=h2, y=y, yc=yc, lse=lse, h3=h3, a2=a2, b2=b2, t1=t1, t2=t2,
                          wf1=wf1, wf2=wf2, w_inp=w_inp, w_outp=w_outp, w_mkvp=w_mkvp))
        xin = xn

    G = {n: [None] * L for n in names}
    ffn_bufs = [None] * (2 * L)
    mix_bufs = [None] * L
    ffn_recv = [None] * (2 * L)
    mix_recv = [None] * L
    dxn = None
    loss_part = None
    for l in reversed(range(L)):
        sv = saved[l]
        if l == L - 1:
            loss_part, dh3, G['final_norm'][l] = _loss_bwd(xn, sv['h3'], row(final_norm, l), target, tm)
        else:
            dh3, G['final_norm'][l] = _norm_bwd(dxn, sv['h3'], row(final_norm, l), tm)
        rider = _Scatter([ffn_bufs[2 * l + 2]]) if l < L - 1 else None
        (dh2, G['ffn2_norm'][l], da, db, n, dy), got = _ffn_bwd_act(
            dh3, sv['h2'], row(ffn2_norm, l), sv['a2'], sv['b2'], sv['wf2'], tm, tf, rider=rider)
        if got:
            ffn_recv[2 * l + 2] = got[0]
        ffn_bufs[2 * l + 1] = _ffn_bwd_w(da, db, sv['t2'], n, dy, tmw, tfw).reshape(3, 4, Fs, D)
        dyc, do, delta, dwo, G['conv_ln_g'][l], G['conv_ln_b'][l] = _outproj_bwd(
            dh2, sv['y'], sv['yc'], row(conv_ln_g, l), row(conv_ln_b, l), sv['w_outp'], tm)
        (dph, dgq, dgk, dgqm, dsink, dmk, dmv), got = _attn_bwd(
            sv['p'], do, sv['lse'], delta, cosT, sinT, sv['gq'], sv['gk'], sinks_p[l],
            sv['mk'], sv['mv'], sv['gqm'], tm, rider=_Scatter([ffn_bufs[2 * l + 1]]))
        ffn_recv[2 * l + 1] = got[0]
        dpu, dcw, G['conv_b'][l] = _conv_bwd(sv['pu'], dyc, conv_wP[l], tm)
        dwm, G['mem_norm'][l], dgk_m = _mem_kv_bwd(dmk, dmv, sv['mraw'], sv['nm'], mem0, row(mem_norm, l),
                                                   sv['w_mkvp'], sv['gk_m'])
        dh1, G['mix_norm'][l], dwi = _proj_bwd(dpu, dph, sv['h1'], dh2, row(mix_norm, l), sv['n2'], sv['w_inp'], tm)
        dwiT = dwi.reshape(4, r_in, D)
        dwoF = dwo.reshape(4, r_out, D)
        dwmF = dwm.reshape(D, 2 * N_MEMH, SLOT)[:, :, :HEAD_DIM].reshape(4, mkv_rows, D)
        mix_bufs[l] = jnp.concatenate([dwiT, dwoF, dwmF], axis=1).astype(BF16).reshape(1, 4, rm, D)
        (dxl, G['ffn1_norm'][l], da, db, n, dy), got = _ffn_bwd_act(
            dh1, sv['xin'], row(ffn1_norm, l), sv['a1'], sv['b1'], sv['wf1'], tm, tf,
            rider=_Scatter([mix_bufs[l]]))
        mix_recv[l] = got[0]
        ffn_bufs[2 * l] = _ffn_bwd_w(da, db, sv['t1'], n, dy, tmw, tfw).reshape(3, 4, Fs, D)
        dxn = dxl
        G['conv_w'][l] = dcw[:CONV_WIDTH]
        G['swa_q_norm'][l] = dgq[0, :HEAD_DIM] + dgq[0, HEAD_DIM:]
        G['swa_k_norm'][l] = dgk[0, :HEAD_DIM] + dgk[0, HEAD_DIM:]
        G['mem_q_norm'][l] = dgqm[0, :HEAD_DIM] + dgqm[0, HEAD_DIM:]
        G['mem_k_norm'][l] = dgk_m[0, :HEAD_DIM]
        G['swa_sinks'][l] = dsink[0, :N_Q]
    ffn_recv[0] = _run_rider(_Scatter([ffn_bufs[0]]), "scatter_last")[0]
    grad_x = dxn[None]
    loss = lax.psum(loss_part[0, 0], AXES)

    parts = []
    for b, r in zip(ffn_bufs + mix_bufs, ffn_recv + mix_recv):
        own = lax.dynamic_index_in_dim(b, my_chip, axis=1, keepdims=False)
        parts.append(_sum4(own, r))
    theirs = _swap_with_sibling(parts)
    gsum = [_add2(a.reshape(-1, D), b.reshape(-1, D)).reshape(a.shape) for a, b in zip(parts, theirs)]
    for l in range(L):
        for f, pre in enumerate(('ffn1', 'ffn2')):
            g3 = gsum[2 * l + f]
            G[pre + '_w1'][l] = g3[0].T
            G[pre + '_w3'][l] = g3[1].T
            G[pre + '_w2'][l] = g3[2]
        gm = gsum[2 * L + l][0]
        G['w_in'][l] = gm[:r_in].T
        G['w_out'][l] = gm[r_in:r_in + r_out]
        G['w_mem_kv'][l] = gm[r_in + r_out:].reshape(w_mem_kv.shape[1], MEM_KV)

    small = ['ffn1_norm', 'mix_norm', 'conv_b', 'conv_ln_g', 'conv_ln_b', 'swa_q_norm', 'swa_k_norm', 'swa_sinks',
             'mem_norm', 'mem_q_norm', 'mem_k_norm', 'ffn2_norm', 'final_norm']
    gsmall = [jnp.stack([G[n][l].reshape(-1) for l in range(L)]) for n in small]
    gcw = jnp.stack(G['conv_w'])
    cw_cols = 4 * cw_sh
    full_of = lambda a: lax.dynamic_update_slice(jnp.zeros((L, CONV_WIDTH, cw_cols), F32), a, (0, 0, my_chip * cw_sh))
    pk = _Pack([W[n].shape for n in small] + [(L, CONV_WIDTH, cw_cols)])
    g8 = _all_gather_small(pk.pack(gsmall + [gcw])[None])
    outs4 = _small_sum_adam(g8, pk.pack([W[n] for n in small] + [full_of(conv_w)]),
                            pk.pack([M1[n] for n in small] + [full_of(m_conv_w)]),
                            pk.pack([V1[n] for n in small] + [full_of(v_conv_w)]))
    un = [pk.unpack(o) for o in outs4]
    grads, deltas, new_m, new_v = {}, {}, {}, {}
    for k, n in enumerate(small):
        grads[n], deltas[n], new_m[n], new_v[n] = un[0][k], un[1][k], un[2][k], un[3][k]
    mine = lambda a: lax.dynamic_slice(a, (0, 0, my_chip * cw_sh), (L, CONV_WIDTH, cw_sh))
    grads['conv_w'], deltas['conv_w'], new_m['conv_w'], new_v['conv_w'] = [mine(u[-1]) for u in un]

    for n in ('ffn1_w1', 'ffn1_w3', 'ffn1_w2', 'w_in', 'w_mem_kv', 'w_out', 'ffn2_w1', 'ffn2_w3', 'ffn2_w2'):
        g = jnp.stack(G[n])
        shp = W[n].shape
        v2 = lambda a: a.reshape(-1, shp[-1])
        d_, m_, v_ = _adam(v2(W[n]), v2(g), v2(M1[n]), v2(V1[n]))
        grads[n], deltas[n], new_m[n], new_v[n] = g, d_.reshape(shp), m_.reshape(shp), v_.reshape(shp)

    return (loss, grad_x, *[grads[n] for n in names], *[deltas[n] for n in names],
            *[new_m[n] for n in names], *[new_v[n] for n in names])
```

```python
import functools

import jax
import jax.numpy as jnp
from jax import lax
from jax.experimental import pallas as pl
from jax.experimental.pallas import tpu as pltpu

F32 = jnp.float32
BF16 = jnp.bfloat16
MESH = pl.DeviceIdType.MESH
AXES = ("x", "y", "c")

EPS = 1e-6
HEAD_DIM = 64
SLOT = 128
CONV_CH = 384
CONV_WIDTH = 31
N_Q, N_KV, N_MEMH = 6, 2, 4
GROUP = N_Q // N_KV
BLK = 128
HALO = 32
CONV_ROWS = 64
ROPE_THETA = 10000.0
SCALE = HEAD_DIM ** -0.5
NEG = -1e30

N_HEADS_IN = N_Q + 2 * N_KV + N_MEMH
PU = 2 * CONV_CH
PH = HEAD_DIM * N_HEADS_IN
PP = PU + PH
QO = 0
KO = QO + HEAD_DIM * N_Q
VO = KO + HEAD_DIM * N_KV
MO = VO + HEAD_DIM * N_KV
NH = N_Q + N_MEMH
STAT_ROWS = 16
YH = HEAD_DIM * NH
YP = CONV_CH + YH
YS = CONV_CH
YM = YS + HEAD_DIM * N_Q
D_IN = PP
D_MIX = YP
MEM_KV = 2 * HEAD_DIM * N_MEMH

ADAM_LR, ADAM_B1, ADAM_B2, ADAM_EPS, ADAM_WD, ADAM_STEP = 0.001, 0.9, 0.999, 1e-08, 0.01, 10

VMEM_LIMIT_MB = 56


def _cp(mb=VMEM_LIMIT_MB):
    return pltpu.CompilerParams(vmem_limit_bytes=mb * 1024 * 1024)


def _dot_nn(a, b):
    return lax.dot_general(a, b, (((1,), (0,)), ((), ())), preferred_element_type=F32)


def _dot_nt(a, b):
    return lax.dot_general(a, b, (((1,), (1,)), ((), ())), preferred_element_type=F32)


def _dot_tn(a, b):
    return lax.dot_general(a, b, (((0,), (0,)), ((), ())), preferred_element_type=F32)


def _sigmoid(x):
    return 1.0 / (1.0 + jnp.exp(-x))


def _rms(x):
    return lax.rsqrt(jnp.mean(x * x, axis=-1, keepdims=True) + EPS)


def _rms_bwd(dn, x, r, g):
    xhat = x * r
    dxhat = dn * g
    dx = r * (dxhat - xhat * jnp.mean(dxhat * xhat, axis=-1, keepdims=True))
    return dx, dn * xhat


def _colsum(v):
    return jnp.sum(v, axis=0, keepdims=True)


def _lane(n):
    return lax.broadcasted_iota(jnp.int32, (n, SLOT), 1)


def _partner(v, lane):
    up = pltpu.roll(v, SLOT - HEAD_DIM // 2, 1)
    dn = pltpu.roll(v, HEAD_DIM // 2, 1)
    return jnp.where(lane < HEAD_DIM // 2, up, jnp.where(lane < HEAD_DIM, dn, 0.0))


def _head_rms(xs):
    return lax.rsqrt(jnp.sum(xs * xs, axis=-1, keepdims=True) * (1.0 / HEAD_DIM) + EPS)


def _head_fwd(xs, g, cosv, sinv, lane):
    xn = xs * _head_rms(xs) * g
    if cosv is None:
        return xn
    return xn * cosv + _partner(xn, lane) * sinv


def _head_bwd(dout, xs, g, cosv, sinv, lane):
    if cosv is not None:
        dout = dout * cosv + _partner(dout * sinv, lane)
    r = _head_rms(xs)
    xhat = xs * r
    dxhat = dout * g
    dx = r * (dxhat - xhat * (jnp.sum(dxhat * xhat, axis=-1, keepdims=True) * (1.0 / HEAD_DIM)))
    return dx, dout * xhat


def _col(v, h, lane):
    return jnp.sum(jnp.where(lane == h, v, 0.0), axis=-1, keepdims=True)


def _halves(v, lane):
    lo = jnp.sum(jnp.where(lane < HEAD_DIM, v, 0.0), axis=-1, keepdims=True)
    hi = jnp.sum(jnp.where(lane < HEAD_DIM, 0.0, v), axis=-1, keepdims=True)
    return jnp.where(lane < HEAD_DIM, lo, hi)


def _pair_rms(x, lane):
    return lax.rsqrt(_halves(x * x, lane) * (1.0 / HEAD_DIM) + EPS)


def _pair_partner(v, lane):
    return jnp.where((lane & (HEAD_DIM - 1)) < HEAD_DIM // 2,
                     pltpu.roll(v, SLOT - HEAD_DIM // 2, 1), pltpu.roll(v, HEAD_DIM // 2, 1))


def _pair_fwd(x, g2, cosv, sinv, lane):
    xn = x * _pair_rms(x, lane) * g2
    if cosv is None:
        return xn
    return xn * cosv + _pair_partner(xn, lane) * sinv


def _pair_bwd(dout, x, g2, cosv, sinv, lane):
    if cosv is not None:
        dout = dout * cosv + _pair_partner(dout * sinv, lane)
    r = _pair_rms(x, lane)
    xhat = x * r
    dxhat = dout * g2
    dx = r * (dxhat - xhat * (_halves(dxhat * xhat, lane) * (1.0 / HEAD_DIM)))
    return dx, dout * xhat


def _lo(x, half, lane):
    if half:
        x = pltpu.roll(x, HEAD_DIM, 1)
    return jnp.where(lane < HEAD_DIM, x, 0.0)


def _pack(even, odd, lane):
    return jnp.where(lane < HEAD_DIM, even, pltpu.roll(odd, HEAD_DIM, 1))


def _place():
    x, y, c = lax.axis_index("x"), lax.axis_index("y"), lax.axis_index("c")
    chips = [(1 - x, y), (x, 1 - y), (1 - x, 1 - y)]
    return x, y, c, chips


class _Gather:
    tag = "_gather"

    def __init__(self, bufs):
        self.bufs = list(bufs)
        nb = len(self.bufs)
        self.out_shape = [jax.ShapeDtypeStruct((b.shape[0], 4) + b.shape[2:], b.dtype) for b in self.bufs]
        self.sems = [pltpu.SemaphoreType.DMA((3 * nb,)), pltpu.SemaphoreType.DMA((3 * nb,)),
                     pltpu.SemaphoreType.DMA((nb,))]

    def _copies(self, ins, outs, sems):
        ssem, rsem, lsem = sems
        nb = len(self.bufs)
        x, y, c, chips = _place()
        mine = 2 * x + y

        def copy(b, p, shard):
            return pltpu.make_async_remote_copy(
                src_ref=ins[b], dst_ref=outs[b].at[:, pl.ds(shard, 1)],
                send_sem=ssem.at[3 * b + p], recv_sem=rsem.at[3 * b + p],
                device_id=(chips[p][0], chips[p][1], c), device_id_type=MESH)

        local = [pltpu.make_async_copy(ins[b], outs[b].at[:, pl.ds(mine, 1)], lsem.at[b]) for b in range(nb)]
        sends = [copy(b, p, mine) for b in range(nb) for p in range(3)]
        recvs = [copy(b, p, 2 * chips[p][0] + chips[p][1]) for b in range(nb) for p in range(3)]
        return local, sends, recvs

    def start(self, ins, outs, sems):
        local, sends, _ = self._copies(ins, outs, sems)
        for cp in local + sends:
            cp.start()

    def wait(self, ins, outs, sems):
        local, sends, recvs = self._copies(ins, outs, sems)
        for cp in recvs:
            cp.wait_recv()
        for cp in sends:
            cp.wait_send()
        for cp in local:
            cp.wait()


class _Scatter:
    tag = "_scatter"

    def __init__(self, bufs):
        self.bufs = list(bufs)
        nb = len(self.bufs)
        self.out_shape = [jax.ShapeDtypeStruct((3, b.shape[0], 1) + b.shape[2:], b.dtype) for b in self.bufs]
        self.sems = [pltpu.SemaphoreType.DMA((3 * nb,)), pltpu.SemaphoreType.DMA((3 * nb,))]

    def _copies(self, ins, outs, sems):
        ssem, rsem = sems
        x, y, c, chips = _place()

        def copy(b, p):
            shard = 2 * chips[p][0] + chips[p][1]
            return pltpu.make_async_remote_copy(
                src_ref=ins[b].at[:, pl.ds(shard, 1)], dst_ref=outs[b].at[p],
                send_sem=ssem.at[3 * b + p], recv_sem=rsem.at[3 * b + p],
                device_id=(chips[p][0], chips[p][1], c), device_id_type=MESH)

        return [copy(b, p) for b in range(len(self.bufs)) for p in range(3)]

    def start(self, ins, outs, sems):
        for cp in self._copies(ins, outs, sems):
            cp.start()

    def wait(self, ins, outs, sems):
        cps = self._copies(ins, outs, sems)
        for cp in cps:
            cp.wait_recv()
        for cp in cps:
            cp.wait_send()


def _run_rider(rider, name):
    nb = len(rider.bufs)

    def body(*refs):
        ins, outs, sems = refs[:nb], refs[nb:2 * nb], refs[2 * nb:]
        rider.start(ins, outs, sems)
        rider.wait(ins, outs, sems)

    hbm = pl.BlockSpec(memory_space=pl.ANY)
    return pl.pallas_call(body, name=name, in_specs=[hbm] * nb, out_specs=[hbm] * nb,
                          out_shape=rider.out_shape, scratch_shapes=rider.sems)(*rider.bufs)


def _call(body, *, name, grid, in_specs, out_specs, out_shape, args, scratch=(), rider=None):
    if rider is None:
        outs = pl.pallas_call(body, name=name, grid=grid, in_specs=list(in_specs), out_specs=list(out_specs),
                              out_shape=list(out_shape), scratch_shapes=list(scratch),
                              compiler_params=_cp())(*args)
        return list(outs), None
    n_in, n_out, n_scr, nb = len(in_specs), len(out_specs), len(scratch), len(rider.bufs)

    def wrapped(*refs):
        cuts = [n_in, nb, n_out, nb, n_scr]
        parts, o = [], 0
        for n in cuts:
            parts.append(refs[o:o + n])
            o += n
        ins, rin, outs, rout, scr = parts
        sems = refs[o:]
        ids = [pl.program_id(k) for k in range(len(grid))]
        first = functools.reduce(jnp.logical_and, [i == 0 for i in ids])
        last = functools.reduce(jnp.logical_and, [i == n - 1 for i, n in zip(ids, grid)])

        @pl.when(first)
        def _():
            rider.start(rin, rout, sems)

        body(*ins, *outs, *scr)

        @pl.when(last)
        def _():
            rider.wait(rin, rout, sems)

    hbm = pl.BlockSpec(memory_space=pl.ANY)
    res = pl.pallas_call(
        wrapped, name=name + rider.tag, grid=grid,
        in_specs=list(in_specs) + [hbm] * nb, out_specs=list(out_specs) + [hbm] * nb,
        out_shape=list(out_shape) + rider.out_shape, scratch_shapes=list(scratch) + rider.sems,
        compiler_params=_cp())(*args, *rider.bufs)
    return list(res[:n_out]), list(res[n_out:])


def _rope_tables(pos, invf, tm):
    S = pos.shape[0]

    def body(pos_ref, f_ref, cos_ref, sin_ref):
        ang = pos_ref[...].astype(F32) * f_ref[...]
        lane = _lane(tm)
        cos_ref[...] = jnp.cos(ang)
        s = jnp.sin(ang)
        sin_ref[...] = jnp.where((lane & (HEAD_DIM - 1)) < HEAD_DIM // 2, -s, s)

    return pl.pallas_call(
        body, name="rope_tables", grid=(S // tm,),
        in_specs=[pl.BlockSpec((tm, 1), lambda i: (i, 0)), pl.BlockSpec((1, SLOT), lambda i: (0, 0))],
        out_specs=[pl.BlockSpec((tm, SLOT), lambda i: (i, 0))] * 2,
        out_shape=[jax.ShapeDtypeStruct((S, SLOT), F32)] * 2,
    )(pos, invf)


def _ffn_fwd(x, g, wf, gfin, tm, tf, rider=None):
    S, D = x.shape
    F = wf.shape[1]
    nf = F // tf
    final = gfin is not None

    chunks = [(c, min(256, tf - c)) for c in range(0, tf, 256)]

    def body(*refs):
        if final:
            x_ref, g_ref, w1_ref, w3_ref, w2_ref, gf_ref, h_ref, a_ref, b_ref, t_ref, xn_ref, n_scr, acc = refs
        else:
            x_ref, g_ref, w1_ref, w3_ref, w2_ref, h_ref, a_ref, b_ref, t_ref, n_scr, acc = refs
        j = pl.program_id(1)

        @pl.when(j == 0)
        def _():
            xv = x_ref[...]
            n_scr[...] = (xv * _rms(xv) * g_ref[...]).astype(BF16)
            acc[...] = jnp.zeros_like(acc)

        n = n_scr[...]
        for c0, cw in chunks:
            cols = slice(c0, c0 + cw)
            a = _dot_nt(n, w1_ref[cols, :])
            b = _dot_nt(n, w3_ref[cols, :])
            a_ref[:, cols] = a.astype(BF16)
            b_ref[:, cols] = b.astype(BF16)
            t_ref[:, cols] = (a * _sigmoid(a) * b).astype(BF16)
        acc[...] += _dot_nn(t_ref[...], w2_ref[...])

        @pl.when(j == nf - 1)
        def _():
            h = x_ref[...] + 0.5 * acc[...]
            h_ref[...] = h
            if final:
                xn_ref[...] = h * _rms(h) * gf_ref[...]

    def wspec(k):
        return pl.BlockSpec((None, tf, D), lambda i, j, k=k: (k, j, 0))

    row = pl.BlockSpec((tm, D), lambda i, j: (i, 0))
    vec = pl.BlockSpec((1, D), lambda i, j: (0, 0))
    act = pl.BlockSpec((tm, tf), lambda i, j: (i, j))
    in_specs = [row, vec, wspec(0), wspec(1), wspec(2)] + ([vec] if final else [])
    out_specs = [row, act, act, act] + ([row] if final else [])
    out_shape = [jax.ShapeDtypeStruct((S, D), F32)] + [jax.ShapeDtypeStruct((S, F), BF16)] * 3 \
        + ([jax.ShapeDtypeStruct((S, D), F32)] if final else [])
    args = [x, g, wf, wf, wf] + ([gfin] if final else [])
    return _call(body, name="ffn_fwd_final" if final else "ffn_fwd", grid=(S // tm, nf),
                 in_specs=in_specs, out_specs=out_specs, out_shape=out_shape, args=args,
                 scratch=[pltpu.VMEM((tm, D), BF16), pltpu.VMEM((tm, D), F32)], rider=rider)


def _ffn_bwd_act(dh, x, g, a, b, wf, tm, tf, rider=None):
    S, D = x.shape
    F = wf.shape[1]
    nf = F // tf

    chunks = [(c, min(256, tf - c)) for c in range(0, tf, 256)]

    def body(dh_ref, x_ref, g_ref, a_ref, b_ref, w1_ref, w3_ref, w2_ref,
             dx_ref, dg_ref, da_ref, db_ref, n_ref, dy_ref, acc):
        i, j = pl.program_id(0), pl.program_id(1)

        @pl.when(j == 0)
        def _():
            xv = x_ref[...]
            n_ref[...] = (xv * _rms(xv) * g_ref[...]).astype(BF16)
            dy_ref[...] = (0.5 * dh_ref[...]).astype(BF16)
            acc[...] = jnp.zeros_like(acc)

            @pl.when(i == 0)
            def _():
                dg_ref[...] = jnp.zeros_like(dg_ref)

        dyv = dy_ref[...]
        for c0, cw in chunks:
            cols = slice(c0, c0 + cw)
            av = a_ref[:, cols].astype(F32)
            bv = b_ref[:, cols].astype(F32)
            sg = _sigmoid(av)
            dt = _dot_nt(dyv, w2_ref[cols, :])
            db_ref[:, cols] = (dt * (av * sg)).astype(BF16)
            da_ref[:, cols] = (dt * bv * (sg * (1.0 + av * (1.0 - sg)))).astype(BF16)
        acc[...] += _dot_nn(da_ref[...], w1_ref[...]) + _dot_nn(db_ref[...], w3_ref[...])

        @pl.when(j == nf - 1)
        def _():
            xv = x_ref[...]
            dx, dgrow = _rms_bwd(acc[...], xv, _rms(xv), g_ref[...])
            dx_ref[...] = dh_ref[...] + dx
            dg_ref[...] += _colsum(dgrow)

    def wspec(k):
        return pl.BlockSpec((None, tf, D), lambda i, j, k=k: (k, j, 0))

    row = pl.BlockSpec((tm, D), lambda i, j: (i, 0))
    vec = pl.BlockSpec((1, D), lambda i, j: (0, 0))
    act = pl.BlockSpec((tm, tf), lambda i, j: (i, j))
    sd = lambda shp, dt: jax.ShapeDtypeStruct(shp, dt)
    return _call(body, name="ffn_bwd_act", grid=(S // tm, nf),
                 in_specs=[row, row, vec, act, act, wspec(0), wspec(1), wspec(2)],
                 out_specs=[row, vec, act, act, row, row],
                 out_shape=[sd((S, D), F32), sd((1, D), F32), sd((S, F), BF16), sd((S, F), BF16),
                            sd((S, D), BF16), sd((S, D), BF16)],
                 args=[dh, x, g, a, b, wf, wf, wf], scratch=[pltpu.VMEM((tm, D), F32)], rider=rider)


def _ffn_bwd_w(da, db, t, n, dy, tm, tf):
    S, F = da.shape
    D = n.shape[1]
    nt = S // tm

    def body(da_ref, db_ref, t_ref, n_ref, dy_ref, out_ref, acc):
        i = pl.program_id(1)

        @pl.when(i == 0)
        def _():
            acc[...] = jnp.zeros_like(acc)

        nv = n_ref[...]
        acc[0] += _dot_tn(da_ref[...], nv)
        acc[1] += _dot_tn(db_ref[...], nv)
        acc[2] += _dot_tn(t_ref[...], dy_ref[...])

        @pl.when(i == nt - 1)
        def _():
            out_ref[...] = acc[...].astype(BF16)

    act = pl.BlockSpec((tm, tf), lambda j, i: (i, j))
    row = pl.BlockSpec((tm, D), lambda j, i: (i, 0))
    return pl.pallas_call(
        body, name="ffn_bwd_w", grid=(F // tf, nt),
        in_specs=[act, act, act, row, row],
        out_specs=pl.BlockSpec((3, tf, D), lambda j, i: (0, j, 0)),
        out_shape=jax.ShapeDtypeStruct((3, F, D), BF16),
        scratch_shapes=[pltpu.VMEM((3, tf, D), F32)],
        compiler_params=_cp(),
    )(da, db, t, n, dy)


def _proj_fwd(h, g, w_inp, tm):
    S, D = h.shape

    def body(h_ref, g_ref, w_ref, pu_ref, ph_ref, n_ref):
        hv = h_ref[...]
        n = (hv * _rms(hv) * g_ref[...]).astype(BF16)
        n_ref[...] = n
        pu_ref[...] = _dot_nt(n, w_ref[0:PU, :])
        ph_ref[...] = _dot_nt(n, w_ref[PU:PP, :])

    cur = lambda w: pl.BlockSpec((tm, w), lambda i: (i, 0))
    return pl.pallas_call(
        body, name="proj_fwd", grid=(S // tm,),
        in_specs=[cur(D), pl.BlockSpec((1, D), lambda i: (0, 0)), pl.BlockSpec((PP, D), lambda i: (0, 0))],
        out_specs=[cur(PU), cur(PH), cur(D)],
        out_shape=[jax.ShapeDtypeStruct((S, PU), F32), jax.ShapeDtypeStruct((S, PH), F32),
                   jax.ShapeDtypeStruct((S, D), BF16)],
        compiler_params=_cp(),
    )(h, g, w_inp)


def _glu(u):
    return u[:, :CONV_CH] * _sigmoid(u[:, CONV_CH:2 * CONV_CH])


def _shifted_copies(ext8):
    n = ext8.shape[1]
    for b in range(1, 8):
        ext8[b, 0:n - 8, :] = ext8[0, b:b + n - 8, :]


def _window(ext8, off, rows, r0=0):
    return ext8[off % 8, pl.ds(r0 + (off - off % 8), rows), :]


def _layer_norm_stats(yc):
    mu = jnp.mean(yc, axis=-1, keepdims=True)
    d = yc - mu
    rstd = lax.rsqrt(jnp.mean(d * d, axis=-1, keepdims=True) + EPS)
    return d * rstd, rstd


def _mem_kv_fwd(mem, g, w_mkvp, gk):
    M, D = mem.shape
    W = SLOT * N_MEMH

    def body(mem_ref, g_ref, w_ref, gk_ref, nm_ref, raw_ref, mk_ref, mv_ref):
        mv_ = mem_ref[...]
        nm = (mv_ * _rms(mv_) * g_ref[...]).astype(BF16)
        nm_ref[...] = nm
        raw = _dot_nn(nm, w_ref[...])
        raw_ref[...] = raw
        for hh in range(N_MEMH):
            sl = slice(SLOT * hh, SLOT * (hh + 1))
            mk_ref[:, sl] = _head_fwd(raw[:, sl], gk_ref[...], None, None, None).astype(BF16)
        mv_ref[...] = raw[:, W:].astype(BF16)

    sd = jax.ShapeDtypeStruct
    return pl.pallas_call(
        body, name="mem_kv_fwd",
        out_shape=[sd((M, D), BF16), sd((M, 2 * W), F32), sd((M, W), BF16), sd((M, W), BF16)],
        compiler_params=_cp(),
    )(mem, g, w_mkvp, gk)


def _mem_kv_bwd(dmk, dmv, raw, nm, mem, g, w_mkvp, gk):
    M, D = mem.shape
    W = SLOT * N_MEMH

    def body(dmk_ref, dmv_ref, raw_ref, nm_ref, mem_ref, g_ref, w_ref, gk_ref, dw_ref, dg_ref, dgk_ref, draw):
        dgk = jnp.zeros((1, SLOT), F32)
        for hh in range(N_MEMH):
            sl = slice(SLOT * hh, SLOT * (hh + 1))
            dx, prod = _head_bwd(dmk_ref[:, sl], raw_ref[:, sl], gk_ref[...], None, None, None)
            draw[:, sl] = dx.astype(BF16)
            dgk = dgk + _colsum(prod)
        dgk_ref[...] = dgk
        draw[:, W:] = dmv_ref[...].astype(BF16)
        dr = draw[...]
        dw_ref[...] = _dot_tn(nm_ref[...], dr)
        dnm = _dot_nt(dr, w_ref[...])
        mv_ = mem_ref[...]
        dg_ref[...] = _colsum(dnm * (mv_ * _rms(mv_)))

    sd = jax.ShapeDtypeStruct
    return pl.pallas_call(
        body, name="mem_kv_bwd",
        out_shape=[sd((D, 2 * W), F32), sd((1, D), F32), sd((1, SLOT), F32)],
        scratch_shapes=[pltpu.VMEM((M, 2 * W), BF16)],
        compiler_params=_cp(),
    )(dmk, dmv, raw, nm, mem, g, w_mkvp, gk)


def _mixer_fwd(pu, ph, h, cosT, sinT, conv_w, conv_b, ln_g, ln_b, gq, gk, sinks, mk, mv, gqm, w_outp, tm, rider=None):
    S, D = h.shape
    M = mk.shape[0]
    nb = tm // BLK
    nblocks = S // BLK

    def body(pu_ref, pup_ref, p_ref, ph_ref, h_ref, cos_ref, cosh_ref, sin_ref, sinh_ref, cw_ref, cb_ref,
             lg_ref, lb_ref, gq_ref, gk_ref, sink_ref, mk_ref, mv_ref, gqm_ref, wo_ref,
             h2_ref, y_ref, yc_ref, lse_ref, ext, y_scr):
        i = pl.program_id(0)
        not_first = (i > 0).astype(F32)
        lane = _lane(tm)
        lane_e = _lane(tm + BLK)

        ext[0, 0:HALO, :] = _glu(pup_ref[...]) * not_first
        ext[0, HALO:HALO + tm, :] = _glu(pu_ref[...])
        _shifted_copies(ext)

        def rows_chunk(r, carry):
            r0 = pl.multiple_of(r * CONV_ROWS, CONV_ROWS)
            yc = jnp.zeros((CONV_ROWS, CONV_CH), F32) + cb_ref[...]
            for k in range(CONV_WIDTH):
                yc = yc + cw_ref[k:k + 1, :] * _window(ext, HALO - (CONV_WIDTH - 1) + k, CONV_ROWS, r0)
            yc_ref[pl.ds(r0, CONV_ROWS), :] = yc
            z, _ = _layer_norm_stats(yc)
            ln = z * lg_ref[...] + lb_ref[...]
            y_scr[pl.ds(r0, CONV_ROWS), 0:CONV_CH] = (ln * _sigmoid(ln)).astype(BF16)
            return carry

        lax.fori_loop(0, tm // CONV_ROWS, rows_chunk, 0)

        cos_e = jnp.concatenate([cosh_ref[...], cos_ref[...]], axis=0)
        sin_e = jnp.concatenate([sinh_ref[...], sin_ref[...]], axis=0)
        qi = lax.broadcasted_iota(jnp.int32, (GROUP * BLK, 2 * BLK), 0) & (BLK - 1)
        kj = lax.broadcasted_iota(jnp.int32, (GROUP * BLK, 2 * BLK), 1)
        band = (kj > qi) & (kj <= qi + BLK)
        band0 = band & ((kj >= BLK) | (i > 0))
        lse = jnp.zeros((tm, SLOT), F32)
        k_pair = jnp.concatenate([ph_ref[:, KO:KO + SLOT], p_ref[:, KO:KO + SLOT]], axis=0)
        k_pair = _pair_fwd(k_pair, gk_ref[...], cos_e, sin_e, lane_e)
        v_pair = jnp.concatenate([ph_ref[:, VO:VO + SLOT], p_ref[:, VO:VO + SLOT]], axis=0)
        k_e = [_lo(k_pair, kvh, lane_e).astype(BF16) for kvh in range(N_KV)]
        v_e = [_lo(v_pair, kvh, lane_e).astype(BF16) for kvh in range(N_KV)]
        q_lo = []
        for j in range(N_Q // 2):
            q_pair = _pair_fwd(p_ref[:, QO + SLOT * j:QO + SLOT * (j + 1)], gq_ref[...],
                               cos_ref[...], sin_ref[...], lane)
            q_lo += [_lo(q_pair, 0, lane).astype(BF16), _lo(q_pair, 1, lane).astype(BF16)]
        outs = [[] for _ in range(N_Q)]
        lses = [[] for _ in range(N_Q)]
        for kvh in range(N_KV):
            hs = [GROUP * kvh + gi for gi in range(GROUP)]
            sink3 = jnp.concatenate([jnp.full((BLK, 1), sink_ref[h], F32) for h in hs], axis=0)
            for m in range(nb):
                rows = slice(BLK * m, BLK * (m + 1))
                win = slice(BLK * m, BLK * (m + 2))
                q3 = jnp.concatenate([q_lo[h][rows] for h in hs], axis=0)
                s = _dot_nt(q3, k_e[kvh][win]) * SCALE
                s = jnp.where(band0 if m == 0 else band, s, NEG)
                mx = jnp.maximum(jnp.max(s, axis=-1, keepdims=True), sink3)
                e = jnp.exp(s - mx)
                den = jnp.sum(e, axis=-1, keepdims=True) + jnp.exp(sink3 - mx)
                o3 = _dot_nn((e / den).astype(BF16), v_e[kvh][win])
                l3 = mx + jnp.log(den)
                for gi, h in enumerate(hs):
                    outs[h].append(o3[BLK * gi:BLK * (gi + 1)])
                    lses[h].append(l3[BLK * gi:BLK * (gi + 1)])
        for h in range(N_Q):
            lse = jnp.where(lane == h, jnp.concatenate(lses[h], axis=0), lse)
        for j in range(N_Q // 2):
            y_scr[:, YS + SLOT * j:YS + SLOT * (j + 1)] = _pack(
                jnp.concatenate(outs[2 * j], axis=0), jnp.concatenate(outs[2 * j + 1], axis=0), lane).astype(BF16)

        heads = []
        for hm in range(N_MEMH):
            ms = slice(SLOT * hm, SLOT * (hm + 1))
            if hm % 2 == 0:
                qm_pair = _pair_fwd(p_ref[:, MO + SLOT * (hm // 2):MO + SLOT * (hm // 2 + 1)], gqm_ref[...],
                                    None, None, lane)
            s = _dot_nt(_lo(qm_pair, hm % 2, lane).astype(BF16), mk_ref[:, ms]) * SCALE
            mx = jnp.max(s, axis=-1, keepdims=True)
            e = jnp.exp(s - mx)
            den = jnp.sum(e, axis=-1, keepdims=True)
            heads.append(_dot_nn((e / den).astype(BF16), mv_ref[:, ms]))
            lse = jnp.where(lane == N_Q + hm, mx + jnp.log(den), lse)
            if hm % 2 == 1:
                y_scr[:, YM + SLOT * (hm // 2):YM + SLOT * (hm // 2 + 1)] = _pack(heads[-2], heads[-1], lane).astype(BF16)
        lse_ref[...] = lse.T[0:STAT_ROWS, :]

        yv = y_scr[...]
        y_ref[...] = yv
        h2_ref[...] = h_ref[...] + _dot_nn(yv, wo_ref[...])

    cur = lambda w: pl.BlockSpec((tm, w), lambda i: (i, 0))
    prev = lambda w: pl.BlockSpec((BLK, w), lambda i: (jnp.maximum(i * nb - 1, 0), 0))
    full = lambda a: pl.BlockSpec(a.shape, lambda i: (0,) * a.ndim)
    sd = jax.ShapeDtypeStruct
    prev32 = pl.BlockSpec((HALO, PU), lambda i: (jnp.maximum(i * (tm // HALO) - 1, 0), 0))
    return _call(
        body, name="mixer_fwd", grid=(S // tm,),
        in_specs=[cur(PU), prev32, cur(PH), prev(PH), cur(D), cur(SLOT), prev(SLOT), cur(SLOT), prev(SLOT),
                  full(conv_w), full(conv_b), full(ln_g), full(ln_b), full(gq), full(gk),
                  pl.BlockSpec(memory_space=pltpu.SMEM), full(mk), full(mv), full(gqm), full(w_outp)],
        out_specs=[cur(D), cur(YP), cur(CONV_CH), pl.BlockSpec((STAT_ROWS, tm), lambda i: (0, i))],
        out_shape=[sd((S, D), F32), sd((S, YP), BF16), sd((S, CONV_CH), F32), sd((STAT_ROWS, S), F32)],
        args=[pu, pu, ph, ph, h, cosT, cosT, sinT, sinT, conv_w, conv_b, ln_g, ln_b, gq, gk, sinks, mk, mv, gqm,
              w_outp],
        scratch=[pltpu.VMEM((8, tm + HALO, CONV_CH), F32), pltpu.VMEM((tm, YP), BF16)], rider=rider)


def _outproj_bwd(dh2, y, yc, ln_g, ln_b, w_outp, tm):
    S, D = dh2.shape

    def body(dh_ref, y_ref, yc_ref, lg_ref, lb_ref, wo_ref, dyc_ref, do_ref, del_ref, dwo_ref, dlg_ref, dlb_ref):
        i = pl.program_id(0)

        @pl.when(i == 0)
        def _():
            dwo_ref[...] = jnp.zeros_like(dwo_ref)
            dlg_ref[...] = jnp.zeros_like(dlg_ref)
            dlb_ref[...] = jnp.zeros_like(dlb_ref)

        dhb = dh_ref[...].astype(BF16)
        yv = y_ref[...]
        dy = _dot_nt(dhb, wo_ref[...])
        dwo_ref[...] += _dot_tn(yv, dhb)

        z, rstd = _layer_norm_stats(yc_ref[...])
        ln = z * lg_ref[...] + lb_ref[...]
        sg = _sigmoid(ln)
        dln = dy[:, 0:CONV_CH] * (sg * (1.0 + ln * (1.0 - sg)))
        dlg_ref[...] += _colsum(dln * z)
        dlb_ref[...] += _colsum(dln)
        dz = dln * lg_ref[...]
        dyc_ref[...] = rstd * (dz - jnp.mean(dz, axis=-1, keepdims=True)
                               - z * jnp.mean(dz * z, axis=-1, keepdims=True))
        do_ref[...] = dy[:, CONV_CH:].astype(BF16)

        lane = _lane(tm)
        delta = jnp.zeros((tm, SLOT), F32)
        for j in range(NH // 2):
            sl = slice(YS + SLOT * j, YS + SLOT * (j + 1))
            prod = dy[:, sl] * yv[:, sl].astype(F32)
            lo = jnp.sum(jnp.where(lane < HEAD_DIM, prod, 0.0), axis=-1, keepdims=True)
            hi = jnp.sum(jnp.where(lane < HEAD_DIM, 0.0, prod), axis=-1, keepdims=True)
            delta = jnp.where(lane == 2 * j, lo, jnp.where(lane == 2 * j + 1, hi, delta))
        del_ref[...] = delta.T[0:STAT_ROWS, :]

    cur = lambda w: pl.BlockSpec((tm, w), lambda i: (i, 0))
    full = lambda a: pl.BlockSpec(a.shape, lambda i: (0,) * a.ndim)
    sd = jax.ShapeDtypeStruct
    return pl.pallas_call(
        body, name="outproj_bwd", grid=(S // tm,),
        in_specs=[cur(D), cur(YP), cur(CONV_CH), full(ln_g), full(ln_b), full(w_outp)],
        out_specs=[cur(CONV_CH), cur(YH), pl.BlockSpec((STAT_ROWS, tm), lambda i: (0, i)),
                   pl.BlockSpec((YP, D), lambda i: (0, 0)),
                   pl.BlockSpec((1, CONV_CH), lambda i: (0, 0)), pl.BlockSpec((1, CONV_CH), lambda i: (0, 0))],
        out_shape=[sd((S, CONV_CH), F32), sd((S, YH), BF16), sd((STAT_ROWS, S), F32), sd((YP, D), F32),
                   sd((1, CONV_CH), F32), sd((1, CONV_CH), F32)],
        compiler_params=_cp(),
    )(dh2, y, yc, ln_g, ln_b, w_outp)


def _conv_bwd(pu, dyc, conv_w, tm):
    S = pu.shape[0]
    nt = S // tm
    nh = tm // HALO

    def body(pu_ref, pup_ref, dy_ref, dyn_ref, cw_ref, dpu_ref, dcw_ref, dcb_ref, ext, ext2, dcw8):
        i = pl.program_id(0)

        @pl.when(i == 0)
        def _():
            dcw8[...] = jnp.zeros_like(dcw8)
            dcb_ref[...] = jnp.zeros_like(dcb_ref)

        not_first = (i > 0).astype(F32)
        not_last = (i < nt - 1).astype(F32)
        ext[0, 0:HALO, :] = _glu(pup_ref[...]) * not_first
        ext[0, HALO:HALO + tm, :] = _glu(pu_ref[...])
        _shifted_copies(ext)
        ext2[0, 0:tm, :] = dy_ref[...]
        ext2[0, tm:tm + HALO, :] = dyn_ref[...] * not_last
        _shifted_copies(ext2)
        dcb_ref[...] += _colsum(dy_ref[...])

        def rows_chunk(r, carry):
            r0 = pl.multiple_of(r * CONV_ROWS, CONV_ROWS)
            dyc_ = dy_ref[pl.ds(r0, CONV_ROWS), :]
            dyg = jnp.zeros((CONV_ROWS, CONV_CH), F32)
            for k in range(CONV_WIDTH):
                prod = dyc_ * _window(ext, HALO - (CONV_WIDTH - 1) + k, CONV_ROWS, r0)
                dcw8[k] += jnp.sum(prod.reshape(CONV_ROWS // 8, 8, CONV_CH), axis=0)
                dyg = dyg + cw_ref[k:k + 1, :] * _window(ext2, CONV_WIDTH - 1 - k, CONV_ROWS, r0)
            u = pu_ref[pl.ds(r0, CONV_ROWS), :]
            a_, sg = u[:, :CONV_CH], _sigmoid(u[:, CONV_CH:])
            dpu_ref[pl.ds(r0, CONV_ROWS), 0:CONV_CH] = (dyg * sg).astype(BF16)
            dpu_ref[pl.ds(r0, CONV_ROWS), CONV_CH:PU] = (dyg * a_ * sg * (1.0 - sg)).astype(BF16)
            return carry

        lax.fori_loop(0, tm // CONV_ROWS, rows_chunk, 0)

        @pl.when(i == nt - 1)
        def _():
            dcw_ref[...] = jnp.sum(dcw8[...], axis=1)

    cur = lambda w: pl.BlockSpec((tm, w), lambda i: (i, 0))
    prev = lambda w: pl.BlockSpec((HALO, w), lambda i: (jnp.maximum(i * nh - 1, 0), 0))
    nxt = lambda w: pl.BlockSpec((HALO, w), lambda i: (jnp.minimum((i + 1) * nh, S // HALO - 1), 0))
    acc = lambda r, w: pl.BlockSpec((r, w), lambda i: (0, 0))
    sd = jax.ShapeDtypeStruct
    return pl.pallas_call(
        body, name="conv_bwd", grid=(nt,),
        in_specs=[cur(PU), prev(PU), cur(CONV_CH), nxt(CONV_CH), acc(32, CONV_CH)],
        out_specs=[cur(PU), acc(32, CONV_CH), acc(1, CONV_CH)],
        out_shape=[sd((S, PU), BF16), sd((32, CONV_CH), F32), sd((1, CONV_CH), F32)],
        scratch_shapes=[pltpu.VMEM((8, tm + HALO, CONV_CH), F32), pltpu.VMEM((8, tm + HALO, CONV_CH), F32),
                        pltpu.VMEM((32, 8, CONV_CH), F32)],
        compiler_params=_cp(),
    )(pu, pu, dyc, dyc, conv_w)


def _attn_bwd(p, do, lse, delta, cosT, sinT, gq, gk, sinks, mk, mv, gqm, tm, rider=None):
    S = p.shape[0]
    M = mk.shape[0]
    nb = tm // BLK
    nt = S // tm
    nblocks = S // BLK
    W = SLOT * N_MEMH

    def body(p_ref, pp_ref, pn_ref, dy_ref, dyn_ref, lse_ref, lsen_ref, del_ref, deln_ref,
             cos_ref, cosp_ref, cosn_ref, sin_ref, sinp_ref, sinn_ref,
             gq_ref, gk_ref, sink_ref, mk_ref, mv_ref, gqm_ref,
             dp_ref, dgq_ref, dgk_ref, dgqm_ref, dsink_ref, dmk_ref, dmv_ref):
        i = pl.program_id(0)

        @pl.when(i == 0)
        def _():
            for r in (dgq_ref, dgk_ref, dgqm_ref, dsink_ref, dmk_ref, dmv_ref):
                r[...] = jnp.zeros_like(r)

        lane = _lane(tm)
        lane_e = _lane(tm + BLK)

        cos_k = jnp.concatenate([cosp_ref[...], cos_ref[...]], axis=0)
        sin_k = jnp.concatenate([sinp_ref[...], sin_ref[...]], axis=0)
        cos_q = jnp.concatenate([cos_ref[...], cosn_ref[...]], axis=0)
        sin_q = jnp.concatenate([sin_ref[...], sinn_ref[...]], axis=0)
        lse_e = jnp.concatenate([lse_ref[...], lsen_ref[...]], axis=1)
        del_e = jnp.concatenate([del_ref[...], deln_ref[...]], axis=1)
        kj = lax.broadcasted_iota(jnp.int32, (BLK, GROUP * BLK), 0)
        qi = lax.broadcasted_iota(jnp.int32, (BLK, GROUP * BLK), 1) & (BLK - 1)
        diag = kj <= qi
        offd = kj > qi
        dgq = jnp.zeros((1, SLOT), F32)
        dgk = jnp.zeros((1, SLOT), F32)
        dsink = jnp.zeros((1, SLOT), F32)
        lane1 = lax.broadcasted_iota(jnp.int32, (1, SLOT), 1)
        k_pair = jnp.concatenate([pp_ref[:, KO:KO + SLOT], p_ref[:, KO:KO + SLOT]], axis=0)
        k_pair = _pair_fwd(k_pair, gk_ref[...], cos_k, sin_k, lane_e)
        v_pair = jnp.concatenate([pp_ref[:, VO:VO + SLOT], p_ref[:, VO:VO + SLOT]], axis=0)
        k_e = [_lo(k_pair, kvh, lane_e).astype(BF16) for kvh in range(N_KV)]
        v_e = [_lo(v_pair, kvh, lane_e).astype(BF16) for kvh in range(N_KV)]
        dk = [[jnp.zeros((BLK, SLOT), F32) for _ in range(nb)] for _ in range(N_KV)]
        dv = [[jnp.zeros((BLK, SLOT), F32) for _ in range(nb)] for _ in range(N_KV)]
        q_e, do_e = [], []
        for j in range(N_Q // 2):
            js = slice(SLOT * j, SLOT * (j + 1))
            q_pair = _pair_fwd(jnp.concatenate([p_ref[:, js], pn_ref[:, js]], axis=0), gq_ref[...],
                               cos_q, sin_q, lane_e)
            do_pair = jnp.concatenate([dy_ref[:, js], dyn_ref[:, js]], axis=0).astype(F32)
            for half in range(2):
                q_e.append(_lo(q_pair, half, lane_e).astype(BF16))
                do_e.append(_lo(do_pair, half, lane_e).astype(BF16))
        dq_heads = [None] * N_Q
        for kvh in range(N_KV):
            hs = [GROUP * kvh + gi for gi in range(GROUP)]
            dq3 = [None] * nb
            for m in range(nb + 1):
                rows = slice(BLK * m, BLK * (m + 1))
                q3 = jnp.concatenate([q_e[h][rows] for h in hs], axis=0)
                do3 = jnp.concatenate([do_e[h][rows] for h in hs], axis=0)
                lb3 = jnp.concatenate([lse_e[h:h + 1, rows] for h in hs], axis=1)
                db3 = jnp.concatenate([del_e[h:h + 1, rows] for h in hs], axis=1)
                for n in (m - 1, m):
                    if n == nb:
                        continue
                    krows = slice(BLK * (n + 1), BLK * (n + 2))
                    kb, vb = k_e[kvh][krows], v_e[kvh][krows]
                    s = _dot_nt(kb, q3) * SCALE
                    mask = diag if n == m else offd
                    if n == -1:
                        mask = mask & (i > 0)
                    if m == nb:
                        mask = mask & (i < nt - 1)
                    prob = jnp.where(mask, jnp.exp(jnp.where(mask, s - lb3, NEG)), 0.0)
                    dpb = _dot_nt(vb, do3)
                    ds = (prob * (dpb - db3) * SCALE).astype(BF16)
                    if m < nb:
                        dqc = _dot_tn(ds, kb)
                        dq3[m] = dqc if dq3[m] is None else dq3[m] + dqc
                    if n >= 0:
                        dk[kvh][n] = dk[kvh][n] + _dot_nn(ds, q3)
                        dv[kvh][n] = dv[kvh][n] + _dot_nn(prob.astype(BF16), do3)
            for gi, h in enumerate(hs):
                dq_heads[h] = jnp.concatenate([dq3[m][BLK * gi:BLK * (gi + 1)] for m in range(nb)], axis=0)
                psink = jnp.exp(sink_ref[h] - lse_e[h:h + 1, 0:tm])
                dsink = dsink + jnp.where(
                    lane1 == h, -jnp.sum(psink * del_e[h:h + 1, 0:tm], axis=-1, keepdims=True), 0.0)
        for j in range(N_Q // 2):
            js = slice(SLOT * j, SLOT * (j + 1))
            dqr, prod = _pair_bwd(_pack(dq_heads[2 * j], dq_heads[2 * j + 1], lane), p_ref[:, js], gq_ref[...],
                                  cos_ref[...], sin_ref[...], lane)
            dp_ref[:, js] = dqr.astype(BF16)
            dgq = dgq + _colsum(prod)
        dk_pair = _pack(jnp.concatenate(dk[0], axis=0), jnp.concatenate(dk[1], axis=0), lane)
        dkr, prod = _pair_bwd(dk_pair, p_ref[:, KO:KO + SLOT], gk_ref[...], cos_ref[...], sin_ref[...], lane)
        dp_ref[:, KO:KO + SLOT] = dkr.astype(BF16)
        dp_ref[:, VO:VO + SLOT] = _pack(jnp.concatenate(dv[0], axis=0), jnp.concatenate(dv[1], axis=0),
                                        lane).astype(BF16)
        dgq_ref[...] += dgq
        dgk_ref[...] += _colsum(prod)
        dsink_ref[...] += dsink

        dgqm = jnp.zeros((1, SLOT), F32)
        dq_heads = []
        for hm in range(N_MEMH):
            ms = slice(SLOT * hm, SLOT * (hm + 1))
            js = slice(MO + SLOT * (hm // 2), MO + SLOT * (hm // 2 + 1))
            os_ = slice(SLOT * ((N_Q + hm) // 2), SLOT * ((N_Q + hm) // 2 + 1))
            if hm % 2 == 0:
                qm_pair = _pair_fwd(p_ref[:, js], gqm_ref[...], None, None, lane)
                do_pair = dy_ref[:, os_].astype(F32)
            qm = _lo(qm_pair, hm % 2, lane).astype(BF16)
            dob = _lo(do_pair, hm % 2, lane).astype(BF16)
            kb, vb = mk_ref[:, ms], mv_ref[:, ms]
            s = _dot_nt(kb, qm) * SCALE
            prob = jnp.exp(s - lse_ref[N_Q + hm:N_Q + hm + 1, :])
            dpb = _dot_nt(vb, dob)
            ds = (prob * (dpb - del_ref[N_Q + hm:N_Q + hm + 1, :]) * SCALE).astype(BF16)
            dq_heads.append(_dot_tn(ds, kb))
            dmk_ref[:, ms] += _dot_nn(ds, qm)
            dmv_ref[:, ms] += _dot_nn(prob.astype(BF16), dob)
            if hm % 2 == 1:
                dqr, prod = _pair_bwd(_pack(dq_heads[-2], dq_heads[-1], lane), p_ref[:, js], gqm_ref[...],
                                      None, None, lane)
                dp_ref[:, js] = dqr.astype(BF16)
                dgqm = dgqm + _colsum(prod)
        dgqm_ref[...] += dgqm

    cur = lambda w: pl.BlockSpec((tm, w), lambda i: (i, 0))
    prev = lambda w: pl.BlockSpec((BLK, w), lambda i: (jnp.maximum(i * nb - 1, 0), 0))
    nxt = lambda w: pl.BlockSpec((BLK, w), lambda i: (jnp.minimum((i + 1) * nb, nblocks - 1), 0))
    full = lambda a: pl.BlockSpec(a.shape, lambda i: (0,) * a.ndim)
    acc = lambda r, w: pl.BlockSpec((r, w), lambda i: (0, 0))
    sd = jax.ShapeDtypeStruct
    stat = pl.BlockSpec((STAT_ROWS, tm), lambda i: (0, i))
    stat_n = pl.BlockSpec((STAT_ROWS, BLK), lambda i: (0, jnp.minimum((i + 1) * nb, nblocks - 1)))
    return _call(
        body, name="attn_bwd", grid=(nt,),
        in_specs=[cur(PH), prev(PH), nxt(PH), cur(YH), nxt(YH), stat, stat_n, stat, stat_n,
                  cur(SLOT), prev(SLOT), nxt(SLOT), cur(SLOT), prev(SLOT), nxt(SLOT),
                  full(gq), full(gk), pl.BlockSpec(memory_space=pltpu.SMEM), full(mk), full(mv), full(gqm)],
        out_specs=[cur(PH), acc(1, SLOT), acc(1, SLOT), acc(1, SLOT), acc(1, SLOT), acc(M, W), acc(M, W)],
        out_shape=[sd((S, PH), BF16), sd((1, SLOT), F32), sd((1, SLOT), F32), sd((1, SLOT), F32),
                   sd((1, SLOT), F32), sd((M, W), F32), sd((M, W), F32)],
        args=[p, p, p, do, do, lse, lse, delta, delta, cosT, cosT, cosT, sinT, sinT, sinT,
              gq, gk, sinks, mk, mv, gqm],
        rider=rider)


def _proj_bwd(dpu, dph, h, dh2, g, n, w_inp, tm):
    S, D = h.shape

    def body(dpu_ref, dph_ref, h_ref, dh2_ref, g_ref, n_ref, w_ref, dh_ref, dg_ref, dw_ref):
        i = pl.program_id(0)

        @pl.when(i == 0)
        def _():
            dg_ref[...] = jnp.zeros_like(dg_ref)
            dw_ref[...] = jnp.zeros_like(dw_ref)

        dpu, dph, nv = dpu_ref[...], dph_ref[...], n_ref[...]
        dn = _dot_nn(dpu, w_ref[0:PU, :]) + _dot_nn(dph, w_ref[PU:PP, :])
        dw_ref[0:PU, :] += _dot_tn(dpu, nv)
        dw_ref[PU:PP, :] += _dot_tn(dph, nv)
        hv = h_ref[...]
        dx, dgrow = _rms_bwd(dn, hv, _rms(hv), g_ref[...])
        dh_ref[...] = dh2_ref[...] + dx
        dg_ref[...] += _colsum(dgrow)

    cur = lambda w: pl.BlockSpec((tm, w), lambda i: (i, 0))
    sd = jax.ShapeDtypeStruct
    return pl.pallas_call(
        body, name="proj_bwd", grid=(S // tm,),
        in_specs=[cur(PU), cur(PH), cur(D), cur(D), pl.BlockSpec((1, D), lambda i: (0, 0)), cur(D),
                  pl.BlockSpec((PP, D), lambda i: (0, 0))],
        out_specs=[cur(D), pl.BlockSpec((1, D), lambda i: (0, 0)), pl.BlockSpec((PP, D), lambda i: (0, 0))],
        out_shape=[sd((S, D), F32), sd((1, D), F32), sd((PP, D), F32)],
        compiler_params=_cp(),
    )(dpu, dph, h, dh2, g, n, w_inp)


def _norm_bwd(dxn, h, g, tm):
    S, D = h.shape

    def body(d_ref, h_ref, g_ref, dh_ref, dg_ref):
        @pl.when(pl.program_id(0) == 0)
        def _():
            dg_ref[...] = jnp.zeros_like(dg_ref)

        hv = h_ref[...]
        dx, dgrow = _rms_bwd(d_ref[...], hv, _rms(hv), g_ref[...])
        dh_ref[...] = dx
        dg_ref[...] += _colsum(dgrow)

    cur = pl.BlockSpec((tm, D), lambda i: (i, 0))
    vec = pl.BlockSpec((1, D), lambda i: (0, 0))
    return pl.pallas_call(
        body, name="norm_bwd", grid=(S // tm,), in_specs=[cur, cur, vec], out_specs=[cur, vec],
        out_shape=[jax.ShapeDtypeStruct((S, D), F32), jax.ShapeDtypeStruct((1, D), F32)],
        compiler_params=_cp(),
    )(dxn, h, g)


def _loss_bwd(xn, h, g, target, tm):
    S, D = h.shape

    def body(y_ref, h_ref, g_ref, t_ref, loss_ref, dh_ref, dg_ref):
        @pl.when(pl.program_id(0) == 0)
        def _():
            dg_ref[...] = jnp.zeros_like(dg_ref)
            loss_ref[...] = jnp.zeros_like(loss_ref)

        err = y_ref[...] - t_ref[...]
        part = jnp.sum(jnp.mean(err * err, axis=-1, keepdims=True), axis=0, keepdims=True)
        loss_ref[...] += 0.5 * part
        hv = h_ref[...]
        dx, dgrow = _rms_bwd(err * (1.0 / D), hv, _rms(hv), g_ref[...])
        dh_ref[...] = dx
        dg_ref[...] += _colsum(dgrow)

    cur = pl.BlockSpec((tm, D), lambda i: (i, 0))
    vec = pl.BlockSpec((1, D), lambda i: (0, 0))
    return pl.pallas_call(
        body, name="loss_bwd", grid=(S // tm,), in_specs=[cur, cur, vec, cur],
        out_specs=[pl.BlockSpec((1, SLOT), lambda i: (0, 0)), cur, vec],
        out_shape=[jax.ShapeDtypeStruct((1, SLOT), F32), jax.ShapeDtypeStruct((S, D), F32),
                   jax.ShapeDtypeStruct((1, D), F32)],
        compiler_params=_cp(),
    )(xn, h, g, target)


def _swap_with_sibling(bufs):
    nbuf = len(bufs)

    def body(*refs):
        ins, outs = refs[:nbuf], refs[nbuf:2 * nbuf]
        ssem, rsem = refs[2 * nbuf:]
        x, y, c, _ = _place()
        sends = [pltpu.make_async_remote_copy(src_ref=ins[b], dst_ref=outs[b], send_sem=ssem.at[b],
                                              recv_sem=rsem.at[b], device_id=(x, y, 1 - c), device_id_type=MESH)
                 for b in range(nbuf)]
        for cp in sends:
            cp.start()
        for cp in sends:
            cp.wait_recv()
        for cp in sends:
            cp.wait_send()

    hbm = pl.BlockSpec(memory_space=pl.ANY)
    return pl.pallas_call(
        body, name="swap_with_sibling",
        in_specs=[hbm] * nbuf, out_specs=[hbm] * nbuf,
        out_shape=[jax.ShapeDtypeStruct(b.shape, b.dtype) for b in bufs],
        scratch_shapes=[pltpu.SemaphoreType.DMA((nbuf,)), pltpu.SemaphoreType.DMA((nbuf,))],
    )(*bufs)


def _all_gather_small(buf):
    _, R, W = buf.shape

    def body(in_ref, out_ref, ssem, rsem, lsem):
        x, y, c, _ = _place()
        me = 4 * x + 2 * y + c
        local = pltpu.make_async_copy(in_ref, out_ref.at[pl.ds(me, 1)], lsem)
        local.start()

        def copy(k, block):
            fx, fy, fc = (k >> 2) & 1, (k >> 1) & 1, k & 1
            peer = (x ^ fx, y ^ fy, c ^ fc)
            return pltpu.make_async_remote_copy(
                src_ref=in_ref, dst_ref=out_ref.at[pl.ds(block, 1)], send_sem=ssem.at[k - 1],
                recv_sem=rsem.at[k - 1], device_id=peer, device_id_type=MESH)

        sends = [copy(k, me) for k in range(1, 8)]
        for cp in sends:
            cp.start()
        for k in range(1, 8):
            copy(k, me ^ k).wait_recv()
        for cp in sends:
            cp.wait_send()
        local.wait()

    hbm = pl.BlockSpec(memory_space=pl.ANY)
    return pl.pallas_call(
        body, name="all_gather_small", in_specs=[hbm], out_specs=hbm,
        out_shape=jax.ShapeDtypeStruct((8, R, W), buf.dtype),
        scratch_shapes=[pltpu.SemaphoreType.DMA((7,)), pltpu.SemaphoreType.DMA((7,)), pltpu.SemaphoreType.DMA],
    )(buf)


def _row_tile(n, cap=1024):
    for t in range(min(n, cap) // 8 * 8, 7, -8):
        if n % t == 0:
            return t
    return n


def _sum4(own, recv):
    n, rows, D = own.shape
    tr = _row_tile(rows)

    def body(o_ref, r0_ref, r1_ref, r2_ref, out_ref):
        out_ref[...] = ((o_ref[...].astype(F32) + r0_ref[...].astype(F32)) + r1_ref[...].astype(F32)) \
            + r2_ref[...].astype(F32)

    def rspec(p):
        return pl.BlockSpec((None, None, None, tr, D), lambda k, i, p=p: (p, k, 0, i, 0))

    blk = pl.BlockSpec((None, tr, D), lambda k, i: (k, i, 0))
    return pl.pallas_call(
        body, name="sum4", grid=(n, rows // tr),
        in_specs=[blk, rspec(0), rspec(1), rspec(2)], out_specs=blk,
        out_shape=jax.ShapeDtypeStruct((n, rows, D), F32),
    )(own, recv, recv, recv)


def _add2(a, b):
    rows, D = a.shape
    tr = _row_tile(rows)

    def body(a_ref, b_ref, o_ref):
        o_ref[...] = a_ref[...] + b_ref[...]

    blk = pl.BlockSpec((tr, D), lambda i: (i, 0))
    return pl.pallas_call(body, name="add2", grid=(rows // tr,), in_specs=[blk, blk], out_specs=blk,
                          out_shape=jax.ShapeDtypeStruct((rows, D), F32))(a, b)


def _adam_math(w, g, m, v):
    m = ADAM_B1 * m + (1.0 - ADAM_B1) * g
    v = ADAM_B2 * v + (1.0 - ADAM_B2) * (g * g)
    m_hat = m / (1.0 - ADAM_B1 ** ADAM_STEP)
    v_hat = v / (1.0 - ADAM_B2 ** ADAM_STEP)
    delta = -ADAM_LR * (m_hat / (jnp.sqrt(v_hat) + ADAM_EPS) + ADAM_WD * w)
    return delta, m, v


def _adam(w, g, m, v):
    rows, cols = w.shape
    tr = _row_tile(rows, 512)

    def body(w_ref, g_ref, m_ref, v_ref, d_ref, nm_ref, nv_ref):
        d, m_, v_ = _adam_math(w_ref[...], g_ref[...], m_ref[...], v_ref[...])
        d_ref[...] = d
        nm_ref[...] = m_
        nv_ref[...] = v_

    blk = pl.BlockSpec((tr, cols), lambda i: (i, 0))
    return pl.pallas_call(body, name="adam", grid=(rows // tr,), in_specs=[blk] * 4, out_specs=[blk] * 3,
                          out_shape=[jax.ShapeDtypeStruct((rows, cols), F32)] * 3)(w, g, m, v)


def _small_sum_adam(g8, w, m, v):
    _, R, W = g8.shape

    def body(g_ref, w_ref, m_ref, v_ref, go_ref, d_ref, nm_ref, nv_ref):
        g = g_ref[0]
        for k in range(1, 8):
            g = g + g_ref[k]
        go_ref[...] = g
        d, m_, v_ = _adam_math(w_ref[...], g, m_ref[...], v_ref[...])
        d_ref[...] = d
        nm_ref[...] = m_
        nv_ref[...] = v_

    return pl.pallas_call(body, name="small_sum_adam",
                          out_shape=[jax.ShapeDtypeStruct((R, W), F32)] * 4)(g8, w, m, v)


def _pad_heads_rows(w, first):
    lead, D = w.shape[:-2], w.shape[-1]
    heads = w[..., first:, :]
    n = heads.shape[-2] // HEAD_DIM
    heads = heads.reshape(lead + (n, HEAD_DIM, D))
    heads = jnp.pad(heads, [(0, 0)] * (len(lead) + 1) + [(0, SLOT - HEAD_DIM), (0, 0)])
    return jnp.concatenate([w[..., :first, :], heads.reshape(lead + (n * SLOT, D))], axis=-2)


def _unpad_heads_rows(w, first):
    lead, D = w.shape[:-2], w.shape[-1]
    heads = w[..., first:, :]
    n = heads.shape[-2] // SLOT
    heads = heads.reshape(lead + (n, SLOT, D))[..., :HEAD_DIM, :]
    return jnp.concatenate([w[..., :first, :], heads.reshape(lead + (n * HEAD_DIM, D))], axis=-2)


def _pad_vec(v):
    return jnp.pad(v, (0, SLOT - v.shape[0]))[None, :]


class _Pack:
    def __init__(self, shapes):
        self.shapes = shapes
        self.sizes = [int(functools.reduce(lambda a, b: a * b, s, 1)) for s in shapes]
        total = sum(self.sizes)
        self.rows = -(-total // (8 * SLOT)) * 8
        self.pad = self.rows * SLOT - total

    def pack(self, arrs):
        flat = jnp.concatenate([a.reshape(-1).astype(F32) for a in arrs] + [jnp.zeros((self.pad,), F32)])
        return flat.reshape(self.rows, SLOT)

    def unpack(self, buf):
        flat, out, o = buf.reshape(-1), [], 0
        for s, n in zip(self.shapes, self.sizes):
            out.append(flat[o:o + n].reshape(s))
            o += n
        return out


def kernel(x, mem, positions, ffn1_norm, ffn1_w1, ffn1_w3, ffn1_w2, mix_norm, w_in, conv_w, conv_b, conv_ln_g, conv_ln_b, swa_q_norm, swa_k_norm, swa_sinks, mem_norm, w_mem_kv, mem_q_norm, mem_k_norm, w_out, ffn2_norm, ffn2_w1, ffn2_w3, ffn2_w2, final_norm, loss_target, m_ffn1_norm, m_ffn1_w1, m_ffn1_w3, m_ffn1_w2, m_mix_norm, m_w_in, m_conv_w, m_conv_b, m_conv_ln_g, m_conv_ln_b, m_swa_q_norm, m_swa_k_norm, m_swa_sinks, m_mem_norm, m_w_mem_kv, m_mem_q_norm, m_mem_k_norm, m_w_out, m_ffn2_norm, m_ffn2_w1, m_ffn2_w3, m_ffn2_w2, m_final_norm, v_ffn1_norm, v_ffn1_w1, v_ffn1_w3, v_ffn1_w2, v_mix_norm, v_w_in, v_conv_w, v_conv_b, v_conv_ln_g, v_conv_ln_b, v_swa_q_norm, v_swa_k_norm, v_swa_sinks, v_mem_norm, v_w_mem_kv, v_mem_q_norm, v_mem_k_norm, v_w_out, v_ffn2_norm, v_ffn2_w1, v_ffn2_w3, v_ffn2_w2, v_final_norm):
    names = ['ffn1_norm', 'ffn1_w1', 'ffn1_w3', 'ffn1_w2', 'mix_norm', 'w_in', 'conv_w', 'conv_b', 'conv_ln_g',
             'conv_ln_b', 'swa_q_norm', 'swa_k_norm', 'swa_sinks', 'mem_norm', 'w_mem_kv', 'mem_q_norm',
             'mem_k_norm', 'w_out', 'ffn2_norm', 'ffn2_w1', 'ffn2_w3', 'ffn2_w2', 'final_norm']
    loc = locals()
    W = {n: loc[n] for n in names}
    M1 = {n: loc['m_' + n] for n in names}
    V1 = {n: loc['v_' + n] for n in names}

    S, D = x.shape[1], x.shape[2]
    L = ffn1_norm.shape[0]
    Fs = ffn1_w1.shape[2]
    F = 4 * Fs
    Mlen = mem.shape[1]
    cw_sh = conv_w.shape[2]
    tm = 512 if S >= 2048 else 256
    tf = 1408 if F % 1408 == 0 else 256
    tfw = 256
    tmw = 2048 if S >= 2048 else 256
    x0 = x[0]
    mem0 = mem[0]
    target = loss_target[0]
    my_chip = 2 * lax.axis_index("x") + lax.axis_index("y")

    mkv_rows = w_mem_kv.shape[1] * MEM_KV // D
    r_in, r_out = D_IN // 4, D_MIX // 4
    rm = r_in + r_out + mkv_rows

    def ffn_group(w1, w3, w2):
        return jnp.stack([w1.T, w3.T, w2]).astype(BF16).reshape(3, 1, Fs, D)

    groups = []
    for l in range(L):
        groups.append(ffn_group(ffn1_w1[l], ffn1_w3[l], ffn1_w2[l]))
        groups.append(jnp.concatenate([w_in[l].T, w_out[l], w_mem_kv[l].reshape(mkv_rows, D)])
                      .astype(BF16).reshape(1, 1, rm, D))
        groups.append(ffn_group(ffn2_w1[l], ffn2_w3[l], ffn2_w2[l]))
    gathered = [None] * len(groups)
    cw_rows = -(-(L * CONV_WIDTH) // 8) * 8
    cw_pad = jnp.pad(conv_w.reshape(L * CONV_WIDTH, cw_sh), ((0, cw_rows - L * CONV_WIDTH), (0, SLOT - cw_sh)))
    gathered[0], cw_g = _run_rider(_Gather([groups[0], cw_pad.reshape(1, 1, cw_rows, SLOT)]), "all_gather_first")
    conv_wF = cw_g[0, :, :L * CONV_WIDTH, :cw_sh].reshape(4, L, CONV_WIDTH, cw_sh)
    conv_wF = jnp.moveaxis(conv_wF, 0, 2).reshape(L, CONV_WIDTH, 4 * cw_sh)
    conv_wP = jnp.pad(conv_wF, ((0, 0), (0, 32 - CONV_WIDTH), (0, 0)))

    def gather_rider(j):
        want = [k for k in ([1, 2] if j == 0 else [j + 2]) if k < len(groups)]
        return (_Gather([groups[k] for k in want]), want) if want else (None, want)

    def mix_weights(l):
        g = gathered[3 * l + 1][0]
        w_inp = g[:, :r_in].reshape(D_IN, D)
        w_outp = g[:, r_in:r_in + r_out].reshape(D_MIX, D)
        w_mkvp = jnp.pad(g[:, r_in + r_out:].reshape(D, 2 * N_MEMH, HEAD_DIM),
                         ((0, 0), (0, 0), (0, SLOT - HEAD_DIM))).reshape(D, 2 * N_MEMH * SLOT)
        return w_inp, w_outp, w_mkvp

    inv_freq = ROPE_THETA ** (-jnp.arange(0, HEAD_DIM, 2, dtype=F32) / HEAD_DIM)
    invf = jnp.tile(inv_freq, SLOT // (HEAD_DIM // 2))[None, :]
    cosT, sinT = _rope_tables(positions.reshape(S, 1), invf, tm)

    row = lambda a, l: a[l][None, :]
    sinks_p = jnp.pad(swa_sinks, ((0, 0), (0, 8 - N_Q)))

    saved = []
    xin = x0
    xn = None
    for l in range(L):
        wf1 = gathered[3 * l].reshape(3, F, D)
        rider, want = gather_rider(3 * l)
        (h1, a1, b1, t1), got = _ffn_fwd(xin, row(ffn1_norm, l), wf1, None, tm, tf, rider=rider)
        for k, g in zip(want, got or []):
            gathered[k] = g
        w_inp, w_outp, w_mkvp = mix_weights(l)
        pu, p, n2 = _proj_fwd(h1, row(mix_norm, l), w_inp, tm)
        gk_m = _pad_vec(mem_k_norm[l])
        nm, mraw, mk, mv = _mem_kv_fwd(mem0, row(mem_norm, l), w_mkvp, gk_m)
        twice = lambda v: jnp.tile(v, 2)[None, :]
        gq, gk, gqm = twice(swa_q_norm[l]), twice(swa_k_norm[l]), twice(mem_q_norm[l])
        rider, want = gather_rider(3 * l + 1)
        (h2, y, yc, lse), got = _mixer_fwd(pu, p, h1, cosT, sinT, conv_wP[l], row(conv_b, l), row(conv_ln_g, l),
                                           row(conv_ln_b, l), gq, gk, sinks_p[l], mk, mv, gqm, w_outp, tm,
                                           rider=rider)
        for k, g in zip(want, got or []):
            gathered[k] = g
        wf2 = gathered[3 * l + 2].reshape(3, F, D)
        rider, want = gather_rider(3 * l + 2)
        (h3, a2, b2, t2, xn), got = _ffn_fwd(h2, row(ffn2_norm, l), wf2, row(final_norm, l), tm, tf, rider=rider)
        for k, g in zip(want, got or []):
            gathered[k] = g
        saved.append(dict(xin=xin, h1=h1, a1=a1, b1=b1, pu=pu, p=p, n2=n2, nm=nm, mraw=mraw, mk=mk, mv=mv, gk_m=gk_m,
                          gq=gq, gk=gk, gqm=gqm, h2=h2, y=y, yc=yc, lse=lse, h3=h3, a2=a2, b2=b2, t1=t1, t2=t2,
                          wf1=wf1, wf2=wf2, w_inp=w_inp, w_outp=w_outp, w_mkvp=w_mkvp))
        xin = xn

    G = {n: [None] * L for n in names}
    ffn_bufs = [None] * (2 * L)
    mix_bufs = [None] * L
    ffn_recv = [None] * (2 * L)
    mix_recv = [None] * L
    dxn = None
    loss_part = None
    for l in reversed(range(L)):
        sv = saved[l]
        if l == L - 1:
            loss_part, dh3, G['final_norm'][l] = _loss_bwd(xn, sv['h3'], row(final_norm, l), target, tm)
        else:
            dh3, G['final_norm'][l] = _norm_bwd(dxn, sv['h3'], row(final_norm, l), tm)
        rider = _Scatter([ffn_bufs[2 * l + 2]]) if l < L - 1 else None
        (dh2, G['ffn2_norm'][l], da, db, n, dy), got = _ffn_bwd_act(
            dh3, sv['h2'], row(ffn2_norm, l), sv['a2'], sv['b2'], sv['wf2'], tm, tf, rider=rider)
        if got:
            ffn_recv[2 * l + 2] = got[0]
        ffn_bufs[2 * l + 1] = _ffn_bwd_w(da, db, sv['t2'], n, dy, tmw, tfw).reshape(3, 4, Fs, D)
        dyc, do, delta, dwo, G['conv_ln_g'][l], G['conv_ln_b'][l] = _outproj_bwd(
            dh2, sv['y'], sv['yc'], row(conv_ln_g, l), row(conv_ln_b, l), sv['w_outp'], tm)
        (dph, dgq, dgk, dgqm, dsink, dmk, dmv), got = _attn_bwd(
            sv['p'], do, sv['lse'], delta, cosT, sinT, sv['gq'], sv['gk'], sinks_p[l],
            sv['mk'], sv['mv'], sv['gqm'], tm, rider=_Scatter([ffn_bufs[2 * l + 1]]))
        ffn_recv[2 * l + 1] = got[0]
        dpu, dcw, G['conv_b'][l] = _conv_bwd(sv['pu'], dyc, conv_wP[l], tm)
        dwm, G['mem_norm'][l], dgk_m = _mem_kv_bwd(dmk, dmv, sv['mraw'], sv['nm'], mem0, row(mem_norm, l),
                                                   sv['w_mkvp'], sv['gk_m'])
        dh1, G['mix_norm'][l], dwi = _proj_bwd(dpu, dph, sv['h1'], dh2, row(mix_norm, l), sv['n2'], sv['w_inp'], tm)
        dwiT = dwi.reshape(4, r_in, D)
        dwoF = dwo.reshape(4, r_out, D)
        dwmF = dwm.reshape(D, 2 * N_MEMH, SLOT)[:, :, :HEAD_DIM].reshape(4, mkv_rows, D)
        mix_bufs[l] = jnp.concatenate([dwiT, dwoF, dwmF], axis=1).astype(BF16).reshape(1, 4, rm, D)
        (dxl, G['ffn1_norm'][l], da, db, n, dy), got = _ffn_bwd_act(
            dh1, sv['xin'], row(ffn1_norm, l), sv['a1'], sv['b1'], sv['wf1'], tm, tf,
            rider=_Scatter([mix_bufs[l]]))
        mix_recv[l] = got[0]
        ffn_bufs[2 * l] = _ffn_bwd_w(da, db, sv['t1'], n, dy, tmw, tfw).reshape(3, 4, Fs, D)
        dxn = dxl
        G['conv_w'][l] = dcw[:CONV_WIDTH]
        G['swa_q_norm'][l] = dgq[0, :HEAD_DIM] + dgq[0, HEAD_DIM:]
        G['swa_k_norm'][l] = dgk[0, :HEAD_DIM] + dgk[0, HEAD_DIM:]
        G['mem_q_norm'][l] = dgqm[0, :HEAD_DIM] + dgqm[0, HEAD_DIM:]
        G['mem_k_norm'][l] = dgk_m[0, :HEAD_DIM]
        G['swa_sinks'][l] = dsink[0, :N_Q]
    ffn_recv[0] = _run_rider(_Scatter([ffn_bufs[0]]), "scatter_last")[0]
    grad_x = dxn[None]
    loss = lax.psum(loss_part[0, 0], AXES)

    parts = []
    for b, r in zip(ffn_bufs + mix_bufs, ffn_recv + mix_recv):
        own = lax.dynamic_index_in_dim(b, my_chip, axis=1, keepdims=False)
        parts.append(_sum4(own, r))
    theirs = _swap_with_sibling(parts)
    gsum = [_add2(a.reshape(-1, D), b.reshape(-1, D)).reshape(a.shape) for a, b in zip(parts, theirs)]
    for l in range(L):
        for f, pre in enumerate(('ffn1', 'ffn2')):
            g3 = gsum[2 * l + f]
            G[pre + '_w1'][l] = g3[0].T
            G[pre + '_w3'][l] = g3[1].T
            G[pre + '_w2'][l] = g3[2]
        gm = gsum[2 * L + l][0]
        G['w_in'][l] = gm[:r_in].T
        G['w_out'][l] = gm[r_in:r_in + r_out]
        G['w_mem_kv'][l] = gm[r_in + r_out:].reshape(w_mem_kv.shape[1], MEM_KV)

    small = ['ffn1_norm', 'mix_norm', 'conv_b', 'conv_ln_g', 'conv_ln_b', 'swa_q_norm', 'swa_k_norm', 'swa_sinks',
             'mem_norm', 'mem_q_norm', 'mem_k_norm', 'ffn2_norm', 'final_norm']
    gsmall = [jnp.stack([G[n][l].reshape(-1) for l in range(L)]) for n in small]
    gcw = jnp.stack(G['conv_w'])
    cw_cols = 4 * cw_sh
    full_of = lambda a: lax.dynamic_update_slice(jnp.zeros((L, CONV_WIDTH, cw_cols), F32), a, (0, 0, my_chip * cw_sh))
    pk = _Pack([W[n].shape for n in small] + [(L, CONV_WIDTH, cw_cols)])
    g8 = _all_gather_small(pk.pack(gsmall + [gcw])[None])
    outs4 = _small_sum_adam(g8, pk.pack([W[n] for n in small] + [full_of(conv_w)]),
                            pk.pack([M1[n] for n in small] + [full_of(m_conv_w)]),
                            pk.pack([V1[n] for n in small] + [full_of(v_conv_w)]))
    un = [pk.unpack(o) for o in outs4]
    grads, deltas, new_m, new_v = {}, {}, {}, {}
    for k, n in enumerate(small):
        grads[n], deltas[n], new_m[n], new_v[n] = un[0][k], un[1][k], un[2][k], un[3][k]
    mine = lambda a: lax.dynamic_slice(a, (0, 0, my_chip * cw_sh), (L, CONV_WIDTH, cw_sh))
    grads['conv_w'], deltas['conv_w'], new_m['conv_w'], new_v['conv_w'] = [mine(u[-1]) for u in un]

    for n in ('ffn1_w1', 'ffn1_w3', 'ffn1_w2', 'w_in', 'w_mem_kv', 'w_out', 'ffn2_w1', 'ffn2_w3', 'ffn2_w2'):
        g = jnp.stack(G[n])
        shp = W[n].shape
        v2 = lambda a: a.reshape(-1, shp[-1])
        d_, m_, v_ = _adam(v2(W[n]), v2(g), v2(M1[n]), v2(V1[n]))
        grads[n], deltas[n], new_m[n], new_v[n] = g, d_.reshape(shp), m_.reshape(shp), v_.reshape(shp)

    return (loss, grad_x, *[grads[n] for n in names], *[deltas[n] for n in names],
            *[new_m[n] for n in names], *[new_v[n] for n in names])
```

```python
import functools

import jax
import jax.numpy as jnp
from jax import lax
from jax.experimental import pallas as pl
from jax.experimental.pallas import tpu as pltpu

F32 = jnp.float32
BF16 = jnp.bfloat16
MESH = pl.DeviceIdType.MESH
AXES = ("x", "y", "c")

EPS = 1e-6
HEAD_DIM = 64
SLOT = 128
CONV_CH = 384
CONV_WIDTH = 31
N_Q, N_KV, N_MEMH = 6, 2, 4
GROUP = N_Q // N_KV
BLK = 128
HALO = 32
CONV_ROWS = 64
FFN_CHUNK = 256
ROPE_THETA = 10000.0
SCALE = HEAD_DIM ** -0.5
NEG = -1e30

N_HEADS_IN = N_Q + 2 * N_KV + N_MEMH
PU = 2 * CONV_CH
PH = HEAD_DIM * N_HEADS_IN
PP = PU + PH
QO = 0
KO = QO + HEAD_DIM * N_Q
VO = KO + HEAD_DIM * N_KV
MO = VO + HEAD_DIM * N_KV
NH = N_Q + N_MEMH
STAT_ROWS = 16
YH = HEAD_DIM * NH
YP = CONV_CH + YH
YS = CONV_CH
YM = YS + HEAD_DIM * N_Q
D_IN = PP
D_MIX = YP
MEM_KV = 2 * HEAD_DIM * N_MEMH

ADAM_LR, ADAM_B1, ADAM_B2, ADAM_EPS, ADAM_WD, ADAM_STEP = 0.001, 0.9, 0.999, 1e-08, 0.01, 10

VMEM_LIMIT_MB = 56


def _cp(mb=VMEM_LIMIT_MB):
    return pltpu.CompilerParams(vmem_limit_bytes=mb * 1024 * 1024)


def _dot_nn(a, b):
    return lax.dot_general(a, b, (((1,), (0,)), ((), ())), preferred_element_type=F32)


def _dot_nt(a, b):
    return lax.dot_general(a, b, (((1,), (1,)), ((), ())), preferred_element_type=F32)


def _dot_tn(a, b):
    return lax.dot_general(a, b, (((0,), (0,)), ((), ())), preferred_element_type=F32)


def _sigmoid(x):
    return 1.0 / (1.0 + jnp.exp(-x))


def _rms(x):
    return lax.rsqrt(jnp.mean(x * x, axis=-1, keepdims=True) + EPS)


def _rms_bwd(dn, x, r, g):
    xhat = x * r
    dxhat = dn * g
    dx = r * (dxhat - xhat * jnp.mean(dxhat * xhat, axis=-1, keepdims=True))
    return dx, dn * xhat


def _colsum(v):
    return jnp.sum(v, axis=0, keepdims=True)


def _lane(n):
    return lax.broadcasted_iota(jnp.int32, (n, SLOT), 1)


def _slot_rms(xs):
    return lax.rsqrt(jnp.sum(xs * xs, axis=-1, keepdims=True) * (1.0 / HEAD_DIM) + EPS)


def _slot_norm(xs, g):
    return xs * _slot_rms(xs) * g


def _slot_norm_bwd(dout, xs, g):
    r = _slot_rms(xs)
    xhat = xs * r
    dxhat = dout * g
    dx = r * (dxhat - xhat * (jnp.sum(dxhat * xhat, axis=-1, keepdims=True) * (1.0 / HEAD_DIM)))
    return dx, dout * xhat


def _halves(v, lane):
    lo = jnp.sum(jnp.where(lane < HEAD_DIM, v, 0.0), axis=-1, keepdims=True)
    hi = jnp.sum(jnp.where(lane < HEAD_DIM, 0.0, v), axis=-1, keepdims=True)
    return jnp.where(lane < HEAD_DIM, lo, hi)


def _pair_rms(x, lane):
    return lax.rsqrt(_halves(x * x, lane) * (1.0 / HEAD_DIM) + EPS)


def _pair_partner(v, lane):
    return jnp.where((lane & (HEAD_DIM - 1)) < HEAD_DIM // 2,
                     pltpu.roll(v, SLOT - HEAD_DIM // 2, 1), pltpu.roll(v, HEAD_DIM // 2, 1))


def _pair_fwd(x, g2, cosv, sinv, lane):
    xn = x * _pair_rms(x, lane) * g2
    if cosv is None:
        return xn
    return xn * cosv + _pair_partner(xn, lane) * sinv


def _pair_bwd(dout, x, g2, cosv, sinv, lane):
    if cosv is not None:
        dout = dout * cosv + _pair_partner(dout * sinv, lane)
    r = _pair_rms(x, lane)
    xhat = x * r
    dxhat = dout * g2
    dx = r * (dxhat - xhat * (_halves(dxhat * xhat, lane) * (1.0 / HEAD_DIM)))
    return dx, dout * xhat


def _lo(x, half, lane):
    if half:
        x = pltpu.roll(x, HEAD_DIM, 1)
    return jnp.where(lane < HEAD_DIM, x, 0.0)


def _pack(even, odd, lane):
    return jnp.where(lane < HEAD_DIM, even, pltpu.roll(odd, HEAD_DIM, 1))


def _place():
    x, y, c = lax.axis_index("x"), lax.axis_index("y"), lax.axis_index("c")
    chips = [(1 - x, y), (x, 1 - y), (1 - x, 1 - y)]
    return x, y, c, chips


class _Gather:
    tag = "_gather"

    def __init__(self, bufs):
        self.bufs = list(bufs)
        nb = len(self.bufs)
        self.out_shape = [jax.ShapeDtypeStruct((b.shape[0], 4) + b.shape[2:], b.dtype) for b in self.bufs]
        self.sems = [pltpu.SemaphoreType.DMA((3 * nb,)), pltpu.SemaphoreType.DMA((3 * nb,)),
                     pltpu.SemaphoreType.DMA((nb,))]

    def _copies(self, ins, outs, sems):
        ssem, rsem, lsem = sems
        nb = len(self.bufs)
        x, y, c, chips = _place()
        mine = 2 * x + y

        def copy(b, p, shard):
            return pltpu.make_async_remote_copy(
                src_ref=ins[b], dst_ref=outs[b].at[:, pl.ds(shard, 1)],
                send_sem=ssem.at[3 * b + p], recv_sem=rsem.at[3 * b + p],
                device_id=(chips[p][0], chips[p][1], c), device_id_type=MESH)

        local = [pltpu.make_async_copy(ins[b], outs[b].at[:, pl.ds(mine, 1)], lsem.at[b]) for b in range(nb)]
        sends = [copy(b, p, mine) for b in range(nb) for p in range(3)]
        recvs = [copy(b, p, 2 * chips[p][0] + chips[p][1]) for b in range(nb) for p in range(3)]
        return local, sends, recvs

    def start(self, ins, outs, sems):
        local, sends, _ = self._copies(ins, outs, sems)
        for cp in local + sends:
            cp.start()

    def wait(self, ins, outs, sems):
        local, sends, recvs = self._copies(ins, outs, sems)
        for cp in recvs:
            cp.wait_recv()
        for cp in sends:
            cp.wait_send()
        for cp in local:
            cp.wait()


class _Scatter:
    tag = "_scatter"

    def __init__(self, bufs):
        self.bufs = list(bufs)
        nb = len(self.bufs)
        self.out_shape = [jax.ShapeDtypeStruct((3, b.shape[0], 1) + b.shape[2:], b.dtype) for b in self.bufs]
        self.sems = [pltpu.SemaphoreType.DMA((3 * nb,)), pltpu.SemaphoreType.DMA((3 * nb,))]

    def _copies(self, ins, outs, sems):
        ssem, rsem = sems
        x, y, c, chips = _place()

        def copy(b, p):
            shard = 2 * chips[p][0] + chips[p][1]
            return pltpu.make_async_remote_copy(
                src_ref=ins[b].at[:, pl.ds(shard, 1)], dst_ref=outs[b].at[p],
                send_sem=ssem.at[3 * b + p], recv_sem=rsem.at[3 * b + p],
                device_id=(chips[p][0], chips[p][1], c), device_id_type=MESH)

        return [copy(b, p) for b in range(len(self.bufs)) for p in range(3)]

    def start(self, ins, outs, sems):
        for cp in self._copies(ins, outs, sems):
            cp.start()

    def wait(self, ins, outs, sems):
        cps = self._copies(ins, outs, sems)
        for cp in cps:
            cp.wait_recv()
        for cp in cps:
            cp.wait_send()


def _run_rider(rider, name):
    nb = len(rider.bufs)

    def body(*refs):
        ins, outs, sems = refs[:nb], refs[nb:2 * nb], refs[2 * nb:]
        rider.start(ins, outs, sems)
        rider.wait(ins, outs, sems)

    hbm = pl.BlockSpec(memory_space=pl.ANY)
    return pl.pallas_call(body, name=name, in_specs=[hbm] * nb, out_specs=[hbm] * nb,
                          out_shape=rider.out_shape, scratch_shapes=rider.sems)(*rider.bufs)


def _call(body, *, name, grid, in_specs, out_specs, out_shape, args, scratch=(), rider=None):
    if rider is None:
        outs = pl.pallas_call(body, name=name, grid=grid, in_specs=list(in_specs), out_specs=list(out_specs),
                              out_shape=list(out_shape), scratch_shapes=list(scratch),
                              compiler_params=_cp())(*args)
        return list(outs), None
    n_in, n_out, n_scr, nb = len(in_specs), len(out_specs), len(scratch), len(rider.bufs)

    def wrapped(*refs):
        cuts = [n_in, nb, n_out, nb, n_scr]
        parts, o = [], 0
        for n in cuts:
            parts.append(refs[o:o + n])
            o += n
        ins, rin, outs, rout, scr = parts
        sems = refs[o:]
        ids = [pl.program_id(k) for k in range(len(grid))]
        first = functools.reduce(jnp.logical_and, [i == 0 for i in ids])
        last = functools.reduce(jnp.logical_and, [i == n - 1 for i, n in zip(ids, grid)])

        @pl.when(first)
        def _():
            rider.start(rin, rout, sems)

        body(*ins, *outs, *scr)

        @pl.when(last)
        def _():
            rider.wait(rin, rout, sems)

    hbm = pl.BlockSpec(memory_space=pl.ANY)
    res = pl.pallas_call(
        wrapped, name=name + rider.tag, grid=grid,
        in_specs=list(in_specs) + [hbm] * nb, out_specs=list(out_specs) + [hbm] * nb,
        out_shape=list(out_shape) + rider.out_shape, scratch_shapes=list(scratch) + rider.sems,
        compiler_params=_cp())(*args, *rider.bufs)
    return list(res[:n_out]), list(res[n_out:])


def _rope_tables(pos, invf, tm):
    S = pos.shape[0]

    def body(pos_ref, f_ref, cos_ref, sin_ref):
        ang = pos_ref[...].astype(F32) * f_ref[...]
        lane = _lane(tm)
        cos_ref[...] = jnp.cos(ang)
        s = jnp.sin(ang)
        sin_ref[...] = jnp.where((lane & (HEAD_DIM - 1)) < HEAD_DIM // 2, -s, s)

    return pl.pallas_call(
        body, name="rope_tables", grid=(S // tm,),
        in_specs=[pl.BlockSpec((tm, 1), lambda i: (i, 0)), pl.BlockSpec((1, SLOT), lambda i: (0, 0))],
        out_specs=[pl.BlockSpec((tm, SLOT), lambda i: (i, 0))] * 2,
        out_shape=[jax.ShapeDtypeStruct((S, SLOT), F32)] * 2,
    )(pos, invf)


def _ffn_fwd(x, g, wf, gfin, tm, tf, rider=None):
    S, D = x.shape
    F = wf[0].shape[0]
    nf = F // tf
    final = gfin is not None

    chunks = [(c, min(FFN_CHUNK, tf - c)) for c in range(0, tf, FFN_CHUNK)]

    def body(*refs):
        if final:
            x_ref, g_ref, w1_ref, w3_ref, w2_ref, gf_ref, h_ref, a_ref, b_ref, t_ref, xn_ref, n_scr, acc = refs
        else:
            x_ref, g_ref, w1_ref, w3_ref, w2_ref, h_ref, a_ref, b_ref, t_ref, n_scr, acc = refs
        j = pl.program_id(1)

        @pl.when(j == 0)
        def _():
            xv = x_ref[...]
            n_scr[...] = (xv * _rms(xv) * g_ref[...]).astype(BF16)
            acc[...] = jnp.zeros_like(acc)

        n = n_scr[...]
        for c0, cw in chunks:
            cols = slice(c0, c0 + cw)
            a = _dot_nt(n, w1_ref[cols, :])
            b = _dot_nt(n, w3_ref[cols, :])
            a_ref[:, cols] = a.astype(BF16)
            b_ref[:, cols] = b.astype(BF16)
            t_ref[:, cols] = (a * _sigmoid(a) * b).astype(BF16)
        acc[...] += _dot_nn(t_ref[...], w2_ref[...])

        @pl.when(j == nf - 1)
        def _():
            h = x_ref[...] + 0.5 * acc[...]
            h_ref[...] = h
            if final:
                xn_ref[...] = h * _rms(h) * gf_ref[...]

    def wspec(k):
        return pl.BlockSpec((tf, D), lambda i, j: (j, 0))

    row = pl.BlockSpec((tm, D), lambda i, j: (i, 0))
    vec = pl.BlockSpec((1, D), lambda i, j: (0, 0))
    act = pl.BlockSpec((tm, tf), lambda i, j: (i, j))
    in_specs = [row, vec, wspec(0), wspec(1), wspec(2)] + ([vec] if final else [])
    out_specs = [row, act, act, act] + ([row] if final else [])
    out_shape = [jax.ShapeDtypeStruct((S, D), F32)] + [jax.ShapeDtypeStruct((S, F), BF16)] * 3 \
        + ([jax.ShapeDtypeStruct((S, D), F32)] if final else [])
    args = [x, g, *wf] + ([gfin] if final else [])
    return _call(body, name="ffn_fwd_final" if final else "ffn_fwd", grid=(S // tm, nf),
                 in_specs=in_specs, out_specs=out_specs, out_shape=out_shape, args=args,
                 scratch=[pltpu.VMEM((tm, D), BF16), pltpu.VMEM((tm, D), F32)], rider=rider)


def _ffn_bwd_act(dh, x, g, a, b, wf, tm, tf, rider=None):
    S, D = x.shape
    F = wf[0].shape[0]
    nf = F // tf

    chunks = [(c, min(FFN_CHUNK, tf - c)) for c in range(0, tf, FFN_CHUNK)]

    def body(dh_ref, x_ref, g_ref, a_ref, b_ref, w1_ref, w3_ref, w2_ref,
             dx_ref, dg_ref, da_ref, db_ref, n_ref, dy_ref, acc):
        i, j = pl.program_id(0), pl.program_id(1)

        @pl.when(j == 0)
        def _():
            xv = x_ref[...]
            n_ref[...] = (xv * _rms(xv) * g_ref[...]).astype(BF16)
            dy_ref[...] = (0.5 * dh_ref[...]).astype(BF16)
            acc[...] = jnp.zeros_like(acc)

            @pl.when(i == 0)
            def _():
                dg_ref[...] = jnp.zeros_like(dg_ref)

        dyv = dy_ref[...]
        for c0, cw in chunks:
            cols = slice(c0, c0 + cw)
            av = a_ref[:, cols].astype(F32)
            bv = b_ref[:, cols].astype(F32)
            sg = _sigmoid(av)
            dt = _dot_nt(dyv, w2_ref[cols, :])
            db_ref[:, cols] = (dt * (av * sg)).astype(BF16)
            da_ref[:, cols] = (dt * bv * (sg * (1.0 + av * (1.0 - sg)))).astype(BF16)
        acc[...] += _dot_nn(da_ref[...], w1_ref[...]) + _dot_nn(db_ref[...], w3_ref[...])

        @pl.when(j == nf - 1)
        def _():
            xv = x_ref[...]
            dx, dgrow = _rms_bwd(acc[...], xv, _rms(xv), g_ref[...])
            dx_ref[...] = dh_ref[...] + dx
            dg_ref[...] += _colsum(dgrow)

    def wspec(k):
        return pl.BlockSpec((tf, D), lambda i, j: (j, 0))

    row = pl.BlockSpec((tm, D), lambda i, j: (i, 0))
    vec = pl.BlockSpec((1, D), lambda i, j: (0, 0))
    act = pl.BlockSpec((tm, tf), lambda i, j: (i, j))
    sd = lambda shp, dt: jax.ShapeDtypeStruct(shp, dt)
    return _call(body, name="ffn_bwd_act", grid=(S // tm, nf),
                 in_specs=[row, row, vec, act, act, wspec(0), wspec(1), wspec(2)],
                 out_specs=[row, vec, act, act, row, row],
                 out_shape=[sd((S, D), F32), sd((1, D), F32), sd((S, F), BF16), sd((S, F), BF16),
                            sd((S, D), BF16), sd((S, D), BF16)],
                 args=[dh, x, g, a, b, *wf], scratch=[pltpu.VMEM((tm, D), F32)], rider=rider)


def _ffn_bwd_w(da, db, t, n, dy, tm, tf):
    S, F = da.shape
    D = n.shape[1]
    nt = S // tm

    def body(da_ref, db_ref, t_ref, n_ref, dy_ref, out_ref, acc):
        i = pl.program_id(1)

        @pl.when(i == 0)
        def _():
            acc[...] = jnp.zeros_like(acc)

        nv = n_ref[...]
        acc[0] += _dot_tn(da_ref[...], nv)
        acc[1] += _dot_tn(db_ref[...], nv)
        acc[2] += _dot_tn(t_ref[...], dy_ref[...])

        @pl.when(i == nt - 1)
        def _():
            out_ref[...] = acc[...].astype(BF16)

    act = pl.BlockSpec((tm, tf), lambda j, i: (i, j))
    row = pl.BlockSpec((tm, D), lambda j, i: (i, 0))
    return pl.pallas_call(
        body, name="ffn_bwd_w", grid=(F // tf, nt),
        in_specs=[act, act, act, row, row],
        out_specs=pl.BlockSpec((3, tf, D), lambda j, i: (0, j, 0)),
        out_shape=jax.ShapeDtypeStruct((3, F, D), BF16),
        scratch_shapes=[pltpu.VMEM((3, tf, D), F32)],
        compiler_params=_cp(),
    )(da, db, t, n, dy)


def _proj_fwd(h, g, w_inp, tm):
    S, D = h.shape

    def body(h_ref, g_ref, w_ref, pu_ref, ph_ref, n_ref):
        hv = h_ref[...]
        n = (hv * _rms(hv) * g_ref[...]).astype(BF16)
        n_ref[...] = n
        pu_ref[...] = _dot_nt(n, w_ref[0:PU, :])
        ph_ref[...] = _dot_nt(n, w_ref[PU:PP, :])

    cur = lambda w: pl.BlockSpec((tm, w), lambda i: (i, 0))
    return pl.pallas_call(
        body, name="proj_fwd", grid=(S // tm,),
        in_specs=[cur(D), pl.BlockSpec((1, D), lambda i: (0, 0)), pl.BlockSpec((PP, D), lambda i: (0, 0))],
        out_specs=[cur(PU), cur(PH), cur(D)],
        out_shape=[jax.ShapeDtypeStruct((S, PU), F32), jax.ShapeDtypeStruct((S, PH), F32),
                   jax.ShapeDtypeStruct((S, D), BF16)],
        compiler_params=_cp(),
    )(h, g, w_inp)


def _glu(u):
    return u[:, :CONV_CH] * _sigmoid(u[:, CONV_CH:2 * CONV_CH])


def _shifted_copies(ext8):
    n = ext8.shape[1]
    for b in range(1, 8):
        ext8[b, 0:n - 8, :] = ext8[0, b:b + n - 8, :]


def _window(ext8, off, rows, r0=0):
    return ext8[off % 8, pl.ds(r0 + (off - off % 8), rows), :]


def _layer_norm_stats(yc):
    mu = jnp.mean(yc, axis=-1, keepdims=True)
    d = yc - mu
    rstd = lax.rsqrt(jnp.mean(d * d, axis=-1, keepdims=True) + EPS)
    return d * rstd, rstd


def _mem_kv_fwd(mem, g, w_mkvp, gk):
    M, D = mem.shape
    W = SLOT * N_MEMH

    def body(mem_ref, g_ref, w_ref, gk_ref, nm_ref, raw_ref, mk_ref, mv_ref):
        mv_ = mem_ref[...]
        nm = (mv_ * _rms(mv_) * g_ref[...]).astype(BF16)
        nm_ref[...] = nm
        raw = _dot_nn(nm, w_ref[...])
        raw_ref[...] = raw
        for hh in range(N_MEMH):
            sl = slice(SLOT * hh, SLOT * (hh + 1))
            mk_ref[:, sl] = _slot_norm(raw[:, sl], gk_ref[...]).astype(BF16)
        mv_ref[...] = raw[:, W:].astype(BF16)

    sd = jax.ShapeDtypeStruct
    return pl.pallas_call(
        body, name="mem_kv_fwd",
        out_shape=[sd((M, D), BF16), sd((M, 2 * W), F32), sd((M, W), BF16), sd((M, W), BF16)],
        compiler_params=_cp(),
    )(mem, g, w_mkvp, gk)


def _mem_kv_bwd(dmk, dmv, raw, nm, mem, g, w_mkvp, gk):
    M, D = mem.shape
    W = SLOT * N_MEMH

    def body(dmk_ref, dmv_ref, raw_ref, nm_ref, mem_ref, g_ref, w_ref, gk_ref, dw_ref, dg_ref, dgk_ref, draw):
        dgk = jnp.zeros((1, SLOT), F32)
        for hh in range(N_MEMH):
            sl = slice(SLOT * hh, SLOT * (hh + 1))
            dx, prod = _slot_norm_bwd(dmk_ref[:, sl], raw_ref[:, sl], gk_ref[...])
            draw[:, sl] = dx.astype(BF16)
            dgk = dgk + _colsum(prod)
        dgk_ref[...] = dgk
        draw[:, W:] = dmv_ref[...].astype(BF16)
        dr = draw[...]
        dw_ref[...] = _dot_tn(nm_ref[...], dr)
        dnm = _dot_nt(dr, w_ref[...])
        mv_ = mem_ref[...]
        dg_ref[...] = _colsum(dnm * (mv_ * _rms(mv_)))

    sd = jax.ShapeDtypeStruct
    return pl.pallas_call(
        body, name="mem_kv_bwd",
        out_shape=[sd((D, 2 * W), F32), sd((1, D), F32), sd((1, SLOT), F32)],
        scratch_shapes=[pltpu.VMEM((M, 2 * W), BF16)],
        compiler_params=_cp(),
    )(dmk, dmv, raw, nm, mem, g, w_mkvp, gk)


def _mixer_fwd(pu, ph, h, cosT, sinT, conv_w, conv_b, ln_g, ln_b, gq, gk, sinks, mk, mv, gqm, w_outp, tm, rider=None):
    S, D = h.shape
    M = mk.shape[0]
    nb = tm // BLK
    nblocks = S // BLK

    def body(pu_ref, pup_ref, p_ref, ph_ref, h_ref, cos_ref, cosh_ref, sin_ref, sinh_ref, cw_ref, cb_ref,
             lg_ref, lb_ref, gq_ref, gk_ref, sink_ref, mk_ref, mv_ref, gqm_ref, wo_ref,
             h2_ref, y_ref, yc_ref, lse_ref, ext, y_scr):
        i = pl.program_id(0)
        not_first = (i > 0).astype(F32)
        lane = _lane(tm)
        lane_e = _lane(tm + BLK)

        ext[0, 0:HALO, :] = _glu(pup_ref[...]) * not_first
        ext[0, HALO:HALO + tm, :] = _glu(pu_ref[...])
        _shifted_copies(ext)

        def rows_chunk(r, carry):
            r0 = pl.multiple_of(r * CONV_ROWS, CONV_ROWS)
            yc = jnp.zeros((CONV_ROWS, CONV_CH), F32) + cb_ref[...]
            for k in range(CONV_WIDTH):
                yc = yc + cw_ref[k:k + 1, :] * _window(ext, HALO - (CONV_WIDTH - 1) + k, CONV_ROWS, r0)
            yc_ref[pl.ds(r0, CONV_ROWS), :] = yc
            z, _ = _layer_norm_stats(yc)
            ln = z * lg_ref[...] + lb_ref[...]
            y_scr[pl.ds(r0, CONV_ROWS), 0:CONV_CH] = (ln * _sigmoid(ln)).astype(BF16)
            return carry

        lax.fori_loop(0, tm // CONV_ROWS, rows_chunk, 0)

        cos_e = jnp.concatenate([cosh_ref[...], cos_ref[...]], axis=0)
        sin_e = jnp.concatenate([sinh_ref[...], sin_ref[...]], axis=0)
        qi = lax.broadcasted_iota(jnp.int32, (GROUP * BLK, 2 * BLK), 0) & (BLK - 1)
        kj = lax.broadcasted_iota(jnp.int32, (GROUP * BLK, 2 * BLK), 1)
        band = (kj > qi) & (kj <= qi + BLK)
        band0 = band & ((kj >= BLK) | (i > 0))
        lse = jnp.zeros((tm, SLOT), F32)
        k_pair = jnp.concatenate([ph_ref[:, KO:KO + SLOT], p_ref[:, KO:KO + SLOT]], axis=0)
        k_pair = _pair_fwd(k_pair, gk_ref[...], cos_e, sin_e, lane_e)
        v_pair = jnp.concatenate([ph_ref[:, VO:VO + SLOT], p_ref[:, VO:VO + SLOT]], axis=0)
        k_e = [_lo(k_pair, kvh, lane_e).astype(BF16) for kvh in range(N_KV)]
        v_e = [_lo(v_pair, kvh, lane_e).astype(BF16) for kvh in range(N_KV)]
        q_lo = []
        for j in range(N_Q // 2):
            q_pair = _pair_fwd(p_ref[:, QO + SLOT * j:QO + SLOT * (j + 1)], gq_ref[...],
                               cos_ref[...], sin_ref[...], lane)
            q_lo += [_lo(q_pair, 0, lane).astype(BF16), _lo(q_pair, 1, lane).astype(BF16)]
        outs = [[] for _ in range(N_Q)]
        lses = [[] for _ in range(N_Q)]
        for kvh in range(N_KV):
            hs = [GROUP * kvh + gi for gi in range(GROUP)]
            sink3 = jnp.concatenate([jnp.full((BLK, 1), sink_ref[h], F32) for h in hs], axis=0)
            for m in range(nb):
                rows = slice(BLK * m, BLK * (m + 1))
                win = slice(BLK * m, BLK * (m + 2))
                q3 = jnp.concatenate([q_lo[h][rows] for h in hs], axis=0)
                s = _dot_nt(q3, k_e[kvh][win]) * SCALE
                s = jnp.where(band0 if m == 0 else band, s, NEG)
                mx = jnp.maximum(jnp.max(s, axis=-1, keepdims=True), sink3)
                e = jnp.exp(s - mx)
                den = jnp.sum(e, axis=-1, keepdims=True) + jnp.exp(sink3 - mx)
                o3 = _dot_nn((e / den).astype(BF16), v_e[kvh][win])
                l3 = mx + jnp.log(den)
                for gi, h in enumerate(hs):
                    outs[h].append(o3[BLK * gi:BLK * (gi + 1)])
                    lses[h].append(l3[BLK * gi:BLK * (gi + 1)])
        for h in range(N_Q):
            lse = jnp.where(lane == h, jnp.concatenate(lses[h], axis=0), lse)
        for j in range(N_Q // 2):
            y_scr[:, YS + SLOT * j:YS + SLOT * (j + 1)] = _pack(
                jnp.concatenate(outs[2 * j], axis=0), jnp.concatenate(outs[2 * j + 1], axis=0), lane).astype(BF16)

        heads = []
        for hm in range(N_MEMH):
            ms = slice(SLOT * hm, SLOT * (hm + 1))
            if hm % 2 == 0:
                qm_pair = _pair_fwd(p_ref[:, MO + SLOT * (hm // 2):MO + SLOT * (hm // 2 + 1)], gqm_ref[...],
                                    None, None, lane)
            s = _dot_nt(_lo(qm_pair, hm % 2, lane).astype(BF16), mk_ref[:, ms]) * SCALE
            mx = jnp.max(s, axis=-1, keepdims=True)
            e = jnp.exp(s - mx)
            den = jnp.sum(e, axis=-1, keepdims=True)
            heads.append(_dot_nn((e / den).astype(BF16), mv_ref[:, ms]))
            lse = jnp.where(lane == N_Q + hm, mx + jnp.log(den), lse)
            if hm % 2 == 1:
                y_scr[:, YM + SLOT * (hm // 2):YM + SLOT * (hm // 2 + 1)] = _pack(heads[-2], heads[-1], lane).astype(BF16)
        lse_ref[...] = lse.T[0:STAT_ROWS, :]

        yv = y_scr[...]
        y_ref[...] = yv
        h2_ref[...] = h_ref[...] + _dot_nn(yv, wo_ref[...])

    cur = lambda w: pl.BlockSpec((tm, w), lambda i: (i, 0))
    prev = lambda w: pl.BlockSpec((BLK, w), lambda i: (jnp.maximum(i * nb - 1, 0), 0))
    full = lambda a: pl.BlockSpec(a.shape, lambda i: (0,) * a.ndim)
    sd = jax.ShapeDtypeStruct
    prev32 = pl.BlockSpec((HALO, PU), lambda i: (jnp.maximum(i * (tm // HALO) - 1, 0), 0))
    return _call(
        body, name="mixer_fwd", grid=(S // tm,),
        in_specs=[cur(PU), prev32, cur(PH), prev(PH), cur(D), cur(SLOT), prev(SLOT), cur(SLOT), prev(SLOT),
                  full(conv_w), full(conv_b), full(ln_g), full(ln_b), full(gq), full(gk),
                  pl.BlockSpec(memory_space=pltpu.SMEM), full(mk), full(mv), full(gqm), full(w_outp)],
        out_specs=[cur(D), cur(YP), cur(CONV_CH), pl.BlockSpec((STAT_ROWS, tm), lambda i: (0, i))],
        out_shape=[sd((S, D), F32), sd((S, YP), BF16), sd((S, CONV_CH), F32), sd((STAT_ROWS, S), F32)],
        args=[pu, pu, ph, ph, h, cosT, cosT, sinT, sinT, conv_w, conv_b, ln_g, ln_b, gq, gk, sinks, mk, mv, gqm,
              w_outp],
        scratch=[pltpu.VMEM((8, tm + HALO, CONV_CH), F32), pltpu.VMEM((tm, YP), BF16)], rider=rider)


def _outproj_bwd(dh2, y, yc, ln_g, ln_b, w_outp, tm):
    S, D = dh2.shape

    def body(dh_ref, y_ref, yc_ref, lg_ref, lb_ref, wo_ref, dyc_ref, do_ref, del_ref, dwo_ref, dlg_ref, dlb_ref):
        i = pl.program_id(0)

        @pl.when(i == 0)
        def _():
            dwo_ref[...] = jnp.zeros_like(dwo_ref)
            dlg_ref[...] = jnp.zeros_like(dlg_ref)
            dlb_ref[...] = jnp.zeros_like(dlb_ref)

        dhb = dh_ref[...].astype(BF16)
        yv = y_ref[...]
        dy = _dot_nt(dhb, wo_ref[...])
        dwo_ref[...] += _dot_tn(yv, dhb)

        z, rstd = _layer_norm_stats(yc_ref[...])
        ln = z * lg_ref[...] + lb_ref[...]
        sg = _sigmoid(ln)
        dln = dy[:, 0:CONV_CH] * (sg * (1.0 + ln * (1.0 - sg)))
        dlg_ref[...] += _colsum(dln * z)
        dlb_ref[...] += _colsum(dln)
        dz = dln * lg_ref[...]
        dyc_ref[...] = rstd * (dz - jnp.mean(dz, axis=-1, keepdims=True)
                               - z * jnp.mean(dz * z, axis=-1, keepdims=True))
        do_ref[...] = dy[:, CONV_CH:].astype(BF16)

        lane = _lane(tm)
        delta = jnp.zeros((tm, SLOT), F32)
        for j in range(NH // 2):
            sl = slice(YS + SLOT * j, YS + SLOT * (j + 1))
            prod = dy[:, sl] * yv[:, sl].astype(F32)
            lo = jnp.sum(jnp.where(lane < HEAD_DIM, prod, 0.0), axis=-1, keepdims=True)
            hi = jnp.sum(jnp.where(lane < HEAD_DIM, 0.0, prod), axis=-1, keepdims=True)
            delta = jnp.where(lane == 2 * j, lo, jnp.where(lane == 2 * j + 1, hi, delta))
        del_ref[...] = delta.T[0:STAT_ROWS, :]

    cur = lambda w: pl.BlockSpec((tm, w), lambda i: (i, 0))
    full = lambda a: pl.BlockSpec(a.shape, lambda i: (0,) * a.ndim)
    sd = jax.ShapeDtypeStruct
    return pl.pallas_call(
        body, name="outproj_bwd", grid=(S // tm,),
        in_specs=[cur(D), cur(YP), cur(CONV_CH), full(ln_g), full(ln_b), full(w_outp)],
        out_specs=[cur(CONV_CH), cur(YH), pl.BlockSpec((STAT_ROWS, tm), lambda i: (0, i)),
                   pl.BlockSpec((YP, D), lambda i: (0, 0)),
                   pl.BlockSpec((1, CONV_CH), lambda i: (0, 0)), pl.BlockSpec((1, CONV_CH), lambda i: (0, 0))],
        out_shape=[sd((S, CONV_CH), F32), sd((S, YH), BF16), sd((STAT_ROWS, S), F32), sd((YP, D), F32),
                   sd((1, CONV_CH), F32), sd((1, CONV_CH), F32)],
        compiler_params=_cp(),
    )(dh2, y, yc, ln_g, ln_b, w_outp)


def _conv_bwd(pu, dyc, conv_w, tm):
    S = pu.shape[0]
    nt = S // tm
    nh = tm // HALO

    def body(pu_ref, pup_ref, dy_ref, dyn_ref, cw_ref, dpu_ref, dcw_ref, dcb_ref, ext, ext2, dcw8):
        i = pl.program_id(0)

        @pl.when(i == 0)
        def _():
            dcw8[...] = jnp.zeros_like(dcw8)
            dcb_ref[...] = jnp.zeros_like(dcb_ref)

        not_first = (i > 0).astype(F32)
        not_last = (i < nt - 1).astype(F32)
        ext[0, 0:HALO, :] = _glu(pup_ref[...]) * not_first
        ext[0, HALO:HALO + tm, :] = _glu(pu_ref[...])
        _shifted_copies(ext)
        ext2[0, 0:tm, :] = dy_ref[...]
        ext2[0, tm:tm + HALO, :] = dyn_ref[...] * not_last
        _shifted_copies(ext2)
        dcb_ref[...] += _colsum(dy_ref[...])

        def rows_chunk(r, carry):
            r0 = pl.multiple_of(r * CONV_ROWS, CONV_ROWS)
            dyc_ = dy_ref[pl.ds(r0, CONV_ROWS), :]
            dyg = jnp.zeros((CONV_ROWS, CONV_CH), F32)
            for k in range(CONV_WIDTH):
                prod = dyc_ * _window(ext, HALO - (CONV_WIDTH - 1) + k, CONV_ROWS, r0)
                dcw8[k] += jnp.sum(prod.reshape(CONV_ROWS // 8, 8, CONV_CH), axis=0)
                dyg = dyg + cw_ref[k:k + 1, :] * _window(ext2, CONV_WIDTH - 1 - k, CONV_ROWS, r0)
            u = pu_ref[pl.ds(r0, CONV_ROWS), :]
            a_, sg = u[:, :CONV_CH], _sigmoid(u[:, CONV_CH:])
            dpu_ref[pl.ds(r0, CONV_ROWS), 0:CONV_CH] = (dyg * sg).astype(BF16)
            dpu_ref[pl.ds(r0, CONV_ROWS), CONV_CH:PU] = (dyg * a_ * sg * (1.0 - sg)).astype(BF16)
            return carry

        lax.fori_loop(0, tm // CONV_ROWS, rows_chunk, 0)

        @pl.when(i == nt - 1)
        def _():
            dcw_ref[...] = jnp.sum(dcw8[...], axis=1)

    cur = lambda w: pl.BlockSpec((tm, w), lambda i: (i, 0))
    prev = lambda w: pl.BlockSpec((HALO, w), lambda i: (jnp.maximum(i * nh - 1, 0), 0))
    nxt = lambda w: pl.BlockSpec((HALO, w), lambda i: (jnp.minimum((i + 1) * nh, S // HALO - 1), 0))
    acc = lambda r, w: pl.BlockSpec((r, w), lambda i: (0, 0))
    sd = jax.ShapeDtypeStruct
    return pl.pallas_call(
        body, name="conv_bwd", grid=(nt,),
        in_specs=[cur(PU), prev(PU), cur(CONV_CH), nxt(CONV_CH), acc(32, CONV_CH)],
        out_specs=[cur(PU), acc(32, CONV_CH), acc(1, CONV_CH)],
        out_shape=[sd((S, PU), BF16), sd((32, CONV_CH), F32), sd((1, CONV_CH), F32)],
        scratch_shapes=[pltpu.VMEM((8, tm + HALO, CONV_CH), F32), pltpu.VMEM((8, tm + HALO, CONV_CH), F32),
                        pltpu.VMEM((32, 8, CONV_CH), F32)],
        compiler_params=_cp(),
    )(pu, pu, dyc, dyc, conv_w)


def _attn_bwd(p, do, lse, delta, cosT, sinT, gq, gk, sinks, mk, mv, gqm, tm, rider=None):
    S = p.shape[0]
    M = mk.shape[0]
    nb = tm // BLK
    nt = S // tm
    nblocks = S // BLK
    W = SLOT * N_MEMH

    def body(p_ref, pp_ref, pn_ref, dy_ref, dyn_ref, lse_ref, lsen_ref, del_ref, deln_ref,
             cos_ref, cosp_ref, cosn_ref, sin_ref, sinp_ref, sinn_ref,
             gq_ref, gk_ref, sink_ref, mk_ref, mv_ref, gqm_ref,
             dp_ref, dgq_ref, dgk_ref, dgqm_ref, dsink_ref, dmk_ref, dmv_ref):
        i = pl.program_id(0)

        @pl.when(i == 0)
        def _():
            for r in (dgq_ref, dgk_ref, dgqm_ref, dsink_ref, dmk_ref, dmv_ref):
                r[...] = jnp.zeros_like(r)

        lane = _lane(tm)
        lane_e = _lane(tm + BLK)

        cos_k = jnp.concatenate([cosp_ref[...], cos_ref[...]], axis=0)
        sin_k = jnp.concatenate([sinp_ref[...], sin_ref[...]], axis=0)
        cos_q = jnp.concatenate([cos_ref[...], cosn_ref[...]], axis=0)
        sin_q = jnp.concatenate([sin_ref[...], sinn_ref[...]], axis=0)
        lse_e = jnp.concatenate([lse_ref[...], lsen_ref[...]], axis=1)
        del_e = jnp.concatenate([del_ref[...], deln_ref[...]], axis=1)
        kj = lax.broadcasted_iota(jnp.int32, (BLK, GROUP * BLK), 0)
        qi = lax.broadcasted_iota(jnp.int32, (BLK, GROUP * BLK), 1) & (BLK - 1)
        diag = kj <= qi
        offd = kj > qi
        dgq = jnp.zeros((1, SLOT), F32)
        dgk = jnp.zeros((1, SLOT), F32)
        dsink = jnp.zeros((1, SLOT), F32)
        lane1 = lax.broadcasted_iota(jnp.int32, (1, SLOT), 1)
        k_pair = jnp.concatenate([pp_ref[:, KO:KO + SLOT], p_ref[:, KO:KO + SLOT]], axis=0)
        k_pair = _pair_fwd(k_pair, gk_ref[...], cos_k, sin_k, lane_e)
        v_pair = jnp.concatenate([pp_ref[:, VO:VO + SLOT], p_ref[:, VO:VO + SLOT]], axis=0)
        k_e = [_lo(k_pair, kvh, lane_e).astype(BF16) for kvh in range(N_KV)]
        v_e = [_lo(v_pair, kvh, lane_e).astype(BF16) for kvh in range(N_KV)]
        dk = [[jnp.zeros((BLK, SLOT), F32) for _ in range(nb)] for _ in range(N_KV)]
        dv = [[jnp.zeros((BLK, SLOT), F32) for _ in range(nb)] for _ in range(N_KV)]
        q_e, do_e = [], []
        for j in range(N_Q // 2):
            js = slice(SLOT * j, SLOT * (j + 1))
            q_pair = _pair_fwd(jnp.concatenate([p_ref[:, js], pn_ref[:, js]], axis=0), gq_ref[...],
                               cos_q, sin_q, lane_e)
            do_pair = jnp.concatenate([dy_ref[:, js], dyn_ref[:, js]], axis=0).astype(F32)
            for half in range(2):
                q_e.append(_lo(q_pair, half, lane_e).astype(BF16))
                do_e.append(_lo(do_pair, half, lane_e).astype(BF16))
        dq_heads = [None] * N_Q
        for kvh in range(N_KV):
            hs = [GROUP * kvh + gi for gi in range(GROUP)]
            dq3 = [None] * nb
            for m in range(nb + 1):
                rows = slice(BLK * m, BLK * (m + 1))
                q3 = jnp.concatenate([q_e[h][rows] for h in hs], axis=0)
                do3 = jnp.concatenate([do_e[h][rows] for h in hs], axis=0)
                lb3 = jnp.concatenate([lse_e[h:h + 1, rows] for h in hs], axis=1)
                db3 = jnp.concatenate([del_e[h:h + 1, rows] for h in hs], axis=1)
                for n in (m - 1, m):
                    if n == nb:
                        continue
                    krows = slice(BLK * (n + 1), BLK * (n + 2))
                    kb, vb = k_e[kvh][krows], v_e[kvh][krows]
                    s = _dot_nt(kb, q3) * SCALE
                    mask = diag if n == m else offd
                    if n == -1:
                        mask = mask & (i > 0)
                    if m == nb:
                        mask = mask & (i < nt - 1)
                    prob = jnp.where(mask, jnp.exp(jnp.where(mask, s - lb3, NEG)), 0.0)
                    dpb = _dot_nt(vb, do3)
                    ds = (prob * (dpb - db3) * SCALE).astype(BF16)
                    if m < nb:
                        dqc = _dot_tn(ds, kb)
                        dq3[m] = dqc if dq3[m] is None else dq3[m] + dqc
                    if n >= 0:
                        dk[kvh][n] = dk[kvh][n] + _dot_nn(ds, q3)
                        dv[kvh][n] = dv[kvh][n] + _dot_nn(prob.astype(BF16), do3)
            for gi, h in enumerate(hs):
                dq_heads[h] = jnp.concatenate([dq3[m][BLK * gi:BLK * (gi + 1)] for m in range(nb)], axis=0)
                psink = jnp.exp(sink_ref[h] - lse_e[h:h + 1, 0:tm])
                dsink = dsink + jnp.where(
                    lane1 == h, -jnp.sum(psink * del_e[h:h + 1, 0:tm], axis=-1, keepdims=True), 0.0)
        for j in range(N_Q // 2):
            js = slice(SLOT * j, SLOT * (j + 1))
            dqr, prod = _pair_bwd(_pack(dq_heads[2 * j], dq_heads[2 * j + 1], lane), p_ref[:, js], gq_ref[...],
                                  cos_ref[...], sin_ref[...], lane)
            dp_ref[:, js] = dqr.astype(BF16)
            dgq = dgq + _colsum(prod)
        dk_pair = _pack(jnp.concatenate(dk[0], axis=0), jnp.concatenate(dk[1], axis=0), lane)
        dkr, prod = _pair_bwd(dk_pair, p_ref[:, KO:KO + SLOT], gk_ref[...], cos_ref[...], sin_ref[...], lane)
        dp_ref[:, KO:KO + SLOT] = dkr.astype(BF16)
        dp_ref[:, VO:VO + SLOT] = _pack(jnp.concatenate(dv[0], axis=0), jnp.concatenate(dv[1], axis=0),
                                        lane).astype(BF16)
        dgq_ref[...] += dgq
        dgk_ref[...] += _colsum(prod)
        dsink_ref[...] += dsink

        dgqm = jnp.zeros((1, SLOT), F32)
        dq_heads = []
        for hm in range(N_MEMH):
            ms = slice(SLOT * hm, SLOT * (hm + 1))
            js = slice(MO + SLOT * (hm // 2), MO + SLOT * (hm // 2 + 1))
            os_ = slice(SLOT * ((N_Q + hm) // 2), SLOT * ((N_Q + hm) // 2 + 1))
            if hm % 2 == 0:
                qm_pair = _pair_fwd(p_ref[:, js], gqm_ref[...], None, None, lane)
                do_pair = dy_ref[:, os_].astype(F32)
            qm = _lo(qm_pair, hm % 2, lane).astype(BF16)
            dob = _lo(do_pair, hm % 2, lane).astype(BF16)
            kb, vb = mk_ref[:, ms], mv_ref[:, ms]
            s = _dot_nt(kb, qm) * SCALE
            prob = jnp.exp(s - lse_ref[N_Q + hm:N_Q + hm + 1, :])
            dpb = _dot_nt(vb, dob)
            ds = (prob * (dpb - del_ref[N_Q + hm:N_Q + hm + 1, :]) * SCALE).astype(BF16)
            dq_heads.append(_dot_tn(ds, kb))
            dmk_ref[:, ms] += _dot_nn(ds, qm)
            dmv_ref[:, ms] += _dot_nn(prob.astype(BF16), dob)
            if hm % 2 == 1:
                dqr, prod = _pair_bwd(_pack(dq_heads[-2], dq_heads[-1], lane), p_ref[:, js], gqm_ref[...],
                                      None, None, lane)
                dp_ref[:, js] = dqr.astype(BF16)
                dgqm = dgqm + _colsum(prod)
        dgqm_ref[...] += dgqm

    cur = lambda w: pl.BlockSpec((tm, w), lambda i: (i, 0))
    prev = lambda w: pl.BlockSpec((BLK, w), lambda i: (jnp.maximum(i * nb - 1, 0), 0))
    nxt = lambda w: pl.BlockSpec((BLK, w), lambda i: (jnp.minimum((i + 1) * nb, nblocks - 1), 0))
    full = lambda a: pl.BlockSpec(a.shape, lambda i: (0,) * a.ndim)
    acc = lambda r, w: pl.BlockSpec((r, w), lambda i: (0, 0))
    sd = jax.ShapeDtypeStruct
    stat = pl.BlockSpec((STAT_ROWS, tm), lambda i: (0, i))
    stat_n = pl.BlockSpec((STAT_ROWS, BLK), lambda i: (0, jnp.minimum((i + 1) * nb, nblocks - 1)))
    return _call(
        body, name="attn_bwd", grid=(nt,),
        in_specs=[cur(PH), prev(PH), nxt(PH), cur(YH), nxt(YH), stat, stat_n, stat, stat_n,
                  cur(SLOT), prev(SLOT), nxt(SLOT), cur(SLOT), prev(SLOT), nxt(SLOT),
                  full(gq), full(gk), pl.BlockSpec(memory_space=pltpu.SMEM), full(mk), full(mv), full(gqm)],
        out_specs=[cur(PH), acc(1, SLOT), acc(1, SLOT), acc(1, SLOT), acc(1, SLOT), acc(M, W), acc(M, W)],
        out_shape=[sd((S, PH), BF16), sd((1, SLOT), F32), sd((1, SLOT), F32), sd((1, SLOT), F32),
                   sd((1, SLOT), F32), sd((M, W), F32), sd((M, W), F32)],
        args=[p, p, p, do, do, lse, lse, delta, delta, cosT, cosT, cosT, sinT, sinT, sinT,
              gq, gk, sinks, mk, mv, gqm],
        rider=rider)


def _proj_bwd(dpu, dph, h, dh2, g, n, w_inp, tm):
    S, D = h.shape

    def body(dpu_ref, dph_ref, h_ref, dh2_ref, g_ref, n_ref, w_ref, dh_ref, dg_ref, dw_ref):
        i = pl.program_id(0)

        @pl.when(i == 0)
        def _():
            dg_ref[...] = jnp.zeros_like(dg_ref)
            dw_ref[...] = jnp.zeros_like(dw_ref)

        dpu, dph, nv = dpu_ref[...], dph_ref[...], n_ref[...]
        dn = _dot_nn(dpu, w_ref[0:PU, :]) + _dot_nn(dph, w_ref[PU:PP, :])
        dw_ref[0:PU, :] += _dot_tn(dpu, nv)
        dw_ref[PU:PP, :] += _dot_tn(dph, nv)
        hv = h_ref[...]
        dx, dgrow = _rms_bwd(dn, hv, _rms(hv), g_ref[...])
        dh_ref[...] = dh2_ref[...] + dx
        dg_ref[...] += _colsum(dgrow)

    cur = lambda w: pl.BlockSpec((tm, w), lambda i: (i, 0))
    sd = jax.ShapeDtypeStruct
    return pl.pallas_call(
        body, name="proj_bwd", grid=(S // tm,),
        in_specs=[cur(PU), cur(PH), cur(D), cur(D), pl.BlockSpec((1, D), lambda i: (0, 0)), cur(D),
                  pl.BlockSpec((PP, D), lambda i: (0, 0))],
        out_specs=[cur(D), pl.BlockSpec((1, D), lambda i: (0, 0)), pl.BlockSpec((PP, D), lambda i: (0, 0))],
        out_shape=[sd((S, D), F32), sd((1, D), F32), sd((PP, D), F32)],
        compiler_params=_cp(),
    )(dpu, dph, h, dh2, g, n, w_inp)


def _norm_bwd(dxn, h, g, tm):
    S, D = h.shape

    def body(d_ref, h_ref, g_ref, dh_ref, dg_ref):
        @pl.when(pl.program_id(0) == 0)
        def _():
            dg_ref[...] = jnp.zeros_like(dg_ref)

        hv = h_ref[...]
        dx, dgrow = _rms_bwd(d_ref[...], hv, _rms(hv), g_ref[...])
        dh_ref[...] = dx
        dg_ref[...] += _colsum(dgrow)

    cur = pl.BlockSpec((tm, D), lambda i: (i, 0))
    vec = pl.BlockSpec((1, D), lambda i: (0, 0))
    return pl.pallas_call(
        body, name="norm_bwd", grid=(S // tm,), in_specs=[cur, cur, vec], out_specs=[cur, vec],
        out_shape=[jax.ShapeDtypeStruct((S, D), F32), jax.ShapeDtypeStruct((1, D), F32)],
        compiler_params=_cp(),
    )(dxn, h, g)


def _loss_bwd(xn, h, g, target, tm):
    S, D = h.shape

    def body(y_ref, h_ref, g_ref, t_ref, loss_ref, dh_ref, dg_ref):
        @pl.when(pl.program_id(0) == 0)
        def _():
            dg_ref[...] = jnp.zeros_like(dg_ref)
            loss_ref[...] = jnp.zeros_like(loss_ref)

        err = y_ref[...] - t_ref[...]
        part = jnp.sum(jnp.mean(err * err, axis=-1, keepdims=True), axis=0, keepdims=True)
        loss_ref[...] += 0.5 * part
        hv = h_ref[...]
        dx, dgrow = _rms_bwd(err * (1.0 / D), hv, _rms(hv), g_ref[...])
        dh_ref[...] = dx
        dg_ref[...] += _colsum(dgrow)

    cur = pl.BlockSpec((tm, D), lambda i: (i, 0))
    vec = pl.BlockSpec((1, D), lambda i: (0, 0))
    return pl.pallas_call(
        body, name="loss_bwd", grid=(S // tm,), in_specs=[cur, cur, vec, cur],
        out_specs=[pl.BlockSpec((1, SLOT), lambda i: (0, 0)), cur, vec],
        out_shape=[jax.ShapeDtypeStruct((1, SLOT), F32), jax.ShapeDtypeStruct((S, D), F32),
                   jax.ShapeDtypeStruct((1, D), F32)],
        compiler_params=_cp(),
    )(xn, h, g, target)


def _swap_with_sibling(bufs):
    nbuf = len(bufs)

    def body(*refs):
        ins, outs = refs[:nbuf], refs[nbuf:2 * nbuf]
        ssem, rsem = refs[2 * nbuf:]
        x, y, c, _ = _place()
        sends = [pltpu.make_async_remote_copy(src_ref=ins[b], dst_ref=outs[b], send_sem=ssem.at[b],
                                              recv_sem=rsem.at[b], device_id=(x, y, 1 - c), device_id_type=MESH)
                 for b in range(nbuf)]
        for cp in sends:
            cp.start()
        for cp in sends:
            cp.wait_recv()
        for cp in sends:
            cp.wait_send()

    hbm = pl.BlockSpec(memory_space=pl.ANY)
    return pl.pallas_call(
        body, name="swap_with_sibling",
        in_specs=[hbm] * nbuf, out_specs=[hbm] * nbuf,
        out_shape=[jax.ShapeDtypeStruct(b.shape, b.dtype) for b in bufs],
        scratch_shapes=[pltpu.SemaphoreType.DMA((nbuf,)), pltpu.SemaphoreType.DMA((nbuf,))],
    )(*bufs)


def _all_gather_small(buf):
    _, R, W = buf.shape

    def body(in_ref, out_ref, ssem, rsem, lsem):
        x, y, c, _ = _place()
        me = 4 * x + 2 * y + c
        local = pltpu.make_async_copy(in_ref, out_ref.at[pl.ds(me, 1)], lsem)
        local.start()

        def copy(k, block):
            fx, fy, fc = (k >> 2) & 1, (k >> 1) & 1, k & 1
            peer = (x ^ fx, y ^ fy, c ^ fc)
            return pltpu.make_async_remote_copy(
                src_ref=in_ref, dst_ref=out_ref.at[pl.ds(block, 1)], send_sem=ssem.at[k - 1],
                recv_sem=rsem.at[k - 1], device_id=peer, device_id_type=MESH)

        sends = [copy(k, me) for k in range(1, 8)]
        for cp in sends:
            cp.start()
        for k in range(1, 8):
            copy(k, me ^ k).wait_recv()
        for cp in sends:
            cp.wait_send()
        local.wait()

    hbm = pl.BlockSpec(memory_space=pl.ANY)
    return pl.pallas_call(
        body, name="all_gather_small", in_specs=[hbm], out_specs=hbm,
        out_shape=jax.ShapeDtypeStruct((8, R, W), buf.dtype),
        scratch_shapes=[pltpu.SemaphoreType.DMA((7,)), pltpu.SemaphoreType.DMA((7,)), pltpu.SemaphoreType.DMA],
    )(buf)


def _row_tile(n, cap=1024):
    for t in range(min(n, cap) // 8 * 8, 7, -8):
        if n % t == 0:
            return t
    return n


def _sum4(own, recv):
    n, rows, D = own.shape
    tr = _row_tile(rows)

    def body(o_ref, r0_ref, r1_ref, r2_ref, out_ref):
        out_ref[...] = ((o_ref[...].astype(F32) + r0_ref[...].astype(F32)) + r1_ref[...].astype(F32)) \
            + r2_ref[...].astype(F32)

    def rspec(p):
        return pl.BlockSpec((None, None, None, tr, D), lambda k, i, p=p: (p, k, 0, i, 0))

    blk = pl.BlockSpec((None, tr, D), lambda k, i: (k, i, 0))
    return pl.pallas_call(
        body, name="sum4", grid=(n, rows // tr),
        in_specs=[blk, rspec(0), rspec(1), rspec(2)], out_specs=blk,
        out_shape=jax.ShapeDtypeStruct((n, rows, D), F32),
    )(own, recv, recv, recv)


def _adam_math(w, g, m, v):
    m = ADAM_B1 * m + (1.0 - ADAM_B1) * g
    v = ADAM_B2 * v + (1.0 - ADAM_B2) * (g * g)
    m_hat = m / (1.0 - ADAM_B1 ** ADAM_STEP)
    v_hat = v / (1.0 - ADAM_B2 ** ADAM_STEP)
    delta = -ADAM_LR * (m_hat / (jnp.sqrt(v_hat) + ADAM_EPS) + ADAM_WD * w)
    return delta, m, v


def _adam_fused(w, m, v, parts, theirs, sel, row0, nrows, transpose):
    L, R, C = w.shape
    D = parts[0].shape[2]
    if transpose:
        assert (R, C) == (D, nrows) and row0 == 0
        t = 256 if R % 256 == 0 else R
        gspec = lambda k: pl.BlockSpec((None, nrows, t), lambda l, c, k=k: (sel, 0, jnp.where(l == k, c, 0)))
    else:
        assert (R, C) == (nrows, D)
        t = _row_tile(nrows if row0 == 0 else _gcd(row0, nrows), 512)
        gspec = lambda k: pl.BlockSpec(
            (None, t, D), lambda l, c, k=k: (sel, row0 // t + jnp.where(l == k, c, 0), 0))

    def body(*refs):
        w_ref, m_ref, v_ref = refs[:3]
        p_refs, q_refs = refs[3:3 + L], refs[3 + L:3 + 2 * L]
        g_ref, d_ref, nm_ref, nv_ref, g_scr = refs[3 + 2 * L:]
        l = pl.program_id(0)
        for k in range(L):
            @pl.when(l == k)
            def _(k=k):
                s = p_refs[k][...] + q_refs[k][...]
                g_scr[...] = s.T if transpose else s

        g = g_scr[...]
        g_ref[...] = g
        d, m_, v_ = _adam_math(w_ref[...], g, m_ref[...], v_ref[...])
        d_ref[...] = d
        nm_ref[...] = m_
        nv_ref[...] = v_

    blk = pl.BlockSpec((None, t, C), lambda l, c: (l, c, 0))
    return pl.pallas_call(
        body, name="adam_fused_t" if transpose else "adam_fused", grid=(L, R // t),
        in_specs=[blk] * 3 + [gspec(k) for k in range(L)] * 2, out_specs=[blk] * 4,
        out_shape=[jax.ShapeDtypeStruct((L, R, C), F32)] * 4,
        scratch_shapes=[pltpu.VMEM((t, C), F32)],
        compiler_params=_cp(),
    )(w, m, v, *parts, *theirs)


def _gcd(a, b):
    while b:
        a, b = b, a % b
    return a


def _small_sum_adam(g8, w, m, v):
    _, R, W = g8.shape

    def body(g_ref, w_ref, m_ref, v_ref, go_ref, d_ref, nm_ref, nv_ref):
        g = g_ref[0]
        for k in range(1, 8):
            g = g + g_ref[k]
        go_ref[...] = g
        d, m_, v_ = _adam_math(w_ref[...], g, m_ref[...], v_ref[...])
        d_ref[...] = d
        nm_ref[...] = m_
        nv_ref[...] = v_

    return pl.pallas_call(body, name="small_sum_adam",
                          out_shape=[jax.ShapeDtypeStruct((R, W), F32)] * 4)(g8, w, m, v)


def _pad_vec(v):
    return jnp.pad(v, (0, SLOT - v.shape[0]))[None, :]


class _Pack:
    def __init__(self, shapes):
        self.shapes = shapes
        self.sizes = [int(functools.reduce(lambda a, b: a * b, s, 1)) for s in shapes]
        total = sum(self.sizes)
        self.rows = -(-total // (8 * SLOT)) * 8
        self.pad = self.rows * SLOT - total

    def pack(self, arrs):
        flat = jnp.concatenate([a.reshape(-1).astype(F32) for a in arrs] + [jnp.zeros((self.pad,), F32)])
        return flat.reshape(self.rows, SLOT)

    def unpack(self, buf):
        flat, out, o = buf.reshape(-1), [], 0
        for s, n in zip(self.shapes, self.sizes):
            out.append(flat[o:o + n].reshape(s))
            o += n
        return out


def kernel(x, mem, positions, ffn1_norm, ffn1_w1, ffn1_w3, ffn1_w2, mix_norm, w_in, conv_w, conv_b, conv_ln_g, conv_ln_b, swa_q_norm, swa_k_norm, swa_sinks, mem_norm, w_mem_kv, mem_q_norm, mem_k_norm, w_out, ffn2_norm, ffn2_w1, ffn2_w3, ffn2_w2, final_norm, loss_target, m_ffn1_norm, m_ffn1_w1, m_ffn1_w3, m_ffn1_w2, m_mix_norm, m_w_in, m_conv_w, m_conv_b, m_conv_ln_g, m_conv_ln_b, m_swa_q_norm, m_swa_k_norm, m_swa_sinks, m_mem_norm, m_w_mem_kv, m_mem_q_norm, m_mem_k_norm, m_w_out, m_ffn2_norm, m_ffn2_w1, m_ffn2_w3, m_ffn2_w2, m_final_norm, v_ffn1_norm, v_ffn1_w1, v_ffn1_w3, v_ffn1_w2, v_mix_norm, v_w_in, v_conv_w, v_conv_b, v_conv_ln_g, v_conv_ln_b, v_swa_q_norm, v_swa_k_norm, v_swa_sinks, v_mem_norm, v_w_mem_kv, v_mem_q_norm, v_mem_k_norm, v_w_out, v_ffn2_norm, v_ffn2_w1, v_ffn2_w3, v_ffn2_w2, v_final_norm):
    names = ['ffn1_norm', 'ffn1_w1', 'ffn1_w3', 'ffn1_w2', 'mix_norm', 'w_in', 'conv_w', 'conv_b', 'conv_ln_g',
             'conv_ln_b', 'swa_q_norm', 'swa_k_norm', 'swa_sinks', 'mem_norm', 'w_mem_kv', 'mem_q_norm',
             'mem_k_norm', 'w_out', 'ffn2_norm', 'ffn2_w1', 'ffn2_w3', 'ffn2_w2', 'final_norm']
    loc = locals()
    W = {n: loc[n] for n in names}
    M1 = {n: loc['m_' + n] for n in names}
    V1 = {n: loc['v_' + n] for n in names}

    S, D = x.shape[1], x.shape[2]
    L = ffn1_norm.shape[0]
    Fs = ffn1_w1.shape[2]
    F = 4 * Fs
    Mlen = mem.shape[1]
    cw_sh = conv_w.shape[2]
    tm = 512 if S >= 2048 else 256
    tf = 1408 if F % 1408 == 0 else 256
    tfw = 256
    tmw = 2048 if S >= 2048 else 256
    x0 = x[0]
    mem0 = mem[0]
    target = loss_target[0]
    my_chip = 2 * lax.axis_index("x") + lax.axis_index("y")

    mkv_rows = w_mem_kv.shape[1] * MEM_KV // D
    r_in, r_out = D_IN // 4, D_MIX // 4
    rm = r_in + r_out + mkv_rows

    shard = lambda w: w.astype(BF16).reshape((1, 1) + w.shape)

    groups = []
    for l in range(L):
        groups.append([shard(ffn1_w1[l].T), shard(ffn1_w3[l].T), shard(ffn1_w2[l])])
        groups.append([shard(w_in[l].T), shard(w_out[l]), shard(w_mem_kv[l].reshape(mkv_rows, D))])
        groups.append([shard(ffn2_w1[l].T), shard(ffn2_w3[l].T), shard(ffn2_w2[l])])
    gathered = [None] * len(groups)
    cw_rows = -(-(L * CONV_WIDTH) // 8) * 8
    cw_pad = jnp.pad(conv_w.reshape(L * CONV_WIDTH, cw_sh), ((0, cw_rows - L * CONV_WIDTH), (0, SLOT - cw_sh)))
    *gathered[0], cw_g = _run_rider(_Gather(groups[0] + [cw_pad.reshape(1, 1, cw_rows, SLOT)]), "all_gather_first")
    conv_wF = cw_g[0, :, :L * CONV_WIDTH, :cw_sh].reshape(4, L, CONV_WIDTH, cw_sh)
    conv_wF = jnp.moveaxis(conv_wF, 0, 2).reshape(L, CONV_WIDTH, 4 * cw_sh)
    conv_wP = jnp.pad(conv_wF, ((0, 0), (0, 32 - CONV_WIDTH), (0, 0)))

    def gather_rider(j):
        want = [k for k in ([1, 2] if j == 0 else [j + 2]) if k < len(groups)]
        return (_Gather([b for k in want for b in groups[k]]), want) if want else (None, want)

    def keep(want, got):
        for n, k in enumerate(want):
            gathered[k] = got[3 * n:3 * n + 3]

    def ffn_weights(j):
        return tuple(g.reshape(F, D) for g in gathered[j])

    def mix_weights(l):
        g_in, g_out, g_mkv = gathered[3 * l + 1]
        w_inp = g_in.reshape(D_IN, D)
        w_outp = g_out.reshape(D_MIX, D)
        w_mkvp = jnp.pad(g_mkv.reshape(D, 2 * N_MEMH, HEAD_DIM),
                         ((0, 0), (0, 0), (0, SLOT - HEAD_DIM))).reshape(D, 2 * N_MEMH * SLOT)
        return w_inp, w_outp, w_mkvp

    inv_freq = ROPE_THETA ** (-jnp.arange(0, HEAD_DIM, 2, dtype=F32) / HEAD_DIM)
    invf = jnp.tile(inv_freq, SLOT // (HEAD_DIM // 2))[None, :]
    cosT, sinT = _rope_tables(positions.reshape(S, 1), invf, tm)

    row = lambda a, l: a[l][None, :]
    sinks_p = jnp.pad(swa_sinks, ((0, 0), (0, 8 - N_Q)))

    saved = []
    xin = x0
    xn = None
    for l in range(L):
        wf1 = ffn_weights(3 * l)
        rider, want = gather_rider(3 * l)
        (h1, a1, b1, t1), got = _ffn_fwd(xin, row(ffn1_norm, l), wf1, None, tm, tf, rider=rider)
        keep(want, got)
        w_inp, w_outp, w_mkvp = mix_weights(l)
        pu, p, n2 = _proj_fwd(h1, row(mix_norm, l), w_inp, tm)
        gk_m = _pad_vec(mem_k_norm[l])
        nm, mraw, mk, mv = _mem_kv_fwd(mem0, row(mem_norm, l), w_mkvp, gk_m)
        twice = lambda v: jnp.tile(v, 2)[None, :]
        gq, gk, gqm = twice(swa_q_norm[l]), twice(swa_k_norm[l]), twice(mem_q_norm[l])
        rider, want = gather_rider(3 * l + 1)
        (h2, y, yc, lse), got = _mixer_fwd(pu, p, h1, cosT, sinT, conv_wP[l], row(conv_b, l), row(conv_ln_g, l),
                                           row(conv_ln_b, l), gq, gk, sinks_p[l], mk, mv, gqm, w_outp, tm,
                                           rider=rider)
        keep(want, got)
        wf2 = ffn_weights(3 * l + 2)
        rider, want = gather_rider(3 * l + 2)
        (h3, a2, b2, t2, xn), got = _ffn_fwd(h2, row(ffn2_norm, l), wf2, row(final_norm, l), tm, tf, rider=rider)
        keep(want, got)
        saved.append(dict(xin=xin, h1=h1, a1=a1, b1=b1, pu=pu, p=p, n2=n2, nm=nm, mraw=mraw, mk=mk, mv=mv, gk_m=gk_m,
                          gq=gq, gk=gk, gqm=gqm, h2=h2, y=y, yc=yc, lse=lse, h3=h3, a2=a2, b2=b2, t1=t1, t2=t2,
                          wf1=wf1, wf2=wf2, w_inp=w_inp, w_outp=w_outp, w_mkvp=w_mkvp))
        xin = xn

    G = {n: [None] * L for n in names}
    ffn_bufs = [None] * (2 * L)
    mix_bufs = [None] * L
    ffn_recv = [None] * (2 * L)
    mix_recv = [None] * L
    dxn = None
    loss_part = None
    for l in reversed(range(L)):
        sv = saved[l]
        if l == L - 1:
            loss_part, dh3, G['final_norm'][l] = _loss_bwd(xn, sv['h3'], row(final_norm, l), target, tm)
        else:
            dh3, G['final_norm'][l] = _norm_bwd(dxn, sv['h3'], row(final_norm, l), tm)
        rider = _Scatter([ffn_bufs[2 * l + 2]]) if l < L - 1 else None
        (dh2, G['ffn2_norm'][l], da, db, n, dy), got = _ffn_bwd_act(
            dh3, sv['h2'], row(ffn2_norm, l), sv['a2'], sv['b2'], sv['wf2'], tm, tf, rider=rider)
        if got:
            ffn_recv[2 * l + 2] = got[0]
        ffn_bufs[2 * l + 1] = _ffn_bwd_w(da, db, sv['t2'], n, dy, tmw, tfw).reshape(3, 4, Fs, D)
        dyc, do, delta, dwo, G['conv_ln_g'][l], G['conv_ln_b'][l] = _outproj_bwd(
            dh2, sv['y'], sv['yc'], row(conv_ln_g, l), row(conv_ln_b, l), sv['w_outp'], tm)
        (dph, dgq, dgk, dgqm, dsink, dmk, dmv), got = _attn_bwd(
            sv['p'], do, sv['lse'], delta, cosT, sinT, sv['gq'], sv['gk'], sinks_p[l],
            sv['mk'], sv['mv'], sv['gqm'], tm, rider=_Scatter([ffn_bufs[2 * l + 1]]))
        ffn_recv[2 * l + 1] = got[0]
        dpu, dcw, G['conv_b'][l] = _conv_bwd(sv['pu'], dyc, conv_wP[l], tm)
        dwm, G['mem_norm'][l], dgk_m = _mem_kv_bwd(dmk, dmv, sv['mraw'], sv['nm'], mem0, row(mem_norm, l),
                                                   sv['w_mkvp'], sv['gk_m'])
        dh1, G['mix_norm'][l], dwi = _proj_bwd(dpu, dph, sv['h1'], dh2, row(mix_norm, l), sv['n2'], sv['w_inp'], tm)
        dwiT = dwi.reshape(4, r_in, D)
        dwoF = dwo.reshape(4, r_out, D)
        dwmF = dwm.reshape(D, 2 * N_MEMH, SLOT)[:, :, :HEAD_DIM].reshape(4, mkv_rows, D)
        mix_bufs[l] = jnp.concatenate([dwiT, dwoF, dwmF], axis=1).astype(BF16).reshape(1, 4, rm, D)
        (dxl, G['ffn1_norm'][l], da, db, n, dy), got = _ffn_bwd_act(
            dh1, sv['xin'], row(ffn1_norm, l), sv['a1'], sv['b1'], sv['wf1'], tm, tf,
            rider=_Scatter([mix_bufs[l]]))
        mix_recv[l] = got[0]
        ffn_bufs[2 * l] = _ffn_bwd_w(da, db, sv['t1'], n, dy, tmw, tfw).reshape(3, 4, Fs, D)
        dxn = dxl
        G['conv_w'][l] = dcw[:CONV_WIDTH]
        G['swa_q_norm'][l] = dgq[0, :HEAD_DIM] + dgq[0, HEAD_DIM:]
        G['swa_k_norm'][l] = dgk[0, :HEAD_DIM] + dgk[0, HEAD_DIM:]
        G['mem_q_norm'][l] = dgqm[0, :HEAD_DIM] + dgqm[0, HEAD_DIM:]
        G['mem_k_norm'][l] = dgk_m[0, :HEAD_DIM]
        G['swa_sinks'][l] = dsink[0, :N_Q]
    ffn_recv[0] = _run_rider(_Scatter([ffn_bufs[0]]), "scatter_last")[0]
    grad_x = dxn[None]
    loss = lax.psum(loss_part[0, 0], AXES)

    parts = []
    for b, r in zip(ffn_bufs + mix_bufs, ffn_recv + mix_recv):
        own = lax.dynamic_index_in_dim(b, my_chip, axis=1, keepdims=False)
        parts.append(_sum4(own, r))
    theirs = _swap_with_sibling(parts)

    small = ['ffn1_norm', 'mix_norm', 'conv_b', 'conv_ln_g', 'conv_ln_b', 'swa_q_norm', 'swa_k_norm', 'swa_sinks',
             'mem_norm', 'mem_q_norm', 'mem_k_norm', 'ffn2_norm', 'final_norm']
    gsmall = [jnp.stack([G[n][l].reshape(-1) for l in range(L)]) for n in small]
    gcw = jnp.stack(G['conv_w'])
    cw_cols = 4 * cw_sh
    full_of = lambda a: lax.dynamic_update_slice(jnp.zeros((L, CONV_WIDTH, cw_cols), F32), a, (0, 0, my_chip * cw_sh))
    pk = _Pack([W[n].shape for n in small] + [(L, CONV_WIDTH, cw_cols)])
    g8 = _all_gather_small(pk.pack(gsmall + [gcw])[None])
    outs4 = _small_sum_adam(g8, pk.pack([W[n] for n in small] + [full_of(conv_w)]),
                            pk.pack([M1[n] for n in small] + [full_of(m_conv_w)]),
                            pk.pack([V1[n] for n in small] + [full_of(v_conv_w)]))
    un = [pk.unpack(o) for o in outs4]
    grads, deltas, new_m, new_v = {}, {}, {}, {}
    for k, n in enumerate(small):
        grads[n], deltas[n], new_m[n], new_v[n] = un[0][k], un[1][k], un[2][k], un[3][k]
    mine = lambda a: lax.dynamic_slice(a, (0, 0, my_chip * cw_sh), (L, CONV_WIDTH, cw_sh))
    grads['conv_w'], deltas['conv_w'], new_m['conv_w'], new_v['conv_w'] = [mine(u[-1]) for u in un]

    halves = lambda idx: ([parts[i] for i in idx], [theirs[i] for i in idx])
    ffn1_h, ffn2_h = halves([2 * l for l in range(L)]), halves([2 * l + 1 for l in range(L)])
    mix_h = halves([2 * L + l for l in range(L)])
    plan = {'ffn1_w1': (ffn1_h, 0, 0, Fs, True), 'ffn1_w3': (ffn1_h, 1, 0, Fs, True), 'ffn1_w2': (ffn1_h, 2, 0, Fs, False),
            'ffn2_w1': (ffn2_h, 0, 0, Fs, True), 'ffn2_w3': (ffn2_h, 1, 0, Fs, True), 'ffn2_w2': (ffn2_h, 2, 0, Fs, False),
            'w_in': (mix_h, 0, 0, r_in, True), 'w_out': (mix_h, 0, r_in, r_out, False),
            'w_mem_kv': (mix_h, 0, r_in + r_out, mkv_rows, False)}
    for n, ((ps, qs), sel, row0, nrows, transposed) in plan.items():
        shp = W[n].shape
        view = (lambda a: a.reshape(L, mkv_rows, D)) if n == 'w_mem_kv' else (lambda a: a)
        res = _adam_fused(view(W[n]), view(M1[n]), view(V1[n]), ps, qs, sel, row0, nrows, transposed)
        grads[n], deltas[n], new_m[n], new_v[n] = [r.reshape(shp) for r in res]

    return (loss, grad_x, *[grads[n] for n in names], *[deltas[n] for n in names],
            *[new_m[n] for n in names], *[new_v[n] for n in names])
```

```python
import functools

import jax
import jax.numpy as jnp
from jax import lax
from jax.experimental import pallas as pl
from jax.experimental.pallas import tpu as pltpu

F32 = jnp.float32
BF16 = jnp.bfloat16
MESH = pl.DeviceIdType.MESH
AXES = ("x", "y", "c")

EPS = 1e-6
HEAD_DIM = 64
SLOT = 128
CONV_CH = 384
CONV_WIDTH = 31
N_Q, N_KV, N_MEMH = 6, 2, 4
GROUP = N_Q // N_KV
BLK = 128
HALO = 32
CONV_ROWS = 64
FFN_CHUNK = 256
ROPE_THETA = 10000.0
SCALE = HEAD_DIM ** -0.5
NEG = -1e30

N_HEADS_IN = N_Q + 2 * N_KV + N_MEMH
PU = 2 * CONV_CH
PH = HEAD_DIM * N_HEADS_IN
PP = PU + PH
QO = 0
KO = QO + HEAD_DIM * N_Q
VO = KO + HEAD_DIM * N_KV
MO = VO + HEAD_DIM * N_KV
NH = N_Q + N_MEMH
STAT_ROWS = 16
YH = HEAD_DIM * NH
YP = CONV_CH + YH
YS = CONV_CH
YM = YS + HEAD_DIM * N_Q
D_IN = PP
D_MIX = YP
MEM_KV = 2 * HEAD_DIM * N_MEMH

ADAM_LR, ADAM_B1, ADAM_B2, ADAM_EPS, ADAM_WD, ADAM_STEP = 0.001, 0.9, 0.999, 1e-08, 0.01, 10

VMEM_LIMIT_MB = 56


def _cp(mb=VMEM_LIMIT_MB):
    return pltpu.CompilerParams(vmem_limit_bytes=mb * 1024 * 1024)


def _dot_nn(a, b):
    return lax.dot_general(a, b, (((1,), (0,)), ((), ())), preferred_element_type=F32)


def _dot_nt(a, b):
    return lax.dot_general(a, b, (((1,), (1,)), ((), ())), preferred_element_type=F32)


def _dot_tn(a, b):
    return lax.dot_general(a, b, (((0,), (0,)), ((), ())), preferred_element_type=F32)


def _sigmoid(x):
    return 1.0 / (1.0 + jnp.exp(-x))


def _rms(x):
    return lax.rsqrt(jnp.mean(x * x, axis=-1, keepdims=True) + EPS)


def _rms_bwd(dn, x, r, g):
    xhat = x * r
    dxhat = dn * g
    dx = r * (dxhat - xhat * jnp.mean(dxhat * xhat, axis=-1, keepdims=True))
    return dx, dn * xhat


def _colsum(v):
    return jnp.sum(v, axis=0, keepdims=True)


def _lane(n):
    return lax.broadcasted_iota(jnp.int32, (n, SLOT), 1)


def _slot_rms(xs):
    return lax.rsqrt(jnp.sum(xs * xs, axis=-1, keepdims=True) * (1.0 / HEAD_DIM) + EPS)


def _slot_norm(xs, g):
    return xs * _slot_rms(xs) * g


def _slot_norm_bwd(dout, xs, g):
    r = _slot_rms(xs)
    xhat = xs * r
    dxhat = dout * g
    dx = r * (dxhat - xhat * (jnp.sum(dxhat * xhat, axis=-1, keepdims=True) * (1.0 / HEAD_DIM)))
    return dx, dout * xhat


def _halves(v, lane):
    lo = jnp.sum(jnp.where(lane < HEAD_DIM, v, 0.0), axis=-1, keepdims=True)
    hi = jnp.sum(jnp.where(lane < HEAD_DIM, 0.0, v), axis=-1, keepdims=True)
    return jnp.where(lane < HEAD_DIM, lo, hi)


def _pair_rms(x, lane):
    return lax.rsqrt(_halves(x * x, lane) * (1.0 / HEAD_DIM) + EPS)


def _pair_partner(v, lane):
    return jnp.where((lane & (HEAD_DIM - 1)) < HEAD_DIM // 2,
                     pltpu.roll(v, SLOT - HEAD_DIM // 2, 1), pltpu.roll(v, HEAD_DIM // 2, 1))


def _pair_fwd(x, g2, cosv, sinv, lane):
    xn = x * _pair_rms(x, lane) * g2
    if cosv is None:
        return xn
    return xn * cosv + _pair_partner(xn, lane) * sinv


def _pair_bwd(dout, x, g2, cosv, sinv, lane):
    if cosv is not None:
        dout = dout * cosv + _pair_partner(dout * sinv, lane)
    r = _pair_rms(x, lane)
    xhat = x * r
    dxhat = dout * g2
    dx = r * (dxhat - xhat * (_halves(dxhat * xhat, lane) * (1.0 / HEAD_DIM)))
    return dx, dout * xhat


def _lo(x, half, lane):
    if half:
        x = pltpu.roll(x, HEAD_DIM, 1)
    return jnp.where(lane < HEAD_DIM, x, 0.0)


def _pack(even, odd, lane):
    return jnp.where(lane < HEAD_DIM, even, pltpu.roll(odd, HEAD_DIM, 1))


def _place():
    x, y, c = lax.axis_index("x"), lax.axis_index("y"), lax.axis_index("c")
    chips = [(1 - x, y), (x, 1 - y), (1 - x, 1 - y)]
    return x, y, c, chips


class _Gather:
    tag = "_gather"

    def __init__(self, bufs):
        self.bufs = list(bufs)
        nb = len(self.bufs)
        self.out_shape = [jax.ShapeDtypeStruct((b.shape[0], 4) + b.shape[2:], b.dtype) for b in self.bufs]
        self.sems = [pltpu.SemaphoreType.DMA((3 * nb,)), pltpu.SemaphoreType.DMA((3 * nb,)),
                     pltpu.SemaphoreType.DMA((nb,))]

    def _copies(self, ins, outs, sems):
        ssem, rsem, lsem = sems
        nb = len(self.bufs)
        x, y, c, chips = _place()
        mine = 2 * x + y

        def copy(b, p, shard):
            return pltpu.make_async_remote_copy(
                src_ref=ins[b], dst_ref=outs[b].at[:, pl.ds(shard, 1)],
                send_sem=ssem.at[3 * b + p], recv_sem=rsem.at[3 * b + p],
                device_id=(chips[p][0], chips[p][1], c), device_id_type=MESH)

        local = [pltpu.make_async_copy(ins[b], outs[b].at[:, pl.ds(mine, 1)], lsem.at[b]) for b in range(nb)]
        sends = [copy(b, p, mine) for b in range(nb) for p in range(3)]
        recvs = [copy(b, p, 2 * chips[p][0] + chips[p][1]) for b in range(nb) for p in range(3)]
        return local, sends, recvs

    def start(self, ins, outs, sems):
        local, sends, _ = self._copies(ins, outs, sems)
        for cp in local + sends:
            cp.start()

    def wait(self, ins, outs, sems):
        local, sends, recvs = self._copies(ins, outs, sems)
        for cp in recvs:
            cp.wait_recv()
        for cp in sends:
            cp.wait_send()
        for cp in local:
            cp.wait()


class _Scatter:
    tag = "_scatter"

    def __init__(self, bufs):
        self.bufs = list(bufs)
        nb = len(self.bufs)
        self.out_shape = [jax.ShapeDtypeStruct((3, b.shape[0], 1) + b.shape[2:], b.dtype) for b in self.bufs]
        self.sems = [pltpu.SemaphoreType.DMA((3 * nb,)), pltpu.SemaphoreType.DMA((3 * nb,))]

    def _copies(self, ins, outs, sems):
        ssem, rsem = sems
        x, y, c, chips = _place()

        def copy(b, p):
            shard = 2 * chips[p][0] + chips[p][1]
            return pltpu.make_async_remote_copy(
                src_ref=ins[b].at[:, pl.ds(shard, 1)], dst_ref=outs[b].at[p],
                send_sem=ssem.at[3 * b + p], recv_sem=rsem.at[3 * b + p],
                device_id=(chips[p][0], chips[p][1], c), device_id_type=MESH)

        return [copy(b, p) for b in range(len(self.bufs)) for p in range(3)]

    def start(self, ins, outs, sems):
        for cp in self._copies(ins, outs, sems):
            cp.start()

    def wait(self, ins, outs, sems):
        cps = self._copies(ins, outs, sems)
        for cp in cps:
            cp.wait_recv()
        for cp in cps:
            cp.wait_send()


def _run_rider(rider, name):
    nb = len(rider.bufs)

    def body(*refs):
        ins, outs, sems = refs[:nb], refs[nb:2 * nb], refs[2 * nb:]
        rider.start(ins, outs, sems)
        rider.wait(ins, outs, sems)

    hbm = pl.BlockSpec(memory_space=pl.ANY)
    return pl.pallas_call(body, name=name, in_specs=[hbm] * nb, out_specs=[hbm] * nb,
                          out_shape=rider.out_shape, scratch_shapes=rider.sems)(*rider.bufs)


def _call(body, *, name, grid, in_specs, out_specs, out_shape, args, scratch=(), rider=None):
    if rider is None:
        outs = pl.pallas_call(body, name=name, grid=grid, in_specs=list(in_specs), out_specs=list(out_specs),
                              out_shape=list(out_shape), scratch_shapes=list(scratch),
                              compiler_params=_cp())(*args)
        return list(outs), None
    n_in, n_out, n_scr, nb = len(in_specs), len(out_specs), len(scratch), len(rider.bufs)

    def wrapped(*refs):
        cuts = [n_in, nb, n_out, nb, n_scr]
        parts, o = [], 0
        for n in cuts:
            parts.append(refs[o:o + n])
            o += n
        ins, rin, outs, rout, scr = parts
        sems = refs[o:]
        ids = [pl.program_id(k) for k in range(len(grid))]
        first = functools.reduce(jnp.logical_and, [i == 0 for i in ids])
        last = functools.reduce(jnp.logical_and, [i == n - 1 for i, n in zip(ids, grid)])

        @pl.when(first)
        def _():
            rider.start(rin, rout, sems)

        body(*ins, *outs, *scr)

        @pl.when(last)
        def _():
            rider.wait(rin, rout, sems)

    hbm = pl.BlockSpec(memory_space=pl.ANY)
    res = pl.pallas_call(
        wrapped, name=name + rider.tag, grid=grid,
        in_specs=list(in_specs) + [hbm] * nb, out_specs=list(out_specs) + [hbm] * nb,
        out_shape=list(out_shape) + rider.out_shape, scratch_shapes=list(scratch) + rider.sems,
        compiler_params=_cp())(*args, *rider.bufs)
    return list(res[:n_out]), list(res[n_out:])


def _rope_tables(pos, invf, tm):
    S = pos.shape[0]

    def body(pos_ref, f_ref, cos_ref, sin_ref):
        ang = pos_ref[...].astype(F32) * f_ref[...]
        lane = _lane(tm)
        cos_ref[...] = jnp.cos(ang)
        s = jnp.sin(ang)
        sin_ref[...] = jnp.where((lane & (HEAD_DIM - 1)) < HEAD_DIM // 2, -s, s)

    return pl.pallas_call(
        body, name="rope_tables", grid=(S // tm,),
        in_specs=[pl.BlockSpec((tm, 1), lambda i: (i, 0)), pl.BlockSpec((1, SLOT), lambda i: (0, 0))],
        out_specs=[pl.BlockSpec((tm, SLOT), lambda i: (i, 0))] * 2,
        out_shape=[jax.ShapeDtypeStruct((S, SLOT), F32)] * 2,
    )(pos, invf)


def _ffn_fwd(x, g, wf, gfin, tm, tf, rider=None):
    S, D = x.shape
    F = wf[0].shape[0]
    nf = F // tf
    final = gfin is not None

    chunks = [(c, min(FFN_CHUNK, tf - c)) for c in range(0, tf, FFN_CHUNK)]

    def body(*refs):
        if final:
            x_ref, g_ref, w1_ref, w3_ref, w2_ref, gf_ref, h_ref, a_ref, b_ref, t_ref, xn_ref, n_scr, acc = refs
        else:
            x_ref, g_ref, w1_ref, w3_ref, w2_ref, h_ref, a_ref, b_ref, t_ref, n_scr, acc = refs
        j = pl.program_id(1)

        @pl.when(j == 0)
        def _():
            xv = x_ref[...]
            n_scr[...] = (xv * _rms(xv) * g_ref[...]).astype(BF16)
            acc[...] = jnp.zeros_like(acc)

        n = n_scr[...]
        for c0, cw in chunks:
            cols = slice(c0, c0 + cw)
            a = _dot_nt(n, w1_ref[cols, :])
            b = _dot_nt(n, w3_ref[cols, :])
            a_ref[:, cols] = a.astype(BF16)
            b_ref[:, cols] = b.astype(BF16)
            t_ref[:, cols] = (a * _sigmoid(a) * b).astype(BF16)
        acc[...] += _dot_nn(t_ref[...], w2_ref[...])

        @pl.when(j == nf - 1)
        def _():
            h = x_ref[...] + 0.5 * acc[...]
            h_ref[...] = h
            if final:
                xn_ref[...] = h * _rms(h) * gf_ref[...]

    def wspec(k):
        return pl.BlockSpec((tf, D), lambda i, j: (j, 0))

    row = pl.BlockSpec((tm, D), lambda i, j: (i, 0))
    vec = pl.BlockSpec((1, D), lambda i, j: (0, 0))
    act = pl.BlockSpec((tm, tf), lambda i, j: (i, j))
    in_specs = [row, vec, wspec(0), wspec(1), wspec(2)] + ([vec] if final else [])
    out_specs = [row, act, act, act] + ([row] if final else [])
    out_shape = [jax.ShapeDtypeStruct((S, D), F32)] + [jax.ShapeDtypeStruct((S, F), BF16)] * 3 \
        + ([jax.ShapeDtypeStruct((S, D), F32)] if final else [])
    args = [x, g, *wf] + ([gfin] if final else [])
    return _call(body, name="ffn_fwd_final" if final else "ffn_fwd", grid=(S // tm, nf),
                 in_specs=in_specs, out_specs=out_specs, out_shape=out_shape, args=args,
                 scratch=[pltpu.VMEM((tm, D), BF16), pltpu.VMEM((tm, D), F32)], rider=rider)


def _ffn_bwd_act(dh, x, g, a, b, wf, tm, tf, rider=None):
    S, D = x.shape
    F = wf[0].shape[0]
    nf = F // tf

    chunks = [(c, min(FFN_CHUNK, tf - c)) for c in range(0, tf, FFN_CHUNK)]

    def body(dh_ref, x_ref, g_ref, a_ref, b_ref, w1_ref, w3_ref, w2_ref,
             dx_ref, dg_ref, da_ref, db_ref, n_ref, dy_ref, acc):
        i, j = pl.program_id(0), pl.program_id(1)

        @pl.when(j == 0)
        def _():
            xv = x_ref[...]
            n_ref[...] = (xv * _rms(xv) * g_ref[...]).astype(BF16)
            dy_ref[...] = (0.5 * dh_ref[...]).astype(BF16)
            acc[...] = jnp.zeros_like(acc)

            @pl.when(i == 0)
            def _():
                dg_ref[...] = jnp.zeros_like(dg_ref)

        dyv = dy_ref[...]
        for c0, cw in chunks:
            cols = slice(c0, c0 + cw)
            av = a_ref[:, cols].astype(F32)
            bv = b_ref[:, cols].astype(F32)
            sg = _sigmoid(av)
            dt = _dot_nt(dyv, w2_ref[cols, :])
            db_ref[:, cols] = (dt * (av * sg)).astype(BF16)
            da_ref[:, cols] = (dt * bv * (sg * (1.0 + av * (1.0 - sg)))).astype(BF16)
        acc[...] += _dot_nn(da_ref[...], w1_ref[...]) + _dot_nn(db_ref[...], w3_ref[...])

        @pl.when(j == nf - 1)
        def _():
            xv = x_ref[...]
            dx, dgrow = _rms_bwd(acc[...], xv, _rms(xv), g_ref[...])
            dx_ref[...] = dh_ref[...] + dx
            dg_ref[...] += _colsum(dgrow)

    def wspec(k):
        return pl.BlockSpec((tf, D), lambda i, j: (j, 0))

    row = pl.BlockSpec((tm, D), lambda i, j: (i, 0))
    vec = pl.BlockSpec((1, D), lambda i, j: (0, 0))
    act = pl.BlockSpec((tm, tf), lambda i, j: (i, j))
    sd = lambda shp, dt: jax.ShapeDtypeStruct(shp, dt)
    return _call(body, name="ffn_bwd_act", grid=(S // tm, nf),
                 in_specs=[row, row, vec, act, act, wspec(0), wspec(1), wspec(2)],
                 out_specs=[row, vec, act, act, row, row],
                 out_shape=[sd((S, D), F32), sd((1, D), F32), sd((S, F), BF16), sd((S, F), BF16),
                            sd((S, D), BF16), sd((S, D), BF16)],
                 args=[dh, x, g, a, b, *wf], scratch=[pltpu.VMEM((tm, D), F32)], rider=rider)


def _ffn_bwd_w(da, db, t, n, dy, tm, tf):
    S, F = da.shape
    D = n.shape[1]
    nt = S // tm

    def body(da_ref, db_ref, t_ref, n_ref, dy_ref, out_ref, acc):
        i = pl.program_id(1)

        @pl.when(i == 0)
        def _():
            acc[...] = jnp.zeros_like(acc)

        nv = n_ref[...]
        acc[0] += _dot_tn(da_ref[...], nv)
        acc[1] += _dot_tn(db_ref[...], nv)
        acc[2] += _dot_tn(t_ref[...], dy_ref[...])

        @pl.when(i == nt - 1)
        def _():
            out_ref[...] = acc[...].astype(BF16)

    act = pl.BlockSpec((tm, tf), lambda j, i: (i, j))
    row = pl.BlockSpec((tm, D), lambda j, i: (i, 0))
    return pl.pallas_call(
        body, name="ffn_bwd_w", grid=(F // tf, nt),
        in_specs=[act, act, act, row, row],
        out_specs=pl.BlockSpec((3, tf, D), lambda j, i: (0, j, 0)),
        out_shape=jax.ShapeDtypeStruct((3, F, D), BF16),
        scratch_shapes=[pltpu.VMEM((3, tf, D), F32)],
        compiler_params=_cp(),
    )(da, db, t, n, dy)


def _proj_fwd(h, g, w_inp, tm):
    S, D = h.shape

    def body(h_ref, g_ref, w_ref, pu_ref, ph_ref, n_ref):
        hv = h_ref[...]
        n = (hv * _rms(hv) * g_ref[...]).astype(BF16)
        n_ref[...] = n
        pu_ref[...] = _dot_nt(n, w_ref[0:PU, :])
        ph_ref[...] = _dot_nt(n, w_ref[PU:PP, :])

    cur = lambda w: pl.BlockSpec((tm, w), lambda i: (i, 0))
    return pl.pallas_call(
        body, name="proj_fwd", grid=(S // tm,),
        in_specs=[cur(D), pl.BlockSpec((1, D), lambda i: (0, 0)), pl.BlockSpec((PP, D), lambda i: (0, 0))],
        out_specs=[cur(PU), cur(PH), cur(D)],
        out_shape=[jax.ShapeDtypeStruct((S, PU), F32), jax.ShapeDtypeStruct((S, PH), F32),
                   jax.ShapeDtypeStruct((S, D), BF16)],
        compiler_params=_cp(),
    )(h, g, w_inp)


def _glu(u):
    return u[:, :CONV_CH] * _sigmoid(u[:, CONV_CH:2 * CONV_CH])


def _shifted_copies(ext8):
    n = ext8.shape[1]
    for b in range(1, 8):
        ext8[b, 0:n - 8, :] = ext8[0, b:b + n - 8, :]


def _window(ext8, off, rows, r0=0):
    return ext8[off % 8, pl.ds(r0 + (off - off % 8), rows), :]


def _layer_norm_stats(yc):
    mu = jnp.mean(yc, axis=-1, keepdims=True)
    d = yc - mu
    rstd = lax.rsqrt(jnp.mean(d * d, axis=-1, keepdims=True) + EPS)
    return d * rstd, rstd


def _mem_kv_fwd(mem, g, w_mkvp, gk):
    M, D = mem.shape
    W = SLOT * N_MEMH

    def body(mem_ref, g_ref, w_ref, gk_ref, nm_ref, raw_ref, mk_ref, mv_ref):
        mv_ = mem_ref[...]
        nm = (mv_ * _rms(mv_) * g_ref[...]).astype(BF16)
        nm_ref[...] = nm
        raw = _dot_nn(nm, w_ref[...])
        raw_ref[...] = raw
        for hh in range(N_MEMH):
            sl = slice(SLOT * hh, SLOT * (hh + 1))
            mk_ref[:, sl] = _slot_norm(raw[:, sl], gk_ref[...]).astype(BF16)
        mv_ref[...] = raw[:, W:].astype(BF16)

    sd = jax.ShapeDtypeStruct
    return pl.pallas_call(
        body, name="mem_kv_fwd",
        out_shape=[sd((M, D), BF16), sd((M, 2 * W), F32), sd((M, W), BF16), sd((M, W), BF16)],
        compiler_params=_cp(),
    )(mem, g, w_mkvp, gk)


def _mem_kv_bwd(dmk, dmv, raw, nm, mem, g, w_mkvp, gk):
    M, D = mem.shape
    W = SLOT * N_MEMH

    def body(dmk_ref, dmv_ref, raw_ref, nm_ref, mem_ref, g_ref, w_ref, gk_ref, dw_ref, dg_ref, dgk_ref, draw):
        dgk = jnp.zeros((1, SLOT), F32)
        for hh in range(N_MEMH):
            sl = slice(SLOT * hh, SLOT * (hh + 1))
            dx, prod = _slot_norm_bwd(dmk_ref[:, sl], raw_ref[:, sl], gk_ref[...])
            draw[:, sl] = dx.astype(BF16)
            dgk = dgk + _colsum(prod)
        dgk_ref[...] = dgk
        draw[:, W:] = dmv_ref[...].astype(BF16)
        dr = draw[...]
        dw_ref[...] = _dot_tn(nm_ref[...], dr)
        dnm = _dot_nt(dr, w_ref[...])
        mv_ = mem_ref[...]
        dg_ref[...] = _colsum(dnm * (mv_ * _rms(mv_)))

    sd = jax.ShapeDtypeStruct
    return pl.pallas_call(
        body, name="mem_kv_bwd",
        out_shape=[sd((D, 2 * W), F32), sd((1, D), F32), sd((1, SLOT), F32)],
        scratch_shapes=[pltpu.VMEM((M, 2 * W), BF16)],
        compiler_params=_cp(),
    )(dmk, dmv, raw, nm, mem, g, w_mkvp, gk)


def _mixer_fwd(pu, ph, h, cosT, sinT, conv_w, conv_b, ln_g, ln_b, gq, gk, sinks, mk, mv, gqm, w_outp, tm, rider=None):
    S, D = h.shape
    M = mk.shape[0]
    nb = tm // BLK
    nblocks = S // BLK

    def body(pu_ref, pup_ref, p_ref, ph_ref, h_ref, cos_ref, cosh_ref, sin_ref, sinh_ref, cw_ref, cb_ref,
             lg_ref, lb_ref, gq_ref, gk_ref, sink_ref, mk_ref, mv_ref, gqm_ref, wo_ref,
             h2_ref, y_ref, yc_ref, lse_ref, ext, y_scr):
        i = pl.program_id(0)
        not_first = (i > 0).astype(F32)
        lane = _lane(tm)
        lane_e = _lane(tm + BLK)

        ext[0, 0:HALO, :] = _glu(pup_ref[...]) * not_first
        ext[0, HALO:HALO + tm, :] = _glu(pu_ref[...])
        _shifted_copies(ext)

        def rows_chunk(r, carry):
            r0 = pl.multiple_of(r * CONV_ROWS, CONV_ROWS)
            yc = jnp.zeros((CONV_ROWS, CONV_CH), F32) + cb_ref[...]
            for k in range(CONV_WIDTH):
                yc = yc + cw_ref[k:k + 1, :] * _window(ext, HALO - (CONV_WIDTH - 1) + k, CONV_ROWS, r0)
            yc_ref[pl.ds(r0, CONV_ROWS), :] = yc
            z, _ = _layer_norm_stats(yc)
            ln = z * lg_ref[...] + lb_ref[...]
            y_scr[pl.ds(r0, CONV_ROWS), 0:CONV_CH] = (ln * _sigmoid(ln)).astype(BF16)
            return carry

        lax.fori_loop(0, tm // CONV_ROWS, rows_chunk, 0)

        cos_e = jnp.concatenate([cosh_ref[...], cos_ref[...]], axis=0)
        sin_e = jnp.concatenate([sinh_ref[...], sin_ref[...]], axis=0)
        qi = lax.broadcasted_iota(jnp.int32, (GROUP * BLK, 2 * BLK), 0) & (BLK - 1)
        kj = lax.broadcasted_iota(jnp.int32, (GROUP * BLK, 2 * BLK), 1)
        band = (kj > qi) & (kj <= qi + BLK)
        band0 = band & ((kj >= BLK) | (i > 0))
        lse = jnp.zeros((tm, SLOT), F32)
        k_pair = jnp.concatenate([ph_ref[:, KO:KO + SLOT], p_ref[:, KO:KO + SLOT]], axis=0)
        k_pair = _pair_fwd(k_pair, gk_ref[...], cos_e, sin_e, lane_e)
        v_pair = jnp.concatenate([ph_ref[:, VO:VO + SLOT], p_ref[:, VO:VO + SLOT]], axis=0)
        k_e = [_lo(k_pair, kvh, lane_e).astype(BF16) for kvh in range(N_KV)]
        v_e = [_lo(v_pair, kvh, lane_e).astype(BF16) for kvh in range(N_KV)]
        q_lo = []
        for j in range(N_Q // 2):
            q_pair = _pair_fwd(p_ref[:, QO + SLOT * j:QO + SLOT * (j + 1)], gq_ref[...],
                               cos_ref[...], sin_ref[...], lane)
            q_lo += [_lo(q_pair, 0, lane).astype(BF16), _lo(q_pair, 1, lane).astype(BF16)]
        outs = [[] for _ in range(N_Q)]
        lses = [[] for _ in range(N_Q)]
        for kvh in range(N_KV):
            hs = [GROUP * kvh + gi for gi in range(GROUP)]
            sink3 = jnp.concatenate([jnp.full((BLK, 1), sink_ref[h], F32) for h in hs], axis=0)
            for m in range(nb):
                rows = slice(BLK * m, BLK * (m + 1))
                win = slice(BLK * m, BLK * (m + 2))
                q3 = jnp.concatenate([q_lo[h][rows] for h in hs], axis=0)
                s = _dot_nt(q3, k_e[kvh][win]) * SCALE
                s = jnp.where(band0 if m == 0 else band, s, NEG)
                mx = jnp.maximum(jnp.max(s, axis=-1, keepdims=True), sink3)
                e = jnp.exp(s - mx)
                den = jnp.sum(e, axis=-1, keepdims=True) + jnp.exp(sink3 - mx)
                o3 = _dot_nn((e / den).astype(BF16), v_e[kvh][win])
                l3 = mx + jnp.log(den)
                for gi, h in enumerate(hs):
                    outs[h].append(o3[BLK * gi:BLK * (gi + 1)])
                    lses[h].append(l3[BLK * gi:BLK * (gi + 1)])
        for h in range(N_Q):
            lse = jnp.where(lane == h, jnp.concatenate(lses[h], axis=0), lse)
        for j in range(N_Q // 2):
            y_scr[:, YS + SLOT * j:YS + SLOT * (j + 1)] = _pack(
                jnp.concatenate(outs[2 * j], axis=0), jnp.concatenate(outs[2 * j + 1], axis=0), lane).astype(BF16)

        heads = []
        for hm in range(N_MEMH):
            ms = slice(SLOT * hm, SLOT * (hm + 1))
            if hm % 2 == 0:
                qm_pair = _pair_fwd(p_ref[:, MO + SLOT * (hm // 2):MO + SLOT * (hm // 2 + 1)], gqm_ref[...],
                                    None, None, lane)
            s = _dot_nt(_lo(qm_pair, hm % 2, lane).astype(BF16), mk_ref[:, ms]) * SCALE
            mx = jnp.max(s, axis=-1, keepdims=True)
            e = jnp.exp(s - mx)
            den = jnp.sum(e, axis=-1, keepdims=True)
            heads.append(_dot_nn((e / den).astype(BF16), mv_ref[:, ms]))
            lse = jnp.where(lane == N_Q + hm, mx + jnp.log(den), lse)
            if hm % 2 == 1:
                y_scr[:, YM + SLOT * (hm // 2):YM + SLOT * (hm // 2 + 1)] = _pack(heads[-2], heads[-1], lane).astype(BF16)
        lse_ref[...] = lse.T[0:STAT_ROWS, :]

        yv = y_scr[...]
        y_ref[...] = yv
        h2_ref[...] = h_ref[...] + _dot_nn(yv, wo_ref[...])

    cur = lambda w: pl.BlockSpec((tm, w), lambda i: (i, 0))
    prev = lambda w: pl.BlockSpec((BLK, w), lambda i: (jnp.maximum(i * nb - 1, 0), 0))
    full = lambda a: pl.BlockSpec(a.shape, lambda i: (0,) * a.ndim)
    sd = jax.ShapeDtypeStruct
    prev32 = pl.BlockSpec((HALO, PU), lambda i: (jnp.maximum(i * (tm // HALO) - 1, 0), 0))
    return _call(
        body, name="mixer_fwd", grid=(S // tm,),
        in_specs=[cur(PU), prev32, cur(PH), prev(PH), cur(D), cur(SLOT), prev(SLOT), cur(SLOT), prev(SLOT),
                  full(conv_w), full(conv_b), full(ln_g), full(ln_b), full(gq), full(gk),
                  pl.BlockSpec(memory_space=pltpu.SMEM), full(mk), full(mv), full(gqm), full(w_outp)],
        out_specs=[cur(D), cur(YP), cur(CONV_CH), pl.BlockSpec((STAT_ROWS, tm), lambda i: (0, i))],
        out_shape=[sd((S, D), F32), sd((S, YP), BF16), sd((S, CONV_CH), F32), sd((STAT_ROWS, S), F32)],
        args=[pu, pu, ph, ph, h, cosT, cosT, sinT, sinT, conv_w, conv_b, ln_g, ln_b, gq, gk, sinks, mk, mv, gqm,
              w_outp],
        scratch=[pltpu.VMEM((8, tm + HALO, CONV_CH), F32), pltpu.VMEM((tm, YP), BF16)], rider=rider)


def _outproj_bwd(dh2, y, yc, ln_g, ln_b, w_outp, tm):
    S, D = dh2.shape

    def body(dh_ref, y_ref, yc_ref, lg_ref, lb_ref, wo_ref, dyc_ref, do_ref, del_ref, dwo_ref, dlg_ref, dlb_ref):
        i = pl.program_id(0)

        @pl.when(i == 0)
        def _():
            dwo_ref[...] = jnp.zeros_like(dwo_ref)
            dlg_ref[...] = jnp.zeros_like(dlg_ref)
            dlb_ref[...] = jnp.zeros_like(dlb_ref)

        dhb = dh_ref[...].astype(BF16)
        yv = y_ref[...]
        dy = _dot_nt(dhb, wo_ref[...])
        dwo_ref[...] += _dot_tn(yv, dhb)

        z, rstd = _layer_norm_stats(yc_ref[...])
        ln = z * lg_ref[...] + lb_ref[...]
        sg = _sigmoid(ln)
        dln = dy[:, 0:CONV_CH] * (sg * (1.0 + ln * (1.0 - sg)))
        dlg_ref[...] += _colsum(dln * z)
        dlb_ref[...] += _colsum(dln)
        dz = dln * lg_ref[...]
        dyc_ref[...] = rstd * (dz - jnp.mean(dz, axis=-1, keepdims=True)
                               - z * jnp.mean(dz * z, axis=-1, keepdims=True))
        do_ref[...] = dy[:, CONV_CH:].astype(BF16)

        lane = _lane(tm)
        delta = jnp.zeros((tm, SLOT), F32)
        for j in range(NH // 2):
            sl = slice(YS + SLOT * j, YS + SLOT * (j + 1))
            prod = dy[:, sl] * yv[:, sl].astype(F32)
            lo = jnp.sum(jnp.where(lane < HEAD_DIM, prod, 0.0), axis=-1, keepdims=True)
            hi = jnp.sum(jnp.where(lane < HEAD_DIM, 0.0, prod), axis=-1, keepdims=True)
            delta = jnp.where(lane == 2 * j, lo, jnp.where(lane == 2 * j + 1, hi, delta))
        del_ref[...] = delta.T[0:STAT_ROWS, :]

    cur = lambda w: pl.BlockSpec((tm, w), lambda i: (i, 0))
    full = lambda a: pl.BlockSpec(a.shape, lambda i: (0,) * a.ndim)
    sd = jax.ShapeDtypeStruct
    return pl.pallas_call(
        body, name="outproj_bwd", grid=(S // tm,),
        in_specs=[cur(D), cur(YP), cur(CONV_CH), full(ln_g), full(ln_b), full(w_outp)],
        out_specs=[cur(CONV_CH), cur(YH), pl.BlockSpec((STAT_ROWS, tm), lambda i: (0, i)),
                   pl.BlockSpec((YP, D), lambda i: (0, 0)),
                   pl.BlockSpec((1, CONV_CH), lambda i: (0, 0)), pl.BlockSpec((1, CONV_CH), lambda i: (0, 0))],
        out_shape=[sd((S, CONV_CH), F32), sd((S, YH), BF16), sd((STAT_ROWS, S), F32), sd((YP, D), F32),
                   sd((1, CONV_CH), F32), sd((1, CONV_CH), F32)],
        compiler_params=_cp(),
    )(dh2, y, yc, ln_g, ln_b, w_outp)


def _conv_bwd(pu, dyc, conv_w, tm):
    S = pu.shape[0]
    nt = S // tm
    nh = tm // HALO

    def body(pu_ref, pup_ref, dy_ref, dyn_ref, cw_ref, dpu_ref, dcw_ref, dcb_ref, ext, ext2, dcw8):
        i = pl.program_id(0)

        @pl.when(i == 0)
        def _():
            dcw8[...] = jnp.zeros_like(dcw8)
            dcb_ref[...] = jnp.zeros_like(dcb_ref)

        not_first = (i > 0).astype(F32)
        not_last = (i < nt - 1).astype(F32)
        ext[0, 0:HALO, :] = _glu(pup_ref[...]) * not_first
        ext[0, HALO:HALO + tm, :] = _glu(pu_ref[...])
        _shifted_copies(ext)
        ext2[0, 0:tm, :] = dy_ref[...]
        ext2[0, tm:tm + HALO, :] = dyn_ref[...] * not_last
        _shifted_copies(ext2)
        dcb_ref[...] += _colsum(dy_ref[...])

        def rows_chunk(r, carry):
            r0 = pl.multiple_of(r * CONV_ROWS, CONV_ROWS)
            dyc_ = dy_ref[pl.ds(r0, CONV_ROWS), :]
            dyg = jnp.zeros((CONV_ROWS, CONV_CH), F32)
            for k in range(CONV_WIDTH):
                prod = dyc_ * _window(ext, HALO - (CONV_WIDTH - 1) + k, CONV_ROWS, r0)
                dcw8[k] += jnp.sum(prod.reshape(CONV_ROWS // 8, 8, CONV_CH), axis=0)
                dyg = dyg + cw_ref[k:k + 1, :] * _window(ext2, CONV_WIDTH - 1 - k, CONV_ROWS, r0)
            u = pu_ref[pl.ds(r0, CONV_ROWS), :]
            a_, sg = u[:, :CONV_CH], _sigmoid(u[:, CONV_CH:])
            dpu_ref[pl.ds(r0, CONV_ROWS), 0:CONV_CH] = (dyg * sg).astype(BF16)
            dpu_ref[pl.ds(r0, CONV_ROWS), CONV_CH:PU] = (dyg * a_ * sg * (1.0 - sg)).astype(BF16)
            return carry

        lax.fori_loop(0, tm // CONV_ROWS, rows_chunk, 0)

        @pl.when(i == nt - 1)
        def _():
            dcw_ref[...] = jnp.sum(dcw8[...], axis=1)

    cur = lambda w: pl.BlockSpec((tm, w), lambda i: (i, 0))
    prev = lambda w: pl.BlockSpec((HALO, w), lambda i: (jnp.maximum(i * nh - 1, 0), 0))
    nxt = lambda w: pl.BlockSpec((HALO, w), lambda i: (jnp.minimum((i + 1) * nh, S // HALO - 1), 0))
    acc = lambda r, w: pl.BlockSpec((r, w), lambda i: (0, 0))
    sd = jax.ShapeDtypeStruct
    return pl.pallas_call(
        body, name="conv_bwd", grid=(nt,),
        in_specs=[cur(PU), prev(PU), cur(CONV_CH), nxt(CONV_CH), acc(32, CONV_CH)],
        out_specs=[cur(PU), acc(32, CONV_CH), acc(1, CONV_CH)],
        out_shape=[sd((S, PU), BF16), sd((32, CONV_CH), F32), sd((1, CONV_CH), F32)],
        scratch_shapes=[pltpu.VMEM((8, tm + HALO, CONV_CH), F32), pltpu.VMEM((8, tm + HALO, CONV_CH), F32),
                        pltpu.VMEM((32, 8, CONV_CH), F32)],
        compiler_params=_cp(),
    )(pu, pu, dyc, dyc, conv_w)


def _attn_bwd(p, do, lse, delta, cosT, sinT, gq, gk, sinks, mk, mv, gqm, tm, rider=None):
    S = p.shape[0]
    M = mk.shape[0]
    nb = tm // BLK
    nt = S // tm
    nblocks = S // BLK
    W = SLOT * N_MEMH

    def body(p_ref, pp_ref, pn_ref, dy_ref, dyn_ref, lse_ref, lsen_ref, del_ref, deln_ref,
             cos_ref, cosp_ref, cosn_ref, sin_ref, sinp_ref, sinn_ref,
             gq_ref, gk_ref, sink_ref, mk_ref, mv_ref, gqm_ref,
             dp_ref, dgq_ref, dgk_ref, dgqm_ref, dsink_ref, dmk_ref, dmv_ref):
        i = pl.program_id(0)

        @pl.when(i == 0)
        def _():
            for r in (dgq_ref, dgk_ref, dgqm_ref, dsink_ref, dmk_ref, dmv_ref):
                r[...] = jnp.zeros_like(r)

        lane = _lane(tm)
        lane_e = _lane(tm + BLK)

        cos_k = jnp.concatenate([cosp_ref[...], cos_ref[...]], axis=0)
        sin_k = jnp.concatenate([sinp_ref[...], sin_ref[...]], axis=0)
        cos_q = jnp.concatenate([cos_ref[...], cosn_ref[...]], axis=0)
        sin_q = jnp.concatenate([sin_ref[...], sinn_ref[...]], axis=0)
        lse_e = jnp.concatenate([lse_ref[...], lsen_ref[...]], axis=1)
        del_e = jnp.concatenate([del_ref[...], deln_ref[...]], axis=1)
        kj = lax.broadcasted_iota(jnp.int32, (BLK, GROUP * BLK), 0)
        qi = lax.broadcasted_iota(jnp.int32, (BLK, GROUP * BLK), 1) & (BLK - 1)
        diag = kj <= qi
        offd = kj > qi
        dgq = jnp.zeros((1, SLOT), F32)
        dgk = jnp.zeros((1, SLOT), F32)
        dsink = jnp.zeros((1, SLOT), F32)
        lane1 = lax.broadcasted_iota(jnp.int32, (1, SLOT), 1)
        k_pair = jnp.concatenate([pp_ref[:, KO:KO + SLOT], p_ref[:, KO:KO + SLOT]], axis=0)
        k_pair = _pair_fwd(k_pair, gk_ref[...], cos_k, sin_k, lane_e)
        v_pair = jnp.concatenate([pp_ref[:, VO:VO + SLOT], p_ref[:, VO:VO + SLOT]], axis=0)
        k_e = [_lo(k_pair, kvh, lane_e).astype(BF16) for kvh in range(N_KV)]
        v_e = [_lo(v_pair, kvh, lane_e).astype(BF16) for kvh in range(N_KV)]
        dk = [[jnp.zeros((BLK, SLOT), F32) for _ in range(nb)] for _ in range(N_KV)]
        dv = [[jnp.zeros((BLK, SLOT), F32) for _ in range(nb)] for _ in range(N_KV)]
        q_e, do_e = [], []
        for j in range(N_Q // 2):
            js = slice(SLOT * j, SLOT * (j + 1))
            q_pair = _pair_fwd(jnp.concatenate([p_ref[:, js], pn_ref[:, js]], axis=0), gq_ref[...],
                               cos_q, sin_q, lane_e)
            do_pair = jnp.concatenate([dy_ref[:, js], dyn_ref[:, js]], axis=0).astype(F32)
            for half in range(2):
                q_e.append(_lo(q_pair, half, lane_e).astype(BF16))
                do_e.append(_lo(do_pair, half, lane_e).astype(BF16))
        dq_heads = [None] * N_Q
        for kvh in range(N_KV):
            hs = [GROUP * kvh + gi for gi in range(GROUP)]
            dq3 = [None] * nb
            for m in range(nb + 1):
                rows = slice(BLK * m, BLK * (m + 1))
                q3 = jnp.concatenate([q_e[h][rows] for h in hs], axis=0)
                do3 = jnp.concatenate([do_e[h][rows] for h in hs], axis=0)
                lb3 = jnp.concatenate([lse_e[h:h + 1, rows] for h in hs], axis=1)
                db3 = jnp.concatenate([del_e[h:h + 1, rows] for h in hs], axis=1)
                for n in (m - 1, m):
                    if n == nb:
                        continue
                    krows = slice(BLK * (n + 1), BLK * (n + 2))
                    kb, vb = k_e[kvh][krows], v_e[kvh][krows]
                    s = _dot_nt(kb, q3) * SCALE
                    mask = diag if n == m else offd
                    if n == -1:
                        mask = mask & (i > 0)
                    if m == nb:
                        mask = mask & (i < nt - 1)
                    prob = jnp.where(mask, jnp.exp(jnp.where(mask, s - lb3, NEG)), 0.0)
                    dpb = _dot_nt(vb, do3)
                    ds = (prob * (dpb - db3) * SCALE).astype(BF16)
                    if m < nb:
                        dqc = _dot_tn(ds, kb)
                        dq3[m] = dqc if dq3[m] is None else dq3[m] + dqc
                    if n >= 0:
                        dk[kvh][n] = dk[kvh][n] + _dot_nn(ds, q3)
                        dv[kvh][n] = dv[kvh][n] + _dot_nn(prob.astype(BF16), do3)
            for gi, h in enumerate(hs):
                dq_heads[h] = jnp.concatenate([dq3[m][BLK * gi:BLK * (gi + 1)] for m in range(nb)], axis=0)
                psink = jnp.exp(sink_ref[h] - lse_e[h:h + 1, 0:tm])
                dsink = dsink + jnp.where(
                    lane1 == h, -jnp.sum(psink * del_e[h:h + 1, 0:tm], axis=-1, keepdims=True), 0.0)
        for j in range(N_Q // 2):
            js = slice(SLOT * j, SLOT * (j + 1))
            dqr, prod = _pair_bwd(_pack(dq_heads[2 * j], dq_heads[2 * j + 1], lane), p_ref[:, js], gq_ref[...],
                                  cos_ref[...], sin_ref[...], lane)
            dp_ref[:, js] = dqr.astype(BF16)
            dgq = dgq + _colsum(prod)
        dk_pair = _pack(jnp.concatenate(dk[0], axis=0), jnp.concatenate(dk[1], axis=0), lane)
        dkr, prod = _pair_bwd(dk_pair, p_ref[:, KO:KO + SLOT], gk_ref[...], cos_ref[...], sin_ref[...], lane)
        dp_ref[:, KO:KO + SLOT] = dkr.astype(BF16)
        dp_ref[:, VO:VO + SLOT] = _pack(jnp.concatenate(dv[0], axis=0), jnp.concatenate(dv[1], axis=0),
                                        lane).astype(BF16)
        dgq_ref[...] += dgq
        dgk_ref[...] += _colsum(prod)
        dsink_ref[...] += dsink

        dgqm = jnp.zeros((1, SLOT), F32)
        dq_heads = []
        for hm in range(N_MEMH):
            ms = slice(SLOT * hm, SLOT * (hm + 1))
            js = slice(MO + SLOT * (hm // 2), MO + SLOT * (hm // 2 + 1))
            os_ = slice(SLOT * ((N_Q + hm) // 2), SLOT * ((N_Q + hm) // 2 + 1))
            if hm % 2 == 0:
                qm_pair = _pair_fwd(p_ref[:, js], gqm_ref[...], None, None, lane)
                do_pair = dy_ref[:, os_].astype(F32)
            qm = _lo(qm_pair, hm % 2, lane).astype(BF16)
            dob = _lo(do_pair, hm % 2, lane).astype(BF16)
            kb, vb = mk_ref[:, ms], mv_ref[:, ms]
            s = _dot_nt(kb, qm) * SCALE
            prob = jnp.exp(s - lse_ref[N_Q + hm:N_Q + hm + 1, :])
            dpb = _dot_nt(vb, dob)
            ds = (prob * (dpb - del_ref[N_Q + hm:N_Q + hm + 1, :]) * SCALE).astype(BF16)
            dq_heads.append(_dot_tn(ds, kb))
            dmk_ref[:, ms] += _dot_nn(ds, qm)
            dmv_ref[:, ms] += _dot_nn(prob.astype(BF16), dob)
            if hm % 2 == 1:
                dqr, prod = _pair_bwd(_pack(dq_heads[-2], dq_heads[-1], lane), p_ref[:, js], gqm_ref[...],
                                      None, None, lane)
                dp_ref[:, js] = dqr.astype(BF16)
                dgqm = dgqm + _colsum(prod)
        dgqm_ref[...] += dgqm

    cur = lambda w: pl.BlockSpec((tm, w), lambda i: (i, 0))
    prev = lambda w: pl.BlockSpec((BLK, w), lambda i: (jnp.maximum(i * nb - 1, 0), 0))
    nxt = lambda w: pl.BlockSpec((BLK, w), lambda i: (jnp.minimum((i + 1) * nb, nblocks - 1), 0))
    full = lambda a: pl.BlockSpec(a.shape, lambda i: (0,) * a.ndim)
    acc = lambda r, w: pl.BlockSpec((r, w), lambda i: (0, 0))
    sd = jax.ShapeDtypeStruct
    stat = pl.BlockSpec((STAT_ROWS, tm), lambda i: (0, i))
    stat_n = pl.BlockSpec((STAT_ROWS, BLK), lambda i: (0, jnp.minimum((i + 1) * nb, nblocks - 1)))
    return _call(
        body, name="attn_bwd", grid=(nt,),
        in_specs=[cur(PH), prev(PH), nxt(PH), cur(YH), nxt(YH), stat, stat_n, stat, stat_n,
                  cur(SLOT), prev(SLOT), nxt(SLOT), cur(SLOT), prev(SLOT), nxt(SLOT),
                  full(gq), full(gk), pl.BlockSpec(memory_space=pltpu.SMEM), full(mk), full(mv), full(gqm)],
        out_specs=[cur(PH), acc(1, SLOT), acc(1, SLOT), acc(1, SLOT), acc(1, SLOT), acc(M, W), acc(M, W)],
        out_shape=[sd((S, PH), BF16), sd((1, SLOT), F32), sd((1, SLOT), F32), sd((1, SLOT), F32),
                   sd((1, SLOT), F32), sd((M, W), F32), sd((M, W), F32)],
        args=[p, p, p, do, do, lse, lse, delta, delta, cosT, cosT, cosT, sinT, sinT, sinT,
              gq, gk, sinks, mk, mv, gqm],
        rider=rider)


def _proj_bwd(dpu, dph, h, dh2, g, n, w_inp, tm):
    S, D = h.shape

    def body(dpu_ref, dph_ref, h_ref, dh2_ref, g_ref, n_ref, w_ref, dh_ref, dg_ref, dw_ref):
        i = pl.program_id(0)

        @pl.when(i == 0)
        def _():
            dg_ref[...] = jnp.zeros_like(dg_ref)
            dw_ref[...] = jnp.zeros_like(dw_ref)

        dpu, dph, nv = dpu_ref[...], dph_ref[...], n_ref[...]
        dn = _dot_nn(dpu, w_ref[0:PU, :]) + _dot_nn(dph, w_ref[PU:PP, :])
        dw_ref[0:PU, :] += _dot_tn(dpu, nv)
        dw_ref[PU:PP, :] += _dot_tn(dph, nv)
        hv = h_ref[...]
        dx, dgrow = _rms_bwd(dn, hv, _rms(hv), g_ref[...])
        dh_ref[...] = dh2_ref[...] + dx
        dg_ref[...] += _colsum(dgrow)

    cur = lambda w: pl.BlockSpec((tm, w), lambda i: (i, 0))
    sd = jax.ShapeDtypeStruct
    return pl.pallas_call(
        body, name="proj_bwd", grid=(S // tm,),
        in_specs=[cur(PU), cur(PH), cur(D), cur(D), pl.BlockSpec((1, D), lambda i: (0, 0)), cur(D),
                  pl.BlockSpec((PP, D), lambda i: (0, 0))],
        out_specs=[cur(D), pl.BlockSpec((1, D), lambda i: (0, 0)), pl.BlockSpec((PP, D), lambda i: (0, 0))],
        out_shape=[sd((S, D), F32), sd((1, D), F32), sd((PP, D), F32)],
        compiler_params=_cp(),
    )(dpu, dph, h, dh2, g, n, w_inp)


def _norm_bwd(dxn, h, g, tm):
    S, D = h.shape

    def body(d_ref, h_ref, g_ref, dh_ref, dg_ref):
        @pl.when(pl.program_id(0) == 0)
        def _():
            dg_ref[...] = jnp.zeros_like(dg_ref)

        hv = h_ref[...]
        dx, dgrow = _rms_bwd(d_ref[...], hv, _rms(hv), g_ref[...])
        dh_ref[...] = dx
        dg_ref[...] += _colsum(dgrow)

    cur = pl.BlockSpec((tm, D), lambda i: (i, 0))
    vec = pl.BlockSpec((1, D), lambda i: (0, 0))
    return pl.pallas_call(
        body, name="norm_bwd", grid=(S // tm,), in_specs=[cur, cur, vec], out_specs=[cur, vec],
        out_shape=[jax.ShapeDtypeStruct((S, D), F32), jax.ShapeDtypeStruct((1, D), F32)],
        compiler_params=_cp(),
    )(dxn, h, g)


def _loss_bwd(xn, h, g, target, tm):
    S, D = h.shape

    def body(y_ref, h_ref, g_ref, t_ref, loss_ref, dh_ref, dg_ref):
        @pl.when(pl.program_id(0) == 0)
        def _():
            dg_ref[...] = jnp.zeros_like(dg_ref)
            loss_ref[...] = jnp.zeros_like(loss_ref)

        err = y_ref[...] - t_ref[...]
        part = jnp.sum(jnp.mean(err * err, axis=-1, keepdims=True), axis=0, keepdims=True)
        loss_ref[...] += 0.5 * part
        hv = h_ref[...]
        dx, dgrow = _rms_bwd(err * (1.0 / D), hv, _rms(hv), g_ref[...])
        dh_ref[...] = dx
        dg_ref[...] += _colsum(dgrow)

    cur = pl.BlockSpec((tm, D), lambda i: (i, 0))
    vec = pl.BlockSpec((1, D), lambda i: (0, 0))
    return pl.pallas_call(
        body, name="loss_bwd", grid=(S // tm,), in_specs=[cur, cur, vec, cur],
        out_specs=[pl.BlockSpec((1, SLOT), lambda i: (0, 0)), cur, vec],
        out_shape=[jax.ShapeDtypeStruct((1, SLOT), F32), jax.ShapeDtypeStruct((S, D), F32),
                   jax.ShapeDtypeStruct((1, D), F32)],
        compiler_params=_cp(),
    )(xn, h, g, target)


def _swap_with_sibling(bufs):
    nbuf = len(bufs)

    def body(*refs):
        ins, outs = refs[:nbuf], refs[nbuf:2 * nbuf]
        ssem, rsem = refs[2 * nbuf:]
        x, y, c, _ = _place()
        sends = [pltpu.make_async_remote_copy(src_ref=ins[b], dst_ref=outs[b], send_sem=ssem.at[b],
                                              recv_sem=rsem.at[b], device_id=(x, y, 1 - c), device_id_type=MESH)
                 for b in range(nbuf)]
        for cp in sends:
            cp.start()
        for cp in sends:
            cp.wait_recv()
        for cp in sends:
            cp.wait_send()

    hbm = pl.BlockSpec(memory_space=pl.ANY)
    return pl.pallas_call(
        body, name="swap_with_sibling",
        in_specs=[hbm] * nbuf, out_specs=[hbm] * nbuf,
        out_shape=[jax.ShapeDtypeStruct(b.shape, b.dtype) for b in bufs],
        scratch_shapes=[pltpu.SemaphoreType.DMA((nbuf,)), pltpu.SemaphoreType.DMA((nbuf,))],
    )(*bufs)


def _all_gather_small(buf):
    _, R, W = buf.shape

    def body(in_ref, out_ref, ssem, rsem, lsem):
        x, y, c, _ = _place()
        me = 4 * x + 2 * y + c
        local = pltpu.make_async_copy(in_ref, out_ref.at[pl.ds(me, 1)], lsem)
        local.start()

        def copy(k, block):
            fx, fy, fc = (k >> 2) & 1, (k >> 1) & 1, k & 1
            peer = (x ^ fx, y ^ fy, c ^ fc)
            return pltpu.make_async_remote_copy(
                src_ref=in_ref, dst_ref=out_ref.at[pl.ds(block, 1)], send_sem=ssem.at[k - 1],
                recv_sem=rsem.at[k - 1], device_id=peer, device_id_type=MESH)

        sends = [copy(k, me) for k in range(1, 8)]
        for cp in sends:
            cp.start()
        for k in range(1, 8):
            copy(k, me ^ k).wait_recv()
        for cp in sends:
            cp.wait_send()
        local.wait()

    hbm = pl.BlockSpec(memory_space=pl.ANY)
    return pl.pallas_call(
        body, name="all_gather_small", in_specs=[hbm], out_specs=hbm,
        out_shape=jax.ShapeDtypeStruct((8, R, W), buf.dtype),
        scratch_shapes=[pltpu.SemaphoreType.DMA((7,)), pltpu.SemaphoreType.DMA((7,)), pltpu.SemaphoreType.DMA],
    )(buf)


def _row_tile(n, cap=1024):
    for t in range(min(n, cap) // 8 * 8, 7, -8):
        if n % t == 0:
            return t
    return n


def _sum4(own, recv):
    n, rows, D = own.shape
    tr = _row_tile(rows)

    def body(o_ref, r0_ref, r1_ref, r2_ref, out_ref):
        out_ref[...] = ((o_ref[...].astype(F32) + r0_ref[...].astype(F32)) + r1_ref[...].astype(F32)) \
            + r2_ref[...].astype(F32)

    def rspec(p):
        return pl.BlockSpec((None, None, None, tr, D), lambda k, i, p=p: (p, k, 0, i, 0))

    blk = pl.BlockSpec((None, tr, D), lambda k, i: (k, i, 0))
    return pl.pallas_call(
        body, name="sum4", grid=(n, rows // tr),
        in_specs=[blk, rspec(0), rspec(1), rspec(2)], out_specs=blk,
        out_shape=jax.ShapeDtypeStruct((n, rows, D), F32),
    )(own, recv, recv, recv)


def _adam_math(w, g, m, v):
    m = ADAM_B1 * m + (1.0 - ADAM_B1) * g
    v = ADAM_B2 * v + (1.0 - ADAM_B2) * (g * g)
    m_hat = m / (1.0 - ADAM_B1 ** ADAM_STEP)
    v_hat = v / (1.0 - ADAM_B2 ** ADAM_STEP)
    delta = -ADAM_LR * (m_hat / (jnp.sqrt(v_hat) + ADAM_EPS) + ADAM_WD * w)
    return delta, m, v


def _adam_fused(w, m, v, parts, theirs, sel, row0, nrows):
    L, R, D = w.shape
    assert R == nrows and parts[0].shape[2] == D
    t = _row_tile(nrows if row0 == 0 else _gcd(row0, nrows), 512)

    def gspec(k):
        return pl.BlockSpec((None, t, D), lambda l, c: (sel, row0 // t + jnp.where(l == k, c, 0), 0))

    def body(*refs):
        w_ref, m_ref, v_ref = refs[:3]
        p_refs, q_refs = refs[3:3 + L], refs[3 + L:3 + 2 * L]
        g_ref, d_ref, nm_ref, nv_ref, g_scr = refs[3 + 2 * L:]
        l = pl.program_id(0)
        for k in range(L):
            @pl.when(l == k)
            def _(k=k):
                g_scr[...] = p_refs[k][...] + q_refs[k][...]

        g = g_scr[...]
        g_ref[...] = g
        d, m_, v_ = _adam_math(w_ref[...], g, m_ref[...], v_ref[...])
        d_ref[...] = d
        nm_ref[...] = m_
        nv_ref[...] = v_

    blk = pl.BlockSpec((None, t, D), lambda l, c: (l, c, 0))
    return pl.pallas_call(
        body, name="adam_fused", grid=(L, R // t),
        in_specs=[blk] * 3 + [gspec(k) for k in range(L)] * 2, out_specs=[blk] * 4,
        out_shape=[jax.ShapeDtypeStruct((L, R, D), F32)] * 4,
        scratch_shapes=[pltpu.VMEM((t, D), F32)],
        compiler_params=_cp(),
    )(w, m, v, *parts, *theirs)


def _gcd(a, b):
    while b:
        a, b = b, a % b
    return a


def _small_sum_adam(g8, w, m, v):
    _, R, W = g8.shape

    def body(g_ref, w_ref, m_ref, v_ref, go_ref, d_ref, nm_ref, nv_ref):
        g = g_ref[0]
        for k in range(1, 8):
            g = g + g_ref[k]
        go_ref[...] = g
        d, m_, v_ = _adam_math(w_ref[...], g, m_ref[...], v_ref[...])
        d_ref[...] = d
        nm_ref[...] = m_
        nv_ref[...] = v_

    return pl.pallas_call(body, name="small_sum_adam",
                          out_shape=[jax.ShapeDtypeStruct((R, W), F32)] * 4)(g8, w, m, v)


def _pad_vec(v):
    return jnp.pad(v, (0, SLOT - v.shape[0]))[None, :]


class _Pack:
    def __init__(self, shapes):
        self.shapes = shapes
        self.sizes = [int(functools.reduce(lambda a, b: a * b, s, 1)) for s in shapes]
        total = sum(self.sizes)
        self.rows = -(-total // (8 * SLOT)) * 8
        self.pad = self.rows * SLOT - total

    def pack(self, arrs):
        flat = jnp.concatenate([a.reshape(-1).astype(F32) for a in arrs] + [jnp.zeros((self.pad,), F32)])
        return flat.reshape(self.rows, SLOT)

    def unpack(self, buf):
        flat, out, o = buf.reshape(-1), [], 0
        for s, n in zip(self.shapes, self.sizes):
            out.append(flat[o:o + n].reshape(s))
            o += n
        return out


def kernel(x, mem, positions, ffn1_norm, ffn1_w1, ffn1_w3, ffn1_w2, mix_norm, w_in, conv_w, conv_b, conv_ln_g, conv_ln_b, swa_q_norm, swa_k_norm, swa_sinks, mem_norm, w_mem_kv, mem_q_norm, mem_k_norm, w_out, ffn2_norm, ffn2_w1, ffn2_w3, ffn2_w2, final_norm, loss_target, m_ffn1_norm, m_ffn1_w1, m_ffn1_w3, m_ffn1_w2, m_mix_norm, m_w_in, m_conv_w, m_conv_b, m_conv_ln_g, m_conv_ln_b, m_swa_q_norm, m_swa_k_norm, m_swa_sinks, m_mem_norm, m_w_mem_kv, m_mem_q_norm, m_mem_k_norm, m_w_out, m_ffn2_norm, m_ffn2_w1, m_ffn2_w3, m_ffn2_w2, m_final_norm, v_ffn1_norm, v_ffn1_w1, v_ffn1_w3, v_ffn1_w2, v_mix_norm, v_w_in, v_conv_w, v_conv_b, v_conv_ln_g, v_conv_ln_b, v_swa_q_norm, v_swa_k_norm, v_swa_sinks, v_mem_norm, v_w_mem_kv, v_mem_q_norm, v_mem_k_norm, v_w_out, v_ffn2_norm, v_ffn2_w1, v_ffn2_w3, v_ffn2_w2, v_final_norm):
    names = ['ffn1_norm', 'ffn1_w1', 'ffn1_w3', 'ffn1_w2', 'mix_norm', 'w_in', 'conv_w', 'conv_b', 'conv_ln_g',
             'conv_ln_b', 'swa_q_norm', 'swa_k_norm', 'swa_sinks', 'mem_norm', 'w_mem_kv', 'mem_q_norm',
             'mem_k_norm', 'w_out', 'ffn2_norm', 'ffn2_w1', 'ffn2_w3', 'ffn2_w2', 'final_norm']
    loc = locals()
    W = {n: loc[n] for n in names}
    M1 = {n: loc['m_' + n] for n in names}
    V1 = {n: loc['v_' + n] for n in names}

    S, D = x.shape[1], x.shape[2]
    L = ffn1_norm.shape[0]
    Fs = ffn1_w1.shape[2]
    F = 4 * Fs
    Mlen = mem.shape[1]
    cw_sh = conv_w.shape[2]
    tm = 512 if S >= 2048 else 256
    tf = 1408 if F % 1408 == 0 else 256
    tfw = 256
    tmw = 2048 if S >= 2048 else 256
    x0 = x[0]
    mem0 = mem[0]
    target = loss_target[0]
    my_chip = 2 * lax.axis_index("x") + lax.axis_index("y")

    mkv_rows = w_mem_kv.shape[1] * MEM_KV // D
    r_in, r_out = D_IN // 4, D_MIX // 4
    rm = r_in + r_out + mkv_rows

    shard = lambda w: w.astype(BF16).reshape((1, 1) + w.shape)

    groups = []
    for l in range(L):
        groups.append([shard(ffn1_w1[l].T), shard(ffn1_w3[l].T), shard(ffn1_w2[l])])
        groups.append([shard(w_in[l].T), shard(w_out[l]), shard(w_mem_kv[l].reshape(mkv_rows, D))])
        groups.append([shard(ffn2_w1[l].T), shard(ffn2_w3[l].T), shard(ffn2_w2[l])])
    gathered = [None] * len(groups)
    cw_rows = -(-(L * CONV_WIDTH) // 8) * 8
    cw_pad = jnp.pad(conv_w.reshape(L * CONV_WIDTH, cw_sh), ((0, cw_rows - L * CONV_WIDTH), (0, SLOT - cw_sh)))
    *gathered[0], cw_g = _run_rider(_Gather(groups[0] + [cw_pad.reshape(1, 1, cw_rows, SLOT)]), "all_gather_first")
    conv_wF = cw_g[0, :, :L * CONV_WIDTH, :cw_sh].reshape(4, L, CONV_WIDTH, cw_sh)
    conv_wF = jnp.moveaxis(conv_wF, 0, 2).reshape(L, CONV_WIDTH, 4 * cw_sh)
    conv_wP = jnp.pad(conv_wF, ((0, 0), (0, 32 - CONV_WIDTH), (0, 0)))

    def gather_rider(j):
        want = [k for k in ([1, 2] if j == 0 else [j + 2]) if k < len(groups)]
        return (_Gather([b for k in want for b in groups[k]]), want) if want else (None, want)

    def keep(want, got):
        for n, k in enumerate(want):
            gathered[k] = got[3 * n:3 * n + 3]

    def ffn_weights(j):
        return tuple(g.reshape(F, D) for g in gathered[j])

    def mix_weights(l):
        g_in, g_out, g_mkv = gathered[3 * l + 1]
        w_inp = g_in.reshape(D_IN, D)
        w_outp = g_out.reshape(D_MIX, D)
        w_mkvp = jnp.pad(g_mkv.reshape(D, 2 * N_MEMH, HEAD_DIM),
                         ((0, 0), (0, 0), (0, SLOT - HEAD_DIM))).reshape(D, 2 * N_MEMH * SLOT)
        return w_inp, w_outp, w_mkvp

    inv_freq = ROPE_THETA ** (-jnp.arange(0, HEAD_DIM, 2, dtype=F32) / HEAD_DIM)
    invf = jnp.tile(inv_freq, SLOT // (HEAD_DIM // 2))[None, :]
    cosT, sinT = _rope_tables(positions.reshape(S, 1), invf, tm)

    row = lambda a, l: a[l][None, :]
    sinks_p = jnp.pad(swa_sinks, ((0, 0), (0, 8 - N_Q)))

    saved = []
    xin = x0
    xn = None
    for l in range(L):
        wf1 = ffn_weights(3 * l)
        rider, want = gather_rider(3 * l)
        (h1, a1, b1, t1), got = _ffn_fwd(xin, row(ffn1_norm, l), wf1, None, tm, tf, rider=rider)
        keep(want, got)
        w_inp, w_outp, w_mkvp = mix_weights(l)
        pu, p, n2 = _proj_fwd(h1, row(mix_norm, l), w_inp, tm)
        gk_m = _pad_vec(mem_k_norm[l])
        nm, mraw, mk, mv = _mem_kv_fwd(mem0, row(mem_norm, l), w_mkvp, gk_m)
        twice = lambda v: jnp.tile(v, 2)[None, :]
        gq, gk, gqm = twice(swa_q_norm[l]), twice(swa_k_norm[l]), twice(mem_q_norm[l])
        rider, want = gather_rider(3 * l + 1)
        (h2, y, yc, lse), got = _mixer_fwd(pu, p, h1, cosT, sinT, conv_wP[l], row(conv_b, l), row(conv_ln_g, l),
                                           row(conv_ln_b, l), gq, gk, sinks_p[l], mk, mv, gqm, w_outp, tm,
                                           rider=rider)
        keep(want, got)
        wf2 = ffn_weights(3 * l + 2)
        rider, want = gather_rider(3 * l + 2)
        (h3, a2, b2, t2, xn), got = _ffn_fwd(h2, row(ffn2_norm, l), wf2, row(final_norm, l), tm, tf, rider=rider)
        keep(want, got)
        saved.append(dict(xin=xin, h1=h1, a1=a1, b1=b1, pu=pu, p=p, n2=n2, nm=nm, mraw=mraw, mk=mk, mv=mv, gk_m=gk_m,
                          gq=gq, gk=gk, gqm=gqm, h2=h2, y=y, yc=yc, lse=lse, h3=h3, a2=a2, b2=b2, t1=t1, t2=t2,
                          wf1=wf1, wf2=wf2, w_inp=w_inp, w_outp=w_outp, w_mkvp=w_mkvp))
        xin = xn

    G = {n: [None] * L for n in names}
    ffn_bufs = [None] * (2 * L)
    mix_bufs = [None] * L
    ffn_recv = [None] * (2 * L)
    mix_recv = [None] * L
    dxn = None
    loss_part = None
    for l in reversed(range(L)):
        sv = saved[l]
        if l == L - 1:
            loss_part, dh3, G['final_norm'][l] = _loss_bwd(xn, sv['h3'], row(final_norm, l), target, tm)
        else:
            dh3, G['final_norm'][l] = _norm_bwd(dxn, sv['h3'], row(final_norm, l), tm)
        rider = _Scatter([ffn_bufs[2 * l + 2]]) if l < L - 1 else None
        (dh2, G['ffn2_norm'][l], da, db, n, dy), got = _ffn_bwd_act(
            dh3, sv['h2'], row(ffn2_norm, l), sv['a2'], sv['b2'], sv['wf2'], tm, tf, rider=rider)
        if got:
            ffn_recv[2 * l + 2] = got[0]
        ffn_bufs[2 * l + 1] = _ffn_bwd_w(da, db, sv['t2'], n, dy, tmw, tfw).reshape(3, 4, Fs, D)
        dyc, do, delta, dwo, G['conv_ln_g'][l], G['conv_ln_b'][l] = _outproj_bwd(
            dh2, sv['y'], sv['yc'], row(conv_ln_g, l), row(conv_ln_b, l), sv['w_outp'], tm)
        (dph, dgq, dgk, dgqm, dsink, dmk, dmv), got = _attn_bwd(
            sv['p'], do, sv['lse'], delta, cosT, sinT, sv['gq'], sv['gk'], sinks_p[l],
            sv['mk'], sv['mv'], sv['gqm'], tm, rider=_Scatter([ffn_bufs[2 * l + 1]]))
        ffn_recv[2 * l + 1] = got[0]
        dpu, dcw, G['conv_b'][l] = _conv_bwd(sv['pu'], dyc, conv_wP[l], tm)
        dwm, G['mem_norm'][l], dgk_m = _mem_kv_bwd(dmk, dmv, sv['mraw'], sv['nm'], mem0, row(mem_norm, l),
                                                   sv['w_mkvp'], sv['gk_m'])
        dh1, G['mix_norm'][l], dwi = _proj_bwd(dpu, dph, sv['h1'], dh2, row(mix_norm, l), sv['n2'], sv['w_inp'], tm)
        dwiT = dwi.reshape(4, r_in, D)
        dwoF = dwo.reshape(4, r_out, D)
        dwmF = dwm.reshape(D, 2 * N_MEMH, SLOT)[:, :, :HEAD_DIM].reshape(4, mkv_rows, D)
        mix_bufs[l] = jnp.concatenate([dwiT, dwoF, dwmF], axis=1).astype(BF16).reshape(1, 4, rm, D)
        (dxl, G['ffn1_norm'][l], da, db, n, dy), got = _ffn_bwd_act(
            dh1, sv['xin'], row(ffn1_norm, l), sv['a1'], sv['b1'], sv['wf1'], tm, tf,
            rider=_Scatter([mix_bufs[l]]))
        mix_recv[l] = got[0]
        ffn_bufs[2 * l] = _ffn_bwd_w(da, db, sv['t1'], n, dy, tmw, tfw).reshape(3, 4, Fs, D)
        dxn = dxl
        G['conv_w'][l] = dcw[:CONV_WIDTH]
        G['swa_q_norm'][l] = dgq[0, :HEAD_DIM] + dgq[0, HEAD_DIM:]
        G['swa_k_norm'][l] = dgk[0, :HEAD_DIM] + dgk[0, HEAD_DIM:]
        G['mem_q_norm'][l] = dgqm[0, :HEAD_DIM] + dgqm[0, HEAD_DIM:]
        G['mem_k_norm'][l] = dgk_m[0, :HEAD_DIM]
        G['swa_sinks'][l] = dsink[0, :N_Q]
    ffn_recv[0] = _run_rider(_Scatter([ffn_bufs[0]]), "scatter_last")[0]
    grad_x = dxn[None]
    loss = lax.psum(loss_part[0, 0], AXES)

    parts = []
    for b, r in zip(ffn_bufs + mix_bufs, ffn_recv + mix_recv):
        own = lax.dynamic_index_in_dim(b, my_chip, axis=1, keepdims=False)
        parts.append(_sum4(own, r))
    theirs = _swap_with_sibling(parts)

    small = ['ffn1_norm', 'mix_norm', 'conv_b', 'conv_ln_g', 'conv_ln_b', 'swa_q_norm', 'swa_k_norm', 'swa_sinks',
             'mem_norm', 'mem_q_norm', 'mem_k_norm', 'ffn2_norm', 'final_norm']
    gsmall = [jnp.stack([G[n][l].reshape(-1) for l in range(L)]) for n in small]
    gcw = jnp.stack(G['conv_w'])
    cw_cols = 4 * cw_sh
    full_of = lambda a: lax.dynamic_update_slice(jnp.zeros((L, CONV_WIDTH, cw_cols), F32), a, (0, 0, my_chip * cw_sh))
    pk = _Pack([W[n].shape for n in small] + [(L, CONV_WIDTH, cw_cols)])
    g8 = _all_gather_small(pk.pack(gsmall + [gcw])[None])
    outs4 = _small_sum_adam(g8, pk.pack([W[n] for n in small] + [full_of(conv_w)]),
                            pk.pack([M1[n] for n in small] + [full_of(m_conv_w)]),
                            pk.pack([V1[n] for n in small] + [full_of(v_conv_w)]))
    un = [pk.unpack(o) for o in outs4]
    grads, deltas, new_m, new_v = {}, {}, {}, {}
    for k, n in enumerate(small):
        grads[n], deltas[n], new_m[n], new_v[n] = un[0][k], un[1][k], un[2][k], un[3][k]
    mine = lambda a: lax.dynamic_slice(a, (0, 0, my_chip * cw_sh), (L, CONV_WIDTH, cw_sh))
    grads['conv_w'], deltas['conv_w'], new_m['conv_w'], new_v['conv_w'] = [mine(u[-1]) for u in un]

    halves = lambda idx: ([parts[i] for i in idx], [theirs[i] for i in idx])
    ffn1_h, ffn2_h = halves([2 * l for l in range(L)]), halves([2 * l + 1 for l in range(L)])
    mix_h = halves([2 * L + l for l in range(L)])
    plan = {'ffn1_w1': (ffn1_h, 0, 0, Fs, True), 'ffn1_w3': (ffn1_h, 1, 0, Fs, True), 'ffn1_w2': (ffn1_h, 2, 0, Fs, False),
            'ffn2_w1': (ffn2_h, 0, 0, Fs, True), 'ffn2_w3': (ffn2_h, 1, 0, Fs, True), 'ffn2_w2': (ffn2_h, 2, 0, Fs, False),
            'w_in': (mix_h, 0, 0, r_in, True), 'w_out': (mix_h, 0, r_in, r_out, False),
            'w_mem_kv': (mix_h, 0, r_in + r_out, mkv_rows, False)}
    for n, ((ps, qs), sel, row0, nrows, held_transposed) in plan.items():
        shp = W[n].shape
        if held_transposed:
            view, back = (lambda a: jnp.swapaxes(a, 1, 2)), (lambda a: jnp.swapaxes(a, 1, 2))
        elif n == 'w_mem_kv':
            view, back = (lambda a: a.reshape(L, mkv_rows, D)), (lambda a: a.reshape(shp))
        else:
            view = back = lambda a: a
        res = _adam_fused(view(W[n]), view(M1[n]), view(V1[n]), ps, qs, sel, row0, nrows)
        grads[n], deltas[n], new_m[n], new_v[n] = [back(r) for r in res]

    return (loss, grad_x, *[grads[n] for n in names], *[deltas[n] for n in names],
            *[new_m[n] for n in names], *[new_v[n] for n in names])
```

```python
import functools

import jax
import jax.numpy as jnp
from jax import lax
from jax.experimental import pallas as pl
from jax.experimental.pallas import tpu as pltpu

F32 = jnp.float32
BF16 = jnp.bfloat16
MESH = pl.DeviceIdType.MESH
AXES = ("x", "y", "c")

EPS = 1e-6
HEAD_DIM = 64
SLOT = 128
CONV_CH = 384
CONV_WIDTH = 31
N_Q, N_KV, N_MEMH = 6, 2, 4
GROUP = N_Q // N_KV
BLK = 128
HALO = 32
CONV_ROWS = 64
FFN_CHUNK = 256
ROPE_THETA = 10000.0
SCALE = HEAD_DIM ** -0.5
NEG = -1e30

N_HEADS_IN = N_Q + 2 * N_KV + N_MEMH
PU = 2 * CONV_CH
PH = HEAD_DIM * N_HEADS_IN
PP = PU + PH
QO = 0
KO = QO + HEAD_DIM * N_Q
VO = KO + HEAD_DIM * N_KV
MO = VO + HEAD_DIM * N_KV
NH = N_Q + N_MEMH
STAT_ROWS = 16
YH = HEAD_DIM * NH
YP = CONV_CH + YH
YS = CONV_CH
YM = YS + HEAD_DIM * N_Q
D_IN = PP
D_MIX = YP
MEM_KV = 2 * HEAD_DIM * N_MEMH

ADAM_LR, ADAM_B1, ADAM_B2, ADAM_EPS, ADAM_WD, ADAM_STEP = 0.001, 0.9, 0.999, 1e-08, 0.01, 10

VMEM_LIMIT_MB = 56


def _cp(mb=VMEM_LIMIT_MB):
    return pltpu.CompilerParams(vmem_limit_bytes=mb * 1024 * 1024)


def _dot_nn(a, b):
    return lax.dot_general(a, b, (((1,), (0,)), ((), ())), preferred_element_type=F32)


def _dot_nt(a, b):
    return lax.dot_general(a, b, (((1,), (1,)), ((), ())), preferred_element_type=F32)


def _dot_tn(a, b):
    return lax.dot_general(a, b, (((0,), (0,)), ((), ())), preferred_element_type=F32)


def _sigmoid(x):
    return 1.0 / (1.0 + jnp.exp(-x))


def _rms(x):
    return lax.rsqrt(jnp.mean(x * x, axis=-1, keepdims=True) + EPS)


def _rms_bwd(dn, x, r, g):
    xhat = x * r
    dxhat = dn * g
    dx = r * (dxhat - xhat * jnp.mean(dxhat * xhat, axis=-1, keepdims=True))
    return dx, dn * xhat


def _colsum(v):
    return jnp.sum(v, axis=0, keepdims=True)


def _lane(n):
    return lax.broadcasted_iota(jnp.int32, (n, SLOT), 1)


def _slot_rms(xs):
    return lax.rsqrt(jnp.sum(xs * xs, axis=-1, keepdims=True) * (1.0 / HEAD_DIM) + EPS)


def _slot_norm(xs, g):
    return xs * _slot_rms(xs) * g


def _slot_norm_bwd(dout, xs, g):
    r = _slot_rms(xs)
    xhat = xs * r
    dxhat = dout * g
    dx = r * (dxhat - xhat * (jnp.sum(dxhat * xhat, axis=-1, keepdims=True) * (1.0 / HEAD_DIM)))
    return dx, dout * xhat


def _halves(v, lane):
    lo = jnp.sum(jnp.where(lane < HEAD_DIM, v, 0.0), axis=-1, keepdims=True)
    hi = jnp.sum(jnp.where(lane < HEAD_DIM, 0.0, v), axis=-1, keepdims=True)
    return jnp.where(lane < HEAD_DIM, lo, hi)


def _pair_rms(x, lane):
    return lax.rsqrt(_halves(x * x, lane) * (1.0 / HEAD_DIM) + EPS)


def _pair_partner(v, lane):
    return jnp.where((lane & (HEAD_DIM - 1)) < HEAD_DIM // 2,
                     pltpu.roll(v, SLOT - HEAD_DIM // 2, 1), pltpu.roll(v, HEAD_DIM // 2, 1))


def _pair_fwd(x, g2, cosv, sinv, lane):
    xn = x * _pair_rms(x, lane) * g2
    if cosv is None:
        return xn
    return xn * cosv + _pair_partner(xn, lane) * sinv


def _pair_bwd(dout, x, g2, cosv, sinv, lane):
    if cosv is not None:
        dout = dout * cosv + _pair_partner(dout * sinv, lane)
    r = _pair_rms(x, lane)
    xhat = x * r
    dxhat = dout * g2
    dx = r * (dxhat - xhat * (_halves(dxhat * xhat, lane) * (1.0 / HEAD_DIM)))
    return dx, dout * xhat


def _lo(x, half, lane):
    if half:
        x = pltpu.roll(x, HEAD_DIM, 1)
    return jnp.where(lane < HEAD_DIM, x, 0.0)


def _pack(even, odd, lane):
    return jnp.where(lane < HEAD_DIM, even, pltpu.roll(odd, HEAD_DIM, 1))


def _place():
    x, y, c = lax.axis_index("x"), lax.axis_index("y"), lax.axis_index("c")
    chips = [(1 - x, y), (x, 1 - y), (1 - x, 1 - y)]
    return x, y, c, chips


class _Gather:
    tag = "_gather"

    def __init__(self, bufs):
        self.bufs = list(bufs)
        nb = len(self.bufs)
        self.out_shape = [jax.ShapeDtypeStruct((b.shape[0], 4) + b.shape[2:], b.dtype) for b in self.bufs]
        self.sems = [pltpu.SemaphoreType.DMA((3 * nb,)), pltpu.SemaphoreType.DMA((3 * nb,)),
                     pltpu.SemaphoreType.DMA((nb,))]

    def _copies(self, ins, outs, sems):
        ssem, rsem, lsem = sems
        nb = len(self.bufs)
        x, y, c, chips = _place()
        mine = 2 * x + y

        def copy(b, p, shard):
            return pltpu.make_async_remote_copy(
                src_ref=ins[b], dst_ref=outs[b].at[:, pl.ds(shard, 1)],
                send_sem=ssem.at[3 * b + p], recv_sem=rsem.at[3 * b + p],
                device_id=(chips[p][0], chips[p][1], c), device_id_type=MESH)

        local = [pltpu.make_async_copy(ins[b], outs[b].at[:, pl.ds(mine, 1)], lsem.at[b]) for b in range(nb)]
        sends = [copy(b, p, mine) for b in range(nb) for p in range(3)]
        recvs = [copy(b, p, 2 * chips[p][0] + chips[p][1]) for b in range(nb) for p in range(3)]
        return local, sends, recvs

    def start(self, ins, outs, sems):
        local, sends, _ = self._copies(ins, outs, sems)
        for cp in local + sends:
            cp.start()

    def wait(self, ins, outs, sems):
        local, sends, recvs = self._copies(ins, outs, sems)
        for cp in recvs:
            cp.wait_recv()
        for cp in sends:
            cp.wait_send()
        for cp in local:
            cp.wait()


class _Scatter:
    tag = "_scatter"

    def __init__(self, bufs):
        self.bufs = list(bufs)
        nb = len(self.bufs)
        self.out_shape = [jax.ShapeDtypeStruct((3, b.shape[0], 1) + b.shape[2:], b.dtype) for b in self.bufs]
        self.sems = [pltpu.SemaphoreType.DMA((3 * nb,)), pltpu.SemaphoreType.DMA((3 * nb,))]

    def _copies(self, ins, outs, sems):
        ssem, rsem = sems
        x, y, c, chips = _place()

        def copy(b, p):
            shard = 2 * chips[p][0] + chips[p][1]
            return pltpu.make_async_remote_copy(
                src_ref=ins[b].at[:, pl.ds(shard, 1)], dst_ref=outs[b].at[p],
                send_sem=ssem.at[3 * b + p], recv_sem=rsem.at[3 * b + p],
                device_id=(chips[p][0], chips[p][1], c), device_id_type=MESH)

        return [copy(b, p) for b in range(len(self.bufs)) for p in range(3)]

    def start(self, ins, outs, sems):
        for cp in self._copies(ins, outs, sems):
            cp.start()

    def wait(self, ins, outs, sems):
        cps = self._copies(ins, outs, sems)
        for cp in cps:
            cp.wait_recv()
        for cp in cps:
            cp.wait_send()


class _Swap:
    tag = "_swap"

    def __init__(self, bufs):
        self.bufs = list(bufs)
        nb = len(self.bufs)
        self.out_shape = [jax.ShapeDtypeStruct(b.shape, b.dtype) for b in self.bufs]
        self.sems = [pltpu.SemaphoreType.DMA((nb,)), pltpu.SemaphoreType.DMA((nb,))]

    def _copies(self, ins, outs, sems):
        ssem, rsem = sems
        x, y, c, _ = _place()
        return [pltpu.make_async_remote_copy(src_ref=ins[b], dst_ref=outs[b], send_sem=ssem.at[b],
                                             recv_sem=rsem.at[b], device_id=(x, y, 1 - c), device_id_type=MESH)
                for b in range(len(self.bufs))]

    def start(self, ins, outs, sems):
        for cp in self._copies(ins, outs, sems):
            cp.start()

    def wait(self, ins, outs, sems):
        cps = self._copies(ins, outs, sems)
        for cp in cps:
            cp.wait_recv()
        for cp in cps:
            cp.wait_send()


class _Multi:
    def __init__(self, riders):
        self.riders = [r for r in riders if r is not None and r.bufs]
        self.tag = "".join(r.tag for r in self.riders)
        self.bufs = [b for r in self.riders for b in r.bufs]
        self.out_shape = [s for r in self.riders for s in r.out_shape]
        self.sems = [s for r in self.riders for s in r.sems]

    def _split(self, ins, outs, sems):
        ob, os_ = 0, 0
        for r in self.riders:
            nb, ns = len(r.bufs), len(r.sems)
            yield r, ins[ob:ob + nb], outs[ob:ob + nb], sems[os_:os_ + ns]
            ob, os_ = ob + nb, os_ + ns

    def start(self, ins, outs, sems):
        for r, i, o, s in self._split(ins, outs, sems):
            r.start(i, o, s)

    def wait(self, ins, outs, sems):
        for r, i, o, s in self._split(ins, outs, sems):
            r.wait(i, o, s)

    def split_outputs(self, got):
        res, ob = [], 0
        for r in self.riders:
            res.append(got[ob:ob + len(r.bufs)])
            ob += len(r.bufs)
        return res


def _run_rider(rider, name):
    nb = len(rider.bufs)

    def body(*refs):
        ins, outs, sems = refs[:nb], refs[nb:2 * nb], refs[2 * nb:]
        rider.start(ins, outs, sems)
        rider.wait(ins, outs, sems)

    hbm = pl.BlockSpec(memory_space=pl.ANY)
    return pl.pallas_call(body, name=name, in_specs=[hbm] * nb, out_specs=[hbm] * nb,
                          out_shape=rider.out_shape, scratch_shapes=rider.sems)(*rider.bufs)


def _call(body, *, name, grid, in_specs, out_specs, out_shape, args, scratch=(), rider=None):
    if rider is None:
        outs = pl.pallas_call(body, name=name, grid=grid, in_specs=list(in_specs), out_specs=list(out_specs),
                              out_shape=list(out_shape), scratch_shapes=list(scratch),
                              compiler_params=_cp())(*args)
        return list(outs), None
    n_in, n_out, n_scr, nb = len(in_specs), len(out_specs), len(scratch), len(rider.bufs)

    def wrapped(*refs):
        cuts = [n_in, nb, n_out, nb, n_scr]
        parts, o = [], 0
        for n in cuts:
            parts.append(refs[o:o + n])
            o += n
        ins, rin, outs, rout, scr = parts
        sems = refs[o:]
        ids = [pl.program_id(k) for k in range(len(grid))]
        first = functools.reduce(jnp.logical_and, [i == 0 for i in ids])
        last = functools.reduce(jnp.logical_and, [i == n - 1 for i, n in zip(ids, grid)])

        @pl.when(first)
        def _():
            rider.start(rin, rout, sems)

        body(*ins, *outs, *scr)

        @pl.when(last)
        def _():
            rider.wait(rin, rout, sems)

    hbm = pl.BlockSpec(memory_space=pl.ANY)
    res = pl.pallas_call(
        wrapped, name=name + rider.tag, grid=grid,
        in_specs=list(in_specs) + [hbm] * nb, out_specs=list(out_specs) + [hbm] * nb,
        out_shape=list(out_shape) + rider.out_shape, scratch_shapes=list(scratch) + rider.sems,
        compiler_params=_cp())(*args, *rider.bufs)
    return list(res[:n_out]), list(res[n_out:])


def _rope_tables(pos, invf, tm):
    S = pos.shape[0]

    def body(pos_ref, f_ref, cos_ref, sin_ref):
        ang = pos_ref[...].astype(F32) * f_ref[...]
        lane = _lane(tm)
        cos_ref[...] = jnp.cos(ang)
        s = jnp.sin(ang)
        sin_ref[...] = jnp.where((lane & (HEAD_DIM - 1)) < HEAD_DIM // 2, -s, s)

    return pl.pallas_call(
        body, name="rope_tables", grid=(S // tm,),
        in_specs=[pl.BlockSpec((tm, 1), lambda i: (i, 0)), pl.BlockSpec((1, SLOT), lambda i: (0, 0))],
        out_specs=[pl.BlockSpec((tm, SLOT), lambda i: (i, 0))] * 2,
        out_shape=[jax.ShapeDtypeStruct((S, SLOT), F32)] * 2,
    )(pos, invf)


def _ffn_fwd(x, g, wf, gfin, tm, tf, rider=None):
    S, D = x.shape
    F = wf[0].shape[0]
    nf = F // tf
    final = gfin is not None

    chunks = [(c, min(FFN_CHUNK, tf - c)) for c in range(0, tf, FFN_CHUNK)]

    def body(*refs):
        if final:
            x_ref, g_ref, w1_ref, w3_ref, w2_ref, gf_ref, h_ref, a_ref, b_ref, t_ref, xn_ref, n_scr, acc = refs
        else:
            x_ref, g_ref, w1_ref, w3_ref, w2_ref, h_ref, a_ref, b_ref, t_ref, n_scr, acc = refs
        j = pl.program_id(1)

        @pl.when(j == 0)
        def _():
            xv = x_ref[...]
            n_scr[...] = (xv * _rms(xv) * g_ref[...]).astype(BF16)
            acc[...] = jnp.zeros_like(acc)

        n = n_scr[...]
        for c0, cw in chunks:
            cols = slice(c0, c0 + cw)
            a = _dot_nt(n, w1_ref[cols, :])
            b = _dot_nt(n, w3_ref[cols, :])
            a_ref[:, cols] = a.astype(BF16)
            b_ref[:, cols] = b.astype(BF16)
            t_ref[:, cols] = (a * _sigmoid(a) * b).astype(BF16)
        acc[...] += _dot_nn(t_ref[...], w2_ref[...])

        @pl.when(j == nf - 1)
        def _():
            h = x_ref[...] + 0.5 * acc[...]
            h_ref[...] = h
            if final:
                xn_ref[...] = h * _rms(h) * gf_ref[...]

    def wspec(k):
        return pl.BlockSpec((tf, D), lambda i, j: (j, 0))

    row = pl.BlockSpec((tm, D), lambda i, j: (i, 0))
    vec = pl.BlockSpec((1, D), lambda i, j: (0, 0))
    act = pl.BlockSpec((tm, tf), lambda i, j: (i, j))
    in_specs = [row, vec, wspec(0), wspec(1), wspec(2)] + ([vec] if final else [])
    out_specs = [row, act, act, act] + ([row] if final else [])
    out_shape = [jax.ShapeDtypeStruct((S, D), F32)] + [jax.ShapeDtypeStruct((S, F), BF16)] * 3 \
        + ([jax.ShapeDtypeStruct((S, D), F32)] if final else [])
    args = [x, g, *wf] + ([gfin] if final else [])
    return _call(body, name="ffn_fwd_final" if final else "ffn_fwd", grid=(S // tm, nf),
                 in_specs=in_specs, out_specs=out_specs, out_shape=out_shape, args=args,
                 scratch=[pltpu.VMEM((tm, D), BF16), pltpu.VMEM((tm, D), F32)], rider=rider)


def _ffn_bwd_act(dh, x, g, a, b, wf, tm, tf, rider=None):
    S, D = x.shape
    F = wf[0].shape[0]
    nf = F // tf

    chunks = [(c, min(FFN_CHUNK, tf - c)) for c in range(0, tf, FFN_CHUNK)]

    def body(dh_ref, x_ref, g_ref, a_ref, b_ref, w1_ref, w3_ref, w2_ref,
             dx_ref, dg_ref, da_ref, db_ref, n_ref, dy_ref, acc):
        i, j = pl.program_id(0), pl.program_id(1)

        @pl.when(j == 0)
        def _():
            xv = x_ref[...]
            n_ref[...] = (xv * _rms(xv) * g_ref[...]).astype(BF16)
            dy_ref[...] = (0.5 * dh_ref[...]).astype(BF16)
            acc[...] = jnp.zeros_like(acc)

            @pl.when(i == 0)
            def _():
                dg_ref[...] = jnp.zeros_like(dg_ref)

        dyv = dy_ref[...]
        for c0, cw in chunks:
            cols = slice(c0, c0 + cw)
            av = a_ref[:, cols].astype(F32)
            bv = b_ref[:, cols].astype(F32)
            sg = _sigmoid(av)
            dt = _dot_nt(dyv, w2_ref[cols, :])
            db_ref[:, cols] = (dt * (av * sg)).astype(BF16)
            da_ref[:, cols] = (dt * bv * (sg * (1.0 + av * (1.0 - sg)))).astype(BF16)
        acc[...] += _dot_nn(da_ref[...], w1_ref[...]) + _dot_nn(db_ref[...], w3_ref[...])

        @pl.when(j == nf - 1)
        def _():
            xv = x_ref[...]
            dx, dgrow = _rms_bwd(acc[...], xv, _rms(xv), g_ref[...])
            dx_ref[...] = dh_ref[...] + dx
            dg_ref[...] += _colsum(dgrow)

    def wspec(k):
        return pl.BlockSpec((tf, D), lambda i, j: (j, 0))

    row = pl.BlockSpec((tm, D), lambda i, j: (i, 0))
    vec = pl.BlockSpec((1, D), lambda i, j: (0, 0))
    act = pl.BlockSpec((tm, tf), lambda i, j: (i, j))
    sd = lambda shp, dt: jax.ShapeDtypeStruct(shp, dt)
    return _call(body, name="ffn_bwd_act", grid=(S // tm, nf),
                 in_specs=[row, row, vec, act, act, wspec(0), wspec(1), wspec(2)],
                 out_specs=[row, vec, act, act, row, row],
                 out_shape=[sd((S, D), F32), sd((1, D), F32), sd((S, F), BF16), sd((S, F), BF16),
                            sd((S, D), BF16), sd((S, D), BF16)],
                 args=[dh, x, g, a, b, *wf], scratch=[pltpu.VMEM((tm, D), F32)], rider=rider)


def _ffn_bwd_w(da, db, t, n, dy, tm, tf):
    S, F = da.shape
    D = n.shape[1]
    nt = S // tm

    def body(da_ref, db_ref, t_ref, n_ref, dy_ref, out_ref, acc):
        i = pl.program_id(1)

        @pl.when(i == 0)
        def _():
            acc[...] = jnp.zeros_like(acc)

        nv = n_ref[...]
        acc[0] += _dot_tn(da_ref[...], nv)
        acc[1] += _dot_tn(db_ref[...], nv)
        acc[2] += _dot_tn(t_ref[...], dy_ref[...])

        @pl.when(i == nt - 1)
        def _():
            out_ref[...] = acc[...].astype(BF16)

    act = pl.BlockSpec((tm, tf), lambda j, i: (i, j))
    row = pl.BlockSpec((tm, D), lambda j, i: (i, 0))
    return pl.pallas_call(
        body, name="ffn_bwd_w", grid=(F // tf, nt),
        in_specs=[act, act, act, row, row],
        out_specs=pl.BlockSpec((3, tf, D), lambda j, i: (0, j, 0)),
        out_shape=jax.ShapeDtypeStruct((3, F, D), BF16),
        scratch_shapes=[pltpu.VMEM((3, tf, D), F32)],
        compiler_params=_cp(),
    )(da, db, t, n, dy)


def _proj_fwd(h, g, w_inp, tm):
    S, D = h.shape

    def body(h_ref, g_ref, w_ref, pu_ref, ph_ref, n_ref):
        hv = h_ref[...]
        n = (hv * _rms(hv) * g_ref[...]).astype(BF16)
        n_ref[...] = n
        pu_ref[...] = _dot_nt(n, w_ref[0:PU, :])
        ph_ref[...] = _dot_nt(n, w_ref[PU:PP, :])

    cur = lambda w: pl.BlockSpec((tm, w), lambda i: (i, 0))
    return pl.pallas_call(
        body, name="proj_fwd", grid=(S // tm,),
        in_specs=[cur(D), pl.BlockSpec((1, D), lambda i: (0, 0)), pl.BlockSpec((PP, D), lambda i: (0, 0))],
        out_specs=[cur(PU), cur(PH), cur(D)],
        out_shape=[jax.ShapeDtypeStruct((S, PU), F32), jax.ShapeDtypeStruct((S, PH), F32),
                   jax.ShapeDtypeStruct((S, D), BF16)],
        compiler_params=_cp(),
    )(h, g, w_inp)


def _glu(u):
    return u[:, :CONV_CH] * _sigmoid(u[:, CONV_CH:2 * CONV_CH])


def _shifted_copies(ext8):
    n = ext8.shape[1]
    for b in range(1, 8):
        ext8[b, 0:n - 8, :] = ext8[0, b:b + n - 8, :]


def _window(ext8, off, rows, r0=0):
    return ext8[off % 8, pl.ds(r0 + (off - off % 8), rows), :]


def _layer_norm_stats(yc):
    mu = jnp.mean(yc, axis=-1, keepdims=True)
    d = yc - mu
    rstd = lax.rsqrt(jnp.mean(d * d, axis=-1, keepdims=True) + EPS)
    return d * rstd, rstd


def _mem_kv_fwd(mem, g, w_mkvp, gk):
    M, D = mem.shape
    W = SLOT * N_MEMH

    def body(mem_ref, g_ref, w_ref, gk_ref, nm_ref, raw_ref, mk_ref, mv_ref):
        mv_ = mem_ref[...]
        nm = (mv_ * _rms(mv_) * g_ref[...]).astype(BF16)
        nm_ref[...] = nm
        raw = _dot_nn(nm, w_ref[...])
        raw_ref[...] = raw
        for hh in range(N_MEMH):
            sl = slice(SLOT * hh, SLOT * (hh + 1))
            mk_ref[:, sl] = _slot_norm(raw[:, sl], gk_ref[...]).astype(BF16)
        mv_ref[...] = raw[:, W:].astype(BF16)

    sd = jax.ShapeDtypeStruct
    return pl.pallas_call(
        body, name="mem_kv_fwd",
        out_shape=[sd((M, D), BF16), sd((M, 2 * W), F32), sd((M, W), BF16), sd((M, W), BF16)],
        compiler_params=_cp(),
    )(mem, g, w_mkvp, gk)


def _mem_kv_bwd(dmk, dmv, raw, nm, mem, g, w_mkvp, gk):
    M, D = mem.shape
    W = SLOT * N_MEMH

    def body(dmk_ref, dmv_ref, raw_ref, nm_ref, mem_ref, g_ref, w_ref, gk_ref, dw_ref, dg_ref, dgk_ref, draw):
        dgk = jnp.zeros((1, SLOT), F32)
        for hh in range(N_MEMH):
            sl = slice(SLOT * hh, SLOT * (hh + 1))
            dx, prod = _slot_norm_bwd(dmk_ref[:, sl], raw_ref[:, sl], gk_ref[...])
            draw[:, sl] = dx.astype(BF16)
            dgk = dgk + _colsum(prod)
        dgk_ref[...] = dgk
        draw[:, W:] = dmv_ref[...].astype(BF16)
        dr = draw[...]
        dw_ref[...] = _dot_tn(nm_ref[...], dr)
        dnm = _dot_nt(dr, w_ref[...])
        mv_ = mem_ref[...]
        dg_ref[...] = _colsum(dnm * (mv_ * _rms(mv_)))

    sd = jax.ShapeDtypeStruct
    return pl.pallas_call(
        body, name="mem_kv_bwd",
        out_shape=[sd((D, 2 * W), F32), sd((1, D), F32), sd((1, SLOT), F32)],
        scratch_shapes=[pltpu.VMEM((M, 2 * W), BF16)],
        compiler_params=_cp(),
    )(dmk, dmv, raw, nm, mem, g, w_mkvp, gk)


def _mixer_fwd(pu, ph, h, cosT, sinT, conv_w, conv_b, ln_g, ln_b, gq, gk, sinks, mk, mv, gqm, w_outp, tm, rider=None):
    S, D = h.shape
    M = mk.shape[0]
    nb = tm // BLK
    nblocks = S // BLK

    def body(pu_ref, pup_ref, p_ref, ph_ref, h_ref, cos_ref, cosh_ref, sin_ref, sinh_ref, cw_ref, cb_ref,
             lg_ref, lb_ref, gq_ref, gk_ref, sink_ref, mk_ref, mv_ref, gqm_ref, wo_ref,
             h2_ref, y_ref, yc_ref, lse_ref, ext, y_scr):
        i = pl.program_id(0)
        not_first = (i > 0).astype(F32)
        lane = _lane(tm)
        lane_e = _lane(tm + BLK)

        ext[0, 0:HALO, :] = _glu(pup_ref[...]) * not_first
        ext[0, HALO:HALO + tm, :] = _glu(pu_ref[...])
        _shifted_copies(ext)

        def rows_chunk(r, carry):
            r0 = pl.multiple_of(r * CONV_ROWS, CONV_ROWS)
            yc = jnp.zeros((CONV_ROWS, CONV_CH), F32) + cb_ref[...]
            for k in range(CONV_WIDTH):
                yc = yc + cw_ref[k:k + 1, :] * _window(ext, HALO - (CONV_WIDTH - 1) + k, CONV_ROWS, r0)
            yc_ref[pl.ds(r0, CONV_ROWS), :] = yc
            z, _ = _layer_norm_stats(yc)
            ln = z * lg_ref[...] + lb_ref[...]
            y_scr[pl.ds(r0, CONV_ROWS), 0:CONV_CH] = (ln * _sigmoid(ln)).astype(BF16)
            return carry

        lax.fori_loop(0, tm // CONV_ROWS, rows_chunk, 0)

        cos_e = jnp.concatenate([cosh_ref[...], cos_ref[...]], axis=0)
        sin_e = jnp.concatenate([sinh_ref[...], sin_ref[...]], axis=0)
        qi = lax.broadcasted_iota(jnp.int32, (GROUP * BLK, 2 * BLK), 0) & (BLK - 1)
        kj = lax.broadcasted_iota(jnp.int32, (GROUP * BLK, 2 * BLK), 1)
        band = (kj > qi) & (kj <= qi + BLK)
        band0 = band & ((kj >= BLK) | (i > 0))
        lse = jnp.zeros((tm, SLOT), F32)
        k_pair = jnp.concatenate([ph_ref[:, KO:KO + SLOT], p_ref[:, KO:KO + SLOT]], axis=0)
        k_pair = _pair_fwd(k_pair, gk_ref[...], cos_e, sin_e, lane_e)
        v_pair = jnp.concatenate([ph_ref[:, VO:VO + SLOT], p_ref[:, VO:VO + SLOT]], axis=0)
        k_e = [_lo(k_pair, kvh, lane_e).astype(BF16) for kvh in range(N_KV)]
        v_e = [_lo(v_pair, kvh, lane_e).astype(BF16) for kvh in range(N_KV)]
        q_lo = []
        for j in range(N_Q // 2):
            q_pair = _pair_fwd(p_ref[:, QO + SLOT * j:QO + SLOT * (j + 1)], gq_ref[...],
                               cos_ref[...], sin_ref[...], lane)
            q_lo += [_lo(q_pair, 0, lane).astype(BF16), _lo(q_pair, 1, lane).astype(BF16)]
        outs = [[] for _ in range(N_Q)]
        lses = [[] for _ in range(N_Q)]
        for kvh in range(N_KV):
            hs = [GROUP * kvh + gi for gi in range(GROUP)]
            sink3 = jnp.concatenate([jnp.full((BLK, 1), sink_ref[h], F32) for h in hs], axis=0)
            for m in range(nb):
                rows = slice(BLK * m, BLK * (m + 1))
                win = slice(BLK * m, BLK * (m + 2))
                q3 = jnp.concatenate([q_lo[h][rows] for h in hs], axis=0)
                s = _dot_nt(q3, k_e[kvh][win]) * SCALE
                s = jnp.where(band0 if m == 0 else band, s, NEG)
                mx = jnp.maximum(jnp.max(s, axis=-1, keepdims=True), sink3)
                e = jnp.exp(s - mx)
                den = jnp.sum(e, axis=-1, keepdims=True) + jnp.exp(sink3 - mx)
                o3 = _dot_nn((e / den).astype(BF16), v_e[kvh][win])
                l3 = mx + jnp.log(den)
                for gi, h in enumerate(hs):
                    outs[h].append(o3[BLK * gi:BLK * (gi + 1)])
                    lses[h].append(l3[BLK * gi:BLK * (gi + 1)])
        for h in range(N_Q):
            lse = jnp.where(lane == h, jnp.concatenate(lses[h], axis=0), lse)
        for j in range(N_Q // 2):
            y_scr[:, YS + SLOT * j:YS + SLOT * (j + 1)] = _pack(
                jnp.concatenate(outs[2 * j], axis=0), jnp.concatenate(outs[2 * j + 1], axis=0), lane).astype(BF16)

        heads = []
        for hm in range(N_MEMH):
            ms = slice(SLOT * hm, SLOT * (hm + 1))
            if hm % 2 == 0:
                qm_pair = _pair_fwd(p_ref[:, MO + SLOT * (hm // 2):MO + SLOT * (hm // 2 + 1)], gqm_ref[...],
                                    None, None, lane)
            s = _dot_nt(_lo(qm_pair, hm % 2, lane).astype(BF16), mk_ref[:, ms]) * SCALE
            mx = jnp.max(s, axis=-1, keepdims=True)
            e = jnp.exp(s - mx)
            den = jnp.sum(e, axis=-1, keepdims=True)
            heads.append(_dot_nn((e / den).astype(BF16), mv_ref[:, ms]))
            lse = jnp.where(lane == N_Q + hm, mx + jnp.log(den), lse)
            if hm % 2 == 1:
                y_scr[:, YM + SLOT * (hm // 2):YM + SLOT * (hm // 2 + 1)] = _pack(heads[-2], heads[-1], lane).astype(BF16)
        lse_ref[...] = lse.T[0:STAT_ROWS, :]

        yv = y_scr[...]
        y_ref[...] = yv
        h2_ref[...] = h_ref[...] + _dot_nn(yv, wo_ref[...])

    cur = lambda w: pl.BlockSpec((tm, w), lambda i: (i, 0))
    prev = lambda w: pl.BlockSpec((BLK, w), lambda i: (jnp.maximum(i * nb - 1, 0), 0))
    full = lambda a: pl.BlockSpec(a.shape, lambda i: (0,) * a.ndim)
    sd = jax.ShapeDtypeStruct
    prev32 = pl.BlockSpec((HALO, PU), lambda i: (jnp.maximum(i * (tm // HALO) - 1, 0), 0))
    return _call(
        body, name="mixer_fwd", grid=(S // tm,),
        in_specs=[cur(PU), prev32, cur(PH), prev(PH), cur(D), cur(SLOT), prev(SLOT), cur(SLOT), prev(SLOT),
                  full(conv_w), full(conv_b), full(ln_g), full(ln_b), full(gq), full(gk),
                  pl.BlockSpec(memory_space=pltpu.SMEM), full(mk), full(mv), full(gqm), full(w_outp)],
        out_specs=[cur(D), cur(YP), cur(CONV_CH), pl.BlockSpec((STAT_ROWS, tm), lambda i: (0, i))],
        out_shape=[sd((S, D), F32), sd((S, YP), BF16), sd((S, CONV_CH), F32), sd((STAT_ROWS, S), F32)],
        args=[pu, pu, ph, ph, h, cosT, cosT, sinT, sinT, conv_w, conv_b, ln_g, ln_b, gq, gk, sinks, mk, mv, gqm,
              w_outp],
        scratch=[pltpu.VMEM((8, tm + HALO, CONV_CH), F32), pltpu.VMEM((tm, YP), BF16)], rider=rider)


def _outproj_bwd(dh2, y, yc, ln_g, ln_b, w_outp, tm):
    S, D = dh2.shape

    def body(dh_ref, y_ref, yc_ref, lg_ref, lb_ref, wo_ref, dyc_ref, do_ref, del_ref, dwo_ref, dlg_ref, dlb_ref):
        i = pl.program_id(0)

        @pl.when(i == 0)
        def _():
            dwo_ref[...] = jnp.zeros_like(dwo_ref)
            dlg_ref[...] = jnp.zeros_like(dlg_ref)
            dlb_ref[...] = jnp.zeros_like(dlb_ref)

        dhb = dh_ref[...].astype(BF16)
        yv = y_ref[...]
        dy = _dot_nt(dhb, wo_ref[...])
        dwo_ref[...] += _dot_tn(yv, dhb)

        z, rstd = _layer_norm_stats(yc_ref[...])
        ln = z * lg_ref[...] + lb_ref[...]
        sg = _sigmoid(ln)
        dln = dy[:, 0:CONV_CH] * (sg * (1.0 + ln * (1.0 - sg)))
        dlg_ref[...] += _colsum(dln * z)
        dlb_ref[...] += _colsum(dln)
        dz = dln * lg_ref[...]
        dyc_ref[...] = rstd * (dz - jnp.mean(dz, axis=-1, keepdims=True)
                               - z * jnp.mean(dz * z, axis=-1, keepdims=True))
        do_ref[...] = dy[:, CONV_CH:].astype(BF16)

        lane = _lane(tm)
        delta = jnp.zeros((tm, SLOT), F32)
        for j in range(NH // 2):
            sl = slice(YS + SLOT * j, YS + SLOT * (j + 1))
            prod = dy[:, sl] * yv[:, sl].astype(F32)
            lo = jnp.sum(jnp.where(lane < HEAD_DIM, prod, 0.0), axis=-1, keepdims=True)
            hi = jnp.sum(jnp.where(lane < HEAD_DIM, 0.0, prod), axis=-1, keepdims=True)
            delta = jnp.where(lane == 2 * j, lo, jnp.where(lane == 2 * j + 1, hi, delta))
        del_ref[...] = delta.T[0:STAT_ROWS, :]

    cur = lambda w: pl.BlockSpec((tm, w), lambda i: (i, 0))
    full = lambda a: pl.BlockSpec(a.shape, lambda i: (0,) * a.ndim)
    sd = jax.ShapeDtypeStruct
    return pl.pallas_call(
        body, name="outproj_bwd", grid=(S // tm,),
        in_specs=[cur(D), cur(YP), cur(CONV_CH), full(ln_g), full(ln_b), full(w_outp)],
        out_specs=[cur(CONV_CH), cur(YH), pl.BlockSpec((STAT_ROWS, tm), lambda i: (0, i)),
                   pl.BlockSpec((YP, D), lambda i: (0, 0)),
                   pl.BlockSpec((1, CONV_CH), lambda i: (0, 0)), pl.BlockSpec((1, CONV_CH), lambda i: (0, 0))],
        out_shape=[sd((S, CONV_CH), F32), sd((S, YH), BF16), sd((STAT_ROWS, S), F32), sd((YP, D), F32),
                   sd((1, CONV_CH), F32), sd((1, CONV_CH), F32)],
        compiler_params=_cp(),
    )(dh2, y, yc, ln_g, ln_b, w_outp)


def _conv_bwd(pu, dyc, conv_w, tm):
    S = pu.shape[0]
    nt = S // tm
    nh = tm // HALO

    def body(pu_ref, pup_ref, dy_ref, dyn_ref, cw_ref, dpu_ref, dcw_ref, dcb_ref, ext, ext2, dcw8):
        i = pl.program_id(0)

        @pl.when(i == 0)
        def _():
            dcw8[...] = jnp.zeros_like(dcw8)
            dcb_ref[...] = jnp.zeros_like(dcb_ref)

        not_first = (i > 0).astype(F32)
        not_last = (i < nt - 1).astype(F32)
        ext[0, 0:HALO, :] = _glu(pup_ref[...]) * not_first
        ext[0, HALO:HALO + tm, :] = _glu(pu_ref[...])
        _shifted_copies(ext)
        ext2[0, 0:tm, :] = dy_ref[...]
        ext2[0, tm:tm + HALO, :] = dyn_ref[...] * not_last
        _shifted_copies(ext2)
        dcb_ref[...] += _colsum(dy_ref[...])

        def rows_chunk(r, carry):
            r0 = pl.multiple_of(r * CONV_ROWS, CONV_ROWS)
            dyc_ = dy_ref[pl.ds(r0, CONV_ROWS), :]
            dyg = jnp.zeros((CONV_ROWS, CONV_CH), F32)
            for k in range(CONV_WIDTH):
                prod = dyc_ * _window(ext, HALO - (CONV_WIDTH - 1) + k, CONV_ROWS, r0)
                dcw8[k] += jnp.sum(prod.reshape(CONV_ROWS // 8, 8, CONV_CH), axis=0)
                dyg = dyg + cw_ref[k:k + 1, :] * _window(ext2, CONV_WIDTH - 1 - k, CONV_ROWS, r0)
            u = pu_ref[pl.ds(r0, CONV_ROWS), :]
            a_, sg = u[:, :CONV_CH], _sigmoid(u[:, CONV_CH:])
            dpu_ref[pl.ds(r0, CONV_ROWS), 0:CONV_CH] = (dyg * sg).astype(BF16)
            dpu_ref[pl.ds(r0, CONV_ROWS), CONV_CH:PU] = (dyg * a_ * sg * (1.0 - sg)).astype(BF16)
            return carry

        lax.fori_loop(0, tm // CONV_ROWS, rows_chunk, 0)

        @pl.when(i == nt - 1)
        def _():
            dcw_ref[...] = jnp.sum(dcw8[...], axis=1)

    cur = lambda w: pl.BlockSpec((tm, w), lambda i: (i, 0))
    prev = lambda w: pl.BlockSpec((HALO, w), lambda i: (jnp.maximum(i * nh - 1, 0), 0))
    nxt = lambda w: pl.BlockSpec((HALO, w), lambda i: (jnp.minimum((i + 1) * nh, S // HALO - 1), 0))
    acc = lambda r, w: pl.BlockSpec((r, w), lambda i: (0, 0))
    sd = jax.ShapeDtypeStruct
    return pl.pallas_call(
        body, name="conv_bwd", grid=(nt,),
        in_specs=[cur(PU), prev(PU), cur(CONV_CH), nxt(CONV_CH), acc(32, CONV_CH)],
        out_specs=[cur(PU), acc(32, CONV_CH), acc(1, CONV_CH)],
        out_shape=[sd((S, PU), BF16), sd((32, CONV_CH), F32), sd((1, CONV_CH), F32)],
        scratch_shapes=[pltpu.VMEM((8, tm + HALO, CONV_CH), F32), pltpu.VMEM((8, tm + HALO, CONV_CH), F32),
                        pltpu.VMEM((32, 8, CONV_CH), F32)],
        compiler_params=_cp(),
    )(pu, pu, dyc, dyc, conv_w)


def _attn_bwd(p, do, lse, delta, cosT, sinT, gq, gk, sinks, mk, mv, gqm, tm, rider=None):
    S = p.shape[0]
    M = mk.shape[0]
    nb = tm // BLK
    nt = S // tm
    nblocks = S // BLK
    W = SLOT * N_MEMH

    def body(p_ref, pp_ref, pn_ref, dy_ref, dyn_ref, lse_ref, lsen_ref, del_ref, deln_ref,
             cos_ref, cosp_ref, cosn_ref, sin_ref, sinp_ref, sinn_ref,
             gq_ref, gk_ref, sink_ref, mk_ref, mv_ref, gqm_ref,
             dp_ref, dgq_ref, dgk_ref, dgqm_ref, dsink_ref, dmk_ref, dmv_ref):
        i = pl.program_id(0)

        @pl.when(i == 0)
        def _():
            for r in (dgq_ref, dgk_ref, dgqm_ref, dsink_ref, dmk_ref, dmv_ref):
                r[...] = jnp.zeros_like(r)

        lane = _lane(tm)
        lane_e = _lane(tm + BLK)

        cos_k = jnp.concatenate([cosp_ref[...], cos_ref[...]], axis=0)
        sin_k = jnp.concatenate([sinp_ref[...], sin_ref[...]], axis=0)
        cos_q = jnp.concatenate([cos_ref[...], cosn_ref[...]], axis=0)
        sin_q = jnp.concatenate([sin_ref[...], sinn_ref[...]], axis=0)
        lse_e = jnp.concatenate([lse_ref[...], lsen_ref[...]], axis=1)
        del_e = jnp.concatenate([del_ref[...], deln_ref[...]], axis=1)
        kj = lax.broadcasted_iota(jnp.int32, (BLK, GROUP * BLK), 0)
        qi = lax.broadcasted_iota(jnp.int32, (BLK, GROUP * BLK), 1) & (BLK - 1)
        diag = kj <= qi
        offd = kj > qi
        dgq = jnp.zeros((1, SLOT), F32)
        dgk = jnp.zeros((1, SLOT), F32)
        dsink = jnp.zeros((1, SLOT), F32)
        lane1 = lax.broadcasted_iota(jnp.int32, (1, SLOT), 1)
        k_pair = jnp.concatenate([pp_ref[:, KO:KO + SLOT], p_ref[:, KO:KO + SLOT]], axis=0)
        k_pair = _pair_fwd(k_pair, gk_ref[...], cos_k, sin_k, lane_e)
        v_pair = jnp.concatenate([pp_ref[:, VO:VO + SLOT], p_ref[:, VO:VO + SLOT]], axis=0)
        k_e = [_lo(k_pair, kvh, lane_e).astype(BF16) for kvh in range(N_KV)]
        v_e = [_lo(v_pair, kvh, lane_e).astype(BF16) for kvh in range(N_KV)]
        dk = [[jnp.zeros((BLK, SLOT), F32) for _ in range(nb)] for _ in range(N_KV)]
        dv = [[jnp.zeros((BLK, SLOT), F32) for _ in range(nb)] for _ in range(N_KV)]
        q_e, do_e = [], []
        for j in range(N_Q // 2):
            js = slice(SLOT * j, SLOT * (j + 1))
            q_pair = _pair_fwd(jnp.concatenate([p_ref[:, js], pn_ref[:, js]], axis=0), gq_ref[...],
                               cos_q, sin_q, lane_e)
            do_pair = jnp.concatenate([dy_ref[:, js], dyn_ref[:, js]], axis=0).astype(F32)
            for half in range(2):
                q_e.append(_lo(q_pair, half, lane_e).astype(BF16))
                do_e.append(_lo(do_pair, half, lane_e).astype(BF16))
        dq_heads = [None] * N_Q
        for kvh in range(N_KV):
            hs = [GROUP * kvh + gi for gi in range(GROUP)]
            dq3 = [None] * nb
            for m in range(nb + 1):
                rows = slice(BLK * m, BLK * (m + 1))
                q3 = jnp.concatenate([q_e[h][rows] for h in hs], axis=0)
                do3 = jnp.concatenate([do_e[h][rows] for h in hs], axis=0)
                lb3 = jnp.concatenate([lse_e[h:h + 1, rows] for h in hs], axis=1)
                db3 = jnp.concatenate([del_e[h:h + 1, rows] for h in hs], axis=1)
                for n in (m - 1, m):
                    if n == nb:
                        continue
                    krows = slice(BLK * (n + 1), BLK * (n + 2))
                    kb, vb = k_e[kvh][krows], v_e[kvh][krows]
                    s = _dot_nt(kb, q3) * SCALE
                    mask = diag if n == m else offd
                    if n == -1:
                        mask = mask & (i > 0)
                    if m == nb:
                        mask = mask & (i < nt - 1)
                    prob = jnp.where(mask, jnp.exp(jnp.where(mask, s - lb3, NEG)), 0.0)
                    dpb = _dot_nt(vb, do3)
                    ds = (prob * (dpb - db3) * SCALE).astype(BF16)
                    if m < nb:
                        dqc = _dot_tn(ds, kb)
                        dq3[m] = dqc if dq3[m] is None else dq3[m] + dqc
                    if n >= 0:
                        dk[kvh][n] = dk[kvh][n] + _dot_nn(ds, q3)
                        dv[kvh][n] = dv[kvh][n] + _dot_nn(prob.astype(BF16), do3)
            for gi, h in enumerate(hs):
                dq_heads[h] = jnp.concatenate([dq3[m][BLK * gi:BLK * (gi + 1)] for m in range(nb)], axis=0)
                psink = jnp.exp(sink_ref[h] - lse_e[h:h + 1, 0:tm])
                dsink = dsink + jnp.where(
                    lane1 == h, -jnp.sum(psink * del_e[h:h + 1, 0:tm], axis=-1, keepdims=True), 0.0)
        for j in range(N_Q // 2):
            js = slice(SLOT * j, SLOT * (j + 1))
            dqr, prod = _pair_bwd(_pack(dq_heads[2 * j], dq_heads[2 * j + 1], lane), p_ref[:, js], gq_ref[...],
                                  cos_ref[...], sin_ref[...], lane)
            dp_ref[:, js] = dqr.astype(BF16)
            dgq = dgq + _colsum(prod)
        dk_pair = _pack(jnp.concatenate(dk[0], axis=0), jnp.concatenate(dk[1], axis=0), lane)
        dkr, prod = _pair_bwd(dk_pair, p_ref[:, KO:KO + SLOT], gk_ref[...], cos_ref[...], sin_ref[...], lane)
        dp_ref[:, KO:KO + SLOT] = dkr.astype(BF16)
        dp_ref[:, VO:VO + SLOT] = _pack(jnp.concatenate(dv[0], axis=0), jnp.concatenate(dv[1], axis=0),
                                        lane).astype(BF16)
        dgq_ref[...] += dgq
        dgk_ref[...] += _colsum(prod)
        dsink_ref[...] += dsink

        dgqm = jnp.zeros((1, SLOT), F32)
        dq_heads = []
        for hm in range(N_MEMH):
            ms = slice(SLOT * hm, SLOT * (hm + 1))
            js = slice(MO + SLOT * (hm // 2), MO + SLOT * (hm // 2 + 1))
            os_ = slice(SLOT * ((N_Q + hm) // 2), SLOT * ((N_Q + hm) // 2 + 1))
            if hm % 2 == 0:
                qm_pair = _pair_fwd(p_ref[:, js], gqm_ref[...], None, None, lane)
                do_pair = dy_ref[:, os_].astype(F32)
            qm = _lo(qm_pair, hm % 2, lane).astype(BF16)
            dob = _lo(do_pair, hm % 2, lane).astype(BF16)
            kb, vb = mk_ref[:, ms], mv_ref[:, ms]
            s = _dot_nt(kb, qm) * SCALE
            prob = jnp.exp(s - lse_ref[N_Q + hm:N_Q + hm + 1, :])
            dpb = _dot_nt(vb, dob)
            ds = (prob * (dpb - del_ref[N_Q + hm:N_Q + hm + 1, :]) * SCALE).astype(BF16)
            dq_heads.append(_dot_tn(ds, kb))
            dmk_ref[:, ms] += _dot_nn(ds, qm)
            dmv_ref[:, ms] += _dot_nn(prob.astype(BF16), dob)
            if hm % 2 == 1:
                dqr, prod = _pair_bwd(_pack(dq_heads[-2], dq_heads[-1], lane), p_ref[:, js], gqm_ref[...],
                                      None, None, lane)
                dp_ref[:, js] = dqr.astype(BF16)
                dgqm = dgqm + _colsum(prod)
        dgqm_ref[...] += dgqm

    cur = lambda w: pl.BlockSpec((tm, w), lambda i: (i, 0))
    prev = lambda w: pl.BlockSpec((BLK, w), lambda i: (jnp.maximum(i * nb - 1, 0), 0))
    nxt = lambda w: pl.BlockSpec((BLK, w), lambda i: (jnp.minimum((i + 1) * nb, nblocks - 1), 0))
    full = lambda a: pl.BlockSpec(a.shape, lambda i: (0,) * a.ndim)
    acc = lambda r, w: pl.BlockSpec((r, w), lambda i: (0, 0))
    sd = jax.ShapeDtypeStruct
    stat = pl.BlockSpec((STAT_ROWS, tm), lambda i: (0, i))
    stat_n = pl.BlockSpec((STAT_ROWS, BLK), lambda i: (0, jnp.minimum((i + 1) * nb, nblocks - 1)))
    return _call(
        body, name="attn_bwd", grid=(nt,),
        in_specs=[cur(PH), prev(PH), nxt(PH), cur(YH), nxt(YH), stat, stat_n, stat, stat_n,
                  cur(SLOT), prev(SLOT), nxt(SLOT), cur(SLOT), prev(SLOT), nxt(SLOT),
                  full(gq), full(gk), pl.BlockSpec(memory_space=pltpu.SMEM), full(mk), full(mv), full(gqm)],
        out_specs=[cur(PH), acc(1, SLOT), acc(1, SLOT), acc(1, SLOT), acc(1, SLOT), acc(M, W), acc(M, W)],
        out_shape=[sd((S, PH), BF16), sd((1, SLOT), F32), sd((1, SLOT), F32), sd((1, SLOT), F32),
                   sd((1, SLOT), F32), sd((M, W), F32), sd((M, W), F32)],
        args=[p, p, p, do, do, lse, lse, delta, delta, cosT, cosT, cosT, sinT, sinT, sinT,
              gq, gk, sinks, mk, mv, gqm],
        rider=rider)


def _proj_bwd(dpu, dph, h, dh2, g, n, w_inp, tm):
    S, D = h.shape

    def body(dpu_ref, dph_ref, h_ref, dh2_ref, g_ref, n_ref, w_ref, dh_ref, dg_ref, dw_ref):
        i = pl.program_id(0)

        @pl.when(i == 0)
        def _():
            dg_ref[...] = jnp.zeros_like(dg_ref)
            dw_ref[...] = jnp.zeros_like(dw_ref)

        dpu, dph, nv = dpu_ref[...], dph_ref[...], n_ref[...]
        dn = _dot_nn(dpu, w_ref[0:PU, :]) + _dot_nn(dph, w_ref[PU:PP, :])
        dw_ref[0:PU, :] += _dot_tn(dpu, nv)
        dw_ref[PU:PP, :] += _dot_tn(dph, nv)
        hv = h_ref[...]
        dx, dgrow = _rms_bwd(dn, hv, _rms(hv), g_ref[...])
        dh_ref[...] = dh2_ref[...] + dx
        dg_ref[...] += _colsum(dgrow)

    cur = lambda w: pl.BlockSpec((tm, w), lambda i: (i, 0))
    sd = jax.ShapeDtypeStruct
    return pl.pallas_call(
        body, name="proj_bwd", grid=(S // tm,),
        in_specs=[cur(PU), cur(PH), cur(D), cur(D), pl.BlockSpec((1, D), lambda i: (0, 0)), cur(D),
                  pl.BlockSpec((PP, D), lambda i: (0, 0))],
        out_specs=[cur(D), pl.BlockSpec((1, D), lambda i: (0, 0)), pl.BlockSpec((PP, D), lambda i: (0, 0))],
        out_shape=[sd((S, D), F32), sd((1, D), F32), sd((PP, D), F32)],
        compiler_params=_cp(),
    )(dpu, dph, h, dh2, g, n, w_inp)


def _norm_bwd(dxn, h, g, tm):
    S, D = h.shape

    def body(d_ref, h_ref, g_ref, dh_ref, dg_ref):
        @pl.when(pl.program_id(0) == 0)
        def _():
            dg_ref[...] = jnp.zeros_like(dg_ref)

        hv = h_ref[...]
        dx, dgrow = _rms_bwd(d_ref[...], hv, _rms(hv), g_ref[...])
        dh_ref[...] = dx
        dg_ref[...] += _colsum(dgrow)

    cur = pl.BlockSpec((tm, D), lambda i: (i, 0))
    vec = pl.BlockSpec((1, D), lambda i: (0, 0))
    return pl.pallas_call(
        body, name="norm_bwd", grid=(S // tm,), in_specs=[cur, cur, vec], out_specs=[cur, vec],
        out_shape=[jax.ShapeDtypeStruct((S, D), F32), jax.ShapeDtypeStruct((1, D), F32)],
        compiler_params=_cp(),
    )(dxn, h, g)


def _loss_bwd(xn, h, g, target, tm):
    S, D = h.shape

    def body(y_ref, h_ref, g_ref, t_ref, loss_ref, dh_ref, dg_ref):
        @pl.when(pl.program_id(0) == 0)
        def _():
            dg_ref[...] = jnp.zeros_like(dg_ref)
            loss_ref[...] = jnp.zeros_like(loss_ref)

        err = y_ref[...] - t_ref[...]
        part = jnp.sum(jnp.mean(err * err, axis=-1, keepdims=True), axis=0, keepdims=True)
        loss_ref[...] += 0.5 * part
        hv = h_ref[...]
        dx, dgrow = _rms_bwd(err * (1.0 / D), hv, _rms(hv), g_ref[...])
        dh_ref[...] = dx
        dg_ref[...] += _colsum(dgrow)

    cur = pl.BlockSpec((tm, D), lambda i: (i, 0))
    vec = pl.BlockSpec((1, D), lambda i: (0, 0))
    return pl.pallas_call(
        body, name="loss_bwd", grid=(S // tm,), in_specs=[cur, cur, vec, cur],
        out_specs=[pl.BlockSpec((1, SLOT), lambda i: (0, 0)), cur, vec],
        out_shape=[jax.ShapeDtypeStruct((1, SLOT), F32), jax.ShapeDtypeStruct((S, D), F32),
                   jax.ShapeDtypeStruct((1, D), F32)],
        compiler_params=_cp(),
    )(xn, h, g, target)


def _swap_with_sibling(bufs):
    nbuf = len(bufs)

    def body(*refs):
        ins, outs = refs[:nbuf], refs[nbuf:2 * nbuf]
        ssem, rsem = refs[2 * nbuf:]
        x, y, c, _ = _place()
        sends = [pltpu.make_async_remote_copy(src_ref=ins[b], dst_ref=outs[b], send_sem=ssem.at[b],
                                              recv_sem=rsem.at[b], device_id=(x, y, 1 - c), device_id_type=MESH)
                 for b in range(nbuf)]
        for cp in sends:
            cp.start()
        for cp in sends:
            cp.wait_recv()
        for cp in sends:
            cp.wait_send()

    hbm = pl.BlockSpec(memory_space=pl.ANY)
    return pl.pallas_call(
        body, name="swap_with_sibling",
        in_specs=[hbm] * nbuf, out_specs=[hbm] * nbuf,
        out_shape=[jax.ShapeDtypeStruct(b.shape, b.dtype) for b in bufs],
        scratch_shapes=[pltpu.SemaphoreType.DMA((nbuf,)), pltpu.SemaphoreType.DMA((nbuf,))],
    )(*bufs)


def _all_gather_small(buf):
    _, R, W = buf.shape

    def body(in_ref, out_ref, ssem, rsem, lsem):
        x, y, c, _ = _place()
        me = 4 * x + 2 * y + c
        local = pltpu.make_async_copy(in_ref, out_ref.at[pl.ds(me, 1)], lsem)
        local.start()

        def copy(k, block):
            fx, fy, fc = (k >> 2) & 1, (k >> 1) & 1, k & 1
            peer = (x ^ fx, y ^ fy, c ^ fc)
            return pltpu.make_async_remote_copy(
                src_ref=in_ref, dst_ref=out_ref.at[pl.ds(block, 1)], send_sem=ssem.at[k - 1],
                recv_sem=rsem.at[k - 1], device_id=peer, device_id_type=MESH)

        sends = [copy(k, me) for k in range(1, 8)]
        for cp in sends:
            cp.start()
        for k in range(1, 8):
            copy(k, me ^ k).wait_recv()
        for cp in sends:
            cp.wait_send()
        local.wait()

    hbm = pl.BlockSpec(memory_space=pl.ANY)
    return pl.pallas_call(
        body, name="all_gather_small", in_specs=[hbm], out_specs=hbm,
        out_shape=jax.ShapeDtypeStruct((8, R, W), buf.dtype),
        scratch_shapes=[pltpu.SemaphoreType.DMA((7,)), pltpu.SemaphoreType.DMA((7,)), pltpu.SemaphoreType.DMA],
    )(buf)


def _row_tile(n, cap=1024):
    for t in range(min(n, cap) // 8 * 8, 7, -8):
        if n % t == 0:
            return t
    return n


def _sum4(own, recv):
    n, rows, D = own.shape
    tr = _row_tile(rows)

    def body(o_ref, r0_ref, r1_ref, r2_ref, out_ref):
        out_ref[...] = ((o_ref[...].astype(F32) + r0_ref[...].astype(F32)) + r1_ref[...].astype(F32)) \
            + r2_ref[...].astype(F32)

    def rspec(p):
        return pl.BlockSpec((None, None, None, tr, D), lambda k, i, p=p: (p, k, 0, i, 0))

    blk = pl.BlockSpec((None, tr, D), lambda k, i: (k, i, 0))
    return pl.pallas_call(
        body, name="sum4", grid=(n, rows // tr),
        in_specs=[blk, rspec(0), rspec(1), rspec(2)], out_specs=blk,
        out_shape=jax.ShapeDtypeStruct((n, rows, D), F32),
    )(own, recv, recv, recv)


def _adam_math(w, g, m, v):
    m = ADAM_B1 * m + (1.0 - ADAM_B1) * g
    v = ADAM_B2 * v + (1.0 - ADAM_B2) * (g * g)
    m_hat = m / (1.0 - ADAM_B1 ** ADAM_STEP)
    v_hat = v / (1.0 - ADAM_B2 ** ADAM_STEP)
    delta = -ADAM_LR * (m_hat / (jnp.sqrt(v_hat) + ADAM_EPS) + ADAM_WD * w)
    return delta, m, v


def _adam_fused(w, m, v, parts, theirs, sel, row0, nrows, rider=None):
    L, R, D = w.shape
    assert R == nrows and parts[0].shape[2] == D
    t = _row_tile(nrows if row0 == 0 else _gcd(row0, nrows), 512)

    def gspec(k):
        return pl.BlockSpec((None, t, D), lambda l, c: (sel, row0 // t + jnp.where(l == k, c, 0), 0))

    def body(*refs):
        w_ref, m_ref, v_ref = refs[:3]
        p_refs, q_refs = refs[3:3 + L], refs[3 + L:3 + 2 * L]
        g_ref, d_ref, nm_ref, nv_ref, g_scr = refs[3 + 2 * L:]
        l = pl.program_id(0)
        for k in range(L):
            @pl.when(l == k)
            def _(k=k):
                g_scr[...] = p_refs[k][...] + q_refs[k][...]

        g = g_scr[...]
        g_ref[...] = g
        d, m_, v_ = _adam_math(w_ref[...], g, m_ref[...], v_ref[...])
        d_ref[...] = d
        nm_ref[...] = m_
        nv_ref[...] = v_

    blk = pl.BlockSpec((None, t, D), lambda l, c: (l, c, 0))
    return _call(body, name="adam_fused", grid=(L, R // t),
                 in_specs=[blk] * 3 + [gspec(k) for k in range(L)] * 2, out_specs=[blk] * 4,
                 out_shape=[jax.ShapeDtypeStruct((L, R, D), F32)] * 4,
                 args=[w, m, v, *parts, *theirs], scratch=[pltpu.VMEM((t, D), F32)], rider=rider)


def _gcd(a, b):
    while b:
        a, b = b, a % b
    return a


def _small_sum_adam(g8, w, m, v):
    _, R, W = g8.shape

    def body(g_ref, w_ref, m_ref, v_ref, go_ref, d_ref, nm_ref, nv_ref):
        g = g_ref[0]
        for k in range(1, 8):
            g = g + g_ref[k]
        go_ref[...] = g
        d, m_, v_ = _adam_math(w_ref[...], g, m_ref[...], v_ref[...])
        d_ref[...] = d
        nm_ref[...] = m_
        nv_ref[...] = v_

    return pl.pallas_call(body, name="small_sum_adam",
                          out_shape=[jax.ShapeDtypeStruct((R, W), F32)] * 4)(g8, w, m, v)


def _pad_vec(v):
    return jnp.pad(v, (0, SLOT - v.shape[0]))[None, :]


class _Pack:
    def __init__(self, shapes):
        self.shapes = shapes
        self.sizes = [int(functools.reduce(lambda a, b: a * b, s, 1)) for s in shapes]
        total = sum(self.sizes)
        self.rows = -(-total // (8 * SLOT)) * 8
        self.pad = self.rows * SLOT - total

    def pack(self, arrs):
        flat = jnp.concatenate([a.reshape(-1).astype(F32) for a in arrs] + [jnp.zeros((self.pad,), F32)])
        return flat.reshape(self.rows, SLOT)

    def unpack(self, buf):
        flat, out, o = buf.reshape(-1), [], 0
        for s, n in zip(self.shapes, self.sizes):
            out.append(flat[o:o + n].reshape(s))
            o += n
        return out


def kernel(x, mem, positions, ffn1_norm, ffn1_w1, ffn1_w3, ffn1_w2, mix_norm, w_in, conv_w, conv_b, conv_ln_g, conv_ln_b, swa_q_norm, swa_k_norm, swa_sinks, mem_norm, w_mem_kv, mem_q_norm, mem_k_norm, w_out, ffn2_norm, ffn2_w1, ffn2_w3, ffn2_w2, final_norm, loss_target, m_ffn1_norm, m_ffn1_w1, m_ffn1_w3, m_ffn1_w2, m_mix_norm, m_w_in, m_conv_w, m_conv_b, m_conv_ln_g, m_conv_ln_b, m_swa_q_norm, m_swa_k_norm, m_swa_sinks, m_mem_norm, m_w_mem_kv, m_mem_q_norm, m_mem_k_norm, m_w_out, m_ffn2_norm, m_ffn2_w1, m_ffn2_w3, m_ffn2_w2, m_final_norm, v_ffn1_norm, v_ffn1_w1, v_ffn1_w3, v_ffn1_w2, v_mix_norm, v_w_in, v_conv_w, v_conv_b, v_conv_ln_g, v_conv_ln_b, v_swa_q_norm, v_swa_k_norm, v_swa_sinks, v_mem_norm, v_w_mem_kv, v_mem_q_norm, v_mem_k_norm, v_w_out, v_ffn2_norm, v_ffn2_w1, v_ffn2_w3, v_ffn2_w2, v_final_norm):
    names = ['ffn1_norm', 'ffn1_w1', 'ffn1_w3', 'ffn1_w2', 'mix_norm', 'w_in', 'conv_w', 'conv_b', 'conv_ln_g',
             'conv_ln_b', 'swa_q_norm', 'swa_k_norm', 'swa_sinks', 'mem_norm', 'w_mem_kv', 'mem_q_norm',
             'mem_k_norm', 'w_out', 'ffn2_norm', 'ffn2_w1', 'ffn2_w3', 'ffn2_w2', 'final_norm']
    loc = locals()
    W = {n: loc[n] for n in names}
    M1 = {n: loc['m_' + n] for n in names}
    V1 = {n: loc['v_' + n] for n in names}

    S, D = x.shape[1], x.shape[2]
    L = ffn1_norm.shape[0]
    Fs = ffn1_w1.shape[2]
    F = 4 * Fs
    Mlen = mem.shape[1]
    cw_sh = conv_w.shape[2]
    tm = 512 if S >= 2048 else 256
    tf = 1408 if F % 1408 == 0 else 256
    tfw = 256
    tmw = 2048 if S >= 2048 else 256
    x0 = x[0]
    mem0 = mem[0]
    target = loss_target[0]
    my_chip = 2 * lax.axis_index("x") + lax.axis_index("y")

    mkv_rows = w_mem_kv.shape[1] * MEM_KV // D
    r_in, r_out = D_IN // 4, D_MIX // 4
    rm = r_in + r_out + mkv_rows

    shard = lambda w: w.astype(BF16).reshape((1, 1) + w.shape)

    groups = []
    for l in range(L):
        groups.append([shard(ffn1_w1[l].T), shard(ffn1_w3[l].T), shard(ffn1_w2[l])])
        groups.append([shard(w_in[l].T), shard(w_out[l]), shard(w_mem_kv[l].reshape(mkv_rows, D))])
        groups.append([shard(ffn2_w1[l].T), shard(ffn2_w3[l].T), shard(ffn2_w2[l])])
    gathered = [None] * len(groups)
    cw_rows = -(-(L * CONV_WIDTH) // 8) * 8
    cw_pad = jnp.pad(conv_w.reshape(L * CONV_WIDTH, cw_sh), ((0, cw_rows - L * CONV_WIDTH), (0, SLOT - cw_sh)))
    *gathered[0], cw_g = _run_rider(_Gather(groups[0] + [cw_pad.reshape(1, 1, cw_rows, SLOT)]), "all_gather_first")
    conv_wF = cw_g[0, :, :L * CONV_WIDTH, :cw_sh].reshape(4, L, CONV_WIDTH, cw_sh)
    conv_wF = jnp.moveaxis(conv_wF, 0, 2).reshape(L, CONV_WIDTH, 4 * cw_sh)
    conv_wP = jnp.pad(conv_wF, ((0, 0), (0, 32 - CONV_WIDTH), (0, 0)))

    def gather_rider(j):
        want = [k for k in [j + 1] if k < len(groups)]
        return (_Gather([b for k in want for b in groups[k]]), want) if want else (None, want)

    def keep(want, got):
        for n, k in enumerate(want):
            gathered[k] = got[3 * n:3 * n + 3]

    def ffn_weights(j):
        return tuple(g.reshape(F, D) for g in gathered[j])

    def mix_weights(l):
        g_in, g_out, g_mkv = gathered[3 * l + 1]
        w_inp = g_in.reshape(D_IN, D)
        w_outp = g_out.reshape(D_MIX, D)
        w_mkvp = jnp.pad(g_mkv.reshape(D, 2 * N_MEMH, HEAD_DIM),
                         ((0, 0), (0, 0), (0, SLOT - HEAD_DIM))).reshape(D, 2 * N_MEMH * SLOT)
        return w_inp, w_outp, w_mkvp

    inv_freq = ROPE_THETA ** (-jnp.arange(0, HEAD_DIM, 2, dtype=F32) / HEAD_DIM)
    invf = jnp.tile(inv_freq, SLOT // (HEAD_DIM // 2))[None, :]
    cosT, sinT = _rope_tables(positions.reshape(S, 1), invf, tm)

    row = lambda a, l: a[l][None, :]
    sinks_p = jnp.pad(swa_sinks, ((0, 0), (0, 8 - N_Q)))

    saved = []
    xin = x0
    xn = None
    for l in range(L):
        wf1 = ffn_weights(3 * l)
        rider, want = gather_rider(3 * l)
        (h1, a1, b1, t1), got = _ffn_fwd(xin, row(ffn1_norm, l), wf1, None, tm, tf, rider=rider)
        keep(want, got)
        w_inp, w_outp, w_mkvp = mix_weights(l)
        pu, p, n2 = _proj_fwd(h1, row(mix_norm, l), w_inp, tm)
        gk_m = _pad_vec(mem_k_norm[l])
        nm, mraw, mk, mv = _mem_kv_fwd(mem0, row(mem_norm, l), w_mkvp, gk_m)
        twice = lambda v: jnp.tile(v, 2)[None, :]
        gq, gk, gqm = twice(swa_q_norm[l]), twice(swa_k_norm[l]), twice(mem_q_norm[l])
        rider, want = gather_rider(3 * l + 1)
        (h2, y, yc, lse), got = _mixer_fwd(pu, p, h1, cosT, sinT, conv_wP[l], row(conv_b, l), row(conv_ln_g, l),
                                           row(conv_ln_b, l), gq, gk, sinks_p[l], mk, mv, gqm, w_outp, tm,
                                           rider=rider)
        keep(want, got)
        wf2 = ffn_weights(3 * l + 2)
        rider, want = gather_rider(3 * l + 2)
        (h3, a2, b2, t2, xn), got = _ffn_fwd(h2, row(ffn2_norm, l), wf2, row(final_norm, l), tm, tf, rider=rider)
        keep(want, got)
        saved.append(dict(xin=xin, h1=h1, a1=a1, b1=b1, pu=pu, p=p, n2=n2, nm=nm, mraw=mraw, mk=mk, mv=mv, gk_m=gk_m,
                          gq=gq, gk=gk, gqm=gqm, h2=h2, y=y, yc=yc, lse=lse, h3=h3, a2=a2, b2=b2, t1=t1, t2=t2,
                          wf1=wf1, wf2=wf2, w_inp=w_inp, w_outp=w_outp, w_mkvp=w_mkvp))
        xin = xn

    G = {n: [None] * L for n in names}
    ffn_bufs = [None] * (2 * L)
    mix_bufs = [None] * L
    parts, theirs = {}, {}
    pending_swap = []

    def scattered(key, buf, recv):
        own = lax.dynamic_index_in_dim(buf, my_chip, axis=1, keepdims=False)
        parts[key] = _sum4(own, recv)
        pending_swap.append(key)

    def comm_rider(scatter_buf):
        keys = list(pending_swap)
        pending_swap.clear()
        riders = ([_Scatter([scatter_buf])] if scatter_buf is not None else []) \
            + ([_Swap([parts[k] for k in keys])] if keys else [])
        if not riders:
            return None, lambda got: None
        multi = _Multi(riders)

        def store(got):
            outs = multi.split_outputs(got)
            if keys:
                for k, t in zip(keys, outs[-1]):
                    theirs[k] = t
            return outs[0][0] if scatter_buf is not None else None

        return multi, store

    dxn = None
    loss_part = None
    for l in reversed(range(L)):
        sv = saved[l]
        if l == L - 1:
            loss_part, dh3, G['final_norm'][l] = _loss_bwd(xn, sv['h3'], row(final_norm, l), target, tm)
        else:
            dh3, G['final_norm'][l] = _norm_bwd(dxn, sv['h3'], row(final_norm, l), tm)
        rider, store = comm_rider(ffn_bufs[2 * l + 2] if l < L - 1 else None)
        (dh2, G['ffn2_norm'][l], da, db, n, dy), got = _ffn_bwd_act(
            dh3, sv['h2'], row(ffn2_norm, l), sv['a2'], sv['b2'], sv['wf2'], tm, tf, rider=rider)
        recv = store(got)
        if recv is not None:
            scattered(('f', 2 * l + 2), ffn_bufs[2 * l + 2], recv)
        ffn_bufs[2 * l + 1] = _ffn_bwd_w(da, db, sv['t2'], n, dy, tmw, tfw).reshape(3, 4, Fs, D)
        dyc, do, delta, dwo, G['conv_ln_g'][l], G['conv_ln_b'][l] = _outproj_bwd(
            dh2, sv['y'], sv['yc'], row(conv_ln_g, l), row(conv_ln_b, l), sv['w_outp'], tm)
        rider, store = comm_rider(ffn_bufs[2 * l + 1])
        (dph, dgq, dgk, dgqm, dsink, dmk, dmv), got = _attn_bwd(
            sv['p'], do, sv['lse'], delta, cosT, sinT, sv['gq'], sv['gk'], sinks_p[l],
            sv['mk'], sv['mv'], sv['gqm'], tm, rider=rider)
        scattered(('f', 2 * l + 1), ffn_bufs[2 * l + 1], store(got))
        dpu, dcw, G['conv_b'][l] = _conv_bwd(sv['pu'], dyc, conv_wP[l], tm)
        dwm, G['mem_norm'][l], dgk_m = _mem_kv_bwd(dmk, dmv, sv['mraw'], sv['nm'], mem0, row(mem_norm, l),
                                                   sv['w_mkvp'], sv['gk_m'])
        dh1, G['mix_norm'][l], dwi = _proj_bwd(dpu, dph, sv['h1'], dh2, row(mix_norm, l), sv['n2'], sv['w_inp'], tm)
        dwiT = dwi.reshape(4, r_in, D)
        dwoF = dwo.reshape(4, r_out, D)
        dwmF = dwm.reshape(D, 2 * N_MEMH, SLOT)[:, :, :HEAD_DIM].reshape(4, mkv_rows, D)
        mix_bufs[l] = jnp.concatenate([dwiT, dwoF, dwmF], axis=1).astype(BF16).reshape(1, 4, rm, D)
        rider, store = comm_rider(mix_bufs[l])
        (dxl, G['ffn1_norm'][l], da, db, n, dy), got = _ffn_bwd_act(
            dh1, sv['xin'], row(ffn1_norm, l), sv['a1'], sv['b1'], sv['wf1'], tm, tf, rider=rider)
        scattered(('m', l), mix_bufs[l], store(got))
        ffn_bufs[2 * l] = _ffn_bwd_w(da, db, sv['t1'], n, dy, tmw, tfw).reshape(3, 4, Fs, D)
        dxn = dxl
        G['conv_w'][l] = dcw[:CONV_WIDTH]
        G['swa_q_norm'][l] = dgq[0, :HEAD_DIM] + dgq[0, HEAD_DIM:]
        G['swa_k_norm'][l] = dgk[0, :HEAD_DIM] + dgk[0, HEAD_DIM:]
        G['mem_q_norm'][l] = dgqm[0, :HEAD_DIM] + dgqm[0, HEAD_DIM:]
        G['mem_k_norm'][l] = dgk_m[0, :HEAD_DIM]
        G['swa_sinks'][l] = dsink[0, :N_Q]
    grad_x = dxn[None]
    loss = lax.psum(loss_part[0, 0], AXES)

    small = ['ffn1_norm', 'mix_norm', 'conv_b', 'conv_ln_g', 'conv_ln_b', 'swa_q_norm', 'swa_k_norm', 'swa_sinks',
             'mem_norm', 'mem_q_norm', 'mem_k_norm', 'ffn2_norm', 'final_norm']
    gsmall = [jnp.stack([G[n][l].reshape(-1) for l in range(L)]) for n in small]
    gcw = jnp.stack(G['conv_w'])
    cw_cols = 4 * cw_sh
    full_of = lambda a: lax.dynamic_update_slice(jnp.zeros((L, CONV_WIDTH, cw_cols), F32), a, (0, 0, my_chip * cw_sh))
    pk = _Pack([W[n].shape for n in small] + [(L, CONV_WIDTH, cw_cols)])
    g8 = _all_gather_small(pk.pack(gsmall + [gcw])[None])
    outs4 = _small_sum_adam(g8, pk.pack([W[n] for n in small] + [full_of(conv_w)]),
                            pk.pack([M1[n] for n in small] + [full_of(m_conv_w)]),
                            pk.pack([V1[n] for n in small] + [full_of(v_conv_w)]))
    un = [pk.unpack(o) for o in outs4]
    grads, deltas, new_m, new_v = {}, {}, {}, {}
    for k, n in enumerate(small):
        grads[n], deltas[n], new_m[n], new_v[n] = un[0][k], un[1][k], un[2][k], un[3][k]
    mine = lambda a: lax.dynamic_slice(a, (0, 0, my_chip * cw_sh), (L, CONV_WIDTH, cw_sh))
    grads['conv_w'], deltas['conv_w'], new_m['conv_w'], new_v['conv_w'] = [mine(u[-1]) for u in un]

    ffn1_k, ffn2_k = [('f', 2 * l) for l in range(L)], [('f', 2 * l + 1) for l in range(L)]
    mix_k = [('m', l) for l in range(L)]
    plan = [('ffn2_w1', ffn2_k, 0, 0, Fs, True), ('ffn2_w3', ffn2_k, 1, 0, Fs, True), ('ffn2_w2', ffn2_k, 2, 0, Fs, False),
            ('w_in', mix_k, 0, 0, r_in, True), ('w_out', mix_k, 0, r_in, r_out, False),
            ('w_mem_kv', mix_k, 0, r_in + r_out, mkv_rows, False),
            ('ffn1_w1', ffn1_k, 0, 0, Fs, True), ('ffn1_w3', ffn1_k, 1, 0, Fs, True), ('ffn1_w2', ffn1_k, 2, 0, Fs, False)]
    for step, (n, keys, sel, row0, nrows, held_transposed) in enumerate(plan):
        shp = W[n].shape
        if held_transposed:
            view, back = (lambda a: jnp.swapaxes(a, 1, 2)), (lambda a: jnp.swapaxes(a, 1, 2))
        elif n == 'w_mem_kv':
            view, back = (lambda a: a.reshape(L, mkv_rows, D)), (lambda a: a.reshape(shp))
        else:
            view = back = lambda a: a
        rider, store = comm_rider(ffn_bufs[0] if step == 0 else None)
        res, got = _adam_fused(view(W[n]), view(M1[n]), view(V1[n]), [parts[k] for k in keys],
                               [theirs[k] for k in keys], sel, row0, nrows, rider=rider)
        recv = store(got)
        if recv is not None:
            scattered(('f', 0), ffn_bufs[0], recv)
        grads[n], deltas[n], new_m[n], new_v[n] = [back(r) for r in res]

    return (loss, grad_x, *[grads[n] for n in names], *[deltas[n] for n in names],
            *[new_m[n] for n in names], *[new_v[n] for n in names])
```

```python
import functools

import jax
import jax.numpy as jnp
from jax import lax
from jax.experimental import pallas as pl
from jax.experimental.pallas import tpu as pltpu

F32 = jnp.float32
BF16 = jnp.bfloat16
MESH = pl.DeviceIdType.MESH
AXES = ("x", "y", "c")

EPS = 1e-6
HEAD_DIM = 64
SLOT = 128
CONV_CH = 384
CONV_WIDTH = 31
N_Q, N_KV, N_MEMH = 6, 2, 4
GROUP = N_Q // N_KV
BLK = 128
HALO = 32
CONV_ROWS = 64
FFN_CHUNK = 256
ROPE_THETA = 10000.0
SCALE = HEAD_DIM ** -0.5
NEG = -1e30

N_HEADS_IN = N_Q + 2 * N_KV + N_MEMH
PU = 2 * CONV_CH
PH = HEAD_DIM * N_HEADS_IN
PP = PU + PH
QO = 0
KO = QO + HEAD_DIM * N_Q
VO = KO + HEAD_DIM * N_KV
MO = VO + HEAD_DIM * N_KV
NH = N_Q + N_MEMH
STAT_ROWS = 16
YH = HEAD_DIM * NH
YP = CONV_CH + YH
YS = CONV_CH
YM = YS + HEAD_DIM * N_Q
D_IN = PP
D_MIX = YP
MEM_KV = 2 * HEAD_DIM * N_MEMH

ADAM_LR, ADAM_B1, ADAM_B2, ADAM_EPS, ADAM_WD, ADAM_STEP = 0.001, 0.9, 0.999, 1e-08, 0.01, 10

VMEM_LIMIT_MB = 56


def _cp(mb=VMEM_LIMIT_MB):
    return pltpu.CompilerParams(vmem_limit_bytes=mb * 1024 * 1024)


def _dot_nn(a, b):
    return lax.dot_general(a, b, (((1,), (0,)), ((), ())), preferred_element_type=F32)


def _dot_nt(a, b):
    return lax.dot_general(a, b, (((1,), (1,)), ((), ())), preferred_element_type=F32)


def _dot_tn(a, b):
    return lax.dot_general(a, b, (((0,), (0,)), ((), ())), preferred_element_type=F32)


def _sigmoid(x):
    return 1.0 / (1.0 + jnp.exp(-x))


def _rms(x):
    return lax.rsqrt(jnp.mean(x * x, axis=-1, keepdims=True) + EPS)


def _rms_bwd(dn, x, r, g):
    xhat = x * r
    dxhat = dn * g
    dx = r * (dxhat - xhat * jnp.mean(dxhat * xhat, axis=-1, keepdims=True))
    return dx, dn * xhat


def _colsum(v):
    return jnp.sum(v, axis=0, keepdims=True)


def _lane(n):
    return lax.broadcasted_iota(jnp.int32, (n, SLOT), 1)


def _slot_rms(xs):
    return lax.rsqrt(jnp.sum(xs * xs, axis=-1, keepdims=True) * (1.0 / HEAD_DIM) + EPS)


def _slot_norm(xs, g):
    return xs * _slot_rms(xs) * g


def _slot_norm_bwd(dout, xs, g):
    r = _slot_rms(xs)
    xhat = xs * r
    dxhat = dout * g
    dx = r * (dxhat - xhat * (jnp.sum(dxhat * xhat, axis=-1, keepdims=True) * (1.0 / HEAD_DIM)))
    return dx, dout * xhat


def _halves(v, lane):
    lo = jnp.sum(jnp.where(lane < HEAD_DIM, v, 0.0), axis=-1, keepdims=True)
    hi = jnp.sum(jnp.where(lane < HEAD_DIM, 0.0, v), axis=-1, keepdims=True)
    return jnp.where(lane < HEAD_DIM, lo, hi)


def _pair_rms(x, lane):
    return lax.rsqrt(_halves(x * x, lane) * (1.0 / HEAD_DIM) + EPS)


def _pair_partner(v, lane):
    return jnp.where((lane & (HEAD_DIM - 1)) < HEAD_DIM // 2,
                     pltpu.roll(v, SLOT - HEAD_DIM // 2, 1), pltpu.roll(v, HEAD_DIM // 2, 1))


def _pair_fwd(x, g2, cosv, sinv, lane):
    xn = x * _pair_rms(x, lane) * g2
    if cosv is None:
        return xn
    return xn * cosv + _pair_partner(xn, lane) * sinv


def _pair_bwd(dout, x, g2, cosv, sinv, lane):
    if cosv is not None:
        dout = dout * cosv + _pair_partner(dout * sinv, lane)
    r = _pair_rms(x, lane)
    xhat = x * r
    dxhat = dout * g2
    dx = r * (dxhat - xhat * (_halves(dxhat * xhat, lane) * (1.0 / HEAD_DIM)))
    return dx, dout * xhat


def _lo(x, half, lane):
    if half:
        x = pltpu.roll(x, HEAD_DIM, 1)
    return jnp.where(lane < HEAD_DIM, x, 0.0)


def _pack(even, odd, lane):
    return jnp.where(lane < HEAD_DIM, even, pltpu.roll(odd, HEAD_DIM, 1))


def _place():
    x, y, c = lax.axis_index("x"), lax.axis_index("y"), lax.axis_index("c")
    chips = [(1 - x, y), (x, 1 - y), (1 - x, 1 - y)]
    return x, y, c, chips


class _Gather:
    tag = "_gather"

    def __init__(self, bufs):
        self.bufs = list(bufs)
        nb = len(self.bufs)
        self.out_shape = [jax.ShapeDtypeStruct((b.shape[0], 4) + b.shape[2:], b.dtype) for b in self.bufs]
        self.sems = [pltpu.SemaphoreType.DMA((3 * nb,)), pltpu.SemaphoreType.DMA((3 * nb,)),
                     pltpu.SemaphoreType.DMA((nb,))]

    def _copies(self, ins, outs, sems):
        ssem, rsem, lsem = sems
        nb = len(self.bufs)
        x, y, c, chips = _place()
        mine = 2 * x + y

        def copy(b, p, shard):
            return pltpu.make_async_remote_copy(
                src_ref=ins[b], dst_ref=outs[b].at[:, pl.ds(shard, 1)],
                send_sem=ssem.at[3 * b + p], recv_sem=rsem.at[3 * b + p],
                device_id=(chips[p][0], chips[p][1], c), device_id_type=MESH)

        local = [pltpu.make_async_copy(ins[b], outs[b].at[:, pl.ds(mine, 1)], lsem.at[b]) for b in range(nb)]
        sends = [copy(b, p, mine) for b in range(nb) for p in range(3)]
        recvs = [copy(b, p, 2 * chips[p][0] + chips[p][1]) for b in range(nb) for p in range(3)]
        return local, sends, recvs

    def start(self, ins, outs, sems):
        local, sends, _ = self._copies(ins, outs, sems)
        for cp in local + sends:
            cp.start()

    def wait(self, ins, outs, sems):
        local, sends, recvs = self._copies(ins, outs, sems)
        for cp in recvs:
            cp.wait_recv()
        for cp in sends:
            cp.wait_send()
        for cp in local:
            cp.wait()


class _Scatter:
    tag = "_scatter"

    def __init__(self, bufs):
        self.bufs = list(bufs)
        nb = len(self.bufs)
        self.out_shape = [jax.ShapeDtypeStruct((3, b.shape[0], 1) + b.shape[2:], b.dtype) for b in self.bufs]
        self.sems = [pltpu.SemaphoreType.DMA((3 * nb,)), pltpu.SemaphoreType.DMA((3 * nb,))]

    def _copies(self, ins, outs, sems):
        ssem, rsem = sems
        x, y, c, chips = _place()

        def copy(b, p):
            shard = 2 * chips[p][0] + chips[p][1]
            return pltpu.make_async_remote_copy(
                src_ref=ins[b].at[:, pl.ds(shard, 1)], dst_ref=outs[b].at[p],
                send_sem=ssem.at[3 * b + p], recv_sem=rsem.at[3 * b + p],
                device_id=(chips[p][0], chips[p][1], c), device_id_type=MESH)

        return [copy(b, p) for b in range(len(self.bufs)) for p in range(3)]

    def start(self, ins, outs, sems):
        for cp in self._copies(ins, outs, sems):
            cp.start()

    def wait(self, ins, outs, sems):
        cps = self._copies(ins, outs, sems)
        for cp in cps:
            cp.wait_recv()
        for cp in cps:
            cp.wait_send()


class _Swap:
    tag = "_swap"

    def __init__(self, bufs):
        self.bufs = list(bufs)
        nb = len(self.bufs)
        self.out_shape = [jax.ShapeDtypeStruct(b.shape, b.dtype) for b in self.bufs]
        self.sems = [pltpu.SemaphoreType.DMA((nb,)), pltpu.SemaphoreType.DMA((nb,))]

    def _copies(self, ins, outs, sems):
        ssem, rsem = sems
        x, y, c, _ = _place()
        return [pltpu.make_async_remote_copy(src_ref=ins[b], dst_ref=outs[b], send_sem=ssem.at[b],
                                             recv_sem=rsem.at[b], device_id=(x, y, 1 - c), device_id_type=MESH)
                for b in range(len(self.bufs))]

    def start(self, ins, outs, sems):
        for cp in self._copies(ins, outs, sems):
            cp.start()

    def wait(self, ins, outs, sems):
        cps = self._copies(ins, outs, sems)
        for cp in cps:
            cp.wait_recv()
        for cp in cps:
            cp.wait_send()


class _Multi:
    def __init__(self, riders):
        self.riders = [r for r in riders if r is not None and r.bufs]
        self.tag = "".join(r.tag for r in self.riders)
        self.bufs = [b for r in self.riders for b in r.bufs]
        self.out_shape = [s for r in self.riders for s in r.out_shape]
        self.sems = [s for r in self.riders for s in r.sems]

    def _split(self, ins, outs, sems):
        ob, os_ = 0, 0
        for r in self.riders:
            nb, ns = len(r.bufs), len(r.sems)
            yield r, ins[ob:ob + nb], outs[ob:ob + nb], sems[os_:os_ + ns]
            ob, os_ = ob + nb, os_ + ns

    def start(self, ins, outs, sems):
        for r, i, o, s in self._split(ins, outs, sems):
            r.start(i, o, s)

    def wait(self, ins, outs, sems):
        for r, i, o, s in self._split(ins, outs, sems):
            r.wait(i, o, s)

    def split_outputs(self, got):
        res, ob = [], 0
        for r in self.riders:
            res.append(got[ob:ob + len(r.bufs)])
            ob += len(r.bufs)
        return res


def _run_rider(rider, name):
    nb = len(rider.bufs)

    def body(*refs):
        ins, outs, sems = refs[:nb], refs[nb:2 * nb], refs[2 * nb:]
        rider.start(ins, outs, sems)
        rider.wait(ins, outs, sems)

    hbm = pl.BlockSpec(memory_space=pl.ANY)
    return pl.pallas_call(body, name=name, in_specs=[hbm] * nb, out_specs=[hbm] * nb,
                          out_shape=rider.out_shape, scratch_shapes=rider.sems)(*rider.bufs)


def _call(body, *, name, grid, in_specs, out_specs, out_shape, args, scratch=(), rider=None):
    if rider is None:
        outs = pl.pallas_call(body, name=name, grid=grid, in_specs=list(in_specs), out_specs=list(out_specs),
                              out_shape=list(out_shape), scratch_shapes=list(scratch),
                              compiler_params=_cp())(*args)
        return list(outs), None
    n_in, n_out, n_scr, nb = len(in_specs), len(out_specs), len(scratch), len(rider.bufs)

    def wrapped(*refs):
        cuts = [n_in, nb, n_out, nb, n_scr]
        parts, o = [], 0
        for n in cuts:
            parts.append(refs[o:o + n])
            o += n
        ins, rin, outs, rout, scr = parts
        sems = refs[o:]
        ids = [pl.program_id(k) for k in range(len(grid))]
        first = functools.reduce(jnp.logical_and, [i == 0 for i in ids])
        last = functools.reduce(jnp.logical_and, [i == n - 1 for i, n in zip(ids, grid)])

        @pl.when(first)
        def _():
            rider.start(rin, rout, sems)

        body(*ins, *outs, *scr)

        @pl.when(last)
        def _():
            rider.wait(rin, rout, sems)

    hbm = pl.BlockSpec(memory_space=pl.ANY)
    res = pl.pallas_call(
        wrapped, name=name + rider.tag, grid=grid,
        in_specs=list(in_specs) + [hbm] * nb, out_specs=list(out_specs) + [hbm] * nb,
        out_shape=list(out_shape) + rider.out_shape, scratch_shapes=list(scratch) + rider.sems,
        compiler_params=_cp())(*args, *rider.bufs)
    return list(res[:n_out]), list(res[n_out:])


def _rope_tables(pos, invf, tm):
    S = pos.shape[0]

    def body(pos_ref, f_ref, cos_ref, sin_ref):
        ang = pos_ref[...].astype(F32) * f_ref[...]
        lane = _lane(tm)
        cos_ref[...] = jnp.cos(ang)
        s = jnp.sin(ang)
        sin_ref[...] = jnp.where((lane & (HEAD_DIM - 1)) < HEAD_DIM // 2, -s, s)

    return pl.pallas_call(
        body, name="rope_tables", grid=(S // tm,),
        in_specs=[pl.BlockSpec((tm, 1), lambda i: (i, 0)), pl.BlockSpec((1, SLOT), lambda i: (0, 0))],
        out_specs=[pl.BlockSpec((tm, SLOT), lambda i: (i, 0))] * 2,
        out_shape=[jax.ShapeDtypeStruct((S, SLOT), F32)] * 2,
    )(pos, invf)


def _ffn_fwd(x, g, wf, gfin, tm, tf, rider=None):
    S, D = x.shape
    F = wf[0].shape[0]
    nf = F // tf
    final = gfin is not None

    chunks = [(c, min(FFN_CHUNK, tf - c)) for c in range(0, tf, FFN_CHUNK)]

    def body(*refs):
        if final:
            x_ref, g_ref, w1_ref, w3_ref, w2_ref, gf_ref, h_ref, a_ref, b_ref, t_ref, xn_ref, n_scr, acc = refs
        else:
            x_ref, g_ref, w1_ref, w3_ref, w2_ref, h_ref, a_ref, b_ref, t_ref, n_scr, acc = refs
        j = pl.program_id(1)

        @pl.when(j == 0)
        def _():
            xv = x_ref[...]
            n_scr[...] = (xv * _rms(xv) * g_ref[...]).astype(BF16)
            acc[...] = jnp.zeros_like(acc)

        n = n_scr[...]
        for c0, cw in chunks:
            cols = slice(c0, c0 + cw)
            a = _dot_nt(n, w1_ref[cols, :])
            b = _dot_nt(n, w3_ref[cols, :])
            a_ref[:, cols] = a.astype(BF16)
            b_ref[:, cols] = b.astype(BF16)
            t_ref[:, cols] = (a * _sigmoid(a) * b).astype(BF16)
        acc[...] += _dot_nn(t_ref[...], w2_ref[...])

        @pl.when(j == nf - 1)
        def _():
            h = x_ref[...] + 0.5 * acc[...]
            h_ref[...] = h
            if final:
                xn_ref[...] = h * _rms(h) * gf_ref[...]

    def wspec(k):
        return pl.BlockSpec((tf, D), lambda i, j: (j, 0))

    row = pl.BlockSpec((tm, D), lambda i, j: (i, 0))
    vec = pl.BlockSpec((1, D), lambda i, j: (0, 0))
    act = pl.BlockSpec((tm, tf), lambda i, j: (i, j))
    in_specs = [row, vec, wspec(0), wspec(1), wspec(2)] + ([vec] if final else [])
    out_specs = [row, act, act, act] + ([row] if final else [])
    out_shape = [jax.ShapeDtypeStruct((S, D), F32)] + [jax.ShapeDtypeStruct((S, F), BF16)] * 3 \
        + ([jax.ShapeDtypeStruct((S, D), F32)] if final else [])
    args = [x, g, *wf] + ([gfin] if final else [])
    return _call(body, name="ffn_fwd_final" if final else "ffn_fwd", grid=(S // tm, nf),
                 in_specs=in_specs, out_specs=out_specs, out_shape=out_shape, args=args,
                 scratch=[pltpu.VMEM((tm, D), BF16), pltpu.VMEM((tm, D), F32)], rider=rider)


def _ffn_bwd_act(dh, x, g, a, b, wf, tm, tf, rider=None):
    S, D = x.shape
    F = wf[0].shape[0]
    nf = F // tf

    chunks = [(c, min(FFN_CHUNK, tf - c)) for c in range(0, tf, FFN_CHUNK)]

    def body(dh_ref, x_ref, g_ref, a_ref, b_ref, w1_ref, w3_ref, w2_ref,
             dx_ref, dg_ref, da_ref, db_ref, n_ref, dy_ref, acc):
        i, j = pl.program_id(0), pl.program_id(1)

        @pl.when(j == 0)
        def _():
            xv = x_ref[...]
            n_ref[...] = (xv * _rms(xv) * g_ref[...]).astype(BF16)
            dy_ref[...] = (0.5 * dh_ref[...]).astype(BF16)
            acc[...] = jnp.zeros_like(acc)

            @pl.when(i == 0)
            def _():
                dg_ref[...] = jnp.zeros_like(dg_ref)

        dyv = dy_ref[...]
        for c0, cw in chunks:
            cols = slice(c0, c0 + cw)
            av = a_ref[:, cols].astype(F32)
            bv = b_ref[:, cols].astype(F32)
            sg = _sigmoid(av)
            dt = _dot_nt(dyv, w2_ref[cols, :])
            db_ref[:, cols] = (dt * (av * sg)).astype(BF16)
            da_ref[:, cols] = (dt * bv * (sg * (1.0 + av * (1.0 - sg)))).astype(BF16)
        acc[...] += _dot_nn(da_ref[...], w1_ref[...]) + _dot_nn(db_ref[...], w3_ref[...])

        @pl.when(j == nf - 1)
        def _():
            xv = x_ref[...]
            dx, dgrow = _rms_bwd(acc[...], xv, _rms(xv), g_ref[...])
            dx_ref[...] = dh_ref[...] + dx
            dg_ref[...] += _colsum(dgrow)

    def wspec(k):
        return pl.BlockSpec((tf, D), lambda i, j: (j, 0))

    row = pl.BlockSpec((tm, D), lambda i, j: (i, 0))
    vec = pl.BlockSpec((1, D), lambda i, j: (0, 0))
    act = pl.BlockSpec((tm, tf), lambda i, j: (i, j))
    sd = lambda shp, dt: jax.ShapeDtypeStruct(shp, dt)
    return _call(body, name="ffn_bwd_act", grid=(S // tm, nf),
                 in_specs=[row, row, vec, act, act, wspec(0), wspec(1), wspec(2)],
                 out_specs=[row, vec, act, act, row, row],
                 out_shape=[sd((S, D), F32), sd((1, D), F32), sd((S, F), BF16), sd((S, F), BF16),
                            sd((S, D), BF16), sd((S, D), BF16)],
                 args=[dh, x, g, a, b, *wf], scratch=[pltpu.VMEM((tm, D), F32)], rider=rider)


def _ffn_bwd_w(da, db, t, n, dy, tm, tf, rider=None):
    S, F = da.shape
    D = n.shape[1]
    nt = S // tm

    def body(da_ref, db_ref, t_ref, n_ref, dy_ref, out_ref, acc):
        i = pl.program_id(1)

        @pl.when(i == 0)
        def _():
            acc[...] = jnp.zeros_like(acc)

        nv = n_ref[...]
        acc[0] += _dot_tn(da_ref[...], nv)
        acc[1] += _dot_tn(db_ref[...], nv)
        acc[2] += _dot_tn(t_ref[...], dy_ref[...])

        @pl.when(i == nt - 1)
        def _():
            out_ref[...] = acc[...].astype(BF16)

    act = pl.BlockSpec((tm, tf), lambda j, i: (i, j))
    row = pl.BlockSpec((tm, D), lambda j, i: (i, 0))
    outs, got = _call(body, name="ffn_bwd_w", grid=(F // tf, nt),
                      in_specs=[act, act, act, row, row],
                      out_specs=[pl.BlockSpec((3, tf, D), lambda j, i: (0, j, 0))],
                      out_shape=[jax.ShapeDtypeStruct((3, F, D), BF16)],
                      args=[da, db, t, n, dy], scratch=[pltpu.VMEM((3, tf, D), F32)], rider=rider)
    return outs[0], got


def _proj_fwd(h, g, w_inp, tm):
    S, D = h.shape

    def body(h_ref, g_ref, w_ref, pu_ref, ph_ref, n_ref):
        hv = h_ref[...]
        n = (hv * _rms(hv) * g_ref[...]).astype(BF16)
        n_ref[...] = n
        pu_ref[...] = _dot_nt(n, w_ref[0:PU, :])
        ph_ref[...] = _dot_nt(n, w_ref[PU:PP, :])

    cur = lambda w: pl.BlockSpec((tm, w), lambda i: (i, 0))
    return pl.pallas_call(
        body, name="proj_fwd", grid=(S // tm,),
        in_specs=[cur(D), pl.BlockSpec((1, D), lambda i: (0, 0)), pl.BlockSpec((PP, D), lambda i: (0, 0))],
        out_specs=[cur(PU), cur(PH), cur(D)],
        out_shape=[jax.ShapeDtypeStruct((S, PU), F32), jax.ShapeDtypeStruct((S, PH), F32),
                   jax.ShapeDtypeStruct((S, D), BF16)],
        compiler_params=_cp(),
    )(h, g, w_inp)


def _glu(u):
    return u[:, :CONV_CH] * _sigmoid(u[:, CONV_CH:2 * CONV_CH])


def _shifted_copies(ext8):
    n = ext8.shape[1]
    for b in range(1, 8):
        ext8[b, 0:n - 8, :] = ext8[0, b:b + n - 8, :]


def _window(ext8, off, rows, r0=0):
    return ext8[off % 8, pl.ds(r0 + (off - off % 8), rows), :]


def _layer_norm_stats(yc):
    mu = jnp.mean(yc, axis=-1, keepdims=True)
    d = yc - mu
    rstd = lax.rsqrt(jnp.mean(d * d, axis=-1, keepdims=True) + EPS)
    return d * rstd, rstd


def _mem_kv_fwd(mem, g, w_mkvp, gk):
    M, D = mem.shape
    W = SLOT * N_MEMH

    def body(mem_ref, g_ref, w_ref, gk_ref, nm_ref, raw_ref, mk_ref, mv_ref):
        mv_ = mem_ref[...]
        nm = (mv_ * _rms(mv_) * g_ref[...]).astype(BF16)
        nm_ref[...] = nm
        raw = _dot_nn(nm, w_ref[...])
        raw_ref[...] = raw
        for hh in range(N_MEMH):
            sl = slice(SLOT * hh, SLOT * (hh + 1))
            mk_ref[:, sl] = _slot_norm(raw[:, sl], gk_ref[...]).astype(BF16)
        mv_ref[...] = raw[:, W:].astype(BF16)

    sd = jax.ShapeDtypeStruct
    return pl.pallas_call(
        body, name="mem_kv_fwd",
        out_shape=[sd((M, D), BF16), sd((M, 2 * W), F32), sd((M, W), BF16), sd((M, W), BF16)],
        compiler_params=_cp(),
    )(mem, g, w_mkvp, gk)


def _mem_kv_bwd(dmk, dmv, raw, nm, mem, g, w_mkvp, gk):
    M, D = mem.shape
    W = SLOT * N_MEMH

    def body(dmk_ref, dmv_ref, raw_ref, nm_ref, mem_ref, g_ref, w_ref, gk_ref, dw_ref, dg_ref, dgk_ref, draw):
        dgk = jnp.zeros((1, SLOT), F32)
        for hh in range(N_MEMH):
            sl = slice(SLOT * hh, SLOT * (hh + 1))
            dx, prod = _slot_norm_bwd(dmk_ref[:, sl], raw_ref[:, sl], gk_ref[...])
            draw[:, sl] = dx.astype(BF16)
            dgk = dgk + _colsum(prod)
        dgk_ref[...] = dgk
        draw[:, W:] = dmv_ref[...].astype(BF16)
        dr = draw[...]
        dw_ref[...] = _dot_tn(nm_ref[...], dr)
        dnm = _dot_nt(dr, w_ref[...])
        mv_ = mem_ref[...]
        dg_ref[...] = _colsum(dnm * (mv_ * _rms(mv_)))

    sd = jax.ShapeDtypeStruct
    return pl.pallas_call(
        body, name="mem_kv_bwd",
        out_shape=[sd((D, 2 * W), F32), sd((1, D), F32), sd((1, SLOT), F32)],
        scratch_shapes=[pltpu.VMEM((M, 2 * W), BF16)],
        compiler_params=_cp(),
    )(dmk, dmv, raw, nm, mem, g, w_mkvp, gk)


def _mixer_fwd(pu, ph, h, cosT, sinT, conv_w, conv_b, ln_g, ln_b, gq, gk, sinks, mk, mv, gqm, w_outp, tm, rider=None):
    S, D = h.shape
    M = mk.shape[0]
    nb = tm // BLK
    nblocks = S // BLK

    def body(pu_ref, pup_ref, p_ref, ph_ref, h_ref, cos_ref, cosh_ref, sin_ref, sinh_ref, cw_ref, cb_ref,
             lg_ref, lb_ref, gq_ref, gk_ref, sink_ref, mk_ref, mv_ref, gqm_ref, wo_ref,
             h2_ref, y_ref, yc_ref, lse_ref, ext, y_scr):
        i = pl.program_id(0)
        not_first = (i > 0).astype(F32)
        lane = _lane(tm)
        lane_e = _lane(tm + BLK)

        ext[0, 0:HALO, :] = _glu(pup_ref[...]) * not_first
        ext[0, HALO:HALO + tm, :] = _glu(pu_ref[...])
        _shifted_copies(ext)

        def rows_chunk(r, carry):
            r0 = pl.multiple_of(r * CONV_ROWS, CONV_ROWS)
            yc = jnp.zeros((CONV_ROWS, CONV_CH), F32) + cb_ref[...]
            for k in range(CONV_WIDTH):
                yc = yc + cw_ref[k:k + 1, :] * _window(ext, HALO - (CONV_WIDTH - 1) + k, CONV_ROWS, r0)
            yc_ref[pl.ds(r0, CONV_ROWS), :] = yc
            z, _ = _layer_norm_stats(yc)
            ln = z * lg_ref[...] + lb_ref[...]
            y_scr[pl.ds(r0, CONV_ROWS), 0:CONV_CH] = (ln * _sigmoid(ln)).astype(BF16)
            return carry

        lax.fori_loop(0, tm // CONV_ROWS, rows_chunk, 0)

        cos_e = jnp.concatenate([cosh_ref[...], cos_ref[...]], axis=0)
        sin_e = jnp.concatenate([sinh_ref[...], sin_ref[...]], axis=0)
        qi = lax.broadcasted_iota(jnp.int32, (GROUP * BLK, 2 * BLK), 0) & (BLK - 1)
        kj = lax.broadcasted_iota(jnp.int32, (GROUP * BLK, 2 * BLK), 1)
        band = (kj > qi) & (kj <= qi + BLK)
        band0 = band & ((kj >= BLK) | (i > 0))
        lse = jnp.zeros((tm, SLOT), F32)
        k_pair = jnp.concatenate([ph_ref[:, KO:KO + SLOT], p_ref[:, KO:KO + SLOT]], axis=0)
        k_pair = _pair_fwd(k_pair, gk_ref[...], cos_e, sin_e, lane_e)
        v_pair = jnp.concatenate([ph_ref[:, VO:VO + SLOT], p_ref[:, VO:VO + SLOT]], axis=0)
        k_e = [_lo(k_pair, kvh, lane_e).astype(BF16) for kvh in range(N_KV)]
        v_e = [_lo(v_pair, kvh, lane_e).astype(BF16) for kvh in range(N_KV)]
        q_lo = []
        for j in range(N_Q // 2):
            q_pair = _pair_fwd(p_ref[:, QO + SLOT * j:QO + SLOT * (j + 1)], gq_ref[...],
                               cos_ref[...], sin_ref[...], lane)
            q_lo += [_lo(q_pair, 0, lane).astype(BF16), _lo(q_pair, 1, lane).astype(BF16)]
        outs = [[] for _ in range(N_Q)]
        lses = [[] for _ in range(N_Q)]
        for kvh in range(N_KV):
            hs = [GROUP * kvh + gi for gi in range(GROUP)]
            sink3 = jnp.concatenate([jnp.full((BLK, 1), sink_ref[h], F32) for h in hs], axis=0)
            for m in range(nb):
                rows = slice(BLK * m, BLK * (m + 1))
                win = slice(BLK * m, BLK * (m + 2))
                q3 = jnp.concatenate([q_lo[h][rows] for h in hs], axis=0)
                s = _dot_nt(q3, k_e[kvh][win]) * SCALE
                s = jnp.where(band0 if m == 0 else band, s, NEG)
                mx = jnp.maximum(jnp.max(s, axis=-1, keepdims=True), sink3)
                e = jnp.exp(s - mx)
                den = jnp.sum(e, axis=-1, keepdims=True) + jnp.exp(sink3 - mx)
                o3 = _dot_nn((e / den).astype(BF16), v_e[kvh][win])
                l3 = mx + jnp.log(den)
                for gi, h in enumerate(hs):
                    outs[h].append(o3[BLK * gi:BLK * (gi + 1)])
                    lses[h].append(l3[BLK * gi:BLK * (gi + 1)])
        for h in range(N_Q):
            lse = jnp.where(lane == h, jnp.concatenate(lses[h], axis=0), lse)
        for j in range(N_Q // 2):
            y_scr[:, YS + SLOT * j:YS + SLOT * (j + 1)] = _pack(
                jnp.concatenate(outs[2 * j], axis=0), jnp.concatenate(outs[2 * j + 1], axis=0), lane).astype(BF16)

        heads = []
        for hm in range(N_MEMH):
            ms = slice(SLOT * hm, SLOT * (hm + 1))
            if hm % 2 == 0:
                qm_pair = _pair_fwd(p_ref[:, MO + SLOT * (hm // 2):MO + SLOT * (hm // 2 + 1)], gqm_ref[...],
                                    None, None, lane)
            s = _dot_nt(_lo(qm_pair, hm % 2, lane).astype(BF16), mk_ref[:, ms]) * SCALE
            mx = jnp.max(s, axis=-1, keepdims=True)
            e = jnp.exp(s - mx)
            den = jnp.sum(e, axis=-1, keepdims=True)
            heads.append(_dot_nn((e / den).astype(BF16), mv_ref[:, ms]))
            lse = jnp.where(lane == N_Q + hm, mx + jnp.log(den), lse)
            if hm % 2 == 1:
                y_scr[:, YM + SLOT * (hm // 2):YM + SLOT * (hm // 2 + 1)] = _pack(heads[-2], heads[-1], lane).astype(BF16)
        lse_ref[...] = lse.T[0:STAT_ROWS, :]

        yv = y_scr[...]
        y_ref[...] = yv
        h2_ref[...] = h_ref[...] + _dot_nn(yv, wo_ref[...])

    cur = lambda w: pl.BlockSpec((tm, w), lambda i: (i, 0))
    prev = lambda w: pl.BlockSpec((BLK, w), lambda i: (jnp.maximum(i * nb - 1, 0), 0))
    full = lambda a: pl.BlockSpec(a.shape, lambda i: (0,) * a.ndim)
    sd = jax.ShapeDtypeStruct
    prev32 = pl.BlockSpec((HALO, PU), lambda i: (jnp.maximum(i * (tm // HALO) - 1, 0), 0))
    return _call(
        body, name="mixer_fwd", grid=(S // tm,),
        in_specs=[cur(PU), prev32, cur(PH), prev(PH), cur(D), cur(SLOT), prev(SLOT), cur(SLOT), prev(SLOT),
                  full(conv_w), full(conv_b), full(ln_g), full(ln_b), full(gq), full(gk),
                  pl.BlockSpec(memory_space=pltpu.SMEM), full(mk), full(mv), full(gqm), full(w_outp)],
        out_specs=[cur(D), cur(YP), cur(CONV_CH), pl.BlockSpec((STAT_ROWS, tm), lambda i: (0, i))],
        out_shape=[sd((S, D), F32), sd((S, YP), BF16), sd((S, CONV_CH), F32), sd((STAT_ROWS, S), F32)],
        args=[pu, pu, ph, ph, h, cosT, cosT, sinT, sinT, conv_w, conv_b, ln_g, ln_b, gq, gk, sinks, mk, mv, gqm,
              w_outp],
        scratch=[pltpu.VMEM((8, tm + HALO, CONV_CH), F32), pltpu.VMEM((tm, YP), BF16)], rider=rider)


def _outproj_bwd(dh2, y, yc, ln_g, ln_b, w_outp, tm):
    S, D = dh2.shape

    def body(dh_ref, y_ref, yc_ref, lg_ref, lb_ref, wo_ref, dyc_ref, do_ref, del_ref, dwo_ref, dlg_ref, dlb_ref):
        i = pl.program_id(0)

        @pl.when(i == 0)
        def _():
            dwo_ref[...] = jnp.zeros_like(dwo_ref)
            dlg_ref[...] = jnp.zeros_like(dlg_ref)
            dlb_ref[...] = jnp.zeros_like(dlb_ref)

        dhb = dh_ref[...].astype(BF16)
        yv = y_ref[...]
        dy = _dot_nt(dhb, wo_ref[...])
        dwo_ref[...] += _dot_tn(yv, dhb)

        z, rstd = _layer_norm_stats(yc_ref[...])
        ln = z * lg_ref[...] + lb_ref[...]
        sg = _sigmoid(ln)
        dln = dy[:, 0:CONV_CH] * (sg * (1.0 + ln * (1.0 - sg)))
        dlg_ref[...] += _colsum(dln * z)
        dlb_ref[...] += _colsum(dln)
        dz = dln * lg_ref[...]
        dyc_ref[...] = rstd * (dz - jnp.mean(dz, axis=-1, keepdims=True)
                               - z * jnp.mean(dz * z, axis=-1, keepdims=True))
        do_ref[...] = dy[:, CONV_CH:].astype(BF16)

        lane = _lane(tm)
        delta = jnp.zeros((tm, SLOT), F32)
        for j in range(NH // 2):
            sl = slice(YS + SLOT * j, YS + SLOT * (j + 1))
            prod = dy[:, sl] * yv[:, sl].astype(F32)
            lo = jnp.sum(jnp.where(lane < HEAD_DIM, prod, 0.0), axis=-1, keepdims=True)
            hi = jnp.sum(jnp.where(lane < HEAD_DIM, 0.0, prod), axis=-1, keepdims=True)
            delta = jnp.where(lane == 2 * j, lo, jnp.where(lane == 2 * j + 1, hi, delta))
        del_ref[...] = delta.T[0:STAT_ROWS, :]

    cur = lambda w: pl.BlockSpec((tm, w), lambda i: (i, 0))
    full = lambda a: pl.BlockSpec(a.shape, lambda i: (0,) * a.ndim)
    sd = jax.ShapeDtypeStruct
    return pl.pallas_call(
        body, name="outproj_bwd", grid=(S // tm,),
        in_specs=[cur(D), cur(YP), cur(CONV_CH), full(ln_g), full(ln_b), full(w_outp)],
        out_specs=[cur(CONV_CH), cur(YH), pl.BlockSpec((STAT_ROWS, tm), lambda i: (0, i)),
                   pl.BlockSpec((YP, D), lambda i: (0, 0)),
                   pl.BlockSpec((1, CONV_CH), lambda i: (0, 0)), pl.BlockSpec((1, CONV_CH), lambda i: (0, 0))],
        out_shape=[sd((S, CONV_CH), F32), sd((S, YH), BF16), sd((STAT_ROWS, S), F32), sd((YP, D), F32),
                   sd((1, CONV_CH), F32), sd((1, CONV_CH), F32)],
        compiler_params=_cp(),
    )(dh2, y, yc, ln_g, ln_b, w_outp)


def _conv_bwd(pu, dyc, conv_w, tm):
    S = pu.shape[0]
    nt = S // tm
    nh = tm // HALO

    def body(pu_ref, pup_ref, dy_ref, dyn_ref, cw_ref, dpu_ref, dcw_ref, dcb_ref, ext, ext2, dcw8):
        i = pl.program_id(0)

        @pl.when(i == 0)
        def _():
            dcw8[...] = jnp.zeros_like(dcw8)
            dcb_ref[...] = jnp.zeros_like(dcb_ref)

        not_first = (i > 0).astype(F32)
        not_last = (i < nt - 1).astype(F32)
        ext[0, 0:HALO, :] = _glu(pup_ref[...]) * not_first
        ext[0, HALO:HALO + tm, :] = _glu(pu_ref[...])
        _shifted_copies(ext)
        ext2[0, 0:tm, :] = dy_ref[...]
        ext2[0, tm:tm + HALO, :] = dyn_ref[...] * not_last
        _shifted_copies(ext2)
        dcb_ref[...] += _colsum(dy_ref[...])

        def rows_chunk(r, carry):
            r0 = pl.multiple_of(r * CONV_ROWS, CONV_ROWS)
            dyc_ = dy_ref[pl.ds(r0, CONV_ROWS), :]
            dyg = jnp.zeros((CONV_ROWS, CONV_CH), F32)
            for k in range(CONV_WIDTH):
                prod = dyc_ * _window(ext, HALO - (CONV_WIDTH - 1) + k, CONV_ROWS, r0)
                dcw8[k] += jnp.sum(prod.reshape(CONV_ROWS // 8, 8, CONV_CH), axis=0)
                dyg = dyg + cw_ref[k:k + 1, :] * _window(ext2, CONV_WIDTH - 1 - k, CONV_ROWS, r0)
            u = pu_ref[pl.ds(r0, CONV_ROWS), :]
            a_, sg = u[:, :CONV_CH], _sigmoid(u[:, CONV_CH:])
            dpu_ref[pl.ds(r0, CONV_ROWS), 0:CONV_CH] = (dyg * sg).astype(BF16)
            dpu_ref[pl.ds(r0, CONV_ROWS), CONV_CH:PU] = (dyg * a_ * sg * (1.0 - sg)).astype(BF16)
            return carry

        lax.fori_loop(0, tm // CONV_ROWS, rows_chunk, 0)

        @pl.when(i == nt - 1)
        def _():
            dcw_ref[...] = jnp.sum(dcw8[...], axis=1)

    cur = lambda w: pl.BlockSpec((tm, w), lambda i: (i, 0))
    prev = lambda w: pl.BlockSpec((HALO, w), lambda i: (jnp.maximum(i * nh - 1, 0), 0))
    nxt = lambda w: pl.BlockSpec((HALO, w), lambda i: (jnp.minimum((i + 1) * nh, S // HALO - 1), 0))
    acc = lambda r, w: pl.BlockSpec((r, w), lambda i: (0, 0))
    sd = jax.ShapeDtypeStruct
    return pl.pallas_call(
        body, name="conv_bwd", grid=(nt,),
        in_specs=[cur(PU), prev(PU), cur(CONV_CH), nxt(CONV_CH), acc(32, CONV_CH)],
        out_specs=[cur(PU), acc(32, CONV_CH), acc(1, CONV_CH)],
        out_shape=[sd((S, PU), BF16), sd((32, CONV_CH), F32), sd((1, CONV_CH), F32)],
        scratch_shapes=[pltpu.VMEM((8, tm + HALO, CONV_CH), F32), pltpu.VMEM((8, tm + HALO, CONV_CH), F32),
                        pltpu.VMEM((32, 8, CONV_CH), F32)],
        compiler_params=_cp(),
    )(pu, pu, dyc, dyc, conv_w)


def _attn_bwd(p, do, lse, delta, cosT, sinT, gq, gk, sinks, mk, mv, gqm, tm, rider=None):
    S = p.shape[0]
    M = mk.shape[0]
    nb = tm // BLK
    nt = S // tm
    nblocks = S // BLK
    W = SLOT * N_MEMH

    def body(p_ref, pp_ref, pn_ref, dy_ref, dyn_ref, lse_ref, lsen_ref, del_ref, deln_ref,
             cos_ref, cosp_ref, cosn_ref, sin_ref, sinp_ref, sinn_ref,
             gq_ref, gk_ref, sink_ref, mk_ref, mv_ref, gqm_ref,
             dp_ref, dgq_ref, dgk_ref, dgqm_ref, dsink_ref, dmk_ref, dmv_ref):
        i = pl.program_id(0)

        @pl.when(i == 0)
        def _():
            for r in (dgq_ref, dgk_ref, dgqm_ref, dsink_ref, dmk_ref, dmv_ref):
                r[...] = jnp.zeros_like(r)

        lane = _lane(tm)
        lane_e = _lane(tm + BLK)

        cos_k = jnp.concatenate([cosp_ref[...], cos_ref[...]], axis=0)
        sin_k = jnp.concatenate([sinp_ref[...], sin_ref[...]], axis=0)
        cos_q = jnp.concatenate([cos_ref[...], cosn_ref[...]], axis=0)
        sin_q = jnp.concatenate([sin_ref[...], sinn_ref[...]], axis=0)
        lse_e = jnp.concatenate([lse_ref[...], lsen_ref[...]], axis=1)
        del_e = jnp.concatenate([del_ref[...], deln_ref[...]], axis=1)
        kj = lax.broadcasted_iota(jnp.int32, (BLK, GROUP * BLK), 0)
        qi = lax.broadcasted_iota(jnp.int32, (BLK, GROUP * BLK), 1) & (BLK - 1)
        diag = kj <= qi
        offd = kj > qi
        dgq = jnp.zeros((1, SLOT), F32)
        dgk = jnp.zeros((1, SLOT), F32)
        dsink = jnp.zeros((1, SLOT), F32)
        lane1 = lax.broadcasted_iota(jnp.int32, (1, SLOT), 1)
        k_pair = jnp.concatenate([pp_ref[:, KO:KO + SLOT], p_ref[:, KO:KO + SLOT]], axis=0)
        k_pair = _pair_fwd(k_pair, gk_ref[...], cos_k, sin_k, lane_e)
        v_pair = jnp.concatenate([pp_ref[:, VO:VO + SLOT], p_ref[:, VO:VO + SLOT]], axis=0)
        k_e = [_lo(k_pair, kvh, lane_e).astype(BF16) for kvh in range(N_KV)]
        v_e = [_lo(v_pair, kvh, lane_e).astype(BF16) for kvh in range(N_KV)]
        dk = [[jnp.zeros((BLK, SLOT), F32) for _ in range(nb)] for _ in range(N_KV)]
        dv = [[jnp.zeros((BLK, SLOT), F32) for _ in range(nb)] for _ in range(N_KV)]
        q_e, do_e = [], []
        for j in range(N_Q // 2):
            js = slice(SLOT * j, SLOT * (j + 1))
            q_pair = _pair_fwd(jnp.concatenate([p_ref[:, js], pn_ref[:, js]], axis=0), gq_ref[...],
                               cos_q, sin_q, lane_e)
            do_pair = jnp.concatenate([dy_ref[:, js], dyn_ref[:, js]], axis=0).astype(F32)
            for half in range(2):
                q_e.append(_lo(q_pair, half, lane_e).astype(BF16))
                do_e.append(_lo(do_pair, half, lane_e).astype(BF16))
        dq_heads = [None] * N_Q
        for kvh in range(N_KV):
            hs = [GROUP * kvh + gi for gi in range(GROUP)]
            dq3 = [None] * nb
            for m in range(nb + 1):
                rows = slice(BLK * m, BLK * (m + 1))
                q3 = jnp.concatenate([q_e[h][rows] for h in hs], axis=0)
                do3 = jnp.concatenate([do_e[h][rows] for h in hs], axis=0)
                lb3 = jnp.concatenate([lse_e[h:h + 1, rows] for h in hs], axis=1)
                db3 = jnp.concatenate([del_e[h:h + 1, rows] for h in hs], axis=1)
                for n in (m - 1, m):
                    if n == nb:
                        continue
                    krows = slice(BLK * (n + 1), BLK * (n + 2))
                    kb, vb = k_e[kvh][krows], v_e[kvh][krows]
                    s = _dot_nt(kb, q3) * SCALE
                    mask = diag if n == m else offd
                    if n == -1:
                        mask = mask & (i > 0)
                    if m == nb:
                        mask = mask & (i < nt - 1)
                    prob = jnp.where(mask, jnp.exp(jnp.where(mask, s - lb3, NEG)), 0.0)
                    dpb = _dot_nt(vb, do3)
                    ds = (prob * (dpb - db3) * SCALE).astype(BF16)
                    if m < nb:
                        dqc = _dot_tn(ds, kb)
                        dq3[m] = dqc if dq3[m] is None else dq3[m] + dqc
                    if n >= 0:
                        dk[kvh][n] = dk[kvh][n] + _dot_nn(ds, q3)
                        dv[kvh][n] = dv[kvh][n] + _dot_nn(prob.astype(BF16), do3)
            for gi, h in enumerate(hs):
                dq_heads[h] = jnp.concatenate([dq3[m][BLK * gi:BLK * (gi + 1)] for m in range(nb)], axis=0)
                psink = jnp.exp(sink_ref[h] - lse_e[h:h + 1, 0:tm])
                dsink = dsink + jnp.where(
                    lane1 == h, -jnp.sum(psink * del_e[h:h + 1, 0:tm], axis=-1, keepdims=True), 0.0)
        for j in range(N_Q // 2):
            js = slice(SLOT * j, SLOT * (j + 1))
            dqr, prod = _pair_bwd(_pack(dq_heads[2 * j], dq_heads[2 * j + 1], lane), p_ref[:, js], gq_ref[...],
                                  cos_ref[...], sin_ref[...], lane)
            dp_ref[:, js] = dqr.astype(BF16)
            dgq = dgq + _colsum(prod)
        dk_pair = _pack(jnp.concatenate(dk[0], axis=0), jnp.concatenate(dk[1], axis=0), lane)
        dkr, prod = _pair_bwd(dk_pair, p_ref[:, KO:KO + SLOT], gk_ref[...], cos_ref[...], sin_ref[...], lane)
        dp_ref[:, KO:KO + SLOT] = dkr.astype(BF16)
        dp_ref[:, VO:VO + SLOT] = _pack(jnp.concatenate(dv[0], axis=0), jnp.concatenate(dv[1], axis=0),
                                        lane).astype(BF16)
        dgq_ref[...] += dgq
        dgk_ref[...] += _colsum(prod)
        dsink_ref[...] += dsink

        dgqm = jnp.zeros((1, SLOT), F32)
        dq_heads = []
        for hm in range(N_MEMH):
            ms = slice(SLOT * hm, SLOT * (hm + 1))
            js = slice(MO + SLOT * (hm // 2), MO + SLOT * (hm // 2 + 1))
            os_ = slice(SLOT * ((N_Q + hm) // 2), SLOT * ((N_Q + hm) // 2 + 1))
            if hm % 2 == 0:
                qm_pair = _pair_fwd(p_ref[:, js], gqm_ref[...], None, None, lane)
                do_pair = dy_ref[:, os_].astype(F32)
            qm = _lo(qm_pair, hm % 2, lane).astype(BF16)
            dob = _lo(do_pair, hm % 2, lane).astype(BF16)
            kb, vb = mk_ref[:, ms], mv_ref[:, ms]
            s = _dot_nt(kb, qm) * SCALE
            prob = jnp.exp(s - lse_ref[N_Q + hm:N_Q + hm + 1, :])
            dpb = _dot_nt(vb, dob)
            ds = (prob * (dpb - del_ref[N_Q + hm:N_Q + hm + 1, :]) * SCALE).astype(BF16)
            dq_heads.append(_dot_tn(ds, kb))
            dmk_ref[:, ms] += _dot_nn(ds, qm)
            dmv_ref[:, ms] += _dot_nn(prob.astype(BF16), dob)
            if hm % 2 == 1:
                dqr, prod = _pair_bwd(_pack(dq_heads[-2], dq_heads[-1], lane), p_ref[:, js], gqm_ref[...],
                                      None, None, lane)
                dp_ref[:, js] = dqr.astype(BF16)
                dgqm = dgqm + _colsum(prod)
        dgqm_ref[...] += dgqm

    cur = lambda w: pl.BlockSpec((tm, w), lambda i: (i, 0))
    prev = lambda w: pl.BlockSpec((BLK, w), lambda i: (jnp.maximum(i * nb - 1, 0), 0))
    nxt = lambda w: pl.BlockSpec((BLK, w), lambda i: (jnp.minimum((i + 1) * nb, nblocks - 1), 0))
    full = lambda a: pl.BlockSpec(a.shape, lambda i: (0,) * a.ndim)
    acc = lambda r, w: pl.BlockSpec((r, w), lambda i: (0, 0))
    sd = jax.ShapeDtypeStruct
    stat = pl.BlockSpec((STAT_ROWS, tm), lambda i: (0, i))
    stat_n = pl.BlockSpec((STAT_ROWS, BLK), lambda i: (0, jnp.minimum((i + 1) * nb, nblocks - 1)))
    return _call(
        body, name="attn_bwd", grid=(nt,),
        in_specs=[cur(PH), prev(PH), nxt(PH), cur(YH), nxt(YH), stat, stat_n, stat, stat_n,
                  cur(SLOT), prev(SLOT), nxt(SLOT), cur(SLOT), prev(SLOT), nxt(SLOT),
                  full(gq), full(gk), pl.BlockSpec(memory_space=pltpu.SMEM), full(mk), full(mv), full(gqm)],
        out_specs=[cur(PH), acc(1, SLOT), acc(1, SLOT), acc(1, SLOT), acc(1, SLOT), acc(M, W), acc(M, W)],
        out_shape=[sd((S, PH), BF16), sd((1, SLOT), F32), sd((1, SLOT), F32), sd((1, SLOT), F32),
                   sd((1, SLOT), F32), sd((M, W), F32), sd((M, W), F32)],
        args=[p, p, p, do, do, lse, lse, delta, delta, cosT, cosT, cosT, sinT, sinT, sinT,
              gq, gk, sinks, mk, mv, gqm],
        rider=rider)


def _proj_bwd(dpu, dph, h, dh2, g, n, w_inp, tm):
    S, D = h.shape

    def body(dpu_ref, dph_ref, h_ref, dh2_ref, g_ref, n_ref, w_ref, dh_ref, dg_ref, dw_ref):
        i = pl.program_id(0)

        @pl.when(i == 0)
        def _():
            dg_ref[...] = jnp.zeros_like(dg_ref)
            dw_ref[...] = jnp.zeros_like(dw_ref)

        dpu, dph, nv = dpu_ref[...], dph_ref[...], n_ref[...]
        dn = _dot_nn(dpu, w_ref[0:PU, :]) + _dot_nn(dph, w_ref[PU:PP, :])
        dw_ref[0:PU, :] += _dot_tn(dpu, nv)
        dw_ref[PU:PP, :] += _dot_tn(dph, nv)
        hv = h_ref[...]
        dx, dgrow = _rms_bwd(dn, hv, _rms(hv), g_ref[...])
        dh_ref[...] = dh2_ref[...] + dx
        dg_ref[...] += _colsum(dgrow)

    cur = lambda w: pl.BlockSpec((tm, w), lambda i: (i, 0))
    sd = jax.ShapeDtypeStruct
    return pl.pallas_call(
        body, name="proj_bwd", grid=(S // tm,),
        in_specs=[cur(PU), cur(PH), cur(D), cur(D), pl.BlockSpec((1, D), lambda i: (0, 0)), cur(D),
                  pl.BlockSpec((PP, D), lambda i: (0, 0))],
        out_specs=[cur(D), pl.BlockSpec((1, D), lambda i: (0, 0)), pl.BlockSpec((PP, D), lambda i: (0, 0))],
        out_shape=[sd((S, D), F32), sd((1, D), F32), sd((PP, D), F32)],
        compiler_params=_cp(),
    )(dpu, dph, h, dh2, g, n, w_inp)


def _norm_bwd(dxn, h, g, tm):
    S, D = h.shape

    def body(d_ref, h_ref, g_ref, dh_ref, dg_ref):
        @pl.when(pl.program_id(0) == 0)
        def _():
            dg_ref[...] = jnp.zeros_like(dg_ref)

        hv = h_ref[...]
        dx, dgrow = _rms_bwd(d_ref[...], hv, _rms(hv), g_ref[...])
        dh_ref[...] = dx
        dg_ref[...] += _colsum(dgrow)

    cur = pl.BlockSpec((tm, D), lambda i: (i, 0))
    vec = pl.BlockSpec((1, D), lambda i: (0, 0))
    return pl.pallas_call(
        body, name="norm_bwd", grid=(S // tm,), in_specs=[cur, cur, vec], out_specs=[cur, vec],
        out_shape=[jax.ShapeDtypeStruct((S, D), F32), jax.ShapeDtypeStruct((1, D), F32)],
        compiler_params=_cp(),
    )(dxn, h, g)


def _loss_bwd(xn, h, g, target, tm):
    S, D = h.shape

    def body(y_ref, h_ref, g_ref, t_ref, loss_ref, dh_ref, dg_ref):
        @pl.when(pl.program_id(0) == 0)
        def _():
            dg_ref[...] = jnp.zeros_like(dg_ref)
            loss_ref[...] = jnp.zeros_like(loss_ref)

        err = y_ref[...] - t_ref[...]
        part = jnp.sum(jnp.mean(err * err, axis=-1, keepdims=True), axis=0, keepdims=True)
        loss_ref[...] += 0.5 * part
        hv = h_ref[...]
        dx, dgrow = _rms_bwd(err * (1.0 / D), hv, _rms(hv), g_ref[...])
        dh_ref[...] = dx
        dg_ref[...] += _colsum(dgrow)

    cur = pl.BlockSpec((tm, D), lambda i: (i, 0))
    vec = pl.BlockSpec((1, D), lambda i: (0, 0))
    return pl.pallas_call(
        body, name="loss_bwd", grid=(S // tm,), in_specs=[cur, cur, vec, cur],
        out_specs=[pl.BlockSpec((1, SLOT), lambda i: (0, 0)), cur, vec],
        out_shape=[jax.ShapeDtypeStruct((1, SLOT), F32), jax.ShapeDtypeStruct((S, D), F32),
                   jax.ShapeDtypeStruct((1, D), F32)],
        compiler_params=_cp(),
    )(xn, h, g, target)


def _all_gather_small(buf):
    _, R, W = buf.shape

    def body(in_ref, out_ref, ssem, rsem, lsem):
        x, y, c, _ = _place()
        me = 4 * x + 2 * y + c
        local = pltpu.make_async_copy(in_ref, out_ref.at[pl.ds(me, 1)], lsem)
        local.start()

        def copy(k, block):
            fx, fy, fc = (k >> 2) & 1, (k >> 1) & 1, k & 1
            peer = (x ^ fx, y ^ fy, c ^ fc)
            return pltpu.make_async_remote_copy(
                src_ref=in_ref, dst_ref=out_ref.at[pl.ds(block, 1)], send_sem=ssem.at[k - 1],
                recv_sem=rsem.at[k - 1], device_id=peer, device_id_type=MESH)

        sends = [copy(k, me) for k in range(1, 8)]
        for cp in sends:
            cp.start()
        for k in range(1, 8):
            copy(k, me ^ k).wait_recv()
        for cp in sends:
            cp.wait_send()
        local.wait()

    hbm = pl.BlockSpec(memory_space=pl.ANY)
    return pl.pallas_call(
        body, name="all_gather_small", in_specs=[hbm], out_specs=hbm,
        out_shape=jax.ShapeDtypeStruct((8, R, W), buf.dtype),
        scratch_shapes=[pltpu.SemaphoreType.DMA((7,)), pltpu.SemaphoreType.DMA((7,)), pltpu.SemaphoreType.DMA],
    )(buf)


def _row_tile(n, cap=1024):
    for t in range(min(n, cap) // 8 * 8, 7, -8):
        if n % t == 0:
            return t
    return n


def _sum4(own, recv):
    n, rows, D = own.shape
    tr = _row_tile(rows)

    def body(o_ref, r0_ref, r1_ref, r2_ref, out_ref):
        out_ref[...] = ((o_ref[...].astype(F32) + r0_ref[...].astype(F32)) + r1_ref[...].astype(F32)) \
            + r2_ref[...].astype(F32)

    def rspec(p):
        return pl.BlockSpec((None, None, None, tr, D), lambda k, i, p=p: (p, k, 0, i, 0))

    blk = pl.BlockSpec((None, tr, D), lambda k, i: (k, i, 0))
    return pl.pallas_call(
        body, name="sum4", grid=(n, rows // tr),
        in_specs=[blk, rspec(0), rspec(1), rspec(2)], out_specs=blk,
        out_shape=jax.ShapeDtypeStruct((n, rows, D), F32),
    )(own, recv, recv, recv)


def _adam_math(w, g, m, v):
    m = ADAM_B1 * m + (1.0 - ADAM_B1) * g
    v = ADAM_B2 * v + (1.0 - ADAM_B2) * (g * g)
    m_hat = m / (1.0 - ADAM_B1 ** ADAM_STEP)
    v_hat = v / (1.0 - ADAM_B2 ** ADAM_STEP)
    delta = -ADAM_LR * (m_hat / (jnp.sqrt(v_hat) + ADAM_EPS) + ADAM_WD * w)
    return delta, m, v


def _adam_fused(w, m, v, parts, theirs, sel, row0, nrows, rider=None):
    L, R, D = w.shape
    assert R == nrows and parts[0].shape[2] == D
    t = _row_tile(nrows if row0 == 0 else _gcd(row0, nrows), 512)

    def gspec(k):
        return pl.BlockSpec((None, t, D), lambda l, c: (sel, row0 // t + jnp.where(l == k, c, 0), 0))

    def body(*refs):
        w_ref, m_ref, v_ref = refs[:3]
        p_refs, q_refs = refs[3:3 + L], refs[3 + L:3 + 2 * L]
        g_ref, d_ref, nm_ref, nv_ref, g_scr = refs[3 + 2 * L:]
        l = pl.program_id(0)
        for k in range(L):
            @pl.when(l == k)
            def _(k=k):
                g_scr[...] = p_refs[k][...] + q_refs[k][...]

        g = g_scr[...]
        g_ref[...] = g
        d, m_, v_ = _adam_math(w_ref[...], g, m_ref[...], v_ref[...])
        d_ref[...] = d
        nm_ref[...] = m_
        nv_ref[...] = v_

    blk = pl.BlockSpec((None, t, D), lambda l, c: (l, c, 0))
    return _call(body, name="adam_fused", grid=(L, R // t),
                 in_specs=[blk] * 3 + [gspec(k) for k in range(L)] * 2, out_specs=[blk] * 4,
                 out_shape=[jax.ShapeDtypeStruct((L, R, D), F32)] * 4,
                 args=[w, m, v, *parts, *theirs], scratch=[pltpu.VMEM((t, D), F32)], rider=rider)


def _gcd(a, b):
    while b:
        a, b = b, a % b
    return a


def _small_sum_adam(g8, w, m, v):
    _, R, W = g8.shape

    def body(g_ref, w_ref, m_ref, v_ref, go_ref, d_ref, nm_ref, nv_ref):
        g = g_ref[0]
        for k in range(1, 8):
            g = g + g_ref[k]
        go_ref[...] = g
        d, m_, v_ = _adam_math(w_ref[...], g, m_ref[...], v_ref[...])
        d_ref[...] = d
        nm_ref[...] = m_
        nv_ref[...] = v_

    return pl.pallas_call(body, name="small_sum_adam",
                          out_shape=[jax.ShapeDtypeStruct((R, W), F32)] * 4)(g8, w, m, v)


def _pad_vec(v):
    return jnp.pad(v, (0, SLOT - v.shape[0]))[None, :]


class _Pack:
    def __init__(self, shapes):
        self.shapes = shapes
        self.sizes = [int(functools.reduce(lambda a, b: a * b, s, 1)) for s in shapes]
        total = sum(self.sizes)
        self.rows = -(-total // (8 * SLOT)) * 8
        self.pad = self.rows * SLOT - total

    def pack(self, arrs):
        flat = jnp.concatenate([a.reshape(-1).astype(F32) for a in arrs] + [jnp.zeros((self.pad,), F32)])
        return flat.reshape(self.rows, SLOT)

    def unpack(self, buf):
        flat, out, o = buf.reshape(-1), [], 0
        for s, n in zip(self.shapes, self.sizes):
            out.append(flat[o:o + n].reshape(s))
            o += n
        return out


def kernel(x, mem, positions, ffn1_norm, ffn1_w1, ffn1_w3, ffn1_w2, mix_norm, w_in, conv_w, conv_b, conv_ln_g, conv_ln_b, swa_q_norm, swa_k_norm, swa_sinks, mem_norm, w_mem_kv, mem_q_norm, mem_k_norm, w_out, ffn2_norm, ffn2_w1, ffn2_w3, ffn2_w2, final_norm, loss_target, m_ffn1_norm, m_ffn1_w1, m_ffn1_w3, m_ffn1_w2, m_mix_norm, m_w_in, m_conv_w, m_conv_b, m_conv_ln_g, m_conv_ln_b, m_swa_q_norm, m_swa_k_norm, m_swa_sinks, m_mem_norm, m_w_mem_kv, m_mem_q_norm, m_mem_k_norm, m_w_out, m_ffn2_norm, m_ffn2_w1, m_ffn2_w3, m_ffn2_w2, m_final_norm, v_ffn1_norm, v_ffn1_w1, v_ffn1_w3, v_ffn1_w2, v_mix_norm, v_w_in, v_conv_w, v_conv_b, v_conv_ln_g, v_conv_ln_b, v_swa_q_norm, v_swa_k_norm, v_swa_sinks, v_mem_norm, v_w_mem_kv, v_mem_q_norm, v_mem_k_norm, v_w_out, v_ffn2_norm, v_ffn2_w1, v_ffn2_w3, v_ffn2_w2, v_final_norm):
    names = ['ffn1_norm', 'ffn1_w1', 'ffn1_w3', 'ffn1_w2', 'mix_norm', 'w_in', 'conv_w', 'conv_b', 'conv_ln_g',
             'conv_ln_b', 'swa_q_norm', 'swa_k_norm', 'swa_sinks', 'mem_norm', 'w_mem_kv', 'mem_q_norm',
             'mem_k_norm', 'w_out', 'ffn2_norm', 'ffn2_w1', 'ffn2_w3', 'ffn2_w2', 'final_norm']
    loc = locals()
    W = {n: loc[n] for n in names}
    M1 = {n: loc['m_' + n] for n in names}
    V1 = {n: loc['v_' + n] for n in names}

    S, D = x.shape[1], x.shape[2]
    L = ffn1_norm.shape[0]
    Fs = ffn1_w1.shape[2]
    F = 4 * Fs
    Mlen = mem.shape[1]
    cw_sh = conv_w.shape[2]
    tm = 512 if S >= 2048 else 256
    tf = 1408 if F % 1408 == 0 else 256
    tfw = 256
    tmw = 2048 if S >= 2048 else 256
    x0 = x[0]
    mem0 = mem[0]
    target = loss_target[0]
    my_chip = 2 * lax.axis_index("x") + lax.axis_index("y")

    mkv_rows = w_mem_kv.shape[1] * MEM_KV // D
    r_in, r_out = D_IN // 4, D_MIX // 4
    rm = r_in + r_out + mkv_rows

    shard = lambda w: w.astype(BF16).reshape((1, 1) + w.shape)

    groups = []
    for l in range(L):
        groups.append([shard(ffn1_w1[l].T), shard(ffn1_w3[l].T), shard(ffn1_w2[l])])
        groups.append([shard(w_in[l].T), shard(w_out[l]), shard(w_mem_kv[l].reshape(mkv_rows, D))])
        groups.append([shard(ffn2_w1[l].T), shard(ffn2_w3[l].T), shard(ffn2_w2[l])])
    gathered = [None] * len(groups)
    cw_rows = -(-(L * CONV_WIDTH) // 8) * 8
    cw_pad = jnp.pad(conv_w.reshape(L * CONV_WIDTH, cw_sh), ((0, cw_rows - L * CONV_WIDTH), (0, SLOT - cw_sh)))
    *gathered[0], cw_g = _run_rider(_Gather(groups[0] + [cw_pad.reshape(1, 1, cw_rows, SLOT)]), "all_gather_first")
    conv_wF = cw_g[0, :, :L * CONV_WIDTH, :cw_sh].reshape(4, L, CONV_WIDTH, cw_sh)
    conv_wF = jnp.moveaxis(conv_wF, 0, 2).reshape(L, CONV_WIDTH, 4 * cw_sh)
    conv_wP = jnp.pad(conv_wF, ((0, 0), (0, 32 - CONV_WIDTH), (0, 0)))

    def gather_rider(j):
        want = [k for k in [j + 1] if k < len(groups)]
        return (_Gather([b for k in want for b in groups[k]]), want) if want else (None, want)

    def keep(want, got):
        for n, k in enumerate(want):
            gathered[k] = got[3 * n:3 * n + 3]

    def ffn_weights(j):
        return tuple(g.reshape(F, D) for g in gathered[j])

    def mix_weights(l):
        g_in, g_out, g_mkv = gathered[3 * l + 1]
        w_inp = g_in.reshape(D_IN, D)
        w_outp = g_out.reshape(D_MIX, D)
        w_mkvp = jnp.pad(g_mkv.reshape(D, 2 * N_MEMH, HEAD_DIM),
                         ((0, 0), (0, 0), (0, SLOT - HEAD_DIM))).reshape(D, 2 * N_MEMH * SLOT)
        return w_inp, w_outp, w_mkvp

    inv_freq = ROPE_THETA ** (-jnp.arange(0, HEAD_DIM, 2, dtype=F32) / HEAD_DIM)
    invf = jnp.tile(inv_freq, SLOT // (HEAD_DIM // 2))[None, :]
    cosT, sinT = _rope_tables(positions.reshape(S, 1), invf, tm)

    row = lambda a, l: a[l][None, :]
    sinks_p = jnp.pad(swa_sinks, ((0, 0), (0, 8 - N_Q)))

    saved = []
    xin = x0
    xn = None
    for l in range(L):
        wf1 = ffn_weights(3 * l)
        rider, want = gather_rider(3 * l)
        (h1, a1, b1, t1), got = _ffn_fwd(xin, row(ffn1_norm, l), wf1, None, tm, tf, rider=rider)
        keep(want, got)
        w_inp, w_outp, w_mkvp = mix_weights(l)
        pu, p, n2 = _proj_fwd(h1, row(mix_norm, l), w_inp, tm)
        gk_m = _pad_vec(mem_k_norm[l])
        nm, mraw, mk, mv = _mem_kv_fwd(mem0, row(mem_norm, l), w_mkvp, gk_m)
        twice = lambda v: jnp.tile(v, 2)[None, :]
        gq, gk, gqm = twice(swa_q_norm[l]), twice(swa_k_norm[l]), twice(mem_q_norm[l])
        rider, want = gather_rider(3 * l + 1)
        (h2, y, yc, lse), got = _mixer_fwd(pu, p, h1, cosT, sinT, conv_wP[l], row(conv_b, l), row(conv_ln_g, l),
                                           row(conv_ln_b, l), gq, gk, sinks_p[l], mk, mv, gqm, w_outp, tm,
                                           rider=rider)
        keep(want, got)
        wf2 = ffn_weights(3 * l + 2)
        rider, want = gather_rider(3 * l + 2)
        (h3, a2, b2, t2, xn), got = _ffn_fwd(h2, row(ffn2_norm, l), wf2, row(final_norm, l), tm, tf, rider=rider)
        keep(want, got)
        saved.append(dict(xin=xin, h1=h1, a1=a1, b1=b1, pu=pu, p=p, n2=n2, nm=nm, mraw=mraw, mk=mk, mv=mv, gk_m=gk_m,
                          gq=gq, gk=gk, gqm=gqm, h2=h2, y=y, yc=yc, lse=lse, h3=h3, a2=a2, b2=b2, t1=t1, t2=t2,
                          wf1=wf1, wf2=wf2, w_inp=w_inp, w_outp=w_outp, w_mkvp=w_mkvp))
        xin = xn

    G = {n: [None] * L for n in names}
    ffn_bufs = [None] * (2 * L)
    mix_bufs = [None] * L
    parts, theirs = {}, {}
    pending_swap = []

    def scattered(key, buf, recv):
        own = lax.dynamic_index_in_dim(buf, my_chip, axis=1, keepdims=False)
        parts[key] = _sum4(own, recv)
        pending_swap.append(key)

    def comm_rider(scatter_buf):
        keys = list(pending_swap)
        pending_swap.clear()
        riders = ([_Scatter([scatter_buf])] if scatter_buf is not None else []) \
            + ([_Swap([parts[k] for k in keys])] if keys else [])
        if not riders:
            return None, lambda got: None
        multi = _Multi(riders)

        def store(got):
            outs = multi.split_outputs(got)
            if keys:
                for k, t in zip(keys, outs[-1]):
                    theirs[k] = t
            return outs[0][0] if scatter_buf is not None else None

        return multi, store

    dxn = None
    loss_part = None
    for l in reversed(range(L)):
        sv = saved[l]
        if l == L - 1:
            loss_part, dh3, G['final_norm'][l] = _loss_bwd(xn, sv['h3'], row(final_norm, l), target, tm)
        else:
            dh3, G['final_norm'][l] = _norm_bwd(dxn, sv['h3'], row(final_norm, l), tm)
        rider, store = comm_rider(ffn_bufs[2 * l + 2] if l < L - 1 else None)
        (dh2, G['ffn2_norm'][l], da, db, n, dy), got = _ffn_bwd_act(
            dh3, sv['h2'], row(ffn2_norm, l), sv['a2'], sv['b2'], sv['wf2'], tm, tf, rider=rider)
        recv = store(got)
        if recv is not None:
            scattered(('f', 2 * l + 2), ffn_bufs[2 * l + 2], recv)
        ffn_bufs[2 * l + 1] = _ffn_bwd_w(da, db, sv['t2'], n, dy, tmw, tfw)[0].reshape(3, 4, Fs, D)
        dyc, do, delta, dwo, G['conv_ln_g'][l], G['conv_ln_b'][l] = _outproj_bwd(
            dh2, sv['y'], sv['yc'], row(conv_ln_g, l), row(conv_ln_b, l), sv['w_outp'], tm)
        rider, store = comm_rider(ffn_bufs[2 * l + 1])
        (dph, dgq, dgk, dgqm, dsink, dmk, dmv), got = _attn_bwd(
            sv['p'], do, sv['lse'], delta, cosT, sinT, sv['gq'], sv['gk'], sinks_p[l],
            sv['mk'], sv['mv'], sv['gqm'], tm, rider=rider)
        scattered(('f', 2 * l + 1), ffn_bufs[2 * l + 1], store(got))
        dpu, dcw, G['conv_b'][l] = _conv_bwd(sv['pu'], dyc, conv_wP[l], tm)
        dwm, G['mem_norm'][l], dgk_m = _mem_kv_bwd(dmk, dmv, sv['mraw'], sv['nm'], mem0, row(mem_norm, l),
                                                   sv['w_mkvp'], sv['gk_m'])
        dh1, G['mix_norm'][l], dwi = _proj_bwd(dpu, dph, sv['h1'], dh2, row(mix_norm, l), sv['n2'], sv['w_inp'], tm)
        dwiT = dwi.reshape(4, r_in, D)
        dwoF = dwo.reshape(4, r_out, D)
        dwmF = dwm.reshape(D, 2 * N_MEMH, SLOT)[:, :, :HEAD_DIM].reshape(4, mkv_rows, D)
        mix_bufs[l] = jnp.concatenate([dwiT, dwoF, dwmF], axis=1).astype(BF16).reshape(1, 4, rm, D)
        rider, store = comm_rider(mix_bufs[l]) if l > 0 else (None, None)
        (dxl, G['ffn1_norm'][l], da, db, n, dy), got = _ffn_bwd_act(
            dh1, sv['xin'], row(ffn1_norm, l), sv['a1'], sv['b1'], sv['wf1'], tm, tf, rider=rider)
        if l > 0:
            scattered(('m', l), mix_bufs[l], store(got))
            rider, store = None, None
        else:
            rider, store = comm_rider(mix_bufs[l])
        dwf, got = _ffn_bwd_w(da, db, sv['t1'], n, dy, tmw, tfw, rider=rider)
        if l == 0:
            scattered(('m', l), mix_bufs[l], store(got))
        ffn_bufs[2 * l] = dwf.reshape(3, 4, Fs, D)
        dxn = dxl
        G['conv_w'][l] = dcw[:CONV_WIDTH]
        G['swa_q_norm'][l] = dgq[0, :HEAD_DIM] + dgq[0, HEAD_DIM:]
        G['swa_k_norm'][l] = dgk[0, :HEAD_DIM] + dgk[0, HEAD_DIM:]
        G['mem_q_norm'][l] = dgqm[0, :HEAD_DIM] + dgqm[0, HEAD_DIM:]
        G['mem_k_norm'][l] = dgk_m[0, :HEAD_DIM]
        G['swa_sinks'][l] = dsink[0, :N_Q]
    grad_x = dxn[None]
    loss = lax.psum(loss_part[0, 0], AXES)
    rider, store = comm_rider(ffn_bufs[0])
    scattered(('f', 0), ffn_bufs[0], store(_run_rider(rider, "scatter_last")))
    rider, store = comm_rider(None)
    store(_run_rider(rider, "swap_last"))

    small = ['ffn1_norm', 'mix_norm', 'conv_b', 'conv_ln_g', 'conv_ln_b', 'swa_q_norm', 'swa_k_norm', 'swa_sinks',
             'mem_norm', 'mem_q_norm', 'mem_k_norm', 'ffn2_norm', 'final_norm']
    gsmall = [jnp.stack([G[n][l].reshape(-1) for l in range(L)]) for n in small]
    gcw = jnp.stack(G['conv_w'])
    cw_cols = 4 * cw_sh
    full_of = lambda a: lax.dynamic_update_slice(jnp.zeros((L, CONV_WIDTH, cw_cols), F32), a, (0, 0, my_chip * cw_sh))
    pk = _Pack([W[n].shape for n in small] + [(L, CONV_WIDTH, cw_cols)])
    g8 = _all_gather_small(pk.pack(gsmall + [gcw])[None])
    outs4 = _small_sum_adam(g8, pk.pack([W[n] for n in small] + [full_of(conv_w)]),
                            pk.pack([M1[n] for n in small] + [full_of(m_conv_w)]),
                            pk.pack([V1[n] for n in small] + [full_of(v_conv_w)]))
    un = [pk.unpack(o) for o in outs4]
    grads, deltas, new_m, new_v = {}, {}, {}, {}
    for k, n in enumerate(small):
        grads[n], deltas[n], new_m[n], new_v[n] = un[0][k], un[1][k], un[2][k], un[3][k]
    mine = lambda a: lax.dynamic_slice(a, (0, 0, my_chip * cw_sh), (L, CONV_WIDTH, cw_sh))
    grads['conv_w'], deltas['conv_w'], new_m['conv_w'], new_v['conv_w'] = [mine(u[-1]) for u in un]

    ffn1_k, ffn2_k = [('f', 2 * l) for l in range(L)], [('f', 2 * l + 1) for l in range(L)]
    mix_k = [('m', l) for l in range(L)]
    plan = [('ffn2_w1', ffn2_k, 0, 0, Fs, True), ('ffn2_w3', ffn2_k, 1, 0, Fs, True), ('ffn2_w2', ffn2_k, 2, 0, Fs, False),
            ('w_in', mix_k, 0, 0, r_in, True), ('w_out', mix_k, 0, r_in, r_out, False),
            ('w_mem_kv', mix_k, 0, r_in + r_out, mkv_rows, False),
            ('ffn1_w1', ffn1_k, 0, 0, Fs, True), ('ffn1_w3', ffn1_k, 1, 0, Fs, True), ('ffn1_w2', ffn1_k, 2, 0, Fs, False)]
    for n, keys, sel, row0, nrows, held_transposed in plan:
        shp = W[n].shape
        if held_transposed:
            view, back = (lambda a: jnp.swapaxes(a, 1, 2)), (lambda a: jnp.swapaxes(a, 1, 2))
        elif n == 'w_mem_kv':
            view, back = (lambda a: a.reshape(L, mkv_rows, D)), (lambda a: a.reshape(shp))
        else:
            view = back = lambda a: a
        res, _ = _adam_fused(view(W[n]), view(M1[n]), view(V1[n]), [parts[k] for k in keys],
                             [theirs[k] for k in keys], sel, row0, nrows)
        grads[n], deltas[n], new_m[n], new_v[n] = [back(r) for r in res]

    return (loss, grad_x, *[grads[n] for n in names], *[deltas[n] for n in names],
            *[new_m[n] for n in names], *[new_v[n] for n in names])
```

```python
import functools

import jax
import jax.numpy as jnp
from jax import lax
from jax.experimental import pallas as pl
from jax.experimental.pallas import tpu as pltpu

F32 = jnp.float32
BF16 = jnp.bfloat16
MESH = pl.DeviceIdType.MESH
AXES = ("x", "y", "c")

EPS = 1e-6
HEAD_DIM = 64
SLOT = 128
CONV_CH = 384
CONV_WIDTH = 31
N_Q, N_KV, N_MEMH = 6, 2, 4
GROUP = N_Q // N_KV
BLK = 128
HALO = 32
CONV_ROWS = 64
FFN_CHUNK = 256
ROPE_THETA = 10000.0
SCALE = HEAD_DIM ** -0.5
NEG = -1e30

N_HEADS_IN = N_Q + 2 * N_KV + N_MEMH
PU = 2 * CONV_CH
PH = HEAD_DIM * N_HEADS_IN
PP = PU + PH
QO = 0
KO = QO + HEAD_DIM * N_Q
VO = KO + HEAD_DIM * N_KV
MO = VO + HEAD_DIM * N_KV
NH = N_Q + N_MEMH
STAT_ROWS = 16
YH = HEAD_DIM * NH
YP = CONV_CH + YH
YS = CONV_CH
YM = YS + HEAD_DIM * N_Q
D_IN = PP
D_MIX = YP
MEM_KV = 2 * HEAD_DIM * N_MEMH

ADAM_LR, ADAM_B1, ADAM_B2, ADAM_EPS, ADAM_WD, ADAM_STEP = 0.001, 0.9, 0.999, 1e-08, 0.01, 10

VMEM_LIMIT_MB = 56


def _cp(mb=VMEM_LIMIT_MB):
    return pltpu.CompilerParams(vmem_limit_bytes=mb * 1024 * 1024)


def _dot_nn(a, b):
    return lax.dot_general(a, b, (((1,), (0,)), ((), ())), preferred_element_type=F32)


def _dot_nt(a, b):
    return lax.dot_general(a, b, (((1,), (1,)), ((), ())), preferred_element_type=F32)


def _dot_tn(a, b):
    return lax.dot_general(a, b, (((0,), (0,)), ((), ())), preferred_element_type=F32)


def _sigmoid(x):
    return 1.0 / (1.0 + jnp.exp(-x))


def _rms(x):
    return lax.rsqrt(jnp.mean(x * x, axis=-1, keepdims=True) + EPS)


def _rms_bwd(dn, x, r, g):
    xhat = x * r
    dxhat = dn * g
    dx = r * (dxhat - xhat * jnp.mean(dxhat * xhat, axis=-1, keepdims=True))
    return dx, dn * xhat


def _colsum(v):
    return jnp.sum(v, axis=0, keepdims=True)


def _lane(n):
    return lax.broadcasted_iota(jnp.int32, (n, SLOT), 1)


def _slot_rms(xs):
    return lax.rsqrt(jnp.sum(xs * xs, axis=-1, keepdims=True) * (1.0 / HEAD_DIM) + EPS)


def _slot_norm(xs, g):
    return xs * _slot_rms(xs) * g


def _slot_norm_bwd(dout, xs, g):
    r = _slot_rms(xs)
    xhat = xs * r
    dxhat = dout * g
    dx = r * (dxhat - xhat * (jnp.sum(dxhat * xhat, axis=-1, keepdims=True) * (1.0 / HEAD_DIM)))
    return dx, dout * xhat


def _halves(v, lane):
    lo = jnp.sum(jnp.where(lane < HEAD_DIM, v, 0.0), axis=-1, keepdims=True)
    hi = jnp.sum(jnp.where(lane < HEAD_DIM, 0.0, v), axis=-1, keepdims=True)
    return jnp.where(lane < HEAD_DIM, lo, hi)


def _pair_rms(x, lane):
    return lax.rsqrt(_halves(x * x, lane) * (1.0 / HEAD_DIM) + EPS)


def _pair_partner(v, lane):
    return jnp.where((lane & (HEAD_DIM - 1)) < HEAD_DIM // 2,
                     pltpu.roll(v, SLOT - HEAD_DIM // 2, 1), pltpu.roll(v, HEAD_DIM // 2, 1))


def _pair_fwd(x, g2, cosv, sinv, lane):
    xn = x * _pair_rms(x, lane) * g2
    if cosv is None:
        return xn
    return xn * cosv + _pair_partner(xn, lane) * sinv


def _pair_bwd(dout, x, g2, cosv, sinv, lane):
    if cosv is not None:
        dout = dout * cosv + _pair_partner(dout * sinv, lane)
    r = _pair_rms(x, lane)
    xhat = x * r
    dxhat = dout * g2
    dx = r * (dxhat - xhat * (_halves(dxhat * xhat, lane) * (1.0 / HEAD_DIM)))
    return dx, dout * xhat


def _lo(x, half, lane):
    if half:
        x = pltpu.roll(x, HEAD_DIM, 1)
    return jnp.where(lane < HEAD_DIM, x, 0.0)


def _pack(even, odd, lane):
    return jnp.where(lane < HEAD_DIM, even, pltpu.roll(odd, HEAD_DIM, 1))


def _place():
    x, y, c = lax.axis_index("x"), lax.axis_index("y"), lax.axis_index("c")
    chips = [(1 - x, y), (x, 1 - y), (1 - x, 1 - y)]
    return x, y, c, chips


class _Gather:
    tag = "_gather"

    def __init__(self, bufs):
        self.bufs = list(bufs)
        nb = len(self.bufs)
        self.out_shape = [jax.ShapeDtypeStruct((b.shape[0], 4) + b.shape[2:], b.dtype) for b in self.bufs]
        self.sems = [pltpu.SemaphoreType.DMA((3 * nb,)), pltpu.SemaphoreType.DMA((3 * nb,)),
                     pltpu.SemaphoreType.DMA((nb,))]

    def _copies(self, ins, outs, sems):
        ssem, rsem, lsem = sems
        nb = len(self.bufs)
        x, y, c, chips = _place()
        mine = 2 * x + y

        def copy(b, p, shard):
            return pltpu.make_async_remote_copy(
                src_ref=ins[b], dst_ref=outs[b].at[:, pl.ds(shard, 1)],
                send_sem=ssem.at[3 * b + p], recv_sem=rsem.at[3 * b + p],
                device_id=(chips[p][0], chips[p][1], c), device_id_type=MESH)

        local = [pltpu.make_async_copy(ins[b], outs[b].at[:, pl.ds(mine, 1)], lsem.at[b]) for b in range(nb)]
        sends = [copy(b, p, mine) for b in range(nb) for p in range(3)]
        recvs = [copy(b, p, 2 * chips[p][0] + chips[p][1]) for b in range(nb) for p in range(3)]
        return local, sends, recvs

    def start(self, ins, outs, sems):
        local, sends, _ = self._copies(ins, outs, sems)
        for cp in local + sends:
            cp.start()

    def wait(self, ins, outs, sems):
        local, sends, recvs = self._copies(ins, outs, sems)
        for cp in recvs:
            cp.wait_recv()
        for cp in sends:
            cp.wait_send()
        for cp in local:
            cp.wait()


class _GatherHalves:
    tag = "_gather_halves"

    def __init__(self, bufs):
        self.bufs = list(bufs)
        nb = len(self.bufs)
        assert all(b.shape[2] % 32 == 0 for b in self.bufs)
        self.out_shape = [jax.ShapeDtypeStruct((b.shape[0], 4) + b.shape[2:], b.dtype) for b in self.bufs]
        self.sems = [pltpu.SemaphoreType.DMA((3 * nb,)), pltpu.SemaphoreType.DMA((3 * nb,)),
                     pltpu.SemaphoreType.DMA((3 * nb,)), pltpu.SemaphoreType.DMA((3 * nb,)),
                     pltpu.SemaphoreType.DMA((nb,))]

    def start(self, ins, outs, sems):
        ssem, rsem, fsem, gsem, lsem = sems
        nb = len(self.bufs)
        x, y, c, chips = _place()
        mine = 2 * x + y
        shard_of = [2 * cx + cy for cx, cy in chips]

        def half(b, h):
            n = self.bufs[b].shape[2] // 2
            return pl.ds(pl.multiple_of(h * n, 16), n)

        def over_links(b, p, shard, h):
            return pltpu.make_async_remote_copy(
                src_ref=ins[b].at[:, :, half(b, h)], dst_ref=outs[b].at[:, pl.ds(shard, 1), half(b, h)],
                send_sem=ssem.at[3 * b + p], recv_sem=rsem.at[3 * b + p],
                device_id=(chips[p][0], chips[p][1], c), device_id_type=MESH)

        def between_cores(b, p, h):
            blk = outs[b].at[:, pl.ds(shard_of[p], 1), half(b, h)]
            return pltpu.make_async_remote_copy(
                src_ref=blk, dst_ref=blk, send_sem=fsem.at[3 * b + p], recv_sem=gsem.at[3 * b + p],
                device_id=(x, y, 1 - c), device_id_type=MESH)

        local = [pltpu.make_async_copy(ins[b], outs[b].at[:, pl.ds(mine, 1)], lsem.at[b]) for b in range(nb)]
        sends = [over_links(b, p, mine, c) for b in range(nb) for p in range(3)]
        for cp in local + sends:
            cp.start()
        handed = []
        for b in range(nb):
            for p in range(3):
                over_links(b, p, shard_of[p], c).wait_recv()
                cp = between_cores(b, p, c)
                cp.start()
                handed.append(cp)
        for b in range(nb):
            for p in range(3):
                between_cores(b, p, 1 - c).wait_recv()
        for cp in sends + handed:
            cp.wait_send()
        for cp in local:
            cp.wait()

    def wait(self, ins, outs, sems):
        pass


class _Scatter:
    tag = "_scatter"

    def __init__(self, bufs):
        self.bufs = list(bufs)
        nb = len(self.bufs)
        self.out_shape = [jax.ShapeDtypeStruct((3, b.shape[0], 1) + b.shape[2:], b.dtype) for b in self.bufs]
        self.sems = [pltpu.SemaphoreType.DMA((3 * nb,)), pltpu.SemaphoreType.DMA((3 * nb,))]

    def _copies(self, ins, outs, sems):
        ssem, rsem = sems
        x, y, c, chips = _place()

        def copy(b, p):
            shard = 2 * chips[p][0] + chips[p][1]
            return pltpu.make_async_remote_copy(
                src_ref=ins[b].at[:, pl.ds(shard, 1)], dst_ref=outs[b].at[p],
                send_sem=ssem.at[3 * b + p], recv_sem=rsem.at[3 * b + p],
                device_id=(chips[p][0], chips[p][1], c), device_id_type=MESH)

        return [copy(b, p) for b in range(len(self.bufs)) for p in range(3)]

    def start(self, ins, outs, sems):
        for cp in self._copies(ins, outs, sems):
            cp.start()

    def wait(self, ins, outs, sems):
        cps = self._copies(ins, outs, sems)
        for cp in cps:
            cp.wait_recv()
        for cp in cps:
            cp.wait_send()


class _Swap:
    tag = "_swap"

    def __init__(self, bufs):
        self.bufs = list(bufs)
        nb = len(self.bufs)
        self.out_shape = [jax.ShapeDtypeStruct(b.shape, b.dtype) for b in self.bufs]
        self.sems = [pltpu.SemaphoreType.DMA((nb,)), pltpu.SemaphoreType.DMA((nb,))]

    def _copies(self, ins, outs, sems):
        ssem, rsem = sems
        x, y, c, _ = _place()
        return [pltpu.make_async_remote_copy(src_ref=ins[b], dst_ref=outs[b], send_sem=ssem.at[b],
                                             recv_sem=rsem.at[b], device_id=(x, y, 1 - c), device_id_type=MESH)
                for b in range(len(self.bufs))]

    def start(self, ins, outs, sems):
        for cp in self._copies(ins, outs, sems):
            cp.start()

    def wait(self, ins, outs, sems):
        cps = self._copies(ins, outs, sems)
        for cp in cps:
            cp.wait_recv()
        for cp in cps:
            cp.wait_send()


class _Multi:
    def __init__(self, riders):
        self.riders = [r for r in riders if r is not None and r.bufs]
        self.tag = "".join(r.tag for r in self.riders)
        self.bufs = [b for r in self.riders for b in r.bufs]
        self.out_shape = [s for r in self.riders for s in r.out_shape]
        self.sems = [s for r in self.riders for s in r.sems]

    def _split(self, ins, outs, sems):
        ob, os_ = 0, 0
        for r in self.riders:
            nb, ns = len(r.bufs), len(r.sems)
            yield r, ins[ob:ob + nb], outs[ob:ob + nb], sems[os_:os_ + ns]
            ob, os_ = ob + nb, os_ + ns

    def start(self, ins, outs, sems):
        for r, i, o, s in self._split(ins, outs, sems):
            r.start(i, o, s)

    def wait(self, ins, outs, sems):
        for r, i, o, s in self._split(ins, outs, sems):
            r.wait(i, o, s)

    def split_outputs(self, got):
        res, ob = [], 0
        for r in self.riders:
            res.append(got[ob:ob + len(r.bufs)])
            ob += len(r.bufs)
        return res


def _run_rider(rider, name):
    nb = len(rider.bufs)

    def body(*refs):
        ins, outs, sems = refs[:nb], refs[nb:2 * nb], refs[2 * nb:]
        rider.start(ins, outs, sems)
        rider.wait(ins, outs, sems)

    hbm = pl.BlockSpec(memory_space=pl.ANY)
    return pl.pallas_call(body, name=name, in_specs=[hbm] * nb, out_specs=[hbm] * nb,
                          out_shape=rider.out_shape, scratch_shapes=rider.sems)(*rider.bufs)


def _call(body, *, name, grid, in_specs, out_specs, out_shape, args, scratch=(), rider=None):
    if rider is None:
        outs = pl.pallas_call(body, name=name, grid=grid, in_specs=list(in_specs), out_specs=list(out_specs),
                              out_shape=list(out_shape), scratch_shapes=list(scratch),
                              compiler_params=_cp())(*args)
        return list(outs), None
    n_in, n_out, n_scr, nb = len(in_specs), len(out_specs), len(scratch), len(rider.bufs)

    def wrapped(*refs):
        cuts = [n_in, nb, n_out, nb, n_scr]
        parts, o = [], 0
        for n in cuts:
            parts.append(refs[o:o + n])
            o += n
        ins, rin, outs, rout, scr = parts
        sems = refs[o:]
        ids = [pl.program_id(k) for k in range(len(grid))]
        first = functools.reduce(jnp.logical_and, [i == 0 for i in ids])
        last = functools.reduce(jnp.logical_and, [i == n - 1 for i, n in zip(ids, grid)])

        @pl.when(first)
        def _():
            rider.start(rin, rout, sems)

        body(*ins, *outs, *scr)

        @pl.when(last)
        def _():
            rider.wait(rin, rout, sems)

    hbm = pl.BlockSpec(memory_space=pl.ANY)
    res = pl.pallas_call(
        wrapped, name=name + rider.tag, grid=grid,
        in_specs=list(in_specs) + [hbm] * nb, out_specs=list(out_specs) + [hbm] * nb,
        out_shape=list(out_shape) + rider.out_shape, scratch_shapes=list(scratch) + rider.sems,
        compiler_params=_cp())(*args, *rider.bufs)
    return list(res[:n_out]), list(res[n_out:])


def _rope_tables(pos, invf, tm):
    S = pos.shape[0]

    def body(pos_ref, f_ref, cos_ref, sin_ref):
        ang = pos_ref[...].astype(F32) * f_ref[...]
        lane = _lane(tm)
        cos_ref[...] = jnp.cos(ang)
        s = jnp.sin(ang)
        sin_ref[...] = jnp.where((lane & (HEAD_DIM - 1)) < HEAD_DIM // 2, -s, s)

    return pl.pallas_call(
        body, name="rope_tables", grid=(S // tm,),
        in_specs=[pl.BlockSpec((tm, 1), lambda i: (i, 0)), pl.BlockSpec((1, SLOT), lambda i: (0, 0))],
        out_specs=[pl.BlockSpec((tm, SLOT), lambda i: (i, 0))] * 2,
        out_shape=[jax.ShapeDtypeStruct((S, SLOT), F32)] * 2,
    )(pos, invf)


def _ffn_fwd(x, g, wf, gfin, tm, tf, rider=None):
    S, D = x.shape
    F = wf[0].shape[0]
    nf = F // tf
    final = gfin is not None

    chunks = [(c, min(FFN_CHUNK, tf - c)) for c in range(0, tf, FFN_CHUNK)]

    def body(*refs):
        if final:
            x_ref, g_ref, w1_ref, w3_ref, w2_ref, gf_ref, h_ref, a_ref, b_ref, t_ref, xn_ref, n_scr, acc = refs
        else:
            x_ref, g_ref, w1_ref, w3_ref, w2_ref, h_ref, a_ref, b_ref, t_ref, n_scr, acc = refs
        j = pl.program_id(1)

        @pl.when(j == 0)
        def _():
            xv = x_ref[...]
            n_scr[...] = (xv * _rms(xv) * g_ref[...]).astype(BF16)
            acc[...] = jnp.zeros_like(acc)

        n = n_scr[...]
        for c0, cw in chunks:
            cols = slice(c0, c0 + cw)
            a = _dot_nt(n, w1_ref[cols, :])
            b = _dot_nt(n, w3_ref[cols, :])
            a_ref[:, cols] = a.astype(BF16)
            b_ref[:, cols] = b.astype(BF16)
            t_ref[:, cols] = (a * _sigmoid(a) * b).astype(BF16)
        acc[...] += _dot_nn(t_ref[...], w2_ref[...])

        @pl.when(j == nf - 1)
        def _():
            h = x_ref[...] + 0.5 * acc[...]
            h_ref[...] = h
            if final:
                xn_ref[...] = h * _rms(h) * gf_ref[...]

    def wspec(k):
        return pl.BlockSpec((tf, D), lambda i, j: (j, 0))

    row = pl.BlockSpec((tm, D), lambda i, j: (i, 0))
    vec = pl.BlockSpec((1, D), lambda i, j: (0, 0))
    act = pl.BlockSpec((tm, tf), lambda i, j: (i, j))
    in_specs = [row, vec, wspec(0), wspec(1), wspec(2)] + ([vec] if final else [])
    out_specs = [row, act, act, act] + ([row] if final else [])
    out_shape = [jax.ShapeDtypeStruct((S, D), F32)] + [jax.ShapeDtypeStruct((S, F), BF16)] * 3 \
        + ([jax.ShapeDtypeStruct((S, D), F32)] if final else [])
    args = [x, g, *wf] + ([gfin] if final else [])
    return _call(body, name="ffn_fwd_final" if final else "ffn_fwd", grid=(S // tm, nf),
                 in_specs=in_specs, out_specs=out_specs, out_shape=out_shape, args=args,
                 scratch=[pltpu.VMEM((tm, D), BF16), pltpu.VMEM((tm, D), F32)], rider=rider)


def _ffn_bwd_act(dh, x, g, a, b, wf, tm, tf, rider=None):
    S, D = x.shape
    F = wf[0].shape[0]
    nf = F // tf

    chunks = [(c, min(FFN_CHUNK, tf - c)) for c in range(0, tf, FFN_CHUNK)]

    def body(dh_ref, x_ref, g_ref, a_ref, b_ref, w1_ref, w3_ref, w2_ref,
             dx_ref, dg_ref, da_ref, db_ref, n_ref, dy_ref, acc):
        i, j = pl.program_id(0), pl.program_id(1)

        @pl.when(j == 0)
        def _():
            xv = x_ref[...]
            n_ref[...] = (xv * _rms(xv) * g_ref[...]).astype(BF16)
            dy_ref[...] = (0.5 * dh_ref[...]).astype(BF16)
            acc[...] = jnp.zeros_like(acc)

            @pl.when(i == 0)
            def _():
                dg_ref[...] = jnp.zeros_like(dg_ref)

        dyv = dy_ref[...]
        for c0, cw in chunks:
            cols = slice(c0, c0 + cw)
            av = a_ref[:, cols].astype(F32)
            bv = b_ref[:, cols].astype(F32)
            sg = _sigmoid(av)
            dt = _dot_nt(dyv, w2_ref[cols, :])
            db_ref[:, cols] = (dt * (av * sg)).astype(BF16)
            da_ref[:, cols] = (dt * bv * (sg * (1.0 + av * (1.0 - sg)))).astype(BF16)
        acc[...] += _dot_nn(da_ref[...], w1_ref[...]) + _dot_nn(db_ref[...], w3_ref[...])

        @pl.when(j == nf - 1)
        def _():
            xv = x_ref[...]
            dx, dgrow = _rms_bwd(acc[...], xv, _rms(xv), g_ref[...])
            dx_ref[...] = dh_ref[...] + dx
            dg_ref[...] += _colsum(dgrow)

    def wspec(k):
        return pl.BlockSpec((tf, D), lambda i, j: (j, 0))

    row = pl.BlockSpec((tm, D), lambda i, j: (i, 0))
    vec = pl.BlockSpec((1, D), lambda i, j: (0, 0))
    act = pl.BlockSpec((tm, tf), lambda i, j: (i, j))
    sd = lambda shp, dt: jax.ShapeDtypeStruct(shp, dt)
    return _call(body, name="ffn_bwd_act", grid=(S // tm, nf),
                 in_specs=[row, row, vec, act, act, wspec(0), wspec(1), wspec(2)],
                 out_specs=[row, vec, act, act, row, row],
                 out_shape=[sd((S, D), F32), sd((1, D), F32), sd((S, F), BF16), sd((S, F), BF16),
                            sd((S, D), BF16), sd((S, D), BF16)],
                 args=[dh, x, g, a, b, *wf], scratch=[pltpu.VMEM((tm, D), F32)], rider=rider)


def _ffn_bwd_w(da, db, t, n, dy, tm, tf, rider=None):
    S, F = da.shape
    D = n.shape[1]
    nt = S // tm

    def body(da_ref, db_ref, t_ref, n_ref, dy_ref, out_ref, acc):
        i = pl.program_id(1)

        @pl.when(i == 0)
        def _():
            acc[...] = jnp.zeros_like(acc)

        nv = n_ref[...]
        acc[0] += _dot_tn(da_ref[...], nv)
        acc[1] += _dot_tn(db_ref[...], nv)
        acc[2] += _dot_tn(t_ref[...], dy_ref[...])

        @pl.when(i == nt - 1)
        def _():
            out_ref[...] = acc[...].astype(BF16)

    act = pl.BlockSpec((tm, tf), lambda j, i: (i, j))
    row = pl.BlockSpec((tm, D), lambda j, i: (i, 0))
    outs, got = _call(body, name="ffn_bwd_w", grid=(F // tf, nt),
                      in_specs=[act, act, act, row, row],
                      out_specs=[pl.BlockSpec((3, tf, D), lambda j, i: (0, j, 0))],
                      out_shape=[jax.ShapeDtypeStruct((3, F, D), BF16)],
                      args=[da, db, t, n, dy], scratch=[pltpu.VMEM((3, tf, D), F32)], rider=rider)
    return outs[0], got


def _proj_fwd(h, g, w_inp, tm):
    S, D = h.shape

    def body(h_ref, g_ref, w_ref, pu_ref, ph_ref, n_ref):
        hv = h_ref[...]
        n = (hv * _rms(hv) * g_ref[...]).astype(BF16)
        n_ref[...] = n
        pu_ref[...] = _dot_nt(n, w_ref[0:PU, :])
        ph_ref[...] = _dot_nt(n, w_ref[PU:PP, :])

    cur = lambda w: pl.BlockSpec((tm, w), lambda i: (i, 0))
    return pl.pallas_call(
        body, name="proj_fwd", grid=(S // tm,),
        in_specs=[cur(D), pl.BlockSpec((1, D), lambda i: (0, 0)), pl.BlockSpec((PP, D), lambda i: (0, 0))],
        out_specs=[cur(PU), cur(PH), cur(D)],
        out_shape=[jax.ShapeDtypeStruct((S, PU), F32), jax.ShapeDtypeStruct((S, PH), F32),
                   jax.ShapeDtypeStruct((S, D), BF16)],
        compiler_params=_cp(),
    )(h, g, w_inp)


def _glu(u):
    return u[:, :CONV_CH] * _sigmoid(u[:, CONV_CH:2 * CONV_CH])


def _shifted_copies(ext8):
    n = ext8.shape[1]
    for b in range(1, 8):
        ext8[b, 0:n - 8, :] = ext8[0, b:b + n - 8, :]


def _window(ext8, off, rows, r0=0):
    return ext8[off % 8, pl.ds(r0 + (off - off % 8), rows), :]


def _layer_norm_stats(yc):
    mu = jnp.mean(yc, axis=-1, keepdims=True)
    d = yc - mu
    rstd = lax.rsqrt(jnp.mean(d * d, axis=-1, keepdims=True) + EPS)
    return d * rstd, rstd


def _mem_kv_fwd(mem, g, w_mkvp, gk):
    M, D = mem.shape
    W = SLOT * N_MEMH

    def body(mem_ref, g_ref, w_ref, gk_ref, nm_ref, raw_ref, mk_ref, mv_ref):
        mv_ = mem_ref[...]
        nm = (mv_ * _rms(mv_) * g_ref[...]).astype(BF16)
        nm_ref[...] = nm
        raw = _dot_nn(nm, w_ref[...])
        raw_ref[...] = raw
        for hh in range(N_MEMH):
            sl = slice(SLOT * hh, SLOT * (hh + 1))
            mk_ref[:, sl] = _slot_norm(raw[:, sl], gk_ref[...]).astype(BF16)
        mv_ref[...] = raw[:, W:].astype(BF16)

    sd = jax.ShapeDtypeStruct
    return pl.pallas_call(
        body, name="mem_kv_fwd",
        out_shape=[sd((M, D), BF16), sd((M, 2 * W), F32), sd((M, W), BF16), sd((M, W), BF16)],
        compiler_params=_cp(),
    )(mem, g, w_mkvp, gk)


def _mem_kv_bwd(dmk, dmv, raw, nm, mem, g, w_mkvp, gk):
    M, D = mem.shape
    W = SLOT * N_MEMH

    def body(dmk_ref, dmv_ref, raw_ref, nm_ref, mem_ref, g_ref, w_ref, gk_ref, dw_ref, dg_ref, dgk_ref, draw):
        dgk = jnp.zeros((1, SLOT), F32)
        for hh in range(N_MEMH):
            sl = slice(SLOT * hh, SLOT * (hh + 1))
            dx, prod = _slot_norm_bwd(dmk_ref[:, sl], raw_ref[:, sl], gk_ref[...])
            draw[:, sl] = dx.astype(BF16)
            dgk = dgk + _colsum(prod)
        dgk_ref[...] = dgk
        draw[:, W:] = dmv_ref[...].astype(BF16)
        dr = draw[...]
        dw_ref[...] = _dot_tn(nm_ref[...], dr)
        dnm = _dot_nt(dr, w_ref[...])
        mv_ = mem_ref[...]
        dg_ref[...] = _colsum(dnm * (mv_ * _rms(mv_)))

    sd = jax.ShapeDtypeStruct
    return pl.pallas_call(
        body, name="mem_kv_bwd",
        out_shape=[sd((D, 2 * W), F32), sd((1, D), F32), sd((1, SLOT), F32)],
        scratch_shapes=[pltpu.VMEM((M, 2 * W), BF16)],
        compiler_params=_cp(),
    )(dmk, dmv, raw, nm, mem, g, w_mkvp, gk)


def _mixer_fwd(pu, ph, h, cosT, sinT, conv_w, conv_b, ln_g, ln_b, gq, gk, sinks, mk, mv, gqm, w_outp, tm, rider=None):
    S, D = h.shape
    M = mk.shape[0]
    nb = tm // BLK
    nblocks = S // BLK

    def body(pu_ref, pup_ref, p_ref, ph_ref, h_ref, cos_ref, cosh_ref, sin_ref, sinh_ref, cw_ref, cb_ref,
             lg_ref, lb_ref, gq_ref, gk_ref, sink_ref, mk_ref, mv_ref, gqm_ref, wo_ref,
             h2_ref, y_ref, yc_ref, lse_ref, ext, y_scr):
        i = pl.program_id(0)
        not_first = (i > 0).astype(F32)
        lane = _lane(tm)
        lane_e = _lane(tm + BLK)

        ext[0, 0:HALO, :] = _glu(pup_ref[...]) * not_first
        ext[0, HALO:HALO + tm, :] = _glu(pu_ref[...])
        _shifted_copies(ext)

        def rows_chunk(r, carry):
            r0 = pl.multiple_of(r * CONV_ROWS, CONV_ROWS)
            yc = jnp.zeros((CONV_ROWS, CONV_CH), F32) + cb_ref[...]
            for k in range(CONV_WIDTH):
                yc = yc + cw_ref[k:k + 1, :] * _window(ext, HALO - (CONV_WIDTH - 1) + k, CONV_ROWS, r0)
            yc_ref[pl.ds(r0, CONV_ROWS), :] = yc
            z, _ = _layer_norm_stats(yc)
            ln = z * lg_ref[...] + lb_ref[...]
            y_scr[pl.ds(r0, CONV_ROWS), 0:CONV_CH] = (ln * _sigmoid(ln)).astype(BF16)
            return carry

        lax.fori_loop(0, tm // CONV_ROWS, rows_chunk, 0)

        cos_e = jnp.concatenate([cosh_ref[...], cos_ref[...]], axis=0)
        sin_e = jnp.concatenate([sinh_ref[...], sin_ref[...]], axis=0)
        qi = lax.broadcasted_iota(jnp.int32, (GROUP * BLK, 2 * BLK), 0) & (BLK - 1)
        kj = lax.broadcasted_iota(jnp.int32, (GROUP * BLK, 2 * BLK), 1)
        band = (kj > qi) & (kj <= qi + BLK)
        band0 = band & ((kj >= BLK) | (i > 0))
        lse = jnp.zeros((tm, SLOT), F32)
        k_pair = jnp.concatenate([ph_ref[:, KO:KO + SLOT], p_ref[:, KO:KO + SLOT]], axis=0)
        k_pair = _pair_fwd(k_pair, gk_ref[...], cos_e, sin_e, lane_e)
        v_pair = jnp.concatenate([ph_ref[:, VO:VO + SLOT], p_ref[:, VO:VO + SLOT]], axis=0)
        k_e = [_lo(k_pair, kvh, lane_e).astype(BF16) for kvh in range(N_KV)]
        v_e = [_lo(v_pair, kvh, lane_e).astype(BF16) for kvh in range(N_KV)]
        q_lo = []
        for j in range(N_Q // 2):
            q_pair = _pair_fwd(p_ref[:, QO + SLOT * j:QO + SLOT * (j + 1)], gq_ref[...],
                               cos_ref[...], sin_ref[...], lane)
            q_lo += [_lo(q_pair, 0, lane).astype(BF16), _lo(q_pair, 1, lane).astype(BF16)]
        outs = [[] for _ in range(N_Q)]
        lses = [[] for _ in range(N_Q)]
        for kvh in range(N_KV):
            hs = [GROUP * kvh + gi for gi in range(GROUP)]
            sink3 = jnp.concatenate([jnp.full((BLK, 1), sink_ref[h], F32) for h in hs], axis=0)
            for m in range(nb):
                rows = slice(BLK * m, BLK * (m + 1))
                win = slice(BLK * m, BLK * (m + 2))
                q3 = jnp.concatenate([q_lo[h][rows] for h in hs], axis=0)
                s = _dot_nt(q3, k_e[kvh][win]) * SCALE
                s = jnp.where(band0 if m == 0 else band, s, NEG)
                mx = jnp.maximum(jnp.max(s, axis=-1, keepdims=True), sink3)
                e = jnp.exp(s - mx)
                den = jnp.sum(e, axis=-1, keepdims=True) + jnp.exp(sink3 - mx)
                o3 = _dot_nn((e / den).astype(BF16), v_e[kvh][win])
                l3 = mx + jnp.log(den)
                for gi, h in enumerate(hs):
                    outs[h].append(o3[BLK * gi:BLK * (gi + 1)])
                    lses[h].append(l3[BLK * gi:BLK * (gi + 1)])
        for h in range(N_Q):
            lse = jnp.where(lane == h, jnp.concatenate(lses[h], axis=0), lse)
        for j in range(N_Q // 2):
            y_scr[:, YS + SLOT * j:YS + SLOT * (j + 1)] = _pack(
                jnp.concatenate(outs[2 * j], axis=0), jnp.concatenate(outs[2 * j + 1], axis=0), lane).astype(BF16)

        heads = []
        for hm in range(N_MEMH):
            ms = slice(SLOT * hm, SLOT * (hm + 1))
            if hm % 2 == 0:
                qm_pair = _pair_fwd(p_ref[:, MO + SLOT * (hm // 2):MO + SLOT * (hm // 2 + 1)], gqm_ref[...],
                                    None, None, lane)
            s = _dot_nt(_lo(qm_pair, hm % 2, lane).astype(BF16), mk_ref[:, ms]) * SCALE
            mx = jnp.max(s, axis=-1, keepdims=True)
            e = jnp.exp(s - mx)
            den = jnp.sum(e, axis=-1, keepdims=True)
            heads.append(_dot_nn((e / den).astype(BF16), mv_ref[:, ms]))
            lse = jnp.where(lane == N_Q + hm, mx + jnp.log(den), lse)
            if hm % 2 == 1:
                y_scr[:, YM + SLOT * (hm // 2):YM + SLOT * (hm // 2 + 1)] = _pack(heads[-2], heads[-1], lane).astype(BF16)
        lse_ref[...] = lse.T[0:STAT_ROWS, :]

        yv = y_scr[...]
        y_ref[...] = yv
        h2_ref[...] = h_ref[...] + _dot_nn(yv, wo_ref[...])

    cur = lambda w: pl.BlockSpec((tm, w), lambda i: (i, 0))
    prev = lambda w: pl.BlockSpec((BLK, w), lambda i: (jnp.maximum(i * nb - 1, 0), 0))
    full = lambda a: pl.BlockSpec(a.shape, lambda i: (0,) * a.ndim)
    sd = jax.ShapeDtypeStruct
    prev32 = pl.BlockSpec((HALO, PU), lambda i: (jnp.maximum(i * (tm // HALO) - 1, 0), 0))
    return _call(
        body, name="mixer_fwd", grid=(S // tm,),
        in_specs=[cur(PU), prev32, cur(PH), prev(PH), cur(D), cur(SLOT), prev(SLOT), cur(SLOT), prev(SLOT),
                  full(conv_w), full(conv_b), full(ln_g), full(ln_b), full(gq), full(gk),
                  pl.BlockSpec(memory_space=pltpu.SMEM), full(mk), full(mv), full(gqm), full(w_outp)],
        out_specs=[cur(D), cur(YP), cur(CONV_CH), pl.BlockSpec((STAT_ROWS, tm), lambda i: (0, i))],
        out_shape=[sd((S, D), F32), sd((S, YP), BF16), sd((S, CONV_CH), F32), sd((STAT_ROWS, S), F32)],
        args=[pu, pu, ph, ph, h, cosT, cosT, sinT, sinT, conv_w, conv_b, ln_g, ln_b, gq, gk, sinks, mk, mv, gqm,
              w_outp],
        scratch=[pltpu.VMEM((8, tm + HALO, CONV_CH), F32), pltpu.VMEM((tm, YP), BF16)], rider=rider)


def _outproj_bwd(dh2, y, yc, ln_g, ln_b, w_outp, tm):
    S, D = dh2.shape

    def body(dh_ref, y_ref, yc_ref, lg_ref, lb_ref, wo_ref, dyc_ref, do_ref, del_ref, dwo_ref, dlg_ref, dlb_ref):
        i = pl.program_id(0)

        @pl.when(i == 0)
        def _():
            dwo_ref[...] = jnp.zeros_like(dwo_ref)
            dlg_ref[...] = jnp.zeros_like(dlg_ref)
            dlb_ref[...] = jnp.zeros_like(dlb_ref)

        dhb = dh_ref[...].astype(BF16)
        yv = y_ref[...]
        dy = _dot_nt(dhb, wo_ref[...])
        dwo_ref[...] += _dot_tn(yv, dhb)

        z, rstd = _layer_norm_stats(yc_ref[...])
        ln = z * lg_ref[...] + lb_ref[...]
        sg = _sigmoid(ln)
        dln = dy[:, 0:CONV_CH] * (sg * (1.0 + ln * (1.0 - sg)))
        dlg_ref[...] += _colsum(dln * z)
        dlb_ref[...] += _colsum(dln)
        dz = dln * lg_ref[...]
        dyc_ref[...] = rstd * (dz - jnp.mean(dz, axis=-1, keepdims=True)
                               - z * jnp.mean(dz * z, axis=-1, keepdims=True))
        do_ref[...] = dy[:, CONV_CH:].astype(BF16)

        lane = _lane(tm)
        delta = jnp.zeros((tm, SLOT), F32)
        for j in range(NH // 2):
            sl = slice(YS + SLOT * j, YS + SLOT * (j + 1))
            prod = dy[:, sl] * yv[:, sl].astype(F32)
            lo = jnp.sum(jnp.where(lane < HEAD_DIM, prod, 0.0), axis=-1, keepdims=True)
            hi = jnp.sum(jnp.where(lane < HEAD_DIM, 0.0, prod), axis=-1, keepdims=True)
            delta = jnp.where(lane == 2 * j, lo, jnp.where(lane == 2 * j + 1, hi, delta))
        del_ref[...] = delta.T[0:STAT_ROWS, :]

    cur = lambda w: pl.BlockSpec((tm, w), lambda i: (i, 0))
    full = lambda a: pl.BlockSpec(a.shape, lambda i: (0,) * a.ndim)
    sd = jax.ShapeDtypeStruct
    return pl.pallas_call(
        body, name="outproj_bwd", grid=(S // tm,),
        in_specs=[cur(D), cur(YP), cur(CONV_CH), full(ln_g), full(ln_b), full(w_outp)],
        out_specs=[cur(CONV_CH), cur(YH), pl.BlockSpec((STAT_ROWS, tm), lambda i: (0, i)),
                   pl.BlockSpec((YP, D), lambda i: (0, 0)),
                   pl.BlockSpec((1, CONV_CH), lambda i: (0, 0)), pl.BlockSpec((1, CONV_CH), lambda i: (0, 0))],
        out_shape=[sd((S, CONV_CH), F32), sd((S, YH), BF16), sd((STAT_ROWS, S), F32), sd((YP, D), F32),
                   sd((1, CONV_CH), F32), sd((1, CONV_CH), F32)],
        compiler_params=_cp(),
    )(dh2, y, yc, ln_g, ln_b, w_outp)


def _conv_bwd(pu, dyc, conv_w, tm):
    S = pu.shape[0]
    nt = S // tm
    nh = tm // HALO

    def body(pu_ref, pup_ref, dy_ref, dyn_ref, cw_ref, dpu_ref, dcw_ref, dcb_ref, ext, ext2, dcw8):
        i = pl.program_id(0)

        @pl.when(i == 0)
        def _():
            dcw8[...] = jnp.zeros_like(dcw8)
            dcb_ref[...] = jnp.zeros_like(dcb_ref)

        not_first = (i > 0).astype(F32)
        not_last = (i < nt - 1).astype(F32)
        ext[0, 0:HALO, :] = _glu(pup_ref[...]) * not_first
        ext[0, HALO:HALO + tm, :] = _glu(pu_ref[...])
        _shifted_copies(ext)
        ext2[0, 0:tm, :] = dy_ref[...]
        ext2[0, tm:tm + HALO, :] = dyn_ref[...] * not_last
        _shifted_copies(ext2)
        dcb_ref[...] += _colsum(dy_ref[...])

        def rows_chunk(r, carry):
            r0 = pl.multiple_of(r * CONV_ROWS, CONV_ROWS)
            dyc_ = dy_ref[pl.ds(r0, CONV_ROWS), :]
            dyg = jnp.zeros((CONV_ROWS, CONV_CH), F32)
            for k in range(CONV_WIDTH):
                prod = dyc_ * _window(ext, HALO - (CONV_WIDTH - 1) + k, CONV_ROWS, r0)
                dcw8[k] += jnp.sum(prod.reshape(CONV_ROWS // 8, 8, CONV_CH), axis=0)
                dyg = dyg + cw_ref[k:k + 1, :] * _window(ext2, CONV_WIDTH - 1 - k, CONV_ROWS, r0)
            u = pu_ref[pl.ds(r0, CONV_ROWS), :]
            a_, sg = u[:, :CONV_CH], _sigmoid(u[:, CONV_CH:])
            dpu_ref[pl.ds(r0, CONV_ROWS), 0:CONV_CH] = (dyg * sg).astype(BF16)
            dpu_ref[pl.ds(r0, CONV_ROWS), CONV_CH:PU] = (dyg * a_ * sg * (1.0 - sg)).astype(BF16)
            return carry

        lax.fori_loop(0, tm // CONV_ROWS, rows_chunk, 0)

        @pl.when(i == nt - 1)
        def _():
            dcw_ref[...] = jnp.sum(dcw8[...], axis=1)

    cur = lambda w: pl.BlockSpec((tm, w), lambda i: (i, 0))
    prev = lambda w: pl.BlockSpec((HALO, w), lambda i: (jnp.maximum(i * nh - 1, 0), 0))
    nxt = lambda w: pl.BlockSpec((HALO, w), lambda i: (jnp.minimum((i + 1) * nh, S // HALO - 1), 0))
    acc = lambda r, w: pl.BlockSpec((r, w), lambda i: (0, 0))
    sd = jax.ShapeDtypeStruct
    return pl.pallas_call(
        body, name="conv_bwd", grid=(nt,),
        in_specs=[cur(PU), prev(PU), cur(CONV_CH), nxt(CONV_CH), acc(32, CONV_CH)],
        out_specs=[cur(PU), acc(32, CONV_CH), acc(1, CONV_CH)],
        out_shape=[sd((S, PU), BF16), sd((32, CONV_CH), F32), sd((1, CONV_CH), F32)],
        scratch_shapes=[pltpu.VMEM((8, tm + HALO, CONV_CH), F32), pltpu.VMEM((8, tm + HALO, CONV_CH), F32),
                        pltpu.VMEM((32, 8, CONV_CH), F32)],
        compiler_params=_cp(),
    )(pu, pu, dyc, dyc, conv_w)


def _attn_bwd(p, do, lse, delta, cosT, sinT, gq, gk, sinks, mk, mv, gqm, tm, rider=None):
    S = p.shape[0]
    M = mk.shape[0]
    nb = tm // BLK
    nt = S // tm
    nblocks = S // BLK
    W = SLOT * N_MEMH

    def body(p_ref, pp_ref, pn_ref, dy_ref, dyn_ref, lse_ref, lsen_ref, del_ref, deln_ref,
             cos_ref, cosp_ref, cosn_ref, sin_ref, sinp_ref, sinn_ref,
             gq_ref, gk_ref, sink_ref, mk_ref, mv_ref, gqm_ref,
             dp_ref, dgq_ref, dgk_ref, dgqm_ref, dsink_ref, dmk_ref, dmv_ref):
        i = pl.program_id(0)

        @pl.when(i == 0)
        def _():
            for r in (dgq_ref, dgk_ref, dgqm_ref, dsink_ref, dmk_ref, dmv_ref):
                r[...] = jnp.zeros_like(r)

        lane = _lane(tm)
        lane_e = _lane(tm + BLK)

        cos_k = jnp.concatenate([cosp_ref[...], cos_ref[...]], axis=0)
        sin_k = jnp.concatenate([sinp_ref[...], sin_ref[...]], axis=0)
        cos_q = jnp.concatenate([cos_ref[...], cosn_ref[...]], axis=0)
        sin_q = jnp.concatenate([sin_ref[...], sinn_ref[...]], axis=0)
        lse_e = jnp.concatenate([lse_ref[...], lsen_ref[...]], axis=1)
        del_e = jnp.concatenate([del_ref[...], deln_ref[...]], axis=1)
        kj = lax.broadcasted_iota(jnp.int32, (BLK, GROUP * BLK), 0)
        qi = lax.broadcasted_iota(jnp.int32, (BLK, GROUP * BLK), 1) & (BLK - 1)
        diag = kj <= qi
        offd = kj > qi
        dgq = jnp.zeros((1, SLOT), F32)
        dgk = jnp.zeros((1, SLOT), F32)
        dsink = jnp.zeros((1, SLOT), F32)
        lane1 = lax.broadcasted_iota(jnp.int32, (1, SLOT), 1)
        k_pair = jnp.concatenate([pp_ref[:, KO:KO + SLOT], p_ref[:, KO:KO + SLOT]], axis=0)
        k_pair = _pair_fwd(k_pair, gk_ref[...], cos_k, sin_k, lane_e)
        v_pair = jnp.concatenate([pp_ref[:, VO:VO + SLOT], p_ref[:, VO:VO + SLOT]], axis=0)
        k_e = [_lo(k_pair, kvh, lane_e).astype(BF16) for kvh in range(N_KV)]
        v_e = [_lo(v_pair, kvh, lane_e).astype(BF16) for kvh in range(N_KV)]
        dk = [[jnp.zeros((BLK, SLOT), F32) for _ in range(nb)] for _ in range(N_KV)]
        dv = [[jnp.zeros((BLK, SLOT), F32) for _ in range(nb)] for _ in range(N_KV)]
        q_e, do_e = [], []
        for j in range(N_Q // 2):
            js = slice(SLOT * j, SLOT * (j + 1))
            q_pair = _pair_fwd(jnp.concatenate([p_ref[:, js], pn_ref[:, js]], axis=0), gq_ref[...],
                               cos_q, sin_q, lane_e)
            do_pair = jnp.concatenate([dy_ref[:, js], dyn_ref[:, js]], axis=0).astype(F32)
            for half in range(2):
                q_e.append(_lo(q_pair, half, lane_e).astype(BF16))
                do_e.append(_lo(do_pair, half, lane_e).astype(BF16))
        dq_heads = [None] * N_Q
        for kvh in range(N_KV):
            hs = [GROUP * kvh + gi for gi in range(GROUP)]
            dq3 = [None] * nb
            for m in range(nb + 1):
                rows = slice(BLK * m, BLK * (m + 1))
                q3 = jnp.concatenate([q_e[h][rows] for h in hs], axis=0)
                do3 = jnp.concatenate([do_e[h][rows] for h in hs], axis=0)
                lb3 = jnp.concatenate([lse_e[h:h + 1, rows] for h in hs], axis=1)
                db3 = jnp.concatenate([del_e[h:h + 1, rows] for h in hs], axis=1)
                for n in (m - 1, m):
                    if n == nb:
                        continue
                    krows = slice(BLK * (n + 1), BLK * (n + 2))
                    kb, vb = k_e[kvh][krows], v_e[kvh][krows]
                    s = _dot_nt(kb, q3) * SCALE
                    mask = diag if n == m else offd
                    if n == -1:
                        mask = mask & (i > 0)
                    if m == nb:
                        mask = mask & (i < nt - 1)
                    prob = jnp.where(mask, jnp.exp(jnp.where(mask, s - lb3, NEG)), 0.0)
                    dpb = _dot_nt(vb, do3)
                    ds = (prob * (dpb - db3) * SCALE).astype(BF16)
                    if m < nb:
                        dqc = _dot_tn(ds, kb)
                        dq3[m] = dqc if dq3[m] is None else dq3[m] + dqc
                    if n >= 0:
                        dk[kvh][n] = dk[kvh][n] + _dot_nn(ds, q3)
                        dv[kvh][n] = dv[kvh][n] + _dot_nn(prob.astype(BF16), do3)
            for gi, h in enumerate(hs):
                dq_heads[h] = jnp.concatenate([dq3[m][BLK * gi:BLK * (gi + 1)] for m in range(nb)], axis=0)
                psink = jnp.exp(sink_ref[h] - lse_e[h:h + 1, 0:tm])
                dsink = dsink + jnp.where(
                    lane1 == h, -jnp.sum(psink * del_e[h:h + 1, 0:tm], axis=-1, keepdims=True), 0.0)
        for j in range(N_Q // 2):
            js = slice(SLOT * j, SLOT * (j + 1))
            dqr, prod = _pair_bwd(_pack(dq_heads[2 * j], dq_heads[2 * j + 1], lane), p_ref[:, js], gq_ref[...],
                                  cos_ref[...], sin_ref[...], lane)
            dp_ref[:, js] = dqr.astype(BF16)
            dgq = dgq + _colsum(prod)
        dk_pair = _pack(jnp.concatenate(dk[0], axis=0), jnp.concatenate(dk[1], axis=0), lane)
        dkr, prod = _pair_bwd(dk_pair, p_ref[:, KO:KO + SLOT], gk_ref[...], cos_ref[...], sin_ref[...], lane)
        dp_ref[:, KO:KO + SLOT] = dkr.astype(BF16)
        dp_ref[:, VO:VO + SLOT] = _pack(jnp.concatenate(dv[0], axis=0), jnp.concatenate(dv[1], axis=0),
                                        lane).astype(BF16)
        dgq_ref[...] += dgq
        dgk_ref[...] += _colsum(prod)
        dsink_ref[...] += dsink

        dgqm = jnp.zeros((1, SLOT), F32)
        dq_heads = []
        for hm in range(N_MEMH):
            ms = slice(SLOT * hm, SLOT * (hm + 1))
            js = slice(MO + SLOT * (hm // 2), MO + SLOT * (hm // 2 + 1))
            os_ = slice(SLOT * ((N_Q + hm) // 2), SLOT * ((N_Q + hm) // 2 + 1))
            if hm % 2 == 0:
                qm_pair = _pair_fwd(p_ref[:, js], gqm_ref[...], None, None, lane)
                do_pair = dy_ref[:, os_].astype(F32)
            qm = _lo(qm_pair, hm % 2, lane).astype(BF16)
            dob = _lo(do_pair, hm % 2, lane).astype(BF16)
            kb, vb = mk_ref[:, ms], mv_ref[:, ms]
            s = _dot_nt(kb, qm) * SCALE
            prob = jnp.exp(s - lse_ref[N_Q + hm:N_Q + hm + 1, :])
            dpb = _dot_nt(vb, dob)
            ds = (prob * (dpb - del_ref[N_Q + hm:N_Q + hm + 1, :]) * SCALE).astype(BF16)
            dq_heads.append(_dot_tn(ds, kb))
            dmk_ref[:, ms] += _dot_nn(ds, qm)
            dmv_ref[:, ms] += _dot_nn(prob.astype(BF16), dob)
            if hm % 2 == 1:
                dqr, prod = _pair_bwd(_pack(dq_heads[-2], dq_heads[-1], lane), p_ref[:, js], gqm_ref[...],
                                      None, None, lane)
                dp_ref[:, js] = dqr.astype(BF16)
                dgqm = dgqm + _colsum(prod)
        dgqm_ref[...] += dgqm

    cur = lambda w: pl.BlockSpec((tm, w), lambda i: (i, 0))
    prev = lambda w: pl.BlockSpec((BLK, w), lambda i: (jnp.maximum(i * nb - 1, 0), 0))
    nxt = lambda w: pl.BlockSpec((BLK, w), lambda i: (jnp.minimum((i + 1) * nb, nblocks - 1), 0))
    full = lambda a: pl.BlockSpec(a.shape, lambda i: (0,) * a.ndim)
    acc = lambda r, w: pl.BlockSpec((r, w), lambda i: (0, 0))
    sd = jax.ShapeDtypeStruct
    stat = pl.BlockSpec((STAT_ROWS, tm), lambda i: (0, i))
    stat_n = pl.BlockSpec((STAT_ROWS, BLK), lambda i: (0, jnp.minimum((i + 1) * nb, nblocks - 1)))
    return _call(
        body, name="attn_bwd", grid=(nt,),
        in_specs=[cur(PH), prev(PH), nxt(PH), cur(YH), nxt(YH), stat, stat_n, stat, stat_n,
                  cur(SLOT), prev(SLOT), nxt(SLOT), cur(SLOT), prev(SLOT), nxt(SLOT),
                  full(gq), full(gk), pl.BlockSpec(memory_space=pltpu.SMEM), full(mk), full(mv), full(gqm)],
        out_specs=[cur(PH), acc(1, SLOT), acc(1, SLOT), acc(1, SLOT), acc(1, SLOT), acc(M, W), acc(M, W)],
        out_shape=[sd((S, PH), BF16), sd((1, SLOT), F32), sd((1, SLOT), F32), sd((1, SLOT), F32),
                   sd((1, SLOT), F32), sd((M, W), F32), sd((M, W), F32)],
        args=[p, p, p, do, do, lse, lse, delta, delta, cosT, cosT, cosT, sinT, sinT, sinT,
              gq, gk, sinks, mk, mv, gqm],
        rider=rider)


def _proj_bwd(dpu, dph, h, dh2, g, n, w_inp, tm):
    S, D = h.shape

    def body(dpu_ref, dph_ref, h_ref, dh2_ref, g_ref, n_ref, w_ref, dh_ref, dg_ref, dw_ref):
        i = pl.program_id(0)

        @pl.when(i == 0)
        def _():
            dg_ref[...] = jnp.zeros_like(dg_ref)
            dw_ref[...] = jnp.zeros_like(dw_ref)

        dpu, dph, nv = dpu_ref[...], dph_ref[...], n_ref[...]
        dn = _dot_nn(dpu, w_ref[0:PU, :]) + _dot_nn(dph, w_ref[PU:PP, :])
        dw_ref[0:PU, :] += _dot_tn(dpu, nv)
        dw_ref[PU:PP, :] += _dot_tn(dph, nv)
        hv = h_ref[...]
        dx, dgrow = _rms_bwd(dn, hv, _rms(hv), g_ref[...])
        dh_ref[...] = dh2_ref[...] + dx
        dg_ref[...] += _colsum(dgrow)

    cur = lambda w: pl.BlockSpec((tm, w), lambda i: (i, 0))
    sd = jax.ShapeDtypeStruct
    return pl.pallas_call(
        body, name="proj_bwd", grid=(S // tm,),
        in_specs=[cur(PU), cur(PH), cur(D), cur(D), pl.BlockSpec((1, D), lambda i: (0, 0)), cur(D),
                  pl.BlockSpec((PP, D), lambda i: (0, 0))],
        out_specs=[cur(D), pl.BlockSpec((1, D), lambda i: (0, 0)), pl.BlockSpec((PP, D), lambda i: (0, 0))],
        out_shape=[sd((S, D), F32), sd((1, D), F32), sd((PP, D), F32)],
        compiler_params=_cp(),
    )(dpu, dph, h, dh2, g, n, w_inp)


def _norm_bwd(dxn, h, g, tm):
    S, D = h.shape

    def body(d_ref, h_ref, g_ref, dh_ref, dg_ref):
        @pl.when(pl.program_id(0) == 0)
        def _():
            dg_ref[...] = jnp.zeros_like(dg_ref)

        hv = h_ref[...]
        dx, dgrow = _rms_bwd(d_ref[...], hv, _rms(hv), g_ref[...])
        dh_ref[...] = dx
        dg_ref[...] += _colsum(dgrow)

    cur = pl.BlockSpec((tm, D), lambda i: (i, 0))
    vec = pl.BlockSpec((1, D), lambda i: (0, 0))
    return pl.pallas_call(
        body, name="norm_bwd", grid=(S // tm,), in_specs=[cur, cur, vec], out_specs=[cur, vec],
        out_shape=[jax.ShapeDtypeStruct((S, D), F32), jax.ShapeDtypeStruct((1, D), F32)],
        compiler_params=_cp(),
    )(dxn, h, g)


def _loss_bwd(xn, h, g, target, tm):
    S, D = h.shape

    def body(y_ref, h_ref, g_ref, t_ref, loss_ref, dh_ref, dg_ref):
        @pl.when(pl.program_id(0) == 0)
        def _():
            dg_ref[...] = jnp.zeros_like(dg_ref)
            loss_ref[...] = jnp.zeros_like(loss_ref)

        err = y_ref[...] - t_ref[...]
        part = jnp.sum(jnp.mean(err * err, axis=-1, keepdims=True), axis=0, keepdims=True)
        loss_ref[...] += 0.5 * part
        hv = h_ref[...]
        dx, dgrow = _rms_bwd(err * (1.0 / D), hv, _rms(hv), g_ref[...])
        dh_ref[...] = dx
        dg_ref[...] += _colsum(dgrow)

    cur = pl.BlockSpec((tm, D), lambda i: (i, 0))
    vec = pl.BlockSpec((1, D), lambda i: (0, 0))
    return pl.pallas_call(
        body, name="loss_bwd", grid=(S // tm,), in_specs=[cur, cur, vec, cur],
        out_specs=[pl.BlockSpec((1, SLOT), lambda i: (0, 0)), cur, vec],
        out_shape=[jax.ShapeDtypeStruct((1, SLOT), F32), jax.ShapeDtypeStruct((S, D), F32),
                   jax.ShapeDtypeStruct((1, D), F32)],
        compiler_params=_cp(),
    )(xn, h, g, target)


def _all_gather_small(buf):
    _, R, W = buf.shape

    def body(in_ref, out_ref, ssem, rsem, lsem):
        x, y, c, _ = _place()
        me = 4 * x + 2 * y + c
        local = pltpu.make_async_copy(in_ref, out_ref.at[pl.ds(me, 1)], lsem)
        local.start()

        def copy(k, block):
            fx, fy, fc = (k >> 2) & 1, (k >> 1) & 1, k & 1
            peer = (x ^ fx, y ^ fy, c ^ fc)
            return pltpu.make_async_remote_copy(
                src_ref=in_ref, dst_ref=out_ref.at[pl.ds(block, 1)], send_sem=ssem.at[k - 1],
                recv_sem=rsem.at[k - 1], device_id=peer, device_id_type=MESH)

        sends = [copy(k, me) for k in range(1, 8)]
        for cp in sends:
            cp.start()
        for k in range(1, 8):
            copy(k, me ^ k).wait_recv()
        for cp in sends:
            cp.wait_send()
        local.wait()

    hbm = pl.BlockSpec(memory_space=pl.ANY)
    return pl.pallas_call(
        body, name="all_gather_small", in_specs=[hbm], out_specs=hbm,
        out_shape=jax.ShapeDtypeStruct((8, R, W), buf.dtype),
        scratch_shapes=[pltpu.SemaphoreType.DMA((7,)), pltpu.SemaphoreType.DMA((7,)), pltpu.SemaphoreType.DMA],
    )(buf)


def _row_tile(n, cap=1024):
    for t in range(min(n, cap) // 8 * 8, 7, -8):
        if n % t == 0:
            return t
    return n


def _sum4(own, recv):
    n, rows, D = own.shape
    tr = _row_tile(rows)

    def body(o_ref, r0_ref, r1_ref, r2_ref, out_ref):
        out_ref[...] = ((o_ref[...].astype(F32) + r0_ref[...].astype(F32)) + r1_ref[...].astype(F32)) \
            + r2_ref[...].astype(F32)

    def rspec(p):
        return pl.BlockSpec((None, None, None, tr, D), lambda k, i, p=p: (p, k, 0, i, 0))

    blk = pl.BlockSpec((None, tr, D), lambda k, i: (k, i, 0))
    return pl.pallas_call(
        body, name="sum4", grid=(n, rows // tr),
        in_specs=[blk, rspec(0), rspec(1), rspec(2)], out_specs=blk,
        out_shape=jax.ShapeDtypeStruct((n, rows, D), F32),
    )(own, recv, recv, recv)


def _adam_math(w, g, m, v):
    m = ADAM_B1 * m + (1.0 - ADAM_B1) * g
    v = ADAM_B2 * v + (1.0 - ADAM_B2) * (g * g)
    m_hat = m / (1.0 - ADAM_B1 ** ADAM_STEP)
    v_hat = v / (1.0 - ADAM_B2 ** ADAM_STEP)
    delta = -ADAM_LR * (m_hat / (jnp.sqrt(v_hat) + ADAM_EPS) + ADAM_WD * w)
    return delta, m, v


def _adam_fused(w, m, v, parts, theirs, sel, row0, nrows, rider=None):
    L, R, D = w.shape
    assert R == nrows and parts[0].shape[2] == D
    t = _row_tile(nrows if row0 == 0 else _gcd(row0, nrows), 512)

    def gspec(k):
        return pl.BlockSpec((None, t, D), lambda l, c: (sel, row0 // t + jnp.where(l == k, c, 0), 0))

    def body(*refs):
        w_ref, m_ref, v_ref = refs[:3]
        p_refs, q_refs = refs[3:3 + L], refs[3 + L:3 + 2 * L]
        g_ref, d_ref, nm_ref, nv_ref, g_scr = refs[3 + 2 * L:]
        l = pl.program_id(0)
        for k in range(L):
            @pl.when(l == k)
            def _(k=k):
                g_scr[...] = p_refs[k][...] + q_refs[k][...]

        g = g_scr[...]
        g_ref[...] = g
        d, m_, v_ = _adam_math(w_ref[...], g, m_ref[...], v_ref[...])
        d_ref[...] = d
        nm_ref[...] = m_
        nv_ref[...] = v_

    blk = pl.BlockSpec((None, t, D), lambda l, c: (l, c, 0))
    return _call(body, name="adam_fused", grid=(L, R // t),
                 in_specs=[blk] * 3 + [gspec(k) for k in range(L)] * 2, out_specs=[blk] * 4,
                 out_shape=[jax.ShapeDtypeStruct((L, R, D), F32)] * 4,
                 args=[w, m, v, *parts, *theirs], scratch=[pltpu.VMEM((t, D), F32)], rider=rider)


def _gcd(a, b):
    while b:
        a, b = b, a % b
    return a


def _small_sum_adam(g8, w, m, v):
    _, R, W = g8.shape

    def body(g_ref, w_ref, m_ref, v_ref, go_ref, d_ref, nm_ref, nv_ref):
        g = g_ref[0]
        for k in range(1, 8):
            g = g + g_ref[k]
        go_ref[...] = g
        d, m_, v_ = _adam_math(w_ref[...], g, m_ref[...], v_ref[...])
        d_ref[...] = d
        nm_ref[...] = m_
        nv_ref[...] = v_

    return pl.pallas_call(body, name="small_sum_adam",
                          out_shape=[jax.ShapeDtypeStruct((R, W), F32)] * 4)(g8, w, m, v)


def _pad_vec(v):
    return jnp.pad(v, (0, SLOT - v.shape[0]))[None, :]


class _Pack:
    def __init__(self, shapes):
        self.shapes = shapes
        self.sizes = [int(functools.reduce(lambda a, b: a * b, s, 1)) for s in shapes]
        total = sum(self.sizes)
        self.rows = -(-total // (8 * SLOT)) * 8
        self.pad = self.rows * SLOT - total

    def pack(self, arrs):
        flat = jnp.concatenate([a.reshape(-1).astype(F32) for a in arrs] + [jnp.zeros((self.pad,), F32)])
        return flat.reshape(self.rows, SLOT)

    def unpack(self, buf):
        flat, out, o = buf.reshape(-1), [], 0
        for s, n in zip(self.shapes, self.sizes):
            out.append(flat[o:o + n].reshape(s))
            o += n
        return out


def kernel(x, mem, positions, ffn1_norm, ffn1_w1, ffn1_w3, ffn1_w2, mix_norm, w_in, conv_w, conv_b, conv_ln_g, conv_ln_b, swa_q_norm, swa_k_norm, swa_sinks, mem_norm, w_mem_kv, mem_q_norm, mem_k_norm, w_out, ffn2_norm, ffn2_w1, ffn2_w3, ffn2_w2, final_norm, loss_target, m_ffn1_norm, m_ffn1_w1, m_ffn1_w3, m_ffn1_w2, m_mix_norm, m_w_in, m_conv_w, m_conv_b, m_conv_ln_g, m_conv_ln_b, m_swa_q_norm, m_swa_k_norm, m_swa_sinks, m_mem_norm, m_w_mem_kv, m_mem_q_norm, m_mem_k_norm, m_w_out, m_ffn2_norm, m_ffn2_w1, m_ffn2_w3, m_ffn2_w2, m_final_norm, v_ffn1_norm, v_ffn1_w1, v_ffn1_w3, v_ffn1_w2, v_mix_norm, v_w_in, v_conv_w, v_conv_b, v_conv_ln_g, v_conv_ln_b, v_swa_q_norm, v_swa_k_norm, v_swa_sinks, v_mem_norm, v_w_mem_kv, v_mem_q_norm, v_mem_k_norm, v_w_out, v_ffn2_norm, v_ffn2_w1, v_ffn2_w3, v_ffn2_w2, v_final_norm):
    names = ['ffn1_norm', 'ffn1_w1', 'ffn1_w3', 'ffn1_w2', 'mix_norm', 'w_in', 'conv_w', 'conv_b', 'conv_ln_g',
             'conv_ln_b', 'swa_q_norm', 'swa_k_norm', 'swa_sinks', 'mem_norm', 'w_mem_kv', 'mem_q_norm',
             'mem_k_norm', 'w_out', 'ffn2_norm', 'ffn2_w1', 'ffn2_w3', 'ffn2_w2', 'final_norm']
    loc = locals()
    W = {n: loc[n] for n in names}
    M1 = {n: loc['m_' + n] for n in names}
    V1 = {n: loc['v_' + n] for n in names}

    S, D = x.shape[1], x.shape[2]
    L = ffn1_norm.shape[0]
    Fs = ffn1_w1.shape[2]
    F = 4 * Fs
    Mlen = mem.shape[1]
    cw_sh = conv_w.shape[2]
    tm = 512 if S >= 2048 else 256
    tf = 1408 if F % 1408 == 0 else 256
    tfw = 256
    tmw = 2048 if S >= 2048 else 256
    x0 = x[0]
    mem0 = mem[0]
    target = loss_target[0]
    my_chip = 2 * lax.axis_index("x") + lax.axis_index("y")

    mkv_rows = w_mem_kv.shape[1] * MEM_KV // D
    r_in, r_out = D_IN // 4, D_MIX // 4
    rm = r_in + r_out + mkv_rows

    shard = lambda w: w.astype(BF16).reshape((1, 1) + w.shape)

    groups = []
    for l in range(L):
        groups.append([shard(ffn1_w1[l].T), shard(ffn1_w3[l].T), shard(ffn1_w2[l])])
        groups.append([shard(w_in[l].T), shard(w_out[l]), shard(w_mem_kv[l].reshape(mkv_rows, D))])
        groups.append([shard(ffn2_w1[l].T), shard(ffn2_w3[l].T), shard(ffn2_w2[l])])
    gathered = [None] * len(groups)
    cw_rows = -(-(L * CONV_WIDTH) // 8) * 8
    cw_pad = jnp.pad(conv_w.reshape(L * CONV_WIDTH, cw_sh), ((0, cw_rows - L * CONV_WIDTH), (0, SLOT - cw_sh)))
    *gathered[0], cw_g = _run_rider(_GatherHalves(groups[0] + [cw_pad.reshape(1, 1, cw_rows, SLOT)]),
                                    "all_gather_first")
    conv_wF = cw_g[0, :, :L * CONV_WIDTH, :cw_sh].reshape(4, L, CONV_WIDTH, cw_sh)
    conv_wF = jnp.moveaxis(conv_wF, 0, 2).reshape(L, CONV_WIDTH, 4 * cw_sh)
    conv_wP = jnp.pad(conv_wF, ((0, 0), (0, 32 - CONV_WIDTH), (0, 0)))

    def gather_rider(j):
        want = [k for k in [j + 1] if k < len(groups)]
        return (_Gather([b for k in want for b in groups[k]]), want) if want else (None, want)

    def keep(want, got):
        for n, k in enumerate(want):
            gathered[k] = got[3 * n:3 * n + 3]

    def ffn_weights(j):
        return tuple(g.reshape(F, D) for g in gathered[j])

    def mix_weights(l):
        g_in, g_out, g_mkv = gathered[3 * l + 1]
        w_inp = g_in.reshape(D_IN, D)
        w_outp = g_out.reshape(D_MIX, D)
        w_mkvp = jnp.pad(g_mkv.reshape(D, 2 * N_MEMH, HEAD_DIM),
                         ((0, 0), (0, 0), (0, SLOT - HEAD_DIM))).reshape(D, 2 * N_MEMH * SLOT)
        return w_inp, w_outp, w_mkvp

    inv_freq = ROPE_THETA ** (-jnp.arange(0, HEAD_DIM, 2, dtype=F32) / HEAD_DIM)
    invf = jnp.tile(inv_freq, SLOT // (HEAD_DIM // 2))[None, :]
    cosT, sinT = _rope_tables(positions.reshape(S, 1), invf, tm)

    row = lambda a, l: a[l][None, :]
    sinks_p = jnp.pad(swa_sinks, ((0, 0), (0, 8 - N_Q)))

    saved = []
    xin = x0
    xn = None
    for l in range(L):
        wf1 = ffn_weights(3 * l)
        rider, want = gather_rider(3 * l)
        (h1, a1, b1, t1), got = _ffn_fwd(xin, row(ffn1_norm, l), wf1, None, tm, tf, rider=rider)
        keep(want, got)
        w_inp, w_outp, w_mkvp = mix_weights(l)
        pu, p, n2 = _proj_fwd(h1, row(mix_norm, l), w_inp, tm)
        gk_m = _pad_vec(mem_k_norm[l])
        nm, mraw, mk, mv = _mem_kv_fwd(mem0, row(mem_norm, l), w_mkvp, gk_m)
        twice = lambda v: jnp.tile(v, 2)[None, :]
        gq, gk, gqm = twice(swa_q_norm[l]), twice(swa_k_norm[l]), twice(mem_q_norm[l])
        rider, want = gather_rider(3 * l + 1)
        (h2, y, yc, lse), got = _mixer_fwd(pu, p, h1, cosT, sinT, conv_wP[l], row(conv_b, l), row(conv_ln_g, l),
                                           row(conv_ln_b, l), gq, gk, sinks_p[l], mk, mv, gqm, w_outp, tm,
                                           rider=rider)
        keep(want, got)
        wf2 = ffn_weights(3 * l + 2)
        rider, want = gather_rider(3 * l + 2)
        (h3, a2, b2, t2, xn), got = _ffn_fwd(h2, row(ffn2_norm, l), wf2, row(final_norm, l), tm, tf, rider=rider)
        keep(want, got)
        saved.append(dict(xin=xin, h1=h1, a1=a1, b1=b1, pu=pu, p=p, n2=n2, nm=nm, mraw=mraw, mk=mk, mv=mv, gk_m=gk_m,
                          gq=gq, gk=gk, gqm=gqm, h2=h2, y=y, yc=yc, lse=lse, h3=h3, a2=a2, b2=b2, t1=t1, t2=t2,
                          wf1=wf1, wf2=wf2, w_inp=w_inp, w_outp=w_outp, w_mkvp=w_mkvp))
        xin = xn

    G = {n: [None] * L for n in names}
    ffn_bufs = [None] * (2 * L)
    mix_bufs = [None] * L
    parts, theirs = {}, {}
    pending_swap = []

    def scattered(key, buf, recv):
        own = lax.dynamic_index_in_dim(buf, my_chip, axis=1, keepdims=False)
        parts[key] = _sum4(own, recv)
        pending_swap.append(key)

    def comm_rider(scatter_buf):
        keys = list(pending_swap)
        pending_swap.clear()
        riders = ([_Scatter([scatter_buf])] if scatter_buf is not None else []) \
            + ([_Swap([parts[k] for k in keys])] if keys else [])
        if not riders:
            return None, lambda got: None
        multi = _Multi(riders)

        def store(got):
            outs = multi.split_outputs(got)
            if keys:
                for k, t in zip(keys, outs[-1]):
                    theirs[k] = t
            return outs[0][0] if scatter_buf is not None else None

        return multi, store

    dxn = None
    loss_part = None
    for l in reversed(range(L)):
        sv = saved[l]
        if l == L - 1:
            loss_part, dh3, G['final_norm'][l] = _loss_bwd(xn, sv['h3'], row(final_norm, l), target, tm)
        else:
            dh3, G['final_norm'][l] = _norm_bwd(dxn, sv['h3'], row(final_norm, l), tm)
        rider, store = comm_rider(ffn_bufs[2 * l + 2] if l < L - 1 else None)
        (dh2, G['ffn2_norm'][l], da, db, n, dy), got = _ffn_bwd_act(
            dh3, sv['h2'], row(ffn2_norm, l), sv['a2'], sv['b2'], sv['wf2'], tm, tf, rider=rider)
        recv = store(got)
        if recv is not None:
            scattered(('f', 2 * l + 2), ffn_bufs[2 * l + 2], recv)
        ffn_bufs[2 * l + 1] = _ffn_bwd_w(da, db, sv['t2'], n, dy, tmw, tfw)[0].reshape(3, 4, Fs, D)
        dyc, do, delta, dwo, G['conv_ln_g'][l], G['conv_ln_b'][l] = _outproj_bwd(
            dh2, sv['y'], sv['yc'], row(conv_ln_g, l), row(conv_ln_b, l), sv['w_outp'], tm)
        rider, store = comm_rider(ffn_bufs[2 * l + 1])
        (dph, dgq, dgk, dgqm, dsink, dmk, dmv), got = _attn_bwd(
            sv['p'], do, sv['lse'], delta, cosT, sinT, sv['gq'], sv['gk'], sinks_p[l],
            sv['mk'], sv['mv'], sv['gqm'], tm, rider=rider)
        scattered(('f', 2 * l + 1), ffn_bufs[2 * l + 1], store(got))
        dpu, dcw, G['conv_b'][l] = _conv_bwd(sv['pu'], dyc, conv_wP[l], tm)
        dwm, G['mem_norm'][l], dgk_m = _mem_kv_bwd(dmk, dmv, sv['mraw'], sv['nm'], mem0, row(mem_norm, l),
                                                   sv['w_mkvp'], sv['gk_m'])
        dh1, G['mix_norm'][l], dwi = _proj_bwd(dpu, dph, sv['h1'], dh2, row(mix_norm, l), sv['n2'], sv['w_inp'], tm)
        dwiT = dwi.reshape(4, r_in, D)
        dwoF = dwo.reshape(4, r_out, D)
        dwmF = dwm.reshape(D, 2 * N_MEMH, SLOT)[:, :, :HEAD_DIM].reshape(4, mkv_rows, D)
        mix_bufs[l] = jnp.concatenate([dwiT, dwoF, dwmF], axis=1).astype(BF16).reshape(1, 4, rm, D)
        rider, store = comm_rider(mix_bufs[l]) if l > 0 else (None, None)
        (dxl, G['ffn1_norm'][l], da, db, n, dy), got = _ffn_bwd_act(
            dh1, sv['xin'], row(ffn1_norm, l), sv['a1'], sv['b1'], sv['wf1'], tm, tf, rider=rider)
        if l > 0:
            scattered(('m', l), mix_bufs[l], store(got))
            rider, store = None, None
        else:
            rider, store = comm_rider(mix_bufs[l])
        dwf, got = _ffn_bwd_w(da, db, sv['t1'], n, dy, tmw, tfw, rider=rider)
        if l == 0:
            scattered(('m', l), mix_bufs[l], store(got))
        ffn_bufs[2 * l] = dwf.reshape(3, 4, Fs, D)
        dxn = dxl
        G['conv_w'][l] = dcw[:CONV_WIDTH]
        G['swa_q_norm'][l] = dgq[0, :HEAD_DIM] + dgq[0, HEAD_DIM:]
        G['swa_k_norm'][l] = dgk[0, :HEAD_DIM] + dgk[0, HEAD_DIM:]
        G['mem_q_norm'][l] = dgqm[0, :HEAD_DIM] + dgqm[0, HEAD_DIM:]
        G['mem_k_norm'][l] = dgk_m[0, :HEAD_DIM]
        G['swa_sinks'][l] = dsink[0, :N_Q]
    grad_x = dxn[None]
    loss = lax.psum(loss_part[0, 0], AXES)
    rider, store = comm_rider(ffn_bufs[0])
    scattered(('f', 0), ffn_bufs[0], store(_run_rider(rider, "scatter_last")))
    rider, store = comm_rider(None)
    store(_run_rider(rider, "swap_last"))

    small = ['ffn1_norm', 'mix_norm', 'conv_b', 'conv_ln_g', 'conv_ln_b', 'swa_q_norm', 'swa_k_norm', 'swa_sinks',
             'mem_norm', 'mem_q_norm', 'mem_k_norm', 'ffn2_norm', 'final_norm']
    gsmall = [jnp.stack([G[n][l].reshape(-1) for l in range(L)]) for n in small]
    gcw = jnp.stack(G['conv_w'])
    cw_cols = 4 * cw_sh
    full_of = lambda a: lax.dynamic_update_slice(jnp.zeros((L, CONV_WIDTH, cw_cols), F32), a, (0, 0, my_chip * cw_sh))
    pk = _Pack([W[n].shape for n in small] + [(L, CONV_WIDTH, cw_cols)])
    g8 = _all_gather_small(pk.pack(gsmall + [gcw])[None])
    outs4 = _small_sum_adam(g8, pk.pack([W[n] for n in small] + [full_of(conv_w)]),
                            pk.pack([M1[n] for n in small] + [full_of(m_conv_w)]),
                            pk.pack([V1[n] for n in small] + [full_of(v_conv_w)]))
    un = [pk.unpack(o) for o in outs4]
    grads, deltas, new_m, new_v = {}, {}, {}, {}
    for k, n in enumerate(small):
        grads[n], deltas[n], new_m[n], new_v[n] = un[0][k], un[1][k], un[2][k], un[3][k]
    mine = lambda a: lax.dynamic_slice(a, (0, 0, my_chip * cw_sh), (L, CONV_WIDTH, cw_sh))
    grads['conv_w'], deltas['conv_w'], new_m['conv_w'], new_v['conv_w'] = [mine(u[-1]) for u in un]

    ffn1_k, ffn2_k = [('f', 2 * l) for l in range(L)], [('f', 2 * l + 1) for l in range(L)]
    mix_k = [('m', l) for l in range(L)]
    plan = [('ffn2_w1', ffn2_k, 0, 0, Fs, True), ('ffn2_w3', ffn2_k, 1, 0, Fs, True), ('ffn2_w2', ffn2_k, 2, 0, Fs, False),
            ('w_in', mix_k, 0, 0, r_in, True), ('w_out', mix_k, 0, r_in, r_out, False),
            ('w_mem_kv', mix_k, 0, r_in + r_out, mkv_rows, False),
            ('ffn1_w1', ffn1_k, 0, 0, Fs, True), ('ffn1_w3', ffn1_k, 1, 0, Fs, True), ('ffn1_w2', ffn1_k, 2, 0, Fs, False)]
    for n, keys, sel, row0, nrows, held_transposed in plan:
        shp = W[n].shape
        if held_transposed:
            view, back = (lambda a: jnp.swapaxes(a, 1, 2)), (lambda a: jnp.swapaxes(a, 1, 2))
        elif n == 'w_mem_kv':
            view, back = (lambda a: a.reshape(L, mkv_rows, D)), (lambda a: a.reshape(shp))
        else:
            view = back = lambda a: a
        res, _ = _adam_fused(view(W[n]), view(M1[n]), view(V1[n]), [parts[k] for k in keys],
                             [theirs[k] for k in keys], sel, row0, nrows)
        grads[n], deltas[n], new_m[n], new_v[n] = [back(r) for r in res]

    return (loss, grad_x, *[grads[n] for n in names], *[deltas[n] for n in names],
            *[new_m[n] for n in names], *[new_v[n] for n in names])
```

```python
import functools

import jax
import jax.numpy as jnp
from jax import lax
from jax.experimental import pallas as pl
from jax.experimental.pallas import tpu as pltpu

F32 = jnp.float32
BF16 = jnp.bfloat16
MESH = pl.DeviceIdType.MESH
AXES = ("x", "y", "c")

EPS = 1e-6
HEAD_DIM = 64
SLOT = 128
CONV_CH = 384
CONV_WIDTH = 31
N_Q, N_KV, N_MEMH = 6, 2, 4
GROUP = N_Q // N_KV
BLK = 128
HALO = 32
CONV_ROWS = 64
FFN_CHUNK = 256
ROPE_THETA = 10000.0
SCALE = HEAD_DIM ** -0.5
NEG = -1e30

N_HEADS_IN = N_Q + 2 * N_KV + N_MEMH
PU = 2 * CONV_CH
PH = HEAD_DIM * N_HEADS_IN
PP = PU + PH
QO = 0
KO = QO + HEAD_DIM * N_Q
VO = KO + HEAD_DIM * N_KV
MO = VO + HEAD_DIM * N_KV
NH = N_Q + N_MEMH
STAT_ROWS = 16
YH = HEAD_DIM * NH
YP = CONV_CH + YH
YS = CONV_CH
YM = YS + HEAD_DIM * N_Q
D_IN = PP
D_MIX = YP
MEM_KV = 2 * HEAD_DIM * N_MEMH

ADAM_LR, ADAM_B1, ADAM_B2, ADAM_EPS, ADAM_WD, ADAM_STEP = 0.001, 0.9, 0.999, 1e-08, 0.01, 10

VMEM_LIMIT_MB = 56


def _cp(mb=VMEM_LIMIT_MB):
    return pltpu.CompilerParams(vmem_limit_bytes=mb * 1024 * 1024)


def _dot_nn(a, b):
    return lax.dot_general(a, b, (((1,), (0,)), ((), ())), preferred_element_type=F32)


def _dot_nt(a, b):
    return lax.dot_general(a, b, (((1,), (1,)), ((), ())), preferred_element_type=F32)


def _dot_tn(a, b):
    return lax.dot_general(a, b, (((0,), (0,)), ((), ())), preferred_element_type=F32)


def _sigmoid(x):
    return 1.0 / (1.0 + jnp.exp(-x))


def _rms(x):
    return lax.rsqrt(jnp.mean(x * x, axis=-1, keepdims=True) + EPS)


def _rms_bwd(dn, x, r, g):
    xhat = x * r
    dxhat = dn * g
    dx = r * (dxhat - xhat * jnp.mean(dxhat * xhat, axis=-1, keepdims=True))
    return dx, dn * xhat


def _colsum(v):
    return jnp.sum(v, axis=0, keepdims=True)


def _lane(n):
    return lax.broadcasted_iota(jnp.int32, (n, SLOT), 1)


def _slot_rms(xs):
    return lax.rsqrt(jnp.sum(xs * xs, axis=-1, keepdims=True) * (1.0 / HEAD_DIM) + EPS)


def _slot_norm(xs, g):
    return xs * _slot_rms(xs) * g


def _slot_norm_bwd(dout, xs, g):
    r = _slot_rms(xs)
    xhat = xs * r
    dxhat = dout * g
    dx = r * (dxhat - xhat * (jnp.sum(dxhat * xhat, axis=-1, keepdims=True) * (1.0 / HEAD_DIM)))
    return dx, dout * xhat


def _halves(v, lane):
    lo = jnp.sum(jnp.where(lane < HEAD_DIM, v, 0.0), axis=-1, keepdims=True)
    hi = jnp.sum(jnp.where(lane < HEAD_DIM, 0.0, v), axis=-1, keepdims=True)
    return jnp.where(lane < HEAD_DIM, lo, hi)


def _pair_rms(x, lane):
    return lax.rsqrt(_halves(x * x, lane) * (1.0 / HEAD_DIM) + EPS)


def _pair_partner(v, lane):
    return jnp.where((lane & (HEAD_DIM - 1)) < HEAD_DIM // 2,
                     pltpu.roll(v, SLOT - HEAD_DIM // 2, 1), pltpu.roll(v, HEAD_DIM // 2, 1))


def _pair_fwd(x, g2, cosv, sinv, lane):
    xn = x * _pair_rms(x, lane) * g2
    if cosv is None:
        return xn
    return xn * cosv + _pair_partner(xn, lane) * sinv


def _pair_bwd(dout, x, g2, cosv, sinv, lane):
    if cosv is not None:
        dout = dout * cosv + _pair_partner(dout * sinv, lane)
    r = _pair_rms(x, lane)
    xhat = x * r
    dxhat = dout * g2
    dx = r * (dxhat - xhat * (_halves(dxhat * xhat, lane) * (1.0 / HEAD_DIM)))
    return dx, dout * xhat


def _lo(x, half, lane):
    if half:
        x = pltpu.roll(x, HEAD_DIM, 1)
    return jnp.where(lane < HEAD_DIM, x, 0.0)


def _pack(even, odd, lane):
    return jnp.where(lane < HEAD_DIM, even, pltpu.roll(odd, HEAD_DIM, 1))


def _place():
    x, y, c = lax.axis_index("x"), lax.axis_index("y"), lax.axis_index("c")
    chips = [(1 - x, y), (x, 1 - y), (1 - x, 1 - y)]
    return x, y, c, chips


class _Gather:
    tag = "_gather"

    def __init__(self, bufs):
        self.bufs = list(bufs)
        nb = len(self.bufs)
        self.out_shape = [jax.ShapeDtypeStruct((b.shape[0], 4) + b.shape[2:], b.dtype) for b in self.bufs]
        self.sems = [pltpu.SemaphoreType.DMA((3 * nb,)), pltpu.SemaphoreType.DMA((3 * nb,)),
                     pltpu.SemaphoreType.DMA((nb,))]

    def _copies(self, ins, outs, sems):
        ssem, rsem, lsem = sems
        nb = len(self.bufs)
        x, y, c, chips = _place()
        mine = 2 * x + y

        def copy(b, p, shard):
            return pltpu.make_async_remote_copy(
                src_ref=ins[b], dst_ref=outs[b].at[:, pl.ds(shard, 1)],
                send_sem=ssem.at[3 * b + p], recv_sem=rsem.at[3 * b + p],
                device_id=(chips[p][0], chips[p][1], c), device_id_type=MESH)

        local = [pltpu.make_async_copy(ins[b], outs[b].at[:, pl.ds(mine, 1)], lsem.at[b]) for b in range(nb)]
        sends = [copy(b, p, mine) for b in range(nb) for p in range(3)]
        recvs = [copy(b, p, 2 * chips[p][0] + chips[p][1]) for b in range(nb) for p in range(3)]
        return local, sends, recvs

    def start(self, ins, outs, sems):
        local, sends, _ = self._copies(ins, outs, sems)
        for cp in local + sends:
            cp.start()

    def wait(self, ins, outs, sems):
        local, sends, recvs = self._copies(ins, outs, sems)
        for cp in recvs:
            cp.wait_recv()
        for cp in sends:
            cp.wait_send()
        for cp in local:
            cp.wait()


class _GatherHalves:
    tag = "_gather_halves"

    def __init__(self, bufs):
        self.bufs = list(bufs)
        nb = len(self.bufs)
        assert all(b.shape[2] % 32 == 0 for b in self.bufs)
        self.out_shape = [jax.ShapeDtypeStruct((b.shape[0], 4) + b.shape[2:], b.dtype) for b in self.bufs]
        self.sems = [pltpu.SemaphoreType.DMA((3 * nb,)), pltpu.SemaphoreType.DMA((3 * nb,)),
                     pltpu.SemaphoreType.DMA((3 * nb,)), pltpu.SemaphoreType.DMA((3 * nb,)),
                     pltpu.SemaphoreType.DMA((nb,))]

    def start(self, ins, outs, sems):
        ssem, rsem, fsem, gsem, lsem = sems
        nb = len(self.bufs)
        x, y, c, chips = _place()
        mine = 2 * x + y
        shard_of = [2 * cx + cy for cx, cy in chips]

        def half(b, h):
            n = self.bufs[b].shape[2] // 2
            return pl.ds(pl.multiple_of(h * n, 16), n)

        def over_links(b, p, shard, h):
            return pltpu.make_async_remote_copy(
                src_ref=ins[b].at[:, :, half(b, h)], dst_ref=outs[b].at[:, pl.ds(shard, 1), half(b, h)],
                send_sem=ssem.at[3 * b + p], recv_sem=rsem.at[3 * b + p],
                device_id=(chips[p][0], chips[p][1], c), device_id_type=MESH)

        def between_cores(b, p, h):
            blk = outs[b].at[:, pl.ds(shard_of[p], 1), half(b, h)]
            return pltpu.make_async_remote_copy(
                src_ref=blk, dst_ref=blk, send_sem=fsem.at[3 * b + p], recv_sem=gsem.at[3 * b + p],
                device_id=(x, y, 1 - c), device_id_type=MESH)

        local = [pltpu.make_async_copy(ins[b], outs[b].at[:, pl.ds(mine, 1)], lsem.at[b]) for b in range(nb)]
        sends = [over_links(b, p, mine, c) for b in range(nb) for p in range(3)]
        for cp in local + sends:
            cp.start()
        handed = []
        for b in range(nb):
            for p in range(3):
                over_links(b, p, shard_of[p], c).wait_recv()
                cp = between_cores(b, p, c)
                cp.start()
                handed.append(cp)
        for b in range(nb):
            for p in range(3):
                between_cores(b, p, 1 - c).wait_recv()
        for cp in sends + handed:
            cp.wait_send()
        for cp in local:
            cp.wait()

    def wait(self, ins, outs, sems):
        pass


class _Scatter:
    tag = "_scatter"

    def __init__(self, bufs):
        self.bufs = list(bufs)
        nb = len(self.bufs)
        self.out_shape = [jax.ShapeDtypeStruct((3, b.shape[0], 1) + b.shape[2:], b.dtype) for b in self.bufs]
        self.sems = [pltpu.SemaphoreType.DMA((3 * nb,)), pltpu.SemaphoreType.DMA((3 * nb,))]

    def _copies(self, ins, outs, sems):
        ssem, rsem = sems
        x, y, c, chips = _place()

        def copy(b, p):
            shard = 2 * chips[p][0] + chips[p][1]
            return pltpu.make_async_remote_copy(
                src_ref=ins[b].at[:, pl.ds(shard, 1)], dst_ref=outs[b].at[p],
                send_sem=ssem.at[3 * b + p], recv_sem=rsem.at[3 * b + p],
                device_id=(chips[p][0], chips[p][1], c), device_id_type=MESH)

        return [copy(b, p) for b in range(len(self.bufs)) for p in range(3)]

    def start(self, ins, outs, sems):
        for cp in self._copies(ins, outs, sems):
            cp.start()

    def wait(self, ins, outs, sems):
        cps = self._copies(ins, outs, sems)
        for cp in cps:
            cp.wait_recv()
        for cp in cps:
            cp.wait_send()


class _Swap:
    tag = "_swap"

    def __init__(self, bufs):
        self.bufs = list(bufs)
        nb = len(self.bufs)
        self.out_shape = [jax.ShapeDtypeStruct(b.shape, b.dtype) for b in self.bufs]
        self.sems = [pltpu.SemaphoreType.DMA((nb,)), pltpu.SemaphoreType.DMA((nb,))]

    def _copies(self, ins, outs, sems):
        ssem, rsem = sems
        x, y, c, _ = _place()
        return [pltpu.make_async_remote_copy(src_ref=ins[b], dst_ref=outs[b], send_sem=ssem.at[b],
                                             recv_sem=rsem.at[b], device_id=(x, y, 1 - c), device_id_type=MESH)
                for b in range(len(self.bufs))]

    def start(self, ins, outs, sems):
        for cp in self._copies(ins, outs, sems):
            cp.start()

    def wait(self, ins, outs, sems):
        cps = self._copies(ins, outs, sems)
        for cp in cps:
            cp.wait_recv()
        for cp in cps:
            cp.wait_send()


class _Multi:
    def __init__(self, riders):
        self.riders = [r for r in riders if r is not None and r.bufs]
        self.tag = "".join(r.tag for r in self.riders)
        self.bufs = [b for r in self.riders for b in r.bufs]
        self.out_shape = [s for r in self.riders for s in r.out_shape]
        self.sems = [s for r in self.riders for s in r.sems]

    def _split(self, ins, outs, sems):
        ob, os_ = 0, 0
        for r in self.riders:
            nb, ns = len(r.bufs), len(r.sems)
            yield r, ins[ob:ob + nb], outs[ob:ob + nb], sems[os_:os_ + ns]
            ob, os_ = ob + nb, os_ + ns

    def start(self, ins, outs, sems):
        for r, i, o, s in self._split(ins, outs, sems):
            r.start(i, o, s)

    def wait(self, ins, outs, sems):
        for r, i, o, s in self._split(ins, outs, sems):
            r.wait(i, o, s)

    def split_outputs(self, got):
        res, ob = [], 0
        for r in self.riders:
            res.append(got[ob:ob + len(r.bufs)])
            ob += len(r.bufs)
        return res


def _run_rider(rider, name):
    nb = len(rider.bufs)

    def body(*refs):
        ins, outs, sems = refs[:nb], refs[nb:2 * nb], refs[2 * nb:]
        rider.start(ins, outs, sems)
        rider.wait(ins, outs, sems)

    hbm = pl.BlockSpec(memory_space=pl.ANY)
    return pl.pallas_call(body, name=name, in_specs=[hbm] * nb, out_specs=[hbm] * nb,
                          out_shape=rider.out_shape, scratch_shapes=rider.sems)(*rider.bufs)


def _call(body, *, name, grid, in_specs, out_specs, out_shape, args, scratch=(), rider=None):
    if rider is None:
        outs = pl.pallas_call(body, name=name, grid=grid, in_specs=list(in_specs), out_specs=list(out_specs),
                              out_shape=list(out_shape), scratch_shapes=list(scratch),
                              compiler_params=_cp())(*args)
        return list(outs), None
    n_in, n_out, n_scr, nb = len(in_specs), len(out_specs), len(scratch), len(rider.bufs)

    def wrapped(*refs):
        cuts = [n_in, nb, n_out, nb, n_scr]
        parts, o = [], 0
        for n in cuts:
            parts.append(refs[o:o + n])
            o += n
        ins, rin, outs, rout, scr = parts
        sems = refs[o:]
        ids = [pl.program_id(k) for k in range(len(grid))]
        first = functools.reduce(jnp.logical_and, [i == 0 for i in ids])
        last = functools.reduce(jnp.logical_and, [i == n - 1 for i, n in zip(ids, grid)])

        @pl.when(first)
        def _():
            rider.start(rin, rout, sems)

        body(*ins, *outs, *scr)

        @pl.when(last)
        def _():
            rider.wait(rin, rout, sems)

    hbm = pl.BlockSpec(memory_space=pl.ANY)
    res = pl.pallas_call(
        wrapped, name=name + rider.tag, grid=grid,
        in_specs=list(in_specs) + [hbm] * nb, out_specs=list(out_specs) + [hbm] * nb,
        out_shape=list(out_shape) + rider.out_shape, scratch_shapes=list(scratch) + rider.sems,
        compiler_params=_cp())(*args, *rider.bufs)
    return list(res[:n_out]), list(res[n_out:])


def _rope_tables(pos, invf, tm):
    S = pos.shape[0]

    def body(pos_ref, f_ref, cos_ref, sin_ref):
        ang = pos_ref[...].astype(F32) * f_ref[...]
        lane = _lane(tm)
        cos_ref[...] = jnp.cos(ang)
        s = jnp.sin(ang)
        sin_ref[...] = jnp.where((lane & (HEAD_DIM - 1)) < HEAD_DIM // 2, -s, s)

    return pl.pallas_call(
        body, name="rope_tables", grid=(S // tm,),
        in_specs=[pl.BlockSpec((tm, 1), lambda i: (i, 0)), pl.BlockSpec((1, SLOT), lambda i: (0, 0))],
        out_specs=[pl.BlockSpec((tm, SLOT), lambda i: (i, 0))] * 2,
        out_shape=[jax.ShapeDtypeStruct((S, SLOT), F32)] * 2,
    )(pos, invf)


def _ffn_fwd(x, g, wf, gfin, tm, tf, rider=None):
    S, D = x.shape
    F = wf[0].shape[0]
    nf = F // tf
    final = gfin is not None

    chunks = [(c, min(FFN_CHUNK, tf - c)) for c in range(0, tf, FFN_CHUNK)]

    def body(*refs):
        if final:
            x_ref, g_ref, w1_ref, w3_ref, w2_ref, gf_ref, h_ref, a_ref, b_ref, t_ref, xn_ref, n_scr, acc = refs
        else:
            x_ref, g_ref, w1_ref, w3_ref, w2_ref, h_ref, a_ref, b_ref, t_ref, n_scr, acc = refs
        j = pl.program_id(1)

        @pl.when(j == 0)
        def _():
            xv = x_ref[...]
            n_scr[...] = (xv * _rms(xv) * g_ref[...]).astype(BF16)
            acc[...] = jnp.zeros_like(acc)

        n = n_scr[...]
        for c0, cw in chunks:
            cols = slice(c0, c0 + cw)
            a = _dot_nt(n, w1_ref[cols, :])
            b = _dot_nt(n, w3_ref[cols, :])
            a_ref[:, cols] = a.astype(BF16)
            b_ref[:, cols] = b.astype(BF16)
            t_ref[:, cols] = (a * _sigmoid(a) * b).astype(BF16)
        acc[...] += _dot_nn(t_ref[...], w2_ref[...])

        @pl.when(j == nf - 1)
        def _():
            h = x_ref[...] + 0.5 * acc[...]
            h_ref[...] = h
            if final:
                xn_ref[...] = h * _rms(h) * gf_ref[...]

    def wspec(k):
        return pl.BlockSpec((tf, D), lambda i, j: (j, 0))

    row = pl.BlockSpec((tm, D), lambda i, j: (i, 0))
    vec = pl.BlockSpec((1, D), lambda i, j: (0, 0))
    act = pl.BlockSpec((tm, tf), lambda i, j: (i, j))
    in_specs = [row, vec, wspec(0), wspec(1), wspec(2)] + ([vec] if final else [])
    out_specs = [row, act, act, act] + ([row] if final else [])
    out_shape = [jax.ShapeDtypeStruct((S, D), F32)] + [jax.ShapeDtypeStruct((S, F), BF16)] * 3 \
        + ([jax.ShapeDtypeStruct((S, D), F32)] if final else [])
    args = [x, g, *wf] + ([gfin] if final else [])
    return _call(body, name="ffn_fwd_final" if final else "ffn_fwd", grid=(S // tm, nf),
                 in_specs=in_specs, out_specs=out_specs, out_shape=out_shape, args=args,
                 scratch=[pltpu.VMEM((tm, D), BF16), pltpu.VMEM((tm, D), F32)], rider=rider)


def _ffn_bwd_act(dh, x, g, a, b, wf, tm, tf, rider=None):
    S, D = x.shape
    F = wf[0].shape[0]
    nf = F // tf

    chunks = [(c, min(FFN_CHUNK, tf - c)) for c in range(0, tf, FFN_CHUNK)]

    def body(dh_ref, x_ref, g_ref, a_ref, b_ref, w1_ref, w3_ref, w2_ref,
             dx_ref, dg_ref, da_ref, db_ref, n_ref, dy_ref, acc):
        i, j = pl.program_id(0), pl.program_id(1)

        @pl.when(j == 0)
        def _():
            xv = x_ref[...]
            n_ref[...] = (xv * _rms(xv) * g_ref[...]).astype(BF16)
            dy_ref[...] = (0.5 * dh_ref[...]).astype(BF16)
            acc[...] = jnp.zeros_like(acc)

            @pl.when(i == 0)
            def _():
                dg_ref[...] = jnp.zeros_like(dg_ref)

        dyv = dy_ref[...]
        for c0, cw in chunks:
            cols = slice(c0, c0 + cw)
            av = a_ref[:, cols].astype(F32)
            bv = b_ref[:, cols].astype(F32)
            sg = _sigmoid(av)
            dt = _dot_nt(dyv, w2_ref[cols, :])
            db_ref[:, cols] = (dt * (av * sg)).astype(BF16)
            da_ref[:, cols] = (dt * bv * (sg * (1.0 + av * (1.0 - sg)))).astype(BF16)
        acc[...] += _dot_nn(da_ref[...], w1_ref[...]) + _dot_nn(db_ref[...], w3_ref[...])

        @pl.when(j == nf - 1)
        def _():
            xv = x_ref[...]
            dx, dgrow = _rms_bwd(acc[...], xv, _rms(xv), g_ref[...])
            dx_ref[...] = dh_ref[...] + dx
            dg_ref[...] += _colsum(dgrow)

    def wspec(k):
        return pl.BlockSpec((tf, D), lambda i, j: (j, 0))

    row = pl.BlockSpec((tm, D), lambda i, j: (i, 0))
    vec = pl.BlockSpec((1, D), lambda i, j: (0, 0))
    act = pl.BlockSpec((tm, tf), lambda i, j: (i, j))
    sd = lambda shp, dt: jax.ShapeDtypeStruct(shp, dt)
    return _call(body, name="ffn_bwd_act", grid=(S // tm, nf),
                 in_specs=[row, row, vec, act, act, wspec(0), wspec(1), wspec(2)],
                 out_specs=[row, vec, act, act, row, row],
                 out_shape=[sd((S, D), F32), sd((1, D), F32), sd((S, F), BF16), sd((S, F), BF16),
                            sd((S, D), BF16), sd((S, D), BF16)],
                 args=[dh, x, g, a, b, *wf], scratch=[pltpu.VMEM((tm, D), F32)], rider=rider)


def _ffn_bwd_w(da, db, t, n, dy, tm, tf, rider=None):
    S, F = da.shape
    D = n.shape[1]
    nt = S // tm

    def body(da_ref, db_ref, t_ref, n_ref, dy_ref, out_ref, acc):
        i = pl.program_id(1)

        @pl.when(i == 0)
        def _():
            acc[...] = jnp.zeros_like(acc)

        nv = n_ref[...]
        acc[0] += _dot_tn(da_ref[...], nv)
        acc[1] += _dot_tn(db_ref[...], nv)
        acc[2] += _dot_tn(t_ref[...], dy_ref[...])

        @pl.when(i == nt - 1)
        def _():
            out_ref[...] = acc[...].astype(BF16)

    act = pl.BlockSpec((tm, tf), lambda j, i: (i, j))
    row = pl.BlockSpec((tm, D), lambda j, i: (i, 0))
    outs, got = _call(body, name="ffn_bwd_w", grid=(F // tf, nt),
                      in_specs=[act, act, act, row, row],
                      out_specs=[pl.BlockSpec((3, tf, D), lambda j, i: (0, j, 0))],
                      out_shape=[jax.ShapeDtypeStruct((3, F, D), BF16)],
                      args=[da, db, t, n, dy], scratch=[pltpu.VMEM((3, tf, D), F32)], rider=rider)
    return outs[0], got


def _proj_fwd(h, g, w_inp, tm):
    S, D = h.shape

    def body(h_ref, g_ref, w_ref, pu_ref, ph_ref, n_ref):
        hv = h_ref[...]
        n = (hv * _rms(hv) * g_ref[...]).astype(BF16)
        n_ref[...] = n
        pu_ref[...] = _dot_nt(n, w_ref[0:PU, :])
        ph_ref[...] = _dot_nt(n, w_ref[PU:PP, :])

    cur = lambda w: pl.BlockSpec((tm, w), lambda i: (i, 0))
    return pl.pallas_call(
        body, name="proj_fwd", grid=(S // tm,),
        in_specs=[cur(D), pl.BlockSpec((1, D), lambda i: (0, 0)), pl.BlockSpec((PP, D), lambda i: (0, 0))],
        out_specs=[cur(PU), cur(PH), cur(D)],
        out_shape=[jax.ShapeDtypeStruct((S, PU), F32), jax.ShapeDtypeStruct((S, PH), F32),
                   jax.ShapeDtypeStruct((S, D), BF16)],
        compiler_params=_cp(),
    )(h, g, w_inp)


def _glu(u):
    return u[:, :CONV_CH] * _sigmoid(u[:, CONV_CH:2 * CONV_CH])


def _shifted_copies(ext8):
    n = ext8.shape[1]
    for b in range(1, 8):
        ext8[b, 0:n - 8, :] = ext8[0, b:b + n - 8, :]


def _window(ext8, off, rows, r0=0):
    return ext8[off % 8, pl.ds(r0 + (off - off % 8), rows), :]


def _layer_norm_stats(yc):
    mu = jnp.mean(yc, axis=-1, keepdims=True)
    d = yc - mu
    rstd = lax.rsqrt(jnp.mean(d * d, axis=-1, keepdims=True) + EPS)
    return d * rstd, rstd


def _mem_kv_fwd(mem, g, w_mkvp, gk):
    M, D = mem.shape
    W = SLOT * N_MEMH

    def body(mem_ref, g_ref, w_ref, gk_ref, nm_ref, raw_ref, mk_ref, mv_ref):
        mv_ = mem_ref[...]
        nm = (mv_ * _rms(mv_) * g_ref[...]).astype(BF16)
        nm_ref[...] = nm
        raw = _dot_nn(nm, w_ref[...])
        raw_ref[...] = raw
        for hh in range(N_MEMH):
            sl = slice(SLOT * hh, SLOT * (hh + 1))
            mk_ref[:, sl] = _slot_norm(raw[:, sl], gk_ref[...]).astype(BF16)
        mv_ref[...] = raw[:, W:].astype(BF16)

    sd = jax.ShapeDtypeStruct
    return pl.pallas_call(
        body, name="mem_kv_fwd",
        out_shape=[sd((M, D), BF16), sd((M, 2 * W), F32), sd((M, W), BF16), sd((M, W), BF16)],
        compiler_params=_cp(),
    )(mem, g, w_mkvp, gk)


def _mem_kv_bwd(dmk, dmv, raw, nm, mem, g, w_mkvp, gk):
    M, D = mem.shape
    W = SLOT * N_MEMH

    def body(dmk_ref, dmv_ref, raw_ref, nm_ref, mem_ref, g_ref, w_ref, gk_ref, dw_ref, dg_ref, dgk_ref, draw):
        dgk = jnp.zeros((1, SLOT), F32)
        for hh in range(N_MEMH):
            sl = slice(SLOT * hh, SLOT * (hh + 1))
            dx, prod = _slot_norm_bwd(dmk_ref[:, sl], raw_ref[:, sl], gk_ref[...])
            draw[:, sl] = dx.astype(BF16)
            dgk = dgk + _colsum(prod)
        dgk_ref[...] = dgk
        draw[:, W:] = dmv_ref[...].astype(BF16)
        dr = draw[...]
        dw_ref[...] = _dot_tn(nm_ref[...], dr)
        dnm = _dot_nt(dr, w_ref[...])
        mv_ = mem_ref[...]
        dg_ref[...] = _colsum(dnm * (mv_ * _rms(mv_)))

    sd = jax.ShapeDtypeStruct
    return pl.pallas_call(
        body, name="mem_kv_bwd",
        out_shape=[sd((D, 2 * W), F32), sd((1, D), F32), sd((1, SLOT), F32)],
        scratch_shapes=[pltpu.VMEM((M, 2 * W), BF16)],
        compiler_params=_cp(),
    )(dmk, dmv, raw, nm, mem, g, w_mkvp, gk)


def _mixer_fwd(pu, ph, h, cosT, sinT, conv_w, conv_b, ln_g, ln_b, gq, gk, sinks, mk, mv, gqm, w_outp, tm, rider=None):
    S, D = h.shape
    M = mk.shape[0]
    nb = tm // BLK
    nblocks = S // BLK

    def body(pu_ref, pup_ref, p_ref, ph_ref, h_ref, cos_ref, cosh_ref, sin_ref, sinh_ref, cw_ref, cb_ref,
             lg_ref, lb_ref, gq_ref, gk_ref, sink_ref, mk_ref, mv_ref, gqm_ref, wo_ref,
             h2_ref, y_ref, yc_ref, lse_ref, ext, y_scr):
        i = pl.program_id(0)
        not_first = (i > 0).astype(F32)
        lane = _lane(tm)
        lane_e = _lane(tm + BLK)

        ext[0, 0:HALO, :] = _glu(pup_ref[...]) * not_first
        ext[0, HALO:HALO + tm, :] = _glu(pu_ref[...])
        _shifted_copies(ext)

        def rows_chunk(r, carry):
            r0 = pl.multiple_of(r * CONV_ROWS, CONV_ROWS)
            yc = jnp.zeros((CONV_ROWS, CONV_CH), F32) + cb_ref[...]
            for k in range(CONV_WIDTH):
                yc = yc + cw_ref[k:k + 1, :] * _window(ext, HALO - (CONV_WIDTH - 1) + k, CONV_ROWS, r0)
            yc_ref[pl.ds(r0, CONV_ROWS), :] = yc
            z, _ = _layer_norm_stats(yc)
            ln = z * lg_ref[...] + lb_ref[...]
            y_scr[pl.ds(r0, CONV_ROWS), 0:CONV_CH] = (ln * _sigmoid(ln)).astype(BF16)
            return carry

        lax.fori_loop(0, tm // CONV_ROWS, rows_chunk, 0)

        cos_e = jnp.concatenate([cosh_ref[...], cos_ref[...]], axis=0)
        sin_e = jnp.concatenate([sinh_ref[...], sin_ref[...]], axis=0)
        qi = lax.broadcasted_iota(jnp.int32, (GROUP * BLK, 2 * BLK), 0) & (BLK - 1)
        kj = lax.broadcasted_iota(jnp.int32, (GROUP * BLK, 2 * BLK), 1)
        band = (kj > qi) & (kj <= qi + BLK)
        band0 = band & ((kj >= BLK) | (i > 0))
        lse = jnp.zeros((tm, SLOT), F32)
        k_pair = jnp.concatenate([ph_ref[:, KO:KO + SLOT], p_ref[:, KO:KO + SLOT]], axis=0)
        k_pair = _pair_fwd(k_pair, gk_ref[...], cos_e, sin_e, lane_e)
        v_pair = jnp.concatenate([ph_ref[:, VO:VO + SLOT], p_ref[:, VO:VO + SLOT]], axis=0)
        k_e = [_lo(k_pair, kvh, lane_e).astype(BF16) for kvh in range(N_KV)]
        v_e = [_lo(v_pair, kvh, lane_e).astype(BF16) for kvh in range(N_KV)]
        q_lo = []
        for j in range(N_Q // 2):
            q_pair = _pair_fwd(p_ref[:, QO + SLOT * j:QO + SLOT * (j + 1)], gq_ref[...],
                               cos_ref[...], sin_ref[...], lane)
            q_lo += [_lo(q_pair, 0, lane).astype(BF16), _lo(q_pair, 1, lane).astype(BF16)]
        outs = [[] for _ in range(N_Q)]
        lses = [[] for _ in range(N_Q)]
        for kvh in range(N_KV):
            hs = [GROUP * kvh + gi for gi in range(GROUP)]
            sink3 = jnp.concatenate([jnp.full((BLK, 1), sink_ref[h], F32) for h in hs], axis=0)
            for m in range(nb):
                rows = slice(BLK * m, BLK * (m + 1))
                win = slice(BLK * m, BLK * (m + 2))
                q3 = jnp.concatenate([q_lo[h][rows] for h in hs], axis=0)
                s = _dot_nt(q3, k_e[kvh][win]) * SCALE
                s = jnp.where(band0 if m == 0 else band, s, NEG)
                mx = jnp.maximum(jnp.max(s, axis=-1, keepdims=True), sink3)
                e = jnp.exp(s - mx)
                den = jnp.sum(e, axis=-1, keepdims=True) + jnp.exp(sink3 - mx)
                o3 = _dot_nn((e / den).astype(BF16), v_e[kvh][win])
                l3 = mx + jnp.log(den)
                for gi, h in enumerate(hs):
                    outs[h].append(o3[BLK * gi:BLK * (gi + 1)])
                    lses[h].append(l3[BLK * gi:BLK * (gi + 1)])
        for h in range(N_Q):
            lse = jnp.where(lane == h, jnp.concatenate(lses[h], axis=0), lse)
        for j in range(N_Q // 2):
            y_scr[:, YS + SLOT * j:YS + SLOT * (j + 1)] = _pack(
                jnp.concatenate(outs[2 * j], axis=0), jnp.concatenate(outs[2 * j + 1], axis=0), lane).astype(BF16)

        heads = []
        for hm in range(N_MEMH):
            ms = slice(SLOT * hm, SLOT * (hm + 1))
            if hm % 2 == 0:
                qm_pair = _pair_fwd(p_ref[:, MO + SLOT * (hm // 2):MO + SLOT * (hm // 2 + 1)], gqm_ref[...],
                                    None, None, lane)
            s = _dot_nt(_lo(qm_pair, hm % 2, lane).astype(BF16), mk_ref[:, ms]) * SCALE
            mx = jnp.max(s, axis=-1, keepdims=True)
            e = jnp.exp(s - mx)
            den = jnp.sum(e, axis=-1, keepdims=True)
            heads.append(_dot_nn((e / den).astype(BF16), mv_ref[:, ms]))
            lse = jnp.where(lane == N_Q + hm, mx + jnp.log(den), lse)
            if hm % 2 == 1:
                y_scr[:, YM + SLOT * (hm // 2):YM + SLOT * (hm // 2 + 1)] = _pack(heads[-2], heads[-1], lane).astype(BF16)
        lse_ref[...] = lse.T[0:STAT_ROWS, :]

        yv = y_scr[...]
        y_ref[...] = yv
        h2_ref[...] = h_ref[...] + _dot_nn(yv, wo_ref[...])

    cur = lambda w: pl.BlockSpec((tm, w), lambda i: (i, 0))
    prev = lambda w: pl.BlockSpec((BLK, w), lambda i: (jnp.maximum(i * nb - 1, 0), 0))
    full = lambda a: pl.BlockSpec(a.shape, lambda i: (0,) * a.ndim)
    sd = jax.ShapeDtypeStruct
    prev32 = pl.BlockSpec((HALO, PU), lambda i: (jnp.maximum(i * (tm // HALO) - 1, 0), 0))
    return _call(
        body, name="mixer_fwd", grid=(S // tm,),
        in_specs=[cur(PU), prev32, cur(PH), prev(PH), cur(D), cur(SLOT), prev(SLOT), cur(SLOT), prev(SLOT),
                  full(conv_w), full(conv_b), full(ln_g), full(ln_b), full(gq), full(gk),
                  pl.BlockSpec(memory_space=pltpu.SMEM), full(mk), full(mv), full(gqm), full(w_outp)],
        out_specs=[cur(D), cur(YP), cur(CONV_CH), pl.BlockSpec((STAT_ROWS, tm), lambda i: (0, i))],
        out_shape=[sd((S, D), F32), sd((S, YP), BF16), sd((S, CONV_CH), F32), sd((STAT_ROWS, S), F32)],
        args=[pu, pu, ph, ph, h, cosT, cosT, sinT, sinT, conv_w, conv_b, ln_g, ln_b, gq, gk, sinks, mk, mv, gqm,
              w_outp],
        scratch=[pltpu.VMEM((8, tm + HALO, CONV_CH), F32), pltpu.VMEM((tm, YP), BF16)], rider=rider)


def _outproj_bwd(dh2, y, yc, ln_g, ln_b, w_outp, tm):
    S, D = dh2.shape

    def body(dh_ref, y_ref, yc_ref, lg_ref, lb_ref, wo_ref, dyc_ref, do_ref, del_ref, dwo_ref, dlg_ref, dlb_ref):
        i = pl.program_id(0)

        @pl.when(i == 0)
        def _():
            dwo_ref[...] = jnp.zeros_like(dwo_ref)
            dlg_ref[...] = jnp.zeros_like(dlg_ref)
            dlb_ref[...] = jnp.zeros_like(dlb_ref)

        dhb = dh_ref[...].astype(BF16)
        yv = y_ref[...]
        dy = _dot_nt(dhb, wo_ref[...])
        dwo_ref[...] += _dot_tn(yv, dhb)

        z, rstd = _layer_norm_stats(yc_ref[...])
        ln = z * lg_ref[...] + lb_ref[...]
        sg = _sigmoid(ln)
        dln = dy[:, 0:CONV_CH] * (sg * (1.0 + ln * (1.0 - sg)))
        dlg_ref[...] += _colsum(dln * z)
        dlb_ref[...] += _colsum(dln)
        dz = dln * lg_ref[...]
        dyc_ref[...] = rstd * (dz - jnp.mean(dz, axis=-1, keepdims=True)
                               - z * jnp.mean(dz * z, axis=-1, keepdims=True))
        do_ref[...] = dy[:, CONV_CH:].astype(BF16)

        lane = _lane(tm)
        delta = jnp.zeros((tm, SLOT), F32)
        for j in range(NH // 2):
            sl = slice(YS + SLOT * j, YS + SLOT * (j + 1))
            prod = dy[:, sl] * yv[:, sl].astype(F32)
            lo = jnp.sum(jnp.where(lane < HEAD_DIM, prod, 0.0), axis=-1, keepdims=True)
            hi = jnp.sum(jnp.where(lane < HEAD_DIM, 0.0, prod), axis=-1, keepdims=True)
            delta = jnp.where(lane == 2 * j, lo, jnp.where(lane == 2 * j + 1, hi, delta))
        del_ref[...] = delta.T[0:STAT_ROWS, :]

    cur = lambda w: pl.BlockSpec((tm, w), lambda i: (i, 0))
    full = lambda a: pl.BlockSpec(a.shape, lambda i: (0,) * a.ndim)
    sd = jax.ShapeDtypeStruct
    return pl.pallas_call(
        body, name="outproj_bwd", grid=(S // tm,),
        in_specs=[cur(D), cur(YP), cur(CONV_CH), full(ln_g), full(ln_b), full(w_outp)],
        out_specs=[cur(CONV_CH), cur(YH), pl.BlockSpec((STAT_ROWS, tm), lambda i: (0, i)),
                   pl.BlockSpec((YP, D), lambda i: (0, 0)),
                   pl.BlockSpec((1, CONV_CH), lambda i: (0, 0)), pl.BlockSpec((1, CONV_CH), lambda i: (0, 0))],
        out_shape=[sd((S, CONV_CH), F32), sd((S, YH), BF16), sd((STAT_ROWS, S), F32), sd((YP, D), F32),
                   sd((1, CONV_CH), F32), sd((1, CONV_CH), F32)],
        compiler_params=_cp(),
    )(dh2, y, yc, ln_g, ln_b, w_outp)


def _conv_bwd(pu, dyc, conv_w, tm):
    S = pu.shape[0]
    nt = S // tm
    nh = tm // HALO

    def body(pu_ref, pup_ref, dy_ref, dyn_ref, cw_ref, dpu_ref, dcw_ref, dcb_ref, ext, ext2, dcw8):
        i = pl.program_id(0)

        @pl.when(i == 0)
        def _():
            dcw8[...] = jnp.zeros_like(dcw8)
            dcb_ref[...] = jnp.zeros_like(dcb_ref)

        not_first = (i > 0).astype(F32)
        not_last = (i < nt - 1).astype(F32)
        ext[0, 0:HALO, :] = _glu(pup_ref[...]) * not_first
        ext[0, HALO:HALO + tm, :] = _glu(pu_ref[...])
        _shifted_copies(ext)
        ext2[0, 0:tm, :] = dy_ref[...]
        ext2[0, tm:tm + HALO, :] = dyn_ref[...] * not_last
        _shifted_copies(ext2)
        dcb_ref[...] += _colsum(dy_ref[...])

        def rows_chunk(r, carry):
            r0 = pl.multiple_of(r * CONV_ROWS, CONV_ROWS)
            dyc_ = dy_ref[pl.ds(r0, CONV_ROWS), :]
            dyg = jnp.zeros((CONV_ROWS, CONV_CH), F32)
            for k in range(CONV_WIDTH):
                prod = dyc_ * _window(ext, HALO - (CONV_WIDTH - 1) + k, CONV_ROWS, r0)
                dcw8[k] += jnp.sum(prod.reshape(CONV_ROWS // 8, 8, CONV_CH), axis=0)
                dyg = dyg + cw_ref[k:k + 1, :] * _window(ext2, CONV_WIDTH - 1 - k, CONV_ROWS, r0)
            u = pu_ref[pl.ds(r0, CONV_ROWS), :]
            a_, sg = u[:, :CONV_CH], _sigmoid(u[:, CONV_CH:])
            dpu_ref[pl.ds(r0, CONV_ROWS), 0:CONV_CH] = (dyg * sg).astype(BF16)
            dpu_ref[pl.ds(r0, CONV_ROWS), CONV_CH:PU] = (dyg * a_ * sg * (1.0 - sg)).astype(BF16)
            return carry

        lax.fori_loop(0, tm // CONV_ROWS, rows_chunk, 0)

        @pl.when(i == nt - 1)
        def _():
            dcw_ref[...] = jnp.sum(dcw8[...], axis=1)

    cur = lambda w: pl.BlockSpec((tm, w), lambda i: (i, 0))
    prev = lambda w: pl.BlockSpec((HALO, w), lambda i: (jnp.maximum(i * nh - 1, 0), 0))
    nxt = lambda w: pl.BlockSpec((HALO, w), lambda i: (jnp.minimum((i + 1) * nh, S // HALO - 1), 0))
    acc = lambda r, w: pl.BlockSpec((r, w), lambda i: (0, 0))
    sd = jax.ShapeDtypeStruct
    return pl.pallas_call(
        body, name="conv_bwd", grid=(nt,),
        in_specs=[cur(PU), prev(PU), cur(CONV_CH), nxt(CONV_CH), acc(32, CONV_CH)],
        out_specs=[cur(PU), acc(32, CONV_CH), acc(1, CONV_CH)],
        out_shape=[sd((S, PU), BF16), sd((32, CONV_CH), F32), sd((1, CONV_CH), F32)],
        scratch_shapes=[pltpu.VMEM((8, tm + HALO, CONV_CH), F32), pltpu.VMEM((8, tm + HALO, CONV_CH), F32),
                        pltpu.VMEM((32, 8, CONV_CH), F32)],
        compiler_params=_cp(),
    )(pu, pu, dyc, dyc, conv_w)


def _attn_bwd(p, do, lse, delta, cosT, sinT, gq, gk, sinks, mk, mv, gqm, tm, rider=None):
    S = p.shape[0]
    M = mk.shape[0]
    nb = tm // BLK
    nt = S // tm
    nblocks = S // BLK
    W = SLOT * N_MEMH

    def body(p_ref, pp_ref, pn_ref, dy_ref, dyn_ref, lse_ref, lsen_ref, del_ref, deln_ref,
             cos_ref, cosp_ref, cosn_ref, sin_ref, sinp_ref, sinn_ref,
             gq_ref, gk_ref, sink_ref, mk_ref, mv_ref, gqm_ref,
             dp_ref, dgq_ref, dgk_ref, dgqm_ref, dsink_ref, dmk_ref, dmv_ref):
        i = pl.program_id(0)

        @pl.when(i == 0)
        def _():
            for r in (dgq_ref, dgk_ref, dgqm_ref, dsink_ref, dmk_ref, dmv_ref):
                r[...] = jnp.zeros_like(r)

        lane = _lane(tm)
        lane_e = _lane(tm + BLK)

        cos_k = jnp.concatenate([cosp_ref[...], cos_ref[...]], axis=0)
        sin_k = jnp.concatenate([sinp_ref[...], sin_ref[...]], axis=0)
        cos_q = jnp.concatenate([cos_ref[...], cosn_ref[...]], axis=0)
        sin_q = jnp.concatenate([sin_ref[...], sinn_ref[...]], axis=0)
        lse_e = jnp.concatenate([lse_ref[...], lsen_ref[...]], axis=1)
        del_e = jnp.concatenate([del_ref[...], deln_ref[...]], axis=1)
        kj = lax.broadcasted_iota(jnp.int32, (BLK, GROUP * BLK), 0)
        qi = lax.broadcasted_iota(jnp.int32, (BLK, GROUP * BLK), 1) & (BLK - 1)
        diag = kj <= qi
        offd = kj > qi
        dgq = jnp.zeros((1, SLOT), F32)
        dgk = jnp.zeros((1, SLOT), F32)
        dsink = jnp.zeros((1, SLOT), F32)
        lane1 = lax.broadcasted_iota(jnp.int32, (1, SLOT), 1)
        k_pair = jnp.concatenate([pp_ref[:, KO:KO + SLOT], p_ref[:, KO:KO + SLOT]], axis=0)
        k_pair = _pair_fwd(k_pair, gk_ref[...], cos_k, sin_k, lane_e)
        v_pair = jnp.concatenate([pp_ref[:, VO:VO + SLOT], p_ref[:, VO:VO + SLOT]], axis=0)
        k_e = [_lo(k_pair, kvh, lane_e).astype(BF16) for kvh in range(N_KV)]
        v_e = [_lo(v_pair, kvh, lane_e).astype(BF16) for kvh in range(N_KV)]
        dk = [[jnp.zeros((BLK, SLOT), F32) for _ in range(nb)] for _ in range(N_KV)]
        dv = [[jnp.zeros((BLK, SLOT), F32) for _ in range(nb)] for _ in range(N_KV)]
        q_e, do_e = [], []
        for j in range(N_Q // 2):
            js = slice(SLOT * j, SLOT * (j + 1))
            q_pair = _pair_fwd(jnp.concatenate([p_ref[:, js], pn_ref[:, js]], axis=0), gq_ref[...],
                               cos_q, sin_q, lane_e)
            do_pair = jnp.concatenate([dy_ref[:, js], dyn_ref[:, js]], axis=0).astype(F32)
            for half in range(2):
                q_e.append(_lo(q_pair, half, lane_e).astype(BF16))
                do_e.append(_lo(do_pair, half, lane_e).astype(BF16))
        dq_heads = [None] * N_Q
        for kvh in range(N_KV):
            hs = [GROUP * kvh + gi for gi in range(GROUP)]
            dq3 = [None] * nb
            for m in range(nb + 1):
                rows = slice(BLK * m, BLK * (m + 1))
                q3 = jnp.concatenate([q_e[h][rows] for h in hs], axis=0)
                do3 = jnp.concatenate([do_e[h][rows] for h in hs], axis=0)
                lb3 = jnp.concatenate([lse_e[h:h + 1, rows] for h in hs], axis=1)
                db3 = jnp.concatenate([del_e[h:h + 1, rows] for h in hs], axis=1)
                for n in (m - 1, m):
                    if n == nb:
                        continue
                    krows = slice(BLK * (n + 1), BLK * (n + 2))
                    kb, vb = k_e[kvh][krows], v_e[kvh][krows]
                    s = _dot_nt(kb, q3) * SCALE
                    mask = diag if n == m else offd
                    if n == -1:
                        mask = mask & (i > 0)
                    if m == nb:
                        mask = mask & (i < nt - 1)
                    prob = jnp.where(mask, jnp.exp(jnp.where(mask, s - lb3, NEG)), 0.0)
                    dpb = _dot_nt(vb, do3)
                    ds = (prob * (dpb - db3) * SCALE).astype(BF16)
                    if m < nb:
                        dqc = _dot_tn(ds, kb)
                        dq3[m] = dqc if dq3[m] is None else dq3[m] + dqc
                    if n >= 0:
                        dk[kvh][n] = dk[kvh][n] + _dot_nn(ds, q3)
                        dv[kvh][n] = dv[kvh][n] + _dot_nn(prob.astype(BF16), do3)
            for gi, h in enumerate(hs):
                dq_heads[h] = jnp.concatenate([dq3[m][BLK * gi:BLK * (gi + 1)] for m in range(nb)], axis=0)
                psink = jnp.exp(sink_ref[h] - lse_e[h:h + 1, 0:tm])
                dsink = dsink + jnp.where(
                    lane1 == h, -jnp.sum(psink * del_e[h:h + 1, 0:tm], axis=-1, keepdims=True), 0.0)
        for j in range(N_Q // 2):
            js = slice(SLOT * j, SLOT * (j + 1))
            dqr, prod = _pair_bwd(_pack(dq_heads[2 * j], dq_heads[2 * j + 1], lane), p_ref[:, js], gq_ref[...],
                                  cos_ref[...], sin_ref[...], lane)
            dp_ref[:, js] = dqr.astype(BF16)
            dgq = dgq + _colsum(prod)
        dk_pair = _pack(jnp.concatenate(dk[0], axis=0), jnp.concatenate(dk[1], axis=0), lane)
        dkr, prod = _pair_bwd(dk_pair, p_ref[:, KO:KO + SLOT], gk_ref[...], cos_ref[...], sin_ref[...], lane)
        dp_ref[:, KO:KO + SLOT] = dkr.astype(BF16)
        dp_ref[:, VO:VO + SLOT] = _pack(jnp.concatenate(dv[0], axis=0), jnp.concatenate(dv[1], axis=0),
                                        lane).astype(BF16)
        dgq_ref[...] += dgq
        dgk_ref[...] += _colsum(prod)
        dsink_ref[...] += dsink

        dgqm = jnp.zeros((1, SLOT), F32)
        dq_heads = []
        for hm in range(N_MEMH):
            ms = slice(SLOT * hm, SLOT * (hm + 1))
            js = slice(MO + SLOT * (hm // 2), MO + SLOT * (hm // 2 + 1))
            os_ = slice(SLOT * ((N_Q + hm) // 2), SLOT * ((N_Q + hm) // 2 + 1))
            if hm % 2 == 0:
                qm_pair = _pair_fwd(p_ref[:, js], gqm_ref[...], None, None, lane)
                do_pair = dy_ref[:, os_].astype(F32)
            qm = _lo(qm_pair, hm % 2, lane).astype(BF16)
            dob = _lo(do_pair, hm % 2, lane).astype(BF16)
            kb, vb = mk_ref[:, ms], mv_ref[:, ms]
            s = _dot_nt(kb, qm) * SCALE
            prob = jnp.exp(s - lse_ref[N_Q + hm:N_Q + hm + 1, :])
            dpb = _dot_nt(vb, dob)
            ds = (prob * (dpb - del_ref[N_Q + hm:N_Q + hm + 1, :]) * SCALE).astype(BF16)
            dq_heads.append(_dot_tn(ds, kb))
            dmk_ref[:, ms] += _dot_nn(ds, qm)
            dmv_ref[:, ms] += _dot_nn(prob.astype(BF16), dob)
            if hm % 2 == 1:
                dqr, prod = _pair_bwd(_pack(dq_heads[-2], dq_heads[-1], lane), p_ref[:, js], gqm_ref[...],
                                      None, None, lane)
                dp_ref[:, js] = dqr.astype(BF16)
                dgqm = dgqm + _colsum(prod)
        dgqm_ref[...] += dgqm

    cur = lambda w: pl.BlockSpec((tm, w), lambda i: (i, 0))
    prev = lambda w: pl.BlockSpec((BLK, w), lambda i: (jnp.maximum(i * nb - 1, 0), 0))
    nxt = lambda w: pl.BlockSpec((BLK, w), lambda i: (jnp.minimum((i + 1) * nb, nblocks - 1), 0))
    full = lambda a: pl.BlockSpec(a.shape, lambda i: (0,) * a.ndim)
    acc = lambda r, w: pl.BlockSpec((r, w), lambda i: (0, 0))
    sd = jax.ShapeDtypeStruct
    stat = pl.BlockSpec((STAT_ROWS, tm), lambda i: (0, i))
    stat_n = pl.BlockSpec((STAT_ROWS, BLK), lambda i: (0, jnp.minimum((i + 1) * nb, nblocks - 1)))
    return _call(
        body, name="attn_bwd", grid=(nt,),
        in_specs=[cur(PH), prev(PH), nxt(PH), cur(YH), nxt(YH), stat, stat_n, stat, stat_n,
                  cur(SLOT), prev(SLOT), nxt(SLOT), cur(SLOT), prev(SLOT), nxt(SLOT),
                  full(gq), full(gk), pl.BlockSpec(memory_space=pltpu.SMEM), full(mk), full(mv), full(gqm)],
        out_specs=[cur(PH), acc(1, SLOT), acc(1, SLOT), acc(1, SLOT), acc(1, SLOT), acc(M, W), acc(M, W)],
        out_shape=[sd((S, PH), BF16), sd((1, SLOT), F32), sd((1, SLOT), F32), sd((1, SLOT), F32),
                   sd((1, SLOT), F32), sd((M, W), F32), sd((M, W), F32)],
        args=[p, p, p, do, do, lse, lse, delta, delta, cosT, cosT, cosT, sinT, sinT, sinT,
              gq, gk, sinks, mk, mv, gqm],
        rider=rider)


def _proj_bwd(dpu, dph, h, dh2, g, n, w_inp, tm):
    S, D = h.shape

    def body(dpu_ref, dph_ref, h_ref, dh2_ref, g_ref, n_ref, w_ref, dh_ref, dg_ref, dw_ref):
        i = pl.program_id(0)

        @pl.when(i == 0)
        def _():
            dg_ref[...] = jnp.zeros_like(dg_ref)
            dw_ref[...] = jnp.zeros_like(dw_ref)

        dpu, dph, nv = dpu_ref[...], dph_ref[...], n_ref[...]
        dn = _dot_nn(dpu, w_ref[0:PU, :]) + _dot_nn(dph, w_ref[PU:PP, :])
        dw_ref[0:PU, :] += _dot_tn(dpu, nv)
        dw_ref[PU:PP, :] += _dot_tn(dph, nv)
        hv = h_ref[...]
        dx, dgrow = _rms_bwd(dn, hv, _rms(hv), g_ref[...])
        dh_ref[...] = dh2_ref[...] + dx
        dg_ref[...] += _colsum(dgrow)

    cur = lambda w: pl.BlockSpec((tm, w), lambda i: (i, 0))
    sd = jax.ShapeDtypeStruct
    return pl.pallas_call(
        body, name="proj_bwd", grid=(S // tm,),
        in_specs=[cur(PU), cur(PH), cur(D), cur(D), pl.BlockSpec((1, D), lambda i: (0, 0)), cur(D),
                  pl.BlockSpec((PP, D), lambda i: (0, 0))],
        out_specs=[cur(D), pl.BlockSpec((1, D), lambda i: (0, 0)), pl.BlockSpec((PP, D), lambda i: (0, 0))],
        out_shape=[sd((S, D), F32), sd((1, D), F32), sd((PP, D), F32)],
        compiler_params=_cp(),
    )(dpu, dph, h, dh2, g, n, w_inp)


def _norm_bwd(dxn, h, g, tm):
    S, D = h.shape

    def body(d_ref, h_ref, g_ref, dh_ref, dg_ref):
        @pl.when(pl.program_id(0) == 0)
        def _():
            dg_ref[...] = jnp.zeros_like(dg_ref)

        hv = h_ref[...]
        dx, dgrow = _rms_bwd(d_ref[...], hv, _rms(hv), g_ref[...])
        dh_ref[...] = dx
        dg_ref[...] += _colsum(dgrow)

    cur = pl.BlockSpec((tm, D), lambda i: (i, 0))
    vec = pl.BlockSpec((1, D), lambda i: (0, 0))
    return pl.pallas_call(
        body, name="norm_bwd", grid=(S // tm,), in_specs=[cur, cur, vec], out_specs=[cur, vec],
        out_shape=[jax.ShapeDtypeStruct((S, D), F32), jax.ShapeDtypeStruct((1, D), F32)],
        compiler_params=_cp(),
    )(dxn, h, g)


def _loss_bwd(xn, h, g, target, tm):
    S, D = h.shape

    def body(y_ref, h_ref, g_ref, t_ref, loss_ref, dh_ref, dg_ref):
        @pl.when(pl.program_id(0) == 0)
        def _():
            dg_ref[...] = jnp.zeros_like(dg_ref)
            loss_ref[...] = jnp.zeros_like(loss_ref)

        err = y_ref[...] - t_ref[...]
        part = jnp.sum(jnp.mean(err * err, axis=-1, keepdims=True), axis=0, keepdims=True)
        loss_ref[...] += 0.5 * part
        hv = h_ref[...]
        dx, dgrow = _rms_bwd(err * (1.0 / D), hv, _rms(hv), g_ref[...])
        dh_ref[...] = dx
        dg_ref[...] += _colsum(dgrow)

    cur = pl.BlockSpec((tm, D), lambda i: (i, 0))
    vec = pl.BlockSpec((1, D), lambda i: (0, 0))
    return pl.pallas_call(
        body, name="loss_bwd", grid=(S // tm,), in_specs=[cur, cur, vec, cur],
        out_specs=[pl.BlockSpec((1, SLOT), lambda i: (0, 0)), cur, vec],
        out_shape=[jax.ShapeDtypeStruct((1, SLOT), F32), jax.ShapeDtypeStruct((S, D), F32),
                   jax.ShapeDtypeStruct((1, D), F32)],
        compiler_params=_cp(),
    )(xn, h, g, target)


def _all_gather_small(buf):
    _, R, W = buf.shape

    def body(in_ref, out_ref, ssem, rsem, lsem):
        x, y, c, _ = _place()
        me = 4 * x + 2 * y + c
        local = pltpu.make_async_copy(in_ref, out_ref.at[pl.ds(me, 1)], lsem)
        local.start()

        def copy(k, block):
            fx, fy, fc = (k >> 2) & 1, (k >> 1) & 1, k & 1
            peer = (x ^ fx, y ^ fy, c ^ fc)
            return pltpu.make_async_remote_copy(
                src_ref=in_ref, dst_ref=out_ref.at[pl.ds(block, 1)], send_sem=ssem.at[k - 1],
                recv_sem=rsem.at[k - 1], device_id=peer, device_id_type=MESH)

        sends = [copy(k, me) for k in range(1, 8)]
        for cp in sends:
            cp.start()
        for k in range(1, 8):
            copy(k, me ^ k).wait_recv()
        for cp in sends:
            cp.wait_send()
        local.wait()

    hbm = pl.BlockSpec(memory_space=pl.ANY)
    return pl.pallas_call(
        body, name="all_gather_small", in_specs=[hbm], out_specs=hbm,
        out_shape=jax.ShapeDtypeStruct((8, R, W), buf.dtype),
        scratch_shapes=[pltpu.SemaphoreType.DMA((7,)), pltpu.SemaphoreType.DMA((7,)), pltpu.SemaphoreType.DMA],
    )(buf)


def _row_tile(n, cap=1024):
    for t in range(min(n, cap) // 8 * 8, 7, -8):
        if n % t == 0:
            return t
    return n


def _sum4(own, recv):
    n, rows, D = own.shape
    tr = _row_tile(rows)

    def body(o_ref, r0_ref, r1_ref, r2_ref, out_ref):
        out_ref[...] = ((o_ref[...].astype(F32) + r0_ref[...].astype(F32)) + r1_ref[...].astype(F32)) \
            + r2_ref[...].astype(F32)

    def rspec(p):
        return pl.BlockSpec((None, None, None, tr, D), lambda k, i, p=p: (p, k, 0, i, 0))

    blk = pl.BlockSpec((None, tr, D), lambda k, i: (k, i, 0))
    return pl.pallas_call(
        body, name="sum4", grid=(n, rows // tr),
        in_specs=[blk, rspec(0), rspec(1), rspec(2)], out_specs=blk,
        out_shape=jax.ShapeDtypeStruct((n, rows, D), F32),
    )(own, recv, recv, recv)


def _adam_math(w, g, m, v):
    m = ADAM_B1 * m + (1.0 - ADAM_B1) * g
    v = ADAM_B2 * v + (1.0 - ADAM_B2) * (g * g)
    m_hat = m / (1.0 - ADAM_B1 ** ADAM_STEP)
    v_hat = v / (1.0 - ADAM_B2 ** ADAM_STEP)
    delta = -ADAM_LR * (m_hat / (jnp.sqrt(v_hat) + ADAM_EPS) + ADAM_WD * w)
    return delta, m, v


def _adam_fused(w, m, v, parts, theirs, sel, row0, nrows, rider=None):
    L, R, D = w.shape
    assert R == nrows and parts[0].shape[2] == D
    t = _row_tile(nrows if row0 == 0 else _gcd(row0, nrows), 512)

    def gspec(k):
        return pl.BlockSpec((None, t, D), lambda l, c: (sel, row0 // t + jnp.where(l == k, c, 0), 0))

    def body(*refs):
        w_ref, m_ref, v_ref = refs[:3]
        p_refs, q_refs = refs[3:3 + L], refs[3 + L:3 + 2 * L]
        g_ref, d_ref, nm_ref, nv_ref, g_scr = refs[3 + 2 * L:]
        l = pl.program_id(0)
        for k in range(L):
            @pl.when(l == k)
            def _(k=k):
                g_scr[...] = p_refs[k][...] + q_refs[k][...]

        g = g_scr[...]
        g_ref[...] = g
        d, m_, v_ = _adam_math(w_ref[...], g, m_ref[...], v_ref[...])
        d_ref[...] = d
        nm_ref[...] = m_
        nv_ref[...] = v_

    blk = pl.BlockSpec((None, t, D), lambda l, c: (l, c, 0))
    return _call(body, name="adam_fused", grid=(L, R // t),
                 in_specs=[blk] * 3 + [gspec(k) for k in range(L)] * 2, out_specs=[blk] * 4,
                 out_shape=[jax.ShapeDtypeStruct((L, R, D), F32)] * 4,
                 args=[w, m, v, *parts, *theirs], scratch=[pltpu.VMEM((t, D), F32)], rider=rider)


def _gcd(a, b):
    while b:
        a, b = b, a % b
    return a


def _small_sum_adam(g8, w, m, v):
    _, R, W = g8.shape

    def body(g_ref, w_ref, m_ref, v_ref, go_ref, d_ref, nm_ref, nv_ref):
        g = g_ref[0]
        for k in range(1, 8):
            g = g + g_ref[k]
        go_ref[...] = g
        d, m_, v_ = _adam_math(w_ref[...], g, m_ref[...], v_ref[...])
        d_ref[...] = d
        nm_ref[...] = m_
        nv_ref[...] = v_

    return pl.pallas_call(body, name="small_sum_adam",
                          out_shape=[jax.ShapeDtypeStruct((R, W), F32)] * 4)(g8, w, m, v)


def _pad_vec(v):
    return jnp.pad(v, (0, SLOT - v.shape[0]))[None, :]


class _Pack:
    def __init__(self, shapes):
        self.shapes = shapes
        self.sizes = [int(functools.reduce(lambda a, b: a * b, s, 1)) for s in shapes]
        total = sum(self.sizes)
        self.rows = -(-total // (8 * SLOT)) * 8
        self.pad = self.rows * SLOT - total

    def pack(self, arrs):
        flat = jnp.concatenate([a.reshape(-1).astype(F32) for a in arrs] + [jnp.zeros((self.pad,), F32)])
        return flat.reshape(self.rows, SLOT)

    def unpack(self, buf):
        flat, out, o = buf.reshape(-1), [], 0
        for s, n in zip(self.shapes, self.sizes):
            out.append(flat[o:o + n].reshape(s))
            o += n
        return out


def kernel(x, mem, positions, ffn1_norm, ffn1_w1, ffn1_w3, ffn1_w2, mix_norm, w_in, conv_w, conv_b, conv_ln_g, conv_ln_b, swa_q_norm, swa_k_norm, swa_sinks, mem_norm, w_mem_kv, mem_q_norm, mem_k_norm, w_out, ffn2_norm, ffn2_w1, ffn2_w3, ffn2_w2, final_norm, loss_target, m_ffn1_norm, m_ffn1_w1, m_ffn1_w3, m_ffn1_w2, m_mix_norm, m_w_in, m_conv_w, m_conv_b, m_conv_ln_g, m_conv_ln_b, m_swa_q_norm, m_swa_k_norm, m_swa_sinks, m_mem_norm, m_w_mem_kv, m_mem_q_norm, m_mem_k_norm, m_w_out, m_ffn2_norm, m_ffn2_w1, m_ffn2_w3, m_ffn2_w2, m_final_norm, v_ffn1_norm, v_ffn1_w1, v_ffn1_w3, v_ffn1_w2, v_mix_norm, v_w_in, v_conv_w, v_conv_b, v_conv_ln_g, v_conv_ln_b, v_swa_q_norm, v_swa_k_norm, v_swa_sinks, v_mem_norm, v_w_mem_kv, v_mem_q_norm, v_mem_k_norm, v_w_out, v_ffn2_norm, v_ffn2_w1, v_ffn2_w3, v_ffn2_w2, v_final_norm):
    names = ['ffn1_norm', 'ffn1_w1', 'ffn1_w3', 'ffn1_w2', 'mix_norm', 'w_in', 'conv_w', 'conv_b', 'conv_ln_g',
             'conv_ln_b', 'swa_q_norm', 'swa_k_norm', 'swa_sinks', 'mem_norm', 'w_mem_kv', 'mem_q_norm',
             'mem_k_norm', 'w_out', 'ffn2_norm', 'ffn2_w1', 'ffn2_w3', 'ffn2_w2', 'final_norm']
    loc = locals()
    W = {n: loc[n] for n in names}
    M1 = {n: loc['m_' + n] for n in names}
    V1 = {n: loc['v_' + n] for n in names}

    S, D = x.shape[1], x.shape[2]
    L = ffn1_norm.shape[0]
    Fs = ffn1_w1.shape[2]
    F = 4 * Fs
    Mlen = mem.shape[1]
    cw_sh = conv_w.shape[2]
    tm = 512 if S >= 2048 else 256
    tf = 1408 if F % 1408 == 0 else 256
    tfw = 256
    tmw = 2048 if S >= 2048 else 256
    tma = 1024 if S >= 2048 else 256
    x0 = x[0]
    mem0 = mem[0]
    target = loss_target[0]
    my_chip = 2 * lax.axis_index("x") + lax.axis_index("y")

    mkv_rows = w_mem_kv.shape[1] * MEM_KV // D
    r_in, r_out = D_IN // 4, D_MIX // 4
    rm = r_in + r_out + mkv_rows

    shard = lambda w: w.astype(BF16).reshape((1, 1) + w.shape)

    groups = []
    for l in range(L):
        groups.append([shard(ffn1_w1[l].T), shard(ffn1_w3[l].T), shard(ffn1_w2[l])])
        groups.append([shard(w_in[l].T), shard(w_out[l]), shard(w_mem_kv[l].reshape(mkv_rows, D))])
        groups.append([shard(ffn2_w1[l].T), shard(ffn2_w3[l].T), shard(ffn2_w2[l])])
    gathered = [None] * len(groups)
    cw_rows = -(-(L * CONV_WIDTH) // 8) * 8
    cw_pad = jnp.pad(conv_w.reshape(L * CONV_WIDTH, cw_sh), ((0, cw_rows - L * CONV_WIDTH), (0, SLOT - cw_sh)))
    *gathered[0], cw_g = _run_rider(_GatherHalves(groups[0] + [cw_pad.reshape(1, 1, cw_rows, SLOT)]),
                                    "all_gather_first")
    conv_wF = cw_g[0, :, :L * CONV_WIDTH, :cw_sh].reshape(4, L, CONV_WIDTH, cw_sh)
    conv_wF = jnp.moveaxis(conv_wF, 0, 2).reshape(L, CONV_WIDTH, 4 * cw_sh)
    conv_wP = jnp.pad(conv_wF, ((0, 0), (0, 32 - CONV_WIDTH), (0, 0)))

    def gather_rider(j):
        want = [k for k in [j + 1] if k < len(groups)]
        return (_Gather([b for k in want for b in groups[k]]), want) if want else (None, want)

    def keep(want, got):
        for n, k in enumerate(want):
            gathered[k] = got[3 * n:3 * n + 3]

    def ffn_weights(j):
        return tuple(g.reshape(F, D) for g in gathered[j])

    def mix_weights(l):
        g_in, g_out, g_mkv = gathered[3 * l + 1]
        w_inp = g_in.reshape(D_IN, D)
        w_outp = g_out.reshape(D_MIX, D)
        w_mkvp = jnp.pad(g_mkv.reshape(D, 2 * N_MEMH, HEAD_DIM),
                         ((0, 0), (0, 0), (0, SLOT - HEAD_DIM))).reshape(D, 2 * N_MEMH * SLOT)
        return w_inp, w_outp, w_mkvp

    inv_freq = ROPE_THETA ** (-jnp.arange(0, HEAD_DIM, 2, dtype=F32) / HEAD_DIM)
    invf = jnp.tile(inv_freq, SLOT // (HEAD_DIM // 2))[None, :]
    cosT, sinT = _rope_tables(positions.reshape(S, 1), invf, tm)

    row = lambda a, l: a[l][None, :]
    sinks_p = jnp.pad(swa_sinks, ((0, 0), (0, 8 - N_Q)))

    saved = []
    xin = x0
    xn = None
    for l in range(L):
        wf1 = ffn_weights(3 * l)
        rider, want = gather_rider(3 * l)
        (h1, a1, b1, t1), got = _ffn_fwd(xin, row(ffn1_norm, l), wf1, None, tm, tf, rider=rider)
        keep(want, got)
        w_inp, w_outp, w_mkvp = mix_weights(l)
        pu, p, n2 = _proj_fwd(h1, row(mix_norm, l), w_inp, tm)
        gk_m = _pad_vec(mem_k_norm[l])
        nm, mraw, mk, mv = _mem_kv_fwd(mem0, row(mem_norm, l), w_mkvp, gk_m)
        twice = lambda v: jnp.tile(v, 2)[None, :]
        gq, gk, gqm = twice(swa_q_norm[l]), twice(swa_k_norm[l]), twice(mem_q_norm[l])
        rider, want = gather_rider(3 * l + 1)
        (h2, y, yc, lse), got = _mixer_fwd(pu, p, h1, cosT, sinT, conv_wP[l], row(conv_b, l), row(conv_ln_g, l),
                                           row(conv_ln_b, l), gq, gk, sinks_p[l], mk, mv, gqm, w_outp, tm,
                                           rider=rider)
        keep(want, got)
        wf2 = ffn_weights(3 * l + 2)
        rider, want = gather_rider(3 * l + 2)
        (h3, a2, b2, t2, xn), got = _ffn_fwd(h2, row(ffn2_norm, l), wf2, row(final_norm, l), tm, tf, rider=rider)
        keep(want, got)
        saved.append(dict(xin=xin, h1=h1, a1=a1, b1=b1, pu=pu, p=p, n2=n2, nm=nm, mraw=mraw, mk=mk, mv=mv, gk_m=gk_m,
                          gq=gq, gk=gk, gqm=gqm, h2=h2, y=y, yc=yc, lse=lse, h3=h3, a2=a2, b2=b2, t1=t1, t2=t2,
                          wf1=wf1, wf2=wf2, w_inp=w_inp, w_outp=w_outp, w_mkvp=w_mkvp))
        xin = xn

    G = {n: [None] * L for n in names}
    ffn_bufs = [None] * (2 * L)
    mix_bufs = [None] * L
    parts, theirs = {}, {}
    pending_swap = []

    def scattered(key, buf, recv):
        own = lax.dynamic_index_in_dim(buf, my_chip, axis=1, keepdims=False)
        parts[key] = _sum4(own, recv)
        pending_swap.append(key)

    def comm_rider(scatter_buf):
        keys = list(pending_swap)
        pending_swap.clear()
        riders = ([_Scatter([scatter_buf])] if scatter_buf is not None else []) \
            + ([_Swap([parts[k] for k in keys])] if keys else [])
        if not riders:
            return None, lambda got: None
        multi = _Multi(riders)

        def store(got):
            outs = multi.split_outputs(got)
            if keys:
                for k, t in zip(keys, outs[-1]):
                    theirs[k] = t
            return outs[0][0] if scatter_buf is not None else None

        return multi, store

    dxn = None
    loss_part = None
    for l in reversed(range(L)):
        sv = saved[l]
        if l == L - 1:
            loss_part, dh3, G['final_norm'][l] = _loss_bwd(xn, sv['h3'], row(final_norm, l), target, tm)
        else:
            dh3, G['final_norm'][l] = _norm_bwd(dxn, sv['h3'], row(final_norm, l), tm)
        rider, store = comm_rider(ffn_bufs[2 * l + 2] if l < L - 1 else None)
        (dh2, G['ffn2_norm'][l], da, db, n, dy), got = _ffn_bwd_act(
            dh3, sv['h2'], row(ffn2_norm, l), sv['a2'], sv['b2'], sv['wf2'], tm, tf, rider=rider)
        recv = store(got)
        if recv is not None:
            scattered(('f', 2 * l + 2), ffn_bufs[2 * l + 2], recv)
        ffn_bufs[2 * l + 1] = _ffn_bwd_w(da, db, sv['t2'], n, dy, tmw, tfw)[0].reshape(3, 4, Fs, D)
        dyc, do, delta, dwo, G['conv_ln_g'][l], G['conv_ln_b'][l] = _outproj_bwd(
            dh2, sv['y'], sv['yc'], row(conv_ln_g, l), row(conv_ln_b, l), sv['w_outp'], tm)
        rider, store = comm_rider(ffn_bufs[2 * l + 1])
        (dph, dgq, dgk, dgqm, dsink, dmk, dmv), got = _attn_bwd(
            sv['p'], do, sv['lse'], delta, cosT, sinT, sv['gq'], sv['gk'], sinks_p[l],
            sv['mk'], sv['mv'], sv['gqm'], tma, rider=rider)
        scattered(('f', 2 * l + 1), ffn_bufs[2 * l + 1], store(got))
        dpu, dcw, G['conv_b'][l] = _conv_bwd(sv['pu'], dyc, conv_wP[l], tma)
        dwm, G['mem_norm'][l], dgk_m = _mem_kv_bwd(dmk, dmv, sv['mraw'], sv['nm'], mem0, row(mem_norm, l),
                                                   sv['w_mkvp'], sv['gk_m'])
        dh1, G['mix_norm'][l], dwi = _proj_bwd(dpu, dph, sv['h1'], dh2, row(mix_norm, l), sv['n2'], sv['w_inp'], tm)
        dwiT = dwi.reshape(4, r_in, D)
        dwoF = dwo.reshape(4, r_out, D)
        dwmF = dwm.reshape(D, 2 * N_MEMH, SLOT)[:, :, :HEAD_DIM].reshape(4, mkv_rows, D)
        mix_bufs[l] = jnp.concatenate([dwiT, dwoF, dwmF], axis=1).astype(BF16).reshape(1, 4, rm, D)
        rider, store = comm_rider(mix_bufs[l]) if l > 0 else (None, None)
        (dxl, G['ffn1_norm'][l], da, db, n, dy), got = _ffn_bwd_act(
            dh1, sv['xin'], row(ffn1_norm, l), sv['a1'], sv['b1'], sv['wf1'], tm, tf, rider=rider)
        if l > 0:
            scattered(('m', l), mix_bufs[l], store(got))
            rider, store = None, None
        else:
            rider, store = comm_rider(mix_bufs[l])
        dwf, got = _ffn_bwd_w(da, db, sv['t1'], n, dy, tmw, tfw, rider=rider)
        if l == 0:
            scattered(('m', l), mix_bufs[l], store(got))
        ffn_bufs[2 * l] = dwf.reshape(3, 4, Fs, D)
        dxn = dxl
        G['conv_w'][l] = dcw[:CONV_WIDTH]
        G['swa_q_norm'][l] = dgq[0, :HEAD_DIM] + dgq[0, HEAD_DIM:]
        G['swa_k_norm'][l] = dgk[0, :HEAD_DIM] + dgk[0, HEAD_DIM:]
        G['mem_q_norm'][l] = dgqm[0, :HEAD_DIM] + dgqm[0, HEAD_DIM:]
        G['mem_k_norm'][l] = dgk_m[0, :HEAD_DIM]
        G['swa_sinks'][l] = dsink[0, :N_Q]
    grad_x = dxn[None]
    loss = lax.psum(loss_part[0, 0], AXES)
    rider, store = comm_rider(ffn_bufs[0])
    scattered(('f', 0), ffn_bufs[0], store(_run_rider(rider, "scatter_last")))
    rider, store = comm_rider(None)
    store(_run_rider(rider, "swap_last"))

    small = ['ffn1_norm', 'mix_norm', 'conv_b', 'conv_ln_g', 'conv_ln_b', 'swa_q_norm', 'swa_k_norm', 'swa_sinks',
             'mem_norm', 'mem_q_norm', 'mem_k_norm', 'ffn2_norm', 'final_norm']
    gsmall = [jnp.stack([G[n][l].reshape(-1) for l in range(L)]) for n in small]
    gcw = jnp.stack(G['conv_w'])
    cw_cols = 4 * cw_sh
    full_of = lambda a: lax.dynamic_update_slice(jnp.zeros((L, CONV_WIDTH, cw_cols), F32), a, (0, 0, my_chip * cw_sh))
    pk = _Pack([W[n].shape for n in small] + [(L, CONV_WIDTH, cw_cols)])
    g8 = _all_gather_small(pk.pack(gsmall + [gcw])[None])
    outs4 = _small_sum_adam(g8, pk.pack([W[n] for n in small] + [full_of(conv_w)]),
                            pk.pack([M1[n] for n in small] + [full_of(m_conv_w)]),
                            pk.pack([V1[n] for n in small] + [full_of(v_conv_w)]))
    un = [pk.unpack(o) for o in outs4]
    grads, deltas, new_m, new_v = {}, {}, {}, {}
    for k, n in enumerate(small):
        grads[n], deltas[n], new_m[n], new_v[n] = un[0][k], un[1][k], un[2][k], un[3][k]
    mine = lambda a: lax.dynamic_slice(a, (0, 0, my_chip * cw_sh), (L, CONV_WIDTH, cw_sh))
    grads['conv_w'], deltas['conv_w'], new_m['conv_w'], new_v['conv_w'] = [mine(u[-1]) for u in un]

    ffn1_k, ffn2_k = [('f', 2 * l) for l in range(L)], [('f', 2 * l + 1) for l in range(L)]
    mix_k = [('m', l) for l in range(L)]
    plan = [('ffn2_w1', ffn2_k, 0, 0, Fs, True), ('ffn2_w3', ffn2_k, 1, 0, Fs, True), ('ffn2_w2', ffn2_k, 2, 0, Fs, False),
            ('w_in', mix_k, 0, 0, r_in, True), ('w_out', mix_k, 0, r_in, r_out, False),
            ('w_mem_kv', mix_k, 0, r_in + r_out, mkv_rows, False),
            ('ffn1_w1', ffn1_k, 0, 0, Fs, True), ('ffn1_w3', ffn1_k, 1, 0, Fs, True), ('ffn1_w2', ffn1_k, 2, 0, Fs, False)]
    for n, keys, sel, row0, nrows, held_transposed in plan:
        shp = W[n].shape
        if held_transposed:
            view, back = (lambda a: jnp.swapaxes(a, 1, 2)), (lambda a: jnp.swapaxes(a, 1, 2))
        elif n == 'w_mem_kv':
            view, back = (lambda a: a.reshape(L, mkv_rows, D)), (lambda a: a.reshape(shp))
        else:
            view = back = lambda a: a
        res, _ = _adam_fused(view(W[n]), view(M1[n]), view(V1[n]), [parts[k] for k in keys],
                             [theirs[k] for k in keys], sel, row0, nrows)
        grads[n], deltas[n], new_m[n], new_v[n] = [back(r) for r in res]

    return (loss, grad_x, *[grads[n] for n in names], *[deltas[n] for n in names],
            *[new_m[n] for n in names], *[new_v[n] for n in names])
```

```python
import functools

import jax
import jax.numpy as jnp
from jax import lax
from jax.experimental import pallas as pl
from jax.experimental.pallas import tpu as pltpu

F32 = jnp.float32
BF16 = jnp.bfloat16
MESH = pl.DeviceIdType.MESH
AXES = ("x", "y", "c")

EPS = 1e-6
HEAD_DIM = 64
SLOT = 128
CONV_CH = 384
CONV_WIDTH = 31
N_Q, N_KV, N_MEMH = 6, 2, 4
GROUP = N_Q // N_KV
BLK = 128
HALO = 32
CONV_ROWS = 64
FFN_CHUNK = 256
ROPE_THETA = 10000.0
SCALE = HEAD_DIM ** -0.5
NEG = -1e30

N_HEADS_IN = N_Q + 2 * N_KV + N_MEMH
PU = 2 * CONV_CH
PH = HEAD_DIM * N_HEADS_IN
PP = PU + PH
QO = 0
KO = QO + HEAD_DIM * N_Q
VO = KO + HEAD_DIM * N_KV
MO = VO + HEAD_DIM * N_KV
NH = N_Q + N_MEMH
STAT_ROWS = 16
YH = HEAD_DIM * NH
YP = CONV_CH + YH
YS = CONV_CH
YM = YS + HEAD_DIM * N_Q
D_IN = PP
D_MIX = YP
MEM_KV = 2 * HEAD_DIM * N_MEMH

ADAM_LR, ADAM_B1, ADAM_B2, ADAM_EPS, ADAM_WD, ADAM_STEP = 0.001, 0.9, 0.999, 1e-08, 0.01, 10

VMEM_LIMIT_MB = 56


def _cp(mb=VMEM_LIMIT_MB):
    return pltpu.CompilerParams(vmem_limit_bytes=mb * 1024 * 1024)


def _dot_nn(a, b):
    return lax.dot_general(a, b, (((1,), (0,)), ((), ())), preferred_element_type=F32)


def _dot_nt(a, b):
    return lax.dot_general(a, b, (((1,), (1,)), ((), ())), preferred_element_type=F32)


def _dot_tn(a, b):
    return lax.dot_general(a, b, (((0,), (0,)), ((), ())), preferred_element_type=F32)


def _sigmoid(x):
    return 1.0 / (1.0 + jnp.exp(-x))


def _rms(x):
    return lax.rsqrt(jnp.mean(x * x, axis=-1, keepdims=True) + EPS)


def _rms_bwd(dn, x, r, g):
    xhat = x * r
    dxhat = dn * g
    dx = r * (dxhat - xhat * jnp.mean(dxhat * xhat, axis=-1, keepdims=True))
    return dx, dn * xhat


def _colsum(v):
    return jnp.sum(v, axis=0, keepdims=True)


def _lane(n):
    return lax.broadcasted_iota(jnp.int32, (n, SLOT), 1)


def _slot_rms(xs):
    return lax.rsqrt(jnp.sum(xs * xs, axis=-1, keepdims=True) * (1.0 / HEAD_DIM) + EPS)


def _slot_norm(xs, g):
    return xs * _slot_rms(xs) * g


def _slot_norm_bwd(dout, xs, g):
    r = _slot_rms(xs)
    xhat = xs * r
    dxhat = dout * g
    dx = r * (dxhat - xhat * (jnp.sum(dxhat * xhat, axis=-1, keepdims=True) * (1.0 / HEAD_DIM)))
    return dx, dout * xhat


def _halves(v, lane):
    lo = jnp.sum(jnp.where(lane < HEAD_DIM, v, 0.0), axis=-1, keepdims=True)
    hi = jnp.sum(jnp.where(lane < HEAD_DIM, 0.0, v), axis=-1, keepdims=True)
    return jnp.where(lane < HEAD_DIM, lo, hi)


def _pair_rms(x, lane):
    return lax.rsqrt(_halves(x * x, lane) * (1.0 / HEAD_DIM) + EPS)


def _pair_partner(v, lane):
    return jnp.where((lane & (HEAD_DIM - 1)) < HEAD_DIM // 2,
                     pltpu.roll(v, SLOT - HEAD_DIM // 2, 1), pltpu.roll(v, HEAD_DIM // 2, 1))


def _pair_fwd(x, g2, cosv, sinv, lane):
    xn = x * _pair_rms(x, lane) * g2
    if cosv is None:
        return xn
    return xn * cosv + _pair_partner(xn, lane) * sinv


def _pair_bwd(dout, x, g2, cosv, sinv, lane):
    if cosv is not None:
        dout = dout * cosv + _pair_partner(dout * sinv, lane)
    r = _pair_rms(x, lane)
    xhat = x * r
    dxhat = dout * g2
    dx = r * (dxhat - xhat * (_halves(dxhat * xhat, lane) * (1.0 / HEAD_DIM)))
    return dx, dout * xhat


def _lo(x, half, lane):
    if half:
        x = pltpu.roll(x, HEAD_DIM, 1)
    return jnp.where(lane < HEAD_DIM, x, 0.0)


def _pack(even, odd, lane):
    return jnp.where(lane < HEAD_DIM, even, pltpu.roll(odd, HEAD_DIM, 1))


def _place():
    x, y, c = lax.axis_index("x"), lax.axis_index("y"), lax.axis_index("c")
    chips = [(1 - x, y), (x, 1 - y), (1 - x, 1 - y)]
    return x, y, c, chips


class _Gather:
    tag = "_gather"

    def __init__(self, bufs):
        self.bufs = list(bufs)
        nb = len(self.bufs)
        self.out_shape = [jax.ShapeDtypeStruct((b.shape[0], 4) + b.shape[2:], b.dtype) for b in self.bufs]
        self.sems = [pltpu.SemaphoreType.DMA((3 * nb,)), pltpu.SemaphoreType.DMA((3 * nb,)),
                     pltpu.SemaphoreType.DMA((nb,))]

    def _copies(self, ins, outs, sems):
        ssem, rsem, lsem = sems
        nb = len(self.bufs)
        x, y, c, chips = _place()
        mine = 2 * x + y

        def copy(b, p, shard):
            return pltpu.make_async_remote_copy(
                src_ref=ins[b], dst_ref=outs[b].at[:, pl.ds(shard, 1)],
                send_sem=ssem.at[3 * b + p], recv_sem=rsem.at[3 * b + p],
                device_id=(chips[p][0], chips[p][1], c), device_id_type=MESH)

        local = [pltpu.make_async_copy(ins[b], outs[b].at[:, pl.ds(mine, 1)], lsem.at[b]) for b in range(nb)]
        sends = [copy(b, p, mine) for b in range(nb) for p in range(3)]
        recvs = [copy(b, p, 2 * chips[p][0] + chips[p][1]) for b in range(nb) for p in range(3)]
        return local, sends, recvs

    def start(self, ins, outs, sems):
        local, sends, _ = self._copies(ins, outs, sems)
        for cp in local + sends:
            cp.start()

    def wait(self, ins, outs, sems):
        local, sends, recvs = self._copies(ins, outs, sems)
        for cp in recvs:
            cp.wait_recv()
        for cp in sends:
            cp.wait_send()
        for cp in local:
            cp.wait()


class _GatherHalves:
    tag = "_gather_halves"

    def __init__(self, bufs):
        self.bufs = list(bufs)
        nb = len(self.bufs)
        assert all(b.shape[2] % 32 == 0 for b in self.bufs)
        self.out_shape = [jax.ShapeDtypeStruct((b.shape[0], 4) + b.shape[2:], b.dtype) for b in self.bufs]
        self.sems = [pltpu.SemaphoreType.DMA((3 * nb,)), pltpu.SemaphoreType.DMA((3 * nb,)),
                     pltpu.SemaphoreType.DMA((3 * nb,)), pltpu.SemaphoreType.DMA((3 * nb,)),
                     pltpu.SemaphoreType.DMA((nb,))]

    def start(self, ins, outs, sems):
        ssem, rsem, fsem, gsem, lsem = sems
        nb = len(self.bufs)
        x, y, c, chips = _place()
        mine = 2 * x + y
        shard_of = [2 * cx + cy for cx, cy in chips]

        def half(b, h):
            n = self.bufs[b].shape[2] // 2
            return pl.ds(pl.multiple_of(h * n, 16), n)

        def over_links(b, p, shard, h):
            return pltpu.make_async_remote_copy(
                src_ref=ins[b].at[:, :, half(b, h)], dst_ref=outs[b].at[:, pl.ds(shard, 1), half(b, h)],
                send_sem=ssem.at[3 * b + p], recv_sem=rsem.at[3 * b + p],
                device_id=(chips[p][0], chips[p][1], c), device_id_type=MESH)

        def between_cores(b, p, h):
            blk = outs[b].at[:, pl.ds(shard_of[p], 1), half(b, h)]
            return pltpu.make_async_remote_copy(
                src_ref=blk, dst_ref=blk, send_sem=fsem.at[3 * b + p], recv_sem=gsem.at[3 * b + p],
                device_id=(x, y, 1 - c), device_id_type=MESH)

        local = [pltpu.make_async_copy(ins[b], outs[b].at[:, pl.ds(mine, 1)], lsem.at[b]) for b in range(nb)]
        sends = [over_links(b, p, mine, c) for b in range(nb) for p in range(3)]
        for cp in local + sends:
            cp.start()
        handed = []
        for b in range(nb):
            for p in range(3):
                over_links(b, p, shard_of[p], c).wait_recv()
                cp = between_cores(b, p, c)
                cp.start()
                handed.append(cp)
        for b in range(nb):
            for p in range(3):
                between_cores(b, p, 1 - c).wait_recv()
        for cp in sends + handed:
            cp.wait_send()
        for cp in local:
            cp.wait()

    def wait(self, ins, outs, sems):
        pass


class _Scatter:
    tag = "_scatter"

    def __init__(self, bufs):
        self.bufs = list(bufs)
        nb = len(self.bufs)
        self.out_shape = [jax.ShapeDtypeStruct((3, b.shape[0], 1) + b.shape[2:], b.dtype) for b in self.bufs]
        self.sems = [pltpu.SemaphoreType.DMA((3 * nb,)), pltpu.SemaphoreType.DMA((3 * nb,))]

    def _copies(self, ins, outs, sems):
        ssem, rsem = sems
        x, y, c, chips = _place()

        def copy(b, p):
            shard = 2 * chips[p][0] + chips[p][1]
            return pltpu.make_async_remote_copy(
                src_ref=ins[b].at[:, pl.ds(shard, 1)], dst_ref=outs[b].at[p],
                send_sem=ssem.at[3 * b + p], recv_sem=rsem.at[3 * b + p],
                device_id=(chips[p][0], chips[p][1], c), device_id_type=MESH)

        return [copy(b, p) for b in range(len(self.bufs)) for p in range(3)]

    def start(self, ins, outs, sems):
        for cp in self._copies(ins, outs, sems):
            cp.start()

    def wait(self, ins, outs, sems):
        cps = self._copies(ins, outs, sems)
        for cp in cps:
            cp.wait_recv()
        for cp in cps:
            cp.wait_send()


class _Swap:
    tag = "_swap"

    def __init__(self, bufs):
        self.bufs = list(bufs)
        nb = len(self.bufs)
        self.out_shape = [jax.ShapeDtypeStruct(b.shape, b.dtype) for b in self.bufs]
        self.sems = [pltpu.SemaphoreType.DMA((nb,)), pltpu.SemaphoreType.DMA((nb,))]

    def _copies(self, ins, outs, sems):
        ssem, rsem = sems
        x, y, c, _ = _place()
        return [pltpu.make_async_remote_copy(src_ref=ins[b], dst_ref=outs[b], send_sem=ssem.at[b],
                                             recv_sem=rsem.at[b], device_id=(x, y, 1 - c), device_id_type=MESH)
                for b in range(len(self.bufs))]

    def start(self, ins, outs, sems):
        for cp in self._copies(ins, outs, sems):
            cp.start()

    def wait(self, ins, outs, sems):
        cps = self._copies(ins, outs, sems)
        for cp in cps:
            cp.wait_recv()
        for cp in cps:
            cp.wait_send()


class _Multi:
    def __init__(self, riders):
        self.riders = [r for r in riders if r is not None and r.bufs]
        self.tag = "".join(r.tag for r in self.riders)
        self.bufs = [b for r in self.riders for b in r.bufs]
        self.out_shape = [s for r in self.riders for s in r.out_shape]
        self.sems = [s for r in self.riders for s in r.sems]

    def _split(self, ins, outs, sems):
        ob, os_ = 0, 0
        for r in self.riders:
            nb, ns = len(r.bufs), len(r.sems)
            yield r, ins[ob:ob + nb], outs[ob:ob + nb], sems[os_:os_ + ns]
            ob, os_ = ob + nb, os_ + ns

    def start(self, ins, outs, sems):
        for r, i, o, s in self._split(ins, outs, sems):
            r.start(i, o, s)

    def wait(self, ins, outs, sems):
        for r, i, o, s in self._split(ins, outs, sems):
            r.wait(i, o, s)

    def split_outputs(self, got):
        res, ob = [], 0
        for r in self.riders:
            res.append(got[ob:ob + len(r.bufs)])
            ob += len(r.bufs)
        return res


def _run_rider(rider, name):
    nb = len(rider.bufs)

    def body(*refs):
        ins, outs, sems = refs[:nb], refs[nb:2 * nb], refs[2 * nb:]
        rider.start(ins, outs, sems)
        rider.wait(ins, outs, sems)

    hbm = pl.BlockSpec(memory_space=pl.ANY)
    return pl.pallas_call(body, name=name, in_specs=[hbm] * nb, out_specs=[hbm] * nb,
                          out_shape=rider.out_shape, scratch_shapes=rider.sems)(*rider.bufs)


def _call(body, *, name, grid, in_specs, out_specs, out_shape, args, scratch=(), rider=None):
    if rider is None:
        outs = pl.pallas_call(body, name=name, grid=grid, in_specs=list(in_specs), out_specs=list(out_specs),
                              out_shape=list(out_shape), scratch_shapes=list(scratch),
                              compiler_params=_cp())(*args)
        return list(outs), None
    n_in, n_out, n_scr, nb = len(in_specs), len(out_specs), len(scratch), len(rider.bufs)

    def wrapped(*refs):
        cuts = [n_in, nb, n_out, nb, n_scr]
        parts, o = [], 0
        for n in cuts:
            parts.append(refs[o:o + n])
            o += n
        ins, rin, outs, rout, scr = parts
        sems = refs[o:]
        ids = [pl.program_id(k) for k in range(len(grid))]
        first = functools.reduce(jnp.logical_and, [i == 0 for i in ids])
        last = functools.reduce(jnp.logical_and, [i == n - 1 for i, n in zip(ids, grid)])

        @pl.when(first)
        def _():
            rider.start(rin, rout, sems)

        body(*ins, *outs, *scr)

        @pl.when(last)
        def _():
            rider.wait(rin, rout, sems)

    hbm = pl.BlockSpec(memory_space=pl.ANY)
    res = pl.pallas_call(
        wrapped, name=name + rider.tag, grid=grid,
        in_specs=list(in_specs) + [hbm] * nb, out_specs=list(out_specs) + [hbm] * nb,
        out_shape=list(out_shape) + rider.out_shape, scratch_shapes=list(scratch) + rider.sems,
        compiler_params=_cp())(*args, *rider.bufs)
    return list(res[:n_out]), list(res[n_out:])


def _rope_tables(pos, invf, tm):
    S = pos.shape[0]

    def body(pos_ref, f_ref, cos_ref, sin_ref):
        ang = pos_ref[...].astype(F32) * f_ref[...]
        lane = _lane(tm)
        cos_ref[...] = jnp.cos(ang)
        s = jnp.sin(ang)
        sin_ref[...] = jnp.where((lane & (HEAD_DIM - 1)) < HEAD_DIM // 2, -s, s)

    return pl.pallas_call(
        body, name="rope_tables", grid=(S // tm,),
        in_specs=[pl.BlockSpec((tm, 1), lambda i: (i, 0)), pl.BlockSpec((1, SLOT), lambda i: (0, 0))],
        out_specs=[pl.BlockSpec((tm, SLOT), lambda i: (i, 0))] * 2,
        out_shape=[jax.ShapeDtypeStruct((S, SLOT), F32)] * 2,
    )(pos, invf)


def _ffn_fwd(x, g, wf, gfin, tm, tf, rider=None):
    S, D = x.shape
    F = wf[0].shape[0]
    nf = F // tf
    final = gfin is not None

    chunks = [(c, min(FFN_CHUNK, tf - c)) for c in range(0, tf, FFN_CHUNK)]

    def body(*refs):
        if final:
            x_ref, g_ref, w1_ref, w3_ref, w2_ref, gf_ref, h_ref, a_ref, b_ref, t_ref, xn_ref, n_scr, acc = refs
        else:
            x_ref, g_ref, w1_ref, w3_ref, w2_ref, h_ref, a_ref, b_ref, t_ref, n_scr, acc = refs
        j = pl.program_id(1)

        @pl.when(j == 0)
        def _():
            xv = x_ref[...]
            n_scr[...] = (xv * _rms(xv) * g_ref[...]).astype(BF16)
            acc[...] = jnp.zeros_like(acc)

        n = n_scr[...]
        for c0, cw in chunks:
            cols = slice(c0, c0 + cw)
            a = _dot_nt(n, w1_ref[cols, :])
            b = _dot_nt(n, w3_ref[cols, :])
            a_ref[:, cols] = a.astype(BF16)
            b_ref[:, cols] = b.astype(BF16)
            t_ref[:, cols] = (a * _sigmoid(a) * b).astype(BF16)
        acc[...] += _dot_nn(t_ref[...], w2_ref[...])

        @pl.when(j == nf - 1)
        def _():
            h = x_ref[...] + 0.5 * acc[...]
            h_ref[...] = h
            if final:
                xn_ref[...] = h * _rms(h) * gf_ref[...]

    def wspec(k):
        return pl.BlockSpec((tf, D), lambda i, j: (j, 0))

    row = pl.BlockSpec((tm, D), lambda i, j: (i, 0))
    vec = pl.BlockSpec((1, D), lambda i, j: (0, 0))
    act = pl.BlockSpec((tm, tf), lambda i, j: (i, j))
    in_specs = [row, vec, wspec(0), wspec(1), wspec(2)] + ([vec] if final else [])
    out_specs = [row, act, act, act] + ([row] if final else [])
    out_shape = [jax.ShapeDtypeStruct((S, D), F32)] + [jax.ShapeDtypeStruct((S, F), BF16)] * 3 \
        + ([jax.ShapeDtypeStruct((S, D), F32)] if final else [])
    args = [x, g, *wf] + ([gfin] if final else [])
    return _call(body, name="ffn_fwd_final" if final else "ffn_fwd", grid=(S // tm, nf),
                 in_specs=in_specs, out_specs=out_specs, out_shape=out_shape, args=args,
                 scratch=[pltpu.VMEM((tm, D), BF16), pltpu.VMEM((tm, D), F32)], rider=rider)


def _ffn_bwd_act(dh, x, g, a, b, wf, tm, tf, rider=None):
    S, D = x.shape
    F = wf[0].shape[0]
    nf = F // tf

    chunks = [(c, min(FFN_CHUNK, tf - c)) for c in range(0, tf, FFN_CHUNK)]

    def body(dh_ref, x_ref, g_ref, a_ref, b_ref, w1_ref, w3_ref, w2_ref,
             dx_ref, dg_ref, da_ref, db_ref, n_ref, dy_ref, acc):
        i, j = pl.program_id(0), pl.program_id(1)

        @pl.when(j == 0)
        def _():
            xv = x_ref[...]
            n_ref[...] = (xv * _rms(xv) * g_ref[...]).astype(BF16)
            dy_ref[...] = (0.5 * dh_ref[...]).astype(BF16)
            acc[...] = jnp.zeros_like(acc)

            @pl.when(i == 0)
            def _():
                dg_ref[...] = jnp.zeros_like(dg_ref)

        dyv = dy_ref[...]
        for c0, cw in chunks:
            cols = slice(c0, c0 + cw)
            av = a_ref[:, cols].astype(F32)
            bv = b_ref[:, cols].astype(F32)
            sg = _sigmoid(av)
            dt = _dot_nt(dyv, w2_ref[cols, :])
            db_ref[:, cols] = (dt * (av * sg)).astype(BF16)
            da_ref[:, cols] = (dt * bv * (sg * (1.0 + av * (1.0 - sg)))).astype(BF16)
        acc[...] += _dot_nn(da_ref[...], w1_ref[...]) + _dot_nn(db_ref[...], w3_ref[...])

        @pl.when(j == nf - 1)
        def _():
            xv = x_ref[...]
            dx, dgrow = _rms_bwd(acc[...], xv, _rms(xv), g_ref[...])
            dx_ref[...] = dh_ref[...] + dx
            dg_ref[...] += _colsum(dgrow)

    def wspec(k):
        return pl.BlockSpec((tf, D), lambda i, j: (j, 0))

    row = pl.BlockSpec((tm, D), lambda i, j: (i, 0))
    vec = pl.BlockSpec((1, D), lambda i, j: (0, 0))
    act = pl.BlockSpec((tm, tf), lambda i, j: (i, j))
    sd = lambda shp, dt: jax.ShapeDtypeStruct(shp, dt)
    return _call(body, name="ffn_bwd_act", grid=(S // tm, nf),
                 in_specs=[row, row, vec, act, act, wspec(0), wspec(1), wspec(2)],
                 out_specs=[row, vec, act, act, row, row],
                 out_shape=[sd((S, D), F32), sd((1, D), F32), sd((S, F), BF16), sd((S, F), BF16),
                            sd((S, D), BF16), sd((S, D), BF16)],
                 args=[dh, x, g, a, b, *wf], scratch=[pltpu.VMEM((tm, D), F32)], rider=rider)


def _ffn_bwd_w(da, db, t, n, dy, tm, tf, rider=None):
    S, F = da.shape
    D = n.shape[1]
    nt = S // tm

    def body(da_ref, db_ref, t_ref, n_ref, dy_ref, out_ref, acc):
        i = pl.program_id(1)

        @pl.when(i == 0)
        def _():
            acc[...] = jnp.zeros_like(acc)

        nv = n_ref[...]
        acc[0] += _dot_tn(da_ref[...], nv)
        acc[1] += _dot_tn(db_ref[...], nv)
        acc[2] += _dot_tn(t_ref[...], dy_ref[...])

        @pl.when(i == nt - 1)
        def _():
            out_ref[...] = acc[...].astype(BF16)

    act = pl.BlockSpec((tm, tf), lambda j, i: (i, j))
    row = pl.BlockSpec((tm, D), lambda j, i: (i, 0))
    outs, got = _call(body, name="ffn_bwd_w", grid=(F // tf, nt),
                      in_specs=[act, act, act, row, row],
                      out_specs=[pl.BlockSpec((3, tf, D), lambda j, i: (0, j, 0))],
                      out_shape=[jax.ShapeDtypeStruct((3, F, D), BF16)],
                      args=[da, db, t, n, dy], scratch=[pltpu.VMEM((3, tf, D), F32)], rider=rider)
    return outs[0], got


def _proj_fwd(h, g, w_inp, tm):
    S, D = h.shape

    def body(h_ref, g_ref, w_ref, pu_ref, ph_ref, n_ref):
        hv = h_ref[...]
        n = (hv * _rms(hv) * g_ref[...]).astype(BF16)
        n_ref[...] = n
        pu_ref[...] = _dot_nt(n, w_ref[0:PU, :])
        ph_ref[...] = _dot_nt(n, w_ref[PU:PP, :])

    cur = lambda w: pl.BlockSpec((tm, w), lambda i: (i, 0))
    return pl.pallas_call(
        body, name="proj_fwd", grid=(S // tm,),
        in_specs=[cur(D), pl.BlockSpec((1, D), lambda i: (0, 0)), pl.BlockSpec((PP, D), lambda i: (0, 0))],
        out_specs=[cur(PU), cur(PH), cur(D)],
        out_shape=[jax.ShapeDtypeStruct((S, PU), F32), jax.ShapeDtypeStruct((S, PH), F32),
                   jax.ShapeDtypeStruct((S, D), BF16)],
        compiler_params=_cp(),
    )(h, g, w_inp)


def _glu(u):
    return u[:, :CONV_CH] * _sigmoid(u[:, CONV_CH:2 * CONV_CH])


def _shifted_copies(ext8):
    n = ext8.shape[1]
    for b in range(1, 8):
        ext8[b, 0:n - 8, :] = ext8[0, b:b + n - 8, :]


def _window(ext8, off, rows, r0=0):
    return ext8[off % 8, pl.ds(r0 + (off - off % 8), rows), :]


def _layer_norm_stats(yc):
    mu = jnp.mean(yc, axis=-1, keepdims=True)
    d = yc - mu
    rstd = lax.rsqrt(jnp.mean(d * d, axis=-1, keepdims=True) + EPS)
    return d * rstd, rstd


def _mem_kv_fwd(mem, g, w_mkvp, gk):
    M, D = mem.shape
    W = SLOT * N_MEMH

    def body(mem_ref, g_ref, w_ref, gk_ref, nm_ref, raw_ref, mk_ref, mv_ref):
        mv_ = mem_ref[...]
        nm = (mv_ * _rms(mv_) * g_ref[...]).astype(BF16)
        nm_ref[...] = nm
        raw = _dot_nn(nm, w_ref[...])
        raw_ref[...] = raw
        for hh in range(N_MEMH):
            sl = slice(SLOT * hh, SLOT * (hh + 1))
            mk_ref[:, sl] = _slot_norm(raw[:, sl], gk_ref[...]).astype(BF16)
        mv_ref[...] = raw[:, W:].astype(BF16)

    sd = jax.ShapeDtypeStruct
    return pl.pallas_call(
        body, name="mem_kv_fwd",
        out_shape=[sd((M, D), BF16), sd((M, 2 * W), F32), sd((M, W), BF16), sd((M, W), BF16)],
        compiler_params=_cp(),
    )(mem, g, w_mkvp, gk)


def _mem_kv_bwd(dmk, dmv, raw, nm, mem, g, w_mkvp, gk):
    M, D = mem.shape
    W = SLOT * N_MEMH

    def body(dmk_ref, dmv_ref, raw_ref, nm_ref, mem_ref, g_ref, w_ref, gk_ref, dw_ref, dg_ref, dgk_ref, draw):
        dgk = jnp.zeros((1, SLOT), F32)
        for hh in range(N_MEMH):
            sl = slice(SLOT * hh, SLOT * (hh + 1))
            dx, prod = _slot_norm_bwd(dmk_ref[:, sl], raw_ref[:, sl], gk_ref[...])
            draw[:, sl] = dx.astype(BF16)
            dgk = dgk + _colsum(prod)
        dgk_ref[...] = dgk
        draw[:, W:] = dmv_ref[...].astype(BF16)
        dr = draw[...]
        dw_ref[...] = _dot_tn(nm_ref[...], dr)
        dnm = _dot_nt(dr, w_ref[...])
        mv_ = mem_ref[...]
        dg_ref[...] = _colsum(dnm * (mv_ * _rms(mv_)))

    sd = jax.ShapeDtypeStruct
    return pl.pallas_call(
        body, name="mem_kv_bwd",
        out_shape=[sd((D, 2 * W), F32), sd((1, D), F32), sd((1, SLOT), F32)],
        scratch_shapes=[pltpu.VMEM((M, 2 * W), BF16)],
        compiler_params=_cp(),
    )(dmk, dmv, raw, nm, mem, g, w_mkvp, gk)


def _mixer_fwd(pu, ph, h, cosT, sinT, conv_w, conv_b, ln_g, ln_b, gq, gk, sinks, mk, mv, gqm, w_outp, tm, rider=None):
    S, D = h.shape
    M = mk.shape[0]
    nb = tm // BLK
    nblocks = S // BLK

    def body(pu_ref, pup_ref, p_ref, ph_ref, h_ref, cos_ref, cosh_ref, sin_ref, sinh_ref, cw_ref, cb_ref,
             lg_ref, lb_ref, gq_ref, gk_ref, sink_ref, mk_ref, mv_ref, gqm_ref, wo_ref,
             h2_ref, y_ref, yc_ref, lse_ref, ext, y_scr):
        i = pl.program_id(0)
        not_first = (i > 0).astype(F32)
        lane = _lane(tm)
        lane_e = _lane(tm + BLK)

        ext[0, 0:HALO, :] = _glu(pup_ref[...]) * not_first
        ext[0, HALO:HALO + tm, :] = _glu(pu_ref[...])
        _shifted_copies(ext)

        def rows_chunk(r, carry):
            r0 = pl.multiple_of(r * CONV_ROWS, CONV_ROWS)
            yc = jnp.zeros((CONV_ROWS, CONV_CH), F32) + cb_ref[...]
            for k in range(CONV_WIDTH):
                yc = yc + cw_ref[k:k + 1, :] * _window(ext, HALO - (CONV_WIDTH - 1) + k, CONV_ROWS, r0)
            yc_ref[pl.ds(r0, CONV_ROWS), :] = yc
            z, _ = _layer_norm_stats(yc)
            ln = z * lg_ref[...] + lb_ref[...]
            y_scr[pl.ds(r0, CONV_ROWS), 0:CONV_CH] = (ln * _sigmoid(ln)).astype(BF16)
            return carry

        lax.fori_loop(0, tm // CONV_ROWS, rows_chunk, 0)

        cos_e = jnp.concatenate([cosh_ref[...], cos_ref[...]], axis=0)
        sin_e = jnp.concatenate([sinh_ref[...], sin_ref[...]], axis=0)
        qi = lax.broadcasted_iota(jnp.int32, (GROUP * BLK, 2 * BLK), 0) & (BLK - 1)
        kj = lax.broadcasted_iota(jnp.int32, (GROUP * BLK, 2 * BLK), 1)
        band = (kj > qi) & (kj <= qi + BLK)
        band0 = band & ((kj >= BLK) | (i > 0))
        lse = jnp.zeros((tm, SLOT), F32)
        k_pair = jnp.concatenate([ph_ref[:, KO:KO + SLOT], p_ref[:, KO:KO + SLOT]], axis=0)
        k_pair = _pair_fwd(k_pair, gk_ref[...], cos_e, sin_e, lane_e)
        v_pair = jnp.concatenate([ph_ref[:, VO:VO + SLOT], p_ref[:, VO:VO + SLOT]], axis=0)
        k_e = [_lo(k_pair, kvh, lane_e).astype(BF16) for kvh in range(N_KV)]
        v_e = [_lo(v_pair, kvh, lane_e).astype(BF16) for kvh in range(N_KV)]
        q_lo = []
        for j in range(N_Q // 2):
            q_pair = _pair_fwd(p_ref[:, QO + SLOT * j:QO + SLOT * (j + 1)], gq_ref[...],
                               cos_ref[...], sin_ref[...], lane)
            q_lo += [_lo(q_pair, 0, lane).astype(BF16), _lo(q_pair, 1, lane).astype(BF16)]
        outs = [[] for _ in range(N_Q)]
        lses = [[] for _ in range(N_Q)]
        for kvh in range(N_KV):
            hs = [GROUP * kvh + gi for gi in range(GROUP)]
            sink3 = jnp.concatenate([jnp.full((BLK, 1), sink_ref[h], F32) for h in hs], axis=0)
            for m in range(nb):
                rows = slice(BLK * m, BLK * (m + 1))
                win = slice(BLK * m, BLK * (m + 2))
                q3 = jnp.concatenate([q_lo[h][rows] for h in hs], axis=0)
                s = _dot_nt(q3, k_e[kvh][win]) * SCALE
                s = jnp.where(band0 if m == 0 else band, s, NEG)
                mx = jnp.maximum(jnp.max(s, axis=-1, keepdims=True), sink3)
                e = jnp.exp(s - mx)
                den = jnp.sum(e, axis=-1, keepdims=True) + jnp.exp(sink3 - mx)
                o3 = _dot_nn((e / den).astype(BF16), v_e[kvh][win])
                l3 = mx + jnp.log(den)
                for gi, h in enumerate(hs):
                    outs[h].append(o3[BLK * gi:BLK * (gi + 1)])
                    lses[h].append(l3[BLK * gi:BLK * (gi + 1)])
        for h in range(N_Q):
            lse = jnp.where(lane == h, jnp.concatenate(lses[h], axis=0), lse)
        for j in range(N_Q // 2):
            y_scr[:, YS + SLOT * j:YS + SLOT * (j + 1)] = _pack(
                jnp.concatenate(outs[2 * j], axis=0), jnp.concatenate(outs[2 * j + 1], axis=0), lane).astype(BF16)

        heads = []
        for hm in range(N_MEMH):
            ms = slice(SLOT * hm, SLOT * (hm + 1))
            if hm % 2 == 0:
                qm_pair = _pair_fwd(p_ref[:, MO + SLOT * (hm // 2):MO + SLOT * (hm // 2 + 1)], gqm_ref[...],
                                    None, None, lane)
            s = _dot_nt(_lo(qm_pair, hm % 2, lane).astype(BF16), mk_ref[:, ms]) * SCALE
            mx = jnp.max(s, axis=-1, keepdims=True)
            e = jnp.exp(s - mx)
            den = jnp.sum(e, axis=-1, keepdims=True)
            heads.append(_dot_nn((e / den).astype(BF16), mv_ref[:, ms]))
            lse = jnp.where(lane == N_Q + hm, mx + jnp.log(den), lse)
            if hm % 2 == 1:
                y_scr[:, YM + SLOT * (hm // 2):YM + SLOT * (hm // 2 + 1)] = _pack(heads[-2], heads[-1], lane).astype(BF16)
        lse_ref[...] = lse.T[0:STAT_ROWS, :]

        yv = y_scr[...]
        y_ref[...] = yv
        h2_ref[...] = h_ref[...] + _dot_nn(yv, wo_ref[...])

    cur = lambda w: pl.BlockSpec((tm, w), lambda i: (i, 0))
    prev = lambda w: pl.BlockSpec((BLK, w), lambda i: (jnp.maximum(i * nb - 1, 0), 0))
    full = lambda a: pl.BlockSpec(a.shape, lambda i: (0,) * a.ndim)
    sd = jax.ShapeDtypeStruct
    prev32 = pl.BlockSpec((HALO, PU), lambda i: (jnp.maximum(i * (tm // HALO) - 1, 0), 0))
    return _call(
        body, name="mixer_fwd", grid=(S // tm,),
        in_specs=[cur(PU), prev32, cur(PH), prev(PH), cur(D), cur(SLOT), prev(SLOT), cur(SLOT), prev(SLOT),
                  full(conv_w), full(conv_b), full(ln_g), full(ln_b), full(gq), full(gk),
                  pl.BlockSpec(memory_space=pltpu.SMEM), full(mk), full(mv), full(gqm), full(w_outp)],
        out_specs=[cur(D), cur(YP), cur(CONV_CH), pl.BlockSpec((STAT_ROWS, tm), lambda i: (0, i))],
        out_shape=[sd((S, D), F32), sd((S, YP), BF16), sd((S, CONV_CH), F32), sd((STAT_ROWS, S), F32)],
        args=[pu, pu, ph, ph, h, cosT, cosT, sinT, sinT, conv_w, conv_b, ln_g, ln_b, gq, gk, sinks, mk, mv, gqm,
              w_outp],
        scratch=[pltpu.VMEM((8, tm + HALO, CONV_CH), F32), pltpu.VMEM((tm, YP), BF16)], rider=rider)


def _outproj_bwd(dh2, y, yc, ln_g, ln_b, w_outp, tm):
    S, D = dh2.shape

    def body(dh_ref, y_ref, yc_ref, lg_ref, lb_ref, wo_ref, dyc_ref, do_ref, del_ref, dwo_ref, dlg_ref, dlb_ref):
        i = pl.program_id(0)

        @pl.when(i == 0)
        def _():
            dwo_ref[...] = jnp.zeros_like(dwo_ref)
            dlg_ref[...] = jnp.zeros_like(dlg_ref)
            dlb_ref[...] = jnp.zeros_like(dlb_ref)

        dhb = dh_ref[...].astype(BF16)
        yv = y_ref[...]
        dy = _dot_nt(dhb, wo_ref[...])
        dwo_ref[...] += _dot_tn(yv, dhb)

        z, rstd = _layer_norm_stats(yc_ref[...])
        ln = z * lg_ref[...] + lb_ref[...]
        sg = _sigmoid(ln)
        dln = dy[:, 0:CONV_CH] * (sg * (1.0 + ln * (1.0 - sg)))
        dlg_ref[...] += _colsum(dln * z)
        dlb_ref[...] += _colsum(dln)
        dz = dln * lg_ref[...]
        dyc_ref[...] = rstd * (dz - jnp.mean(dz, axis=-1, keepdims=True)
                               - z * jnp.mean(dz * z, axis=-1, keepdims=True))
        do_ref[...] = dy[:, CONV_CH:].astype(BF16)

        lane = _lane(tm)
        delta = jnp.zeros((tm, SLOT), F32)
        for j in range(NH // 2):
            sl = slice(YS + SLOT * j, YS + SLOT * (j + 1))
            prod = dy[:, sl] * yv[:, sl].astype(F32)
            lo = jnp.sum(jnp.where(lane < HEAD_DIM, prod, 0.0), axis=-1, keepdims=True)
            hi = jnp.sum(jnp.where(lane < HEAD_DIM, 0.0, prod), axis=-1, keepdims=True)
            delta = jnp.where(lane == 2 * j, lo, jnp.where(lane == 2 * j + 1, hi, delta))
        del_ref[...] = delta.T[0:STAT_ROWS, :]

    cur = lambda w: pl.BlockSpec((tm, w), lambda i: (i, 0))
    full = lambda a: pl.BlockSpec(a.shape, lambda i: (0,) * a.ndim)
    sd = jax.ShapeDtypeStruct
    return pl.pallas_call(
        body, name="outproj_bwd", grid=(S // tm,),
        in_specs=[cur(D), cur(YP), cur(CONV_CH), full(ln_g), full(ln_b), full(w_outp)],
        out_specs=[cur(CONV_CH), cur(YH), pl.BlockSpec((STAT_ROWS, tm), lambda i: (0, i)),
                   pl.BlockSpec((YP, D), lambda i: (0, 0)),
                   pl.BlockSpec((1, CONV_CH), lambda i: (0, 0)), pl.BlockSpec((1, CONV_CH), lambda i: (0, 0))],
        out_shape=[sd((S, CONV_CH), F32), sd((S, YH), BF16), sd((STAT_ROWS, S), F32), sd((YP, D), F32),
                   sd((1, CONV_CH), F32), sd((1, CONV_CH), F32)],
        compiler_params=_cp(),
    )(dh2, y, yc, ln_g, ln_b, w_outp)


def _conv_bwd(pu, dyc, conv_w, tm):
    S = pu.shape[0]
    nt = S // tm
    nh = tm // HALO

    def body(pu_ref, pup_ref, dy_ref, dyn_ref, cw_ref, dpu_ref, dcw_ref, dcb_ref, ext, ext2, dcw8):
        i = pl.program_id(0)

        @pl.when(i == 0)
        def _():
            dcw8[...] = jnp.zeros_like(dcw8)
            dcb_ref[...] = jnp.zeros_like(dcb_ref)

        not_first = (i > 0).astype(F32)
        not_last = (i < nt - 1).astype(F32)
        ext[0, 0:HALO, :] = _glu(pup_ref[...]) * not_first
        ext[0, HALO:HALO + tm, :] = _glu(pu_ref[...])
        _shifted_copies(ext)
        ext2[0, 0:tm, :] = dy_ref[...]
        ext2[0, tm:tm + HALO, :] = dyn_ref[...] * not_last
        _shifted_copies(ext2)
        dcb_ref[...] += _colsum(dy_ref[...])

        def rows_chunk(r, carry):
            r0 = pl.multiple_of(r * CONV_ROWS, CONV_ROWS)
            dyc_ = dy_ref[pl.ds(r0, CONV_ROWS), :]
            dyg = jnp.zeros((CONV_ROWS, CONV_CH), F32)
            for k in range(CONV_WIDTH):
                prod = dyc_ * _window(ext, HALO - (CONV_WIDTH - 1) + k, CONV_ROWS, r0)
                dcw8[k] += jnp.sum(prod.reshape(CONV_ROWS // 8, 8, CONV_CH), axis=0)
                dyg = dyg + cw_ref[k:k + 1, :] * _window(ext2, CONV_WIDTH - 1 - k, CONV_ROWS, r0)
            u = pu_ref[pl.ds(r0, CONV_ROWS), :]
            a_, sg = u[:, :CONV_CH], _sigmoid(u[:, CONV_CH:])
            dpu_ref[pl.ds(r0, CONV_ROWS), 0:CONV_CH] = (dyg * sg).astype(BF16)
            dpu_ref[pl.ds(r0, CONV_ROWS), CONV_CH:PU] = (dyg * a_ * sg * (1.0 - sg)).astype(BF16)
            return carry

        lax.fori_loop(0, tm // CONV_ROWS, rows_chunk, 0)

        @pl.when(i == nt - 1)
        def _():
            dcw_ref[...] = jnp.sum(dcw8[...], axis=1)

    cur = lambda w: pl.BlockSpec((tm, w), lambda i: (i, 0))
    prev = lambda w: pl.BlockSpec((HALO, w), lambda i: (jnp.maximum(i * nh - 1, 0), 0))
    nxt = lambda w: pl.BlockSpec((HALO, w), lambda i: (jnp.minimum((i + 1) * nh, S // HALO - 1), 0))
    acc = lambda r, w: pl.BlockSpec((r, w), lambda i: (0, 0))
    sd = jax.ShapeDtypeStruct
    return pl.pallas_call(
        body, name="conv_bwd", grid=(nt,),
        in_specs=[cur(PU), prev(PU), cur(CONV_CH), nxt(CONV_CH), acc(32, CONV_CH)],
        out_specs=[cur(PU), acc(32, CONV_CH), acc(1, CONV_CH)],
        out_shape=[sd((S, PU), BF16), sd((32, CONV_CH), F32), sd((1, CONV_CH), F32)],
        scratch_shapes=[pltpu.VMEM((8, tm + HALO, CONV_CH), F32), pltpu.VMEM((8, tm + HALO, CONV_CH), F32),
                        pltpu.VMEM((32, 8, CONV_CH), F32)],
        compiler_params=_cp(),
    )(pu, pu, dyc, dyc, conv_w)


def _attn_bwd(p, do, lse, delta, cosT, sinT, gq, gk, sinks, mk, mv, gqm, tm, rider=None):
    S = p.shape[0]
    M = mk.shape[0]
    nb = tm // BLK
    nt = S // tm
    nblocks = S // BLK
    W = SLOT * N_MEMH

    def body(p_ref, pp_ref, pn_ref, dy_ref, dyn_ref, lse_ref, lsen_ref, del_ref, deln_ref,
             cos_ref, cosp_ref, cosn_ref, sin_ref, sinp_ref, sinn_ref,
             gq_ref, gk_ref, sink_ref, mk_ref, mv_ref, gqm_ref,
             dp_ref, dgq_ref, dgk_ref, dgqm_ref, dsink_ref, dmk_ref, dmv_ref):
        i = pl.program_id(0)

        @pl.when(i == 0)
        def _():
            for r in (dgq_ref, dgk_ref, dgqm_ref, dsink_ref, dmk_ref, dmv_ref):
                r[...] = jnp.zeros_like(r)

        lane = _lane(tm)
        lane_e = _lane(tm + BLK)

        cos_k = jnp.concatenate([cosp_ref[...], cos_ref[...]], axis=0)
        sin_k = jnp.concatenate([sinp_ref[...], sin_ref[...]], axis=0)
        cos_q = jnp.concatenate([cos_ref[...], cosn_ref[...]], axis=0)
        sin_q = jnp.concatenate([sin_ref[...], sinn_ref[...]], axis=0)
        lse_e = jnp.concatenate([lse_ref[...], lsen_ref[...]], axis=1)
        del_e = jnp.concatenate([del_ref[...], deln_ref[...]], axis=1)
        kj = lax.broadcasted_iota(jnp.int32, (BLK, GROUP * BLK), 0)
        qi = lax.broadcasted_iota(jnp.int32, (BLK, GROUP * BLK), 1) & (BLK - 1)
        diag = kj <= qi
        offd = kj > qi
        dgq = jnp.zeros((1, SLOT), F32)
        dgk = jnp.zeros((1, SLOT), F32)
        dsink = jnp.zeros((1, SLOT), F32)
        lane1 = lax.broadcasted_iota(jnp.int32, (1, SLOT), 1)
        k_pair = jnp.concatenate([pp_ref[:, KO:KO + SLOT], p_ref[:, KO:KO + SLOT]], axis=0)
        k_pair = _pair_fwd(k_pair, gk_ref[...], cos_k, sin_k, lane_e)
        v_pair = jnp.concatenate([pp_ref[:, VO:VO + SLOT], p_ref[:, VO:VO + SLOT]], axis=0)
        k_e = [_lo(k_pair, kvh, lane_e).astype(BF16) for kvh in range(N_KV)]
        v_e = [_lo(v_pair, kvh, lane_e).astype(BF16) for kvh in range(N_KV)]
        dk = [[jnp.zeros((BLK, SLOT), F32) for _ in range(nb)] for _ in range(N_KV)]
        dv = [[jnp.zeros((BLK, SLOT), F32) for _ in range(nb)] for _ in range(N_KV)]
        q_e, do_e = [], []
        for j in range(N_Q // 2):
            js = slice(SLOT * j, SLOT * (j + 1))
            q_pair = _pair_fwd(jnp.concatenate([p_ref[:, js], pn_ref[:, js]], axis=0), gq_ref[...],
                               cos_q, sin_q, lane_e)
            do_pair = jnp.concatenate([dy_ref[:, js], dyn_ref[:, js]], axis=0).astype(F32)
            for half in range(2):
                q_e.append(_lo(q_pair, half, lane_e).astype(BF16))
                do_e.append(_lo(do_pair, half, lane_e).astype(BF16))
        dq_heads = [None] * N_Q
        for kvh in range(N_KV):
            hs = [GROUP * kvh + gi for gi in range(GROUP)]
            dq3 = [None] * nb
            for m in range(nb + 1):
                rows = slice(BLK * m, BLK * (m + 1))
                q3 = jnp.concatenate([q_e[h][rows] for h in hs], axis=0)
                do3 = jnp.concatenate([do_e[h][rows] for h in hs], axis=0)
                lb3 = jnp.concatenate([lse_e[h:h + 1, rows] for h in hs], axis=1)
                db3 = jnp.concatenate([del_e[h:h + 1, rows] for h in hs], axis=1)
                for n in (m - 1, m):
                    if n == nb:
                        continue
                    krows = slice(BLK * (n + 1), BLK * (n + 2))
                    kb, vb = k_e[kvh][krows], v_e[kvh][krows]
                    s = _dot_nt(kb, q3) * SCALE
                    mask = diag if n == m else offd
                    if n == -1:
                        mask = mask & (i > 0)
                    if m == nb:
                        mask = mask & (i < nt - 1)
                    prob = jnp.where(mask, jnp.exp(jnp.where(mask, s - lb3, NEG)), 0.0)
                    dpb = _dot_nt(vb, do3)
                    ds = (prob * (dpb - db3) * SCALE).astype(BF16)
                    if m < nb:
                        dqc = _dot_tn(ds, kb)
                        dq3[m] = dqc if dq3[m] is None else dq3[m] + dqc
                    if n >= 0:
                        dk[kvh][n] = dk[kvh][n] + _dot_nn(ds, q3)
                        dv[kvh][n] = dv[kvh][n] + _dot_nn(prob.astype(BF16), do3)
            for gi, h in enumerate(hs):
                dq_heads[h] = jnp.concatenate([dq3[m][BLK * gi:BLK * (gi + 1)] for m in range(nb)], axis=0)
                psink = jnp.exp(sink_ref[h] - lse_e[h:h + 1, 0:tm])
                dsink = dsink + jnp.where(
                    lane1 == h, -jnp.sum(psink * del_e[h:h + 1, 0:tm], axis=-1, keepdims=True), 0.0)
        for j in range(N_Q // 2):
            js = slice(SLOT * j, SLOT * (j + 1))
            dqr, prod = _pair_bwd(_pack(dq_heads[2 * j], dq_heads[2 * j + 1], lane), p_ref[:, js], gq_ref[...],
                                  cos_ref[...], sin_ref[...], lane)
            dp_ref[:, js] = dqr.astype(BF16)
            dgq = dgq + _colsum(prod)
        dk_pair = _pack(jnp.concatenate(dk[0], axis=0), jnp.concatenate(dk[1], axis=0), lane)
        dkr, prod = _pair_bwd(dk_pair, p_ref[:, KO:KO + SLOT], gk_ref[...], cos_ref[...], sin_ref[...], lane)
        dp_ref[:, KO:KO + SLOT] = dkr.astype(BF16)
        dp_ref[:, VO:VO + SLOT] = _pack(jnp.concatenate(dv[0], axis=0), jnp.concatenate(dv[1], axis=0),
                                        lane).astype(BF16)
        dgq_ref[...] += dgq
        dgk_ref[...] += _colsum(prod)
        dsink_ref[...] += dsink

        dgqm = jnp.zeros((1, SLOT), F32)
        dq_heads = []
        for hm in range(N_MEMH):
            ms = slice(SLOT * hm, SLOT * (hm + 1))
            js = slice(MO + SLOT * (hm // 2), MO + SLOT * (hm // 2 + 1))
            os_ = slice(SLOT * ((N_Q + hm) // 2), SLOT * ((N_Q + hm) // 2 + 1))
            if hm % 2 == 0:
                qm_pair = _pair_fwd(p_ref[:, js], gqm_ref[...], None, None, lane)
                do_pair = dy_ref[:, os_].astype(F32)
            qm = _lo(qm_pair, hm % 2, lane).astype(BF16)
            dob = _lo(do_pair, hm % 2, lane).astype(BF16)
            kb, vb = mk_ref[:, ms], mv_ref[:, ms]
            s = _dot_nt(kb, qm) * SCALE
            prob = jnp.exp(s - lse_ref[N_Q + hm:N_Q + hm + 1, :])
            dpb = _dot_nt(vb, dob)
            ds = (prob * (dpb - del_ref[N_Q + hm:N_Q + hm + 1, :]) * SCALE).astype(BF16)
            dq_heads.append(_dot_tn(ds, kb))
            dmk_ref[:, ms] += _dot_nn(ds, qm)
            dmv_ref[:, ms] += _dot_nn(prob.astype(BF16), dob)
            if hm % 2 == 1:
                dqr, prod = _pair_bwd(_pack(dq_heads[-2], dq_heads[-1], lane), p_ref[:, js], gqm_ref[...],
                                      None, None, lane)
                dp_ref[:, js] = dqr.astype(BF16)
                dgqm = dgqm + _colsum(prod)
        dgqm_ref[...] += dgqm

    cur = lambda w: pl.BlockSpec((tm, w), lambda i: (i, 0))
    prev = lambda w: pl.BlockSpec((BLK, w), lambda i: (jnp.maximum(i * nb - 1, 0), 0))
    nxt = lambda w: pl.BlockSpec((BLK, w), lambda i: (jnp.minimum((i + 1) * nb, nblocks - 1), 0))
    full = lambda a: pl.BlockSpec(a.shape, lambda i: (0,) * a.ndim)
    acc = lambda r, w: pl.BlockSpec((r, w), lambda i: (0, 0))
    sd = jax.ShapeDtypeStruct
    stat = pl.BlockSpec((STAT_ROWS, tm), lambda i: (0, i))
    stat_n = pl.BlockSpec((STAT_ROWS, BLK), lambda i: (0, jnp.minimum((i + 1) * nb, nblocks - 1)))
    return _call(
        body, name="attn_bwd", grid=(nt,),
        in_specs=[cur(PH), prev(PH), nxt(PH), cur(YH), nxt(YH), stat, stat_n, stat, stat_n,
                  cur(SLOT), prev(SLOT), nxt(SLOT), cur(SLOT), prev(SLOT), nxt(SLOT),
                  full(gq), full(gk), pl.BlockSpec(memory_space=pltpu.SMEM), full(mk), full(mv), full(gqm)],
        out_specs=[cur(PH), acc(1, SLOT), acc(1, SLOT), acc(1, SLOT), acc(1, SLOT), acc(M, W), acc(M, W)],
        out_shape=[sd((S, PH), BF16), sd((1, SLOT), F32), sd((1, SLOT), F32), sd((1, SLOT), F32),
                   sd((1, SLOT), F32), sd((M, W), F32), sd((M, W), F32)],
        args=[p, p, p, do, do, lse, lse, delta, delta, cosT, cosT, cosT, sinT, sinT, sinT,
              gq, gk, sinks, mk, mv, gqm],
        rider=rider)


def _proj_bwd(dpu, dph, h, dh2, g, n, w_inp, tm):
    S, D = h.shape

    def body(dpu_ref, dph_ref, h_ref, dh2_ref, g_ref, n_ref, w_ref, dh_ref, dg_ref, dw_ref):
        i = pl.program_id(0)

        @pl.when(i == 0)
        def _():
            dg_ref[...] = jnp.zeros_like(dg_ref)
            dw_ref[...] = jnp.zeros_like(dw_ref)

        dpu, dph, nv = dpu_ref[...], dph_ref[...], n_ref[...]
        dn = _dot_nn(dpu, w_ref[0:PU, :]) + _dot_nn(dph, w_ref[PU:PP, :])
        dw_ref[0:PU, :] += _dot_tn(dpu, nv)
        dw_ref[PU:PP, :] += _dot_tn(dph, nv)
        hv = h_ref[...]
        dx, dgrow = _rms_bwd(dn, hv, _rms(hv), g_ref[...])
        dh_ref[...] = dh2_ref[...] + dx
        dg_ref[...] += _colsum(dgrow)

    cur = lambda w: pl.BlockSpec((tm, w), lambda i: (i, 0))
    sd = jax.ShapeDtypeStruct
    return pl.pallas_call(
        body, name="proj_bwd", grid=(S // tm,),
        in_specs=[cur(PU), cur(PH), cur(D), cur(D), pl.BlockSpec((1, D), lambda i: (0, 0)), cur(D),
                  pl.BlockSpec((PP, D), lambda i: (0, 0))],
        out_specs=[cur(D), pl.BlockSpec((1, D), lambda i: (0, 0)), pl.BlockSpec((PP, D), lambda i: (0, 0))],
        out_shape=[sd((S, D), F32), sd((1, D), F32), sd((PP, D), F32)],
        compiler_params=_cp(),
    )(dpu, dph, h, dh2, g, n, w_inp)


def _norm_bwd(dxn, h, g, tm):
    S, D = h.shape

    def body(d_ref, h_ref, g_ref, dh_ref, dg_ref):
        @pl.when(pl.program_id(0) == 0)
        def _():
            dg_ref[...] = jnp.zeros_like(dg_ref)

        hv = h_ref[...]
        dx, dgrow = _rms_bwd(d_ref[...], hv, _rms(hv), g_ref[...])
        dh_ref[...] = dx
        dg_ref[...] += _colsum(dgrow)

    cur = pl.BlockSpec((tm, D), lambda i: (i, 0))
    vec = pl.BlockSpec((1, D), lambda i: (0, 0))
    return pl.pallas_call(
        body, name="norm_bwd", grid=(S // tm,), in_specs=[cur, cur, vec], out_specs=[cur, vec],
        out_shape=[jax.ShapeDtypeStruct((S, D), F32), jax.ShapeDtypeStruct((1, D), F32)],
        compiler_params=_cp(),
    )(dxn, h, g)


def _loss_bwd(xn, h, g, target, tm):
    S, D = h.shape

    def body(y_ref, h_ref, g_ref, t_ref, loss_ref, dh_ref, dg_ref):
        @pl.when(pl.program_id(0) == 0)
        def _():
            dg_ref[...] = jnp.zeros_like(dg_ref)
            loss_ref[...] = jnp.zeros_like(loss_ref)

        err = y_ref[...] - t_ref[...]
        part = jnp.sum(jnp.mean(err * err, axis=-1, keepdims=True), axis=0, keepdims=True)
        loss_ref[...] += 0.5 * part
        hv = h_ref[...]
        dx, dgrow = _rms_bwd(err * (1.0 / D), hv, _rms(hv), g_ref[...])
        dh_ref[...] = dx
        dg_ref[...] += _colsum(dgrow)

    cur = pl.BlockSpec((tm, D), lambda i: (i, 0))
    vec = pl.BlockSpec((1, D), lambda i: (0, 0))
    return pl.pallas_call(
        body, name="loss_bwd", grid=(S // tm,), in_specs=[cur, cur, vec, cur],
        out_specs=[pl.BlockSpec((1, SLOT), lambda i: (0, 0)), cur, vec],
        out_shape=[jax.ShapeDtypeStruct((1, SLOT), F32), jax.ShapeDtypeStruct((S, D), F32),
                   jax.ShapeDtypeStruct((1, D), F32)],
        compiler_params=_cp(),
    )(xn, h, g, target)


def _all_gather_small(buf):
    _, R, W = buf.shape

    def body(in_ref, out_ref, ssem, rsem, lsem):
        x, y, c, _ = _place()
        me = 4 * x + 2 * y + c
        local = pltpu.make_async_copy(in_ref, out_ref.at[pl.ds(me, 1)], lsem)
        local.start()

        def copy(k, block):
            fx, fy, fc = (k >> 2) & 1, (k >> 1) & 1, k & 1
            peer = (x ^ fx, y ^ fy, c ^ fc)
            return pltpu.make_async_remote_copy(
                src_ref=in_ref, dst_ref=out_ref.at[pl.ds(block, 1)], send_sem=ssem.at[k - 1],
                recv_sem=rsem.at[k - 1], device_id=peer, device_id_type=MESH)

        sends = [copy(k, me) for k in range(1, 8)]
        for cp in sends:
            cp.start()
        for k in range(1, 8):
            copy(k, me ^ k).wait_recv()
        for cp in sends:
            cp.wait_send()
        local.wait()

    hbm = pl.BlockSpec(memory_space=pl.ANY)
    return pl.pallas_call(
        body, name="all_gather_small", in_specs=[hbm], out_specs=hbm,
        out_shape=jax.ShapeDtypeStruct((8, R, W), buf.dtype),
        scratch_shapes=[pltpu.SemaphoreType.DMA((7,)), pltpu.SemaphoreType.DMA((7,)), pltpu.SemaphoreType.DMA],
    )(buf)


def _row_tile(n, cap=1024):
    for t in range(min(n, cap) // 8 * 8, 7, -8):
        if n % t == 0:
            return t
    return n


def _sum4(own, recv):
    n, rows, D = own.shape
    tr = _row_tile(rows)

    def body(o_ref, r0_ref, r1_ref, r2_ref, out_ref):
        out_ref[...] = ((o_ref[...].astype(F32) + r0_ref[...].astype(F32)) + r1_ref[...].astype(F32)) \
            + r2_ref[...].astype(F32)

    def rspec(p):
        return pl.BlockSpec((None, None, None, tr, D), lambda k, i, p=p: (p, k, 0, i, 0))

    blk = pl.BlockSpec((None, tr, D), lambda k, i: (k, i, 0))
    return pl.pallas_call(
        body, name="sum4", grid=(n, rows // tr),
        in_specs=[blk, rspec(0), rspec(1), rspec(2)], out_specs=blk,
        out_shape=jax.ShapeDtypeStruct((n, rows, D), F32),
    )(own, recv, recv, recv)


def _adam_math(w, g, m, v):
    m = ADAM_B1 * m + (1.0 - ADAM_B1) * g
    v = ADAM_B2 * v + (1.0 - ADAM_B2) * (g * g)
    m_hat = m / (1.0 - ADAM_B1 ** ADAM_STEP)
    v_hat = v / (1.0 - ADAM_B2 ** ADAM_STEP)
    delta = -ADAM_LR * (m_hat / (jnp.sqrt(v_hat) + ADAM_EPS) + ADAM_WD * w)
    return delta, m, v


def _adam_fused(w, m, v, parts, theirs, sel, row0, nrows, rider=None):
    L, R, D = w.shape
    assert R == nrows and parts[0].shape[2] == D
    t = _row_tile(nrows if row0 == 0 else _gcd(row0, nrows), 512)

    def gspec(k):
        return pl.BlockSpec((None, t, D), lambda l, c: (sel, row0 // t + jnp.where(l == k, c, 0), 0))

    def body(*refs):
        w_ref, m_ref, v_ref = refs[:3]
        p_refs, q_refs = refs[3:3 + L], refs[3 + L:3 + 2 * L]
        g_ref, d_ref, nm_ref, nv_ref, g_scr = refs[3 + 2 * L:]
        l = pl.program_id(0)
        for k in range(L):
            @pl.when(l == k)
            def _(k=k):
                g_scr[...] = p_refs[k][...] + q_refs[k][...]

        g = g_scr[...]
        g_ref[...] = g
        d, m_, v_ = _adam_math(w_ref[...], g, m_ref[...], v_ref[...])
        d_ref[...] = d
        nm_ref[...] = m_
        nv_ref[...] = v_

    blk = pl.BlockSpec((None, t, D), lambda l, c: (l, c, 0))
    return _call(body, name="adam_fused", grid=(L, R // t),
                 in_specs=[blk] * 3 + [gspec(k) for k in range(L)] * 2, out_specs=[blk] * 4,
                 out_shape=[jax.ShapeDtypeStruct((L, R, D), F32)] * 4,
                 args=[w, m, v, *parts, *theirs], scratch=[pltpu.VMEM((t, D), F32)], rider=rider)


def _gcd(a, b):
    while b:
        a, b = b, a % b
    return a


def _small_sum_adam(g8, w, m, v):
    _, R, W = g8.shape

    def body(g_ref, w_ref, m_ref, v_ref, go_ref, d_ref, nm_ref, nv_ref):
        g = g_ref[0]
        for k in range(1, 8):
            g = g + g_ref[k]
        go_ref[...] = g
        d, m_, v_ = _adam_math(w_ref[...], g, m_ref[...], v_ref[...])
        d_ref[...] = d
        nm_ref[...] = m_
        nv_ref[...] = v_

    return pl.pallas_call(body, name="small_sum_adam",
                          out_shape=[jax.ShapeDtypeStruct((R, W), F32)] * 4)(g8, w, m, v)


def _pad_vec(v):
    return jnp.pad(v, (0, SLOT - v.shape[0]))[None, :]


class _Pack:
    def __init__(self, shapes):
        self.shapes = shapes
        self.sizes = [int(functools.reduce(lambda a, b: a * b, s, 1)) for s in shapes]
        total = sum(self.sizes)
        self.rows = -(-total // (8 * SLOT)) * 8
        self.pad = self.rows * SLOT - total

    def pack(self, arrs):
        flat = jnp.concatenate([a.reshape(-1).astype(F32) for a in arrs] + [jnp.zeros((self.pad,), F32)])
        return flat.reshape(self.rows, SLOT)

    def unpack(self, buf):
        flat, out, o = buf.reshape(-1), [], 0
        for s, n in zip(self.shapes, self.sizes):
            out.append(flat[o:o + n].reshape(s))
            o += n
        return out


def kernel(x, mem, positions, ffn1_norm, ffn1_w1, ffn1_w3, ffn1_w2, mix_norm, w_in, conv_w, conv_b, conv_ln_g, conv_ln_b, swa_q_norm, swa_k_norm, swa_sinks, mem_norm, w_mem_kv, mem_q_norm, mem_k_norm, w_out, ffn2_norm, ffn2_w1, ffn2_w3, ffn2_w2, final_norm, loss_target, m_ffn1_norm, m_ffn1_w1, m_ffn1_w3, m_ffn1_w2, m_mix_norm, m_w_in, m_conv_w, m_conv_b, m_conv_ln_g, m_conv_ln_b, m_swa_q_norm, m_swa_k_norm, m_swa_sinks, m_mem_norm, m_w_mem_kv, m_mem_q_norm, m_mem_k_norm, m_w_out, m_ffn2_norm, m_ffn2_w1, m_ffn2_w3, m_ffn2_w2, m_final_norm, v_ffn1_norm, v_ffn1_w1, v_ffn1_w3, v_ffn1_w2, v_mix_norm, v_w_in, v_conv_w, v_conv_b, v_conv_ln_g, v_conv_ln_b, v_swa_q_norm, v_swa_k_norm, v_swa_sinks, v_mem_norm, v_w_mem_kv, v_mem_q_norm, v_mem_k_norm, v_w_out, v_ffn2_norm, v_ffn2_w1, v_ffn2_w3, v_ffn2_w2, v_final_norm):
    names = ['ffn1_norm', 'ffn1_w1', 'ffn1_w3', 'ffn1_w2', 'mix_norm', 'w_in', 'conv_w', 'conv_b', 'conv_ln_g',
             'conv_ln_b', 'swa_q_norm', 'swa_k_norm', 'swa_sinks', 'mem_norm', 'w_mem_kv', 'mem_q_norm',
             'mem_k_norm', 'w_out', 'ffn2_norm', 'ffn2_w1', 'ffn2_w3', 'ffn2_w2', 'final_norm']
    loc = locals()
    W = {n: loc[n] for n in names}
    M1 = {n: loc['m_' + n] for n in names}
    V1 = {n: loc['v_' + n] for n in names}

    S, D = x.shape[1], x.shape[2]
    L = ffn1_norm.shape[0]
    Fs = ffn1_w1.shape[2]
    F = 4 * Fs
    Mlen = mem.shape[1]
    cw_sh = conv_w.shape[2]
    tm = 512 if S >= 2048 else 256
    tf = 1408 if F % 1408 == 0 else 256
    tfw = 256
    tmw = 2048 if S >= 2048 else 256
    tma = 1024 if S >= 2048 else 256
    x0 = x[0]
    mem0 = mem[0]
    target = loss_target[0]
    my_chip = 2 * lax.axis_index("x") + lax.axis_index("y")

    mkv_rows = w_mem_kv.shape[1] * MEM_KV // D
    r_in, r_out = D_IN // 4, D_MIX // 4
    rm = r_in + r_out + mkv_rows

    shard = lambda w: w.astype(BF16).reshape((1, 1) + w.shape)

    groups = []
    for l in range(L):
        groups.append([shard(ffn1_w1[l].T), shard(ffn1_w3[l].T), shard(ffn1_w2[l])])
        groups.append([shard(w_in[l].T), shard(w_out[l]), shard(w_mem_kv[l].reshape(mkv_rows, D))])
        groups.append([shard(ffn2_w1[l].T), shard(ffn2_w3[l].T), shard(ffn2_w2[l])])
    gathered = [None] * len(groups)
    cw_rows = -(-(L * CONV_WIDTH) // 8) * 8
    cw_pad = jnp.pad(conv_w.reshape(L * CONV_WIDTH, cw_sh), ((0, cw_rows - L * CONV_WIDTH), (0, SLOT - cw_sh)))
    *gathered[0], cw_g = _run_rider(_GatherHalves(groups[0] + [cw_pad.reshape(1, 1, cw_rows, SLOT)]),
                                    "all_gather_first")
    conv_wF = cw_g[0, :, :L * CONV_WIDTH, :cw_sh].reshape(4, L, CONV_WIDTH, cw_sh)
    conv_wF = jnp.moveaxis(conv_wF, 0, 2).reshape(L, CONV_WIDTH, 4 * cw_sh)
    conv_wP = jnp.pad(conv_wF, ((0, 0), (0, 32 - CONV_WIDTH), (0, 0)))

    def gather_rider(j):
        want = [k for k in [j + 1] if k < len(groups)]
        return (_Gather([b for k in want for b in groups[k]]), want) if want else (None, want)

    def keep(want, got):
        for n, k in enumerate(want):
            gathered[k] = got[3 * n:3 * n + 3]

    def ffn_weights(j):
        return tuple(g.reshape(F, D) for g in gathered[j])

    def mix_weights(l):
        g_in, g_out, g_mkv = gathered[3 * l + 1]
        w_inp = g_in.reshape(D_IN, D)
        w_outp = g_out.reshape(D_MIX, D)
        w_mkvp = jnp.pad(g_mkv.reshape(D, 2 * N_MEMH, HEAD_DIM),
                         ((0, 0), (0, 0), (0, SLOT - HEAD_DIM))).reshape(D, 2 * N_MEMH * SLOT)
        return w_inp, w_outp, w_mkvp

    inv_freq = ROPE_THETA ** (-jnp.arange(0, HEAD_DIM, 2, dtype=F32) / HEAD_DIM)
    invf = jnp.tile(inv_freq, SLOT // (HEAD_DIM // 2))[None, :]
    cosT, sinT = _rope_tables(positions.reshape(S, 1), invf, tm)

    row = lambda a, l: a[l][None, :]
    sinks_p = jnp.pad(swa_sinks, ((0, 0), (0, 8 - N_Q)))

    saved = []
    xin = x0
    xn = None
    for l in range(L):
        wf1 = ffn_weights(3 * l)
        rider, want = gather_rider(3 * l)
        (h1, a1, b1, t1), got = _ffn_fwd(xin, row(ffn1_norm, l), wf1, None, tm, tf, rider=rider)
        keep(want, got)
        w_inp, w_outp, w_mkvp = mix_weights(l)
        pu, p, n2 = _proj_fwd(h1, row(mix_norm, l), w_inp, tm)
        gk_m = _pad_vec(mem_k_norm[l])
        nm, mraw, mk, mv = _mem_kv_fwd(mem0, row(mem_norm, l), w_mkvp, gk_m)
        twice = lambda v: jnp.tile(v, 2)[None, :]
        gq, gk, gqm = twice(swa_q_norm[l]), twice(swa_k_norm[l]), twice(mem_q_norm[l])
        rider, want = gather_rider(3 * l + 1)
        (h2, y, yc, lse), got = _mixer_fwd(pu, p, h1, cosT, sinT, conv_wP[l], row(conv_b, l), row(conv_ln_g, l),
                                           row(conv_ln_b, l), gq, gk, sinks_p[l], mk, mv, gqm, w_outp, tm,
                                           rider=rider)
        keep(want, got)
        wf2 = ffn_weights(3 * l + 2)
        rider, want = gather_rider(3 * l + 2)
        (h3, a2, b2, t2, xn), got = _ffn_fwd(h2, row(ffn2_norm, l), wf2, row(final_norm, l), tm, tf, rider=rider)
        keep(want, got)
        saved.append(dict(xin=xin, h1=h1, a1=a1, b1=b1, pu=pu, p=p, n2=n2, nm=nm, mraw=mraw, mk=mk, mv=mv, gk_m=gk_m,
                          gq=gq, gk=gk, gqm=gqm, h2=h2, y=y, yc=yc, lse=lse, h3=h3, a2=a2, b2=b2, t1=t1, t2=t2,
                          wf1=wf1, wf2=wf2, w_inp=w_inp, w_outp=w_outp, w_mkvp=w_mkvp))
        xin = xn

    G = {n: [None] * L for n in names}
    ffn_bufs = [None] * (2 * L)
    mix_bufs = [None] * L
    parts, theirs = {}, {}
    pending_swap = []

    def scattered(key, buf, recv):
        own = lax.dynamic_index_in_dim(buf, my_chip, axis=1, keepdims=False)
        parts[key] = _sum4(own, recv)
        pending_swap.append(key)

    def comm_rider(scatter_buf):
        keys = list(pending_swap)
        pending_swap.clear()
        riders = ([_Scatter([scatter_buf])] if scatter_buf is not None else []) \
            + ([_Swap([parts[k] for k in keys])] if keys else [])
        if not riders:
            return None, lambda got: None
        multi = _Multi(riders)

        def store(got):
            outs = multi.split_outputs(got)
            if keys:
                for k, t in zip(keys, outs[-1]):
                    theirs[k] = t
            return outs[0][0] if scatter_buf is not None else None

        return multi, store

    dxn = None
    loss_part = None
    for l in reversed(range(L)):
        sv = saved[l]
        if l == L - 1:
            loss_part, dh3, G['final_norm'][l] = _loss_bwd(xn, sv['h3'], row(final_norm, l), target, tm)
        else:
            dh3, G['final_norm'][l] = _norm_bwd(dxn, sv['h3'], row(final_norm, l), tm)
        rider, store = comm_rider(ffn_bufs[2 * l + 2] if l < L - 1 else None)
        (dh2, G['ffn2_norm'][l], da, db, n, dy), got = _ffn_bwd_act(
            dh3, sv['h2'], row(ffn2_norm, l), sv['a2'], sv['b2'], sv['wf2'], tm, tf, rider=rider)
        recv = store(got)
        if recv is not None:
            scattered(('f', 2 * l + 2), ffn_bufs[2 * l + 2], recv)
        ffn_bufs[2 * l + 1] = _ffn_bwd_w(da, db, sv['t2'], n, dy, tmw, tfw)[0].reshape(3, 4, Fs, D)
        dyc, do, delta, dwo, G['conv_ln_g'][l], G['conv_ln_b'][l] = _outproj_bwd(
            dh2, sv['y'], sv['yc'], row(conv_ln_g, l), row(conv_ln_b, l), sv['w_outp'], tm)
        rider, store = comm_rider(ffn_bufs[2 * l + 1])
        (dph, dgq, dgk, dgqm, dsink, dmk, dmv), got = _attn_bwd(
            sv['p'], do, sv['lse'], delta, cosT, sinT, sv['gq'], sv['gk'], sinks_p[l],
            sv['mk'], sv['mv'], sv['gqm'], 2 * tma, rider=rider)
        scattered(('f', 2 * l + 1), ffn_bufs[2 * l + 1], store(got))
        dpu, dcw, G['conv_b'][l] = _conv_bwd(sv['pu'], dyc, conv_wP[l], tma)
        dwm, G['mem_norm'][l], dgk_m = _mem_kv_bwd(dmk, dmv, sv['mraw'], sv['nm'], mem0, row(mem_norm, l),
                                                   sv['w_mkvp'], sv['gk_m'])
        dh1, G['mix_norm'][l], dwi = _proj_bwd(dpu, dph, sv['h1'], dh2, row(mix_norm, l), sv['n2'], sv['w_inp'], tm)
        dwiT = dwi.reshape(4, r_in, D)
        dwoF = dwo.reshape(4, r_out, D)
        dwmF = dwm.reshape(D, 2 * N_MEMH, SLOT)[:, :, :HEAD_DIM].reshape(4, mkv_rows, D)
        mix_bufs[l] = jnp.concatenate([dwiT, dwoF, dwmF], axis=1).astype(BF16).reshape(1, 4, rm, D)
        rider, store = comm_rider(mix_bufs[l]) if l > 0 else (None, None)
        (dxl, G['ffn1_norm'][l], da, db, n, dy), got = _ffn_bwd_act(
            dh1, sv['xin'], row(ffn1_norm, l), sv['a1'], sv['b1'], sv['wf1'], tm, tf, rider=rider)
        if l > 0:
            scattered(('m', l), mix_bufs[l], store(got))
            rider, store = None, None
        else:
            rider, store = comm_rider(mix_bufs[l])
        dwf, got = _ffn_bwd_w(da, db, sv['t1'], n, dy, tmw, tfw, rider=rider)
        if l == 0:
            scattered(('m', l), mix_bufs[l], store(got))
        ffn_bufs[2 * l] = dwf.reshape(3, 4, Fs, D)
        dxn = dxl
        G['conv_w'][l] = dcw[:CONV_WIDTH]
        G['swa_q_norm'][l] = dgq[0, :HEAD_DIM] + dgq[0, HEAD_DIM:]
        G['swa_k_norm'][l] = dgk[0, :HEAD_DIM] + dgk[0, HEAD_DIM:]
        G['mem_q_norm'][l] = dgqm[0, :HEAD_DIM] + dgqm[0, HEAD_DIM:]
        G['mem_k_norm'][l] = dgk_m[0, :HEAD_DIM]
        G['swa_sinks'][l] = dsink[0, :N_Q]
    grad_x = dxn[None]
    loss = lax.psum(loss_part[0, 0], AXES)
    rider, store = comm_rider(ffn_bufs[0])
    scattered(('f', 0), ffn_bufs[0], store(_run_rider(rider, "scatter_last")))
    rider, store = comm_rider(None)
    store(_run_rider(rider, "swap_last"))

    small = ['ffn1_norm', 'mix_norm', 'conv_b', 'conv_ln_g', 'conv_ln_b', 'swa_q_norm', 'swa_k_norm', 'swa_sinks',
             'mem_norm', 'mem_q_norm', 'mem_k_norm', 'ffn2_norm', 'final_norm']
    gsmall = [jnp.stack([G[n][l].reshape(-1) for l in range(L)]) for n in small]
    gcw = jnp.stack(G['conv_w'])
    cw_cols = 4 * cw_sh
    full_of = lambda a: lax.dynamic_update_slice(jnp.zeros((L, CONV_WIDTH, cw_cols), F32), a, (0, 0, my_chip * cw_sh))
    pk = _Pack([W[n].shape for n in small] + [(L, CONV_WIDTH, cw_cols)])
    g8 = _all_gather_small(pk.pack(gsmall + [gcw])[None])
    outs4 = _small_sum_adam(g8, pk.pack([W[n] for n in small] + [full_of(conv_w)]),
                            pk.pack([M1[n] for n in small] + [full_of(m_conv_w)]),
                            pk.pack([V1[n] for n in small] + [full_of(v_conv_w)]))
    un = [pk.unpack(o) for o in outs4]
    grads, deltas, new_m, new_v = {}, {}, {}, {}
    for k, n in enumerate(small):
        grads[n], deltas[n], new_m[n], new_v[n] = un[0][k], un[1][k], un[2][k], un[3][k]
    mine = lambda a: lax.dynamic_slice(a, (0, 0, my_chip * cw_sh), (L, CONV_WIDTH, cw_sh))
    grads['conv_w'], deltas['conv_w'], new_m['conv_w'], new_v['conv_w'] = [mine(u[-1]) for u in un]

    ffn1_k, ffn2_k = [('f', 2 * l) for l in range(L)], [('f', 2 * l + 1) for l in range(L)]
    mix_k = [('m', l) for l in range(L)]
    plan = [('ffn2_w1', ffn2_k, 0, 0, Fs, True), ('ffn2_w3', ffn2_k, 1, 0, Fs, True), ('ffn2_w2', ffn2_k, 2, 0, Fs, False),
            ('w_in', mix_k, 0, 0, r_in, True), ('w_out', mix_k, 0, r_in, r_out, False),
            ('w_mem_kv', mix_k, 0, r_in + r_out, mkv_rows, False),
            ('ffn1_w1', ffn1_k, 0, 0, Fs, True), ('ffn1_w3', ffn1_k, 1, 0, Fs, True), ('ffn1_w2', ffn1_k, 2, 0, Fs, False)]
    for n, keys, sel, row0, nrows, held_transposed in plan:
        shp = W[n].shape
        if held_transposed:
            view, back = (lambda a: jnp.swapaxes(a, 1, 2)), (lambda a: jnp.swapaxes(a, 1, 2))
        elif n == 'w_mem_kv':
            view, back = (lambda a: a.reshape(L, mkv_rows, D)), (lambda a: a.reshape(shp))
        else:
            view = back = lambda a: a
        res, _ = _adam_fused(view(W[n]), view(M1[n]), view(V1[n]), [parts[k] for k in keys],
                             [theirs[k] for k in keys], sel, row0, nrows)
        grads[n], deltas[n], new_m[n], new_v[n] = [back(r) for r in res]

    return (loss, grad_x, *[grads[n] for n in names], *[deltas[n] for n in names],
            *[new_m[n] for n in names], *[new_v[n] for n in names])
```

```python
import functools

import jax
import jax.numpy as jnp
from jax import lax
from jax.experimental import pallas as pl
from jax.experimental.pallas import tpu as pltpu

F32 = jnp.float32
BF16 = jnp.bfloat16
MESH = pl.DeviceIdType.MESH
AXES = ("x", "y", "c")

EPS = 1e-6
HEAD_DIM = 64
SLOT = 128
CONV_CH = 384
CONV_WIDTH = 31
N_Q, N_KV, N_MEMH = 6, 2, 4
GROUP = N_Q // N_KV
BLK = 128
HALO = 32
CONV_ROWS = 64
FFN_CHUNK = 256
ROPE_THETA = 10000.0
SCALE = HEAD_DIM ** -0.5
NEG = -1e30

N_HEADS_IN = N_Q + 2 * N_KV + N_MEMH
PU = 2 * CONV_CH
PH = HEAD_DIM * N_HEADS_IN
PP = PU + PH
QO = 0
KO = QO + HEAD_DIM * N_Q
VO = KO + HEAD_DIM * N_KV
MO = VO + HEAD_DIM * N_KV
NH = N_Q + N_MEMH
STAT_ROWS = 16
YH = HEAD_DIM * NH
YP = CONV_CH + YH
YS = CONV_CH
YM = YS + HEAD_DIM * N_Q
D_IN = PP
D_MIX = YP
MEM_KV = 2 * HEAD_DIM * N_MEMH

ADAM_LR, ADAM_B1, ADAM_B2, ADAM_EPS, ADAM_WD, ADAM_STEP = 0.001, 0.9, 0.999, 1e-08, 0.01, 10

VMEM_LIMIT_MB = 56


def _cp(mb=VMEM_LIMIT_MB):
    return pltpu.CompilerParams(vmem_limit_bytes=mb * 1024 * 1024)


def _dot_nn(a, b):
    return lax.dot_general(a, b, (((1,), (0,)), ((), ())), preferred_element_type=F32)


def _dot_nt(a, b):
    return lax.dot_general(a, b, (((1,), (1,)), ((), ())), preferred_element_type=F32)


def _dot_tn(a, b):
    return lax.dot_general(a, b, (((0,), (0,)), ((), ())), preferred_element_type=F32)


def _sigmoid(x):
    return 1.0 / (1.0 + jnp.exp(-x))


def _rms(x):
    return lax.rsqrt(jnp.mean(x * x, axis=-1, keepdims=True) + EPS)


def _rms_bwd(dn, x, r, g):
    xhat = x * r
    dxhat = dn * g
    dx = r * (dxhat - xhat * jnp.mean(dxhat * xhat, axis=-1, keepdims=True))
    return dx, dn * xhat


def _colsum(v):
    return jnp.sum(v, axis=0, keepdims=True)


def _lane(n):
    return lax.broadcasted_iota(jnp.int32, (n, SLOT), 1)


def _slot_rms(xs):
    return lax.rsqrt(jnp.sum(xs * xs, axis=-1, keepdims=True) * (1.0 / HEAD_DIM) + EPS)


def _slot_norm(xs, g):
    return xs * _slot_rms(xs) * g


def _slot_norm_bwd(dout, xs, g):
    r = _slot_rms(xs)
    xhat = xs * r
    dxhat = dout * g
    dx = r * (dxhat - xhat * (jnp.sum(dxhat * xhat, axis=-1, keepdims=True) * (1.0 / HEAD_DIM)))
    return dx, dout * xhat


def _halves(v, lane):
    lo = jnp.sum(jnp.where(lane < HEAD_DIM, v, 0.0), axis=-1, keepdims=True)
    hi = jnp.sum(jnp.where(lane < HEAD_DIM, 0.0, v), axis=-1, keepdims=True)
    return jnp.where(lane < HEAD_DIM, lo, hi)


def _pair_rms(x, lane):
    return lax.rsqrt(_halves(x * x, lane) * (1.0 / HEAD_DIM) + EPS)


def _pair_partner(v, lane):
    return jnp.where((lane & (HEAD_DIM - 1)) < HEAD_DIM // 2,
                     pltpu.roll(v, SLOT - HEAD_DIM // 2, 1), pltpu.roll(v, HEAD_DIM // 2, 1))


def _pair_fwd(x, g2, cosv, sinv, lane):
    xn = x * _pair_rms(x, lane) * g2
    if cosv is None:
        return xn
    return xn * cosv + _pair_partner(xn, lane) * sinv


def _pair_bwd(dout, x, g2, cosv, sinv, lane):
    if cosv is not None:
        dout = dout * cosv + _pair_partner(dout * sinv, lane)
    r = _pair_rms(x, lane)
    xhat = x * r
    dxhat = dout * g2
    dx = r * (dxhat - xhat * (_halves(dxhat * xhat, lane) * (1.0 / HEAD_DIM)))
    return dx, dout * xhat


def _lo(x, half, lane):
    if half:
        x = pltpu.roll(x, HEAD_DIM, 1)
    return jnp.where(lane < HEAD_DIM, x, 0.0)


def _pack(even, odd, lane):
    return jnp.where(lane < HEAD_DIM, even, pltpu.roll(odd, HEAD_DIM, 1))


def _place():
    x, y, c = lax.axis_index("x"), lax.axis_index("y"), lax.axis_index("c")
    chips = [(1 - x, y), (x, 1 - y), (1 - x, 1 - y)]
    return x, y, c, chips


class _Gather:
    tag = "_gather"

    def __init__(self, bufs):
        self.bufs = list(bufs)
        nb = len(self.bufs)
        self.out_shape = [jax.ShapeDtypeStruct((b.shape[0], 4) + b.shape[2:], b.dtype) for b in self.bufs]
        self.sems = [pltpu.SemaphoreType.DMA((3 * nb,)), pltpu.SemaphoreType.DMA((3 * nb,)),
                     pltpu.SemaphoreType.DMA((nb,))]

    def _copies(self, ins, outs, sems):
        ssem, rsem, lsem = sems
        nb = len(self.bufs)
        x, y, c, chips = _place()
        mine = 2 * x + y

        def copy(b, p, shard):
            return pltpu.make_async_remote_copy(
                src_ref=ins[b], dst_ref=outs[b].at[:, pl.ds(shard, 1)],
                send_sem=ssem.at[3 * b + p], recv_sem=rsem.at[3 * b + p],
                device_id=(chips[p][0], chips[p][1], c), device_id_type=MESH)

        local = [pltpu.make_async_copy(ins[b], outs[b].at[:, pl.ds(mine, 1)], lsem.at[b]) for b in range(nb)]
        sends = [copy(b, p, mine) for b in range(nb) for p in range(3)]
        recvs = [copy(b, p, 2 * chips[p][0] + chips[p][1]) for b in range(nb) for p in range(3)]
        return local, sends, recvs

    def start(self, ins, outs, sems):
        local, sends, _ = self._copies(ins, outs, sems)
        for cp in local + sends:
            cp.start()

    def wait(self, ins, outs, sems):
        local, sends, recvs = self._copies(ins, outs, sems)
        for cp in recvs:
            cp.wait_recv()
        for cp in sends:
            cp.wait_send()
        for cp in local:
            cp.wait()


class _GatherHalves:
    tag = "_gather_halves"

    def __init__(self, bufs):
        self.bufs = list(bufs)
        nb = len(self.bufs)
        assert all(b.shape[2] % 32 == 0 for b in self.bufs)
        self.out_shape = [jax.ShapeDtypeStruct((b.shape[0], 4) + b.shape[2:], b.dtype) for b in self.bufs]
        self.sems = [pltpu.SemaphoreType.DMA((3 * nb,)), pltpu.SemaphoreType.DMA((3 * nb,)),
                     pltpu.SemaphoreType.DMA((3 * nb,)), pltpu.SemaphoreType.DMA((3 * nb,)),
                     pltpu.SemaphoreType.DMA((nb,))]

    def start(self, ins, outs, sems):
        ssem, rsem, fsem, gsem, lsem = sems
        nb = len(self.bufs)
        x, y, c, chips = _place()
        mine = 2 * x + y
        shard_of = [2 * cx + cy for cx, cy in chips]

        def half(b, h):
            n = self.bufs[b].shape[2] // 2
            return pl.ds(pl.multiple_of(h * n, 16), n)

        def over_links(b, p, shard, h):
            return pltpu.make_async_remote_copy(
                src_ref=ins[b].at[:, :, half(b, h)], dst_ref=outs[b].at[:, pl.ds(shard, 1), half(b, h)],
                send_sem=ssem.at[3 * b + p], recv_sem=rsem.at[3 * b + p],
                device_id=(chips[p][0], chips[p][1], c), device_id_type=MESH)

        def between_cores(b, p, h):
            blk = outs[b].at[:, pl.ds(shard_of[p], 1), half(b, h)]
            return pltpu.make_async_remote_copy(
                src_ref=blk, dst_ref=blk, send_sem=fsem.at[3 * b + p], recv_sem=gsem.at[3 * b + p],
                device_id=(x, y, 1 - c), device_id_type=MESH)

        local = [pltpu.make_async_copy(ins[b], outs[b].at[:, pl.ds(mine, 1)], lsem.at[b]) for b in range(nb)]
        sends = [over_links(b, p, mine, c) for b in range(nb) for p in range(3)]
        for cp in local + sends:
            cp.start()
        handed = []
        for b in range(nb):
            for p in range(3):
                over_links(b, p, shard_of[p], c).wait_recv()
                cp = between_cores(b, p, c)
                cp.start()
                handed.append(cp)
        for b in range(nb):
            for p in range(3):
                between_cores(b, p, 1 - c).wait_recv()
        for cp in sends + handed:
            cp.wait_send()
        for cp in local:
            cp.wait()

    def wait(self, ins, outs, sems):
        pass


class _Scatter:
    tag = "_scatter"

    def __init__(self, bufs):
        self.bufs = list(bufs)
        nb = len(self.bufs)
        self.out_shape = [jax.ShapeDtypeStruct((3, b.shape[0], 1) + b.shape[2:], b.dtype) for b in self.bufs]
        self.sems = [pltpu.SemaphoreType.DMA((3 * nb,)), pltpu.SemaphoreType.DMA((3 * nb,))]

    def _copies(self, ins, outs, sems):
        ssem, rsem = sems
        x, y, c, chips = _place()

        def copy(b, p):
            shard = 2 * chips[p][0] + chips[p][1]
            return pltpu.make_async_remote_copy(
                src_ref=ins[b].at[:, pl.ds(shard, 1)], dst_ref=outs[b].at[p],
                send_sem=ssem.at[3 * b + p], recv_sem=rsem.at[3 * b + p],
                device_id=(chips[p][0], chips[p][1], c), device_id_type=MESH)

        return [copy(b, p) for b in range(len(self.bufs)) for p in range(3)]

    def start(self, ins, outs, sems):
        for cp in self._copies(ins, outs, sems):
            cp.start()

    def wait(self, ins, outs, sems):
        cps = self._copies(ins, outs, sems)
        for cp in cps:
            cp.wait_recv()
        for cp in cps:
            cp.wait_send()


class _Swap:
    tag = "_swap"

    def __init__(self, bufs):
        self.bufs = list(bufs)
        nb = len(self.bufs)
        self.out_shape = [jax.ShapeDtypeStruct(b.shape, b.dtype) for b in self.bufs]
        self.sems = [pltpu.SemaphoreType.DMA((nb,)), pltpu.SemaphoreType.DMA((nb,))]

    def _copies(self, ins, outs, sems):
        ssem, rsem = sems
        x, y, c, _ = _place()
        return [pltpu.make_async_remote_copy(src_ref=ins[b], dst_ref=outs[b], send_sem=ssem.at[b],
                                             recv_sem=rsem.at[b], device_id=(x, y, 1 - c), device_id_type=MESH)
                for b in range(len(self.bufs))]

    def start(self, ins, outs, sems):
        for cp in self._copies(ins, outs, sems):
            cp.start()

    def wait(self, ins, outs, sems):
        cps = self._copies(ins, outs, sems)
        for cp in cps:
            cp.wait_recv()
        for cp in cps:
            cp.wait_send()


class _Multi:
    def __init__(self, riders):
        self.riders = [r for r in riders if r is not None and r.bufs]
        self.tag = "".join(r.tag for r in self.riders)
        self.bufs = [b for r in self.riders for b in r.bufs]
        self.out_shape = [s for r in self.riders for s in r.out_shape]
        self.sems = [s for r in self.riders for s in r.sems]

    def _split(self, ins, outs, sems):
        ob, os_ = 0, 0
        for r in self.riders:
            nb, ns = len(r.bufs), len(r.sems)
            yield r, ins[ob:ob + nb], outs[ob:ob + nb], sems[os_:os_ + ns]
            ob, os_ = ob + nb, os_ + ns

    def start(self, ins, outs, sems):
        for r, i, o, s in self._split(ins, outs, sems):
            r.start(i, o, s)

    def wait(self, ins, outs, sems):
        for r, i, o, s in self._split(ins, outs, sems):
            r.wait(i, o, s)

    def split_outputs(self, got):
        res, ob = [], 0
        for r in self.riders:
            res.append(got[ob:ob + len(r.bufs)])
            ob += len(r.bufs)
        return res


def _run_rider(rider, name):
    nb = len(rider.bufs)

    def body(*refs):
        ins, outs, sems = refs[:nb], refs[nb:2 * nb], refs[2 * nb:]
        rider.start(ins, outs, sems)
        rider.wait(ins, outs, sems)

    hbm = pl.BlockSpec(memory_space=pl.ANY)
    return pl.pallas_call(body, name=name, in_specs=[hbm] * nb, out_specs=[hbm] * nb,
                          out_shape=rider.out_shape, scratch_shapes=rider.sems)(*rider.bufs)


def _call(body, *, name, grid, in_specs, out_specs, out_shape, args, scratch=(), rider=None):
    if rider is None:
        outs = pl.pallas_call(body, name=name, grid=grid, in_specs=list(in_specs), out_specs=list(out_specs),
                              out_shape=list(out_shape), scratch_shapes=list(scratch),
                              compiler_params=_cp())(*args)
        return list(outs), None
    n_in, n_out, n_scr, nb = len(in_specs), len(out_specs), len(scratch), len(rider.bufs)

    def wrapped(*refs):
        cuts = [n_in, nb, n_out, nb, n_scr]
        parts, o = [], 0
        for n in cuts:
            parts.append(refs[o:o + n])
            o += n
        ins, rin, outs, rout, scr = parts
        sems = refs[o:]
        ids = [pl.program_id(k) for k in range(len(grid))]
        first = functools.reduce(jnp.logical_and, [i == 0 for i in ids])
        last = functools.reduce(jnp.logical_and, [i == n - 1 for i, n in zip(ids, grid)])

        @pl.when(first)
        def _():
            rider.start(rin, rout, sems)

        body(*ins, *outs, *scr)

        @pl.when(last)
        def _():
            rider.wait(rin, rout, sems)

    hbm = pl.BlockSpec(memory_space=pl.ANY)
    res = pl.pallas_call(
        wrapped, name=name + rider.tag, grid=grid,
        in_specs=list(in_specs) + [hbm] * nb, out_specs=list(out_specs) + [hbm] * nb,
        out_shape=list(out_shape) + rider.out_shape, scratch_shapes=list(scratch) + rider.sems,
        compiler_params=_cp())(*args, *rider.bufs)
    return list(res[:n_out]), list(res[n_out:])


def _rope_tables(pos, invf, tm):
    S = pos.shape[0]

    def body(pos_ref, f_ref, cos_ref, sin_ref):
        ang = pos_ref[...].astype(F32) * f_ref[...]
        lane = _lane(tm)
        cos_ref[...] = jnp.cos(ang)
        s = jnp.sin(ang)
        sin_ref[...] = jnp.where((lane & (HEAD_DIM - 1)) < HEAD_DIM // 2, -s, s)

    return pl.pallas_call(
        body, name="rope_tables", grid=(S // tm,),
        in_specs=[pl.BlockSpec((tm, 1), lambda i: (i, 0)), pl.BlockSpec((1, SLOT), lambda i: (0, 0))],
        out_specs=[pl.BlockSpec((tm, SLOT), lambda i: (i, 0))] * 2,
        out_shape=[jax.ShapeDtypeStruct((S, SLOT), F32)] * 2,
    )(pos, invf)


def _ffn_fwd(x, g, wf, gfin, tm, tf, rider=None):
    S, D = x.shape
    F = wf[0].shape[0]
    nf = F // tf
    final = gfin is not None

    chunks = [(c, min(FFN_CHUNK, tf - c)) for c in range(0, tf, FFN_CHUNK)]

    def body(*refs):
        if final:
            x_ref, g_ref, w1_ref, w3_ref, w2_ref, gf_ref, h_ref, a_ref, b_ref, t_ref, xn_ref, n_scr, acc = refs
        else:
            x_ref, g_ref, w1_ref, w3_ref, w2_ref, h_ref, a_ref, b_ref, t_ref, n_scr, acc = refs
        j = pl.program_id(1)

        @pl.when(j == 0)
        def _():
            xv = x_ref[...]
            n_scr[...] = (xv * _rms(xv) * g_ref[...]).astype(BF16)
            acc[...] = jnp.zeros_like(acc)

        n = n_scr[...]
        for c0, cw in chunks:
            cols = slice(c0, c0 + cw)
            a = _dot_nt(n, w1_ref[cols, :])
            b = _dot_nt(n, w3_ref[cols, :])
            a_ref[:, cols] = a.astype(BF16)
            b_ref[:, cols] = b.astype(BF16)
            t_ref[:, cols] = (a * _sigmoid(a) * b).astype(BF16)
        acc[...] += _dot_nn(t_ref[...], w2_ref[...])

        @pl.when(j == nf - 1)
        def _():
            h = x_ref[...] + 0.5 * acc[...]
            h_ref[...] = h
            if final:
                xn_ref[...] = h * _rms(h) * gf_ref[...]

    def wspec(k):
        return pl.BlockSpec((tf, D), lambda i, j: (j, 0))

    row = pl.BlockSpec((tm, D), lambda i, j: (i, 0))
    vec = pl.BlockSpec((1, D), lambda i, j: (0, 0))
    act = pl.BlockSpec((tm, tf), lambda i, j: (i, j))
    in_specs = [row, vec, wspec(0), wspec(1), wspec(2)] + ([vec] if final else [])
    out_specs = [row, act, act, act] + ([row] if final else [])
    out_shape = [jax.ShapeDtypeStruct((S, D), F32)] + [jax.ShapeDtypeStruct((S, F), BF16)] * 3 \
        + ([jax.ShapeDtypeStruct((S, D), F32)] if final else [])
    args = [x, g, *wf] + ([gfin] if final else [])
    return _call(body, name="ffn_fwd_final" if final else "ffn_fwd", grid=(S // tm, nf),
                 in_specs=in_specs, out_specs=out_specs, out_shape=out_shape, args=args,
                 scratch=[pltpu.VMEM((tm, D), BF16), pltpu.VMEM((tm, D), F32)], rider=rider)


def _ffn_bwd_act(dh, x, g, a, b, wf, tm, tf, rider=None):
    S, D = x.shape
    F = wf[0].shape[0]
    nf = F // tf

    chunks = [(c, min(FFN_CHUNK, tf - c)) for c in range(0, tf, FFN_CHUNK)]

    def body(dh_ref, x_ref, g_ref, a_ref, b_ref, w1_ref, w3_ref, w2_ref,
             dx_ref, dg_ref, da_ref, db_ref, n_ref, dy_ref, acc):
        i, j = pl.program_id(0), pl.program_id(1)

        @pl.when(j == 0)
        def _():
            xv = x_ref[...]
            n_ref[...] = (xv * _rms(xv) * g_ref[...]).astype(BF16)
            dy_ref[...] = (0.5 * dh_ref[...]).astype(BF16)
            acc[...] = jnp.zeros_like(acc)

            @pl.when(i == 0)
            def _():
                dg_ref[...] = jnp.zeros_like(dg_ref)

        dyv = dy_ref[...]
        for c0, cw in chunks:
            cols = slice(c0, c0 + cw)
            av = a_ref[:, cols].astype(F32)
            bv = b_ref[:, cols].astype(F32)
            sg = _sigmoid(av)
            dt = _dot_nt(dyv, w2_ref[cols, :])
            db_ref[:, cols] = (dt * (av * sg)).astype(BF16)
            da_ref[:, cols] = (dt * bv * (sg * (1.0 + av * (1.0 - sg)))).astype(BF16)
        acc[...] += _dot_nn(da_ref[...], w1_ref[...]) + _dot_nn(db_ref[...], w3_ref[...])

        @pl.when(j == nf - 1)
        def _():
            xv = x_ref[...]
            dx, dgrow = _rms_bwd(acc[...], xv, _rms(xv), g_ref[...])
            dx_ref[...] = dh_ref[...] + dx
            dg_ref[...] += _colsum(dgrow)

    def wspec(k):
        return pl.BlockSpec((tf, D), lambda i, j: (j, 0))

    row = pl.BlockSpec((tm, D), lambda i, j: (i, 0))
    vec = pl.BlockSpec((1, D), lambda i, j: (0, 0))
    act = pl.BlockSpec((tm, tf), lambda i, j: (i, j))
    sd = lambda shp, dt: jax.ShapeDtypeStruct(shp, dt)
    return _call(body, name="ffn_bwd_act", grid=(S // tm, nf),
                 in_specs=[row, row, vec, act, act, wspec(0), wspec(1), wspec(2)],
                 out_specs=[row, vec, act, act, row, row],
                 out_shape=[sd((S, D), F32), sd((1, D), F32), sd((S, F), BF16), sd((S, F), BF16),
                            sd((S, D), BF16), sd((S, D), BF16)],
                 args=[dh, x, g, a, b, *wf], scratch=[pltpu.VMEM((tm, D), F32)], rider=rider)


def _ffn_bwd_w(da, db, t, n, dy, tm, tf, rider=None):
    S, F = da.shape
    D = n.shape[1]
    nt = S // tm

    def body(da_ref, db_ref, t_ref, n_ref, dy_ref, out_ref, acc):
        i = pl.program_id(1)

        @pl.when(i == 0)
        def _():
            acc[...] = jnp.zeros_like(acc)

        both = _dot_tn(jnp.concatenate([da_ref[...], db_ref[...]], axis=1), n_ref[...])
        acc[0] += both[0:tf]
        acc[1] += both[tf:2 * tf]
        acc[2] += _dot_tn(t_ref[...], dy_ref[...])

        @pl.when(i == nt - 1)
        def _():
            out_ref[...] = acc[...].astype(BF16)

    act = pl.BlockSpec((tm, tf), lambda j, i: (i, j))
    row = pl.BlockSpec((tm, D), lambda j, i: (i, 0))
    outs, got = _call(body, name="ffn_bwd_w", grid=(F // tf, nt),
                      in_specs=[act, act, act, row, row],
                      out_specs=[pl.BlockSpec((3, tf, D), lambda j, i: (0, j, 0))],
                      out_shape=[jax.ShapeDtypeStruct((3, F, D), BF16)],
                      args=[da, db, t, n, dy], scratch=[pltpu.VMEM((3, tf, D), F32)], rider=rider)
    return outs[0], got


def _proj_fwd(h, g, w_inp, tm):
    S, D = h.shape

    def body(h_ref, g_ref, w_ref, pu_ref, ph_ref, n_ref):
        hv = h_ref[...]
        n = (hv * _rms(hv) * g_ref[...]).astype(BF16)
        n_ref[...] = n
        pu_ref[...] = _dot_nt(n, w_ref[0:PU, :])
        ph_ref[...] = _dot_nt(n, w_ref[PU:PP, :])

    cur = lambda w: pl.BlockSpec((tm, w), lambda i: (i, 0))
    return pl.pallas_call(
        body, name="proj_fwd", grid=(S // tm,),
        in_specs=[cur(D), pl.BlockSpec((1, D), lambda i: (0, 0)), pl.BlockSpec((PP, D), lambda i: (0, 0))],
        out_specs=[cur(PU), cur(PH), cur(D)],
        out_shape=[jax.ShapeDtypeStruct((S, PU), F32), jax.ShapeDtypeStruct((S, PH), F32),
                   jax.ShapeDtypeStruct((S, D), BF16)],
        compiler_params=_cp(),
    )(h, g, w_inp)


def _glu(u):
    return u[:, :CONV_CH] * _sigmoid(u[:, CONV_CH:2 * CONV_CH])


def _shifted_copies(ext8):
    n = ext8.shape[1]
    for b in range(1, 8):
        ext8[b, 0:n - 8, :] = ext8[0, b:b + n - 8, :]


def _window(ext8, off, rows, r0=0):
    return ext8[off % 8, pl.ds(r0 + (off - off % 8), rows), :]


def _layer_norm_stats(yc):
    mu = jnp.mean(yc, axis=-1, keepdims=True)
    d = yc - mu
    rstd = lax.rsqrt(jnp.mean(d * d, axis=-1, keepdims=True) + EPS)
    return d * rstd, rstd


def _mem_kv_fwd(mem, g, w_mkvp, gk):
    M, D = mem.shape
    W = SLOT * N_MEMH

    def body(mem_ref, g_ref, w_ref, gk_ref, nm_ref, raw_ref, mk_ref, mv_ref):
        mv_ = mem_ref[...]
        nm = (mv_ * _rms(mv_) * g_ref[...]).astype(BF16)
        nm_ref[...] = nm
        raw = _dot_nn(nm, w_ref[...])
        raw_ref[...] = raw
        for hh in range(N_MEMH):
            sl = slice(SLOT * hh, SLOT * (hh + 1))
            mk_ref[:, sl] = _slot_norm(raw[:, sl], gk_ref[...]).astype(BF16)
        mv_ref[...] = raw[:, W:].astype(BF16)

    sd = jax.ShapeDtypeStruct
    return pl.pallas_call(
        body, name="mem_kv_fwd",
        out_shape=[sd((M, D), BF16), sd((M, 2 * W), F32), sd((M, W), BF16), sd((M, W), BF16)],
        compiler_params=_cp(),
    )(mem, g, w_mkvp, gk)


def _mem_kv_bwd(dmk, dmv, raw, nm, mem, g, w_mkvp, gk):
    M, D = mem.shape
    W = SLOT * N_MEMH

    def body(dmk_ref, dmv_ref, raw_ref, nm_ref, mem_ref, g_ref, w_ref, gk_ref, dw_ref, dg_ref, dgk_ref, draw):
        dgk = jnp.zeros((1, SLOT), F32)
        for hh in range(N_MEMH):
            sl = slice(SLOT * hh, SLOT * (hh + 1))
            dx, prod = _slot_norm_bwd(dmk_ref[:, sl], raw_ref[:, sl], gk_ref[...])
            draw[:, sl] = dx.astype(BF16)
            dgk = dgk + _colsum(prod)
        dgk_ref[...] = dgk
        draw[:, W:] = dmv_ref[...].astype(BF16)
        dr = draw[...]
        dw_ref[...] = _dot_tn(nm_ref[...], dr)
        dnm = _dot_nt(dr, w_ref[...])
        mv_ = mem_ref[...]
        dg_ref[...] = _colsum(dnm * (mv_ * _rms(mv_)))

    sd = jax.ShapeDtypeStruct
    return pl.pallas_call(
        body, name="mem_kv_bwd",
        out_shape=[sd((D, 2 * W), F32), sd((1, D), F32), sd((1, SLOT), F32)],
        scratch_shapes=[pltpu.VMEM((M, 2 * W), BF16)],
        compiler_params=_cp(),
    )(dmk, dmv, raw, nm, mem, g, w_mkvp, gk)


def _mixer_fwd(pu, ph, h, cosT, sinT, conv_w, conv_b, ln_g, ln_b, gq, gk, sinks, mk, mv, gqm, w_outp, tm, rider=None):
    S, D = h.shape
    M = mk.shape[0]
    nb = tm // BLK
    nblocks = S // BLK

    def body(pu_ref, pup_ref, p_ref, ph_ref, h_ref, cos_ref, cosh_ref, sin_ref, sinh_ref, cw_ref, cb_ref,
             lg_ref, lb_ref, gq_ref, gk_ref, sink_ref, mk_ref, mv_ref, gqm_ref, wo_ref,
             h2_ref, y_ref, yc_ref, lse_ref, ext, y_scr):
        i = pl.program_id(0)
        not_first = (i > 0).astype(F32)
        lane = _lane(tm)
        lane_e = _lane(tm + BLK)

        ext[0, 0:HALO, :] = _glu(pup_ref[...]) * not_first
        ext[0, HALO:HALO + tm, :] = _glu(pu_ref[...])
        _shifted_copies(ext)

        def rows_chunk(r, carry):
            r0 = pl.multiple_of(r * CONV_ROWS, CONV_ROWS)
            yc = jnp.zeros((CONV_ROWS, CONV_CH), F32) + cb_ref[...]
            for k in range(CONV_WIDTH):
                yc = yc + cw_ref[k:k + 1, :] * _window(ext, HALO - (CONV_WIDTH - 1) + k, CONV_ROWS, r0)
            yc_ref[pl.ds(r0, CONV_ROWS), :] = yc
            z, _ = _layer_norm_stats(yc)
            ln = z * lg_ref[...] + lb_ref[...]
            y_scr[pl.ds(r0, CONV_ROWS), 0:CONV_CH] = (ln * _sigmoid(ln)).astype(BF16)
            return carry

        lax.fori_loop(0, tm // CONV_ROWS, rows_chunk, 0)

        cos_e = jnp.concatenate([cosh_ref[...], cos_ref[...]], axis=0)
        sin_e = jnp.concatenate([sinh_ref[...], sin_ref[...]], axis=0)
        qi = lax.broadcasted_iota(jnp.int32, (GROUP * BLK, 2 * BLK), 0) & (BLK - 1)
        kj = lax.broadcasted_iota(jnp.int32, (GROUP * BLK, 2 * BLK), 1)
        band = (kj > qi) & (kj <= qi + BLK)
        band0 = band & ((kj >= BLK) | (i > 0))
        lse = jnp.zeros((tm, SLOT), F32)
        k_pair = jnp.concatenate([ph_ref[:, KO:KO + SLOT], p_ref[:, KO:KO + SLOT]], axis=0)
        k_pair = _pair_fwd(k_pair, gk_ref[...], cos_e, sin_e, lane_e)
        v_pair = jnp.concatenate([ph_ref[:, VO:VO + SLOT], p_ref[:, VO:VO + SLOT]], axis=0)
        k_e = [_lo(k_pair, kvh, lane_e).astype(BF16) for kvh in range(N_KV)]
        v_e = [_lo(v_pair, kvh, lane_e).astype(BF16) for kvh in range(N_KV)]
        q_lo = []
        for j in range(N_Q // 2):
            q_pair = _pair_fwd(p_ref[:, QO + SLOT * j:QO + SLOT * (j + 1)], gq_ref[...],
                               cos_ref[...], sin_ref[...], lane)
            q_lo += [_lo(q_pair, 0, lane).astype(BF16), _lo(q_pair, 1, lane).astype(BF16)]
        outs = [[] for _ in range(N_Q)]
        lses = [[] for _ in range(N_Q)]
        for kvh in range(N_KV):
            hs = [GROUP * kvh + gi for gi in range(GROUP)]
            sink3 = jnp.concatenate([jnp.full((BLK, 1), sink_ref[h], F32) for h in hs], axis=0)
            for m in range(nb):
                rows = slice(BLK * m, BLK * (m + 1))
                win = slice(BLK * m, BLK * (m + 2))
                q3 = jnp.concatenate([q_lo[h][rows] for h in hs], axis=0)
                s = _dot_nt(q3, k_e[kvh][win]) * SCALE
                s = jnp.where(band0 if m == 0 else band, s, NEG)
                mx = jnp.maximum(jnp.max(s, axis=-1, keepdims=True), sink3)
                e = jnp.exp(s - mx)
                den = jnp.sum(e, axis=-1, keepdims=True) + jnp.exp(sink3 - mx)
                o3 = _dot_nn((e / den).astype(BF16), v_e[kvh][win])
                l3 = mx + jnp.log(den)
                for gi, h in enumerate(hs):
                    outs[h].append(o3[BLK * gi:BLK * (gi + 1)])
                    lses[h].append(l3[BLK * gi:BLK * (gi + 1)])
        for h in range(N_Q):
            lse = jnp.where(lane == h, jnp.concatenate(lses[h], axis=0), lse)
        for j in range(N_Q // 2):
            y_scr[:, YS + SLOT * j:YS + SLOT * (j + 1)] = _pack(
                jnp.concatenate(outs[2 * j], axis=0), jnp.concatenate(outs[2 * j + 1], axis=0), lane).astype(BF16)

        heads = []
        for hm in range(N_MEMH):
            ms = slice(SLOT * hm, SLOT * (hm + 1))
            if hm % 2 == 0:
                qm_pair = _pair_fwd(p_ref[:, MO + SLOT * (hm // 2):MO + SLOT * (hm // 2 + 1)], gqm_ref[...],
                                    None, None, lane)
            s = _dot_nt(_lo(qm_pair, hm % 2, lane).astype(BF16), mk_ref[:, ms]) * SCALE
            mx = jnp.max(s, axis=-1, keepdims=True)
            e = jnp.exp(s - mx)
            den = jnp.sum(e, axis=-1, keepdims=True)
            heads.append(_dot_nn((e / den).astype(BF16), mv_ref[:, ms]))
            lse = jnp.where(lane == N_Q + hm, mx + jnp.log(den), lse)
            if hm % 2 == 1:
                y_scr[:, YM + SLOT * (hm // 2):YM + SLOT * (hm // 2 + 1)] = _pack(heads[-2], heads[-1], lane).astype(BF16)
        lse_ref[...] = lse.T[0:STAT_ROWS, :]

        yv = y_scr[...]
        y_ref[...] = yv
        h2_ref[...] = h_ref[...] + _dot_nn(yv, wo_ref[...])

    cur = lambda w: pl.BlockSpec((tm, w), lambda i: (i, 0))
    prev = lambda w: pl.BlockSpec((BLK, w), lambda i: (jnp.maximum(i * nb - 1, 0), 0))
    full = lambda a: pl.BlockSpec(a.shape, lambda i: (0,) * a.ndim)
    sd = jax.ShapeDtypeStruct
    prev32 = pl.BlockSpec((HALO, PU), lambda i: (jnp.maximum(i * (tm // HALO) - 1, 0), 0))
    return _call(
        body, name="mixer_fwd", grid=(S // tm,),
        in_specs=[cur(PU), prev32, cur(PH), prev(PH), cur(D), cur(SLOT), prev(SLOT), cur(SLOT), prev(SLOT),
                  full(conv_w), full(conv_b), full(ln_g), full(ln_b), full(gq), full(gk),
                  pl.BlockSpec(memory_space=pltpu.SMEM), full(mk), full(mv), full(gqm), full(w_outp)],
        out_specs=[cur(D), cur(YP), cur(CONV_CH), pl.BlockSpec((STAT_ROWS, tm), lambda i: (0, i))],
        out_shape=[sd((S, D), F32), sd((S, YP), BF16), sd((S, CONV_CH), F32), sd((STAT_ROWS, S), F32)],
        args=[pu, pu, ph, ph, h, cosT, cosT, sinT, sinT, conv_w, conv_b, ln_g, ln_b, gq, gk, sinks, mk, mv, gqm,
              w_outp],
        scratch=[pltpu.VMEM((8, tm + HALO, CONV_CH), F32), pltpu.VMEM((tm, YP), BF16)], rider=rider)


def _outproj_bwd(dh2, y, yc, ln_g, ln_b, w_outp, tm):
    S, D = dh2.shape

    def body(dh_ref, y_ref, yc_ref, lg_ref, lb_ref, wo_ref, dyc_ref, do_ref, del_ref, dwo_ref, dlg_ref, dlb_ref):
        i = pl.program_id(0)

        @pl.when(i == 0)
        def _():
            dwo_ref[...] = jnp.zeros_like(dwo_ref)
            dlg_ref[...] = jnp.zeros_like(dlg_ref)
            dlb_ref[...] = jnp.zeros_like(dlb_ref)

        dhb = dh_ref[...].astype(BF16)
        yv = y_ref[...]
        dy = _dot_nt(dhb, wo_ref[...])
        dwo_ref[...] += _dot_tn(yv, dhb)

        z, rstd = _layer_norm_stats(yc_ref[...])
        ln = z * lg_ref[...] + lb_ref[...]
        sg = _sigmoid(ln)
        dln = dy[:, 0:CONV_CH] * (sg * (1.0 + ln * (1.0 - sg)))
        dlg_ref[...] += _colsum(dln * z)
        dlb_ref[...] += _colsum(dln)
        dz = dln * lg_ref[...]
        dyc_ref[...] = rstd * (dz - jnp.mean(dz, axis=-1, keepdims=True)
                               - z * jnp.mean(dz * z, axis=-1, keepdims=True))
        do_ref[...] = dy[:, CONV_CH:].astype(BF16)

        lane = _lane(tm)
        delta = jnp.zeros((tm, SLOT), F32)
        for j in range(NH // 2):
            sl = slice(YS + SLOT * j, YS + SLOT * (j + 1))
            prod = dy[:, sl] * yv[:, sl].astype(F32)
            lo = jnp.sum(jnp.where(lane < HEAD_DIM, prod, 0.0), axis=-1, keepdims=True)
            hi = jnp.sum(jnp.where(lane < HEAD_DIM, 0.0, prod), axis=-1, keepdims=True)
            delta = jnp.where(lane == 2 * j, lo, jnp.where(lane == 2 * j + 1, hi, delta))
        del_ref[...] = delta.T[0:STAT_ROWS, :]

    cur = lambda w: pl.BlockSpec((tm, w), lambda i: (i, 0))
    full = lambda a: pl.BlockSpec(a.shape, lambda i: (0,) * a.ndim)
    sd = jax.ShapeDtypeStruct
    return pl.pallas_call(
        body, name="outproj_bwd", grid=(S // tm,),
        in_specs=[cur(D), cur(YP), cur(CONV_CH), full(ln_g), full(ln_b), full(w_outp)],
        out_specs=[cur(CONV_CH), cur(YH), pl.BlockSpec((STAT_ROWS, tm), lambda i: (0, i)),
                   pl.BlockSpec((YP, D), lambda i: (0, 0)),
                   pl.BlockSpec((1, CONV_CH), lambda i: (0, 0)), pl.BlockSpec((1, CONV_CH), lambda i: (0, 0))],
        out_shape=[sd((S, CONV_CH), F32), sd((S, YH), BF16), sd((STAT_ROWS, S), F32), sd((YP, D), F32),
                   sd((1, CONV_CH), F32), sd((1, CONV_CH), F32)],
        compiler_params=_cp(),
    )(dh2, y, yc, ln_g, ln_b, w_outp)


def _conv_bwd(pu, dyc, conv_w, tm):
    S = pu.shape[0]
    nt = S // tm
    nh = tm // HALO

    def body(pu_ref, pup_ref, dy_ref, dyn_ref, cw_ref, dpu_ref, dcw_ref, dcb_ref, ext, ext2, dcw8):
        i = pl.program_id(0)

        @pl.when(i == 0)
        def _():
            dcw8[...] = jnp.zeros_like(dcw8)
            dcb_ref[...] = jnp.zeros_like(dcb_ref)

        not_first = (i > 0).astype(F32)
        not_last = (i < nt - 1).astype(F32)
        ext[0, 0:HALO, :] = _glu(pup_ref[...]) * not_first
        ext[0, HALO:HALO + tm, :] = _glu(pu_ref[...])
        _shifted_copies(ext)
        ext2[0, 0:tm, :] = dy_ref[...]
        ext2[0, tm:tm + HALO, :] = dyn_ref[...] * not_last
        _shifted_copies(ext2)
        dcb_ref[...] += _colsum(dy_ref[...])

        def rows_chunk(r, carry):
            r0 = pl.multiple_of(r * CONV_ROWS, CONV_ROWS)
            dyc_ = dy_ref[pl.ds(r0, CONV_ROWS), :]
            dyg = jnp.zeros((CONV_ROWS, CONV_CH), F32)
            for k in range(CONV_WIDTH):
                prod = dyc_ * _window(ext, HALO - (CONV_WIDTH - 1) + k, CONV_ROWS, r0)
                dcw8[k] += jnp.sum(prod.reshape(CONV_ROWS // 8, 8, CONV_CH), axis=0)
                dyg = dyg + cw_ref[k:k + 1, :] * _window(ext2, CONV_WIDTH - 1 - k, CONV_ROWS, r0)
            u = pu_ref[pl.ds(r0, CONV_ROWS), :]
            a_, sg = u[:, :CONV_CH], _sigmoid(u[:, CONV_CH:])
            dpu_ref[pl.ds(r0, CONV_ROWS), 0:CONV_CH] = (dyg * sg).astype(BF16)
            dpu_ref[pl.ds(r0, CONV_ROWS), CONV_CH:PU] = (dyg * a_ * sg * (1.0 - sg)).astype(BF16)
            return carry

        lax.fori_loop(0, tm // CONV_ROWS, rows_chunk, 0)

        @pl.when(i == nt - 1)
        def _():
            dcw_ref[...] = jnp.sum(dcw8[...], axis=1)

    cur = lambda w: pl.BlockSpec((tm, w), lambda i: (i, 0))
    prev = lambda w: pl.BlockSpec((HALO, w), lambda i: (jnp.maximum(i * nh - 1, 0), 0))
    nxt = lambda w: pl.BlockSpec((HALO, w), lambda i: (jnp.minimum((i + 1) * nh, S // HALO - 1), 0))
    acc = lambda r, w: pl.BlockSpec((r, w), lambda i: (0, 0))
    sd = jax.ShapeDtypeStruct
    return pl.pallas_call(
        body, name="conv_bwd", grid=(nt,),
        in_specs=[cur(PU), prev(PU), cur(CONV_CH), nxt(CONV_CH), acc(32, CONV_CH)],
        out_specs=[cur(PU), acc(32, CONV_CH), acc(1, CONV_CH)],
        out_shape=[sd((S, PU), BF16), sd((32, CONV_CH), F32), sd((1, CONV_CH), F32)],
        scratch_shapes=[pltpu.VMEM((8, tm + HALO, CONV_CH), F32), pltpu.VMEM((8, tm + HALO, CONV_CH), F32),
                        pltpu.VMEM((32, 8, CONV_CH), F32)],
        compiler_params=_cp(),
    )(pu, pu, dyc, dyc, conv_w)


def _attn_bwd(p, do, lse, delta, cosT, sinT, gq, gk, sinks, mk, mv, gqm, tm, rider=None):
    S = p.shape[0]
    M = mk.shape[0]
    nb = tm // BLK
    nt = S // tm
    nblocks = S // BLK
    W = SLOT * N_MEMH

    def body(p_ref, pp_ref, pn_ref, dy_ref, dyn_ref, lse_ref, lsen_ref, del_ref, deln_ref,
             cos_ref, cosp_ref, cosn_ref, sin_ref, sinp_ref, sinn_ref,
             gq_ref, gk_ref, sink_ref, mk_ref, mv_ref, gqm_ref,
             dp_ref, dgq_ref, dgk_ref, dgqm_ref, dsink_ref, dmk_ref, dmv_ref):
        i = pl.program_id(0)

        @pl.when(i == 0)
        def _():
            for r in (dgq_ref, dgk_ref, dgqm_ref, dsink_ref, dmk_ref, dmv_ref):
                r[...] = jnp.zeros_like(r)

        lane = _lane(tm)
        lane_e = _lane(tm + BLK)

        cos_k = jnp.concatenate([cosp_ref[...], cos_ref[...]], axis=0)
        sin_k = jnp.concatenate([sinp_ref[...], sin_ref[...]], axis=0)
        cos_q = jnp.concatenate([cos_ref[...], cosn_ref[...]], axis=0)
        sin_q = jnp.concatenate([sin_ref[...], sinn_ref[...]], axis=0)
        lse_e = jnp.concatenate([lse_ref[...], lsen_ref[...]], axis=1)
        del_e = jnp.concatenate([del_ref[...], deln_ref[...]], axis=1)
        kj = lax.broadcasted_iota(jnp.int32, (BLK, GROUP * BLK), 0)
        qi = lax.broadcasted_iota(jnp.int32, (BLK, GROUP * BLK), 1) & (BLK - 1)
        diag = kj <= qi
        offd = kj > qi
        dgq = jnp.zeros((1, SLOT), F32)
        dgk = jnp.zeros((1, SLOT), F32)
        dsink = jnp.zeros((1, SLOT), F32)
        lane1 = lax.broadcasted_iota(jnp.int32, (1, SLOT), 1)
        k_pair = jnp.concatenate([pp_ref[:, KO:KO + SLOT], p_ref[:, KO:KO + SLOT]], axis=0)
        k_pair = _pair_fwd(k_pair, gk_ref[...], cos_k, sin_k, lane_e)
        v_pair = jnp.concatenate([pp_ref[:, VO:VO + SLOT], p_ref[:, VO:VO + SLOT]], axis=0)
        k_e = [_lo(k_pair, kvh, lane_e).astype(BF16) for kvh in range(N_KV)]
        v_e = [_lo(v_pair, kvh, lane_e).astype(BF16) for kvh in range(N_KV)]
        dk = [[jnp.zeros((BLK, SLOT), F32) for _ in range(nb)] for _ in range(N_KV)]
        dv = [[jnp.zeros((BLK, SLOT), F32) for _ in range(nb)] for _ in range(N_KV)]
        q_e, do_e = [], []
        for j in range(N_Q // 2):
            js = slice(SLOT * j, SLOT * (j + 1))
            q_pair = _pair_fwd(jnp.concatenate([p_ref[:, js], pn_ref[:, js]], axis=0), gq_ref[...],
                               cos_q, sin_q, lane_e)
            do_pair = jnp.concatenate([dy_ref[:, js], dyn_ref[:, js]], axis=0).astype(F32)
            for half in range(2):
                q_e.append(_lo(q_pair, half, lane_e).astype(BF16))
                do_e.append(_lo(do_pair, half, lane_e).astype(BF16))
        dq_heads = [None] * N_Q
        for kvh in range(N_KV):
            hs = [GROUP * kvh + gi for gi in range(GROUP)]
            dq3 = [None] * nb
            for m in range(nb + 1):
                rows = slice(BLK * m, BLK * (m + 1))
                q3 = jnp.concatenate([q_e[h][rows] for h in hs], axis=0)
                do3 = jnp.concatenate([do_e[h][rows] for h in hs], axis=0)
                lb3 = jnp.concatenate([lse_e[h:h + 1, rows] for h in hs], axis=1)
                db3 = jnp.concatenate([del_e[h:h + 1, rows] for h in hs], axis=1)
                for n in (m - 1, m):
                    if n == nb:
                        continue
                    krows = slice(BLK * (n + 1), BLK * (n + 2))
                    kb, vb = k_e[kvh][krows], v_e[kvh][krows]
                    s = _dot_nt(kb, q3) * SCALE
                    mask = diag if n == m else offd
                    if n == -1:
                        mask = mask & (i > 0)
                    if m == nb:
                        mask = mask & (i < nt - 1)
                    prob = jnp.where(mask, jnp.exp(jnp.where(mask, s - lb3, NEG)), 0.0)
                    dpb = _dot_nt(vb, do3)
                    ds = (prob * (dpb - db3) * SCALE).astype(BF16)
                    if m < nb:
                        dqc = _dot_tn(ds, kb)
                        dq3[m] = dqc if dq3[m] is None else dq3[m] + dqc
                    if n >= 0:
                        dk[kvh][n] = dk[kvh][n] + _dot_nn(ds, q3)
                        dv[kvh][n] = dv[kvh][n] + _dot_nn(prob.astype(BF16), do3)
            for gi, h in enumerate(hs):
                dq_heads[h] = jnp.concatenate([dq3[m][BLK * gi:BLK * (gi + 1)] for m in range(nb)], axis=0)
                psink = jnp.exp(sink_ref[h] - lse_e[h:h + 1, 0:tm])
                dsink = dsink + jnp.where(
                    lane1 == h, -jnp.sum(psink * del_e[h:h + 1, 0:tm], axis=-1, keepdims=True), 0.0)
        for j in range(N_Q // 2):
            js = slice(SLOT * j, SLOT * (j + 1))
            dqr, prod = _pair_bwd(_pack(dq_heads[2 * j], dq_heads[2 * j + 1], lane), p_ref[:, js], gq_ref[...],
                                  cos_ref[...], sin_ref[...], lane)
            dp_ref[:, js] = dqr.astype(BF16)
            dgq = dgq + _colsum(prod)
        dk_pair = _pack(jnp.concatenate(dk[0], axis=0), jnp.concatenate(dk[1], axis=0), lane)
        dkr, prod = _pair_bwd(dk_pair, p_ref[:, KO:KO + SLOT], gk_ref[...], cos_ref[...], sin_ref[...], lane)
        dp_ref[:, KO:KO + SLOT] = dkr.astype(BF16)
        dp_ref[:, VO:VO + SLOT] = _pack(jnp.concatenate(dv[0], axis=0), jnp.concatenate(dv[1], axis=0),
                                        lane).astype(BF16)
        dgq_ref[...] += dgq
        dgk_ref[...] += _colsum(prod)
        dsink_ref[...] += dsink

        dgqm = jnp.zeros((1, SLOT), F32)
        dq_heads = []
        for hm in range(N_MEMH):
            ms = slice(SLOT * hm, SLOT * (hm + 1))
            js = slice(MO + SLOT * (hm // 2), MO + SLOT * (hm // 2 + 1))
            os_ = slice(SLOT * ((N_Q + hm) // 2), SLOT * ((N_Q + hm) // 2 + 1))
            if hm % 2 == 0:
                qm_pair = _pair_fwd(p_ref[:, js], gqm_ref[...], None, None, lane)
                do_pair = dy_ref[:, os_].astype(F32)
            qm = _lo(qm_pair, hm % 2, lane).astype(BF16)
            dob = _lo(do_pair, hm % 2, lane).astype(BF16)
            kb, vb = mk_ref[:, ms], mv_ref[:, ms]
            s = _dot_nt(kb, qm) * SCALE
            prob = jnp.exp(s - lse_ref[N_Q + hm:N_Q + hm + 1, :])
            dpb = _dot_nt(vb, dob)
            ds = (prob * (dpb - del_ref[N_Q + hm:N_Q + hm + 1, :]) * SCALE).astype(BF16)
            dq_heads.append(_dot_tn(ds, kb))
            dmk_ref[:, ms] += _dot_nn(ds, qm)
            dmv_ref[:, ms] += _dot_nn(prob.astype(BF16), dob)
            if hm % 2 == 1:
                dqr, prod = _pair_bwd(_pack(dq_heads[-2], dq_heads[-1], lane), p_ref[:, js], gqm_ref[...],
                                      None, None, lane)
                dp_ref[:, js] = dqr.astype(BF16)
                dgqm = dgqm + _colsum(prod)
        dgqm_ref[...] += dgqm

    cur = lambda w: pl.BlockSpec((tm, w), lambda i: (i, 0))
    prev = lambda w: pl.BlockSpec((BLK, w), lambda i: (jnp.maximum(i * nb - 1, 0), 0))
    nxt = lambda w: pl.BlockSpec((BLK, w), lambda i: (jnp.minimum((i + 1) * nb, nblocks - 1), 0))
    full = lambda a: pl.BlockSpec(a.shape, lambda i: (0,) * a.ndim)
    acc = lambda r, w: pl.BlockSpec((r, w), lambda i: (0, 0))
    sd = jax.ShapeDtypeStruct
    stat = pl.BlockSpec((STAT_ROWS, tm), lambda i: (0, i))
    stat_n = pl.BlockSpec((STAT_ROWS, BLK), lambda i: (0, jnp.minimum((i + 1) * nb, nblocks - 1)))
    return _call(
        body, name="attn_bwd", grid=(nt,),
        in_specs=[cur(PH), prev(PH), nxt(PH), cur(YH), nxt(YH), stat, stat_n, stat, stat_n,
                  cur(SLOT), prev(SLOT), nxt(SLOT), cur(SLOT), prev(SLOT), nxt(SLOT),
                  full(gq), full(gk), pl.BlockSpec(memory_space=pltpu.SMEM), full(mk), full(mv), full(gqm)],
        out_specs=[cur(PH), acc(1, SLOT), acc(1, SLOT), acc(1, SLOT), acc(1, SLOT), acc(M, W), acc(M, W)],
        out_shape=[sd((S, PH), BF16), sd((1, SLOT), F32), sd((1, SLOT), F32), sd((1, SLOT), F32),
                   sd((1, SLOT), F32), sd((M, W), F32), sd((M, W), F32)],
        args=[p, p, p, do, do, lse, lse, delta, delta, cosT, cosT, cosT, sinT, sinT, sinT,
              gq, gk, sinks, mk, mv, gqm],
        rider=rider)


def _proj_bwd(dpu, dph, h, dh2, g, n, w_inp, tm):
    S, D = h.shape

    def body(dpu_ref, dph_ref, h_ref, dh2_ref, g_ref, n_ref, w_ref, dh_ref, dg_ref, dw_ref):
        i = pl.program_id(0)

        @pl.when(i == 0)
        def _():
            dg_ref[...] = jnp.zeros_like(dg_ref)
            dw_ref[...] = jnp.zeros_like(dw_ref)

        dpu, dph, nv = dpu_ref[...], dph_ref[...], n_ref[...]
        dn = _dot_nn(dpu, w_ref[0:PU, :]) + _dot_nn(dph, w_ref[PU:PP, :])
        dw_ref[0:PU, :] += _dot_tn(dpu, nv)
        dw_ref[PU:PP, :] += _dot_tn(dph, nv)
        hv = h_ref[...]
        dx, dgrow = _rms_bwd(dn, hv, _rms(hv), g_ref[...])
        dh_ref[...] = dh2_ref[...] + dx
        dg_ref[...] += _colsum(dgrow)

    cur = lambda w: pl.BlockSpec((tm, w), lambda i: (i, 0))
    sd = jax.ShapeDtypeStruct
    return pl.pallas_call(
        body, name="proj_bwd", grid=(S // tm,),
        in_specs=[cur(PU), cur(PH), cur(D), cur(D), pl.BlockSpec((1, D), lambda i: (0, 0)), cur(D),
                  pl.BlockSpec((PP, D), lambda i: (0, 0))],
        out_specs=[cur(D), pl.BlockSpec((1, D), lambda i: (0, 0)), pl.BlockSpec((PP, D), lambda i: (0, 0))],
        out_shape=[sd((S, D), F32), sd((1, D), F32), sd((PP, D), F32)],
        compiler_params=_cp(),
    )(dpu, dph, h, dh2, g, n, w_inp)


def _norm_bwd(dxn, h, g, tm):
    S, D = h.shape

    def body(d_ref, h_ref, g_ref, dh_ref, dg_ref):
        @pl.when(pl.program_id(0) == 0)
        def _():
            dg_ref[...] = jnp.zeros_like(dg_ref)

        hv = h_ref[...]
        dx, dgrow = _rms_bwd(d_ref[...], hv, _rms(hv), g_ref[...])
        dh_ref[...] = dx
        dg_ref[...] += _colsum(dgrow)

    cur = pl.BlockSpec((tm, D), lambda i: (i, 0))
    vec = pl.BlockSpec((1, D), lambda i: (0, 0))
    return pl.pallas_call(
        body, name="norm_bwd", grid=(S // tm,), in_specs=[cur, cur, vec], out_specs=[cur, vec],
        out_shape=[jax.ShapeDtypeStruct((S, D), F32), jax.ShapeDtypeStruct((1, D), F32)],
        compiler_params=_cp(),
    )(dxn, h, g)


def _loss_bwd(xn, h, g, target, tm):
    S, D = h.shape

    def body(y_ref, h_ref, g_ref, t_ref, loss_ref, dh_ref, dg_ref):
        @pl.when(pl.program_id(0) == 0)
        def _():
            dg_ref[...] = jnp.zeros_like(dg_ref)
            loss_ref[...] = jnp.zeros_like(loss_ref)

        err = y_ref[...] - t_ref[...]
        part = jnp.sum(jnp.mean(err * err, axis=-1, keepdims=True), axis=0, keepdims=True)
        loss_ref[...] += 0.5 * part
        hv = h_ref[...]
        dx, dgrow = _rms_bwd(err * (1.0 / D), hv, _rms(hv), g_ref[...])
        dh_ref[...] = dx
        dg_ref[...] += _colsum(dgrow)

    cur = pl.BlockSpec((tm, D), lambda i: (i, 0))
    vec = pl.BlockSpec((1, D), lambda i: (0, 0))
    return pl.pallas_call(
        body, name="loss_bwd", grid=(S // tm,), in_specs=[cur, cur, vec, cur],
        out_specs=[pl.BlockSpec((1, SLOT), lambda i: (0, 0)), cur, vec],
        out_shape=[jax.ShapeDtypeStruct((1, SLOT), F32), jax.ShapeDtypeStruct((S, D), F32),
                   jax.ShapeDtypeStruct((1, D), F32)],
        compiler_params=_cp(),
    )(xn, h, g, target)


def _all_gather_small(buf):
    _, R, W = buf.shape

    def body(in_ref, out_ref, ssem, rsem, lsem):
        x, y, c, _ = _place()
        me = 4 * x + 2 * y + c
        local = pltpu.make_async_copy(in_ref, out_ref.at[pl.ds(me, 1)], lsem)
        local.start()

        def copy(k, block):
            fx, fy, fc = (k >> 2) & 1, (k >> 1) & 1, k & 1
            peer = (x ^ fx, y ^ fy, c ^ fc)
            return pltpu.make_async_remote_copy(
                src_ref=in_ref, dst_ref=out_ref.at[pl.ds(block, 1)], send_sem=ssem.at[k - 1],
                recv_sem=rsem.at[k - 1], device_id=peer, device_id_type=MESH)

        sends = [copy(k, me) for k in range(1, 8)]
        for cp in sends:
            cp.start()
        for k in range(1, 8):
            copy(k, me ^ k).wait_recv()
        for cp in sends:
            cp.wait_send()
        local.wait()

    hbm = pl.BlockSpec(memory_space=pl.ANY)
    return pl.pallas_call(
        body, name="all_gather_small", in_specs=[hbm], out_specs=hbm,
        out_shape=jax.ShapeDtypeStruct((8, R, W), buf.dtype),
        scratch_shapes=[pltpu.SemaphoreType.DMA((7,)), pltpu.SemaphoreType.DMA((7,)), pltpu.SemaphoreType.DMA],
    )(buf)


def _row_tile(n, cap=1024):
    for t in range(min(n, cap) // 8 * 8, 7, -8):
        if n % t == 0:
            return t
    return n


def _sum4(own, recv):
    n, rows, D = own.shape
    tr = _row_tile(rows)

    def body(o_ref, r0_ref, r1_ref, r2_ref, out_ref):
        out_ref[...] = ((o_ref[...].astype(F32) + r0_ref[...].astype(F32)) + r1_ref[...].astype(F32)) \
            + r2_ref[...].astype(F32)

    def rspec(p):
        return pl.BlockSpec((None, None, None, tr, D), lambda k, i, p=p: (p, k, 0, i, 0))

    blk = pl.BlockSpec((None, tr, D), lambda k, i: (k, i, 0))
    return pl.pallas_call(
        body, name="sum4", grid=(n, rows // tr),
        in_specs=[blk, rspec(0), rspec(1), rspec(2)], out_specs=blk,
        out_shape=jax.ShapeDtypeStruct((n, rows, D), F32),
    )(own, recv, recv, recv)


def _adam_math(w, g, m, v):
    m = ADAM_B1 * m + (1.0 - ADAM_B1) * g
    v = ADAM_B2 * v + (1.0 - ADAM_B2) * (g * g)
    m_hat = m / (1.0 - ADAM_B1 ** ADAM_STEP)
    v_hat = v / (1.0 - ADAM_B2 ** ADAM_STEP)
    delta = -ADAM_LR * (m_hat / (jnp.sqrt(v_hat) + ADAM_EPS) + ADAM_WD * w)
    return delta, m, v


def _adam_fused(w, m, v, parts, theirs, sel, row0, nrows, rider=None):
    L, R, D = w.shape
    assert R == nrows and parts[0].shape[2] == D
    t = _row_tile(nrows if row0 == 0 else _gcd(row0, nrows), 512)

    def gspec(k):
        return pl.BlockSpec((None, t, D), lambda l, c: (sel, row0 // t + jnp.where(l == k, c, 0), 0))

    def body(*refs):
        w_ref, m_ref, v_ref = refs[:3]
        p_refs, q_refs = refs[3:3 + L], refs[3 + L:3 + 2 * L]
        g_ref, d_ref, nm_ref, nv_ref, g_scr = refs[3 + 2 * L:]
        l = pl.program_id(0)
        for k in range(L):
            @pl.when(l == k)
            def _(k=k):
                g_scr[...] = p_refs[k][...] + q_refs[k][...]

        g = g_scr[...]
        g_ref[...] = g
        d, m_, v_ = _adam_math(w_ref[...], g, m_ref[...], v_ref[...])
        d_ref[...] = d
        nm_ref[...] = m_
        nv_ref[...] = v_

    blk = pl.BlockSpec((None, t, D), lambda l, c: (l, c, 0))
    return _call(body, name="adam_fused", grid=(L, R // t),
                 in_specs=[blk] * 3 + [gspec(k) for k in range(L)] * 2, out_specs=[blk] * 4,
                 out_shape=[jax.ShapeDtypeStruct((L, R, D), F32)] * 4,
                 args=[w, m, v, *parts, *theirs], scratch=[pltpu.VMEM((t, D), F32)], rider=rider)


def _gcd(a, b):
    while b:
        a, b = b, a % b
    return a


def _small_sum_adam(g8, w, m, v):
    _, R, W = g8.shape

    def body(g_ref, w_ref, m_ref, v_ref, go_ref, d_ref, nm_ref, nv_ref):
        g = g_ref[0]
        for k in range(1, 8):
            g = g + g_ref[k]
        go_ref[...] = g
        d, m_, v_ = _adam_math(w_ref[...], g, m_ref[...], v_ref[...])
        d_ref[...] = d
        nm_ref[...] = m_
        nv_ref[...] = v_

    return pl.pallas_call(body, name="small_sum_adam",
                          out_shape=[jax.ShapeDtypeStruct((R, W), F32)] * 4)(g8, w, m, v)


def _pad_vec(v):
    return jnp.pad(v, (0, SLOT - v.shape[0]))[None, :]


class _Pack:
    def __init__(self, shapes):
        self.shapes = shapes
        self.sizes = [int(functools.reduce(lambda a, b: a * b, s, 1)) for s in shapes]
        total = sum(self.sizes)
        self.rows = -(-total // (8 * SLOT)) * 8
        self.pad = self.rows * SLOT - total

    def pack(self, arrs):
        flat = jnp.concatenate([a.reshape(-1).astype(F32) for a in arrs] + [jnp.zeros((self.pad,), F32)])
        return flat.reshape(self.rows, SLOT)

    def unpack(self, buf):
        flat, out, o = buf.reshape(-1), [], 0
        for s, n in zip(self.shapes, self.sizes):
            out.append(flat[o:o + n].reshape(s))
            o += n
        return out


def kernel(x, mem, positions, ffn1_norm, ffn1_w1, ffn1_w3, ffn1_w2, mix_norm, w_in, conv_w, conv_b, conv_ln_g, conv_ln_b, swa_q_norm, swa_k_norm, swa_sinks, mem_norm, w_mem_kv, mem_q_norm, mem_k_norm, w_out, ffn2_norm, ffn2_w1, ffn2_w3, ffn2_w2, final_norm, loss_target, m_ffn1_norm, m_ffn1_w1, m_ffn1_w3, m_ffn1_w2, m_mix_norm, m_w_in, m_conv_w, m_conv_b, m_conv_ln_g, m_conv_ln_b, m_swa_q_norm, m_swa_k_norm, m_swa_sinks, m_mem_norm, m_w_mem_kv, m_mem_q_norm, m_mem_k_norm, m_w_out, m_ffn2_norm, m_ffn2_w1, m_ffn2_w3, m_ffn2_w2, m_final_norm, v_ffn1_norm, v_ffn1_w1, v_ffn1_w3, v_ffn1_w2, v_mix_norm, v_w_in, v_conv_w, v_conv_b, v_conv_ln_g, v_conv_ln_b, v_swa_q_norm, v_swa_k_norm, v_swa_sinks, v_mem_norm, v_w_mem_kv, v_mem_q_norm, v_mem_k_norm, v_w_out, v_ffn2_norm, v_ffn2_w1, v_ffn2_w3, v_ffn2_w2, v_final_norm):
    names = ['ffn1_norm', 'ffn1_w1', 'ffn1_w3', 'ffn1_w2', 'mix_norm', 'w_in', 'conv_w', 'conv_b', 'conv_ln_g',
             'conv_ln_b', 'swa_q_norm', 'swa_k_norm', 'swa_sinks', 'mem_norm', 'w_mem_kv', 'mem_q_norm',
             'mem_k_norm', 'w_out', 'ffn2_norm', 'ffn2_w1', 'ffn2_w3', 'ffn2_w2', 'final_norm']
    loc = locals()
    W = {n: loc[n] for n in names}
    M1 = {n: loc['m_' + n] for n in names}
    V1 = {n: loc['v_' + n] for n in names}

    S, D = x.shape[1], x.shape[2]
    L = ffn1_norm.shape[0]
    Fs = ffn1_w1.shape[2]
    F = 4 * Fs
    Mlen = mem.shape[1]
    cw_sh = conv_w.shape[2]
    tm = 512 if S >= 2048 else 256
    tf = 1408 if F % 1408 == 0 else 256
    tfw = 256
    tmw = 2048 if S >= 2048 else 256
    tma = 1024 if S >= 2048 else 256
    x0 = x[0]
    mem0 = mem[0]
    target = loss_target[0]
    my_chip = 2 * lax.axis_index("x") + lax.axis_index("y")

    mkv_rows = w_mem_kv.shape[1] * MEM_KV // D
    r_in, r_out = D_IN // 4, D_MIX // 4
    rm = r_in + r_out + mkv_rows

    shard = lambda w: w.astype(BF16).reshape((1, 1) + w.shape)

    groups = []
    for l in range(L):
        groups.append([shard(ffn1_w1[l].T), shard(ffn1_w3[l].T), shard(ffn1_w2[l])])
        groups.append([shard(w_in[l].T), shard(w_out[l]), shard(w_mem_kv[l].reshape(mkv_rows, D))])
        groups.append([shard(ffn2_w1[l].T), shard(ffn2_w3[l].T), shard(ffn2_w2[l])])
    gathered = [None] * len(groups)
    cw_rows = -(-(L * CONV_WIDTH) // 8) * 8
    cw_pad = jnp.pad(conv_w.reshape(L * CONV_WIDTH, cw_sh), ((0, cw_rows - L * CONV_WIDTH), (0, SLOT - cw_sh)))
    *gathered[0], cw_g = _run_rider(_GatherHalves(groups[0] + [cw_pad.reshape(1, 1, cw_rows, SLOT)]),
                                    "all_gather_first")
    conv_wF = cw_g[0, :, :L * CONV_WIDTH, :cw_sh].reshape(4, L, CONV_WIDTH, cw_sh)
    conv_wF = jnp.moveaxis(conv_wF, 0, 2).reshape(L, CONV_WIDTH, 4 * cw_sh)
    conv_wP = jnp.pad(conv_wF, ((0, 0), (0, 32 - CONV_WIDTH), (0, 0)))

    def gather_rider(j):
        want = [k for k in [j + 1] if k < len(groups)]
        return (_Gather([b for k in want for b in groups[k]]), want) if want else (None, want)

    def keep(want, got):
        for n, k in enumerate(want):
            gathered[k] = got[3 * n:3 * n + 3]

    def ffn_weights(j):
        return tuple(g.reshape(F, D) for g in gathered[j])

    def mix_weights(l):
        g_in, g_out, g_mkv = gathered[3 * l + 1]
        w_inp = g_in.reshape(D_IN, D)
        w_outp = g_out.reshape(D_MIX, D)
        w_mkvp = jnp.pad(g_mkv.reshape(D, 2 * N_MEMH, HEAD_DIM),
                         ((0, 0), (0, 0), (0, SLOT - HEAD_DIM))).reshape(D, 2 * N_MEMH * SLOT)
        return w_inp, w_outp, w_mkvp

    inv_freq = ROPE_THETA ** (-jnp.arange(0, HEAD_DIM, 2, dtype=F32) / HEAD_DIM)
    invf = jnp.tile(inv_freq, SLOT // (HEAD_DIM // 2))[None, :]
    cosT, sinT = _rope_tables(positions.reshape(S, 1), invf, tm)

    row = lambda a, l: a[l][None, :]
    sinks_p = jnp.pad(swa_sinks, ((0, 0), (0, 8 - N_Q)))

    saved = []
    xin = x0
    xn = None
    for l in range(L):
        wf1 = ffn_weights(3 * l)
        rider, want = gather_rider(3 * l)
        (h1, a1, b1, t1), got = _ffn_fwd(xin, row(ffn1_norm, l), wf1, None, tm, tf, rider=rider)
        keep(want, got)
        w_inp, w_outp, w_mkvp = mix_weights(l)
        pu, p, n2 = _proj_fwd(h1, row(mix_norm, l), w_inp, tm)
        gk_m = _pad_vec(mem_k_norm[l])
        nm, mraw, mk, mv = _mem_kv_fwd(mem0, row(mem_norm, l), w_mkvp, gk_m)
        twice = lambda v: jnp.tile(v, 2)[None, :]
        gq, gk, gqm = twice(swa_q_norm[l]), twice(swa_k_norm[l]), twice(mem_q_norm[l])
        rider, want = gather_rider(3 * l + 1)
        (h2, y, yc, lse), got = _mixer_fwd(pu, p, h1, cosT, sinT, conv_wP[l], row(conv_b, l), row(conv_ln_g, l),
                                           row(conv_ln_b, l), gq, gk, sinks_p[l], mk, mv, gqm, w_outp, tm,
                                           rider=rider)
        keep(want, got)
        wf2 = ffn_weights(3 * l + 2)
        rider, want = gather_rider(3 * l + 2)
        (h3, a2, b2, t2, xn), got = _ffn_fwd(h2, row(ffn2_norm, l), wf2, row(final_norm, l), tm, tf, rider=rider)
        keep(want, got)
        saved.append(dict(xin=xin, h1=h1, a1=a1, b1=b1, pu=pu, p=p, n2=n2, nm=nm, mraw=mraw, mk=mk, mv=mv, gk_m=gk_m,
                          gq=gq, gk=gk, gqm=gqm, h2=h2, y=y, yc=yc, lse=lse, h3=h3, a2=a2, b2=b2, t1=t1, t2=t2,
                          wf1=wf1, wf2=wf2, w_inp=w_inp, w_outp=w_outp, w_mkvp=w_mkvp))
        xin = xn

    G = {n: [None] * L for n in names}
    ffn_bufs = [None] * (2 * L)
    mix_bufs = [None] * L
    parts, theirs = {}, {}
    pending_swap = []

    def scattered(key, buf, recv):
        own = lax.dynamic_index_in_dim(buf, my_chip, axis=1, keepdims=False)
        parts[key] = _sum4(own, recv)
        pending_swap.append(key)

    def comm_rider(scatter_buf):
        keys = list(pending_swap)
        pending_swap.clear()
        riders = ([_Scatter([scatter_buf])] if scatter_buf is not None else []) \
            + ([_Swap([parts[k] for k in keys])] if keys else [])
        if not riders:
            return None, lambda got: None
        multi = _Multi(riders)

        def store(got):
            outs = multi.split_outputs(got)
            if keys:
                for k, t in zip(keys, outs[-1]):
                    theirs[k] = t
            return outs[0][0] if scatter_buf is not None else None

        return multi, store

    dxn = None
    loss_part = None
    for l in reversed(range(L)):
        sv = saved[l]
        if l == L - 1:
            loss_part, dh3, G['final_norm'][l] = _loss_bwd(xn, sv['h3'], row(final_norm, l), target, tm)
        else:
            dh3, G['final_norm'][l] = _norm_bwd(dxn, sv['h3'], row(final_norm, l), tm)
        rider, store = comm_rider(ffn_bufs[2 * l + 2] if l < L - 1 else None)
        (dh2, G['ffn2_norm'][l], da, db, n, dy), got = _ffn_bwd_act(
            dh3, sv['h2'], row(ffn2_norm, l), sv['a2'], sv['b2'], sv['wf2'], tm, tf, rider=rider)
        recv = store(got)
        if recv is not None:
            scattered(('f', 2 * l + 2), ffn_bufs[2 * l + 2], recv)
        ffn_bufs[2 * l + 1] = _ffn_bwd_w(da, db, sv['t2'], n, dy, tmw, tfw)[0].reshape(3, 4, Fs, D)
        dyc, do, delta, dwo, G['conv_ln_g'][l], G['conv_ln_b'][l] = _outproj_bwd(
            dh2, sv['y'], sv['yc'], row(conv_ln_g, l), row(conv_ln_b, l), sv['w_outp'], tm)
        rider, store = comm_rider(ffn_bufs[2 * l + 1])
        (dph, dgq, dgk, dgqm, dsink, dmk, dmv), got = _attn_bwd(
            sv['p'], do, sv['lse'], delta, cosT, sinT, sv['gq'], sv['gk'], sinks_p[l],
            sv['mk'], sv['mv'], sv['gqm'], tma, rider=rider)
        scattered(('f', 2 * l + 1), ffn_bufs[2 * l + 1], store(got))
        dpu, dcw, G['conv_b'][l] = _conv_bwd(sv['pu'], dyc, conv_wP[l], tma)
        dwm, G['mem_norm'][l], dgk_m = _mem_kv_bwd(dmk, dmv, sv['mraw'], sv['nm'], mem0, row(mem_norm, l),
                                                   sv['w_mkvp'], sv['gk_m'])
        dh1, G['mix_norm'][l], dwi = _proj_bwd(dpu, dph, sv['h1'], dh2, row(mix_norm, l), sv['n2'], sv['w_inp'], tm)
        dwiT = dwi.reshape(4, r_in, D)
        dwoF = dwo.reshape(4, r_out, D)
        dwmF = dwm.reshape(D, 2 * N_MEMH, SLOT)[:, :, :HEAD_DIM].reshape(4, mkv_rows, D)
        mix_bufs[l] = jnp.concatenate([dwiT, dwoF, dwmF], axis=1).astype(BF16).reshape(1, 4, rm, D)
        rider, store = comm_rider(mix_bufs[l]) if l > 0 else (None, None)
        (dxl, G['ffn1_norm'][l], da, db, n, dy), got = _ffn_bwd_act(
            dh1, sv['xin'], row(ffn1_norm, l), sv['a1'], sv['b1'], sv['wf1'], tm, tf, rider=rider)
        if l > 0:
            scattered(('m', l), mix_bufs[l], store(got))
            rider, store = None, None
        else:
            rider, store = comm_rider(mix_bufs[l])
        dwf, got = _ffn_bwd_w(da, db, sv['t1'], n, dy, tmw, tfw, rider=rider)
        if l == 0:
            scattered(('m', l), mix_bufs[l], store(got))
        ffn_bufs[2 * l] = dwf.reshape(3, 4, Fs, D)
        dxn = dxl
        G['conv_w'][l] = dcw[:CONV_WIDTH]
        G['swa_q_norm'][l] = dgq[0, :HEAD_DIM] + dgq[0, HEAD_DIM:]
        G['swa_k_norm'][l] = dgk[0, :HEAD_DIM] + dgk[0, HEAD_DIM:]
        G['mem_q_norm'][l] = dgqm[0, :HEAD_DIM] + dgqm[0, HEAD_DIM:]
        G['mem_k_norm'][l] = dgk_m[0, :HEAD_DIM]
        G['swa_sinks'][l] = dsink[0, :N_Q]
    grad_x = dxn[None]
    loss = lax.psum(loss_part[0, 0], AXES)
    rider, store = comm_rider(ffn_bufs[0])
    scattered(('f', 0), ffn_bufs[0], store(_run_rider(rider, "scatter_last")))
    rider, store = comm_rider(None)
    store(_run_rider(rider, "swap_last"))

    small = ['ffn1_norm', 'mix_norm', 'conv_b', 'conv_ln_g', 'conv_ln_b', 'swa_q_norm', 'swa_k_norm', 'swa_sinks',
             'mem_norm', 'mem_q_norm', 'mem_k_norm', 'ffn2_norm', 'final_norm']
    gsmall = [jnp.stack([G[n][l].reshape(-1) for l in range(L)]) for n in small]
    gcw = jnp.stack(G['conv_w'])
    cw_cols = 4 * cw_sh
    full_of = lambda a: lax.dynamic_update_slice(jnp.zeros((L, CONV_WIDTH, cw_cols), F32), a, (0, 0, my_chip * cw_sh))
    pk = _Pack([W[n].shape for n in small] + [(L, CONV_WIDTH, cw_cols)])
    g8 = _all_gather_small(pk.pack(gsmall + [gcw])[None])
    outs4 = _small_sum_adam(g8, pk.pack([W[n] for n in small] + [full_of(conv_w)]),
                            pk.pack([M1[n] for n in small] + [full_of(m_conv_w)]),
                            pk.pack([V1[n] for n in small] + [full_of(v_conv_w)]))
    un = [pk.unpack(o) for o in outs4]
    grads, deltas, new_m, new_v = {}, {}, {}, {}
    for k, n in enumerate(small):
        grads[n], deltas[n], new_m[n], new_v[n] = un[0][k], un[1][k], un[2][k], un[3][k]
    mine = lambda a: lax.dynamic_slice(a, (0, 0, my_chip * cw_sh), (L, CONV_WIDTH, cw_sh))
    grads['conv_w'], deltas['conv_w'], new_m['conv_w'], new_v['conv_w'] = [mine(u[-1]) for u in un]

    ffn1_k, ffn2_k = [('f', 2 * l) for l in range(L)], [('f', 2 * l + 1) for l in range(L)]
    mix_k = [('m', l) for l in range(L)]
    plan = [('ffn2_w1', ffn2_k, 0, 0, Fs, True), ('ffn2_w3', ffn2_k, 1, 0, Fs, True), ('ffn2_w2', ffn2_k, 2, 0, Fs, False),
            ('w_in', mix_k, 0, 0, r_in, True), ('w_out', mix_k, 0, r_in, r_out, False),
            ('w_mem_kv', mix_k, 0, r_in + r_out, mkv_rows, False),
            ('ffn1_w1', ffn1_k, 0, 0, Fs, True), ('ffn1_w3', ffn1_k, 1, 0, Fs, True), ('ffn1_w2', ffn1_k, 2, 0, Fs, False)]
    for n, keys, sel, row0, nrows, held_transposed in plan:
        shp = W[n].shape
        if held_transposed:
            view, back = (lambda a: jnp.swapaxes(a, 1, 2)), (lambda a: jnp.swapaxes(a, 1, 2))
        elif n == 'w_mem_kv':
            view, back = (lambda a: a.reshape(L, mkv_rows, D)), (lambda a: a.reshape(shp))
        else:
            view = back = lambda a: a
        res, _ = _adam_fused(view(W[n]), view(M1[n]), view(V1[n]), [parts[k] for k in keys],
                             [theirs[k] for k in keys], sel, row0, nrows)
        grads[n], deltas[n], new_m[n], new_v[n] = [back(r) for r in res]

    return (loss, grad_x, *[grads[n] for n in names], *[deltas[n] for n in names],
            *[new_m[n] for n in names], *[new_v[n] for n in names])
```
